```python
import jax, jax.numpy as jnp
from jax import lax
import numpy as np

D_MODEL = 1024
BATCH = 8
SEQ = 4096
DEPTH = 1

CHUNK = 64
N_META = 16
D_A = 1024
D_B = 1024
CONV_A = 31
CONV_B = 3
EPS = 1e-6
SPLITS = (D_A, D_A, D_A, D_B, D_B, D_B, D_B, D_MODEL, D_MODEL)
D_IN = sum(SPLITS)

kernel_name = "hybrid_gated_conformer_shortconv_block"


def _rmsnorm(x, g):
    xf = x.astype(jnp.float32)
    y = xf * lax.rsqrt(jnp.mean(xf * xf, axis=-1, keepdims=True) + EPS)
    return (y * g.astype(jnp.float32)).astype(x.dtype)


def _layernorm(x, g, b):
    xf = x.astype(jnp.float32)
    mu = jnp.mean(xf, axis=-1, keepdims=True)
    var = jnp.mean(jnp.square(xf - mu), axis=-1, keepdims=True)
    y = (xf - mu) * lax.rsqrt(var + EPS)
    return (y * g.astype(jnp.float32) + b.astype(jnp.float32)).astype(x.dtype)


def _causal_dwconv(x, w):
    k = w.shape[0]
    return lax.conv_general_dilated(
        x, w.astype(x.dtype)[:, None, :], window_strides=(1,), padding=[(k - 1, 0)],
        dimension_numbers=("NWC", "WIO", "NWC"), feature_group_count=x.shape[-1])


def _fwd_setup_inputs(seed: int = 0) -> dict:
    key = jax.random.key(seed)
    ks = jax.random.split(key, 16)
    f = jnp.float32
    L = DEPTH
    nrm = lambda k, shp, s: jax.random.normal(k, shp, f) * s
    return {
        "x": jax.random.normal(ks[0], (BATCH, SEQ, D_MODEL), f),
        "meta_tokens": nrm(ks[1], (N_META, D_MODEL), 1.0),
        "norm_g": 1.0 + nrm(ks[2], (L, D_MODEL), 0.02),
        "w_in": nrm(ks[3], (L, D_MODEL, D_IN), D_MODEL ** -0.5),
        "conv_a_w": nrm(ks[4], (L, CONV_A, D_A), CONV_A ** -0.5),
        "conv_a_b": nrm(ks[5], (L, D_A), 0.02),
        "ln_a_g": 1.0 + nrm(ks[6], (L, D_A), 0.02),
        "ln_a_b": nrm(ks[7], (L, D_A), 0.02),
        "w_a_out": nrm(ks[8], (L, D_A, D_MODEL), D_A ** -0.5),
        "b_a_out": nrm(ks[9], (L, D_MODEL), 0.02),
        "conv_b_w": nrm(ks[10], (L, CONV_B, D_B), CONV_B ** -0.5),
        "w_b_out": nrm(ks[11], (L, D_B, D_MODEL), D_B ** -0.5),
        "w_out": nrm(ks[12], (L, D_MODEL, D_MODEL), D_MODEL ** -0.5),
        "final_g": 1.0 + nrm(ks[13], (D_MODEL,), 0.02),
    }


def _fwd_reference(x, meta_tokens, norm_g, w_in, conv_a_w, conv_a_b, ln_a_g, ln_a_b,
              w_a_out, b_a_out, conv_b_w, w_b_out, w_out, final_g):
    bsz = x.shape[0]
    meta = jnp.broadcast_to(meta_tokens.astype(x.dtype)[None], (bsz, N_META, D_MODEL))
    s = jnp.concatenate([meta, x], axis=1)
    idx = np.cumsum(SPLITS)[:-1].tolist()
    for l in range(DEPTH):
        h = _rmsnorm(s, norm_g[l])
        proj = jnp.einsum("bld,de->ble", h, w_in[l])
        a_val, a_glu, a_z, b_B, b_C, b_x, b_z, g_a, g_b = jnp.split(proj, idx, axis=-1)
        ua = a_val * jax.nn.sigmoid(a_glu)
        ua = _causal_dwconv(ua, conv_a_w[l]) + conv_a_b[l]
        ua = jax.nn.silu(_layernorm(ua, ln_a_g[l], ln_a_b[l]))
        ya = jnp.einsum("blc,cd->bld", ua * jax.nn.silu(a_z), w_a_out[l]) + b_a_out[l]
        ub = b_B * _causal_dwconv(b_C * b_x, conv_b_w[l])
        yb = jnp.einsum("blc,cd->bld", ub * jax.nn.silu(b_z), w_b_out[l])
        m = jax.nn.sigmoid(g_a) * ya + jax.nn.sigmoid(g_b) * yb
        s = s + jnp.einsum("bld,de->ble", m, w_out[l])
    y = _rmsnorm(s, final_g)
    return y[:, N_META:, :]


import jax as _jax
import jax.numpy as _jnp

TWIN_FORMAT = 'train_step'
FWD_PARAMS = ['x', 'meta_tokens', 'norm_g', 'w_in', 'conv_a_w', 'conv_a_b', 'ln_a_g', 'ln_a_b', 'w_a_out', 'b_a_out', 'conv_b_w', 'w_b_out', 'w_out', 'final_g']
TWIN_WEIGHTS = ['meta_tokens', 'norm_g', 'w_in', 'conv_a_w', 'conv_a_b', 'ln_a_g', 'ln_a_b', 'w_a_out', 'b_a_out', 'conv_b_w', 'w_b_out', 'w_out', 'final_g']
TWIN_DIFF_INPUT = 'x'
TWIN_INPUTS = ['x', 'meta_tokens', 'norm_g', 'w_in', 'conv_a_w', 'conv_a_b', 'ln_a_g', 'ln_a_b', 'w_a_out', 'b_a_out', 'conv_b_w', 'w_b_out', 'w_out', 'final_g', 'loss_target', 'm_meta_tokens', 'm_norm_g', 'm_w_in', 'm_conv_a_w', 'm_conv_a_b', 'm_ln_a_g', 'm_ln_a_b', 'm_w_a_out', 'm_b_a_out', 'm_conv_b_w', 'm_w_b_out', 'm_w_out', 'm_final_g', 'v_meta_tokens', 'v_norm_g', 'v_w_in', 'v_conv_a_w', 'v_conv_a_b', 'v_ln_a_g', 'v_ln_a_b', 'v_w_a_out', 'v_b_a_out', 'v_conv_b_w', 'v_w_b_out', 'v_w_out', 'v_final_g']
TWIN_OUTPUTS = ['loss', 'grad_x', 'grad_meta_tokens', 'grad_norm_g', 'grad_w_in', 'grad_conv_a_w', 'grad_conv_a_b', 'grad_ln_a_g', 'grad_ln_a_b', 'grad_w_a_out', 'grad_b_a_out', 'grad_conv_b_w', 'grad_w_b_out', 'grad_w_out', 'grad_final_g', 'delta_meta_tokens', 'delta_norm_g', 'delta_w_in', 'delta_conv_a_w', 'delta_conv_a_b', 'delta_ln_a_g', 'delta_ln_a_b', 'delta_w_a_out', 'delta_b_a_out', 'delta_conv_b_w', 'delta_w_b_out', 'delta_w_out', 'delta_final_g', 'new_m_meta_tokens', 'new_m_norm_g', 'new_m_w_in', 'new_m_conv_a_w', 'new_m_conv_a_b', 'new_m_ln_a_g', 'new_m_ln_a_b', 'new_m_w_a_out', 'new_m_b_a_out', 'new_m_conv_b_w', 'new_m_w_b_out', 'new_m_w_out', 'new_m_final_g', 'new_v_meta_tokens', 'new_v_norm_g', 'new_v_w_in', 'new_v_conv_a_w', 'new_v_conv_a_b', 'new_v_ln_a_g', 'new_v_ln_a_b', 'new_v_w_a_out', 'new_v_b_a_out', 'new_v_conv_b_w', 'new_v_w_b_out', 'new_v_w_out', 'new_v_final_g']
TWIN_LEAF_KINDS = {'loss': 'loss', 'grad_x': 'grad_x', 'grad_meta_tokens': 'grad_w', 'grad_norm_g': 'grad_w', 'grad_w_in': 'grad_w', 'grad_conv_a_w': 'grad_w', 'grad_conv_a_b': 'grad_w', 'grad_ln_a_g': 'grad_w', 'grad_ln_a_b': 'grad_w', 'grad_w_a_out': 'grad_w', 'grad_b_a_out': 'grad_w', 'grad_conv_b_w': 'grad_w', 'grad_w_b_out': 'grad_w', 'grad_w_out': 'grad_w', 'grad_final_g': 'grad_w', 'delta_meta_tokens': 'delta_w', 'delta_norm_g': 'delta_w', 'delta_w_in': 'delta_w', 'delta_conv_a_w': 'delta_w', 'delta_conv_a_b': 'delta_w', 'delta_ln_a_g': 'delta_w', 'delta_ln_a_b': 'delta_w', 'delta_w_a_out': 'delta_w', 'delta_b_a_out': 'delta_w', 'delta_conv_b_w': 'delta_w', 'delta_w_b_out': 'delta_w', 'delta_w_out': 'delta_w', 'delta_final_g': 'delta_w', 'new_m_meta_tokens': 'new_m', 'new_m_norm_g': 'new_m', 'new_m_w_in': 'new_m', 'new_m_conv_a_w': 'new_m', 'new_m_conv_a_b': 'new_m', 'new_m_ln_a_g': 'new_m', 'new_m_ln_a_b': 'new_m', 'new_m_w_a_out': 'new_m', 'new_m_b_a_out': 'new_m', 'new_m_conv_b_w': 'new_m', 'new_m_w_b_out': 'new_m', 'new_m_w_out': 'new_m', 'new_m_final_g': 'new_m', 'new_v_meta_tokens': 'new_v', 'new_v_norm_g': 'new_v', 'new_v_w_in': 'new_v', 'new_v_conv_a_w': 'new_v', 'new_v_conv_a_b': 'new_v', 'new_v_ln_a_g': 'new_v', 'new_v_ln_a_b': 'new_v', 'new_v_w_a_out': 'new_v', 'new_v_b_a_out': 'new_v', 'new_v_conv_b_w': 'new_v', 'new_v_w_b_out': 'new_v', 'new_v_w_out': 'new_v', 'new_v_final_g': 'new_v'}


def _forward(args):
    return _fwd_reference(*[args[k] for k in FWD_PARAMS])


def _output_shape():
    def fwd():
        inp = _fwd_setup_inputs(0)
        return _fwd_reference(*[inp[k] for k in FWD_PARAMS])
    out = _jax.eval_shape(fwd)
    return out.shape, out.dtype

N_MICROBATCH = 1
ADAM_LR = 0.001
ADAM_B1 = 0.9
ADAM_B2 = 0.999
ADAM_EPS = 1e-08
ADAM_WD = 0.01
ADAM_STEP = 10
PER_EXAMPLE_BATCH_AXIS = {'x': 0, 'loss_target': 0}
SHARED_INPUTS = []
_WEIGHT_DTYPES = {'meta_tokens': _jnp.float32, 'norm_g': _jnp.float32, 'w_in': _jnp.float32, 'conv_a_w': _jnp.float32, 'conv_a_b': _jnp.float32, 'ln_a_g': _jnp.float32, 'ln_a_b': _jnp.float32, 'w_a_out': _jnp.float32, 'b_a_out': _jnp.float32, 'conv_b_w': _jnp.float32, 'w_b_out': _jnp.float32, 'w_out': _jnp.float32, 'final_g': _jnp.float32}
MOMENT_SCALE = {'meta_tokens': 2.381233e-03, 'norm_g': 1.249860e-01, 'w_in': 4.004686e-02, 'conv_a_w': 3.287001e-02, 'conv_a_b': 6.181423e-02, 'ln_a_g': 3.868083e-02, 'ln_a_b': 3.344138e-02, 'w_a_out': 3.223579e-02, 'b_a_out': 9.236053e-02, 'conv_b_w': 5.318666e-02, 'w_b_out': 5.304782e-02, 'w_out': 6.205696e-02, 'final_g': 3.200660e+01}


def _to_microbatches(a, axis):
    t = _jnp.moveaxis(a, axis, 0)
    t = t.reshape((N_MICROBATCH, t.shape[0] // N_MICROBATCH) + t.shape[1:])
    return _jnp.moveaxis(t, 1, axis + 1)


def setup_inputs(seed: int = 0) -> dict:
    inp = _fwd_setup_inputs(seed)
    key = _jax.random.fold_in(_jax.random.key(seed), 7919)
    shape, _ = _output_shape()
    out = dict(inp)
    out["loss_target"] = _jax.random.normal(_jax.random.fold_in(key, 0), shape, _jnp.float32)
    for i, name in enumerate(TWIN_WEIGHTS):
        w = inp[name].astype(_jnp.float32)
        if MOMENT_SCALE is None:
            s = _jnp.sqrt(_jnp.mean(_jnp.square(w)) + 1e-30)
        else:
            s = MOMENT_SCALE[name]
        km, kv = _jax.random.split(_jax.random.fold_in(key, i + 1))
        out[name] = w
        out["m_" + name] = s * _jax.random.normal(km, w.shape, _jnp.float32)
        out["v_" + name] = (s * s) * _jax.random.uniform(kv, w.shape, _jnp.float32, 0.5, 1.5)
    if N_MICROBATCH > 1:
        for name, axis in PER_EXAMPLE_BATCH_AXIS.items():
            out[name] = _to_microbatches(out[name], axis)
    return {'x': out['x'], 'meta_tokens': out['meta_tokens'], 'norm_g': out['norm_g'], 'w_in': out['w_in'], 'conv_a_w': out['conv_a_w'], 'conv_a_b': out['conv_a_b'], 'ln_a_g': out['ln_a_g'], 'ln_a_b': out['ln_a_b'], 'w_a_out': out['w_a_out'], 'b_a_out': out['b_a_out'], 'conv_b_w': out['conv_b_w'], 'w_b_out': out['w_b_out'], 'w_out': out['w_out'], 'final_g': out['final_g'], 'loss_target': out['loss_target'], 'm_meta_tokens': out['m_meta_tokens'], 'm_norm_g': out['m_norm_g'], 'm_w_in': out['m_w_in'], 'm_conv_a_w': out['m_conv_a_w'], 'm_conv_a_b': out['m_conv_a_b'], 'm_ln_a_g': out['m_ln_a_g'], 'm_ln_a_b': out['m_ln_a_b'], 'm_w_a_out': out['m_w_a_out'], 'm_b_a_out': out['m_b_a_out'], 'm_conv_b_w': out['m_conv_b_w'], 'm_w_b_out': out['m_w_b_out'], 'm_w_out': out['m_w_out'], 'm_final_g': out['m_final_g'], 'v_meta_tokens': out['v_meta_tokens'], 'v_norm_g': out['v_norm_g'], 'v_w_in': out['v_w_in'], 'v_conv_a_w': out['v_conv_a_w'], 'v_conv_a_b': out['v_conv_a_b'], 'v_ln_a_g': out['v_ln_a_g'], 'v_ln_a_b': out['v_ln_a_b'], 'v_w_a_out': out['v_w_a_out'], 'v_b_a_out': out['v_b_a_out'], 'v_conv_b_w': out['v_conv_b_w'], 'v_w_b_out': out['v_w_b_out'], 'v_w_out': out['v_w_out'], 'v_final_g': out['v_final_g']}


def _loss(weights, diff, rest, loss_target):
    with _jax.named_scope("forward"):
        args = {**rest, TWIN_DIFF_INPUT: diff, **{k: w.astype(_WEIGHT_DTYPES[k]) for k, w in weights.items()}}
        y = _forward(args)
    with _jax.named_scope("loss_head"):
        err = _jnp.square(y.astype(_jnp.float32) - loss_target)
        return 0.5 * _jnp.sum(_jnp.mean(err, axis=-1)) if err.ndim else 0.5 * err


def _adamw(w, g, m, v):
    m = ADAM_B1 * m + (1.0 - ADAM_B1) * g
    v = ADAM_B2 * v + (1.0 - ADAM_B2) * _jnp.square(g)
    m_hat = m / (1.0 - ADAM_B1 ** ADAM_STEP)
    v_hat = v / (1.0 - ADAM_B2 ** ADAM_STEP)
    delta = -ADAM_LR * (m_hat / (_jnp.sqrt(v_hat) + ADAM_EPS) + ADAM_WD * w)
    return delta, m, v


def reference(x, meta_tokens, norm_g, w_in, conv_a_w, conv_a_b, ln_a_g, ln_a_b, w_a_out, b_a_out, conv_b_w, w_b_out, w_out, final_g, loss_target, m_meta_tokens, m_norm_g, m_w_in, m_conv_a_w, m_conv_a_b, m_ln_a_g, m_ln_a_b, m_w_a_out, m_b_a_out, m_conv_b_w, m_w_b_out, m_w_out, m_final_g, v_meta_tokens, v_norm_g, v_w_in, v_conv_a_w, v_conv_a_b, v_ln_a_g, v_ln_a_b, v_w_a_out, v_b_a_out, v_conv_b_w, v_w_b_out, v_w_out, v_final_g):
    given = dict(x=x, meta_tokens=meta_tokens, norm_g=norm_g, w_in=w_in, conv_a_w=conv_a_w, conv_a_b=conv_a_b, ln_a_g=ln_a_g, ln_a_b=ln_a_b, w_a_out=w_a_out, b_a_out=b_a_out, conv_b_w=conv_b_w, w_b_out=w_b_out, w_out=w_out, final_g=final_g, loss_target=loss_target, m_meta_tokens=m_meta_tokens, m_norm_g=m_norm_g, m_w_in=m_w_in, m_conv_a_w=m_conv_a_w, m_conv_a_b=m_conv_a_b, m_ln_a_g=m_ln_a_g, m_ln_a_b=m_ln_a_b, m_w_a_out=m_w_a_out, m_b_a_out=m_b_a_out, m_conv_b_w=m_conv_b_w, m_w_b_out=m_w_b_out, m_w_out=m_w_out, m_final_g=m_final_g, v_meta_tokens=v_meta_tokens, v_norm_g=v_norm_g, v_w_in=v_w_in, v_conv_a_w=v_conv_a_w, v_conv_a_b=v_conv_a_b, v_ln_a_g=v_ln_a_g, v_ln_a_b=v_ln_a_b, v_w_a_out=v_w_a_out, v_b_a_out=v_b_a_out, v_conv_b_w=v_conv_b_w, v_w_b_out=v_w_b_out, v_w_out=v_w_out, v_final_g=v_final_g)
    weights = {n: given[n] for n in TWIN_WEIGHTS}
    shared = {n: given[n] for n in SHARED_INPUTS}
    per_example = {n: given[n] for n in ['x']}
    grad_fn = _jax.value_and_grad(_loss, argnums=(0, 1))

    def one_microbatch(ex, loss_target):
        ex = dict(ex)
        diff = ex.pop(TWIN_DIFF_INPUT)
        return grad_fn(weights, diff, {**shared, **ex}, loss_target)

    if N_MICROBATCH == 1:
        loss, (grad_w, grad_x) = one_microbatch(per_example, given["loss_target"])
    else:
        def body(carry, xs):
            loss_sum, grad_sum = carry
            l_k, (gw_k, gx_k) = one_microbatch(xs[0], xs[1])
            with _jax.named_scope("update"):
                return (loss_sum + l_k, _jax.tree.map(_jnp.add, grad_sum, gw_k)), gx_k

        init = (_jnp.zeros((), _jnp.float32), _jax.tree.map(_jnp.zeros_like, weights))
        (loss, grad_w), grad_x = _jax.lax.scan(body, init, (per_example, given["loss_target"]))
    with _jax.named_scope("update"):
        delta_w, new_m, new_v = {}, {}, {}
        for n in TWIN_WEIGHTS:
            delta_w[n], new_m[n], new_v[n] = _adamw(weights[n], grad_w[n], given["m_" + n], given["v_" + n])
    return (loss, grad_x, *[grad_w[n] for n in TWIN_WEIGHTS], *[delta_w[n] for n in TWIN_WEIGHTS],
            *[new_m[n] for n in TWIN_WEIGHTS], *[new_v[n] for n in TWIN_WEIGHTS])
```

```python
import functools

import jax
import jax.numpy as jnp
from jax import lax
from jax.experimental import pallas as pl
from jax.experimental.pallas import tpu as pltpu

F32 = jnp.float32
BF16 = jnp.bfloat16
MESH = pl.DeviceIdType.MESH

EPS = 1e-6
N_META = 16
N_SPLIT = 9
CONV_A = 31
CONV_B = 3
HALO_A = 32
HALO_B = 8
SHIFT_ROWS = 24
TM = 256
RB = 32
N_ROW_TILES_BIG = 8
N_CHIPS = 4
VMEM_LIMIT = 56 * 1024 * 1024

ADAM_LR = 0.001
ADAM_B1 = 0.9
ADAM_B2 = 0.999
ADAM_EPS = 1e-08
ADAM_WD = 0.01
ADAM_STEP = 10

ROW_DWA = 0
ROW_DWB = 32
ROW_DCAB = 40
ROW_DLNG = 41
ROW_DLNB = 42
ROW_DBAO = 43
SM_ROWS = 48
ROW_DMETA = 48
ROW_DNG = 64
ROW_DFG = 65
AR_ROWS = 72


def _sigmoid(v):
    return 1.0 / (1.0 + jnp.exp(-v))


def _params(sem, **kw):
    return pltpu.CompilerParams(dimension_semantics=sem, vmem_limit_bytes=VMEM_LIMIT, **kw)


def _rows(rb):
    return pl.ds(pl.multiple_of(rb * RB, RB), RB)


def _mesh_pos():
    x, y, c = lax.axis_index("x"), lax.axis_index("y"), lax.axis_index("c")
    return x, y, c


def _half(ref, j, c):
    h = ref.shape[2] // 2
    return ref.at[:, j, pl.ds(c * h, h), :]


def _all_gather(shards):
    n = len(shards)

    def body(*refs):
        ins, outs = refs[:n], refs[n:2 * n]
        send_sems, recv_sems, local_sems = refs[2 * n:]
        x, y, c = _mesh_pos()
        me = 2 * x + y
        sibling = (x, y, 1 - c)
        chips = [(1 - x, y), (x, 1 - y), (1 - x, 1 - y)]

        def remote(a, k, piece_src, piece_dst, to):
            return pltpu.make_async_remote_copy(
                src_ref=piece_src, dst_ref=piece_dst, send_sem=send_sems.at[6 * a + k],
                recv_sem=recv_sems.at[6 * a + k], device_id=to, device_id_type=MESH)

        locals_ = [pltpu.make_async_copy(ins[a], outs[a].at[:, me], local_sems.at[a]) for a in range(n)]
        for cp in locals_:
            cp.start()
        sends = []
        for a in range(n):
            h = ins[a].shape[1] // 2
            mine = ins[a].at[:, pl.ds(c * h, h), :]
            for k, (px, py) in enumerate(chips):
                sends.append(remote(a, k, mine, _half(outs[a], me, c), (px, py, c)))
        for cp in sends:
            cp.start()
        for a in range(n):
            for k, (px, py) in enumerate(chips):
                piece = _half(outs[a], 2 * px + py, c)
                remote(a, k, piece, piece, (px, py, c)).wait_recv()
                fwd = remote(a, 3 + k, piece, piece, sibling)
                fwd.start()
                sends.append(fwd)
        for a in range(n):
            for k, (px, py) in enumerate(chips):
                piece = _half(outs[a], 2 * px + py, 1 - c)
                remote(a, 3 + k, piece, piece, sibling).wait_recv()
        for cp in sends:
            cp.wait_send()
        for cp in locals_:
            cp.wait()

    any_spec = pl.BlockSpec(memory_space=pl.ANY)
    return pl.pallas_call(
        body, name="ag_weights",
        in_specs=[any_spec] * n, out_specs=[any_spec] * n,
        out_shape=[jax.ShapeDtypeStruct((s.shape[0], N_CHIPS) + s.shape[1:], s.dtype) for s in shards],
        scratch_shapes=[pltpu.SemaphoreType.DMA((6 * n,)), pltpu.SemaphoreType.DMA((6 * n,)),
                        pltpu.SemaphoreType.DMA((n,))],
    )(*shards)


def _sibling_exchange(grads):
    n = len(grads)

    def body(*refs):
        ins, outs = refs[:n], refs[n:2 * n]
        send_sems, recv_sems = refs[2 * n:]
        x, y, c = _mesh_pos()
        copies = []
        for a in range(n):
            h = ins[a].shape[2] // 2
            copies.append(pltpu.make_async_remote_copy(
                src_ref=ins[a].at[:, :, pl.ds((1 - c) * h, h), :], dst_ref=outs[a],
                send_sem=send_sems.at[a], recv_sem=recv_sems.at[a],
                device_id=(x, y, 1 - c), device_id_type=MESH))
        for cp in copies:
            cp.start()
        for cp in copies:
            cp.wait()

    any_spec = pl.BlockSpec(memory_space=pl.ANY)
    return pl.pallas_call(
        body, name="rs_sibling",
        in_specs=[any_spec] * n, out_specs=[any_spec] * n,
        out_shape=[jax.ShapeDtypeStruct(g.shape[:2] + (g.shape[2] // 2, g.shape[3]), g.dtype) for g in grads],
        scratch_shapes=[pltpu.SemaphoreType.DMA((n,)), pltpu.SemaphoreType.DMA((n,))],
    )(*grads)


def _chip_exchange(parts):
    n = len(parts)

    def body(*refs):
        ins, outs = refs[:n], refs[n:2 * n]
        send_sems, recv_sems, local_sems = refs[2 * n:]
        x, y, c = _mesh_pos()
        me = 2 * x + y
        chips = [(1 - x, y), (x, 1 - y), (1 - x, 1 - y)]
        locals_ = [pltpu.make_async_copy(ins[a].at[:, me], outs[a].at[:, me], local_sems.at[a]) for a in range(n)]
        for cp in locals_:
            cp.start()
        sends = []
        for a in range(n):
            for k, (px, py) in enumerate(chips):
                sends.append(pltpu.make_async_remote_copy(
                    src_ref=ins[a].at[:, 2 * px + py], dst_ref=outs[a].at[:, me],
                    send_sem=send_sems.at[3 * a + k], recv_sem=recv_sems.at[3 * a + k],
                    device_id=(px, py, c), device_id_type=MESH))
        for cp in sends:
            cp.start()
        for a in range(n):
            for k, (px, py) in enumerate(chips):
                landed = outs[a].at[:, 2 * px + py]
                pltpu.make_async_remote_copy(
                    src_ref=landed, dst_ref=landed, send_sem=send_sems.at[3 * a + k],
                    recv_sem=recv_sems.at[3 * a + k], device_id=(px, py, c), device_id_type=MESH).wait_recv()
        for cp in sends:
            cp.wait_send()
        for cp in locals_:
            cp.wait()

    any_spec = pl.BlockSpec(memory_space=pl.ANY)
    return pl.pallas_call(
        body, name="rs_chips",
        in_specs=[any_spec] * n, out_specs=[any_spec] * n,
        out_shape=[jax.ShapeDtypeStruct(p.shape, p.dtype) for p in parts],
        scratch_shapes=[pltpu.SemaphoreType.DMA((3 * n,)), pltpu.SemaphoreType.DMA((3 * n,)),
                        pltpu.SemaphoreType.DMA((n,))],
    )(*parts)


def _sibling_merge(halves):
    n = len(halves)

    def body(*refs):
        ins, outs = refs[:n], refs[n:2 * n]
        send_sems, recv_sems, local_sems = refs[2 * n:]
        x, y, c = _mesh_pos()
        locals_ = [pltpu.make_async_copy(ins[a], outs[a].at[:, c], local_sems.at[a]) for a in range(n)]
        copies = [pltpu.make_async_remote_copy(
            src_ref=ins[a], dst_ref=outs[a].at[:, c], send_sem=send_sems.at[a], recv_sem=recv_sems.at[a],
            device_id=(x, y, 1 - c), device_id_type=MESH) for a in range(n)]
        for cp in locals_ + copies:
            cp.start()
        for cp in copies:
            cp.wait_send()
        for a in range(n):
            landed = outs[a].at[:, 1 - c]
            pltpu.make_async_remote_copy(
                src_ref=landed, dst_ref=landed, send_sem=send_sems.at[a], recv_sem=recv_sems.at[a],
                device_id=(x, y, 1 - c), device_id_type=MESH).wait_recv()
        for cp in locals_:
            cp.wait()

    any_spec = pl.BlockSpec(memory_space=pl.ANY)
    return pl.pallas_call(
        body, name="rs_merge",
        in_specs=[any_spec] * n, out_specs=[any_spec] * n,
        out_shape=[jax.ShapeDtypeStruct((h.shape[0], 2) + h.shape[1:], h.dtype) for h in halves],
        scratch_shapes=[pltpu.SemaphoreType.DMA((n,)), pltpu.SemaphoreType.DMA((n,)),
                        pltpu.SemaphoreType.DMA((n,))],
    )(*halves)


def _all_reduce_small(block):
    rows, d = block.shape

    def body(x_ref, out_ref, sib_ref, part_ref, peers_ref, send_sems, recv_sems):
        x, y, c = _mesh_pos()
        me = 2 * x + y
        chips = [(1 - x, y), (x, 1 - y), (1 - x, 1 - y)]
        swap = pltpu.make_async_remote_copy(
            src_ref=x_ref, dst_ref=sib_ref, send_sem=send_sems.at[0], recv_sem=recv_sems.at[0],
            device_id=(x, y, 1 - c), device_id_type=MESH)
        swap.start()
        swap.wait()
        part_ref[...] = x_ref[...] + sib_ref[...]
        peers_ref[me] = part_ref[...]
        sends = [pltpu.make_async_remote_copy(
            src_ref=part_ref, dst_ref=peers_ref.at[me], send_sem=send_sems.at[1 + k], recv_sem=recv_sems.at[1 + k],
            device_id=(px, py, c), device_id_type=MESH) for k, (px, py) in enumerate(chips)]
        for cp in sends:
            cp.start()
        for k, (px, py) in enumerate(chips):
            landed = peers_ref.at[2 * px + py]
            pltpu.make_async_remote_copy(
                src_ref=landed, dst_ref=landed, send_sem=send_sems.at[1 + k], recv_sem=recv_sems.at[1 + k],
                device_id=(px, py, c), device_id_type=MESH).wait_recv()
        for cp in sends:
            cp.wait_send()
        out_ref[...] = ((peers_ref[0] + peers_ref[1]) + peers_ref[2]) + peers_ref[3]

    vm = pl.BlockSpec(memory_space=pltpu.VMEM)
    return pl.pallas_call(
        body, name="ar_small",
        in_specs=[vm], out_specs=vm,
        out_shape=jax.ShapeDtypeStruct((rows, d), F32),
        scratch_shapes=[pltpu.VMEM((rows, d), F32), pltpu.VMEM((rows, d), F32),
                        pltpu.VMEM((N_CHIPS, rows, d), F32),
                        pltpu.SemaphoreType.DMA((4,)), pltpu.SemaphoreType.DMA((4,))],
    )(block)


def _add_sibling(grad, recv, pos, name):
    s, nch, r, c = grad.shape
    h = r // 2
    hb = min(h, 128)

    def body(pos_ref, g_ref, r_ref, p32_ref, pbf_ref):
        p = g_ref[...] + r_ref[...]
        p32_ref[...] = p
        pbf_ref[...] = p.astype(BF16)

    nb = h // hb
    spec_g = pl.BlockSpec((None, None, hb, c), lambda si, j, b, pos_ref: (si, j, pos_ref[0] * nb + b, 0))
    spec_h = pl.BlockSpec((None, None, hb, c), lambda si, j, b, pos_ref: (si, j, b, 0))
    return pl.pallas_call(
        body, name=name,
        grid_spec=pltpu.PrefetchScalarGridSpec(
            num_scalar_prefetch=1, grid=(s, nch, nb), in_specs=[spec_g, spec_h], out_specs=[spec_h, spec_h]),
        out_shape=[jax.ShapeDtypeStruct((s, nch, h, c), F32), jax.ShapeDtypeStruct((s, nch, h, c), BF16)],
        compiler_params=_params(("arbitrary",) * 3),
    )(pos, grad, recv)


def _sum_chips(p32, landed, pos, name):
    s, nch, h, c = p32.shape
    hb = min(h, 128)

    def body(pos_ref, p_ref, l_ref, out_ref):
        me = pos_ref[1]
        acc = None
        for q in range(N_CHIPS):
            term = jnp.where(me == q, p_ref[...], l_ref[q].astype(F32))
            acc = term if acc is None else acc + term
        out_ref[...] = acc

    return pl.pallas_call(
        body, name=name,
        grid_spec=pltpu.PrefetchScalarGridSpec(
            num_scalar_prefetch=1, grid=(s, h // hb),
            in_specs=[pl.BlockSpec((None, None, hb, c), lambda si, b, pos_ref: (si, pos_ref[1], b, 0)),
                      pl.BlockSpec((None, nch, hb, c), lambda si, b, pos_ref: (si, 0, b, 0))],
            out_specs=pl.BlockSpec((None, hb, c), lambda si, b, pos_ref: (si, b, 0))),
        out_shape=jax.ShapeDtypeStruct((s, h, c), F32),
        compiler_params=_params(("arbitrary",) * 2),
    )(pos, p32, landed)


def _adamw(w, g, m, v):
    m = ADAM_B1 * m + (1.0 - ADAM_B1) * g
    v = ADAM_B2 * v + (1.0 - ADAM_B2) * (g * g)
    m_hat = m / (1.0 - ADAM_B1 ** ADAM_STEP)
    v_hat = v / (1.0 - ADAM_B2 ** ADAM_STEP)
    delta = -ADAM_LR * (m_hat / (jnp.sqrt(v_hat) + ADAM_EPS) + ADAM_WD * w)
    return delta, m, v


def _adam_rows(w, g, m, v, name):
    r, c = w.shape
    rb = 128 if r % 128 == 0 else r

    def body(w_ref, g_ref, m_ref, v_ref, d_out, m_out, v_out):
        d_out[...], m_out[...], v_out[...] = _adamw(w_ref[...], g_ref[...], m_ref[...], v_ref[...])

    spec = pl.BlockSpec((rb, c), lambda i: (i, 0))
    return pl.pallas_call(
        body, name=name, grid=(r // rb,), in_specs=[spec] * 4, out_specs=[spec] * 3,
        out_shape=[jax.ShapeDtypeStruct((r, c), F32)] * 3,
        compiler_params=_params(("arbitrary",)),
    )(w, g, m, v)


def _adam_small(items):
    n = len(items)

    def body(*refs):
        ins, outs = refs[:4 * n], refs[4 * n:]
        for a in range(n):
            w_ref, g_ref, m_ref, v_ref = ins[4 * a:4 * a + 4]
            d, m, v = _adamw(w_ref[...], g_ref[...], m_ref[...], v_ref[...])
            outs[3 * a][...] = d
            outs[3 * a + 1][...] = m
            outs[3 * a + 2][...] = v

    vm = pl.BlockSpec(memory_space=pltpu.VMEM)
    flat = [t for it in items for t in it]
    outs = pl.pallas_call(
        body, name="adam_small", in_specs=[vm] * (4 * n), out_specs=[vm] * (3 * n),
        out_shape=[jax.ShapeDtypeStruct(it[0].shape, F32) for it in items for _ in range(3)],
    )(*flat)
    return [tuple(outs[3 * a:3 * a + 3]) for a in range(n)]


def _proj_fwd(s_pad, norm_g, wg_in):
    tp, d = s_pad.shape
    _, nsh, _, sw = wg_in.shape
    tmb = tp // N_ROW_TILES_BIG

    def body(s_ref, g_ref, w_ref, proj_ref):
        s = s_ref[...]
        r = lax.rsqrt(jnp.mean(s * s, axis=-1, keepdims=True) + EPS)
        h = (s * r * g_ref[...]).astype(BF16)
        proj_ref[...] = jnp.dot(h, w_ref[...], preferred_element_type=F32).astype(BF16)

    return pl.pallas_call(
        body, name="f1_proj", grid=(nsh, N_ROW_TILES_BIG),
        in_specs=[pl.BlockSpec((tmb, d), lambda j, i: (i, 0)),
                  pl.BlockSpec((1, d), lambda j, i: (0, 0)),
                  pl.BlockSpec((None, None, d, sw), lambda j, i: (0, j, 0, 0))],
        out_specs=pl.BlockSpec((tmb, sw), lambda j, i: (i, j)),
        out_shape=jax.ShapeDtypeStruct((tp, nsh * sw), BF16),
        compiler_params=_params(("arbitrary", "arbitrary")),
    )(s_pad, norm_g, wg_in)


def _dh_bwd(dproj, wg_in, s_pad, ds2, norm_g):
    tp, d = s_pad.shape
    _, nsh, _, sw = wg_in.shape
    tmb = tp // N_ROW_TILES_BIG

    def body(dp_ref, w_ref, s_ref, ds2_ref, g_ref, ds_ref, dng_ref, acc, gacc):
        i, j = pl.program_id(0), pl.program_id(1)

        @pl.when((i == 0) & (j == 0))
        def _():
            gacc[...] = jnp.zeros_like(gacc)

        part = lax.dot_general(dp_ref[...], w_ref[...], (((1,), (1,)), ((), ())), preferred_element_type=F32)

        @pl.when(j == 0)
        def _():
            acc[...] = part

        @pl.when(j > 0)
        def _():
            acc[...] += part

        @pl.when(j == nsh - 1)
        def _():
            dh = acc[...]
            s = s_ref[...]
            r = lax.rsqrt(jnp.mean(s * s, axis=-1, keepdims=True) + EPS)
            gacc[...] += (dh * s * r).reshape(tmb // 8, 8, d).sum(axis=0)
            t = dh * g_ref[...]
            ds_ref[...] = ds2_ref[...] + r * t - s * (r * r * r) * jnp.mean(t * s, axis=-1, keepdims=True)

        @pl.when((i == N_ROW_TILES_BIG - 1) & (j == nsh - 1))
        def _():
            dng_ref[...] = jnp.broadcast_to(jnp.sum(gacc[...], axis=0, keepdims=True), (8, d))

    return pl.pallas_call(
        body, name="b2_dh", grid=(N_ROW_TILES_BIG, nsh),
        in_specs=[pl.BlockSpec((tmb, sw), lambda i, j: (i, j)),
                  pl.BlockSpec((None, None, d, sw), lambda i, j: (0, j, 0, 0)),
                  pl.BlockSpec((tmb, d), lambda i, j: (i, 0)),
                  pl.BlockSpec((tmb, d), lambda i, j: (i, 0)),
                  pl.BlockSpec((1, d), lambda i, j: (0, 0))],
        out_specs=[pl.BlockSpec((tmb, d), lambda i, j: (i, 0)),
                   pl.BlockSpec((8, d), lambda i, j: (0, 0))],
        out_shape=[jax.ShapeDtypeStruct((tp, d), F32), jax.ShapeDtypeStruct((8, d), F32)],
        scratch_shapes=[pltpu.VMEM((tmb, d), F32), pltpu.VMEM((8, d), F32)],
        compiler_params=_params(("arbitrary", "arbitrary")),
    )(dproj, wg_in, s_pad, ds2, norm_g)


def _dw_in(h, dproj, nsh):
    tp, d = h.shape
    sw = dproj.shape[1] // nsh
    ncol = 2 if (sw // 2) % 128 == 0 else 1
    cw = sw // ncol
    tmb = tp // N_ROW_TILES_BIG

    def body(h_ref, dp_ref, out_ref, acc):
        k = pl.program_id(2)
        part = lax.dot_general(h_ref[...], dp_ref[...], (((0,), (0,)), ((), ())), preferred_element_type=F32)

        @pl.when(k == 0)
        def _():
            acc[...] = part

        @pl.when(k > 0)
        def _():
            acc[...] += part

        @pl.when(k == N_ROW_TILES_BIG - 1)
        def _():
            out_ref[...] = acc[...]

    return pl.pallas_call(
        body, name="dw_in", grid=(nsh, ncol, N_ROW_TILES_BIG),
        in_specs=[pl.BlockSpec((tmb, d), lambda j, n, k: (k, 0)),
                  pl.BlockSpec((tmb, cw), lambda j, n, k: (k, j * ncol + n))],
        out_specs=pl.BlockSpec((None, None, d, cw), lambda j, n, k: (0, j, 0, n)),
        out_shape=jax.ShapeDtypeStruct((1, nsh, d, sw), F32),
        scratch_shapes=[pltpu.VMEM((d, cw), F32)],
        compiler_params=_params(("arbitrary",) * 3),
    )(h, dproj)


def _dw_square(lhs3, rhs3):
    n, tp, d = lhs3.shape
    tmb = tp // N_ROW_TILES_BIG

    def body(a_ref, b_ref, out_ref, acc):
        k = pl.program_id(1)
        part = lax.dot_general(a_ref[...], b_ref[...], (((0,), (0,)), ((), ())), preferred_element_type=F32)

        @pl.when(k == 0)
        def _():
            acc[...] = part

        @pl.when(k > 0)
        def _():
            acc[...] += part

        @pl.when(k == N_ROW_TILES_BIG - 1)
        def _():
            out_ref[...] = acc[...]

    return pl.pallas_call(
        body, name="dw_square", grid=(n, N_ROW_TILES_BIG),
        in_specs=[pl.BlockSpec((None, tmb, d), lambda a, k: (a, k, 0)),
                  pl.BlockSpec((None, tmb, d), lambda a, k: (a, k, 0))],
        out_specs=pl.BlockSpec((None, d, d), lambda a, k: (a, 0, 0)),
        out_shape=jax.ShapeDtypeStruct((n, d, d), F32),
        scratch_shapes=[pltpu.VMEM((d, d), F32)],
        compiler_params=_params(("arbitrary",) * 2),
    )(lhs3, rhs3)


def _conv_a_taps(first_lag, last_lag):
    out = []
    for r in range(8):
        taps = [(q, 8 * q + r) for q in range(5) if first_lag <= 8 * q + r <= last_lag]
        if taps:
            out.append((r, taps))
    return out


def _mix_fwd(s_pad, proj, target, w3, wa, wb, conv_a_b, ln_g, ln_b, b_a_out, final_g, norm_g):
    tp, d = s_pad.shape
    nt = tp // TM
    nrb = TM // RB
    shl = TM + SHIFT_ROWS

    def body(s_ref, proj_ref, tgt_ref, w3_ref, wa_ref, wb_ref, cab_ref, lng_ref, lnb_ref, bao_ref, fg_ref, ng_ref,
             ca_ref, cb_ref, ya_ref, yb_ref, abm_ref, ds2_ref, h_ref, loss_ref, dfg_ref,
             ext_a, ext_b, sh, s2_s, lacc, gacc):
        i = pl.program_id(0)

        def split(k, rows):
            return proj_ref[rows, k * d:(k + 1) * d].astype(F32)

        s_in = s_ref[...]
        h_ref[...] = (s_in * lax.rsqrt(jnp.mean(s_in * s_in, axis=-1, keepdims=True) + EPS)
                      * ng_ref[...]).astype(BF16)

        @pl.when(i == 0)
        def _():
            ext_a[0:HALO_A, :] = jnp.zeros((HALO_A, d), F32)
            ext_b[0:HALO_B, :] = jnp.zeros((HALO_B, d), F32)
            lacc[...] = jnp.zeros_like(lacc)
            gacc[...] = jnp.zeros_like(gacc)

        def conv_in(rb, carry):
            rows = _rows(rb)
            ua0 = split(0, rows) * _sigmoid(split(1, rows))
            ext_a[pl.ds(pl.multiple_of(HALO_A + rb * RB, 8), RB), :] = ua0
            ext_b[pl.ds(pl.multiple_of(HALO_B + rb * RB, 8), RB), :] = split(4, rows) * split(5, rows)
            ca_ref[rows, :] = jnp.broadcast_to(cab_ref[...], (RB, d))
            return carry
        lax.fori_loop(0, nrb, conv_in, 0)

        for r, taps in _conv_a_taps(HALO_A - CONV_A + 1, HALO_A):
            if r == 0:
                src = ext_a
            else:
                sh[...] = ext_a[r:r + shl, :]
                src = sh

            def conv_acc(rb, carry, src=src, taps=taps):
                rows = _rows(rb)
                acc = ca_ref[rows, :]
                for q, lag in taps:
                    k = lag - (HALO_A - CONV_A + 1)
                    acc = acc + src[pl.ds(pl.multiple_of(rb * RB + 8 * q, 8), RB), :] * wa_ref[k:k + 1, :]
                ca_ref[rows, :] = acc
                return carry
            lax.fori_loop(0, nrb, conv_acc, 0)
        ext_a[0:HALO_A, :] = ext_a[TM:TM + HALO_A, :]

        cb_ref[...] = ext_b[HALO_B:HALO_B + TM, :] * wb_ref[2:3, :]
        for k in range(CONV_B - 1):
            off = HALO_B - CONV_B + 1 + k
            sh[0:TM, :] = ext_b[off:off + TM, :]
            cb_ref[...] += sh[0:TM, :] * wb_ref[k:k + 1, :]
        ext_b[0:HALO_B, :] = ext_b[TM:TM + HALO_B, :]

        def branches(rb, carry):
            rows = _rows(rb)
            ca = ca_ref[rows, :]
            mu = jnp.mean(ca, axis=-1, keepdims=True)
            xc = ca - mu
            rstd = lax.rsqrt(jnp.mean(xc * xc, axis=-1, keepdims=True) + EPS)
            ln = xc * rstd * lng_ref[...] + lnb_ref[...]
            ua = ln * _sigmoid(ln)
            a_z = split(2, rows)
            abm_ref[0, rows, :] = (ua * (a_z * _sigmoid(a_z))).astype(BF16)
            b_z = split(6, rows)
            ub = split(3, rows) * cb_ref[rows, :]
            abm_ref[1, rows, :] = (ub * (b_z * _sigmoid(b_z))).astype(BF16)
            return carry
        lax.fori_loop(0, nrb, branches, 0)

        ya_ref[...] = jnp.dot(abm_ref[0], w3_ref[0], preferred_element_type=F32) + bao_ref[...]
        yb_ref[...] = jnp.dot(abm_ref[1], w3_ref[1], preferred_element_type=F32)

        def merge(rb, carry):
            rows = _rows(rb)
            m = _sigmoid(split(7, rows)) * ya_ref[rows, :] + _sigmoid(split(8, rows)) * yb_ref[rows, :]
            abm_ref[2, rows, :] = m.astype(BF16)
            return carry
        lax.fori_loop(0, nrb, merge, 0)

        s2_s[...] = s_ref[...] + jnp.dot(abm_ref[2], w3_ref[2], preferred_element_type=F32)
        live = (i > 0).astype(F32)

        def head(rb, carry):
            rows = _rows(rb)
            s2 = s2_s[rows, :]
            r2 = lax.rsqrt(jnp.mean(s2 * s2, axis=-1, keepdims=True) + EPS)
            diff = (s2 * r2 * fg_ref[...] - tgt_ref[rows, :]) * live
            lacc[...] += diff * diff
            dy = diff * (1.0 / d)
            gacc[...] += (dy * s2 * r2).reshape(RB // 8, 8, d).sum(axis=0)
            t = dy * fg_ref[...]
            ds2_ref[rows, :] = r2 * t - s2 * (r2 * r2 * r2) * jnp.mean(t * s2, axis=-1, keepdims=True)
            return carry
        lax.fori_loop(0, nrb, head, 0)

        @pl.when(i == nt - 1)
        def _():
            loss_ref[...] = jnp.broadcast_to(0.5 * jnp.sum(lacc[...]) * (1.0 / d), (8, 128))
            dfg_ref[...] = jnp.broadcast_to(jnp.sum(gacc[...], axis=0, keepdims=True), (8, d))

    row_f32 = pl.BlockSpec((TM, d), lambda i: (i, 0))
    const = lambda shape: pl.BlockSpec(shape, lambda i: (0,) * len(shape))
    return pl.pallas_call(
        body, name="f2_mix", grid=(nt,),
        in_specs=[row_f32,
                  pl.BlockSpec((TM, N_SPLIT * d), lambda i: (i, 0)),
                  pl.BlockSpec((TM, d), lambda i: (jnp.maximum(i - 1, 0), 0)),
                  const((3, d, d)), const(wa.shape), const(wb.shape)] + [const((1, d))] * 6,
        out_specs=[row_f32, row_f32, row_f32, row_f32,
                   pl.BlockSpec((3, TM, d), lambda i: (0, i, 0)),
                   row_f32, row_f32, const((8, 128)), const((8, d))],
        out_shape=[jax.ShapeDtypeStruct((tp, d), F32)] * 4
        + [jax.ShapeDtypeStruct((3, tp, d), BF16), jax.ShapeDtypeStruct((tp, d), F32),
           jax.ShapeDtypeStruct((tp, d), BF16),
           jax.ShapeDtypeStruct((8, 128), F32), jax.ShapeDtypeStruct((8, d), F32)],
        scratch_shapes=[pltpu.VMEM((HALO_A + TM, d), F32), pltpu.VMEM((HALO_B + TM, d), F32),
                        pltpu.VMEM((shl, d), F32), pltpu.VMEM((TM, d), F32),
                        pltpu.VMEM((RB, d), F32), pltpu.VMEM((8, d), F32)],
        compiler_params=_params(("arbitrary",)),
    )(s_pad, proj, target, w3, wa, wb, conv_a_b, ln_g, ln_b, b_a_out, final_g, norm_g)


def _mix_bwd(ds2, proj, ca, cb, ya, yb, w3, wa, wb, ln_g, ln_b):
    tp, d = ds2.shape
    nt = tp // TM
    nrb = TM // RB
    shl = TM + SHIFT_ROWS
    nt_dims = (((1,), (1,)), ((), ()))

    def body(ds2_ref, proj_ref, ca_ref, cb_ref, ya_ref, yb_ref, w3_ref, wa_ref, wb_ref, lng_ref, lnb_ref,
             dproj_ref, d3_ref, sm_ref, ext_d, ext_e, sh, dm_s, dpa_s, dpb_s, dua0_s, acc):
        step = pl.program_id(0)

        def split(k, rows):
            return proj_ref[rows, k * d:(k + 1) * d].astype(F32)

        def put(k, rows, val):
            dproj_ref[rows, k * d:(k + 1) * d] = val.astype(BF16)

        def accum(row, val):
            acc[row] += val.reshape(RB // 8, 8, d).sum(axis=0)

        @pl.when(step == 0)
        def _():
            ext_d[TM:TM + HALO_A, :] = jnp.zeros((HALO_A, d), F32)
            ext_e[TM:TM + HALO_B, :] = jnp.zeros((HALO_B, d), F32)
            acc[...] = jnp.zeros_like(acc)

        d3_ref[2] = ds2_ref[...].astype(BF16)
        dm_s[...] = lax.dot_general(d3_ref[2], w3_ref[2], nt_dims, preferred_element_type=F32)

        def gates(rb, carry):
            rows = _rows(rb)
            dm = dm_s[rows, :]
            sa = _sigmoid(split(7, rows))
            sb = _sigmoid(split(8, rows))
            ya_v = ya_ref[rows, :]
            yb_v = yb_ref[rows, :]
            put(7, rows, dm * ya_v * sa * (1.0 - sa))
            put(8, rows, dm * yb_v * sb * (1.0 - sb))
            dya = dm * sa
            accum(ROW_DBAO, dya)
            d3_ref[0, rows, :] = dya.astype(BF16)
            d3_ref[1, rows, :] = (dm * sb).astype(BF16)
            return carry
        lax.fori_loop(0, nrb, gates, 0)

        dpa_s[...] = lax.dot_general(d3_ref[0], w3_ref[0], nt_dims, preferred_element_type=F32)
        dpb_s[...] = lax.dot_general(d3_ref[1], w3_ref[1], nt_dims, preferred_element_type=F32)

        def branches(rb, carry):
            rows = _rows(rb)
            ca_v = ca_ref[rows, :]
            mu = jnp.mean(ca_v, axis=-1, keepdims=True)
            xc = ca_v - mu
            rstd = lax.rsqrt(jnp.mean(xc * xc, axis=-1, keepdims=True) + EPS)
            xhat = xc * rstd
            ln = xhat * lng_ref[...] + lnb_ref[...]
            sl = _sigmoid(ln)
            ua = ln * sl
            a_z = split(2, rows)
            sz = _sigmoid(a_z)
            dpa = dpa_s[rows, :]
            put(2, rows, dpa * ua * (sz * (1.0 + a_z * (1.0 - sz))))
            dln = dpa * (a_z * sz) * (sl * (1.0 + ln * (1.0 - sl)))
            accum(ROW_DLNG, dln * xhat)
            accum(ROW_DLNB, dln)
            dxh = dln * lng_ref[...]
            dca = rstd * (dxh - jnp.mean(dxh, axis=-1, keepdims=True)
                          - xhat * jnp.mean(dxh * xhat, axis=-1, keepdims=True))
            accum(ROW_DCAB, dca)
            ext_d[rows, :] = dca
            dua0_s[rows, :] = jnp.zeros((RB, d), F32)
            b_z = split(6, rows)
            szb = _sigmoid(b_z)
            dpb = dpb_s[rows, :]
            b_b = split(3, rows)
            cb_v = cb_ref[rows, :]
            put(6, rows, dpb * (b_b * cb_v) * (szb * (1.0 + b_z * (1.0 - szb))))
            dub = dpb * (b_z * szb)
            put(3, rows, dub * cb_v)
            ext_e[rows, :] = dub * b_b
            return carry
        lax.fori_loop(0, nrb, branches, 0)

        for r, taps in _conv_a_taps(0, CONV_A - 1):
            if r == 0:
                src = ext_d
            else:
                sh[...] = ext_d[r:r + shl, :]
                src = sh

            def conv_t(rb, carry, src=src, taps=taps):
                rows = _rows(rb)
                ua0 = split(0, rows) * _sigmoid(split(1, rows))
                dua0 = dua0_s[rows, :]
                for q, lag in taps:
                    k = CONV_A - 1 - lag
                    slab = src[pl.ds(pl.multiple_of(rb * RB + 8 * q, 8), RB), :]
                    dua0 = dua0 + slab * wa_ref[k:k + 1, :]
                    accum(ROW_DWA + k, slab * ua0)
                dua0_s[rows, :] = dua0
                return carry
            lax.fori_loop(0, nrb, conv_t, 0)
        ext_d[TM:TM + HALO_A, :] = ext_d[0:HALO_A, :]

        dpb_s[...] = ext_e[0:TM, :] * wb_ref[CONV_B - 1:CONV_B, :]
        for lag in range(CONV_B):
            k = CONV_B - 1 - lag
            if lag > 0:
                sh[0:TM, :] = ext_e[lag:lag + TM, :]
                dpb_s[...] += sh[0:TM, :] * wb_ref[k:k + 1, :]
            src = ext_e if lag == 0 else sh

            def conv_b_w(rb, carry, src=src, k=k):
                rows = _rows(rb)
                accum(ROW_DWB + k, src[rows, :] * (split(4, rows) * split(5, rows)))
                return carry
            lax.fori_loop(0, nrb, conv_b_w, 0)
        ext_e[TM:TM + HALO_B, :] = ext_e[0:HALO_B, :]

        def inputs(rb, carry):
            rows = _rows(rb)
            dua0 = dua0_s[rows, :]
            a_val = split(0, rows)
            sg = _sigmoid(split(1, rows))
            put(0, rows, dua0 * sg)
            put(1, rows, dua0 * a_val * sg * (1.0 - sg))
            dcbin = dpb_s[rows, :]
            put(4, rows, dcbin * split(5, rows))
            put(5, rows, dcbin * split(4, rows))
            return carry
        lax.fori_loop(0, nrb, inputs, 0)

        @pl.when(step == nt - 1)
        def _():
            for row in range(SM_ROWS):
                sm_ref[row:row + 1, :] = jnp.sum(acc[row], axis=0, keepdims=True)

    rev = lambda i: (nt - 1 - i, 0)
    row_f32 = pl.BlockSpec((TM, d), rev)
    const = lambda shape: pl.BlockSpec(shape, lambda i: (0,) * len(shape))
    return pl.pallas_call(
        body, name="b1_mix", grid=(nt,),
        in_specs=[row_f32, pl.BlockSpec((TM, N_SPLIT * d), rev), row_f32, row_f32, row_f32, row_f32,
                  const((3, d, d)), const(wa.shape), const(wb.shape), const((1, d)), const((1, d))],
        out_specs=[pl.BlockSpec((TM, N_SPLIT * d), rev),
                   pl.BlockSpec((3, TM, d), lambda i: (0, nt - 1 - i, 0)),
                   const((SM_ROWS, d))],
        out_shape=[jax.ShapeDtypeStruct((tp, N_SPLIT * d), BF16), jax.ShapeDtypeStruct((3, tp, d), BF16),
                   jax.ShapeDtypeStruct((SM_ROWS, d), F32)],
        scratch_shapes=[pltpu.VMEM((TM + HALO_A, d), F32), pltpu.VMEM((TM + HALO_B, d), F32),
                        pltpu.VMEM((shl, d), F32), pltpu.VMEM((TM, d), F32), pltpu.VMEM((TM, d), F32),
                        pltpu.VMEM((TM, d), F32), pltpu.VMEM((TM, d), F32),
                        pltpu.VMEM((SM_ROWS, 8, d), F32)],
        compiler_params=_params(("arbitrary",)),
    )(ds2, proj, ca, cb, ya, yb, w3, wa, wb, ln_g, ln_b)


def kernel(x, meta_tokens, norm_g, w_in, conv_a_w, conv_a_b, ln_a_g, ln_a_b, w_a_out, b_a_out, conv_b_w, w_b_out, w_out, final_g, loss_target, m_meta_tokens, m_norm_g, m_w_in, m_conv_a_w, m_conv_a_b, m_ln_a_g, m_ln_a_b, m_w_a_out, m_b_a_out, m_conv_b_w, m_w_b_out, m_w_out, m_final_g, v_meta_tokens, v_norm_g, v_w_in, v_conv_a_w, v_conv_a_b, v_ln_a_g, v_ln_a_b, v_w_a_out, v_b_a_out, v_conv_b_w, v_w_b_out, v_w_out, v_final_g):
    seq, d = x.shape[1], x.shape[2]
    dc = meta_tokens.shape[1]
    sw = w_in.shape[2]
    rsh = w_a_out.shape[1]
    xi, yi, ci = _mesh_pos()
    me = 2 * xi + yi
    pos = jnp.stack([ci, me]).astype(jnp.int32)

    smalls = jnp.concatenate([
        jnp.pad(conv_a_w[0], ((0, HALO_A - CONV_A), (0, 0))),
        jnp.pad(conv_b_w[0], ((0, HALO_B - CONV_B), (0, 0))),
        meta_tokens, jnp.zeros((8, dc), F32)], axis=0)[None]
    w3_own = jnp.stack([w_a_out[0], w_b_out[0], w_out[0]]).astype(BF16)
    wg_in, wg3, smg = _all_gather([w_in.astype(BF16), w3_own, smalls])
    w3 = wg3.reshape(3, N_CHIPS * rsh, d)
    smg = jnp.transpose(smg[0], (1, 0, 2)).reshape(smalls.shape[1], N_CHIPS * dc)
    wa_full = smg[0:HALO_A]
    wb_full = smg[HALO_A:HALO_A + HALO_B]
    meta_full = smg[HALO_A + HALO_B:HALO_A + HALO_B + N_META]
    fg2 = final_g.reshape(1, d)

    first_tile = jnp.concatenate([jnp.zeros((TM - N_META, d), F32), meta_full], axis=0)
    s_pad = jnp.concatenate([first_tile, x[0]], axis=0)

    proj = _proj_fwd(s_pad, norm_g, wg_in)
    ca, cb, ya, yb, abm, ds2, h, loss8, dfg8 = _mix_fwd(
        s_pad, proj, loss_target[0], w3, wa_full, wb_full, conv_a_b, ln_a_g, ln_a_b, b_a_out, fg2, norm_g)
    dproj, d3, sm = _mix_bwd(ds2, proj, ca, cb, ya, yb, w3, wa_full, wb_full, ln_a_g, ln_a_b)
    ds, dng8 = _dh_bwd(dproj, wg_in, s_pad, ds2, norm_g)
    g_in = _dw_in(h, dproj, N_CHIPS)
    g_sq = _dw_square(abm, d3).reshape(3, N_CHIPS, rsh, d)

    r_in, r_sq = _sibling_exchange([g_in, g_sq])
    p32_in, pbf_in = _add_sibling(g_in, r_in, pos, "rs_add_in")
    p32_sq, pbf_sq = _add_sibling(g_sq, r_sq, pos, "rs_add_sq")
    l_in, l_sq = _chip_exchange([pbf_in, pbf_sq])
    half_in = _sum_chips(p32_in, l_in, pos, "rs_sum_in")
    half_sq = _sum_chips(p32_sq, l_sq, pos, "rs_sum_sq")
    full_in, full_sq = _sibling_merge([half_in, half_sq])
    grad_w_in = full_in.reshape(d, sw)
    grad_sq = full_sq.reshape(3, rsh, d)

    tail_row = lax.broadcasted_iota(jnp.int32, (8, d), 0)
    tail = jnp.where(tail_row == 0, dng8, jnp.where(tail_row == 1, dfg8, 0.0))
    block = jnp.concatenate([sm, ds[TM - N_META:TM], tail], axis=0)
    red = _all_reduce_small(block)
    col = lax.dynamic_slice(red, (0, me * dc), (AR_ROWS, dc))
    g_small = {
        "meta_tokens": col[ROW_DMETA:ROW_DMETA + N_META],
        "norm_g": red[ROW_DNG:ROW_DNG + 1],
        "conv_a_w": col[ROW_DWA:ROW_DWA + CONV_A][None],
        "conv_a_b": red[ROW_DCAB:ROW_DCAB + 1],
        "ln_a_g": red[ROW_DLNG:ROW_DLNG + 1],
        "ln_a_b": red[ROW_DLNB:ROW_DLNB + 1],
        "b_a_out": red[ROW_DBAO:ROW_DBAO + 1],
        "conv_b_w": col[ROW_DWB:ROW_DWB + CONV_B][None],
        "final_g": red[ROW_DFG],
    }

    d_in, nm_in, nv_in = _adam_rows(w_in[0], grad_w_in, m_w_in[0], v_w_in[0], "adam_in")
    w_sq = jnp.concatenate([w_a_out[0], w_b_out[0], w_out[0]], axis=0)
    m_sq = jnp.concatenate([m_w_a_out[0], m_w_b_out[0], m_w_out[0]], axis=0)
    v_sq = jnp.concatenate([v_w_a_out[0], v_w_b_out[0], v_w_out[0]], axis=0)
    d_sq, nm_sq, nv_sq = _adam_rows(w_sq, grad_sq.reshape(3 * rsh, d), m_sq, v_sq, "adam_sq")
    small_w = {"meta_tokens": (meta_tokens, m_meta_tokens, v_meta_tokens), "norm_g": (norm_g, m_norm_g, v_norm_g),
               "conv_a_w": (conv_a_w, m_conv_a_w, v_conv_a_w), "conv_a_b": (conv_a_b, m_conv_a_b, v_conv_a_b),
               "ln_a_g": (ln_a_g, m_ln_a_g, v_ln_a_g), "ln_a_b": (ln_a_b, m_ln_a_b, v_ln_a_b),
               "b_a_out": (b_a_out, m_b_a_out, v_b_a_out), "conv_b_w": (conv_b_w, m_conv_b_w, v_conv_b_w),
               "final_g": (final_g, m_final_g, v_final_g)}
    names_small = list(small_w)
    as2d = lambda t: t.reshape(-1, t.shape[-1])
    upd_small = _adam_small([(as2d(small_w[k][0]), as2d(g_small[k]), as2d(small_w[k][1]), as2d(small_w[k][2]))
                             for k in names_small])

    grads, deltas, new_m, new_v = dict(g_small), {}, {}, {}
    for k, upd in zip(names_small, upd_small):
        deltas[k], new_m[k], new_v[k] = [t.reshape(small_w[k][0].shape) for t in upd]
    grads["w_in"] = grad_w_in[None]
    deltas["w_in"], new_m["w_in"], new_v["w_in"] = d_in[None], nm_in[None], nv_in[None]
    for idx, k in enumerate(["w_a_out", "w_b_out", "w_out"]):
        rows = slice(idx * rsh, (idx + 1) * rsh)
        grads[k] = grad_sq[idx][None]
        deltas[k], new_m[k], new_v[k] = d_sq[rows][None], nm_sq[rows][None], nv_sq[rows][None]

    loss = lax.psum(loss8[0, 0], ("x", "y", "c"))
    grad_x = ds[TM:][None]
    order = ["meta_tokens", "norm_g", "w_in", "conv_a_w", "conv_a_b", "ln_a_g", "ln_a_b", "w_a_out", "b_a_out",
             "conv_b_w", "w_b_out", "w_out", "final_g"]
    return (loss, grad_x, *[grads[k] for k in order], *[deltas[k] for k in order],
            *[new_m[k] for k in order], *[new_v[k] for k in order])
```

```python
import functools

import jax
import jax.numpy as jnp
from jax import lax
from jax.experimental import pallas as pl
from jax.experimental.pallas import tpu as pltpu

F32 = jnp.float32
BF16 = jnp.bfloat16
MESH = pl.DeviceIdType.MESH

EPS = 1e-6
N_META = 16
N_SPLIT = 9
CONV_A = 31
CONV_B = 3
HALO_A = 32
HALO_B = 8
SHIFT_ROWS = 24
TM = 256
RB = 32
N_ROW_TILES_BIG = 8
N_CHIPS = 4
VMEM_LIMIT = 56 * 1024 * 1024

ADAM_LR = 0.001
ADAM_B1 = 0.9
ADAM_B2 = 0.999
ADAM_EPS = 1e-08
ADAM_WD = 0.01
ADAM_STEP = 10

ROW_DWA = 0
ROW_DWB = 32
ROW_DCAB = 40
ROW_DLNG = 41
ROW_DLNB = 42
ROW_DBAO = 43
SM_ROWS = 48
ROW_DMETA = 48
ROW_DNG = 64
ROW_DFG = 65
ROW_LOSS = 66
AR_ROWS = 72


def _sigmoid(v):
    return 1.0 / (1.0 + jnp.exp(-v))


def _params(sem, **kw):
    return pltpu.CompilerParams(dimension_semantics=sem, vmem_limit_bytes=VMEM_LIMIT, **kw)


def _rows(rb):
    return pl.ds(pl.multiple_of(rb * RB, RB), RB)


def _mesh_pos():
    x, y, c = lax.axis_index("x"), lax.axis_index("y"), lax.axis_index("c")
    return x, y, c


def _half(ref, j, c):
    h = ref.shape[2] // 2
    return ref.at[:, j, pl.ds(c * h, h), :]


def _place_own(shard, pos, dtype, name):
    s, r, c = shard.shape
    rb = 128 if r % 128 == 0 else r

    def body(pos_ref, x_ref, o_ref):
        o_ref[...] = x_ref[...].astype(dtype)

    return pl.pallas_call(
        body, name=name,
        grid_spec=pltpu.PrefetchScalarGridSpec(
            num_scalar_prefetch=1, grid=(s, r // rb),
            in_specs=[pl.BlockSpec((None, rb, c), lambda si, b, pos_ref: (si, b, 0))],
            out_specs=pl.BlockSpec((None, None, rb, c), lambda si, b, pos_ref: (si, pos_ref[1], b, 0))),
        out_shape=jax.ShapeDtypeStruct((s, N_CHIPS, r, c), dtype),
        compiler_params=_params(("arbitrary",) * 2),
    )(pos, shard)


def _all_gather(bufs):
    n = len(bufs)

    def body(*refs):
        outs = refs[n:2 * n]
        send_sems, recv_sems = refs[2 * n:]
        x, y, c = _mesh_pos()
        me = 2 * x + y
        sibling = (x, y, 1 - c)
        chips = [(1 - x, y), (x, 1 - y), (1 - x, 1 - y)]

        def remote(a, k, piece_src, piece_dst, to):
            return pltpu.make_async_remote_copy(
                src_ref=piece_src, dst_ref=piece_dst, send_sem=send_sems.at[6 * a + k],
                recv_sem=recv_sems.at[6 * a + k], device_id=to, device_id_type=MESH)

        sends = []
        for a in range(n):
            mine = _half(outs[a], me, c)
            for k, (px, py) in enumerate(chips):
                sends.append(remote(a, k, mine, mine, (px, py, c)))
        for cp in sends:
            cp.start()
        for a in range(n):
            for k, (px, py) in enumerate(chips):
                piece = _half(outs[a], 2 * px + py, c)
                remote(a, k, piece, piece, (px, py, c)).wait_recv()
                fwd = remote(a, 3 + k, piece, piece, sibling)
                fwd.start()
                sends.append(fwd)
        for a in range(n):
            for k, (px, py) in enumerate(chips):
                piece = _half(outs[a], 2 * px + py, 1 - c)
                remote(a, 3 + k, piece, piece, sibling).wait_recv()
        for cp in sends:
            cp.wait_send()

    any_spec = pl.BlockSpec(memory_space=pl.ANY)
    return pl.pallas_call(
        body, name="ag_weights",
        in_specs=[any_spec] * n, out_specs=[any_spec] * n,
        out_shape=[jax.ShapeDtypeStruct(b.shape, b.dtype) for b in bufs],
        input_output_aliases={a: a for a in range(n)},
        scratch_shapes=[pltpu.SemaphoreType.DMA((6 * n,)), pltpu.SemaphoreType.DMA((6 * n,))],
    )(*bufs)


def _sibling_exchange(grads):
    n = len(grads)

    def body(*refs):
        ins, outs = refs[:n], refs[n:2 * n]
        send_sems, recv_sems = refs[2 * n:]
        x, y, c = _mesh_pos()
        copies = []
        for a in range(n):
            h = ins[a].shape[2] // 2
            copies.append(pltpu.make_async_remote_copy(
                src_ref=ins[a].at[:, :, pl.ds((1 - c) * h, h), :], dst_ref=outs[a],
                send_sem=send_sems.at[a], recv_sem=recv_sems.at[a],
                device_id=(x, y, 1 - c), device_id_type=MESH))
        for cp in copies:
            cp.start()
        for cp in copies:
            cp.wait()

    any_spec = pl.BlockSpec(memory_space=pl.ANY)
    return pl.pallas_call(
        body, name="rs_sibling",
        in_specs=[any_spec] * n, out_specs=[any_spec] * n,
        out_shape=[jax.ShapeDtypeStruct(g.shape[:2] + (g.shape[2] // 2, g.shape[3]), g.dtype) for g in grads],
        scratch_shapes=[pltpu.SemaphoreType.DMA((n,)), pltpu.SemaphoreType.DMA((n,))],
    )(*grads)


def _chip_exchange(parts):
    n = len(parts)

    def body(*refs):
        ins, outs = refs[:n], refs[n:2 * n]
        send_sems, recv_sems = refs[2 * n:]
        x, y, c = _mesh_pos()
        me = 2 * x + y
        chips = [(1 - x, y), (x, 1 - y), (1 - x, 1 - y)]
        sends = []
        for a in range(n):
            for k, (px, py) in enumerate(chips):
                sends.append(pltpu.make_async_remote_copy(
                    src_ref=ins[a].at[:, 2 * px + py], dst_ref=outs[a].at[:, me],
                    send_sem=send_sems.at[3 * a + k], recv_sem=recv_sems.at[3 * a + k],
                    device_id=(px, py, c), device_id_type=MESH))
        for cp in sends:
            cp.start()
        for a in range(n):
            for k, (px, py) in enumerate(chips):
                landed = outs[a].at[:, 2 * px + py]
                pltpu.make_async_remote_copy(
                    src_ref=landed, dst_ref=landed, send_sem=send_sems.at[3 * a + k],
                    recv_sem=recv_sems.at[3 * a + k], device_id=(px, py, c), device_id_type=MESH).wait_recv()
        for cp in sends:
            cp.wait_send()

    any_spec = pl.BlockSpec(memory_space=pl.ANY)
    return pl.pallas_call(
        body, name="rs_chips",
        in_specs=[any_spec] * n, out_specs=[any_spec] * n,
        out_shape=[jax.ShapeDtypeStruct(p.shape, p.dtype) for p in parts],
        scratch_shapes=[pltpu.SemaphoreType.DMA((3 * n,)), pltpu.SemaphoreType.DMA((3 * n,))],
    )(*parts)


def _sibling_swap(halves):
    n = len(halves)

    def body(*refs):
        ins, outs = refs[:n], refs[n:2 * n]
        send_sems, recv_sems = refs[2 * n:]
        x, y, c = _mesh_pos()
        copies = [pltpu.make_async_remote_copy(
            src_ref=ins[a], dst_ref=outs[a], send_sem=send_sems.at[a], recv_sem=recv_sems.at[a],
            device_id=(x, y, 1 - c), device_id_type=MESH) for a in range(n)]
        for cp in copies:
            cp.start()
        for cp in copies:
            cp.wait()

    any_spec = pl.BlockSpec(memory_space=pl.ANY)
    return pl.pallas_call(
        body, name="rs_swap",
        in_specs=[any_spec] * n, out_specs=[any_spec] * n,
        out_shape=[jax.ShapeDtypeStruct(h.shape, h.dtype) for h in halves],
        scratch_shapes=[pltpu.SemaphoreType.DMA((n,)), pltpu.SemaphoreType.DMA((n,))],
    )(*halves)


def _all_reduce_small(block):
    rows, d = block.shape

    def body(x_ref, out_ref, sib_ref, part_ref, peers_ref, send_sems, recv_sems):
        x, y, c = _mesh_pos()
        me = 2 * x + y
        chips = [(1 - x, y), (x, 1 - y), (1 - x, 1 - y)]
        swap = pltpu.make_async_remote_copy(
            src_ref=x_ref, dst_ref=sib_ref, send_sem=send_sems.at[0], recv_sem=recv_sems.at[0],
            device_id=(x, y, 1 - c), device_id_type=MESH)
        swap.start()
        swap.wait()
        part_ref[...] = x_ref[...] + sib_ref[...]
        peers_ref[me] = part_ref[...]
        sends = [pltpu.make_async_remote_copy(
            src_ref=part_ref, dst_ref=peers_ref.at[me], send_sem=send_sems.at[1 + k], recv_sem=recv_sems.at[1 + k],
            device_id=(px, py, c), device_id_type=MESH) for k, (px, py) in enumerate(chips)]
        for cp in sends:
            cp.start()
        for k, (px, py) in enumerate(chips):
            landed = peers_ref.at[2 * px + py]
            pltpu.make_async_remote_copy(
                src_ref=landed, dst_ref=landed, send_sem=send_sems.at[1 + k], recv_sem=recv_sems.at[1 + k],
                device_id=(px, py, c), device_id_type=MESH).wait_recv()
        for cp in sends:
            cp.wait_send()
        out_ref[...] = ((peers_ref[0] + peers_ref[1]) + peers_ref[2]) + peers_ref[3]

    vm = pl.BlockSpec(memory_space=pltpu.VMEM)
    return pl.pallas_call(
        body, name="ar_small",
        in_specs=[vm], out_specs=vm,
        out_shape=jax.ShapeDtypeStruct((rows, d), F32),
        scratch_shapes=[pltpu.VMEM((rows, d), F32), pltpu.VMEM((rows, d), F32),
                        pltpu.VMEM((N_CHIPS, rows, d), F32),
                        pltpu.SemaphoreType.DMA((4,)), pltpu.SemaphoreType.DMA((4,))],
    )(block)


def _add_sibling(grad, recv, pos, name):
    s, nch, r, c = grad.shape
    h = r // 2
    hb = min(h, 128)

    def body(pos_ref, g_ref, r_ref, p32_ref, pbf_ref):
        p = g_ref[...] + r_ref[...]
        p32_ref[...] = p
        pbf_ref[...] = p.astype(BF16)

    nb = h // hb
    spec_g = pl.BlockSpec((None, None, hb, c), lambda si, j, b, pos_ref: (si, j, pos_ref[0] * nb + b, 0))
    spec_h = pl.BlockSpec((None, None, hb, c), lambda si, j, b, pos_ref: (si, j, b, 0))
    return pl.pallas_call(
        body, name=name,
        grid_spec=pltpu.PrefetchScalarGridSpec(
            num_scalar_prefetch=1, grid=(s, nch, nb), in_specs=[spec_g, spec_h], out_specs=[spec_h, spec_h]),
        out_shape=[jax.ShapeDtypeStruct((s, nch, h, c), F32), jax.ShapeDtypeStruct((s, nch, h, c), BF16)],
        compiler_params=_params(("arbitrary",) * 3),
    )(pos, grad, recv)


def _sum_chips(p32, landed, pos, name):
    s, nch, h, c = p32.shape
    hb = min(h, 128)

    def body(pos_ref, p_ref, l1_ref, l2_ref, l3_ref, out_ref):
        out_ref[...] = ((p_ref[...] + l1_ref[...].astype(F32)) + l2_ref[...].astype(F32)) + l3_ref[...].astype(F32)

    def slot(k):
        return pl.BlockSpec((None, None, hb, c), lambda si, b, pos_ref: (si, (pos_ref[1] + k) % N_CHIPS, b, 0))

    return pl.pallas_call(
        body, name=name,
        grid_spec=pltpu.PrefetchScalarGridSpec(
            num_scalar_prefetch=1, grid=(s, h // hb),
            in_specs=[slot(0), slot(1), slot(2), slot(3)],
            out_specs=pl.BlockSpec((None, hb, c), lambda si, b, pos_ref: (si, b, 0))),
        out_shape=jax.ShapeDtypeStruct((s, h, c), F32),
        compiler_params=_params(("arbitrary",) * 2),
    )(pos, p32, landed, landed, landed)


def _adamw(w, g, m, v):
    m = ADAM_B1 * m + (1.0 - ADAM_B1) * g
    v = ADAM_B2 * v + (1.0 - ADAM_B2) * (g * g)
    m_hat = m / (1.0 - ADAM_B1 ** ADAM_STEP)
    v_hat = v / (1.0 - ADAM_B2 ** ADAM_STEP)
    delta = -ADAM_LR * (m_hat / (jnp.sqrt(v_hat) + ADAM_EPS) + ADAM_WD * w)
    return delta, m, v


def _adam_halves(ws, ms, vs, g_own, g_recv, pos, name):
    n = len(ws)
    _, r, c = ws[0].shape
    h = r // 2
    rb = min(h, 128)
    nb = h // rb

    def body(pos_ref, *refs):
        w_refs, m_refs, v_refs = refs[:n], refs[n:2 * n], refs[2 * n:3 * n]
        go_ref, gr_ref = refs[3 * n:3 * n + 2]
        outs = refs[3 * n + 2:]
        mine = pl.program_id(0) == pos_ref[0]
        for a in range(n):
            g = jnp.where(mine, go_ref[a], gr_ref[a])
            delta, m, v = _adamw(w_refs[a][...], g, m_refs[a][...], v_refs[a][...])
            outs[4 * a][...], outs[4 * a + 1][...], outs[4 * a + 2][...], outs[4 * a + 3][...] = g, delta, m, v

    spec_w = pl.BlockSpec((None, rb, c), lambda hf, b, pos_ref: (0, hf * nb + b, 0))
    spec_g = pl.BlockSpec((n, rb, c), lambda hf, b, pos_ref: (0, b, 0))
    return pl.pallas_call(
        body, name=name,
        grid_spec=pltpu.PrefetchScalarGridSpec(
            num_scalar_prefetch=1, grid=(2, nb), in_specs=[spec_w] * (3 * n) + [spec_g] * 2,
            out_specs=[spec_w] * (4 * n)),
        out_shape=[jax.ShapeDtypeStruct((1, r, c), F32)] * (4 * n),
        compiler_params=_params(("arbitrary",) * 2),
    )(pos, *ws, *ms, *vs, g_own, g_recv)


def _adam_small(items):
    n = len(items)

    def body(*refs):
        ins, outs = refs[:4 * n], refs[4 * n:]
        for a in range(n):
            w_ref, g_ref, m_ref, v_ref = ins[4 * a:4 * a + 4]
            d, m, v = _adamw(w_ref[...], g_ref[...], m_ref[...], v_ref[...])
            outs[3 * a][...] = d
            outs[3 * a + 1][...] = m
            outs[3 * a + 2][...] = v

    vm = pl.BlockSpec(memory_space=pltpu.VMEM)
    flat = [t for it in items for t in it]
    outs = pl.pallas_call(
        body, name="adam_small", in_specs=[vm] * (4 * n), out_specs=[vm] * (3 * n),
        out_shape=[jax.ShapeDtypeStruct(it[0].shape, F32) for it in items for _ in range(3)],
    )(*flat)
    return [tuple(outs[3 * a:3 * a + 3]) for a in range(n)]


def _proj_fwd(s_pad, norm_g, wg_in):
    tp, d = s_pad.shape
    _, nsh, _, sw = wg_in.shape
    tmb = tp // N_ROW_TILES_BIG

    def body(s_ref, g_ref, w_ref, proj_ref):
        s = s_ref[...]
        r = lax.rsqrt(jnp.mean(s * s, axis=-1, keepdims=True) + EPS)
        h = (s * r * g_ref[...]).astype(BF16)
        proj_ref[...] = jnp.dot(h, w_ref[...], preferred_element_type=F32).astype(BF16)

    return pl.pallas_call(
        body, name="f1_proj", grid=(nsh, N_ROW_TILES_BIG),
        in_specs=[pl.BlockSpec((tmb, d), lambda j, i: (i, 0)),
                  pl.BlockSpec((1, d), lambda j, i: (0, 0)),
                  pl.BlockSpec((None, None, d, sw), lambda j, i: (0, j, 0, 0))],
        out_specs=pl.BlockSpec((tmb, sw), lambda j, i: (i, j)),
        out_shape=jax.ShapeDtypeStruct((tp, nsh * sw), BF16),
        compiler_params=_params(("arbitrary", "arbitrary")),
    )(s_pad, norm_g, wg_in)


def _dh_bwd(dproj, wg_in, s_pad, ds2, norm_g):
    tp, d = s_pad.shape
    _, nsh, _, sw = wg_in.shape
    tmb = tp // N_ROW_TILES_BIG

    def body(dp_ref, w_ref, s_ref, ds2_ref, g_ref, ds_ref, dng_ref, acc, gacc):
        i, j = pl.program_id(0), pl.program_id(1)

        @pl.when((i == 0) & (j == 0))
        def _():
            gacc[...] = jnp.zeros_like(gacc)

        part = lax.dot_general(dp_ref[...], w_ref[...], (((1,), (1,)), ((), ())), preferred_element_type=F32)

        @pl.when(j == 0)
        def _():
            acc[...] = part

        @pl.when(j > 0)
        def _():
            acc[...] += part

        @pl.when(j == nsh - 1)
        def _():
            dh = acc[...]
            s = s_ref[...]
            r = lax.rsqrt(jnp.mean(s * s, axis=-1, keepdims=True) + EPS)
            gacc[...] += (dh * s * r).reshape(tmb // 8, 8, d).sum(axis=0)
            t = dh * g_ref[...]
            ds_ref[...] = ds2_ref[...] + r * t - s * (r * r * r) * jnp.mean(t * s, axis=-1, keepdims=True)

        @pl.when((i == N_ROW_TILES_BIG - 1) & (j == nsh - 1))
        def _():
            dng_ref[...] = jnp.broadcast_to(jnp.sum(gacc[...], axis=0, keepdims=True), (8, d))

    return pl.pallas_call(
        body, name="b2_dh", grid=(N_ROW_TILES_BIG, nsh),
        in_specs=[pl.BlockSpec((tmb, sw), lambda i, j: (i, j)),
                  pl.BlockSpec((None, None, d, sw), lambda i, j: (0, j, 0, 0)),
                  pl.BlockSpec((tmb, d), lambda i, j: (i, 0)),
                  pl.BlockSpec((tmb, d), lambda i, j: (i, 0)),
                  pl.BlockSpec((1, d), lambda i, j: (0, 0))],
        out_specs=[pl.BlockSpec((tmb, d), lambda i, j: (i, 0)),
                   pl.BlockSpec((8, d), lambda i, j: (0, 0))],
        out_shape=[jax.ShapeDtypeStruct((tp, d), F32), jax.ShapeDtypeStruct((8, d), F32)],
        scratch_shapes=[pltpu.VMEM((tmb, d), F32), pltpu.VMEM((8, d), F32)],
        compiler_params=_params(("arbitrary", "arbitrary")),
    )(dproj, wg_in, s_pad, ds2, norm_g)


def _dw_in(h, dproj, nsh):
    tp, d = h.shape
    sw = dproj.shape[1] // nsh
    ncol = 2 if (sw // 2) % 128 == 0 else 1
    cw = sw // ncol
    tmb = tp // N_ROW_TILES_BIG

    def body(h_ref, dp_ref, out_ref, acc):
        k = pl.program_id(2)
        part = lax.dot_general(h_ref[...], dp_ref[...], (((0,), (0,)), ((), ())), preferred_element_type=F32)

        @pl.when(k == 0)
        def _():
            acc[...] = part

        @pl.when(k > 0)
        def _():
            acc[...] += part

        @pl.when(k == N_ROW_TILES_BIG - 1)
        def _():
            out_ref[...] = acc[...]

    return pl.pallas_call(
        body, name="dw_in", grid=(nsh, ncol, N_ROW_TILES_BIG),
        in_specs=[pl.BlockSpec((tmb, d), lambda j, n, k: (k, 0)),
                  pl.BlockSpec((tmb, cw), lambda j, n, k: (k, j * ncol + n))],
        out_specs=pl.BlockSpec((None, None, d, cw), lambda j, n, k: (0, j, 0, n)),
        out_shape=jax.ShapeDtypeStruct((1, nsh, d, sw), F32),
        scratch_shapes=[pltpu.VMEM((d, cw), F32)],
        compiler_params=_params(("arbitrary",) * 3),
    )(h, dproj)


def _dw_square(lhs3, rhs3):
    n, tp, d = lhs3.shape
    tmb = tp // N_ROW_TILES_BIG

    def body(a_ref, b_ref, out_ref, acc):
        k = pl.program_id(1)
        part = lax.dot_general(a_ref[...], b_ref[...], (((0,), (0,)), ((), ())), preferred_element_type=F32)

        @pl.when(k == 0)
        def _():
            acc[...] = part

        @pl.when(k > 0)
        def _():
            acc[...] += part

        @pl.when(k == N_ROW_TILES_BIG - 1)
        def _():
            out_ref[...] = acc[...]

    return pl.pallas_call(
        body, name="dw_square", grid=(n, N_ROW_TILES_BIG),
        in_specs=[pl.BlockSpec((None, tmb, d), lambda a, k: (a, k, 0)),
                  pl.BlockSpec((None, tmb, d), lambda a, k: (a, k, 0))],
        out_specs=pl.BlockSpec((None, d, d), lambda a, k: (a, 0, 0)),
        out_shape=jax.ShapeDtypeStruct((n, d, d), F32),
        scratch_shapes=[pltpu.VMEM((d, d), F32)],
        compiler_params=_params(("arbitrary",) * 2),
    )(lhs3, rhs3)


def _conv_a_taps(first_lag, last_lag):
    out = []
    for r in range(8):
        taps = [(q, 8 * q + r) for q in range(5) if first_lag <= 8 * q + r <= last_lag]
        if taps:
            out.append((r, taps))
    return out


def _mix_fwd(s_pad, proj, target, w3, wa, wb, conv_a_b, ln_g, ln_b, b_a_out, final_g, norm_g):
    tp, d = s_pad.shape
    nt = tp // TM
    nrb = TM // RB
    shl = TM + SHIFT_ROWS

    def body(s_ref, proj_ref, tgt_ref, w3_ref, wa_ref, wb_ref, cab_ref, lng_ref, lnb_ref, bao_ref, fg_ref, ng_ref,
             ca_ref, cb_ref, ya_ref, yb_ref, abm_ref, ds2_ref, h_ref, loss_ref, dfg_ref,
             ext_a, ext_b, sh, s2_s, lacc, gacc):
        i = pl.program_id(0)

        def split(k, rows):
            return proj_ref[rows, k * d:(k + 1) * d].astype(F32)

        s_in = s_ref[...]
        h_ref[...] = (s_in * lax.rsqrt(jnp.mean(s_in * s_in, axis=-1, keepdims=True) + EPS)
                      * ng_ref[...]).astype(BF16)

        @pl.when(i == 0)
        def _():
            ext_a[0:HALO_A, :] = jnp.zeros((HALO_A, d), F32)
            ext_b[0:HALO_B, :] = jnp.zeros((HALO_B, d), F32)
            lacc[...] = jnp.zeros_like(lacc)
            gacc[...] = jnp.zeros_like(gacc)

        def conv_in(rb, carry):
            rows = _rows(rb)
            ua0 = split(0, rows) * _sigmoid(split(1, rows))
            ext_a[pl.ds(pl.multiple_of(HALO_A + rb * RB, 8), RB), :] = ua0
            ext_b[pl.ds(pl.multiple_of(HALO_B + rb * RB, 8), RB), :] = split(4, rows) * split(5, rows)
            ca_ref[rows, :] = jnp.broadcast_to(cab_ref[...], (RB, d))
            return carry
        lax.fori_loop(0, nrb, conv_in, 0)

        for r, taps in _conv_a_taps(HALO_A - CONV_A + 1, HALO_A):
            if r == 0:
                src = ext_a
            else:
                sh[...] = ext_a[r:r + shl, :]
                src = sh

            def conv_acc(rb, carry, src=src, taps=taps):
                rows = _rows(rb)
                acc = ca_ref[rows, :]
                for q, lag in taps:
                    k = lag - (HALO_A - CONV_A + 1)
                    acc = acc + src[pl.ds(pl.multiple_of(rb * RB + 8 * q, 8), RB), :] * wa_ref[k:k + 1, :]
                ca_ref[rows, :] = acc
                return carry
            lax.fori_loop(0, nrb, conv_acc, 0)
        ext_a[0:HALO_A, :] = ext_a[TM:TM + HALO_A, :]

        cb_ref[...] = ext_b[HALO_B:HALO_B + TM, :] * wb_ref[2:3, :]
        for k in range(CONV_B - 1):
            off = HALO_B - CONV_B + 1 + k
            sh[0:TM, :] = ext_b[off:off + TM, :]
            cb_ref[...] += sh[0:TM, :] * wb_ref[k:k + 1, :]
        ext_b[0:HALO_B, :] = ext_b[TM:TM + HALO_B, :]

        def branches(rb, carry):
            rows = _rows(rb)
            ca = ca_ref[rows, :]
            mu = jnp.mean(ca, axis=-1, keepdims=True)
            xc = ca - mu
            rstd = lax.rsqrt(jnp.mean(xc * xc, axis=-1, keepdims=True) + EPS)
            ln = xc * rstd * lng_ref[...] + lnb_ref[...]
            ua = ln * _sigmoid(ln)
            a_z = split(2, rows)
            abm_ref[0, rows, :] = (ua * (a_z * _sigmoid(a_z))).astype(BF16)
            b_z = split(6, rows)
            ub = split(3, rows) * cb_ref[rows, :]
            abm_ref[1, rows, :] = (ub * (b_z * _sigmoid(b_z))).astype(BF16)
            return carry
        lax.fori_loop(0, nrb, branches, 0)

        ya_ref[...] = jnp.dot(abm_ref[0], w3_ref[0], preferred_element_type=F32) + bao_ref[...]
        yb_ref[...] = jnp.dot(abm_ref[1], w3_ref[1], preferred_element_type=F32)

        def merge(rb, carry):
            rows = _rows(rb)
            m = _sigmoid(split(7, rows)) * ya_ref[rows, :] + _sigmoid(split(8, rows)) * yb_ref[rows, :]
            abm_ref[2, rows, :] = m.astype(BF16)
            return carry
        lax.fori_loop(0, nrb, merge, 0)

        s2_s[...] = s_ref[...] + jnp.dot(abm_ref[2], w3_ref[2], preferred_element_type=F32)
        live = (i > 0).astype(F32)

        def head(rb, carry):
            rows = _rows(rb)
            s2 = s2_s[rows, :]
            r2 = lax.rsqrt(jnp.mean(s2 * s2, axis=-1, keepdims=True) + EPS)
            diff = (s2 * r2 * fg_ref[...] - tgt_ref[rows, :]) * live
            lacc[...] += diff * diff
            dy = diff * (1.0 / d)
            gacc[...] += (dy * s2 * r2).reshape(RB // 8, 8, d).sum(axis=0)
            t = dy * fg_ref[...]
            ds2_ref[rows, :] = r2 * t - s2 * (r2 * r2 * r2) * jnp.mean(t * s2, axis=-1, keepdims=True)
            return carry
        lax.fori_loop(0, nrb, head, 0)

        @pl.when(i == nt - 1)
        def _():
            loss_ref[...] = jnp.broadcast_to(0.5 * jnp.sum(lacc[...]) * (1.0 / d), (8, 128))
            dfg_ref[...] = jnp.broadcast_to(jnp.sum(gacc[...], axis=0, keepdims=True), (8, d))

    row_f32 = pl.BlockSpec((TM, d), lambda i: (i, 0))
    const = lambda shape: pl.BlockSpec(shape, lambda i: (0,) * len(shape))
    return pl.pallas_call(
        body, name="f2_mix", grid=(nt,),
        in_specs=[row_f32,
                  pl.BlockSpec((TM, N_SPLIT * d), lambda i: (i, 0)),
                  pl.BlockSpec((TM, d), lambda i: (jnp.maximum(i - 1, 0), 0)),
                  const((3, d, d)), const(wa.shape), const(wb.shape)] + [const((1, d))] * 6,
        out_specs=[row_f32, row_f32, row_f32, row_f32,
                   pl.BlockSpec((3, TM, d), lambda i: (0, i, 0)),
                   row_f32, row_f32, const((8, 128)), const((8, d))],
        out_shape=[jax.ShapeDtypeStruct((tp, d), F32)] * 4
        + [jax.ShapeDtypeStruct((3, tp, d), BF16), jax.ShapeDtypeStruct((tp, d), F32),
           jax.ShapeDtypeStruct((tp, d), BF16),
           jax.ShapeDtypeStruct((8, 128), F32), jax.ShapeDtypeStruct((8, d), F32)],
        scratch_shapes=[pltpu.VMEM((HALO_A + TM, d), F32), pltpu.VMEM((HALO_B + TM, d), F32),
                        pltpu.VMEM((shl, d), F32), pltpu.VMEM((TM, d), F32),
                        pltpu.VMEM((RB, d), F32), pltpu.VMEM((8, d), F32)],
        compiler_params=_params(("arbitrary",)),
    )(s_pad, proj, target, w3, wa, wb, conv_a_b, ln_g, ln_b, b_a_out, final_g, norm_g)


def _mix_bwd(ds2, proj, ca, cb, ya, yb, w3, wa, wb, ln_g, ln_b):
    tp, d = ds2.shape
    nt = tp // TM
    nrb = TM // RB
    shl = TM + SHIFT_ROWS
    nt_dims = (((1,), (1,)), ((), ()))

    def body(ds2_ref, proj_ref, ca_ref, cb_ref, ya_ref, yb_ref, w3_ref, wa_ref, wb_ref, lng_ref, lnb_ref,
             dproj_ref, d3_ref, sm_ref, ext_d, ext_e, sh, dm_s, dpa_s, dpb_s, dua0_s, acc):
        step = pl.program_id(0)

        def split(k, rows):
            return proj_ref[rows, k * d:(k + 1) * d].astype(F32)

        def put(k, rows, val):
            dproj_ref[rows, k * d:(k + 1) * d] = val.astype(BF16)

        def accum(row, val):
            acc[row] += val.reshape(RB // 8, 8, d).sum(axis=0)

        @pl.when(step == 0)
        def _():
            ext_d[TM:TM + HALO_A, :] = jnp.zeros((HALO_A, d), F32)
            ext_e[TM:TM + HALO_B, :] = jnp.zeros((HALO_B, d), F32)
            acc[...] = jnp.zeros_like(acc)

        d3_ref[2] = ds2_ref[...].astype(BF16)
        dm_s[...] = lax.dot_general(d3_ref[2], w3_ref[2], nt_dims, preferred_element_type=F32)

        def gates(rb, carry):
            rows = _rows(rb)
            dm = dm_s[rows, :]
            sa = _sigmoid(split(7, rows))
            sb = _sigmoid(split(8, rows))
            ya_v = ya_ref[rows, :]
            yb_v = yb_ref[rows, :]
            put(7, rows, dm * ya_v * sa * (1.0 - sa))
            put(8, rows, dm * yb_v * sb * (1.0 - sb))
            dya = dm * sa
            accum(ROW_DBAO, dya)
            d3_ref[0, rows, :] = dya.astype(BF16)
            d3_ref[1, rows, :] = (dm * sb).astype(BF16)
            return carry
        lax.fori_loop(0, nrb, gates, 0)

        dpa_s[...] = lax.dot_general(d3_ref[0], w3_ref[0], nt_dims, preferred_element_type=F32)
        dpb_s[...] = lax.dot_general(d3_ref[1], w3_ref[1], nt_dims, preferred_element_type=F32)

        def branches(rb, carry):
            rows = _rows(rb)
            ca_v = ca_ref[rows, :]
            mu = jnp.mean(ca_v, axis=-1, keepdims=True)
            xc = ca_v - mu
            rstd = lax.rsqrt(jnp.mean(xc * xc, axis=-1, keepdims=True) + EPS)
            xhat = xc * rstd
            ln = xhat * lng_ref[...] + lnb_ref[...]
            sl = _sigmoid(ln)
            ua = ln * sl
            a_z = split(2, rows)
            sz = _sigmoid(a_z)
            dpa = dpa_s[rows, :]
            put(2, rows, dpa * ua * (sz * (1.0 + a_z * (1.0 - sz))))
            dln = dpa * (a_z * sz) * (sl * (1.0 + ln * (1.0 - sl)))
            accum(ROW_DLNG, dln * xhat)
            accum(ROW_DLNB, dln)
            dxh = dln * lng_ref[...]
            dca = rstd * (dxh - jnp.mean(dxh, axis=-1, keepdims=True)
                          - xhat * jnp.mean(dxh * xhat, axis=-1, keepdims=True))
            accum(ROW_DCAB, dca)
            ext_d[rows, :] = dca
            dua0_s[rows, :] = jnp.zeros((RB, d), F32)
            b_z = split(6, rows)
            szb = _sigmoid(b_z)
            dpb = dpb_s[rows, :]
            b_b = split(3, rows)
            cb_v = cb_ref[rows, :]
            put(6, rows, dpb * (b_b * cb_v) * (szb * (1.0 + b_z * (1.0 - szb))))
            dub = dpb * (b_z * szb)
            put(3, rows, dub * cb_v)
            ext_e[rows, :] = dub * b_b
            return carry
        lax.fori_loop(0, nrb, branches, 0)

        for r, taps in _conv_a_taps(0, CONV_A - 1):
            if r == 0:
                src = ext_d
            else:
                sh[...] = ext_d[r:r + shl, :]
                src = sh

            def conv_t(rb, carry, src=src, taps=taps):
                rows = _rows(rb)
                ua0 = split(0, rows) * _sigmoid(split(1, rows))
                dua0 = dua0_s[rows, :]
                for q, lag in taps:
                    k = CONV_A - 1 - lag
                    slab = src[pl.ds(pl.multiple_of(rb * RB + 8 * q, 8), RB), :]
                    dua0 = dua0 + slab * wa_ref[k:k + 1, :]
                    accum(ROW_DWA + k, slab * ua0)
                dua0_s[rows, :] = dua0
                return carry
            lax.fori_loop(0, nrb, conv_t, 0)
        ext_d[TM:TM + HALO_A, :] = ext_d[0:HALO_A, :]

        dpb_s[...] = ext_e[0:TM, :] * wb_ref[CONV_B - 1:CONV_B, :]
        for lag in range(CONV_B):
            k = CONV_B - 1 - lag
            if lag > 0:
                sh[0:TM, :] = ext_e[lag:lag + TM, :]
                dpb_s[...] += sh[0:TM, :] * wb_ref[k:k + 1, :]
            src = ext_e if lag == 0 else sh

            def conv_b_w(rb, carry, src=src, k=k):
                rows = _rows(rb)
                accum(ROW_DWB + k, src[rows, :] * (split(4, rows) * split(5, rows)))
                return carry
            lax.fori_loop(0, nrb, conv_b_w, 0)
        ext_e[TM:TM + HALO_B, :] = ext_e[0:HALO_B, :]

        def inputs(rb, carry):
            rows = _rows(rb)
            dua0 = dua0_s[rows, :]
            a_val = split(0, rows)
            sg = _sigmoid(split(1, rows))
            put(0, rows, dua0 * sg)
            put(1, rows, dua0 * a_val * sg * (1.0 - sg))
            dcbin = dpb_s[rows, :]
            put(4, rows, dcbin * split(5, rows))
            put(5, rows, dcbin * split(4, rows))
            return carry
        lax.fori_loop(0, nrb, inputs, 0)

        @pl.when(step == nt - 1)
        def _():
            for row in range(SM_ROWS):
                sm_ref[row:row + 1, :] = jnp.sum(acc[row], axis=0, keepdims=True)

    rev = lambda i: (nt - 1 - i, 0)
    row_f32 = pl.BlockSpec((TM, d), rev)
    const = lambda shape: pl.BlockSpec(shape, lambda i: (0,) * len(shape))
    return pl.pallas_call(
        body, name="b1_mix", grid=(nt,),
        in_specs=[row_f32, pl.BlockSpec((TM, N_SPLIT * d), rev), row_f32, row_f32, row_f32, row_f32,
                  const((3, d, d)), const(wa.shape), const(wb.shape), const((1, d)), const((1, d))],
        out_specs=[pl.BlockSpec((TM, N_SPLIT * d), rev),
                   pl.BlockSpec((3, TM, d), lambda i: (0, nt - 1 - i, 0)),
                   const((SM_ROWS, d))],
        out_shape=[jax.ShapeDtypeStruct((tp, N_SPLIT * d), BF16), jax.ShapeDtypeStruct((3, tp, d), BF16),
                   jax.ShapeDtypeStruct((SM_ROWS, d), F32)],
        scratch_shapes=[pltpu.VMEM((TM + HALO_A, d), F32), pltpu.VMEM((TM + HALO_B, d), F32),
                        pltpu.VMEM((shl, d), F32), pltpu.VMEM((TM, d), F32), pltpu.VMEM((TM, d), F32),
                        pltpu.VMEM((TM, d), F32), pltpu.VMEM((TM, d), F32),
                        pltpu.VMEM((SM_ROWS, 8, d), F32)],
        compiler_params=_params(("arbitrary",)),
    )(ds2, proj, ca, cb, ya, yb, w3, wa, wb, ln_g, ln_b)


def kernel(x, meta_tokens, norm_g, w_in, conv_a_w, conv_a_b, ln_a_g, ln_a_b, w_a_out, b_a_out, conv_b_w, w_b_out, w_out, final_g, loss_target, m_meta_tokens, m_norm_g, m_w_in, m_conv_a_w, m_conv_a_b, m_ln_a_g, m_ln_a_b, m_w_a_out, m_b_a_out, m_conv_b_w, m_w_b_out, m_w_out, m_final_g, v_meta_tokens, v_norm_g, v_w_in, v_conv_a_w, v_conv_a_b, v_ln_a_g, v_ln_a_b, v_w_a_out, v_b_a_out, v_conv_b_w, v_w_b_out, v_w_out, v_final_g):
    seq, d = x.shape[1], x.shape[2]
    dc = meta_tokens.shape[1]
    sw = w_in.shape[2]
    rsh = w_a_out.shape[1]
    xi, yi, ci = _mesh_pos()
    me = 2 * xi + yi
    pos = jnp.stack([ci, me]).astype(jnp.int32)

    smalls = jnp.concatenate([
        jnp.pad(conv_a_w[0], ((0, HALO_A - CONV_A), (0, 0))),
        jnp.pad(conv_b_w[0], ((0, HALO_B - CONV_B), (0, 0))),
        meta_tokens, jnp.zeros((8, dc), F32)], axis=0)[None]
    w3_own = jnp.stack([w_a_out[0], w_b_out[0], w_out[0]])
    wg_in, wg3, smg = _all_gather([_place_own(w_in, pos, BF16, "place_in"),
                                   _place_own(w3_own, pos, BF16, "place_sq"),
                                   _place_own(smalls, pos, F32, "place_small")])
    w3 = wg3.reshape(3, N_CHIPS * rsh, d)
    smg = jnp.transpose(smg[0], (1, 0, 2)).reshape(smalls.shape[1], N_CHIPS * dc)
    wa_full = smg[0:HALO_A]
    wb_full = smg[HALO_A:HALO_A + HALO_B]
    meta_full = smg[HALO_A + HALO_B:HALO_A + HALO_B + N_META]
    fg2 = final_g.reshape(1, d)

    first_tile = jnp.concatenate([jnp.zeros((TM - N_META, d), F32), meta_full], axis=0)
    s_pad = jnp.concatenate([first_tile, x[0]], axis=0)

    proj = _proj_fwd(s_pad, norm_g, wg_in)
    ca, cb, ya, yb, abm, ds2, h, loss8, dfg8 = _mix_fwd(
        s_pad, proj, loss_target[0], w3, wa_full, wb_full, conv_a_b, ln_a_g, ln_a_b, b_a_out, fg2, norm_g)
    dproj, d3, sm = _mix_bwd(ds2, proj, ca, cb, ya, yb, w3, wa_full, wb_full, ln_a_g, ln_a_b)
    ds, dng8 = _dh_bwd(dproj, wg_in, s_pad, ds2, norm_g)
    g_in = _dw_in(h, dproj, N_CHIPS)
    g_sq = _dw_square(abm, d3).reshape(3, N_CHIPS, rsh, d)

    r_in, r_sq = _sibling_exchange([g_in, g_sq])
    p32_in, pbf_in = _add_sibling(g_in, r_in, pos, "rs_add_in")
    p32_sq, pbf_sq = _add_sibling(g_sq, r_sq, pos, "rs_add_sq")
    l_in, l_sq = _chip_exchange([pbf_in, pbf_sq])
    half_in = _sum_chips(p32_in, l_in, pos, "rs_sum_in")
    half_sq = _sum_chips(p32_sq, l_sq, pos, "rs_sum_sq")
    other_in, other_sq = _sibling_swap([half_in, half_sq])

    tail_row = lax.broadcasted_iota(jnp.int32, (8, d), 0)
    tail = jnp.where(tail_row == 0, dng8, jnp.where(tail_row == 1, dfg8,
                     jnp.where(tail_row == 2, loss8[0, 0], 0.0)))
    block = jnp.concatenate([sm, ds[TM - N_META:TM], tail], axis=0)
    red = _all_reduce_small(block)
    col = lax.dynamic_slice(red, (0, me * dc), (AR_ROWS, dc))
    g_small = {
        "meta_tokens": col[ROW_DMETA:ROW_DMETA + N_META],
        "norm_g": red[ROW_DNG:ROW_DNG + 1],
        "conv_a_w": col[ROW_DWA:ROW_DWA + CONV_A][None],
        "conv_a_b": red[ROW_DCAB:ROW_DCAB + 1],
        "ln_a_g": red[ROW_DLNG:ROW_DLNG + 1],
        "ln_a_b": red[ROW_DLNB:ROW_DLNB + 1],
        "b_a_out": red[ROW_DBAO:ROW_DBAO + 1],
        "conv_b_w": col[ROW_DWB:ROW_DWB + CONV_B][None],
        "final_g": red[ROW_DFG],
    }

    upd_in = _adam_halves([w_in], [m_w_in], [v_w_in], half_in, other_in, pos, "adam_in")
    upd_sq = _adam_halves([w_a_out, w_b_out, w_out], [m_w_a_out, m_w_b_out, m_w_out],
                          [v_w_a_out, v_w_b_out, v_w_out], half_sq, other_sq, pos, "adam_sq")
    small_w = {"meta_tokens": (meta_tokens, m_meta_tokens, v_meta_tokens), "norm_g": (norm_g, m_norm_g, v_norm_g),
               "conv_a_w": (conv_a_w, m_conv_a_w, v_conv_a_w), "conv_a_b": (conv_a_b, m_conv_a_b, v_conv_a_b),
               "ln_a_g": (ln_a_g, m_ln_a_g, v_ln_a_g), "ln_a_b": (ln_a_b, m_ln_a_b, v_ln_a_b),
               "b_a_out": (b_a_out, m_b_a_out, v_b_a_out), "conv_b_w": (conv_b_w, m_conv_b_w, v_conv_b_w),
               "final_g": (final_g, m_final_g, v_final_g)}
    names_small = list(small_w)
    as2d = lambda t: t.reshape(-1, t.shape[-1])
    upd_small = _adam_small([(as2d(small_w[k][0]), as2d(g_small[k]), as2d(small_w[k][1]), as2d(small_w[k][2]))
                             for k in names_small])

    grads, deltas, new_m, new_v = dict(g_small), {}, {}, {}
    for k, upd in zip(names_small, upd_small):
        deltas[k], new_m[k], new_v[k] = [t.reshape(small_w[k][0].shape) for t in upd]
    grads["w_in"], deltas["w_in"], new_m["w_in"], new_v["w_in"] = upd_in
    for idx, k in enumerate(["w_a_out", "w_b_out", "w_out"]):
        grads[k], deltas[k], new_m[k], new_v[k] = upd_sq[4 * idx:4 * idx + 4]

    loss = red[ROW_LOSS, 0]
    grad_x = ds[TM:][None]
    order = ["meta_tokens", "norm_g", "w_in", "conv_a_w", "conv_a_b", "ln_a_g", "ln_a_b", "w_a_out", "b_a_out",
             "conv_b_w", "w_b_out", "w_out", "final_g"]
    return (loss, grad_x, *[grads[k] for k in order], *[deltas[k] for k in order],
            *[new_m[k] for k in order], *[new_v[k] for k in order])
```

```python
import functools

import jax
import jax.numpy as jnp
from jax import lax
from jax.experimental import pallas as pl
from jax.experimental.pallas import tpu as pltpu

F32 = jnp.float32
BF16 = jnp.bfloat16
MESH = pl.DeviceIdType.MESH

EPS = 1e-6
N_META = 16
N_SPLIT = 9
CONV_A = 31
CONV_B = 3
HALO_A = 32
HALO_B = 8
SHIFT_ROWS = 24
TM = 256
RB = 32
N_ROW_TILES_BIG = 8
N_CHIPS = 4
VMEM_LIMIT = 56 * 1024 * 1024

ADAM_LR = 0.001
ADAM_B1 = 0.9
ADAM_B2 = 0.999
ADAM_EPS = 1e-08
ADAM_WD = 0.01
ADAM_STEP = 10

ROW_DWA = 0
ROW_DWB = 32
ROW_DCAB = 40
ROW_DLNG = 41
ROW_DLNB = 42
ROW_DBAO = 43
SM_ROWS = 48
ROW_DMETA = 48
ROW_DNG = 64
ROW_DFG = 65
ROW_LOSS = 66
AR_ROWS = 72


def _sigmoid(v):
    return 1.0 / (1.0 + jnp.exp(-v))


def _params(sem, **kw):
    return pltpu.CompilerParams(dimension_semantics=sem, vmem_limit_bytes=VMEM_LIMIT, **kw)


def _rows(rb):
    return pl.ds(pl.multiple_of(rb * RB, RB), RB)


def _mesh_pos():
    x, y, c = lax.axis_index("x"), lax.axis_index("y"), lax.axis_index("c")
    return x, y, c


def _half(ref, j, c):
    h = ref.shape[2] // 2
    return ref.at[:, j, pl.ds(c * h, h), :]


def _place_own(shard, pos, dtype, name):
    s, r, c = shard.shape
    rb = 128 if r % 128 == 0 else r

    def body(pos_ref, x_ref, o_ref):
        o_ref[...] = x_ref[...].astype(dtype)

    return pl.pallas_call(
        body, name=name,
        grid_spec=pltpu.PrefetchScalarGridSpec(
            num_scalar_prefetch=1, grid=(s, r // rb),
            in_specs=[pl.BlockSpec((None, rb, c), lambda si, b, pos_ref: (si, b, 0))],
            out_specs=pl.BlockSpec((None, None, rb, c), lambda si, b, pos_ref: (si, pos_ref[1], b, 0))),
        out_shape=jax.ShapeDtypeStruct((s, N_CHIPS, r, c), dtype),
        compiler_params=_params(("arbitrary",) * 2),
    )(pos, shard)


def _all_gather(bufs):
    n = len(bufs)

    def body(*refs):
        outs = refs[n:2 * n]
        send_sems, recv_sems = refs[2 * n:]
        x, y, c = _mesh_pos()
        me = 2 * x + y
        sibling = (x, y, 1 - c)
        chips = [(1 - x, y), (x, 1 - y), (1 - x, 1 - y)]

        def remote(a, k, piece_src, piece_dst, to):
            return pltpu.make_async_remote_copy(
                src_ref=piece_src, dst_ref=piece_dst, send_sem=send_sems.at[6 * a + k],
                recv_sem=recv_sems.at[6 * a + k], device_id=to, device_id_type=MESH)

        sends = []
        for a in range(n):
            mine = _half(outs[a], me, c)
            for k, (px, py) in enumerate(chips):
                sends.append(remote(a, k, mine, mine, (px, py, c)))
        for cp in sends:
            cp.start()
        for a in range(n):
            for k, (px, py) in enumerate(chips):
                piece = _half(outs[a], 2 * px + py, c)
                remote(a, k, piece, piece, (px, py, c)).wait_recv()
                fwd = remote(a, 3 + k, piece, piece, sibling)
                fwd.start()
                sends.append(fwd)
        for a in range(n):
            for k, (px, py) in enumerate(chips):
                piece = _half(outs[a], 2 * px + py, 1 - c)
                remote(a, 3 + k, piece, piece, sibling).wait_recv()
        for cp in sends:
            cp.wait_send()

    any_spec = pl.BlockSpec(memory_space=pl.ANY)
    return pl.pallas_call(
        body, name="ag_weights",
        in_specs=[any_spec] * n, out_specs=[any_spec] * n,
        out_shape=[jax.ShapeDtypeStruct(b.shape, b.dtype) for b in bufs],
        input_output_aliases={a: a for a in range(n)},
        scratch_shapes=[pltpu.SemaphoreType.DMA((6 * n,)), pltpu.SemaphoreType.DMA((6 * n,))],
    )(*bufs)


def _sibling_exchange(grads):
    n = len(grads)

    def body(*refs):
        ins, outs = refs[:n], refs[n:2 * n]
        send_sems, recv_sems = refs[2 * n:]
        x, y, c = _mesh_pos()
        copies = []
        for a in range(n):
            h = ins[a].shape[2] // 2
            copies.append(pltpu.make_async_remote_copy(
                src_ref=ins[a].at[:, :, pl.ds((1 - c) * h, h), :], dst_ref=outs[a],
                send_sem=send_sems.at[a], recv_sem=recv_sems.at[a],
                device_id=(x, y, 1 - c), device_id_type=MESH))
        for cp in copies:
            cp.start()
        for cp in copies:
            cp.wait()

    any_spec = pl.BlockSpec(memory_space=pl.ANY)
    return pl.pallas_call(
        body, name="rs_sibling",
        in_specs=[any_spec] * n, out_specs=[any_spec] * n,
        out_shape=[jax.ShapeDtypeStruct(g.shape[:2] + (g.shape[2] // 2, g.shape[3]), g.dtype) for g in grads],
        scratch_shapes=[pltpu.SemaphoreType.DMA((n,)), pltpu.SemaphoreType.DMA((n,))],
    )(*grads)


def _chip_exchange(parts):
    n = len(parts)

    def body(*refs):
        ins, outs = refs[:n], refs[n:2 * n]
        send_sems, recv_sems = refs[2 * n:]
        x, y, c = _mesh_pos()
        me = 2 * x + y
        chips = [(1 - x, y), (x, 1 - y), (1 - x, 1 - y)]
        sends = []
        for a in range(n):
            for k, (px, py) in enumerate(chips):
                sends.append(pltpu.make_async_remote_copy(
                    src_ref=ins[a].at[:, 2 * px + py], dst_ref=outs[a].at[:, me],
                    send_sem=send_sems.at[3 * a + k], recv_sem=recv_sems.at[3 * a + k],
                    device_id=(px, py, c), device_id_type=MESH))
        for cp in sends:
            cp.start()
        for a in range(n):
            for k, (px, py) in enumerate(chips):
                landed = outs[a].at[:, 2 * px + py]
                pltpu.make_async_remote_copy(
                    src_ref=landed, dst_ref=landed, send_sem=send_sems.at[3 * a + k],
                    recv_sem=recv_sems.at[3 * a + k], device_id=(px, py, c), device_id_type=MESH).wait_recv()
        for cp in sends:
            cp.wait_send()

    any_spec = pl.BlockSpec(memory_space=pl.ANY)
    return pl.pallas_call(
        body, name="rs_chips",
        in_specs=[any_spec] * n, out_specs=[any_spec] * n,
        out_shape=[jax.ShapeDtypeStruct(p.shape, p.dtype) for p in parts],
        scratch_shapes=[pltpu.SemaphoreType.DMA((3 * n,)), pltpu.SemaphoreType.DMA((3 * n,))],
    )(*parts)


def _sibling_swap(halves):
    n = len(halves)

    def body(*refs):
        ins, outs = refs[:n], refs[n:2 * n]
        send_sems, recv_sems = refs[2 * n:]
        x, y, c = _mesh_pos()
        copies = [pltpu.make_async_remote_copy(
            src_ref=ins[a], dst_ref=outs[a], send_sem=send_sems.at[a], recv_sem=recv_sems.at[a],
            device_id=(x, y, 1 - c), device_id_type=MESH) for a in range(n)]
        for cp in copies:
            cp.start()
        for cp in copies:
            cp.wait()

    any_spec = pl.BlockSpec(memory_space=pl.ANY)
    return pl.pallas_call(
        body, name="rs_swap",
        in_specs=[any_spec] * n, out_specs=[any_spec] * n,
        out_shape=[jax.ShapeDtypeStruct(h.shape, h.dtype) for h in halves],
        scratch_shapes=[pltpu.SemaphoreType.DMA((n,)), pltpu.SemaphoreType.DMA((n,))],
    )(*halves)


def _all_reduce_small(block):
    rows, d = block.shape

    def body(x_ref, out_ref, sib_ref, part_ref, peers_ref, send_sems, recv_sems):
        x, y, c = _mesh_pos()
        me = 2 * x + y
        chips = [(1 - x, y), (x, 1 - y), (1 - x, 1 - y)]
        swap = pltpu.make_async_remote_copy(
            src_ref=x_ref, dst_ref=sib_ref, send_sem=send_sems.at[0], recv_sem=recv_sems.at[0],
            device_id=(x, y, 1 - c), device_id_type=MESH)
        swap.start()
        swap.wait()
        part_ref[...] = x_ref[...] + sib_ref[...]
        peers_ref[me] = part_ref[...]
        sends = [pltpu.make_async_remote_copy(
            src_ref=part_ref, dst_ref=peers_ref.at[me], send_sem=send_sems.at[1 + k], recv_sem=recv_sems.at[1 + k],
            device_id=(px, py, c), device_id_type=MESH) for k, (px, py) in enumerate(chips)]
        for cp in sends:
            cp.start()
        for k, (px, py) in enumerate(chips):
            landed = peers_ref.at[2 * px + py]
            pltpu.make_async_remote_copy(
                src_ref=landed, dst_ref=landed, send_sem=send_sems.at[1 + k], recv_sem=recv_sems.at[1 + k],
                device_id=(px, py, c), device_id_type=MESH).wait_recv()
        for cp in sends:
            cp.wait_send()
        out_ref[...] = ((peers_ref[0] + peers_ref[1]) + peers_ref[2]) + peers_ref[3]

    vm = pl.BlockSpec(memory_space=pltpu.VMEM)
    return pl.pallas_call(
        body, name="ar_small",
        in_specs=[vm], out_specs=vm,
        out_shape=jax.ShapeDtypeStruct((rows, d), F32),
        scratch_shapes=[pltpu.VMEM((rows, d), F32), pltpu.VMEM((rows, d), F32),
                        pltpu.VMEM((N_CHIPS, rows, d), F32),
                        pltpu.SemaphoreType.DMA((4,)), pltpu.SemaphoreType.DMA((4,))],
    )(block)


def _add_sibling(grad, recv, pos, name):
    s, nch, r, c = grad.shape
    h = r // 2
    hb = min(h, 128)

    def body(pos_ref, g_ref, r_ref, p32_ref, pbf_ref):
        p = g_ref[...] + r_ref[...]
        p32_ref[...] = p
        pbf_ref[...] = p.astype(BF16)

    nb = h // hb
    spec_g = pl.BlockSpec((None, None, hb, c), lambda si, j, b, pos_ref: (si, j, pos_ref[0] * nb + b, 0))
    spec_h = pl.BlockSpec((None, None, hb, c), lambda si, j, b, pos_ref: (si, j, b, 0))
    return pl.pallas_call(
        body, name=name,
        grid_spec=pltpu.PrefetchScalarGridSpec(
            num_scalar_prefetch=1, grid=(s, nch, nb), in_specs=[spec_g, spec_h], out_specs=[spec_h, spec_h]),
        out_shape=[jax.ShapeDtypeStruct((s, nch, h, c), F32), jax.ShapeDtypeStruct((s, nch, h, c), BF16)],
        compiler_params=_params(("arbitrary",) * 3),
    )(pos, grad, recv)


def _sum_chips(p32, landed, pos, name):
    s, nch, h, c = p32.shape
    hb = min(h, 128)

    def body(pos_ref, p_ref, l1_ref, l2_ref, l3_ref, out_ref):
        out_ref[...] = ((p_ref[...] + l1_ref[...].astype(F32)) + l2_ref[...].astype(F32)) + l3_ref[...].astype(F32)

    def slot(k):
        return pl.BlockSpec((None, None, hb, c), lambda si, b, pos_ref: (si, (pos_ref[1] + k) % N_CHIPS, b, 0))

    return pl.pallas_call(
        body, name=name,
        grid_spec=pltpu.PrefetchScalarGridSpec(
            num_scalar_prefetch=1, grid=(s, h // hb),
            in_specs=[slot(0), slot(1), slot(2), slot(3)],
            out_specs=pl.BlockSpec((None, hb, c), lambda si, b, pos_ref: (si, b, 0))),
        out_shape=jax.ShapeDtypeStruct((s, h, c), F32),
        compiler_params=_params(("arbitrary",) * 2),
    )(pos, p32, landed, landed, landed)


def _adamw(w, g, m, v):
    m = ADAM_B1 * m + (1.0 - ADAM_B1) * g
    v = ADAM_B2 * v + (1.0 - ADAM_B2) * (g * g)
    m_hat = m / (1.0 - ADAM_B1 ** ADAM_STEP)
    v_hat = v / (1.0 - ADAM_B2 ** ADAM_STEP)
    delta = -ADAM_LR * (m_hat / (jnp.sqrt(v_hat) + ADAM_EPS) + ADAM_WD * w)
    return delta, m, v


def _adam_halves(ws, ms, vs, g_own, g_recv, pos, name):
    n = len(ws)
    _, r, c = ws[0].shape
    h = r // 2
    rb = min(h, 128)
    nb = h // rb

    def body(pos_ref, *refs):
        w_refs, m_refs, v_refs = refs[:n], refs[n:2 * n], refs[2 * n:3 * n]
        go_ref, gr_ref = refs[3 * n:3 * n + 2]
        outs = refs[3 * n + 2:]
        mine = pl.program_id(0) == pos_ref[0]
        for a in range(n):
            g = jnp.where(mine, go_ref[a], gr_ref[a])
            delta, m, v = _adamw(w_refs[a][...], g, m_refs[a][...], v_refs[a][...])
            outs[4 * a][...], outs[4 * a + 1][...], outs[4 * a + 2][...], outs[4 * a + 3][...] = g, delta, m, v

    spec_w = pl.BlockSpec((None, rb, c), lambda hf, b, pos_ref: (0, hf * nb + b, 0))
    spec_g = pl.BlockSpec((n, rb, c), lambda hf, b, pos_ref: (0, b, 0))
    return pl.pallas_call(
        body, name=name,
        grid_spec=pltpu.PrefetchScalarGridSpec(
            num_scalar_prefetch=1, grid=(2, nb), in_specs=[spec_w] * (3 * n) + [spec_g] * 2,
            out_specs=[spec_w] * (4 * n)),
        out_shape=[jax.ShapeDtypeStruct((1, r, c), F32)] * (4 * n),
        compiler_params=_params(("arbitrary",) * 2),
    )(pos, *ws, *ms, *vs, g_own, g_recv)


def _adam_small(items):
    n = len(items)

    def body(*refs):
        ins, outs = refs[:4 * n], refs[4 * n:]
        for a in range(n):
            w_ref, g_ref, m_ref, v_ref = ins[4 * a:4 * a + 4]
            d, m, v = _adamw(w_ref[...], g_ref[...], m_ref[...], v_ref[...])
            outs[3 * a][...] = d
            outs[3 * a + 1][...] = m
            outs[3 * a + 2][...] = v

    vm = pl.BlockSpec(memory_space=pltpu.VMEM)
    flat = [t for it in items for t in it]
    outs = pl.pallas_call(
        body, name="adam_small", in_specs=[vm] * (4 * n), out_specs=[vm] * (3 * n),
        out_shape=[jax.ShapeDtypeStruct(it[0].shape, F32) for it in items for _ in range(3)],
    )(*flat)
    return [tuple(outs[3 * a:3 * a + 3]) for a in range(n)]


def _proj_fwd(s_pad, norm_g, wg_in):
    tp, d = s_pad.shape
    _, nsh, _, sw = wg_in.shape
    tmb = tp // N_ROW_TILES_BIG

    def body(s_ref, g_ref, w_ref, proj_ref):
        s = s_ref[...]
        r = lax.rsqrt(jnp.mean(s * s, axis=-1, keepdims=True) + EPS)
        h = (s * r * g_ref[...]).astype(BF16)
        proj_ref[...] = jnp.dot(h, w_ref[...], preferred_element_type=F32).astype(BF16)

    return pl.pallas_call(
        body, name="f1_proj", grid=(nsh, N_ROW_TILES_BIG),
        in_specs=[pl.BlockSpec((tmb, d), lambda j, i: (i, 0)),
                  pl.BlockSpec((1, d), lambda j, i: (0, 0)),
                  pl.BlockSpec((None, None, d, sw), lambda j, i: (0, j, 0, 0))],
        out_specs=pl.BlockSpec((tmb, sw), lambda j, i: (i, j)),
        out_shape=jax.ShapeDtypeStruct((tp, nsh * sw), BF16),
        compiler_params=_params(("arbitrary", "arbitrary")),
    )(s_pad, norm_g, wg_in)


def _dh_bwd(dproj, wg_in, s_pad, ds2, norm_g):
    tp, d = s_pad.shape
    _, nsh, _, sw = wg_in.shape
    tmb = tp // N_ROW_TILES_BIG

    def body(dp_ref, w_ref, s_ref, ds2_ref, g_ref, ds_ref, dng_ref, acc, gacc):
        i, j = pl.program_id(0), pl.program_id(1)

        @pl.when((i == 0) & (j == 0))
        def _():
            gacc[...] = jnp.zeros_like(gacc)

        part = lax.dot_general(dp_ref[...], w_ref[...], (((1,), (1,)), ((), ())), preferred_element_type=F32)

        @pl.when(j == 0)
        def _():
            acc[...] = part

        @pl.when(j > 0)
        def _():
            acc[...] += part

        @pl.when(j == nsh - 1)
        def _():
            dh = acc[...]
            s = s_ref[...]
            r = lax.rsqrt(jnp.mean(s * s, axis=-1, keepdims=True) + EPS)
            gacc[...] += (dh * s * r).reshape(tmb // 8, 8, d).sum(axis=0)
            t = dh * g_ref[...]
            ds_ref[...] = ds2_ref[...] + r * t - s * (r * r * r) * jnp.mean(t * s, axis=-1, keepdims=True)

        @pl.when((i == N_ROW_TILES_BIG - 1) & (j == nsh - 1))
        def _():
            dng_ref[...] = jnp.broadcast_to(jnp.sum(gacc[...], axis=0, keepdims=True), (8, d))

    return pl.pallas_call(
        body, name="b2_dh", grid=(N_ROW_TILES_BIG, nsh),
        in_specs=[pl.BlockSpec((tmb, sw), lambda i, j: (i, j)),
                  pl.BlockSpec((None, None, d, sw), lambda i, j: (0, j, 0, 0)),
                  pl.BlockSpec((tmb, d), lambda i, j: (i, 0)),
                  pl.BlockSpec((tmb, d), lambda i, j: (i, 0)),
                  pl.BlockSpec((1, d), lambda i, j: (0, 0))],
        out_specs=[pl.BlockSpec((tmb, d), lambda i, j: (i, 0)),
                   pl.BlockSpec((8, d), lambda i, j: (0, 0))],
        out_shape=[jax.ShapeDtypeStruct((tp, d), F32), jax.ShapeDtypeStruct((8, d), F32)],
        scratch_shapes=[pltpu.VMEM((tmb, d), F32), pltpu.VMEM((8, d), F32)],
        compiler_params=_params(("arbitrary", "arbitrary")),
    )(dproj, wg_in, s_pad, ds2, norm_g)


def _col_block(width, cap):
    return max(b for b in range(128, cap + 1, 128) if width % b == 0)


def _dw_in(h_t, dproj, nsh):
    d, tp = h_t.shape
    sw = dproj.shape[1] // nsh
    cw = _col_block(sw, 768)
    ncol = sw // cw

    def body(h_ref, dp_ref, out_ref):
        out_ref[...] = jnp.dot(h_ref[...], dp_ref[...], preferred_element_type=F32)

    return pl.pallas_call(
        body, name="dw_in", grid=(nsh, ncol),
        in_specs=[pl.BlockSpec((d, tp), lambda j, n: (0, 0)),
                  pl.BlockSpec((tp, cw), lambda j, n: (0, j * ncol + n))],
        out_specs=pl.BlockSpec((None, None, d, cw), lambda j, n: (0, j, 0, n)),
        out_shape=jax.ShapeDtypeStruct((1, nsh, d, sw), F32),
        compiler_params=_params(("arbitrary",) * 2),
    )(h_t, dproj)


def _dw_square(lhs3_t, rhs3):
    n, d, tp = lhs3_t.shape
    cw = _col_block(d, 512)

    def body(a_ref, b_ref, out_ref):
        out_ref[...] = jnp.dot(a_ref[...], b_ref[...], preferred_element_type=F32)

    return pl.pallas_call(
        body, name="dw_square", grid=(n, d // cw),
        in_specs=[pl.BlockSpec((None, d, tp), lambda a, c: (a, 0, 0)),
                  pl.BlockSpec((None, tp, cw), lambda a, c: (a, 0, c))],
        out_specs=pl.BlockSpec((None, d, cw), lambda a, c: (a, 0, c)),
        out_shape=jax.ShapeDtypeStruct((n, d, d), F32),
        compiler_params=_params(("arbitrary",) * 2),
    )(lhs3_t, rhs3)


def _conv_a_taps(first_lag, last_lag):
    out = []
    for r in range(8):
        taps = [(q, 8 * q + r) for q in range(5) if first_lag <= 8 * q + r <= last_lag]
        if taps:
            out.append((r, taps))
    return out


def _mix_fwd(s_pad, proj, target, w3, wa, wb, conv_a_b, ln_g, ln_b, b_a_out, final_g, norm_g):
    tp, d = s_pad.shape
    nt = tp // TM
    nrb = TM // RB
    shl = TM + SHIFT_ROWS

    def body(s_ref, proj_ref, tgt_ref, w3_ref, wa_ref, wb_ref, cab_ref, lng_ref, lnb_ref, bao_ref, fg_ref, ng_ref,
             ca_ref, cb_ref, ya_ref, yb_ref, abmt_ref, ds2_ref, ht_ref, loss_ref, dfg_ref,
             abm_ref, ext_a, ext_b, sh, s2_s, lacc, gacc):
        i = pl.program_id(0)

        def split(k, rows):
            return proj_ref[rows, k * d:(k + 1) * d].astype(F32)

        s_in = s_ref[...]
        h = s_in * lax.rsqrt(jnp.mean(s_in * s_in, axis=-1, keepdims=True) + EPS) * ng_ref[...]
        ht_ref[...] = h.T.astype(BF16)

        @pl.when(i == 0)
        def _():
            ext_a[0:HALO_A, :] = jnp.zeros((HALO_A, d), F32)
            ext_b[0:HALO_B, :] = jnp.zeros((HALO_B, d), F32)
            lacc[...] = jnp.zeros_like(lacc)
            gacc[...] = jnp.zeros_like(gacc)

        def conv_in(rb, carry):
            rows = _rows(rb)
            ua0 = split(0, rows) * _sigmoid(split(1, rows))
            ext_a[pl.ds(pl.multiple_of(HALO_A + rb * RB, 8), RB), :] = ua0
            ext_b[pl.ds(pl.multiple_of(HALO_B + rb * RB, 8), RB), :] = split(4, rows) * split(5, rows)
            ca_ref[rows, :] = jnp.broadcast_to(cab_ref[...], (RB, d))
            return carry
        lax.fori_loop(0, nrb, conv_in, 0)

        for r, taps in _conv_a_taps(HALO_A - CONV_A + 1, HALO_A):
            if r == 0:
                src = ext_a
            else:
                sh[...] = ext_a[r:r + shl, :]
                src = sh

            def conv_acc(rb, carry, src=src, taps=taps):
                rows = _rows(rb)
                acc = ca_ref[rows, :]
                for q, lag in taps:
                    k = lag - (HALO_A - CONV_A + 1)
                    acc = acc + src[pl.ds(pl.multiple_of(rb * RB + 8 * q, 8), RB), :] * wa_ref[k:k + 1, :]
                ca_ref[rows, :] = acc
                return carry
            lax.fori_loop(0, nrb, conv_acc, 0)
        ext_a[0:HALO_A, :] = ext_a[TM:TM + HALO_A, :]

        cb_ref[...] = ext_b[HALO_B:HALO_B + TM, :] * wb_ref[2:3, :]
        for k in range(CONV_B - 1):
            off = HALO_B - CONV_B + 1 + k
            sh[0:TM, :] = ext_b[off:off + TM, :]
            cb_ref[...] += sh[0:TM, :] * wb_ref[k:k + 1, :]
        ext_b[0:HALO_B, :] = ext_b[TM:TM + HALO_B, :]

        def branches(rb, carry):
            rows = _rows(rb)
            ca = ca_ref[rows, :]
            mu = jnp.mean(ca, axis=-1, keepdims=True)
            xc = ca - mu
            rstd = lax.rsqrt(jnp.mean(xc * xc, axis=-1, keepdims=True) + EPS)
            ln = xc * rstd * lng_ref[...] + lnb_ref[...]
            ua = ln * _sigmoid(ln)
            a_z = split(2, rows)
            abm_ref[0, rows, :] = (ua * (a_z * _sigmoid(a_z))).astype(BF16)
            b_z = split(6, rows)
            ub = split(3, rows) * cb_ref[rows, :]
            abm_ref[1, rows, :] = (ub * (b_z * _sigmoid(b_z))).astype(BF16)
            return carry
        lax.fori_loop(0, nrb, branches, 0)

        ya_ref[...] = jnp.dot(abm_ref[0], w3_ref[0], preferred_element_type=F32) + bao_ref[...]
        yb_ref[...] = jnp.dot(abm_ref[1], w3_ref[1], preferred_element_type=F32)

        def merge(rb, carry):
            rows = _rows(rb)
            m = _sigmoid(split(7, rows)) * ya_ref[rows, :] + _sigmoid(split(8, rows)) * yb_ref[rows, :]
            abm_ref[2, rows, :] = m.astype(BF16)
            return carry
        lax.fori_loop(0, nrb, merge, 0)

        s2_s[...] = s_ref[...] + jnp.dot(abm_ref[2], w3_ref[2], preferred_element_type=F32)
        for k in range(3):
            abmt_ref[k] = abm_ref[k].astype(F32).T.astype(BF16)
        live = (i > 0).astype(F32)

        def head(rb, carry):
            rows = _rows(rb)
            s2 = s2_s[rows, :]
            r2 = lax.rsqrt(jnp.mean(s2 * s2, axis=-1, keepdims=True) + EPS)
            diff = (s2 * r2 * fg_ref[...] - tgt_ref[rows, :]) * live
            lacc[...] += diff * diff
            dy = diff * (1.0 / d)
            gacc[...] += (dy * s2 * r2).reshape(RB // 8, 8, d).sum(axis=0)
            t = dy * fg_ref[...]
            ds2_ref[rows, :] = r2 * t - s2 * (r2 * r2 * r2) * jnp.mean(t * s2, axis=-1, keepdims=True)
            return carry
        lax.fori_loop(0, nrb, head, 0)

        @pl.when(i == nt - 1)
        def _():
            loss_ref[...] = jnp.broadcast_to(0.5 * jnp.sum(lacc[...]) * (1.0 / d), (8, 128))
            dfg_ref[...] = jnp.broadcast_to(jnp.sum(gacc[...], axis=0, keepdims=True), (8, d))

    row_f32 = pl.BlockSpec((TM, d), lambda i: (i, 0))
    const = lambda shape: pl.BlockSpec(shape, lambda i: (0,) * len(shape))
    return pl.pallas_call(
        body, name="f2_mix", grid=(nt,),
        in_specs=[row_f32,
                  pl.BlockSpec((TM, N_SPLIT * d), lambda i: (i, 0)),
                  pl.BlockSpec((TM, d), lambda i: (jnp.maximum(i - 1, 0), 0)),
                  const((3, d, d)), const(wa.shape), const(wb.shape)] + [const((1, d))] * 6,
        out_specs=[row_f32, row_f32, row_f32, row_f32,
                   pl.BlockSpec((3, d, TM), lambda i: (0, 0, i)),
                   row_f32, pl.BlockSpec((d, TM), lambda i: (0, i)), const((8, 128)), const((8, d))],
        out_shape=[jax.ShapeDtypeStruct((tp, d), F32)] * 4
        + [jax.ShapeDtypeStruct((3, d, tp), BF16), jax.ShapeDtypeStruct((tp, d), F32),
           jax.ShapeDtypeStruct((d, tp), BF16),
           jax.ShapeDtypeStruct((8, 128), F32), jax.ShapeDtypeStruct((8, d), F32)],
        scratch_shapes=[pltpu.VMEM((3, TM, d), BF16),
                        pltpu.VMEM((HALO_A + TM, d), F32), pltpu.VMEM((HALO_B + TM, d), F32),
                        pltpu.VMEM((shl, d), F32), pltpu.VMEM((TM, d), F32),
                        pltpu.VMEM((RB, d), F32), pltpu.VMEM((8, d), F32)],
        compiler_params=_params(("arbitrary",)),
    )(s_pad, proj, target, w3, wa, wb, conv_a_b, ln_g, ln_b, b_a_out, final_g, norm_g)


def _mix_bwd(ds2, proj, ca, cb, ya, yb, w3, wa, wb, ln_g, ln_b):
    tp, d = ds2.shape
    nt = tp // TM
    nrb = TM // RB
    shl = TM + SHIFT_ROWS
    nt_dims = (((1,), (1,)), ((), ()))

    def body(ds2_ref, proj_ref, ca_ref, cb_ref, ya_ref, yb_ref, w3_ref, wa_ref, wb_ref, lng_ref, lnb_ref,
             dproj_ref, d3_ref, sm_ref, ext_d, ext_e, sh, dm_s, dpa_s, dpb_s, dua0_s, acc):
        step = pl.program_id(0)

        def split(k, rows):
            return proj_ref[rows, k * d:(k + 1) * d].astype(F32)

        def put(k, rows, val):
            dproj_ref[rows, k * d:(k + 1) * d] = val.astype(BF16)

        def accum(row, val):
            acc[row] += val.reshape(RB // 8, 8, d).sum(axis=0)

        @pl.when(step == 0)
        def _():
            ext_d[TM:TM + HALO_A, :] = jnp.zeros((HALO_A, d), F32)
            ext_e[TM:TM + HALO_B, :] = jnp.zeros((HALO_B, d), F32)
            acc[...] = jnp.zeros_like(acc)

        d3_ref[2] = ds2_ref[...].astype(BF16)
        dm_s[...] = lax.dot_general(d3_ref[2], w3_ref[2], nt_dims, preferred_element_type=F32)

        def gates(rb, carry):
            rows = _rows(rb)
            dm = dm_s[rows, :]
            sa = _sigmoid(split(7, rows))
            sb = _sigmoid(split(8, rows))
            ya_v = ya_ref[rows, :]
            yb_v = yb_ref[rows, :]
            put(7, rows, dm * ya_v * sa * (1.0 - sa))
            put(8, rows, dm * yb_v * sb * (1.0 - sb))
            dya = dm * sa
            accum(ROW_DBAO, dya)
            d3_ref[0, rows, :] = dya.astype(BF16)
            d3_ref[1, rows, :] = (dm * sb).astype(BF16)
            return carry
        lax.fori_loop(0, nrb, gates, 0)

        dpa_s[...] = lax.dot_general(d3_ref[0], w3_ref[0], nt_dims, preferred_element_type=F32)
        dpb_s[...] = lax.dot_general(d3_ref[1], w3_ref[1], nt_dims, preferred_element_type=F32)

        def branches(rb, carry):
            rows = _rows(rb)
            ca_v = ca_ref[rows, :]
            mu = jnp.mean(ca_v, axis=-1, keepdims=True)
            xc = ca_v - mu
            rstd = lax.rsqrt(jnp.mean(xc * xc, axis=-1, keepdims=True) + EPS)
            xhat = xc * rstd
            ln = xhat * lng_ref[...] + lnb_ref[...]
            sl = _sigmoid(ln)
            ua = ln * sl
            a_z = split(2, rows)
            sz = _sigmoid(a_z)
            dpa = dpa_s[rows, :]
            put(2, rows, dpa * ua * (sz * (1.0 + a_z * (1.0 - sz))))
            dln = dpa * (a_z * sz) * (sl * (1.0 + ln * (1.0 - sl)))
            accum(ROW_DLNG, dln * xhat)
            accum(ROW_DLNB, dln)
            dxh = dln * lng_ref[...]
            dca = rstd * (dxh - jnp.mean(dxh, axis=-1, keepdims=True)
                          - xhat * jnp.mean(dxh * xhat, axis=-1, keepdims=True))
            accum(ROW_DCAB, dca)
            ext_d[rows, :] = dca
            dua0_s[rows, :] = jnp.zeros((RB, d), F32)
            b_z = split(6, rows)
            szb = _sigmoid(b_z)
            dpb = dpb_s[rows, :]
            b_b = split(3, rows)
            cb_v = cb_ref[rows, :]
            put(6, rows, dpb * (b_b * cb_v) * (szb * (1.0 + b_z * (1.0 - szb))))
            dub = dpb * (b_z * szb)
            put(3, rows, dub * cb_v)
            ext_e[rows, :] = dub * b_b
            return carry
        lax.fori_loop(0, nrb, branches, 0)

        for r, taps in _conv_a_taps(0, CONV_A - 1):
            if r == 0:
                src = ext_d
            else:
                sh[...] = ext_d[r:r + shl, :]
                src = sh

            def conv_t(rb, carry, src=src, taps=taps):
                rows = _rows(rb)
                ua0 = split(0, rows) * _sigmoid(split(1, rows))
                dua0 = dua0_s[rows, :]
                for q, lag in taps:
                    k = CONV_A - 1 - lag
                    slab = src[pl.ds(pl.multiple_of(rb * RB + 8 * q, 8), RB), :]
                    dua0 = dua0 + slab * wa_ref[k:k + 1, :]
                    accum(ROW_DWA + k, slab * ua0)
                dua0_s[rows, :] = dua0
                return carry
            lax.fori_loop(0, nrb, conv_t, 0)
        ext_d[TM:TM + HALO_A, :] = ext_d[0:HALO_A, :]

        dpb_s[...] = ext_e[0:TM, :] * wb_ref[CONV_B - 1:CONV_B, :]
        for lag in range(CONV_B):
            k = CONV_B - 1 - lag
            if lag > 0:
                sh[0:TM, :] = ext_e[lag:lag + TM, :]
                dpb_s[...] += sh[0:TM, :] * wb_ref[k:k + 1, :]
            src = ext_e if lag == 0 else sh

            def conv_b_w(rb, carry, src=src, k=k):
                rows = _rows(rb)
                accum(ROW_DWB + k, src[rows, :] * (split(4, rows) * split(5, rows)))
                return carry
            lax.fori_loop(0, nrb, conv_b_w, 0)
        ext_e[TM:TM + HALO_B, :] = ext_e[0:HALO_B, :]

        def inputs(rb, carry):
            rows = _rows(rb)
            dua0 = dua0_s[rows, :]
            a_val = split(0, rows)
            sg = _sigmoid(split(1, rows))
            put(0, rows, dua0 * sg)
            put(1, rows, dua0 * a_val * sg * (1.0 - sg))
            dcbin = dpb_s[rows, :]
            put(4, rows, dcbin * split(5, rows))
            put(5, rows, dcbin * split(4, rows))
            return carry
        lax.fori_loop(0, nrb, inputs, 0)

        @pl.when(step == nt - 1)
        def _():
            for row in range(SM_ROWS):
                sm_ref[row:row + 1, :] = jnp.sum(acc[row], axis=0, keepdims=True)

    rev = lambda i: (nt - 1 - i, 0)
    row_f32 = pl.BlockSpec((TM, d), rev)
    const = lambda shape: pl.BlockSpec(shape, lambda i: (0,) * len(shape))
    return pl.pallas_call(
        body, name="b1_mix", grid=(nt,),
        in_specs=[row_f32, pl.BlockSpec((TM, N_SPLIT * d), rev), row_f32, row_f32, row_f32, row_f32,
                  const((3, d, d)), const(wa.shape), const(wb.shape), const((1, d)), const((1, d))],
        out_specs=[pl.BlockSpec((TM, N_SPLIT * d), rev),
                   pl.BlockSpec((3, TM, d), lambda i: (0, nt - 1 - i, 0)),
                   const((SM_ROWS, d))],
        out_shape=[jax.ShapeDtypeStruct((tp, N_SPLIT * d), BF16), jax.ShapeDtypeStruct((3, tp, d), BF16),
                   jax.ShapeDtypeStruct((SM_ROWS, d), F32)],
        scratch_shapes=[pltpu.VMEM((TM + HALO_A, d), F32), pltpu.VMEM((TM + HALO_B, d), F32),
                        pltpu.VMEM((shl, d), F32), pltpu.VMEM((TM, d), F32), pltpu.VMEM((TM, d), F32),
                        pltpu.VMEM((TM, d), F32), pltpu.VMEM((TM, d), F32),
                        pltpu.VMEM((SM_ROWS, 8, d), F32)],
        compiler_params=_params(("arbitrary",)),
    )(ds2, proj, ca, cb, ya, yb, w3, wa, wb, ln_g, ln_b)


def kernel(x, meta_tokens, norm_g, w_in, conv_a_w, conv_a_b, ln_a_g, ln_a_b, w_a_out, b_a_out, conv_b_w, w_b_out, w_out, final_g, loss_target, m_meta_tokens, m_norm_g, m_w_in, m_conv_a_w, m_conv_a_b, m_ln_a_g, m_ln_a_b, m_w_a_out, m_b_a_out, m_conv_b_w, m_w_b_out, m_w_out, m_final_g, v_meta_tokens, v_norm_g, v_w_in, v_conv_a_w, v_conv_a_b, v_ln_a_g, v_ln_a_b, v_w_a_out, v_b_a_out, v_conv_b_w, v_w_b_out, v_w_out, v_final_g):
    seq, d = x.shape[1], x.shape[2]
    dc = meta_tokens.shape[1]
    sw = w_in.shape[2]
    rsh = w_a_out.shape[1]
    xi, yi, ci = _mesh_pos()
    me = 2 * xi + yi
    pos = jnp.stack([ci, me]).astype(jnp.int32)

    smalls = jnp.concatenate([
        jnp.pad(conv_a_w[0], ((0, HALO_A - CONV_A), (0, 0))),
        jnp.pad(conv_b_w[0], ((0, HALO_B - CONV_B), (0, 0))),
        meta_tokens, jnp.zeros((8, dc), F32)], axis=0)[None]
    w3_own = jnp.stack([w_a_out[0], w_b_out[0], w_out[0]])
    wg_in, wg3, smg = _all_gather([_place_own(w_in, pos, BF16, "place_in"),
                                   _place_own(w3_own, pos, BF16, "place_sq"),
                                   _place_own(smalls, pos, F32, "place_small")])
    w3 = wg3.reshape(3, N_CHIPS * rsh, d)
    smg = jnp.transpose(smg[0], (1, 0, 2)).reshape(smalls.shape[1], N_CHIPS * dc)
    wa_full = smg[0:HALO_A]
    wb_full = smg[HALO_A:HALO_A + HALO_B]
    meta_full = smg[HALO_A + HALO_B:HALO_A + HALO_B + N_META]
    fg2 = final_g.reshape(1, d)

    first_tile = jnp.concatenate([jnp.zeros((TM - N_META, d), F32), meta_full], axis=0)
    s_pad = jnp.concatenate([first_tile, x[0]], axis=0)

    proj = _proj_fwd(s_pad, norm_g, wg_in)
    ca, cb, ya, yb, abm_t, ds2, h_t, loss8, dfg8 = _mix_fwd(
        s_pad, proj, loss_target[0], w3, wa_full, wb_full, conv_a_b, ln_a_g, ln_a_b, b_a_out, fg2, norm_g)
    dproj, d3, sm = _mix_bwd(ds2, proj, ca, cb, ya, yb, w3, wa_full, wb_full, ln_a_g, ln_a_b)
    ds, dng8 = _dh_bwd(dproj, wg_in, s_pad, ds2, norm_g)
    g_in = _dw_in(h_t, dproj, N_CHIPS)
    g_sq = _dw_square(abm_t, d3).reshape(3, N_CHIPS, rsh, d)

    r_in, r_sq = _sibling_exchange([g_in, g_sq])
    p32_in, pbf_in = _add_sibling(g_in, r_in, pos, "rs_add_in")
    p32_sq, pbf_sq = _add_sibling(g_sq, r_sq, pos, "rs_add_sq")
    l_in, l_sq = _chip_exchange([pbf_in, pbf_sq])
    half_in = _sum_chips(p32_in, l_in, pos, "rs_sum_in")
    half_sq = _sum_chips(p32_sq, l_sq, pos, "rs_sum_sq")
    other_in, other_sq = _sibling_swap([half_in, half_sq])

    tail_row = lax.broadcasted_iota(jnp.int32, (8, d), 0)
    tail = jnp.where(tail_row == 0, dng8, jnp.where(tail_row == 1, dfg8,
                     jnp.where(tail_row == 2, loss8[0, 0], 0.0)))
    block = jnp.concatenate([sm, ds[TM - N_META:TM], tail], axis=0)
    red = _all_reduce_small(block)
    col = lax.dynamic_slice(red, (0, me * dc), (AR_ROWS, dc))
    g_small = {
        "meta_tokens": col[ROW_DMETA:ROW_DMETA + N_META],
        "norm_g": red[ROW_DNG:ROW_DNG + 1],
        "conv_a_w": col[ROW_DWA:ROW_DWA + CONV_A][None],
        "conv_a_b": red[ROW_DCAB:ROW_DCAB + 1],
        "ln_a_g": red[ROW_DLNG:ROW_DLNG + 1],
        "ln_a_b": red[ROW_DLNB:ROW_DLNB + 1],
        "b_a_out": red[ROW_DBAO:ROW_DBAO + 1],
        "conv_b_w": col[ROW_DWB:ROW_DWB + CONV_B][None],
        "final_g": red[ROW_DFG],
    }

    upd_in = _adam_halves([w_in], [m_w_in], [v_w_in], half_in, other_in, pos, "adam_in")
    upd_sq = _adam_halves([w_a_out, w_b_out, w_out], [m_w_a_out, m_w_b_out, m_w_out],
                          [v_w_a_out, v_w_b_out, v_w_out], half_sq, other_sq, pos, "adam_sq")
    small_w = {"meta_tokens": (meta_tokens, m_meta_tokens, v_meta_tokens), "norm_g": (norm_g, m_norm_g, v_norm_g),
               "conv_a_w": (conv_a_w, m_conv_a_w, v_conv_a_w), "conv_a_b": (conv_a_b, m_conv_a_b, v_conv_a_b),
               "ln_a_g": (ln_a_g, m_ln_a_g, v_ln_a_g), "ln_a_b": (ln_a_b, m_ln_a_b, v_ln_a_b),
               "b_a_out": (b_a_out, m_b_a_out, v_b_a_out), "conv_b_w": (conv_b_w, m_conv_b_w, v_conv_b_w),
               "final_g": (final_g, m_final_g, v_final_g)}
    names_small = list(small_w)
    as2d = lambda t: t.reshape(-1, t.shape[-1])
    upd_small = _adam_small([(as2d(small_w[k][0]), as2d(g_small[k]), as2d(small_w[k][1]), as2d(small_w[k][2]))
                             for k in names_small])

    grads, deltas, new_m, new_v = dict(g_small), {}, {}, {}
    for k, upd in zip(names_small, upd_small):
        deltas[k], new_m[k], new_v[k] = [t.reshape(small_w[k][0].shape) for t in upd]
    grads["w_in"], deltas["w_in"], new_m["w_in"], new_v["w_in"] = upd_in
    for idx, k in enumerate(["w_a_out", "w_b_out", "w_out"]):
        grads[k], deltas[k], new_m[k], new_v[k] = upd_sq[4 * idx:4 * idx + 4]

    loss = red[ROW_LOSS, 0]
    grad_x = ds[TM:][None]
    order = ["meta_tokens", "norm_g", "w_in", "conv_a_w", "conv_a_b", "ln_a_g", "ln_a_b", "w_a_out", "b_a_out",
             "conv_b_w", "w_b_out", "w_out", "final_g"]
    return (loss, grad_x, *[grads[k] for k in order], *[deltas[k] for k in order],
            *[new_m[k] for k in order], *[new_v[k] for k in order])
```

```python
import functools

import jax
import jax.numpy as jnp
from jax import lax
from jax.experimental import pallas as pl
from jax.experimental.pallas import tpu as pltpu

F32 = jnp.float32
BF16 = jnp.bfloat16
MESH = pl.DeviceIdType.MESH

EPS = 1e-6
N_META = 16
N_SPLIT = 9
CONV_A = 31
CONV_B = 3
HALO_A = 32
HALO_B = 8
SHIFT_ROWS = 24
TM = 256
RB = 64
N_ROW_TILES_BIG = 8
N_CHIPS = 4
VMEM_LIMIT = 56 * 1024 * 1024

ADAM_LR = 0.001
ADAM_B1 = 0.9
ADAM_B2 = 0.999
ADAM_EPS = 1e-08
ADAM_WD = 0.01
ADAM_STEP = 10

ROW_DWA = 0
ROW_DWB = 32
ROW_DCAB = 40
ROW_DLNG = 41
ROW_DLNB = 42
ROW_DBAO = 43
SM_ROWS = 48
ROW_DMETA = 48
ROW_DNG = 64
ROW_DFG = 65
ROW_LOSS = 66
AR_ROWS = 72


def _sigmoid(v):
    return 0.5 * jnp.tanh(0.5 * v) + 0.5


def _params(sem, **kw):
    return pltpu.CompilerParams(dimension_semantics=sem, vmem_limit_bytes=VMEM_LIMIT, **kw)


def _rows(rb):
    return pl.ds(pl.multiple_of(rb * RB, RB), RB)


def _mesh_pos():
    x, y, c = lax.axis_index("x"), lax.axis_index("y"), lax.axis_index("c")
    return x, y, c


def _half(ref, j, c):
    h = ref.shape[2] // 2
    return ref.at[:, j, pl.ds(c * h, h), :]


def _place_own(shard, pos, dtype, name):
    s, r, c = shard.shape
    rb = 128 if r % 128 == 0 else r

    def body(pos_ref, x_ref, o_ref):
        o_ref[...] = x_ref[...].astype(dtype)

    return pl.pallas_call(
        body, name=name,
        grid_spec=pltpu.PrefetchScalarGridSpec(
            num_scalar_prefetch=1, grid=(s, r // rb),
            in_specs=[pl.BlockSpec((None, rb, c), lambda si, b, pos_ref: (si, b, 0))],
            out_specs=pl.BlockSpec((None, None, rb, c), lambda si, b, pos_ref: (si, pos_ref[1], b, 0))),
        out_shape=jax.ShapeDtypeStruct((s, N_CHIPS, r, c), dtype),
        compiler_params=_params(("arbitrary",) * 2),
    )(pos, shard)


def _all_gather(bufs):
    n = len(bufs)

    def body(*refs):
        outs = refs[n:2 * n]
        send_sems, recv_sems = refs[2 * n:]
        x, y, c = _mesh_pos()
        me = 2 * x + y
        sibling = (x, y, 1 - c)
        chips = [(1 - x, y), (x, 1 - y), (1 - x, 1 - y)]

        def remote(a, k, piece_src, piece_dst, to):
            return pltpu.make_async_remote_copy(
                src_ref=piece_src, dst_ref=piece_dst, send_sem=send_sems.at[6 * a + k],
                recv_sem=recv_sems.at[6 * a + k], device_id=to, device_id_type=MESH)

        sends = []
        for a in range(n):
            mine = _half(outs[a], me, c)
            for k, (px, py) in enumerate(chips):
                sends.append(remote(a, k, mine, mine, (px, py, c)))
        for cp in sends:
            cp.start()
        for a in range(n):
            for k, (px, py) in enumerate(chips):
                piece = _half(outs[a], 2 * px + py, c)
                remote(a, k, piece, piece, (px, py, c)).wait_recv()
                fwd = remote(a, 3 + k, piece, piece, sibling)
                fwd.start()
                sends.append(fwd)
        for a in range(n):
            for k, (px, py) in enumerate(chips):
                piece = _half(outs[a], 2 * px + py, 1 - c)
                remote(a, 3 + k, piece, piece, sibling).wait_recv()
        for cp in sends:
            cp.wait_send()

    any_spec = pl.BlockSpec(memory_space=pl.ANY)
    return pl.pallas_call(
        body, name="ag_weights",
        in_specs=[any_spec] * n, out_specs=[any_spec] * n,
        out_shape=[jax.ShapeDtypeStruct(b.shape, b.dtype) for b in bufs],
        input_output_aliases={a: a for a in range(n)},
        scratch_shapes=[pltpu.SemaphoreType.DMA((6 * n,)), pltpu.SemaphoreType.DMA((6 * n,))],
    )(*bufs)


def _sibling_exchange(grads):
    n = len(grads)

    def body(*refs):
        ins, outs = refs[:n], refs[n:2 * n]
        send_sems, recv_sems = refs[2 * n:]
        x, y, c = _mesh_pos()
        copies = []
        for a in range(n):
            h = ins[a].shape[2] // 2
            copies.append(pltpu.make_async_remote_copy(
                src_ref=ins[a].at[:, :, pl.ds((1 - c) * h, h), :], dst_ref=outs[a],
                send_sem=send_sems.at[a], recv_sem=recv_sems.at[a],
                device_id=(x, y, 1 - c), device_id_type=MESH))
        for cp in copies:
            cp.start()
        for cp in copies:
            cp.wait()

    any_spec = pl.BlockSpec(memory_space=pl.ANY)
    return pl.pallas_call(
        body, name="rs_sibling",
        in_specs=[any_spec] * n, out_specs=[any_spec] * n,
        out_shape=[jax.ShapeDtypeStruct(g.shape[:2] + (g.shape[2] // 2, g.shape[3]), g.dtype) for g in grads],
        scratch_shapes=[pltpu.SemaphoreType.DMA((n,)), pltpu.SemaphoreType.DMA((n,))],
    )(*grads)


def _chip_exchange(parts):
    n = len(parts)

    def body(*refs):
        ins, outs = refs[:n], refs[n:2 * n]
        send_sems, recv_sems = refs[2 * n:]
        x, y, c = _mesh_pos()
        me = 2 * x + y
        chips = [(1 - x, y), (x, 1 - y), (1 - x, 1 - y)]
        sends = []
        for a in range(n):
            for k, (px, py) in enumerate(chips):
                sends.append(pltpu.make_async_remote_copy(
                    src_ref=ins[a].at[:, 2 * px + py], dst_ref=outs[a].at[:, me],
                    send_sem=send_sems.at[3 * a + k], recv_sem=recv_sems.at[3 * a + k],
                    device_id=(px, py, c), device_id_type=MESH))
        for cp in sends:
            cp.start()
        for a in range(n):
            for k, (px, py) in enumerate(chips):
                landed = outs[a].at[:, 2 * px + py]
                pltpu.make_async_remote_copy(
                    src_ref=landed, dst_ref=landed, send_sem=send_sems.at[3 * a + k],
                    recv_sem=recv_sems.at[3 * a + k], device_id=(px, py, c), device_id_type=MESH).wait_recv()
        for cp in sends:
            cp.wait_send()

    any_spec = pl.BlockSpec(memory_space=pl.ANY)
    return pl.pallas_call(
        body, name="rs_chips",
        in_specs=[any_spec] * n, out_specs=[any_spec] * n,
        out_shape=[jax.ShapeDtypeStruct(p.shape, p.dtype) for p in parts],
        scratch_shapes=[pltpu.SemaphoreType.DMA((3 * n,)), pltpu.SemaphoreType.DMA((3 * n,))],
    )(*parts)


def _sibling_swap(halves):
    n = len(halves)

    def body(*refs):
        ins, outs = refs[:n], refs[n:2 * n]
        send_sems, recv_sems = refs[2 * n:]
        x, y, c = _mesh_pos()
        copies = [pltpu.make_async_remote_copy(
            src_ref=ins[a], dst_ref=outs[a], send_sem=send_sems.at[a], recv_sem=recv_sems.at[a],
            device_id=(x, y, 1 - c), device_id_type=MESH) for a in range(n)]
        for cp in copies:
            cp.start()
        for cp in copies:
            cp.wait()

    any_spec = pl.BlockSpec(memory_space=pl.ANY)
    return pl.pallas_call(
        body, name="rs_swap",
        in_specs=[any_spec] * n, out_specs=[any_spec] * n,
        out_shape=[jax.ShapeDtypeStruct(h.shape, h.dtype) for h in halves],
        scratch_shapes=[pltpu.SemaphoreType.DMA((n,)), pltpu.SemaphoreType.DMA((n,))],
    )(*halves)


def _all_reduce_small(block):
    rows, d = block.shape

    def body(x_ref, out_ref, sib_ref, part_ref, peers_ref, send_sems, recv_sems):
        x, y, c = _mesh_pos()
        me = 2 * x + y
        chips = [(1 - x, y), (x, 1 - y), (1 - x, 1 - y)]
        swap = pltpu.make_async_remote_copy(
            src_ref=x_ref, dst_ref=sib_ref, send_sem=send_sems.at[0], recv_sem=recv_sems.at[0],
            device_id=(x, y, 1 - c), device_id_type=MESH)
        swap.start()
        swap.wait()
        part_ref[...] = x_ref[...] + sib_ref[...]
        peers_ref[me] = part_ref[...]
        sends = [pltpu.make_async_remote_copy(
            src_ref=part_ref, dst_ref=peers_ref.at[me], send_sem=send_sems.at[1 + k], recv_sem=recv_sems.at[1 + k],
            device_id=(px, py, c), device_id_type=MESH) for k, (px, py) in enumerate(chips)]
        for cp in sends:
            cp.start()
        for k, (px, py) in enumerate(chips):
            landed = peers_ref.at[2 * px + py]
            pltpu.make_async_remote_copy(
                src_ref=landed, dst_ref=landed, send_sem=send_sems.at[1 + k], recv_sem=recv_sems.at[1 + k],
                device_id=(px, py, c), device_id_type=MESH).wait_recv()
        for cp in sends:
            cp.wait_send()
        out_ref[...] = ((peers_ref[0] + peers_ref[1]) + peers_ref[2]) + peers_ref[3]

    vm = pl.BlockSpec(memory_space=pltpu.VMEM)
    return pl.pallas_call(
        body, name="ar_small",
        in_specs=[vm], out_specs=vm,
        out_shape=jax.ShapeDtypeStruct((rows, d), F32),
        scratch_shapes=[pltpu.VMEM((rows, d), F32), pltpu.VMEM((rows, d), F32),
                        pltpu.VMEM((N_CHIPS, rows, d), F32),
                        pltpu.SemaphoreType.DMA((4,)), pltpu.SemaphoreType.DMA((4,))],
    )(block)


def _add_sibling(grad, recv, pos, name):
    s, nch, r, c = grad.shape
    h = r // 2
    hb = min(h, 128)

    def body(pos_ref, g_ref, r_ref, p32_ref, pbf_ref):
        p = g_ref[...] + r_ref[...]
        p32_ref[...] = p
        pbf_ref[...] = p.astype(BF16)

    nb = h // hb
    spec_g = pl.BlockSpec((None, None, hb, c), lambda si, j, b, pos_ref: (si, j, pos_ref[0] * nb + b, 0))
    spec_h = pl.BlockSpec((None, None, hb, c), lambda si, j, b, pos_ref: (si, j, b, 0))
    return pl.pallas_call(
        body, name=name,
        grid_spec=pltpu.PrefetchScalarGridSpec(
            num_scalar_prefetch=1, grid=(s, nch, nb), in_specs=[spec_g, spec_h], out_specs=[spec_h, spec_h]),
        out_shape=[jax.ShapeDtypeStruct((s, nch, h, c), F32), jax.ShapeDtypeStruct((s, nch, h, c), BF16)],
        compiler_params=_params(("arbitrary",) * 3),
    )(pos, grad, recv)


def _sum_chips(p32, landed, pos, name):
    s, nch, h, c = p32.shape
    hb = min(h, 128)

    def body(pos_ref, p_ref, l1_ref, l2_ref, l3_ref, out_ref):
        out_ref[...] = ((p_ref[...] + l1_ref[...].astype(F32)) + l2_ref[...].astype(F32)) + l3_ref[...].astype(F32)

    def slot(k):
        return pl.BlockSpec((None, None, hb, c), lambda si, b, pos_ref: (si, (pos_ref[1] + k) % N_CHIPS, b, 0))

    return pl.pallas_call(
        body, name=name,
        grid_spec=pltpu.PrefetchScalarGridSpec(
            num_scalar_prefetch=1, grid=(s, h // hb),
            in_specs=[slot(0), slot(1), slot(2), slot(3)],
            out_specs=pl.BlockSpec((None, hb, c), lambda si, b, pos_ref: (si, b, 0))),
        out_shape=jax.ShapeDtypeStruct((s, h, c), F32),
        compiler_params=_params(("arbitrary",) * 2),
    )(pos, p32, landed, landed, landed)


def _adamw(w, g, m, v):
    m = ADAM_B1 * m + (1.0 - ADAM_B1) * g
    v = ADAM_B2 * v + (1.0 - ADAM_B2) * (g * g)
    m_hat = m / (1.0 - ADAM_B1 ** ADAM_STEP)
    v_hat = v / (1.0 - ADAM_B2 ** ADAM_STEP)
    delta = -ADAM_LR * (m_hat / (jnp.sqrt(v_hat) + ADAM_EPS) + ADAM_WD * w)
    return delta, m, v


def _adam_halves(ws, ms, vs, g_own, g_recv, pos, name):
    n = len(ws)
    _, r, c = ws[0].shape
    h = r // 2
    rb = min(h, 128)
    nb = h // rb

    def body(pos_ref, *refs):
        w_refs, m_refs, v_refs = refs[:n], refs[n:2 * n], refs[2 * n:3 * n]
        go_ref, gr_ref = refs[3 * n:3 * n + 2]
        outs = refs[3 * n + 2:]
        mine = pl.program_id(0) == pos_ref[0]
        for a in range(n):
            g = jnp.where(mine, go_ref[a], gr_ref[a])
            delta, m, v = _adamw(w_refs[a][...], g, m_refs[a][...], v_refs[a][...])
            outs[4 * a][...], outs[4 * a + 1][...], outs[4 * a + 2][...], outs[4 * a + 3][...] = g, delta, m, v

    spec_w = pl.BlockSpec((None, rb, c), lambda hf, b, pos_ref: (0, hf * nb + b, 0))
    spec_g = pl.BlockSpec((n, rb, c), lambda hf, b, pos_ref: (0, b, 0))
    return pl.pallas_call(
        body, name=name,
        grid_spec=pltpu.PrefetchScalarGridSpec(
            num_scalar_prefetch=1, grid=(2, nb), in_specs=[spec_w] * (3 * n) + [spec_g] * 2,
            out_specs=[spec_w] * (4 * n)),
        out_shape=[jax.ShapeDtypeStruct((1, r, c), F32)] * (4 * n),
        compiler_params=_params(("arbitrary",) * 2),
    )(pos, *ws, *ms, *vs, g_own, g_recv)


def _adam_small(items):
    n = len(items)

    def body(*refs):
        ins, outs = refs[:4 * n], refs[4 * n:]
        for a in range(n):
            w_ref, g_ref, m_ref, v_ref = ins[4 * a:4 * a + 4]
            d, m, v = _adamw(w_ref[...], g_ref[...], m_ref[...], v_ref[...])
            outs[3 * a][...] = d
            outs[3 * a + 1][...] = m
            outs[3 * a + 2][...] = v

    vm = pl.BlockSpec(memory_space=pltpu.VMEM)
    flat = [t for it in items for t in it]
    outs = pl.pallas_call(
        body, name="adam_small", in_specs=[vm] * (4 * n), out_specs=[vm] * (3 * n),
        out_shape=[jax.ShapeDtypeStruct(it[0].shape, F32) for it in items for _ in range(3)],
    )(*flat)
    return [tuple(outs[3 * a:3 * a + 3]) for a in range(n)]


def _proj_fwd(s_pad, norm_g, wg_in):
    tp, d = s_pad.shape
    _, nsh, _, sw = wg_in.shape
    tmb = tp // N_ROW_TILES_BIG

    def body(s_ref, g_ref, w_ref, proj_ref):
        s = s_ref[...]
        r = lax.rsqrt(jnp.mean(s * s, axis=-1, keepdims=True) + EPS)
        h = (s * r * g_ref[...]).astype(BF16)
        proj_ref[...] = jnp.dot(h, w_ref[...], preferred_element_type=F32).astype(BF16)

    return pl.pallas_call(
        body, name="f1_proj", grid=(nsh, N_ROW_TILES_BIG),
        in_specs=[pl.BlockSpec((tmb, d), lambda j, i: (i, 0)),
                  pl.BlockSpec((1, d), lambda j, i: (0, 0)),
                  pl.BlockSpec((None, None, d, sw), lambda j, i: (0, j, 0, 0))],
        out_specs=pl.BlockSpec((tmb, sw), lambda j, i: (i, j)),
        out_shape=jax.ShapeDtypeStruct((tp, nsh * sw), BF16),
        compiler_params=_params(("arbitrary", "arbitrary")),
    )(s_pad, norm_g, wg_in)


def _dh_bwd(dproj, wg_in, s_pad, ds2, norm_g):
    tp, d = s_pad.shape
    _, nsh, _, sw = wg_in.shape
    tmb = tp // N_ROW_TILES_BIG

    def body(dp_ref, w_ref, s_ref, ds2_ref, g_ref, ds_ref, dng_ref, acc, gacc):
        i, j = pl.program_id(0), pl.program_id(1)

        @pl.when((i == 0) & (j == 0))
        def _():
            gacc[...] = jnp.zeros_like(gacc)

        part = lax.dot_general(dp_ref[...], w_ref[...], (((1,), (1,)), ((), ())), preferred_element_type=F32)

        @pl.when(j == 0)
        def _():
            acc[...] = part

        @pl.when(j > 0)
        def _():
            acc[...] += part

        @pl.when(j == nsh - 1)
        def _():
            dh = acc[...]
            s = s_ref[...]
            r = lax.rsqrt(jnp.mean(s * s, axis=-1, keepdims=True) + EPS)
            gacc[...] += (dh * s * r).reshape(tmb // 8, 8, d).sum(axis=0)
            t = dh * g_ref[...]
            ds_ref[...] = ds2_ref[...] + r * t - s * (r * r * r) * jnp.mean(t * s, axis=-1, keepdims=True)

        @pl.when((i == N_ROW_TILES_BIG - 1) & (j == nsh - 1))
        def _():
            dng_ref[...] = jnp.broadcast_to(jnp.sum(gacc[...], axis=0, keepdims=True), (8, d))

    return pl.pallas_call(
        body, name="b2_dh", grid=(N_ROW_TILES_BIG, nsh),
        in_specs=[pl.BlockSpec((tmb, sw), lambda i, j: (i, j)),
                  pl.BlockSpec((None, None, d, sw), lambda i, j: (0, j, 0, 0)),
                  pl.BlockSpec((tmb, d), lambda i, j: (i, 0)),
                  pl.BlockSpec((tmb, d), lambda i, j: (i, 0)),
                  pl.BlockSpec((1, d), lambda i, j: (0, 0))],
        out_specs=[pl.BlockSpec((tmb, d), lambda i, j: (i, 0)),
                   pl.BlockSpec((8, d), lambda i, j: (0, 0))],
        out_shape=[jax.ShapeDtypeStruct((tp, d), F32), jax.ShapeDtypeStruct((8, d), F32)],
        scratch_shapes=[pltpu.VMEM((tmb, d), F32), pltpu.VMEM((8, d), F32)],
        compiler_params=_params(("arbitrary", "arbitrary")),
    )(dproj, wg_in, s_pad, ds2, norm_g)


def _col_block(width, cap):
    return max(b for b in range(128, cap + 1, 128) if width % b == 0)


def _dw_in(h_t, dproj, nsh):
    d, tp = h_t.shape
    sw = dproj.shape[1] // nsh
    cw = _col_block(sw, 768)
    ncol = sw // cw

    def body(h_ref, dp_ref, out_ref):
        out_ref[...] = jnp.dot(h_ref[...], dp_ref[...], preferred_element_type=F32)

    return pl.pallas_call(
        body, name="dw_in", grid=(nsh, ncol),
        in_specs=[pl.BlockSpec((d, tp), lambda j, n: (0, 0)),
                  pl.BlockSpec((tp, cw), lambda j, n: (0, j * ncol + n))],
        out_specs=pl.BlockSpec((None, None, d, cw), lambda j, n: (0, j, 0, n)),
        out_shape=jax.ShapeDtypeStruct((1, nsh, d, sw), F32),
        compiler_params=_params(("arbitrary",) * 2),
    )(h_t, dproj)


def _dw_square(lhs3_t, rhs3):
    n, d, tp = lhs3_t.shape
    cw = _col_block(d, 512)

    def body(a_ref, b_ref, out_ref):
        out_ref[...] = jnp.dot(a_ref[...], b_ref[...], preferred_element_type=F32)

    return pl.pallas_call(
        body, name="dw_square", grid=(n, d // cw),
        in_specs=[pl.BlockSpec((None, d, tp), lambda a, c: (a, 0, 0)),
                  pl.BlockSpec((None, tp, cw), lambda a, c: (a, 0, c))],
        out_specs=pl.BlockSpec((None, d, cw), lambda a, c: (a, 0, c)),
        out_shape=jax.ShapeDtypeStruct((n, d, d), F32),
        compiler_params=_params(("arbitrary",) * 2),
    )(lhs3_t, rhs3)


def _conv_a_taps(first_lag, last_lag):
    out = []
    for r in range(8):
        taps = [(q, 8 * q + r) for q in range(5) if first_lag <= 8 * q + r <= last_lag]
        if taps:
            out.append((r, taps))
    return out


def _mix_fwd(s_pad, proj, target, w3, wa, wb, conv_a_b, ln_g, ln_b, b_a_out, final_g, norm_g):
    tp, d = s_pad.shape
    nt = tp // TM
    nrb = TM // RB
    shl = TM + SHIFT_ROWS

    def body(s_ref, proj_ref, tgt_ref, w3_ref, wa_ref, wb_ref, cab_ref, lng_ref, lnb_ref, bao_ref, fg_ref, ng_ref,
             ca_ref, cb_ref, ya_ref, yb_ref, abmt_ref, ds2_ref, ht_ref, loss_ref, dfg_ref,
             abm_ref, ext_a, ext_b, sh, s2_s, lacc, gacc):
        i = pl.program_id(0)

        def split(k, rows):
            return proj_ref[rows, k * d:(k + 1) * d].astype(F32)

        s_in = s_ref[...]
        h = s_in * lax.rsqrt(jnp.mean(s_in * s_in, axis=-1, keepdims=True) + EPS) * ng_ref[...]
        ht_ref[...] = h.T.astype(BF16)

        @pl.when(i == 0)
        def _():
            ext_a[0:HALO_A, :] = jnp.zeros((HALO_A, d), F32)
            ext_b[0:HALO_B, :] = jnp.zeros((HALO_B, d), F32)
            lacc[...] = jnp.zeros_like(lacc)
            gacc[...] = jnp.zeros_like(gacc)

        def conv_in(rb, carry):
            rows = _rows(rb)
            ua0 = split(0, rows) * _sigmoid(split(1, rows))
            ext_a[pl.ds(pl.multiple_of(HALO_A + rb * RB, 8), RB), :] = ua0
            ext_b[pl.ds(pl.multiple_of(HALO_B + rb * RB, 8), RB), :] = split(4, rows) * split(5, rows)
            ca_ref[rows, :] = jnp.broadcast_to(cab_ref[...], (RB, d))
            return carry
        lax.fori_loop(0, nrb, conv_in, 0)

        for r, taps in _conv_a_taps(HALO_A - CONV_A + 1, HALO_A):
            if r == 0:
                src = ext_a
            else:
                sh[...] = ext_a[r:r + shl, :]
                src = sh

            def conv_acc(rb, carry, src=src, taps=taps):
                rows = _rows(rb)
                acc = ca_ref[rows, :]
                for q, lag in taps:
                    k = lag - (HALO_A - CONV_A + 1)
                    acc = acc + src[pl.ds(pl.multiple_of(rb * RB + 8 * q, 8), RB), :] * wa_ref[k:k + 1, :]
                ca_ref[rows, :] = acc
                return carry
            lax.fori_loop(0, nrb, conv_acc, 0)
        ext_a[0:HALO_A, :] = ext_a[TM:TM + HALO_A, :]

        cb_ref[...] = ext_b[HALO_B:HALO_B + TM, :] * wb_ref[2:3, :]
        for k in range(CONV_B - 1):
            off = HALO_B - CONV_B + 1 + k
            sh[0:TM, :] = ext_b[off:off + TM, :]
            cb_ref[...] += sh[0:TM, :] * wb_ref[k:k + 1, :]
        ext_b[0:HALO_B, :] = ext_b[TM:TM + HALO_B, :]

        def branches(rb, carry):
            rows = _rows(rb)
            ca = ca_ref[rows, :]
            mu = jnp.mean(ca, axis=-1, keepdims=True)
            xc = ca - mu
            rstd = lax.rsqrt(jnp.mean(xc * xc, axis=-1, keepdims=True) + EPS)
            ln = xc * rstd * lng_ref[...] + lnb_ref[...]
            ua = ln * _sigmoid(ln)
            a_z = split(2, rows)
            abm_ref[0, rows, :] = (ua * (a_z * _sigmoid(a_z))).astype(BF16)
            b_z = split(6, rows)
            ub = split(3, rows) * cb_ref[rows, :]
            abm_ref[1, rows, :] = (ub * (b_z * _sigmoid(b_z))).astype(BF16)
            return carry
        lax.fori_loop(0, nrb, branches, 0)

        ya_ref[...] = jnp.dot(abm_ref[0], w3_ref[0], preferred_element_type=F32) + bao_ref[...]
        yb_ref[...] = jnp.dot(abm_ref[1], w3_ref[1], preferred_element_type=F32)

        def merge(rb, carry):
            rows = _rows(rb)
            m = _sigmoid(split(7, rows)) * ya_ref[rows, :] + _sigmoid(split(8, rows)) * yb_ref[rows, :]
            abm_ref[2, rows, :] = m.astype(BF16)
            return carry
        lax.fori_loop(0, nrb, merge, 0)

        s2_s[...] = s_ref[...] + jnp.dot(abm_ref[2], w3_ref[2], preferred_element_type=F32)
        for k in range(3):
            abmt_ref[k] = abm_ref[k].astype(F32).T.astype(BF16)
        live = (i > 0).astype(F32)

        def head(rb, carry):
            rows = _rows(rb)
            s2 = s2_s[rows, :]
            r2 = lax.rsqrt(jnp.mean(s2 * s2, axis=-1, keepdims=True) + EPS)
            diff = (s2 * r2 * fg_ref[...] - tgt_ref[rows, :]) * live
            lacc[...] += diff * diff
            dy = diff * (1.0 / d)
            gacc[...] += (dy * s2 * r2).reshape(RB // 8, 8, d).sum(axis=0)
            t = dy * fg_ref[...]
            ds2_ref[rows, :] = r2 * t - s2 * (r2 * r2 * r2) * jnp.mean(t * s2, axis=-1, keepdims=True)
            return carry
        lax.fori_loop(0, nrb, head, 0)

        @pl.when(i == nt - 1)
        def _():
            loss_ref[...] = jnp.broadcast_to(0.5 * jnp.sum(lacc[...]) * (1.0 / d), (8, 128))
            dfg_ref[...] = jnp.broadcast_to(jnp.sum(gacc[...], axis=0, keepdims=True), (8, d))

    row_f32 = pl.BlockSpec((TM, d), lambda i: (i, 0))
    const = lambda shape: pl.BlockSpec(shape, lambda i: (0,) * len(shape))
    return pl.pallas_call(
        body, name="f2_mix", grid=(nt,),
        in_specs=[row_f32,
                  pl.BlockSpec((TM, N_SPLIT * d), lambda i: (i, 0)),
                  pl.BlockSpec((TM, d), lambda i: (jnp.maximum(i - 1, 0), 0)),
                  const((3, d, d)), const(wa.shape), const(wb.shape)] + [const((1, d))] * 6,
        out_specs=[row_f32, row_f32, row_f32, row_f32,
                   pl.BlockSpec((3, d, TM), lambda i: (0, 0, i)),
                   row_f32, pl.BlockSpec((d, TM), lambda i: (0, i)), const((8, 128)), const((8, d))],
        out_shape=[jax.ShapeDtypeStruct((tp, d), F32)] * 4
        + [jax.ShapeDtypeStruct((3, d, tp), BF16), jax.ShapeDtypeStruct((tp, d), F32),
           jax.ShapeDtypeStruct((d, tp), BF16),
           jax.ShapeDtypeStruct((8, 128), F32), jax.ShapeDtypeStruct((8, d), F32)],
        scratch_shapes=[pltpu.VMEM((3, TM, d), BF16),
                        pltpu.VMEM((HALO_A + TM, d), F32), pltpu.VMEM((HALO_B + TM, d), F32),
                        pltpu.VMEM((shl, d), F32), pltpu.VMEM((TM, d), F32),
                        pltpu.VMEM((RB, d), F32), pltpu.VMEM((8, d), F32)],
        compiler_params=_params(("arbitrary",)),
    )(s_pad, proj, target, w3, wa, wb, conv_a_b, ln_g, ln_b, b_a_out, final_g, norm_g)


def _mix_bwd(ds2, proj, ca, cb, ya, yb, w3, wa, wb, ln_g, ln_b):
    tp, d = ds2.shape
    nt = tp // TM
    nrb = TM // RB
    shl = TM + SHIFT_ROWS
    nt_dims = (((1,), (1,)), ((), ()))

    def body(ds2_ref, proj_ref, ca_ref, cb_ref, ya_ref, yb_ref, w3_ref, wa_ref, wb_ref, lng_ref, lnb_ref,
             dproj_ref, d3_ref, sm_ref, ext_d, ext_e, sh, dm_s, dpa_s, dpb_s, dua0_s, acc):
        step = pl.program_id(0)

        def split(k, rows):
            return proj_ref[rows, k * d:(k + 1) * d].astype(F32)

        def put(k, rows, val):
            dproj_ref[rows, k * d:(k + 1) * d] = val.astype(BF16)

        def accum(row, val):
            acc[row] += val.reshape(RB // 8, 8, d).sum(axis=0)

        @pl.when(step == 0)
        def _():
            ext_d[TM:TM + HALO_A, :] = jnp.zeros((HALO_A, d), F32)
            ext_e[TM:TM + HALO_B, :] = jnp.zeros((HALO_B, d), F32)
            acc[...] = jnp.zeros_like(acc)

        d3_ref[2] = ds2_ref[...].astype(BF16)
        dm_s[...] = lax.dot_general(d3_ref[2], w3_ref[2], nt_dims, preferred_element_type=F32)

        def gates(rb, carry):
            rows = _rows(rb)
            dm = dm_s[rows, :]
            sa = _sigmoid(split(7, rows))
            sb = _sigmoid(split(8, rows))
            ya_v = ya_ref[rows, :]
            yb_v = yb_ref[rows, :]
            put(7, rows, dm * ya_v * sa * (1.0 - sa))
            put(8, rows, dm * yb_v * sb * (1.0 - sb))
            dya = dm * sa
            accum(ROW_DBAO, dya)
            d3_ref[0, rows, :] = dya.astype(BF16)
            d3_ref[1, rows, :] = (dm * sb).astype(BF16)
            return carry
        lax.fori_loop(0, nrb, gates, 0)

        dpa_s[...] = lax.dot_general(d3_ref[0], w3_ref[0], nt_dims, preferred_element_type=F32)
        dpb_s[...] = lax.dot_general(d3_ref[1], w3_ref[1], nt_dims, preferred_element_type=F32)

        def branches(rb, carry):
            rows = _rows(rb)
            ca_v = ca_ref[rows, :]
            mu = jnp.mean(ca_v, axis=-1, keepdims=True)
            xc = ca_v - mu
            rstd = lax.rsqrt(jnp.mean(xc * xc, axis=-1, keepdims=True) + EPS)
            xhat = xc * rstd
            ln = xhat * lng_ref[...] + lnb_ref[...]
            sl = _sigmoid(ln)
            ua = ln * sl
            a_z = split(2, rows)
            sz = _sigmoid(a_z)
            dpa = dpa_s[rows, :]
            put(2, rows, dpa * ua * (sz * (1.0 + a_z * (1.0 - sz))))
            dln = dpa * (a_z * sz) * (sl * (1.0 + ln * (1.0 - sl)))
            accum(ROW_DLNG, dln * xhat)
            accum(ROW_DLNB, dln)
            dxh = dln * lng_ref[...]
            dca = rstd * (dxh - jnp.mean(dxh, axis=-1, keepdims=True)
                          - xhat * jnp.mean(dxh * xhat, axis=-1, keepdims=True))
            accum(ROW_DCAB, dca)
            ext_d[rows, :] = dca
            dua0_s[rows, :] = jnp.zeros((RB, d), F32)
            dm_s[rows, :] = split(0, rows) * _sigmoid(split(1, rows))
            b_z = split(6, rows)
            szb = _sigmoid(b_z)
            dpb = dpb_s[rows, :]
            b_b = split(3, rows)
            cb_v = cb_ref[rows, :]
            put(6, rows, dpb * (b_b * cb_v) * (szb * (1.0 + b_z * (1.0 - szb))))
            dub = dpb * (b_z * szb)
            put(3, rows, dub * cb_v)
            ext_e[rows, :] = dub * b_b
            return carry
        lax.fori_loop(0, nrb, branches, 0)

        for r, taps in _conv_a_taps(0, CONV_A - 1):
            if r == 0:
                src = ext_d
            else:
                sh[...] = ext_d[r:r + shl, :]
                src = sh

            def conv_t(rb, carry, src=src, taps=taps):
                rows = _rows(rb)
                ua0 = dm_s[rows, :]
                dua0 = dua0_s[rows, :]
                for q, lag in taps:
                    k = CONV_A - 1 - lag
                    slab = src[pl.ds(pl.multiple_of(rb * RB + 8 * q, 8), RB), :]
                    dua0 = dua0 + slab * wa_ref[k:k + 1, :]
                    accum(ROW_DWA + k, slab * ua0)
                dua0_s[rows, :] = dua0
                return carry
            lax.fori_loop(0, nrb, conv_t, 0)
        ext_d[TM:TM + HALO_A, :] = ext_d[0:HALO_A, :]

        dpb_s[...] = ext_e[0:TM, :] * wb_ref[CONV_B - 1:CONV_B, :]
        for lag in range(CONV_B):
            k = CONV_B - 1 - lag
            if lag > 0:
                sh[0:TM, :] = ext_e[lag:lag + TM, :]
                dpb_s[...] += sh[0:TM, :] * wb_ref[k:k + 1, :]
            src = ext_e if lag == 0 else sh

            def conv_b_w(rb, carry, src=src, k=k):
                rows = _rows(rb)
                accum(ROW_DWB + k, src[rows, :] * (split(4, rows) * split(5, rows)))
                return carry
            lax.fori_loop(0, nrb, conv_b_w, 0)
        ext_e[TM:TM + HALO_B, :] = ext_e[0:HALO_B, :]

        def inputs(rb, carry):
            rows = _rows(rb)
            dua0 = dua0_s[rows, :]
            a_val = split(0, rows)
            sg = _sigmoid(split(1, rows))
            put(0, rows, dua0 * sg)
            put(1, rows, dua0 * a_val * sg * (1.0 - sg))
            dcbin = dpb_s[rows, :]
            put(4, rows, dcbin * split(5, rows))
            put(5, rows, dcbin * split(4, rows))
            return carry
        lax.fori_loop(0, nrb, inputs, 0)

        @pl.when(step == nt - 1)
        def _():
            for row in range(SM_ROWS):
                sm_ref[row:row + 1, :] = jnp.sum(acc[row], axis=0, keepdims=True)

    rev = lambda i: (nt - 1 - i, 0)
    row_f32 = pl.BlockSpec((TM, d), rev)
    const = lambda shape: pl.BlockSpec(shape, lambda i: (0,) * len(shape))
    return pl.pallas_call(
        body, name="b1_mix", grid=(nt,),
        in_specs=[row_f32, pl.BlockSpec((TM, N_SPLIT * d), rev), row_f32, row_f32, row_f32, row_f32,
                  const((3, d, d)), const(wa.shape), const(wb.shape), const((1, d)), const((1, d))],
        out_specs=[pl.BlockSpec((TM, N_SPLIT * d), rev),
                   pl.BlockSpec((3, TM, d), lambda i: (0, nt - 1 - i, 0)),
                   const((SM_ROWS, d))],
        out_shape=[jax.ShapeDtypeStruct((tp, N_SPLIT * d), BF16), jax.ShapeDtypeStruct((3, tp, d), BF16),
                   jax.ShapeDtypeStruct((SM_ROWS, d), F32)],
        scratch_shapes=[pltpu.VMEM((TM + HALO_A, d), F32), pltpu.VMEM((TM + HALO_B, d), F32),
                        pltpu.VMEM((shl, d), F32), pltpu.VMEM((TM, d), F32), pltpu.VMEM((TM, d), F32),
                        pltpu.VMEM((TM, d), F32), pltpu.VMEM((TM, d), F32),
                        pltpu.VMEM((SM_ROWS, 8, d), F32)],
        compiler_params=_params(("arbitrary",)),
    )(ds2, proj, ca, cb, ya, yb, w3, wa, wb, ln_g, ln_b)


def kernel(x, meta_tokens, norm_g, w_in, conv_a_w, conv_a_b, ln_a_g, ln_a_b, w_a_out, b_a_out, conv_b_w, w_b_out, w_out, final_g, loss_target, m_meta_tokens, m_norm_g, m_w_in, m_conv_a_w, m_conv_a_b, m_ln_a_g, m_ln_a_b, m_w_a_out, m_b_a_out, m_conv_b_w, m_w_b_out, m_w_out, m_final_g, v_meta_tokens, v_norm_g, v_w_in, v_conv_a_w, v_conv_a_b, v_ln_a_g, v_ln_a_b, v_w_a_out, v_b_a_out, v_conv_b_w, v_w_b_out, v_w_out, v_final_g):
    seq, d = x.shape[1], x.shape[2]
    dc = meta_tokens.shape[1]
    sw = w_in.shape[2]
    rsh = w_a_out.shape[1]
    xi, yi, ci = _mesh_pos()
    me = 2 * xi + yi
    pos = jnp.stack([ci, me]).astype(jnp.int32)

    smalls = jnp.concatenate([
        jnp.pad(conv_a_w[0], ((0, HALO_A - CONV_A), (0, 0))),
        jnp.pad(conv_b_w[0], ((0, HALO_B - CONV_B), (0, 0))),
        meta_tokens, jnp.zeros((8, dc), F32)], axis=0)[None]
    w3_own = jnp.stack([w_a_out[0], w_b_out[0], w_out[0]])
    wg_in, wg3, smg = _all_gather([_place_own(w_in, pos, BF16, "place_in"),
                                   _place_own(w3_own, pos, BF16, "place_sq"),
                                   _place_own(smalls, pos, F32, "place_small")])
    w3 = wg3.reshape(3, N_CHIPS * rsh, d)
    smg = jnp.transpose(smg[0], (1, 0, 2)).reshape(smalls.shape[1], N_CHIPS * dc)
    wa_full = smg[0:HALO_A]
    wb_full = smg[HALO_A:HALO_A + HALO_B]
    meta_full = smg[HALO_A + HALO_B:HALO_A + HALO_B + N_META]
    fg2 = final_g.reshape(1, d)

    first_tile = jnp.concatenate([jnp.zeros((TM - N_META, d), F32), meta_full], axis=0)
    s_pad = jnp.concatenate([first_tile, x[0]], axis=0)

    proj = _proj_fwd(s_pad, norm_g, wg_in)
    ca, cb, ya, yb, abm_t, ds2, h_t, loss8, dfg8 = _mix_fwd(
        s_pad, proj, loss_target[0], w3, wa_full, wb_full, conv_a_b, ln_a_g, ln_a_b, b_a_out, fg2, norm_g)
    dproj, d3, sm = _mix_bwd(ds2, proj, ca, cb, ya, yb, w3, wa_full, wb_full, ln_a_g, ln_a_b)
    ds, dng8 = _dh_bwd(dproj, wg_in, s_pad, ds2, norm_g)
    g_in = _dw_in(h_t, dproj, N_CHIPS)
    g_sq = _dw_square(abm_t, d3).reshape(3, N_CHIPS, rsh, d)

    r_in, r_sq = _sibling_exchange([g_in, g_sq])
    p32_in, pbf_in = _add_sibling(g_in, r_in, pos, "rs_add_in")
    p32_sq, pbf_sq = _add_sibling(g_sq, r_sq, pos, "rs_add_sq")
    l_in, l_sq = _chip_exchange([pbf_in, pbf_sq])
    half_in = _sum_chips(p32_in, l_in, pos, "rs_sum_in")
    half_sq = _sum_chips(p32_sq, l_sq, pos, "rs_sum_sq")
    other_in, other_sq = _sibling_swap([half_in, half_sq])

    tail_row = lax.broadcasted_iota(jnp.int32, (8, d), 0)
    tail = jnp.where(tail_row == 0, dng8, jnp.where(tail_row == 1, dfg8,
                     jnp.where(tail_row == 2, loss8[0, 0], 0.0)))
    block = jnp.concatenate([sm, ds[TM - N_META:TM], tail], axis=0)
    red = _all_reduce_small(block)
    col = lax.dynamic_slice(red, (0, me * dc), (AR_ROWS, dc))
    g_small = {
        "meta_tokens": col[ROW_DMETA:ROW_DMETA + N_META],
        "norm_g": red[ROW_DNG:ROW_DNG + 1],
        "conv_a_w": col[ROW_DWA:ROW_DWA + CONV_A][None],
        "conv_a_b": red[ROW_DCAB:ROW_DCAB + 1],
        "ln_a_g": red[ROW_DLNG:ROW_DLNG + 1],
        "ln_a_b": red[ROW_DLNB:ROW_DLNB + 1],
        "b_a_out": red[ROW_DBAO:ROW_DBAO + 1],
        "conv_b_w": col[ROW_DWB:ROW_DWB + CONV_B][None],
        "final_g": red[ROW_DFG],
    }

    upd_in = _adam_halves([w_in], [m_w_in], [v_w_in], half_in, other_in, pos, "adam_in")
    upd_sq = _adam_halves([w_a_out, w_b_out, w_out], [m_w_a_out, m_w_b_out, m_w_out],
                          [v_w_a_out, v_w_b_out, v_w_out], half_sq, other_sq, pos, "adam_sq")
    small_w = {"meta_tokens": (meta_tokens, m_meta_tokens, v_meta_tokens), "norm_g": (norm_g, m_norm_g, v_norm_g),
               "conv_a_w": (conv_a_w, m_conv_a_w, v_conv_a_w), "conv_a_b": (conv_a_b, m_conv_a_b, v_conv_a_b),
               "ln_a_g": (ln_a_g, m_ln_a_g, v_ln_a_g), "ln_a_b": (ln_a_b, m_ln_a_b, v_ln_a_b),
               "b_a_out": (b_a_out, m_b_a_out, v_b_a_out), "conv_b_w": (conv_b_w, m_conv_b_w, v_conv_b_w),
               "final_g": (final_g, m_final_g, v_final_g)}
    names_small = list(small_w)
    as2d = lambda t: t.reshape(-1, t.shape[-1])
    upd_small = _adam_small([(as2d(small_w[k][0]), as2d(g_small[k]), as2d(small_w[k][1]), as2d(small_w[k][2]))
                             for k in names_small])

    grads, deltas, new_m, new_v = dict(g_small), {}, {}, {}
    for k, upd in zip(names_small, upd_small):
        deltas[k], new_m[k], new_v[k] = [t.reshape(small_w[k][0].shape) for t in upd]
    grads["w_in"], deltas["w_in"], new_m["w_in"], new_v["w_in"] = upd_in
    for idx, k in enumerate(["w_a_out", "w_b_out", "w_out"]):
        grads[k], deltas[k], new_m[k], new_v[k] = upd_sq[4 * idx:4 * idx + 4]

    loss = red[ROW_LOSS, 0]
    grad_x = ds[TM:][None]
    order = ["meta_tokens", "norm_g", "w_in", "conv_a_w", "conv_a_b", "ln_a_g", "ln_a_b", "w_a_out", "b_a_out",
             "conv_b_w", "w_b_out", "w_out", "final_g"]
    return (loss, grad_x, *[grads[k] for k in order], *[deltas[k] for k in order],
            *[new_m[k] for k in order], *[new_v[k] for k in order])
```

```python
import functools

import jax
import jax.numpy as jnp
from jax import lax
from jax.experimental import pallas as pl
from jax.experimental.pallas import tpu as pltpu

F32 = jnp.float32
BF16 = jnp.bfloat16
MESH = pl.DeviceIdType.MESH

EPS = 1e-6
N_META = 16
N_SPLIT = 9
CONV_A = 31
CONV_B = 3
HALO_A = 32
HALO_B = 8
SHIFT_ROWS = 24
TM = 256
RB = 64
N_ROW_TILES_BIG = 8
N_CHIPS = 4
VMEM_LIMIT = 56 * 1024 * 1024

ADAM_LR = 0.001
ADAM_B1 = 0.9
ADAM_B2 = 0.999
ADAM_EPS = 1e-08
ADAM_WD = 0.01
ADAM_STEP = 10

ROW_DWA = 0
ROW_DWB = 32
ROW_DCAB = 40
ROW_DLNG = 41
ROW_DLNB = 42
ROW_DBAO = 43
SM_ROWS = 48
ROW_DMETA = 48
ROW_DNG = 64
ROW_DFG = 65
ROW_LOSS = 66
AR_ROWS = 72


def _sigmoid(v):
    return 0.5 * jnp.tanh(0.5 * v) + 0.5


def _params(sem, **kw):
    return pltpu.CompilerParams(dimension_semantics=sem, vmem_limit_bytes=VMEM_LIMIT, **kw)


def _rows(rb):
    return pl.ds(pl.multiple_of(rb * RB, RB), RB)


def _mesh_pos():
    x, y, c = lax.axis_index("x"), lax.axis_index("y"), lax.axis_index("c")
    return x, y, c


def _half(ref, j, c):
    h = ref.shape[2] // 2
    return ref.at[:, j, pl.ds(c * h, h), :]


def _place_own(shard, pos, dtype, name):
    s, r, c = shard.shape
    rb = 128 if r % 128 == 0 else r

    def body(pos_ref, x_ref, o_ref):
        o_ref[...] = x_ref[...].astype(dtype)

    return pl.pallas_call(
        body, name=name,
        grid_spec=pltpu.PrefetchScalarGridSpec(
            num_scalar_prefetch=1, grid=(s, r // rb),
            in_specs=[pl.BlockSpec((None, rb, c), lambda si, b, pos_ref: (si, b, 0))],
            out_specs=pl.BlockSpec((None, None, rb, c), lambda si, b, pos_ref: (si, pos_ref[1], b, 0))),
        out_shape=jax.ShapeDtypeStruct((s, N_CHIPS, r, c), dtype),
        compiler_params=_params(("arbitrary",) * 2),
    )(pos, shard)


def _all_gather(bufs):
    n = len(bufs)

    def body(*refs):
        outs = refs[n:2 * n]
        send_sems, recv_sems = refs[2 * n:]
        x, y, c = _mesh_pos()
        me = 2 * x + y
        sibling = (x, y, 1 - c)
        chips = [(1 - x, y), (x, 1 - y), (1 - x, 1 - y)]

        def remote(a, k, piece_src, piece_dst, to):
            return pltpu.make_async_remote_copy(
                src_ref=piece_src, dst_ref=piece_dst, send_sem=send_sems.at[6 * a + k],
                recv_sem=recv_sems.at[6 * a + k], device_id=to, device_id_type=MESH)

        sends = []
        for a in range(n):
            mine = _half(outs[a], me, c)
            for k, (px, py) in enumerate(chips):
                sends.append(remote(a, k, mine, mine, (px, py, c)))
        for cp in sends:
            cp.start()
        for a in range(n):
            for k, (px, py) in enumerate(chips):
                piece = _half(outs[a], 2 * px + py, c)
                remote(a, k, piece, piece, (px, py, c)).wait_recv()
                fwd = remote(a, 3 + k, piece, piece, sibling)
                fwd.start()
                sends.append(fwd)
        for a in range(n):
            for k, (px, py) in enumerate(chips):
                piece = _half(outs[a], 2 * px + py, 1 - c)
                remote(a, 3 + k, piece, piece, sibling).wait_recv()
        for cp in sends:
            cp.wait_send()

    any_spec = pl.BlockSpec(memory_space=pl.ANY)
    return pl.pallas_call(
        body, name="ag_weights",
        in_specs=[any_spec] * n, out_specs=[any_spec] * n,
        out_shape=[jax.ShapeDtypeStruct(b.shape, b.dtype) for b in bufs],
        input_output_aliases={a: a for a in range(n)},
        scratch_shapes=[pltpu.SemaphoreType.DMA((6 * n,)), pltpu.SemaphoreType.DMA((6 * n,))],
    )(*bufs)


def _sibling_exchange(grads, name):
    n = len(grads)

    def body(*refs):
        ins, outs = refs[:n], refs[n:2 * n]
        send_sems, recv_sems = refs[2 * n:]
        x, y, c = _mesh_pos()
        copies = []
        for a in range(n):
            h = ins[a].shape[2] // 2
            copies.append(pltpu.make_async_remote_copy(
                src_ref=ins[a].at[:, :, pl.ds((1 - c) * h, h), :], dst_ref=outs[a],
                send_sem=send_sems.at[a], recv_sem=recv_sems.at[a],
                device_id=(x, y, 1 - c), device_id_type=MESH))
        for cp in copies:
            cp.start()
        for cp in copies:
            cp.wait()

    any_spec = pl.BlockSpec(memory_space=pl.ANY)
    return pl.pallas_call(
        body, name=name,
        in_specs=[any_spec] * n, out_specs=[any_spec] * n,
        out_shape=[jax.ShapeDtypeStruct(g.shape[:2] + (g.shape[2] // 2, g.shape[3]), g.dtype) for g in grads],
        scratch_shapes=[pltpu.SemaphoreType.DMA((n,)), pltpu.SemaphoreType.DMA((n,))],
    )(*grads)


class _Exchange:
    def __init__(self, sends, recvs):
        self.sends, self.recvs = sends, recvs

    def start(self):
        for cp in self.sends:
            cp.start()

    def finish(self):
        for cp in self.recvs:
            cp.wait_recv()
        for cp in self.sends:
            cp.wait_send()


def _chip_exchange(part_ref, land_ref, send_sems, recv_sems):
    x, y, c = _mesh_pos()
    me = 2 * x + y
    sends, recvs = [], []
    for k, (px, py) in enumerate([(1 - x, y), (x, 1 - y), (1 - x, 1 - y)]):
        sems = dict(send_sem=send_sems.at[k], recv_sem=recv_sems.at[k], device_id=(px, py, c), device_id_type=MESH)
        sends.append(pltpu.make_async_remote_copy(
            src_ref=part_ref.at[:, 2 * px + py], dst_ref=land_ref.at[:, me], **sems))
        landed = land_ref.at[:, 2 * px + py]
        recvs.append(pltpu.make_async_remote_copy(src_ref=landed, dst_ref=landed, **sems))
    return _Exchange(sends, recvs)


def _direct_gather(buf_refs, send_sems, recv_sems):
    x, y, c = _mesh_pos()
    me = 2 * x + y
    sends, recvs = [], []
    for a, buf in enumerate(buf_refs):
        mine = _half(buf, me, c)
        for k, (px, py) in enumerate([(1 - x, y), (x, 1 - y), (1 - x, 1 - y)]):
            for core in range(2):
                sends.append(pltpu.make_async_remote_copy(
                    src_ref=mine, dst_ref=mine, send_sem=send_sems.at[6 * a + 2 * k + core],
                    recv_sem=recv_sems.at[6 * a + 2 * k + c], device_id=(px, py, core), device_id_type=MESH))
                landed = _half(buf, 2 * px + py, core)
                recvs.append(pltpu.make_async_remote_copy(
                    src_ref=landed, dst_ref=landed, send_sem=send_sems.at[6 * a + 2 * k + core],
                    recv_sem=recv_sems.at[6 * a + 2 * k + core], device_id=(px, py, core), device_id_type=MESH))
    return _Exchange(sends, recvs)


def _sibling_swap(halves):
    n = len(halves)

    def body(*refs):
        ins, outs = refs[:n], refs[n:2 * n]
        send_sems, recv_sems = refs[2 * n:]
        x, y, c = _mesh_pos()
        copies = [pltpu.make_async_remote_copy(
            src_ref=ins[a], dst_ref=outs[a], send_sem=send_sems.at[a], recv_sem=recv_sems.at[a],
            device_id=(x, y, 1 - c), device_id_type=MESH) for a in range(n)]
        for cp in copies:
            cp.start()
        for cp in copies:
            cp.wait()

    any_spec = pl.BlockSpec(memory_space=pl.ANY)
    return pl.pallas_call(
        body, name="rs_swap",
        in_specs=[any_spec] * n, out_specs=[any_spec] * n,
        out_shape=[jax.ShapeDtypeStruct(h.shape, h.dtype) for h in halves],
        scratch_shapes=[pltpu.SemaphoreType.DMA((n,)), pltpu.SemaphoreType.DMA((n,))],
    )(*halves)


def _all_reduce_small(block):
    rows, d = block.shape

    def body(x_ref, out_ref, sib_ref, part_ref, peers_ref, send_sems, recv_sems):
        x, y, c = _mesh_pos()
        me = 2 * x + y
        chips = [(1 - x, y), (x, 1 - y), (1 - x, 1 - y)]
        swap = pltpu.make_async_remote_copy(
            src_ref=x_ref, dst_ref=sib_ref, send_sem=send_sems.at[0], recv_sem=recv_sems.at[0],
            device_id=(x, y, 1 - c), device_id_type=MESH)
        swap.start()
        swap.wait()
        part_ref[...] = x_ref[...] + sib_ref[...]
        peers_ref[me] = part_ref[...]
        sends = [pltpu.make_async_remote_copy(
            src_ref=part_ref, dst_ref=peers_ref.at[me], send_sem=send_sems.at[1 + k], recv_sem=recv_sems.at[1 + k],
            device_id=(px, py, c), device_id_type=MESH) for k, (px, py) in enumerate(chips)]
        for cp in sends:
            cp.start()
        for k, (px, py) in enumerate(chips):
            landed = peers_ref.at[2 * px + py]
            pltpu.make_async_remote_copy(
                src_ref=landed, dst_ref=landed, send_sem=send_sems.at[1 + k], recv_sem=recv_sems.at[1 + k],
                device_id=(px, py, c), device_id_type=MESH).wait_recv()
        for cp in sends:
            cp.wait_send()
        out_ref[...] = ((peers_ref[0] + peers_ref[1]) + peers_ref[2]) + peers_ref[3]

    vm = pl.BlockSpec(memory_space=pltpu.VMEM)
    return pl.pallas_call(
        body, name="ar_small",
        in_specs=[vm], out_specs=vm,
        out_shape=jax.ShapeDtypeStruct((rows, d), F32),
        scratch_shapes=[pltpu.VMEM((rows, d), F32), pltpu.VMEM((rows, d), F32),
                        pltpu.VMEM((N_CHIPS, rows, d), F32),
                        pltpu.SemaphoreType.DMA((4,)), pltpu.SemaphoreType.DMA((4,))],
    )(block)


def _add_sibling(grad, recv, pos, name):
    s, nch, r, c = grad.shape
    h = r // 2
    hb = min(h, 128)

    def body(pos_ref, g_ref, r_ref, p32_ref, pbf_ref):
        p = g_ref[...] + r_ref[...]
        p32_ref[...] = p
        pbf_ref[...] = p.astype(BF16)

    nb = h // hb
    spec_g = pl.BlockSpec((None, None, hb, c), lambda si, j, b, pos_ref: (si, j, pos_ref[0] * nb + b, 0))
    spec_h = pl.BlockSpec((None, None, hb, c), lambda si, j, b, pos_ref: (si, j, b, 0))
    return pl.pallas_call(
        body, name=name,
        grid_spec=pltpu.PrefetchScalarGridSpec(
            num_scalar_prefetch=1, grid=(s, nch, nb), in_specs=[spec_g, spec_h], out_specs=[spec_h, spec_h]),
        out_shape=[jax.ShapeDtypeStruct((s, nch, h, c), F32), jax.ShapeDtypeStruct((s, nch, h, c), BF16)],
        compiler_params=_params(("arbitrary",) * 3),
    )(pos, grad, recv)


def _sum_chips(p32, landed, pos, name):
    s, nch, h, c = p32.shape
    hb = min(h, 128)

    def body(pos_ref, p_ref, l1_ref, l2_ref, l3_ref, out_ref):
        out_ref[...] = ((p_ref[...] + l1_ref[...].astype(F32)) + l2_ref[...].astype(F32)) + l3_ref[...].astype(F32)

    def slot(k):
        return pl.BlockSpec((None, None, hb, c), lambda si, b, pos_ref: (si, (pos_ref[1] + k) % N_CHIPS, b, 0))

    return pl.pallas_call(
        body, name=name,
        grid_spec=pltpu.PrefetchScalarGridSpec(
            num_scalar_prefetch=1, grid=(s, h // hb),
            in_specs=[slot(0), slot(1), slot(2), slot(3)],
            out_specs=pl.BlockSpec((None, hb, c), lambda si, b, pos_ref: (si, b, 0))),
        out_shape=jax.ShapeDtypeStruct((s, h, c), F32),
        compiler_params=_params(("arbitrary",) * 2),
    )(pos, p32, landed, landed, landed)


def _adamw(w, g, m, v):
    m = ADAM_B1 * m + (1.0 - ADAM_B1) * g
    v = ADAM_B2 * v + (1.0 - ADAM_B2) * (g * g)
    m_hat = m / (1.0 - ADAM_B1 ** ADAM_STEP)
    v_hat = v / (1.0 - ADAM_B2 ** ADAM_STEP)
    delta = -ADAM_LR * (m_hat / (jnp.sqrt(v_hat) + ADAM_EPS) + ADAM_WD * w)
    return delta, m, v


def _adam_halves(ws, ms, vs, g_own, g_recv, pos, name):
    n = len(ws)
    _, r, c = ws[0].shape
    h = r // 2
    rb = min(h, 128)
    nb = h // rb

    def body(pos_ref, *refs):
        w_refs, m_refs, v_refs = refs[:n], refs[n:2 * n], refs[2 * n:3 * n]
        go_ref, gr_ref = refs[3 * n:3 * n + 2]
        outs = refs[3 * n + 2:]
        mine = pl.program_id(0) == pos_ref[0]
        for a in range(n):
            g = jnp.where(mine, go_ref[a], gr_ref[a])
            delta, m, v = _adamw(w_refs[a][...], g, m_refs[a][...], v_refs[a][...])
            outs[4 * a][...], outs[4 * a + 1][...], outs[4 * a + 2][...], outs[4 * a + 3][...] = g, delta, m, v

    spec_w = pl.BlockSpec((None, rb, c), lambda hf, b, pos_ref: (0, hf * nb + b, 0))
    spec_g = pl.BlockSpec((n, rb, c), lambda hf, b, pos_ref: (0, b, 0))
    return pl.pallas_call(
        body, name=name,
        grid_spec=pltpu.PrefetchScalarGridSpec(
            num_scalar_prefetch=1, grid=(2, nb), in_specs=[spec_w] * (3 * n) + [spec_g] * 2,
            out_specs=[spec_w] * (4 * n)),
        out_shape=[jax.ShapeDtypeStruct((1, r, c), F32)] * (4 * n),
        compiler_params=_params(("arbitrary",) * 2),
    )(pos, *ws, *ms, *vs, g_own, g_recv)


def _adam_small(items):
    n = len(items)

    def body(*refs):
        ins, outs = refs[:4 * n], refs[4 * n:]
        for a in range(n):
            w_ref, g_ref, m_ref, v_ref = ins[4 * a:4 * a + 4]
            d, m, v = _adamw(w_ref[...], g_ref[...], m_ref[...], v_ref[...])
            outs[3 * a][...] = d
            outs[3 * a + 1][...] = m
            outs[3 * a + 2][...] = v

    vm = pl.BlockSpec(memory_space=pltpu.VMEM)
    flat = [t for it in items for t in it]
    outs = pl.pallas_call(
        body, name="adam_small", in_specs=[vm] * (4 * n), out_specs=[vm] * (3 * n),
        out_shape=[jax.ShapeDtypeStruct(it[0].shape, F32) for it in items for _ in range(3)],
    )(*flat)
    return [tuple(outs[3 * a:3 * a + 3]) for a in range(n)]


def _proj_fwd(s_pad, norm_g, wg_in, later_bufs):
    tp, d = s_pad.shape
    _, nsh, _, sw = wg_in.shape
    tmb = tp // N_ROW_TILES_BIG
    n = len(later_bufs)

    def body(s_ref, g_ref, w_ref, *refs):
        proj_ref = refs[n]
        gather = _direct_gather(refs[n + 1:2 * n + 1], *refs[2 * n + 1:])
        j, i = pl.program_id(0), pl.program_id(1)

        @pl.when((j == 0) & (i == 0))
        def _():
            gather.start()

        s = s_ref[...]
        r = lax.rsqrt(jnp.mean(s * s, axis=-1, keepdims=True) + EPS)
        h = (s * r * g_ref[...]).astype(BF16)
        proj_ref[...] = jnp.dot(h, w_ref[...], preferred_element_type=F32).astype(BF16)

        @pl.when((j == nsh - 1) & (i == N_ROW_TILES_BIG - 1))
        def _():
            gather.finish()

    any_spec = pl.BlockSpec(memory_space=pl.ANY)
    outs = pl.pallas_call(
        body, name="f1_proj", grid=(nsh, N_ROW_TILES_BIG),
        in_specs=[pl.BlockSpec((tmb, d), lambda j, i: (i, 0)),
                  pl.BlockSpec((1, d), lambda j, i: (0, 0)),
                  pl.BlockSpec((None, None, d, sw), lambda j, i: (0, j, 0, 0))] + [any_spec] * n,
        out_specs=[pl.BlockSpec((tmb, sw), lambda j, i: (i, j))] + [any_spec] * n,
        out_shape=[jax.ShapeDtypeStruct((tp, nsh * sw), BF16)]
        + [jax.ShapeDtypeStruct(b.shape, b.dtype) for b in later_bufs],
        input_output_aliases={3 + a: 1 + a for a in range(n)},
        scratch_shapes=[pltpu.SemaphoreType.DMA((6 * n,)), pltpu.SemaphoreType.DMA((6 * n,))],
        compiler_params=_params(("arbitrary", "arbitrary")),
    )(s_pad, norm_g, wg_in, *later_bufs)
    return outs[0], outs[1:]


def _dh_bwd(dproj, wg_in, s_pad, ds2, norm_g, part):
    tp, d = s_pad.shape
    _, nsh, _, sw = wg_in.shape
    tmb = tp // N_ROW_TILES_BIG

    def body(dp_ref, w_ref, s_ref, ds2_ref, g_ref, part_ref, ds_ref, dng_ref, land_ref, acc, gacc,
             send_sems, recv_sems):
        exchange = _chip_exchange(part_ref, land_ref, send_sems, recv_sems)
        i, j = pl.program_id(0), pl.program_id(1)

        @pl.when((i == 0) & (j == 0))
        def _():
            exchange.start()
            gacc[...] = jnp.zeros_like(gacc)

        part = lax.dot_general(dp_ref[...], w_ref[...], (((1,), (1,)), ((), ())), preferred_element_type=F32)

        @pl.when(j == 0)
        def _():
            acc[...] = part

        @pl.when(j > 0)
        def _():
            acc[...] += part

        @pl.when(j == nsh - 1)
        def _():
            dh = acc[...]
            s = s_ref[...]
            r = lax.rsqrt(jnp.mean(s * s, axis=-1, keepdims=True) + EPS)
            gacc[...] += (dh * s * r).reshape(tmb // 8, 8, d).sum(axis=0)
            t = dh * g_ref[...]
            ds_ref[...] = ds2_ref[...] + r * t - s * (r * r * r) * jnp.mean(t * s, axis=-1, keepdims=True)

        @pl.when((i == N_ROW_TILES_BIG - 1) & (j == nsh - 1))
        def _():
            dng_ref[...] = jnp.broadcast_to(jnp.sum(gacc[...], axis=0, keepdims=True), (8, d))
            exchange.finish()

    any_spec = pl.BlockSpec(memory_space=pl.ANY)
    return pl.pallas_call(
        body, name="b2_dh", grid=(N_ROW_TILES_BIG, nsh),
        in_specs=[pl.BlockSpec((tmb, sw), lambda i, j: (i, j)),
                  pl.BlockSpec((None, None, d, sw), lambda i, j: (0, j, 0, 0)),
                  pl.BlockSpec((tmb, d), lambda i, j: (i, 0)),
                  pl.BlockSpec((tmb, d), lambda i, j: (i, 0)),
                  pl.BlockSpec((1, d), lambda i, j: (0, 0)), any_spec],
        out_specs=[pl.BlockSpec((tmb, d), lambda i, j: (i, 0)),
                   pl.BlockSpec((8, d), lambda i, j: (0, 0)), any_spec],
        out_shape=[jax.ShapeDtypeStruct((tp, d), F32), jax.ShapeDtypeStruct((8, d), F32),
                   jax.ShapeDtypeStruct(part.shape, part.dtype)],
        scratch_shapes=[pltpu.VMEM((tmb, d), F32), pltpu.VMEM((8, d), F32),
                        pltpu.SemaphoreType.DMA((3,)), pltpu.SemaphoreType.DMA((3,))],
        compiler_params=_params(("arbitrary", "arbitrary")),
    )(dproj, wg_in, s_pad, ds2, norm_g, part)


def _col_block(width, cap):
    return max(b for b in range(128, cap + 1, 128) if width % b == 0)


def _dw_in(h_t, dproj, nsh, part):
    d, tp = h_t.shape
    sw = dproj.shape[1] // nsh
    cw = _col_block(sw, 768)
    ncol = sw // cw

    def body(h_ref, dp_ref, part_ref, out_ref, land_ref, send_sems, recv_sems):
        exchange = _chip_exchange(part_ref, land_ref, send_sems, recv_sems)
        j, n = pl.program_id(0), pl.program_id(1)

        @pl.when((j == 0) & (n == 0))
        def _():
            exchange.start()

        out_ref[...] = jnp.dot(h_ref[...], dp_ref[...], preferred_element_type=F32)

        @pl.when((j == nsh - 1) & (n == ncol - 1))
        def _():
            exchange.finish()

    any_spec = pl.BlockSpec(memory_space=pl.ANY)
    return pl.pallas_call(
        body, name="dw_in", grid=(nsh, ncol),
        in_specs=[pl.BlockSpec((d, tp), lambda j, n: (0, 0)),
                  pl.BlockSpec((tp, cw), lambda j, n: (0, j * ncol + n)), any_spec],
        out_specs=[pl.BlockSpec((None, None, d, cw), lambda j, n: (0, j, 0, n)), any_spec],
        out_shape=[jax.ShapeDtypeStruct((1, nsh, d, sw), F32), jax.ShapeDtypeStruct(part.shape, part.dtype)],
        scratch_shapes=[pltpu.SemaphoreType.DMA((3,)), pltpu.SemaphoreType.DMA((3,))],
        compiler_params=_params(("arbitrary",) * 2),
    )(h_t, dproj, part)


def _dw_square(lhs3_t, rhs3):
    n, d, tp = lhs3_t.shape
    cw = _col_block(d, 512)

    def body(a_ref, b_ref, out_ref):
        out_ref[...] = jnp.dot(a_ref[...], b_ref[...], preferred_element_type=F32)

    return pl.pallas_call(
        body, name="dw_square", grid=(n, d // cw),
        in_specs=[pl.BlockSpec((None, d, tp), lambda a, c: (a, 0, 0)),
                  pl.BlockSpec((None, tp, cw), lambda a, c: (a, 0, c))],
        out_specs=pl.BlockSpec((None, d, cw), lambda a, c: (a, 0, c)),
        out_shape=jax.ShapeDtypeStruct((n, d, d), F32),
        compiler_params=_params(("arbitrary",) * 2),
    )(lhs3_t, rhs3)


def _conv_a_taps(first_lag, last_lag):
    out = []
    for r in range(8):
        taps = [(q, 8 * q + r) for q in range(5) if first_lag <= 8 * q + r <= last_lag]
        if taps:
            out.append((r, taps))
    return out


def _mix_fwd(s_pad, proj, target, w3, wa, wb, conv_a_b, ln_g, ln_b, b_a_out, final_g, norm_g):
    tp, d = s_pad.shape
    nt = tp // TM
    nrb = TM // RB
    shl = TM + SHIFT_ROWS

    def body(s_ref, proj_ref, tgt_ref, w3_ref, wa_ref, wb_ref, cab_ref, lng_ref, lnb_ref, bao_ref, fg_ref, ng_ref,
             ca_ref, cb_ref, ya_ref, yb_ref, abmt_ref, ds2_ref, ht_ref, loss_ref, dfg_ref,
             abm_ref, ext_a, ext_b, sh, s2_s, lacc, gacc):
        i = pl.program_id(0)

        def split(k, rows):
            return proj_ref[rows, k * d:(k + 1) * d].astype(F32)

        s_in = s_ref[...]
        h = s_in * lax.rsqrt(jnp.mean(s_in * s_in, axis=-1, keepdims=True) + EPS) * ng_ref[...]
        ht_ref[...] = h.T.astype(BF16)

        @pl.when(i == 0)
        def _():
            ext_a[0:HALO_A, :] = jnp.zeros((HALO_A, d), F32)
            ext_b[0:HALO_B, :] = jnp.zeros((HALO_B, d), F32)
            lacc[...] = jnp.zeros_like(lacc)
            gacc[...] = jnp.zeros_like(gacc)

        def conv_in(rb, carry):
            rows = _rows(rb)
            ua0 = split(0, rows) * _sigmoid(split(1, rows))
            ext_a[pl.ds(pl.multiple_of(HALO_A + rb * RB, 8), RB), :] = ua0
            ext_b[pl.ds(pl.multiple_of(HALO_B + rb * RB, 8), RB), :] = split(4, rows) * split(5, rows)
            ca_ref[rows, :] = jnp.broadcast_to(cab_ref[...], (RB, d))
            return carry
        lax.fori_loop(0, nrb, conv_in, 0)

        for r, taps in _conv_a_taps(HALO_A - CONV_A + 1, HALO_A):
            if r == 0:
                src = ext_a
            else:
                sh[...] = ext_a[r:r + shl, :]
                src = sh

            def conv_acc(rb, carry, src=src, taps=taps):
                rows = _rows(rb)
                acc = ca_ref[rows, :]
                for q, lag in taps:
                    k = lag - (HALO_A - CONV_A + 1)
                    acc = acc + src[pl.ds(pl.multiple_of(rb * RB + 8 * q, 8), RB), :] * wa_ref[k:k + 1, :]
                ca_ref[rows, :] = acc
                return carry
            lax.fori_loop(0, nrb, conv_acc, 0)
        ext_a[0:HALO_A, :] = ext_a[TM:TM + HALO_A, :]

        cb_ref[...] = ext_b[HALO_B:HALO_B + TM, :] * wb_ref[2:3, :]
        for k in range(CONV_B - 1):
            off = HALO_B - CONV_B + 1 + k
            sh[0:TM, :] = ext_b[off:off + TM, :]
            cb_ref[...] += sh[0:TM, :] * wb_ref[k:k + 1, :]
        ext_b[0:HALO_B, :] = ext_b[TM:TM + HALO_B, :]

        def branches(rb, carry):
            rows = _rows(rb)
            ca = ca_ref[rows, :]
            mu = jnp.mean(ca, axis=-1, keepdims=True)
            xc = ca - mu
            rstd = lax.rsqrt(jnp.mean(xc * xc, axis=-1, keepdims=True) + EPS)
            ln = xc * rstd * lng_ref[...] + lnb_ref[...]
            ua = ln * _sigmoid(ln)
            a_z = split(2, rows)
            abm_ref[0, rows, :] = (ua * (a_z * _sigmoid(a_z))).astype(BF16)
            b_z = split(6, rows)
            ub = split(3, rows) * cb_ref[rows, :]
            abm_ref[1, rows, :] = (ub * (b_z * _sigmoid(b_z))).astype(BF16)
            return carry
        lax.fori_loop(0, nrb, branches, 0)

        ya_ref[...] = jnp.dot(abm_ref[0], w3_ref[0], preferred_element_type=F32) + bao_ref[...]
        yb_ref[...] = jnp.dot(abm_ref[1], w3_ref[1], preferred_element_type=F32)

        def merge(rb, carry):
            rows = _rows(rb)
            m = _sigmoid(split(7, rows)) * ya_ref[rows, :] + _sigmoid(split(8, rows)) * yb_ref[rows, :]
            abm_ref[2, rows, :] = m.astype(BF16)
            return carry
        lax.fori_loop(0, nrb, merge, 0)

        s2_s[...] = s_ref[...] + jnp.dot(abm_ref[2], w3_ref[2], preferred_element_type=F32)
        for k in range(3):
            abmt_ref[k] = abm_ref[k].astype(F32).T.astype(BF16)
        live = (i > 0).astype(F32)

        def head(rb, carry):
            rows = _rows(rb)
            s2 = s2_s[rows, :]
            r2 = lax.rsqrt(jnp.mean(s2 * s2, axis=-1, keepdims=True) + EPS)
            diff = (s2 * r2 * fg_ref[...] - tgt_ref[rows, :]) * live
            lacc[...] += diff * diff
            dy = diff * (1.0 / d)
            gacc[...] += (dy * s2 * r2).reshape(RB // 8, 8, d).sum(axis=0)
            t = dy * fg_ref[...]
            ds2_ref[rows, :] = r2 * t - s2 * (r2 * r2 * r2) * jnp.mean(t * s2, axis=-1, keepdims=True)
            return carry
        lax.fori_loop(0, nrb, head, 0)

        @pl.when(i == nt - 1)
        def _():
            loss_ref[...] = jnp.broadcast_to(0.5 * jnp.sum(lacc[...]) * (1.0 / d), (8, 128))
            dfg_ref[...] = jnp.broadcast_to(jnp.sum(gacc[...], axis=0, keepdims=True), (8, d))

    row_f32 = pl.BlockSpec((TM, d), lambda i: (i, 0))
    const = lambda shape: pl.BlockSpec(shape, lambda i: (0,) * len(shape))
    return pl.pallas_call(
        body, name="f2_mix", grid=(nt,),
        in_specs=[row_f32,
                  pl.BlockSpec((TM, N_SPLIT * d), lambda i: (i, 0)),
                  pl.BlockSpec((TM, d), lambda i: (jnp.maximum(i - 1, 0), 0)),
                  const((3, d, d)), const(wa.shape), const(wb.shape)] + [const((1, d))] * 6,
        out_specs=[row_f32, row_f32, row_f32, row_f32,
                   pl.BlockSpec((3, d, TM), lambda i: (0, 0, i)),
                   row_f32, pl.BlockSpec((d, TM), lambda i: (0, i)), const((8, 128)), const((8, d))],
        out_shape=[jax.ShapeDtypeStruct((tp, d), F32)] * 4
        + [jax.ShapeDtypeStruct((3, d, tp), BF16), jax.ShapeDtypeStruct((tp, d), F32),
           jax.ShapeDtypeStruct((d, tp), BF16),
           jax.ShapeDtypeStruct((8, 128), F32), jax.ShapeDtypeStruct((8, d), F32)],
        scratch_shapes=[pltpu.VMEM((3, TM, d), BF16),
                        pltpu.VMEM((HALO_A + TM, d), F32), pltpu.VMEM((HALO_B + TM, d), F32),
                        pltpu.VMEM((shl, d), F32), pltpu.VMEM((TM, d), F32),
                        pltpu.VMEM((RB, d), F32), pltpu.VMEM((8, d), F32)],
        compiler_params=_params(("arbitrary",)),
    )(s_pad, proj, target, w3, wa, wb, conv_a_b, ln_g, ln_b, b_a_out, final_g, norm_g)


def _mix_bwd(ds2, proj, ca, cb, ya, yb, w3, wa, wb, ln_g, ln_b):
    tp, d = ds2.shape
    nt = tp // TM
    nrb = TM // RB
    shl = TM + SHIFT_ROWS
    nt_dims = (((1,), (1,)), ((), ()))

    def body(ds2_ref, proj_ref, ca_ref, cb_ref, ya_ref, yb_ref, w3_ref, wa_ref, wb_ref, lng_ref, lnb_ref,
             dproj_ref, d3_ref, sm_ref, ext_d, ext_e, sh, dm_s, dpa_s, dpb_s, dua0_s, acc):
        step = pl.program_id(0)

        def split(k, rows):
            return proj_ref[rows, k * d:(k + 1) * d].astype(F32)

        def put(k, rows, val):
            dproj_ref[rows, k * d:(k + 1) * d] = val.astype(BF16)

        def accum(row, val):
            acc[row] += val.reshape(RB // 8, 8, d).sum(axis=0)

        @pl.when(step == 0)
        def _():
            ext_d[TM:TM + HALO_A, :] = jnp.zeros((HALO_A, d), F32)
            ext_e[TM:TM + HALO_B, :] = jnp.zeros((HALO_B, d), F32)
            acc[...] = jnp.zeros_like(acc)

        d3_ref[2] = ds2_ref[...].astype(BF16)
        dm_s[...] = lax.dot_general(d3_ref[2], w3_ref[2], nt_dims, preferred_element_type=F32)

        def gates(rb, carry):
            rows = _rows(rb)
            dm = dm_s[rows, :]
            sa = _sigmoid(split(7, rows))
            sb = _sigmoid(split(8, rows))
            ya_v = ya_ref[rows, :]
            yb_v = yb_ref[rows, :]
            put(7, rows, dm * ya_v * sa * (1.0 - sa))
            put(8, rows, dm * yb_v * sb * (1.0 - sb))
            dya = dm * sa
            accum(ROW_DBAO, dya)
            d3_ref[0, rows, :] = dya.astype(BF16)
            d3_ref[1, rows, :] = (dm * sb).astype(BF16)
            return carry
        lax.fori_loop(0, nrb, gates, 0)

        dpa_s[...] = lax.dot_general(d3_ref[0], w3_ref[0], nt_dims, preferred_element_type=F32)
        dpb_s[...] = lax.dot_general(d3_ref[1], w3_ref[1], nt_dims, preferred_element_type=F32)

        def branches(rb, carry):
            rows = _rows(rb)
            ca_v = ca_ref[rows, :]
            mu = jnp.mean(ca_v, axis=-1, keepdims=True)
            xc = ca_v - mu
            rstd = lax.rsqrt(jnp.mean(xc * xc, axis=-1, keepdims=True) + EPS)
            xhat = xc * rstd
            ln = xhat * lng_ref[...] + lnb_ref[...]
            sl = _sigmoid(ln)
            ua = ln * sl
            a_z = split(2, rows)
            sz = _sigmoid(a_z)
            dpa = dpa_s[rows, :]
            put(2, rows, dpa * ua * (sz * (1.0 + a_z * (1.0 - sz))))
            dln = dpa * (a_z * sz) * (sl * (1.0 + ln * (1.0 - sl)))
            accum(ROW_DLNG, dln * xhat)
            accum(ROW_DLNB, dln)
            dxh = dln * lng_ref[...]
            dca = rstd * (dxh - jnp.mean(dxh, axis=-1, keepdims=True)
                          - xhat * jnp.mean(dxh * xhat, axis=-1, keepdims=True))
            accum(ROW_DCAB, dca)
            ext_d[rows, :] = dca
            dua0_s[rows, :] = jnp.zeros((RB, d), F32)
            dm_s[rows, :] = split(0, rows) * _sigmoid(split(1, rows))
            b_z = split(6, rows)
            szb = _sigmoid(b_z)
            dpb = dpb_s[rows, :]
            b_b = split(3, rows)
            cb_v = cb_ref[rows, :]
            put(6, rows, dpb * (b_b * cb_v) * (szb * (1.0 + b_z * (1.0 - szb))))
            dub = dpb * (b_z * szb)
            put(3, rows, dub * cb_v)
            ext_e[rows, :] = dub * b_b
            return carry
        lax.fori_loop(0, nrb, branches, 0)

        for r, taps in _conv_a_taps(0, CONV_A - 1):
            if r == 0:
                src = ext_d
            else:
                sh[...] = ext_d[r:r + shl, :]
                src = sh

            def conv_t(rb, carry, src=src, taps=taps):
                rows = _rows(rb)
                ua0 = dm_s[rows, :]
                dua0 = dua0_s[rows, :]
                for q, lag in taps:
                    k = CONV_A - 1 - lag
                    slab = src[pl.ds(pl.multiple_of(rb * RB + 8 * q, 8), RB), :]
                    dua0 = dua0 + slab * wa_ref[k:k + 1, :]
                    accum(ROW_DWA + k, slab * ua0)
                dua0_s[rows, :] = dua0
                return carry
            lax.fori_loop(0, nrb, conv_t, 0)
        ext_d[TM:TM + HALO_A, :] = ext_d[0:HALO_A, :]

        dpb_s[...] = ext_e[0:TM, :] * wb_ref[CONV_B - 1:CONV_B, :]
        for lag in range(CONV_B):
            k = CONV_B - 1 - lag
            if lag > 0:
                sh[0:TM, :] = ext_e[lag:lag + TM, :]
                dpb_s[...] += sh[0:TM, :] * wb_ref[k:k + 1, :]
            src = ext_e if lag == 0 else sh

            def conv_b_w(rb, carry, src=src, k=k):
                rows = _rows(rb)
                accum(ROW_DWB + k, src[rows, :] * (split(4, rows) * split(5, rows)))
                return carry
            lax.fori_loop(0, nrb, conv_b_w, 0)
        ext_e[TM:TM + HALO_B, :] = ext_e[0:HALO_B, :]

        def inputs(rb, carry):
            rows = _rows(rb)
            dua0 = dua0_s[rows, :]
            a_val = split(0, rows)
            sg = _sigmoid(split(1, rows))
            put(0, rows, dua0 * sg)
            put(1, rows, dua0 * a_val * sg * (1.0 - sg))
            dcbin = dpb_s[rows, :]
            put(4, rows, dcbin * split(5, rows))
            put(5, rows, dcbin * split(4, rows))
            return carry
        lax.fori_loop(0, nrb, inputs, 0)

        @pl.when(step == nt - 1)
        def _():
            for row in range(SM_ROWS):
                sm_ref[row:row + 1, :] = jnp.sum(acc[row], axis=0, keepdims=True)

    rev = lambda i: (nt - 1 - i, 0)
    row_f32 = pl.BlockSpec((TM, d), rev)
    const = lambda shape: pl.BlockSpec(shape, lambda i: (0,) * len(shape))
    return pl.pallas_call(
        body, name="b1_mix", grid=(nt,),
        in_specs=[row_f32, pl.BlockSpec((TM, N_SPLIT * d), rev), row_f32, row_f32, row_f32, row_f32,
                  const((3, d, d)), const(wa.shape), const(wb.shape), const((1, d)), const((1, d))],
        out_specs=[pl.BlockSpec((TM, N_SPLIT * d), rev),
                   pl.BlockSpec((3, TM, d), lambda i: (0, nt - 1 - i, 0)),
                   const((SM_ROWS, d))],
        out_shape=[jax.ShapeDtypeStruct((tp, N_SPLIT * d), BF16), jax.ShapeDtypeStruct((3, tp, d), BF16),
                   jax.ShapeDtypeStruct((SM_ROWS, d), F32)],
        scratch_shapes=[pltpu.VMEM((TM + HALO_A, d), F32), pltpu.VMEM((TM + HALO_B, d), F32),
                        pltpu.VMEM((shl, d), F32), pltpu.VMEM((TM, d), F32), pltpu.VMEM((TM, d), F32),
                        pltpu.VMEM((TM, d), F32), pltpu.VMEM((TM, d), F32),
                        pltpu.VMEM((SM_ROWS, 8, d), F32)],
        compiler_params=_params(("arbitrary",)),
    )(ds2, proj, ca, cb, ya, yb, w3, wa, wb, ln_g, ln_b)


def kernel(x, meta_tokens, norm_g, w_in, conv_a_w, conv_a_b, ln_a_g, ln_a_b, w_a_out, b_a_out, conv_b_w, w_b_out, w_out, final_g, loss_target, m_meta_tokens, m_norm_g, m_w_in, m_conv_a_w, m_conv_a_b, m_ln_a_g, m_ln_a_b, m_w_a_out, m_b_a_out, m_conv_b_w, m_w_b_out, m_w_out, m_final_g, v_meta_tokens, v_norm_g, v_w_in, v_conv_a_w, v_conv_a_b, v_ln_a_g, v_ln_a_b, v_w_a_out, v_b_a_out, v_conv_b_w, v_w_b_out, v_w_out, v_final_g):
    seq, d = x.shape[1], x.shape[2]
    dc = meta_tokens.shape[1]
    sw = w_in.shape[2]
    rsh = w_a_out.shape[1]
    xi, yi, ci = _mesh_pos()
    me = 2 * xi + yi
    pos = jnp.stack([ci, me]).astype(jnp.int32)

    conv_rows = HALO_A + HALO_B + 8
    convs = jnp.concatenate([
        jnp.pad(conv_a_w[0], ((0, HALO_A - CONV_A), (0, 0))),
        jnp.pad(conv_b_w[0], ((0, HALO_B - CONV_B), (0, 0))), jnp.zeros((8, dc), F32)], axis=0)[None]
    w3_own = jnp.stack([w_a_out[0], w_b_out[0], w_out[0]])
    wg_in, metag = _all_gather([_place_own(w_in, pos, BF16, "place_in"),
                                _place_own(meta_tokens[None], pos, F32, "place_meta")])
    meta_full = jnp.transpose(metag[0], (1, 0, 2)).reshape(N_META, N_CHIPS * dc)
    fg2 = final_g.reshape(1, d)

    first_tile = jnp.concatenate([jnp.zeros((TM - N_META, d), F32), meta_full], axis=0)
    s_pad = jnp.concatenate([first_tile, x[0]], axis=0)

    proj, (wg3, convg) = _proj_fwd(s_pad, norm_g, wg_in, [_place_own(w3_own, pos, BF16, "place_sq"),
                                                          _place_own(convs, pos, F32, "place_conv")])
    w3 = wg3.reshape(3, N_CHIPS * rsh, d)
    convg = jnp.transpose(convg[0], (1, 0, 2)).reshape(conv_rows, N_CHIPS * dc)
    wa_full = convg[0:HALO_A]
    wb_full = convg[HALO_A:HALO_A + HALO_B]
    ca, cb, ya, yb, abm_t, ds2, h_t, loss8, dfg8 = _mix_fwd(
        s_pad, proj, loss_target[0], w3, wa_full, wb_full, conv_a_b, ln_a_g, ln_a_b, b_a_out, fg2, norm_g)
    dproj, d3, sm = _mix_bwd(ds2, proj, ca, cb, ya, yb, w3, wa_full, wb_full, ln_a_g, ln_a_b)
    g_sq = _dw_square(abm_t, d3).reshape(3, N_CHIPS, rsh, d)
    (r_sq,) = _sibling_exchange([g_sq], "rs_sibling_sq")
    p32_sq, pbf_sq = _add_sibling(g_sq, r_sq, pos, "rs_add_sq")
    g_in, l_sq = _dw_in(h_t, dproj, N_CHIPS, pbf_sq)
    (r_in,) = _sibling_exchange([g_in], "rs_sibling_in")
    p32_in, pbf_in = _add_sibling(g_in, r_in, pos, "rs_add_in")
    ds, dng8, l_in = _dh_bwd(dproj, wg_in, s_pad, ds2, norm_g, pbf_in)
    half_in = _sum_chips(p32_in, l_in, pos, "rs_sum_in")
    half_sq = _sum_chips(p32_sq, l_sq, pos, "rs_sum_sq")
    other_in, other_sq = _sibling_swap([half_in, half_sq])

    tail_row = lax.broadcasted_iota(jnp.int32, (8, d), 0)
    tail = jnp.where(tail_row == 0, dng8, jnp.where(tail_row == 1, dfg8,
                     jnp.where(tail_row == 2, loss8[0, 0], 0.0)))
    block = jnp.concatenate([sm, ds[TM - N_META:TM], tail], axis=0)
    red = _all_reduce_small(block)
    col = lax.dynamic_slice(red, (0, me * dc), (AR_ROWS, dc))
    g_small = {
        "meta_tokens": col[ROW_DMETA:ROW_DMETA + N_META],
        "norm_g": red[ROW_DNG:ROW_DNG + 1],
        "conv_a_w": col[ROW_DWA:ROW_DWA + CONV_A][None],
        "conv_a_b": red[ROW_DCAB:ROW_DCAB + 1],
        "ln_a_g": red[ROW_DLNG:ROW_DLNG + 1],
        "ln_a_b": red[ROW_DLNB:ROW_DLNB + 1],
        "b_a_out": red[ROW_DBAO:ROW_DBAO + 1],
        "conv_b_w": col[ROW_DWB:ROW_DWB + CONV_B][None],
        "final_g": red[ROW_DFG],
    }

    upd_in = _adam_halves([w_in], [m_w_in], [v_w_in], half_in, other_in, pos, "adam_in")
    upd_sq = _adam_halves([w_a_out, w_b_out, w_out], [m_w_a_out, m_w_b_out, m_w_out],
                          [v_w_a_out, v_w_b_out, v_w_out], half_sq, other_sq, pos, "adam_sq")
    small_w = {"meta_tokens": (meta_tokens, m_meta_tokens, v_meta_tokens), "norm_g": (norm_g, m_norm_g, v_norm_g),
               "conv_a_w": (conv_a_w, m_conv_a_w, v_conv_a_w), "conv_a_b": (conv_a_b, m_conv_a_b, v_conv_a_b),
               "ln_a_g": (ln_a_g, m_ln_a_g, v_ln_a_g), "ln_a_b": (ln_a_b, m_ln_a_b, v_ln_a_b),
               "b_a_out": (b_a_out, m_b_a_out, v_b_a_out), "conv_b_w": (conv_b_w, m_conv_b_w, v_conv_b_w),
               "final_g": (final_g, m_final_g, v_final_g)}
    names_small = list(small_w)
    as2d = lambda t: t.reshape(-1, t.shape[-1])
    upd_small = _adam_small([(as2d(small_w[k][0]), as2d(g_small[k]), as2d(small_w[k][1]), as2d(small_w[k][2]))
                             for k in names_small])

    grads, deltas, new_m, new_v = dict(g_small), {}, {}, {}
    for k, upd in zip(names_small, upd_small):
        deltas[k], new_m[k], new_v[k] = [t.reshape(small_w[k][0].shape) for t in upd]
    grads["w_in"], deltas["w_in"], new_m["w_in"], new_v["w_in"] = upd_in
    for idx, k in enumerate(["w_a_out", "w_b_out", "w_out"]):
        grads[k], deltas[k], new_m[k], new_v[k] = upd_sq[4 * idx:4 * idx + 4]

    loss = red[ROW_LOSS, 0]
    grad_x = ds[TM:][None]
    order = ["meta_tokens", "norm_g", "w_in", "conv_a_w", "conv_a_b", "ln_a_g", "ln_a_b", "w_a_out", "b_a_out",
             "conv_b_w", "w_b_out", "w_out", "final_g"]
    return (loss, grad_x, *[grads[k] for k in order], *[deltas[k] for k in order],
            *[new_m[k] for k in order], *[new_v[k] for k in order])
```

```python
import functools

import jax
import jax.numpy as jnp
from jax import lax
from jax.experimental import pallas as pl
from jax.experimental.pallas import tpu as pltpu

F32 = jnp.float32
BF16 = jnp.bfloat16
MESH = pl.DeviceIdType.MESH

EPS = 1e-6
N_META = 16
N_SPLIT = 9
CONV_A = 31
CONV_B = 3
HALO_A = 32
HALO_B = 8
SHIFT_ROWS = 24
TM = 256
RB = 64
N_ROW_TILES_BIG = 8
ROW_BLOCK = 256
N_CHIPS = 4
VMEM_LIMIT = 56 * 1024 * 1024

ADAM_LR = 0.001
ADAM_B1 = 0.9
ADAM_B2 = 0.999
ADAM_EPS = 1e-08
ADAM_WD = 0.01
ADAM_STEP = 10

ROW_DWA = 0
ROW_DWB = 32
ROW_DCAB = 40
ROW_DLNG = 41
ROW_DLNB = 42
ROW_DBAO = 43
SM_ROWS = 48
ROW_DMETA = 48
ROW_DNG = 64
ROW_DFG = 65
ROW_LOSS = 66
AR_ROWS = 72


def _sigmoid(v):
    return 0.5 * jnp.tanh(0.5 * v) + 0.5


def _params(sem, **kw):
    return pltpu.CompilerParams(dimension_semantics=sem, vmem_limit_bytes=VMEM_LIMIT, **kw)


def _rows(rb):
    return pl.ds(pl.multiple_of(rb * RB, RB), RB)


def _mesh_pos():
    x, y, c = lax.axis_index("x"), lax.axis_index("y"), lax.axis_index("c")
    return x, y, c


def _half(ref, j, c):
    h = ref.shape[2] // 2
    return ref.at[:, j, pl.ds(c * h, h), :]


def _place_own(shard, pos, dtype, name):
    s, r, c = shard.shape
    rb = ROW_BLOCK if r % ROW_BLOCK == 0 else r

    def body(pos_ref, x_ref, o_ref):
        o_ref[...] = x_ref[...].astype(dtype)

    return pl.pallas_call(
        body, name=name,
        grid_spec=pltpu.PrefetchScalarGridSpec(
            num_scalar_prefetch=1, grid=(s, r // rb),
            in_specs=[pl.BlockSpec((None, rb, c), lambda si, b, pos_ref: (si, b, 0))],
            out_specs=pl.BlockSpec((None, None, rb, c), lambda si, b, pos_ref: (si, pos_ref[1], b, 0))),
        out_shape=jax.ShapeDtypeStruct((s, N_CHIPS, r, c), dtype),
        compiler_params=_params(("arbitrary",) * 2),
    )(pos, shard)


def _all_gather(bufs):
    n = len(bufs)

    def body(*refs):
        outs = refs[n:2 * n]
        send_sems, recv_sems = refs[2 * n:]
        x, y, c = _mesh_pos()
        me = 2 * x + y
        sibling = (x, y, 1 - c)
        chips = [(1 - x, y), (x, 1 - y), (1 - x, 1 - y)]

        def remote(a, k, piece_src, piece_dst, to):
            return pltpu.make_async_remote_copy(
                src_ref=piece_src, dst_ref=piece_dst, send_sem=send_sems.at[6 * a + k],
                recv_sem=recv_sems.at[6 * a + k], device_id=to, device_id_type=MESH)

        sends = []
        for a in range(n):
            mine = _half(outs[a], me, c)
            for k, (px, py) in enumerate(chips):
                sends.append(remote(a, k, mine, mine, (px, py, c)))
        for cp in sends:
            cp.start()
        for a in range(n):
            for k, (px, py) in enumerate(chips):
                piece = _half(outs[a], 2 * px + py, c)
                remote(a, k, piece, piece, (px, py, c)).wait_recv()
                fwd = remote(a, 3 + k, piece, piece, sibling)
                fwd.start()
                sends.append(fwd)
        for a in range(n):
            for k, (px, py) in enumerate(chips):
                piece = _half(outs[a], 2 * px + py, 1 - c)
                remote(a, 3 + k, piece, piece, sibling).wait_recv()
        for cp in sends:
            cp.wait_send()

    any_spec = pl.BlockSpec(memory_space=pl.ANY)
    return pl.pallas_call(
        body, name="ag_weights",
        in_specs=[any_spec] * n, out_specs=[any_spec] * n,
        out_shape=[jax.ShapeDtypeStruct(b.shape, b.dtype) for b in bufs],
        input_output_aliases={a: a for a in range(n)},
        scratch_shapes=[pltpu.SemaphoreType.DMA((6 * n,)), pltpu.SemaphoreType.DMA((6 * n,))],
    )(*bufs)


def _sibling_exchange(grads, name):
    n = len(grads)

    def body(*refs):
        ins, outs = refs[:n], refs[n:2 * n]
        send_sems, recv_sems = refs[2 * n:]
        x, y, c = _mesh_pos()
        copies = []
        for a in range(n):
            h = ins[a].shape[2] // 2
            copies.append(pltpu.make_async_remote_copy(
                src_ref=ins[a].at[:, :, pl.ds((1 - c) * h, h), :], dst_ref=outs[a],
                send_sem=send_sems.at[a], recv_sem=recv_sems.at[a],
                device_id=(x, y, 1 - c), device_id_type=MESH))
        for cp in copies:
            cp.start()
        for cp in copies:
            cp.wait()

    any_spec = pl.BlockSpec(memory_space=pl.ANY)
    return pl.pallas_call(
        body, name=name,
        in_specs=[any_spec] * n, out_specs=[any_spec] * n,
        out_shape=[jax.ShapeDtypeStruct(g.shape[:2] + (g.shape[2] // 2, g.shape[3]), g.dtype) for g in grads],
        scratch_shapes=[pltpu.SemaphoreType.DMA((n,)), pltpu.SemaphoreType.DMA((n,))],
    )(*grads)


class _Exchange:
    def __init__(self, sends, recvs):
        self.sends, self.recvs = sends, recvs

    def start(self):
        for cp in self.sends:
            cp.start()

    def finish(self):
        for cp in self.recvs:
            cp.wait_recv()
        for cp in self.sends:
            cp.wait_send()


def _chip_exchange(part_ref, land_ref, send_sems, recv_sems):
    x, y, c = _mesh_pos()
    me = 2 * x + y
    sends, recvs = [], []
    for k, (px, py) in enumerate([(1 - x, y), (x, 1 - y), (1 - x, 1 - y)]):
        sems = dict(send_sem=send_sems.at[k], recv_sem=recv_sems.at[k], device_id=(px, py, c), device_id_type=MESH)
        sends.append(pltpu.make_async_remote_copy(
            src_ref=part_ref.at[:, 2 * px + py], dst_ref=land_ref.at[:, me], **sems))
        landed = land_ref.at[:, 2 * px + py]
        recvs.append(pltpu.make_async_remote_copy(src_ref=landed, dst_ref=landed, **sems))
    return _Exchange(sends, recvs)


def _sibling_swap(halves):
    n = len(halves)

    def body(*refs):
        ins, outs = refs[:n], refs[n:2 * n]
        send_sems, recv_sems = refs[2 * n:]
        x, y, c = _mesh_pos()
        copies = [pltpu.make_async_remote_copy(
            src_ref=ins[a], dst_ref=outs[a], send_sem=send_sems.at[a], recv_sem=recv_sems.at[a],
            device_id=(x, y, 1 - c), device_id_type=MESH) for a in range(n)]
        for cp in copies:
            cp.start()
        for cp in copies:
            cp.wait()

    any_spec = pl.BlockSpec(memory_space=pl.ANY)
    return pl.pallas_call(
        body, name="rs_swap",
        in_specs=[any_spec] * n, out_specs=[any_spec] * n,
        out_shape=[jax.ShapeDtypeStruct(h.shape, h.dtype) for h in halves],
        scratch_shapes=[pltpu.SemaphoreType.DMA((n,)), pltpu.SemaphoreType.DMA((n,))],
    )(*halves)


def _all_reduce_small(block):
    rows, d = block.shape

    def body(x_ref, out_ref, sib_ref, part_ref, peers_ref, send_sems, recv_sems):
        x, y, c = _mesh_pos()
        me = 2 * x + y
        chips = [(1 - x, y), (x, 1 - y), (1 - x, 1 - y)]
        swap = pltpu.make_async_remote_copy(
            src_ref=x_ref, dst_ref=sib_ref, send_sem=send_sems.at[0], recv_sem=recv_sems.at[0],
            device_id=(x, y, 1 - c), device_id_type=MESH)
        swap.start()
        swap.wait()
        part_ref[...] = x_ref[...] + sib_ref[...]
        peers_ref[me] = part_ref[...]
        sends = [pltpu.make_async_remote_copy(
            src_ref=part_ref, dst_ref=peers_ref.at[me], send_sem=send_sems.at[1 + k], recv_sem=recv_sems.at[1 + k],
            device_id=(px, py, c), device_id_type=MESH) for k, (px, py) in enumerate(chips)]
        for cp in sends:
            cp.start()
        for k, (px, py) in enumerate(chips):
            landed = peers_ref.at[2 * px + py]
            pltpu.make_async_remote_copy(
                src_ref=landed, dst_ref=landed, send_sem=send_sems.at[1 + k], recv_sem=recv_sems.at[1 + k],
                device_id=(px, py, c), device_id_type=MESH).wait_recv()
        for cp in sends:
            cp.wait_send()
        out_ref[...] = ((peers_ref[0] + peers_ref[1]) + peers_ref[2]) + peers_ref[3]

    vm = pl.BlockSpec(memory_space=pltpu.VMEM)
    return pl.pallas_call(
        body, name="ar_small",
        in_specs=[vm], out_specs=vm,
        out_shape=jax.ShapeDtypeStruct((rows, d), F32),
        scratch_shapes=[pltpu.VMEM((rows, d), F32), pltpu.VMEM((rows, d), F32),
                        pltpu.VMEM((N_CHIPS, rows, d), F32),
                        pltpu.SemaphoreType.DMA((4,)), pltpu.SemaphoreType.DMA((4,))],
    )(block)


def _add_sibling(grad, recv, pos, name):
    s, nch, r, c = grad.shape
    h = r // 2
    hb = min(h, ROW_BLOCK)

    def body(pos_ref, g_ref, r_ref, p32_ref, pbf_ref):
        p = g_ref[...] + r_ref[...]
        p32_ref[...] = p
        pbf_ref[...] = p.astype(BF16)

    nb = h // hb
    spec_g = pl.BlockSpec((None, None, hb, c), lambda si, j, b, pos_ref: (si, j, pos_ref[0] * nb + b, 0))
    spec_h = pl.BlockSpec((None, None, hb, c), lambda si, j, b, pos_ref: (si, j, b, 0))
    return pl.pallas_call(
        body, name=name,
        grid_spec=pltpu.PrefetchScalarGridSpec(
            num_scalar_prefetch=1, grid=(s, nch, nb), in_specs=[spec_g, spec_h], out_specs=[spec_h, spec_h]),
        out_shape=[jax.ShapeDtypeStruct((s, nch, h, c), F32), jax.ShapeDtypeStruct((s, nch, h, c), BF16)],
        compiler_params=_params(("arbitrary",) * 3),
    )(pos, grad, recv)


def _sum_chips(p32, landed, pos, name):
    s, nch, h, c = p32.shape
    hb = min(h, ROW_BLOCK)

    def body(pos_ref, p_ref, l1_ref, l2_ref, l3_ref, out_ref):
        out_ref[...] = ((p_ref[...] + l1_ref[...].astype(F32)) + l2_ref[...].astype(F32)) + l3_ref[...].astype(F32)

    def slot(k):
        return pl.BlockSpec((None, None, hb, c), lambda si, b, pos_ref: (si, (pos_ref[1] + k) % N_CHIPS, b, 0))

    return pl.pallas_call(
        body, name=name,
        grid_spec=pltpu.PrefetchScalarGridSpec(
            num_scalar_prefetch=1, grid=(s, h // hb),
            in_specs=[slot(0), slot(1), slot(2), slot(3)],
            out_specs=pl.BlockSpec((None, hb, c), lambda si, b, pos_ref: (si, b, 0))),
        out_shape=jax.ShapeDtypeStruct((s, h, c), F32),
        compiler_params=_params(("arbitrary",) * 2),
    )(pos, p32, landed, landed, landed)


def _adamw(w, g, m, v):
    m = ADAM_B1 * m + (1.0 - ADAM_B1) * g
    v = ADAM_B2 * v + (1.0 - ADAM_B2) * (g * g)
    m_hat = m / (1.0 - ADAM_B1 ** ADAM_STEP)
    v_hat = v / (1.0 - ADAM_B2 ** ADAM_STEP)
    delta = -ADAM_LR * (m_hat / (jnp.sqrt(v_hat) + ADAM_EPS) + ADAM_WD * w)
    return delta, m, v


def _adam_halves(ws, ms, vs, g_own, g_recv, pos, name):
    n = len(ws)
    _, r, c = ws[0].shape
    h = r // 2
    rb = min(h, ROW_BLOCK)
    nb = h // rb

    def body(pos_ref, *refs):
        w_refs, m_refs, v_refs = refs[:n], refs[n:2 * n], refs[2 * n:3 * n]
        go_ref, gr_ref = refs[3 * n:3 * n + 2]
        outs = refs[3 * n + 2:]
        mine = pl.program_id(0) == pos_ref[0]
        for a in range(n):
            g = jnp.where(mine, go_ref[a], gr_ref[a])
            delta, m, v = _adamw(w_refs[a][...], g, m_refs[a][...], v_refs[a][...])
            outs[4 * a][...], outs[4 * a + 1][...], outs[4 * a + 2][...], outs[4 * a + 3][...] = g, delta, m, v

    spec_w = pl.BlockSpec((None, rb, c), lambda hf, b, pos_ref: (0, hf * nb + b, 0))
    spec_g = pl.BlockSpec((n, rb, c), lambda hf, b, pos_ref: (0, b, 0))
    return pl.pallas_call(
        body, name=name,
        grid_spec=pltpu.PrefetchScalarGridSpec(
            num_scalar_prefetch=1, grid=(2, nb), in_specs=[spec_w] * (3 * n) + [spec_g] * 2,
            out_specs=[spec_w] * (4 * n)),
        out_shape=[jax.ShapeDtypeStruct((1, r, c), F32)] * (4 * n),
        compiler_params=_params(("arbitrary",) * 2),
    )(pos, *ws, *ms, *vs, g_own, g_recv)


def _adam_small(items):
    n = len(items)

    def body(*refs):
        ins, outs = refs[:4 * n], refs[4 * n:]
        for a in range(n):
            w_ref, g_ref, m_ref, v_ref = ins[4 * a:4 * a + 4]
            d, m, v = _adamw(w_ref[...], g_ref[...], m_ref[...], v_ref[...])
            outs[3 * a][...] = d
            outs[3 * a + 1][...] = m
            outs[3 * a + 2][...] = v

    vm = pl.BlockSpec(memory_space=pltpu.VMEM)
    flat = [t for it in items for t in it]
    outs = pl.pallas_call(
        body, name="adam_small", in_specs=[vm] * (4 * n), out_specs=[vm] * (3 * n),
        out_shape=[jax.ShapeDtypeStruct(it[0].shape, F32) for it in items for _ in range(3)],
    )(*flat)
    return [tuple(outs[3 * a:3 * a + 3]) for a in range(n)]


def _shard_of_step(js, me):
    flip = jnp.where(js == 1, 2, jnp.where(js == 2, 1, jnp.where(js == 3, 3, 0)))
    return lax.bitwise_xor(me, flip)


def _proj_fwd(s_pad, norm_g, bufs, pos):
    tp, d = s_pad.shape
    _, nsh, _, sw = bufs[0].shape
    tmb = tp // N_ROW_TILES_BIG
    n = len(bufs)

    def body(pos_ref, s_ref, g_ref, *refs):
        proj_ref = refs[n]
        gbufs = refs[n + 1:2 * n + 1]
        wbuf, wsem, send_sems, recv_sems = refs[2 * n + 1:]
        x, y, c = _mesh_pos()
        me = 2 * x + y
        sibling = (x, y, 1 - c)
        chips = [(1 - x, y), (x, 1 - y), (1 - x, 1 - y)]
        js, i = pl.program_id(0), pl.program_id(1)

        def remote(a, k, piece, to):
            return pltpu.make_async_remote_copy(
                src_ref=piece, dst_ref=piece, send_sem=send_sems.at[6 * a + k],
                recv_sem=recv_sems.at[6 * a + k], device_id=to, device_id_type=MESH)

        def fetch(chip):
            cp = pltpu.make_async_copy(gbufs[0].at[0, chip], wbuf, wsem)
            cp.start()
            cp.wait()

        def take(a, k):
            px, py = chips[k]
            remote(a, k, _half(gbufs[a], 2 * px + py, c), (px, py, c)).wait_recv()
            remote(a, 3 + k, _half(gbufs[a], 2 * px + py, c), sibling).start()
            remote(a, 3 + k, _half(gbufs[a], 2 * px + py, 1 - c), sibling).wait_recv()

        @pl.when((js == 0) & (i == 0))
        def _():
            for a in range(n):
                for k, (px, py) in enumerate(chips):
                    remote(a, k, _half(gbufs[a], me, c), (px, py, c)).start()
            fetch(me)

        for k, (px, py) in enumerate(chips):
            @pl.when((js == k + 1) & (i == 0))
            def _(k=k, px=px, py=py):
                take(0, k)
                fetch(2 * px + py)

        s = s_ref[...]
        r = lax.rsqrt(jnp.mean(s * s, axis=-1, keepdims=True) + EPS)
        h = (s * r * g_ref[...]).astype(BF16)
        proj_ref[...] = jnp.dot(h, wbuf[...], preferred_element_type=F32).astype(BF16)

        @pl.when((js == nsh - 1) & (i == N_ROW_TILES_BIG - 1))
        def _():
            for a in range(1, n):
                for k in range(len(chips)):
                    take(a, k)
            for a in range(n):
                for k, (px, py) in enumerate(chips):
                    remote(a, k, _half(gbufs[a], me, c), (px, py, c)).wait_send()
                    remote(a, 3 + k, _half(gbufs[a], 2 * px + py, c), sibling).wait_send()

    any_spec = pl.BlockSpec(memory_space=pl.ANY)
    outs = pl.pallas_call(
        body, name="f1_proj",
        grid_spec=pltpu.PrefetchScalarGridSpec(
            num_scalar_prefetch=1, grid=(nsh, N_ROW_TILES_BIG),
            in_specs=[pl.BlockSpec((tmb, d), lambda js, i, pos_ref: (i, 0)),
                      pl.BlockSpec((1, d), lambda js, i, pos_ref: (0, 0))] + [any_spec] * n,
            out_specs=[pl.BlockSpec((tmb, sw), lambda js, i, pos_ref: (i, _shard_of_step(js, pos_ref[1])))]
            + [any_spec] * n,
            scratch_shapes=[pltpu.VMEM((d, sw), BF16), pltpu.SemaphoreType.DMA,
                            pltpu.SemaphoreType.DMA((6 * n,)), pltpu.SemaphoreType.DMA((6 * n,))]),
        out_shape=[jax.ShapeDtypeStruct((tp, nsh * sw), BF16)]
        + [jax.ShapeDtypeStruct(b.shape, b.dtype) for b in bufs],
        input_output_aliases={3 + a: 1 + a for a in range(n)},
        compiler_params=_params(("arbitrary", "arbitrary")),
    )(pos, s_pad, norm_g, *bufs)
    return outs[0], outs[1:]


def _dh_bwd(dproj, wg_in, s_pad, ds2, norm_g, part):
    tp, d = s_pad.shape
    _, nsh, _, sw = wg_in.shape
    tmb = tp // N_ROW_TILES_BIG

    def body(dp_ref, w_ref, s_ref, ds2_ref, g_ref, part_ref, ds_ref, dng_ref, land_ref, acc, gacc,
             send_sems, recv_sems):
        exchange = _chip_exchange(part_ref, land_ref, send_sems, recv_sems)
        i, j = pl.program_id(0), pl.program_id(1)

        @pl.when((i == 0) & (j == 0))
        def _():
            exchange.start()
            gacc[...] = jnp.zeros_like(gacc)

        part = lax.dot_general(dp_ref[...], w_ref[...], (((1,), (1,)), ((), ())), preferred_element_type=F32)

        @pl.when(j == 0)
        def _():
            acc[...] = part

        @pl.when(j > 0)
        def _():
            acc[...] += part

        @pl.when(j == nsh - 1)
        def _():
            dh = acc[...]
            s = s_ref[...]
            r = lax.rsqrt(jnp.mean(s * s, axis=-1, keepdims=True) + EPS)
            gacc[...] += (dh * s * r).reshape(tmb // 8, 8, d).sum(axis=0)
            t = dh * g_ref[...]
            ds_ref[...] = ds2_ref[...] + r * t - s * (r * r * r) * jnp.mean(t * s, axis=-1, keepdims=True)

        @pl.when((i == N_ROW_TILES_BIG - 1) & (j == nsh - 1))
        def _():
            dng_ref[...] = jnp.broadcast_to(jnp.sum(gacc[...], axis=0, keepdims=True), (8, d))
            exchange.finish()

    any_spec = pl.BlockSpec(memory_space=pl.ANY)
    return pl.pallas_call(
        body, name="b2_dh", grid=(N_ROW_TILES_BIG, nsh),
        in_specs=[pl.BlockSpec((tmb, sw), lambda i, j: (i, j)),
                  pl.BlockSpec((None, None, d, sw), lambda i, j: (0, j, 0, 0)),
                  pl.BlockSpec((tmb, d), lambda i, j: (i, 0)),
                  pl.BlockSpec((tmb, d), lambda i, j: (i, 0)),
                  pl.BlockSpec((1, d), lambda i, j: (0, 0)), any_spec],
        out_specs=[pl.BlockSpec((tmb, d), lambda i, j: (i, 0)),
                   pl.BlockSpec((8, d), lambda i, j: (0, 0)), any_spec],
        out_shape=[jax.ShapeDtypeStruct((tp, d), F32), jax.ShapeDtypeStruct((8, d), F32),
                   jax.ShapeDtypeStruct(part.shape, part.dtype)],
        scratch_shapes=[pltpu.VMEM((tmb, d), F32), pltpu.VMEM((8, d), F32),
                        pltpu.SemaphoreType.DMA((3,)), pltpu.SemaphoreType.DMA((3,))],
        compiler_params=_params(("arbitrary", "arbitrary")),
    )(dproj, wg_in, s_pad, ds2, norm_g, part)


def _col_block(width, cap):
    return max(b for b in range(128, cap + 1, 128) if width % b == 0)


def _dw_in(h_t, dproj, nsh, part):
    d, tp = h_t.shape
    sw = dproj.shape[1] // nsh
    cw = _col_block(sw, 768)
    ncol = sw // cw

    def body(h_ref, dp_ref, part_ref, out_ref, land_ref, send_sems, recv_sems):
        exchange = _chip_exchange(part_ref, land_ref, send_sems, recv_sems)
        j, n = pl.program_id(0), pl.program_id(1)

        @pl.when((j == 0) & (n == 0))
        def _():
            exchange.start()

        out_ref[...] = jnp.dot(h_ref[...], dp_ref[...], preferred_element_type=F32)

        @pl.when((j == nsh - 1) & (n == ncol - 1))
        def _():
            exchange.finish()

    any_spec = pl.BlockSpec(memory_space=pl.ANY)
    return pl.pallas_call(
        body, name="dw_in", grid=(nsh, ncol),
        in_specs=[pl.BlockSpec((d, tp), lambda j, n: (0, 0)),
                  pl.BlockSpec((tp, cw), lambda j, n: (0, j * ncol + n)), any_spec],
        out_specs=[pl.BlockSpec((None, None, d, cw), lambda j, n: (0, j, 0, n)), any_spec],
        out_shape=[jax.ShapeDtypeStruct((1, nsh, d, sw), F32), jax.ShapeDtypeStruct(part.shape, part.dtype)],
        scratch_shapes=[pltpu.SemaphoreType.DMA((3,)), pltpu.SemaphoreType.DMA((3,))],
        compiler_params=_params(("arbitrary",) * 2),
    )(h_t, dproj, part)


def _dw_square(lhs3_t, rhs3):
    n, d, tp = lhs3_t.shape
    cw = _col_block(d, 512)

    def body(a_ref, b_ref, out_ref):
        out_ref[...] = jnp.dot(a_ref[...], b_ref[...], preferred_element_type=F32)

    return pl.pallas_call(
        body, name="dw_square", grid=(n, d // cw),
        in_specs=[pl.BlockSpec((None, d, tp), lambda a, c: (a, 0, 0)),
                  pl.BlockSpec((None, tp, cw), lambda a, c: (a, 0, c))],
        out_specs=pl.BlockSpec((None, d, cw), lambda a, c: (a, 0, c)),
        out_shape=jax.ShapeDtypeStruct((n, d, d), F32),
        compiler_params=_params(("arbitrary",) * 2),
    )(lhs3_t, rhs3)


def _conv_a_taps(first_lag, last_lag):
    out = []
    for r in range(8):
        taps = [(q, 8 * q + r) for q in range(5) if first_lag <= 8 * q + r <= last_lag]
        if taps:
            out.append((r, taps))
    return out


def _mix_fwd(s_pad, proj, target, w3, wa, wb, conv_a_b, ln_g, ln_b, b_a_out, final_g, norm_g):
    tp, d = s_pad.shape
    nt = tp // TM
    nrb = TM // RB
    shl = TM + SHIFT_ROWS

    def body(s_ref, proj_ref, tgt_ref, w3_ref, wa_ref, wb_ref, cab_ref, lng_ref, lnb_ref, bao_ref, fg_ref, ng_ref,
             ca_ref, cb_ref, ya_ref, yb_ref, abmt_ref, ds2_ref, ht_ref, loss_ref, dfg_ref,
             abm_ref, ext_a, ext_b, sh, s2_s, lacc, gacc):
        i = pl.program_id(0)

        def split(k, rows):
            return proj_ref[rows, k * d:(k + 1) * d].astype(F32)

        s_in = s_ref[...]
        h = s_in * lax.rsqrt(jnp.mean(s_in * s_in, axis=-1, keepdims=True) + EPS) * ng_ref[...]
        ht_ref[...] = h.T.astype(BF16)

        @pl.when(i == 0)
        def _():
            ext_a[0:HALO_A, :] = jnp.zeros((HALO_A, d), F32)
            ext_b[0:HALO_B, :] = jnp.zeros((HALO_B, d), F32)
            lacc[...] = jnp.zeros_like(lacc)
            gacc[...] = jnp.zeros_like(gacc)

        def conv_in(rb, carry):
            rows = _rows(rb)
            ua0 = split(0, rows) * _sigmoid(split(1, rows))
            ext_a[pl.ds(pl.multiple_of(HALO_A + rb * RB, 8), RB), :] = ua0
            ext_b[pl.ds(pl.multiple_of(HALO_B + rb * RB, 8), RB), :] = split(4, rows) * split(5, rows)
            ca_ref[rows, :] = jnp.broadcast_to(cab_ref[...], (RB, d))
            return carry
        lax.fori_loop(0, nrb, conv_in, 0)

        for r, taps in _conv_a_taps(HALO_A - CONV_A + 1, HALO_A):
            if r == 0:
                src = ext_a
            else:
                sh[...] = ext_a[r:r + shl, :]
                src = sh

            def conv_acc(rb, carry, src=src, taps=taps):
                rows = _rows(rb)
                acc = ca_ref[rows, :]
                for q, lag in taps:
                    k = lag - (HALO_A - CONV_A + 1)
                    acc = acc + src[pl.ds(pl.multiple_of(rb * RB + 8 * q, 8), RB), :] * wa_ref[k:k + 1, :]
                ca_ref[rows, :] = acc
                return carry
            lax.fori_loop(0, nrb, conv_acc, 0)
        ext_a[0:HALO_A, :] = ext_a[TM:TM + HALO_A, :]

        cb_ref[...] = ext_b[HALO_B:HALO_B + TM, :] * wb_ref[2:3, :]
        for k in range(CONV_B - 1):
            off = HALO_B - CONV_B + 1 + k
            sh[0:TM, :] = ext_b[off:off + TM, :]
            cb_ref[...] += sh[0:TM, :] * wb_ref[k:k + 1, :]
        ext_b[0:HALO_B, :] = ext_b[TM:TM + HALO_B, :]

        def branches(rb, carry):
            rows = _rows(rb)
            ca = ca_ref[rows, :]
            mu = jnp.mean(ca, axis=-1, keepdims=True)
            xc = ca - mu
            rstd = lax.rsqrt(jnp.mean(xc * xc, axis=-1, keepdims=True) + EPS)
            ln = xc * rstd * lng_ref[...] + lnb_ref[...]
            ua = ln * _sigmoid(ln)
            a_z = split(2, rows)
            abm_ref[0, rows, :] = (ua * (a_z * _sigmoid(a_z))).astype(BF16)
            b_z = split(6, rows)
            ub = split(3, rows) * cb_ref[rows, :]
            abm_ref[1, rows, :] = (ub * (b_z * _sigmoid(b_z))).astype(BF16)
            return carry
        lax.fori_loop(0, nrb, branches, 0)

        ya_ref[...] = jnp.dot(abm_ref[0], w3_ref[0], preferred_element_type=F32) + bao_ref[...]
        yb_ref[...] = jnp.dot(abm_ref[1], w3_ref[1], preferred_element_type=F32)

        def merge(rb, carry):
            rows = _rows(rb)
            m = _sigmoid(split(7, rows)) * ya_ref[rows, :] + _sigmoid(split(8, rows)) * yb_ref[rows, :]
            abm_ref[2, rows, :] = m.astype(BF16)
            return carry
        lax.fori_loop(0, nrb, merge, 0)

        s2_s[...] = s_ref[...] + jnp.dot(abm_ref[2], w3_ref[2], preferred_element_type=F32)
        for k in range(3):
            abmt_ref[k] = abm_ref[k].astype(F32).T.astype(BF16)
        live = (i > 0).astype(F32)

        def head(rb, carry):
            rows = _rows(rb)
            s2 = s2_s[rows, :]
            r2 = lax.rsqrt(jnp.mean(s2 * s2, axis=-1, keepdims=True) + EPS)
            diff = (s2 * r2 * fg_ref[...] - tgt_ref[rows, :]) * live
            lacc[...] += diff * diff
            dy = diff * (1.0 / d)
            gacc[...] += (dy * s2 * r2).reshape(RB // 8, 8, d).sum(axis=0)
            t = dy * fg_ref[...]
            ds2_ref[rows, :] = r2 * t - s2 * (r2 * r2 * r2) * jnp.mean(t * s2, axis=-1, keepdims=True)
            return carry
        lax.fori_loop(0, nrb, head, 0)

        @pl.when(i == nt - 1)
        def _():
            loss_ref[...] = jnp.broadcast_to(0.5 * jnp.sum(lacc[...]) * (1.0 / d), (8, 128))
            dfg_ref[...] = jnp.broadcast_to(jnp.sum(gacc[...], axis=0, keepdims=True), (8, d))

    row_f32 = pl.BlockSpec((TM, d), lambda i: (i, 0))
    const = lambda shape: pl.BlockSpec(shape, lambda i: (0,) * len(shape))
    return pl.pallas_call(
        body, name="f2_mix", grid=(nt,),
        in_specs=[row_f32,
                  pl.BlockSpec((TM, N_SPLIT * d), lambda i: (i, 0)),
                  pl.BlockSpec((TM, d), lambda i: (jnp.maximum(i - 1, 0), 0)),
                  const((3, d, d)), const(wa.shape), const(wb.shape)] + [const((1, d))] * 6,
        out_specs=[row_f32, row_f32, row_f32, row_f32,
                   pl.BlockSpec((3, d, TM), lambda i: (0, 0, i)),
                   row_f32, pl.BlockSpec((d, TM), lambda i: (0, i)), const((8, 128)), const((8, d))],
        out_shape=[jax.ShapeDtypeStruct((tp, d), F32)] * 4
        + [jax.ShapeDtypeStruct((3, d, tp), BF16), jax.ShapeDtypeStruct((tp, d), F32),
           jax.ShapeDtypeStruct((d, tp), BF16),
           jax.ShapeDtypeStruct((8, 128), F32), jax.ShapeDtypeStruct((8, d), F32)],
        scratch_shapes=[pltpu.VMEM((3, TM, d), BF16),
                        pltpu.VMEM((HALO_A + TM, d), F32), pltpu.VMEM((HALO_B + TM, d), F32),
                        pltpu.VMEM((shl, d), F32), pltpu.VMEM((TM, d), F32),
                        pltpu.VMEM((RB, d), F32), pltpu.VMEM((8, d), F32)],
        compiler_params=_params(("arbitrary",)),
    )(s_pad, proj, target, w3, wa, wb, conv_a_b, ln_g, ln_b, b_a_out, final_g, norm_g)


def _mix_bwd(ds2, proj, ca, cb, ya, yb, w3, wa, wb, ln_g, ln_b):
    tp, d = ds2.shape
    nt = tp // TM
    nrb = TM // RB
    shl = TM + SHIFT_ROWS
    nt_dims = (((1,), (1,)), ((), ()))

    def body(ds2_ref, proj_ref, ca_ref, cb_ref, ya_ref, yb_ref, w3_ref, wa_ref, wb_ref, lng_ref, lnb_ref,
             dproj_ref, d3_ref, sm_ref, ext_d, ext_e, sh, dm_s, dpa_s, dpb_s, dua0_s, acc):
        step = pl.program_id(0)

        def split(k, rows):
            return proj_ref[rows, k * d:(k + 1) * d].astype(F32)

        def put(k, rows, val):
            dproj_ref[rows, k * d:(k + 1) * d] = val.astype(BF16)

        def accum(row, val):
            acc[row] += val.reshape(RB // 8, 8, d).sum(axis=0)

        @pl.when(step == 0)
        def _():
            ext_d[TM:TM + HALO_A, :] = jnp.zeros((HALO_A, d), F32)
            ext_e[TM:TM + HALO_B, :] = jnp.zeros((HALO_B, d), F32)
            acc[...] = jnp.zeros_like(acc)

        d3_ref[2] = ds2_ref[...].astype(BF16)
        dm_s[...] = lax.dot_general(d3_ref[2], w3_ref[2], nt_dims, preferred_element_type=F32)

        def gates(rb, carry):
            rows = _rows(rb)
            dm = dm_s[rows, :]
            sa = _sigmoid(split(7, rows))
            sb = _sigmoid(split(8, rows))
            ya_v = ya_ref[rows, :]
            yb_v = yb_ref[rows, :]
            put(7, rows, dm * ya_v * sa * (1.0 - sa))
            put(8, rows, dm * yb_v * sb * (1.0 - sb))
            dya = dm * sa
            accum(ROW_DBAO, dya)
            d3_ref[0, rows, :] = dya.astype(BF16)
            d3_ref[1, rows, :] = (dm * sb).astype(BF16)
            return carry
        lax.fori_loop(0, nrb, gates, 0)

        dpa_s[...] = lax.dot_general(d3_ref[0], w3_ref[0], nt_dims, preferred_element_type=F32)
        dpb_s[...] = lax.dot_general(d3_ref[1], w3_ref[1], nt_dims, preferred_element_type=F32)

        def branches(rb, carry):
            rows = _rows(rb)
            ca_v = ca_ref[rows, :]
            mu = jnp.mean(ca_v, axis=-1, keepdims=True)
            xc = ca_v - mu
            rstd = lax.rsqrt(jnp.mean(xc * xc, axis=-1, keepdims=True) + EPS)
            xhat = xc * rstd
            ln = xhat * lng_ref[...] + lnb_ref[...]
            sl = _sigmoid(ln)
            ua = ln * sl
            a_z = split(2, rows)
            sz = _sigmoid(a_z)
            dpa = dpa_s[rows, :]
            put(2, rows, dpa * ua * (sz * (1.0 + a_z * (1.0 - sz))))
            dln = dpa * (a_z * sz) * (sl * (1.0 + ln * (1.0 - sl)))
            accum(ROW_DLNG, dln * xhat)
            accum(ROW_DLNB, dln)
            dxh = dln * lng_ref[...]
            dca = rstd * (dxh - jnp.mean(dxh, axis=-1, keepdims=True)
                          - xhat * jnp.mean(dxh * xhat, axis=-1, keepdims=True))
            accum(ROW_DCAB, dca)
            ext_d[rows, :] = dca
            dua0_s[rows, :] = jnp.zeros((RB, d), F32)
            dm_s[rows, :] = split(0, rows) * _sigmoid(split(1, rows))
            b_z = split(6, rows)
            szb = _sigmoid(b_z)
            dpb = dpb_s[rows, :]
            b_b = split(3, rows)
            cb_v = cb_ref[rows, :]
            put(6, rows, dpb * (b_b * cb_v) * (szb * (1.0 + b_z * (1.0 - szb))))
            dub = dpb * (b_z * szb)
            put(3, rows, dub * cb_v)
            ext_e[rows, :] = dub * b_b
            return carry
        lax.fori_loop(0, nrb, branches, 0)

        for r, taps in _conv_a_taps(0, CONV_A - 1):
            if r == 0:
                src = ext_d
            else:
                sh[...] = ext_d[r:r + shl, :]
                src = sh

            def conv_t(rb, carry, src=src, taps=taps):
                rows = _rows(rb)
                ua0 = dm_s[rows, :]
                dua0 = dua0_s[rows, :]
                for q, lag in taps:
                    k = CONV_A - 1 - lag
                    slab = src[pl.ds(pl.multiple_of(rb * RB + 8 * q, 8), RB), :]
                    dua0 = dua0 + slab * wa_ref[k:k + 1, :]
                    accum(ROW_DWA + k, slab * ua0)
                dua0_s[rows, :] = dua0
                return carry
            lax.fori_loop(0, nrb, conv_t, 0)
        ext_d[TM:TM + HALO_A, :] = ext_d[0:HALO_A, :]

        dpb_s[...] = ext_e[0:TM, :] * wb_ref[CONV_B - 1:CONV_B, :]
        for lag in range(CONV_B):
            k = CONV_B - 1 - lag
            if lag > 0:
                sh[0:TM, :] = ext_e[lag:lag + TM, :]
                dpb_s[...] += sh[0:TM, :] * wb_ref[k:k + 1, :]
            src = ext_e if lag == 0 else sh

            def conv_b_w(rb, carry, src=src, k=k):
                rows = _rows(rb)
                accum(ROW_DWB + k, src[rows, :] * (split(4, rows) * split(5, rows)))
                return carry
            lax.fori_loop(0, nrb, conv_b_w, 0)
        ext_e[TM:TM + HALO_B, :] = ext_e[0:HALO_B, :]

        def inputs(rb, carry):
            rows = _rows(rb)
            dua0 = dua0_s[rows, :]
            a_val = split(0, rows)
            sg = _sigmoid(split(1, rows))
            put(0, rows, dua0 * sg)
            put(1, rows, dua0 * a_val * sg * (1.0 - sg))
            dcbin = dpb_s[rows, :]
            put(4, rows, dcbin * split(5, rows))
            put(5, rows, dcbin * split(4, rows))
            return carry
        lax.fori_loop(0, nrb, inputs, 0)

        @pl.when(step == nt - 1)
        def _():
            for row in range(SM_ROWS):
                sm_ref[row:row + 1, :] = jnp.sum(acc[row], axis=0, keepdims=True)

    rev = lambda i: (nt - 1 - i, 0)
    row_f32 = pl.BlockSpec((TM, d), rev)
    const = lambda shape: pl.BlockSpec(shape, lambda i: (0,) * len(shape))
    return pl.pallas_call(
        body, name="b1_mix", grid=(nt,),
        in_specs=[row_f32, pl.BlockSpec((TM, N_SPLIT * d), rev), row_f32, row_f32, row_f32, row_f32,
                  const((3, d, d)), const(wa.shape), const(wb.shape), const((1, d)), const((1, d))],
        out_specs=[pl.BlockSpec((TM, N_SPLIT * d), rev),
                   pl.BlockSpec((3, TM, d), lambda i: (0, nt - 1 - i, 0)),
                   const((SM_ROWS, d))],
        out_shape=[jax.ShapeDtypeStruct((tp, N_SPLIT * d), BF16), jax.ShapeDtypeStruct((3, tp, d), BF16),
                   jax.ShapeDtypeStruct((SM_ROWS, d), F32)],
        scratch_shapes=[pltpu.VMEM((TM + HALO_A, d), F32), pltpu.VMEM((TM + HALO_B, d), F32),
                        pltpu.VMEM((shl, d), F32), pltpu.VMEM((TM, d), F32), pltpu.VMEM((TM, d), F32),
                        pltpu.VMEM((TM, d), F32), pltpu.VMEM((TM, d), F32),
                        pltpu.VMEM((SM_ROWS, 8, d), F32)],
        compiler_params=_params(("arbitrary",)),
    )(ds2, proj, ca, cb, ya, yb, w3, wa, wb, ln_g, ln_b)


def kernel(x, meta_tokens, norm_g, w_in, conv_a_w, conv_a_b, ln_a_g, ln_a_b, w_a_out, b_a_out, conv_b_w, w_b_out, w_out, final_g, loss_target, m_meta_tokens, m_norm_g, m_w_in, m_conv_a_w, m_conv_a_b, m_ln_a_g, m_ln_a_b, m_w_a_out, m_b_a_out, m_conv_b_w, m_w_b_out, m_w_out, m_final_g, v_meta_tokens, v_norm_g, v_w_in, v_conv_a_w, v_conv_a_b, v_ln_a_g, v_ln_a_b, v_w_a_out, v_b_a_out, v_conv_b_w, v_w_b_out, v_w_out, v_final_g):
    seq, d = x.shape[1], x.shape[2]
    dc = meta_tokens.shape[1]
    sw = w_in.shape[2]
    rsh = w_a_out.shape[1]
    xi, yi, ci = _mesh_pos()
    me = 2 * xi + yi
    pos = jnp.stack([ci, me]).astype(jnp.int32)

    conv_rows = HALO_A + HALO_B + 8
    convs = jnp.concatenate([
        jnp.pad(conv_a_w[0], ((0, HALO_A - CONV_A), (0, 0))),
        jnp.pad(conv_b_w[0], ((0, HALO_B - CONV_B), (0, 0))), jnp.zeros((8, dc), F32)], axis=0)[None]
    w3_own = jnp.stack([w_a_out[0], w_b_out[0], w_out[0]])
    (metag,) = _all_gather([_place_own(meta_tokens[None], pos, F32, "place_meta")])
    meta_full = jnp.transpose(metag[0], (1, 0, 2)).reshape(N_META, N_CHIPS * dc)
    fg2 = final_g.reshape(1, d)

    first_tile = jnp.concatenate([jnp.zeros((TM - N_META, d), F32), meta_full], axis=0)
    s_pad = jnp.concatenate([first_tile, x[0]], axis=0)

    proj, (wg_in, wg3, convg) = _proj_fwd(s_pad, norm_g, [_place_own(w_in, pos, BF16, "place_in"),
                                                          _place_own(w3_own, pos, BF16, "place_sq"),
                                                          _place_own(convs, pos, F32, "place_conv")], pos)
    w3 = wg3.reshape(3, N_CHIPS * rsh, d)
    convg = jnp.transpose(convg[0], (1, 0, 2)).reshape(conv_rows, N_CHIPS * dc)
    wa_full = convg[0:HALO_A]
    wb_full = convg[HALO_A:HALO_A + HALO_B]
    ca, cb, ya, yb, abm_t, ds2, h_t, loss8, dfg8 = _mix_fwd(
        s_pad, proj, loss_target[0], w3, wa_full, wb_full, conv_a_b, ln_a_g, ln_a_b, b_a_out, fg2, norm_g)
    dproj, d3, sm = _mix_bwd(ds2, proj, ca, cb, ya, yb, w3, wa_full, wb_full, ln_a_g, ln_a_b)
    g_sq = _dw_square(abm_t, d3).reshape(3, N_CHIPS, rsh, d)
    (r_sq,) = _sibling_exchange([g_sq], "rs_sibling_sq")
    p32_sq, pbf_sq = _add_sibling(g_sq, r_sq, pos, "rs_add_sq")
    g_in, l_sq = _dw_in(h_t, dproj, N_CHIPS, pbf_sq)
    (r_in,) = _sibling_exchange([g_in], "rs_sibling_in")
    p32_in, pbf_in = _add_sibling(g_in, r_in, pos, "rs_add_in")
    ds, dng8, l_in = _dh_bwd(dproj, wg_in, s_pad, ds2, norm_g, pbf_in)
    half_in = _sum_chips(p32_in, l_in, pos, "rs_sum_in")
    half_sq = _sum_chips(p32_sq, l_sq, pos, "rs_sum_sq")
    other_in, other_sq = _sibling_swap([half_in, half_sq])

    tail_row = lax.broadcasted_iota(jnp.int32, (8, d), 0)
    tail = jnp.where(tail_row == 0, dng8, jnp.where(tail_row == 1, dfg8,
                     jnp.where(tail_row == 2, loss8[0, 0], 0.0)))
    block = jnp.concatenate([sm, ds[TM - N_META:TM], tail], axis=0)
    red = _all_reduce_small(block)
    col = lax.dynamic_slice(red, (0, me * dc), (AR_ROWS, dc))
    g_small = {
        "meta_tokens": col[ROW_DMETA:ROW_DMETA + N_META],
        "norm_g": red[ROW_DNG:ROW_DNG + 1],
        "conv_a_w": col[ROW_DWA:ROW_DWA + CONV_A][None],
        "conv_a_b": red[ROW_DCAB:ROW_DCAB + 1],
        "ln_a_g": red[ROW_DLNG:ROW_DLNG + 1],
        "ln_a_b": red[ROW_DLNB:ROW_DLNB + 1],
        "b_a_out": red[ROW_DBAO:ROW_DBAO + 1],
        "conv_b_w": col[ROW_DWB:ROW_DWB + CONV_B][None],
        "final_g": red[ROW_DFG],
    }

    upd_in = _adam_halves([w_in], [m_w_in], [v_w_in], half_in, other_in, pos, "adam_in")
    upd_sq = _adam_halves([w_a_out, w_b_out, w_out], [m_w_a_out, m_w_b_out, m_w_out],
                          [v_w_a_out, v_w_b_out, v_w_out], half_sq, other_sq, pos, "adam_sq")
    small_w = {"meta_tokens": (meta_tokens, m_meta_tokens, v_meta_tokens), "norm_g": (norm_g, m_norm_g, v_norm_g),
               "conv_a_w": (conv_a_w, m_conv_a_w, v_conv_a_w), "conv_a_b": (conv_a_b, m_conv_a_b, v_conv_a_b),
               "ln_a_g": (ln_a_g, m_ln_a_g, v_ln_a_g), "ln_a_b": (ln_a_b, m_ln_a_b, v_ln_a_b),
               "b_a_out": (b_a_out, m_b_a_out, v_b_a_out), "conv_b_w": (conv_b_w, m_conv_b_w, v_conv_b_w),
               "final_g": (final_g, m_final_g, v_final_g)}
    names_small = list(small_w)
    as2d = lambda t: t.reshape(-1, t.shape[-1])
    upd_small = _adam_small([(as2d(small_w[k][0]), as2d(g_small[k]), as2d(small_w[k][1]), as2d(small_w[k][2]))
                             for k in names_small])

    grads, deltas, new_m, new_v = dict(g_small), {}, {}, {}
    for k, upd in zip(names_small, upd_small):
        deltas[k], new_m[k], new_v[k] = [t.reshape(small_w[k][0].shape) for t in upd]
    grads["w_in"], deltas["w_in"], new_m["w_in"], new_v["w_in"] = upd_in
    for idx, k in enumerate(["w_a_out", "w_b_out", "w_out"]):
        grads[k], deltas[k], new_m[k], new_v[k] = upd_sq[4 * idx:4 * idx + 4]

    loss = red[ROW_LOSS, 0]
    grad_x = ds[TM:][None]
    order = ["meta_tokens", "norm_g", "w_in", "conv_a_w", "conv_a_b", "ln_a_g", "ln_a_b", "w_a_out", "b_a_out",
             "conv_b_w", "w_b_out", "w_out", "final_g"]
    return (loss, grad_x, *[grads[k] for k in order], *[deltas[k] for k in order],
            *[new_m[k] for k in order], *[new_v[k] for k in order])
```

```python
import functools

import jax
import jax.numpy as jnp
from jax import lax
from jax.experimental import pallas as pl
from jax.experimental.pallas import tpu as pltpu

F32 = jnp.float32
BF16 = jnp.bfloat16
MESH = pl.DeviceIdType.MESH

EPS = 1e-6
N_META = 16
N_SPLIT = 9
CONV_A = 31
CONV_B = 3
HALO_A = 32
HALO_B = 8
SHIFT_ROWS = 24
TM = 256
RB = 64
N_ROW_TILES_BIG = 8
ROW_BLOCK = 256
N_CHIPS = 4
VMEM_LIMIT = 56 * 1024 * 1024

ADAM_LR = 0.001
ADAM_B1 = 0.9
ADAM_B2 = 0.999
ADAM_EPS = 1e-08
ADAM_WD = 0.01
ADAM_STEP = 10

ROW_DWA = 0
ROW_DWB = 32
ROW_DCAB = 40
ROW_DLNG = 41
ROW_DLNB = 42
ROW_DBAO = 43
SM_ROWS = 48
ROW_DMETA = 48
ROW_DNG = 64
ROW_DFG = 65
ROW_LOSS = 66
AR_ROWS = 72


def _sigmoid(v):
    return 0.5 * jnp.tanh(0.5 * v) + 0.5


def _params(sem, **kw):
    return pltpu.CompilerParams(dimension_semantics=sem, vmem_limit_bytes=VMEM_LIMIT, **kw)


def _rows(rb):
    return pl.ds(pl.multiple_of(rb * RB, RB), RB)


def _mesh_pos():
    x, y, c = lax.axis_index("x"), lax.axis_index("y"), lax.axis_index("c")
    return x, y, c


def _half(ref, j, c):
    h = ref.shape[2] // 2
    return ref.at[:, j, pl.ds(c * h, h), :]


def _place_own(shard, pos, dtype, name):
    s, r, c = shard.shape
    rb = ROW_BLOCK if r % ROW_BLOCK == 0 else r

    def body(pos_ref, x_ref, o_ref):
        o_ref[...] = x_ref[...].astype(dtype)

    return pl.pallas_call(
        body, name=name,
        grid_spec=pltpu.PrefetchScalarGridSpec(
            num_scalar_prefetch=1, grid=(s, r // rb),
            in_specs=[pl.BlockSpec((None, rb, c), lambda si, b, pos_ref: (si, b, 0))],
            out_specs=pl.BlockSpec((None, None, rb, c), lambda si, b, pos_ref: (si, pos_ref[1], b, 0))),
        out_shape=jax.ShapeDtypeStruct((s, N_CHIPS, r, c), dtype),
        compiler_params=_params(("arbitrary",) * 2),
    )(pos, shard)


def _all_gather(bufs):
    n = len(bufs)

    def body(*refs):
        outs = refs[n:2 * n]
        send_sems, recv_sems = refs[2 * n:]
        x, y, c = _mesh_pos()
        me = 2 * x + y
        sibling = (x, y, 1 - c)
        chips = [(1 - x, y), (x, 1 - y), (1 - x, 1 - y)]

        def remote(a, k, piece_src, piece_dst, to):
            return pltpu.make_async_remote_copy(
                src_ref=piece_src, dst_ref=piece_dst, send_sem=send_sems.at[6 * a + k],
                recv_sem=recv_sems.at[6 * a + k], device_id=to, device_id_type=MESH)

        sends = []
        for a in range(n):
            mine = _half(outs[a], me, c)
            for k, (px, py) in enumerate(chips):
                sends.append(remote(a, k, mine, mine, (px, py, c)))
        for cp in sends:
            cp.start()
        for a in range(n):
            for k, (px, py) in enumerate(chips):
                piece = _half(outs[a], 2 * px + py, c)
                remote(a, k, piece, piece, (px, py, c)).wait_recv()
                fwd = remote(a, 3 + k, piece, piece, sibling)
                fwd.start()
                sends.append(fwd)
        for a in range(n):
            for k, (px, py) in enumerate(chips):
                piece = _half(outs[a], 2 * px + py, 1 - c)
                remote(a, 3 + k, piece, piece, sibling).wait_recv()
        for cp in sends:
            cp.wait_send()

    any_spec = pl.BlockSpec(memory_space=pl.ANY)
    return pl.pallas_call(
        body, name="ag_weights",
        in_specs=[any_spec] * n, out_specs=[any_spec] * n,
        out_shape=[jax.ShapeDtypeStruct(b.shape, b.dtype) for b in bufs],
        input_output_aliases={a: a for a in range(n)},
        scratch_shapes=[pltpu.SemaphoreType.DMA((6 * n,)), pltpu.SemaphoreType.DMA((6 * n,))],
    )(*bufs)


class _Exchange:
    def __init__(self, sends, recvs):
        self.sends, self.recvs = sends, recvs

    def start(self):
        for cp in self.sends:
            cp.start()

    def finish(self):
        for cp in self.recvs:
            cp.wait_recv()
        for cp in self.sends:
            cp.wait_send()


def _chip_exchange(part_ref, land_ref, send_sems, recv_sems):
    x, y, c = _mesh_pos()
    me = 2 * x + y
    sends, recvs = [], []
    for k, (px, py) in enumerate([(1 - x, y), (x, 1 - y), (1 - x, 1 - y)]):
        sems = dict(send_sem=send_sems.at[k], recv_sem=recv_sems.at[k], device_id=(px, py, c), device_id_type=MESH)
        sends.append(pltpu.make_async_remote_copy(
            src_ref=part_ref.at[:, 2 * px + py], dst_ref=land_ref.at[:, me], **sems))
        landed = land_ref.at[:, 2 * px + py]
        recvs.append(pltpu.make_async_remote_copy(src_ref=landed, dst_ref=landed, **sems))
    return _Exchange(sends, recvs)


def _sibling_swap(halves):
    n = len(halves)

    def body(*refs):
        ins, outs = refs[:n], refs[n:2 * n]
        send_sems, recv_sems = refs[2 * n:]
        x, y, c = _mesh_pos()
        copies = [pltpu.make_async_remote_copy(
            src_ref=ins[a], dst_ref=outs[a], send_sem=send_sems.at[a], recv_sem=recv_sems.at[a],
            device_id=(x, y, 1 - c), device_id_type=MESH) for a in range(n)]
        for cp in copies:
            cp.start()
        for cp in copies:
            cp.wait()

    any_spec = pl.BlockSpec(memory_space=pl.ANY)
    return pl.pallas_call(
        body, name="rs_swap",
        in_specs=[any_spec] * n, out_specs=[any_spec] * n,
        out_shape=[jax.ShapeDtypeStruct(h.shape, h.dtype) for h in halves],
        scratch_shapes=[pltpu.SemaphoreType.DMA((n,)), pltpu.SemaphoreType.DMA((n,))],
    )(*halves)


def _all_reduce_small(block):
    rows, d = block.shape

    def body(x_ref, out_ref, sib_ref, part_ref, peers_ref, send_sems, recv_sems):
        x, y, c = _mesh_pos()
        me = 2 * x + y
        chips = [(1 - x, y), (x, 1 - y), (1 - x, 1 - y)]
        swap = pltpu.make_async_remote_copy(
            src_ref=x_ref, dst_ref=sib_ref, send_sem=send_sems.at[0], recv_sem=recv_sems.at[0],
            device_id=(x, y, 1 - c), device_id_type=MESH)
        swap.start()
        swap.wait()
        part_ref[...] = x_ref[...] + sib_ref[...]
        peers_ref[me] = part_ref[...]
        sends = [pltpu.make_async_remote_copy(
            src_ref=part_ref, dst_ref=peers_ref.at[me], send_sem=send_sems.at[1 + k], recv_sem=recv_sems.at[1 + k],
            device_id=(px, py, c), device_id_type=MESH) for k, (px, py) in enumerate(chips)]
        for cp in sends:
            cp.start()
        for k, (px, py) in enumerate(chips):
            landed = peers_ref.at[2 * px + py]
            pltpu.make_async_remote_copy(
                src_ref=landed, dst_ref=landed, send_sem=send_sems.at[1 + k], recv_sem=recv_sems.at[1 + k],
                device_id=(px, py, c), device_id_type=MESH).wait_recv()
        for cp in sends:
            cp.wait_send()
        out_ref[...] = ((peers_ref[0] + peers_ref[1]) + peers_ref[2]) + peers_ref[3]

    vm = pl.BlockSpec(memory_space=pltpu.VMEM)
    return pl.pallas_call(
        body, name="ar_small",
        in_specs=[vm], out_specs=vm,
        out_shape=jax.ShapeDtypeStruct((rows, d), F32),
        scratch_shapes=[pltpu.VMEM((rows, d), F32), pltpu.VMEM((rows, d), F32),
                        pltpu.VMEM((N_CHIPS, rows, d), F32),
                        pltpu.SemaphoreType.DMA((4,)), pltpu.SemaphoreType.DMA((4,))],
    )(block)


def _sum_chips(p32, landed, pos, name):
    s, nch, h, c = p32.shape
    hb = min(h, ROW_BLOCK)

    def body(pos_ref, p_ref, l1_ref, l2_ref, l3_ref, out_ref):
        out_ref[...] = ((p_ref[...] + l1_ref[...].astype(F32)) + l2_ref[...].astype(F32)) + l3_ref[...].astype(F32)

    def slot(k):
        return pl.BlockSpec((None, None, hb, c), lambda si, b, pos_ref: (si, (pos_ref[1] + k) % N_CHIPS, b, 0))

    return pl.pallas_call(
        body, name=name,
        grid_spec=pltpu.PrefetchScalarGridSpec(
            num_scalar_prefetch=1, grid=(s, h // hb),
            in_specs=[slot(0), slot(1), slot(2), slot(3)],
            out_specs=pl.BlockSpec((None, hb, c), lambda si, b, pos_ref: (si, b, 0))),
        out_shape=jax.ShapeDtypeStruct((s, h, c), F32),
        compiler_params=_params(("arbitrary",) * 2),
    )(pos, p32, landed, landed, landed)


def _adamw(w, g, m, v):
    m = ADAM_B1 * m + (1.0 - ADAM_B1) * g
    v = ADAM_B2 * v + (1.0 - ADAM_B2) * (g * g)
    m_hat = m / (1.0 - ADAM_B1 ** ADAM_STEP)
    v_hat = v / (1.0 - ADAM_B2 ** ADAM_STEP)
    delta = -ADAM_LR * (m_hat / (jnp.sqrt(v_hat) + ADAM_EPS) + ADAM_WD * w)
    return delta, m, v


def _adam_halves(ws, ms, vs, g_own, g_recv, pos, name):
    n = len(ws)
    _, r, c = ws[0].shape
    h = r // 2
    rb = min(h, ROW_BLOCK)
    nb = h // rb

    def body(pos_ref, *refs):
        w_refs, m_refs, v_refs = refs[:n], refs[n:2 * n], refs[2 * n:3 * n]
        go_ref, gr_ref = refs[3 * n:3 * n + 2]
        outs = refs[3 * n + 2:]
        mine = pl.program_id(0) == pos_ref[0]
        for a in range(n):
            g = jnp.where(mine, go_ref[a], gr_ref[a])
            delta, m, v = _adamw(w_refs[a][...], g, m_refs[a][...], v_refs[a][...])
            outs[4 * a][...], outs[4 * a + 1][...], outs[4 * a + 2][...], outs[4 * a + 3][...] = g, delta, m, v

    spec_w = pl.BlockSpec((None, rb, c), lambda hf, b, pos_ref: (0, hf * nb + b, 0))
    spec_g = pl.BlockSpec((n, rb, c), lambda hf, b, pos_ref: (0, b, 0))
    return pl.pallas_call(
        body, name=name,
        grid_spec=pltpu.PrefetchScalarGridSpec(
            num_scalar_prefetch=1, grid=(2, nb), in_specs=[spec_w] * (3 * n) + [spec_g] * 2,
            out_specs=[spec_w] * (4 * n)),
        out_shape=[jax.ShapeDtypeStruct((1, r, c), F32)] * (4 * n),
        compiler_params=_params(("arbitrary",) * 2),
    )(pos, *ws, *ms, *vs, g_own, g_recv)


def _adam_small(items):
    n = len(items)

    def body(*refs):
        ins, outs = refs[:4 * n], refs[4 * n:]
        for a in range(n):
            w_ref, g_ref, m_ref, v_ref = ins[4 * a:4 * a + 4]
            d, m, v = _adamw(w_ref[...], g_ref[...], m_ref[...], v_ref[...])
            outs[3 * a][...] = d
            outs[3 * a + 1][...] = m
            outs[3 * a + 2][...] = v

    vm = pl.BlockSpec(memory_space=pltpu.VMEM)
    flat = [t for it in items for t in it]
    outs = pl.pallas_call(
        body, name="adam_small", in_specs=[vm] * (4 * n), out_specs=[vm] * (3 * n),
        out_shape=[jax.ShapeDtypeStruct(it[0].shape, F32) for it in items for _ in range(3)],
    )(*flat)
    return [tuple(outs[3 * a:3 * a + 3]) for a in range(n)]


def _shard_of_step(js, me):
    flip = jnp.where(js == 1, 2, jnp.where(js == 2, 1, jnp.where(js == 3, 3, 0)))
    return lax.bitwise_xor(me, flip)


def _proj_fwd(s_pad, norm_g, bufs, pos):
    tp, d = s_pad.shape
    _, nsh, _, sw = bufs[0].shape
    tmb = tp // N_ROW_TILES_BIG
    n = len(bufs)

    def body(pos_ref, s_ref, g_ref, *refs):
        proj_ref = refs[n]
        gbufs = refs[n + 1:2 * n + 1]
        wbuf, wsem, send_sems, recv_sems = refs[2 * n + 1:]
        x, y, c = _mesh_pos()
        me = 2 * x + y
        sibling = (x, y, 1 - c)
        chips = [(1 - x, y), (x, 1 - y), (1 - x, 1 - y)]
        js, i = pl.program_id(0), pl.program_id(1)

        def remote(a, k, piece, to):
            return pltpu.make_async_remote_copy(
                src_ref=piece, dst_ref=piece, send_sem=send_sems.at[6 * a + k],
                recv_sem=recv_sems.at[6 * a + k], device_id=to, device_id_type=MESH)

        def fetch(chip):
            cp = pltpu.make_async_copy(gbufs[0].at[0, chip], wbuf, wsem)
            cp.start()
            cp.wait()

        def take(a, k):
            px, py = chips[k]
            remote(a, k, _half(gbufs[a], 2 * px + py, c), (px, py, c)).wait_recv()
            remote(a, 3 + k, _half(gbufs[a], 2 * px + py, c), sibling).start()
            remote(a, 3 + k, _half(gbufs[a], 2 * px + py, 1 - c), sibling).wait_recv()

        @pl.when((js == 0) & (i == 0))
        def _():
            for a in range(n):
                for k, (px, py) in enumerate(chips):
                    remote(a, k, _half(gbufs[a], me, c), (px, py, c)).start()
            fetch(me)

        for k, (px, py) in enumerate(chips):
            @pl.when((js == k + 1) & (i == 0))
            def _(k=k, px=px, py=py):
                take(0, k)
                fetch(2 * px + py)

        s = s_ref[...]
        r = lax.rsqrt(jnp.mean(s * s, axis=-1, keepdims=True) + EPS)
        h = (s * r * g_ref[...]).astype(BF16)
        proj_ref[...] = jnp.dot(h, wbuf[...], preferred_element_type=F32).astype(BF16)

        @pl.when((js == nsh - 1) & (i == N_ROW_TILES_BIG - 1))
        def _():
            for a in range(1, n):
                for k in range(len(chips)):
                    take(a, k)
            for a in range(n):
                for k, (px, py) in enumerate(chips):
                    remote(a, k, _half(gbufs[a], me, c), (px, py, c)).wait_send()
                    remote(a, 3 + k, _half(gbufs[a], 2 * px + py, c), sibling).wait_send()

    any_spec = pl.BlockSpec(memory_space=pl.ANY)
    outs = pl.pallas_call(
        body, name="f1_proj",
        grid_spec=pltpu.PrefetchScalarGridSpec(
            num_scalar_prefetch=1, grid=(nsh, N_ROW_TILES_BIG),
            in_specs=[pl.BlockSpec((tmb, d), lambda js, i, pos_ref: (i, 0)),
                      pl.BlockSpec((1, d), lambda js, i, pos_ref: (0, 0))] + [any_spec] * n,
            out_specs=[pl.BlockSpec((tmb, sw), lambda js, i, pos_ref: (i, _shard_of_step(js, pos_ref[1])))]
            + [any_spec] * n,
            scratch_shapes=[pltpu.VMEM((d, sw), BF16), pltpu.SemaphoreType.DMA,
                            pltpu.SemaphoreType.DMA((6 * n,)), pltpu.SemaphoreType.DMA((6 * n,))]),
        out_shape=[jax.ShapeDtypeStruct((tp, nsh * sw), BF16)]
        + [jax.ShapeDtypeStruct(b.shape, b.dtype) for b in bufs],
        input_output_aliases={3 + a: 1 + a for a in range(n)},
        compiler_params=_params(("arbitrary", "arbitrary")),
    )(pos, s_pad, norm_g, *bufs)
    return outs[0], outs[1:]


def _dh_bwd(dproj, wg_in, s_pad, ds2, norm_g, part):
    tp, d = s_pad.shape
    _, nsh, _, sw = wg_in.shape
    tmb = tp // N_ROW_TILES_BIG

    def body(dp_ref, w_ref, s_ref, ds2_ref, g_ref, part_ref, ds_ref, dng_ref, land_ref, acc, gacc,
             send_sems, recv_sems):
        exchange = _chip_exchange(part_ref, land_ref, send_sems, recv_sems)
        i, j = pl.program_id(0), pl.program_id(1)

        @pl.when((i == 0) & (j == 0))
        def _():
            exchange.start()
            gacc[...] = jnp.zeros_like(gacc)

        part = lax.dot_general(dp_ref[...], w_ref[...], (((1,), (1,)), ((), ())), preferred_element_type=F32)

        @pl.when(j == 0)
        def _():
            acc[...] = part

        @pl.when(j > 0)
        def _():
            acc[...] += part

        @pl.when(j == nsh - 1)
        def _():
            dh = acc[...]
            s = s_ref[...]
            r = lax.rsqrt(jnp.mean(s * s, axis=-1, keepdims=True) + EPS)
            gacc[...] += (dh * s * r).reshape(tmb // 8, 8, d).sum(axis=0)
            t = dh * g_ref[...]
            ds_ref[...] = ds2_ref[...] + r * t - s * (r * r * r) * jnp.mean(t * s, axis=-1, keepdims=True)

        @pl.when((i == N_ROW_TILES_BIG - 1) & (j == nsh - 1))
        def _():
            dng_ref[...] = jnp.broadcast_to(jnp.sum(gacc[...], axis=0, keepdims=True), (8, d))
            exchange.finish()

    any_spec = pl.BlockSpec(memory_space=pl.ANY)
    return pl.pallas_call(
        body, name="b2_dh", grid=(N_ROW_TILES_BIG, nsh),
        in_specs=[pl.BlockSpec((tmb, sw), lambda i, j: (i, j)),
                  pl.BlockSpec((None, None, d, sw), lambda i, j: (0, j, 0, 0)),
                  pl.BlockSpec((tmb, d), lambda i, j: (i, 0)),
                  pl.BlockSpec((tmb, d), lambda i, j: (i, 0)),
                  pl.BlockSpec((1, d), lambda i, j: (0, 0)), any_spec],
        out_specs=[pl.BlockSpec((tmb, d), lambda i, j: (i, 0)),
                   pl.BlockSpec((8, d), lambda i, j: (0, 0)), any_spec],
        out_shape=[jax.ShapeDtypeStruct((tp, d), F32), jax.ShapeDtypeStruct((8, d), F32),
                   jax.ShapeDtypeStruct(part.shape, part.dtype)],
        scratch_shapes=[pltpu.VMEM((tmb, d), F32), pltpu.VMEM((8, d), F32),
                        pltpu.SemaphoreType.DMA((3,)), pltpu.SemaphoreType.DMA((3,))],
        compiler_params=_params(("arbitrary", "arbitrary")),
    )(dproj, wg_in, s_pad, ds2, norm_g, part)


def _col_block(width, cap):
    return max(b for b in range(128, cap + 1, 128) if width % b == 0)


def _dw_reduced(lhs_t, rhs, cw, groups, out_dims, out_block, out_index, carried, name):
    na, d, tp = lhs_t.shape
    per_a = rhs.shape[2] // cw
    nblk = na * per_a
    rg = d // groups
    hh = rg // 2

    def body(*refs):
        if carried is None:
            l_ref, r_ref, p32_ref, pbf_ref, sib_ref, res, rbuf, lsem, send_sems, recv_sems = refs
            exchange = _Exchange([], [])
        else:
            (l_ref, r_ref, part_ref, p32_ref, pbf_ref, sib_ref, land_ref, res, rbuf, lsem, send_sems, recv_sems,
             xsend, xrecv) = refs
            exchange = _chip_exchange(part_ref, land_ref, xsend, xrecv)
        x, y, c = _mesh_pos()
        t = pl.program_id(0)
        u = jnp.maximum(t - 1, 0)

        def to_sibling(blk):
            return pltpu.make_async_remote_copy(
                src_ref=res.at[blk % 2, :, pl.ds((1 - c) * hh, hh), :], dst_ref=sib_ref.at[blk],
                send_sem=send_sems.at[blk], recv_sem=recv_sems.at[blk],
                device_id=(x, y, 1 - c), device_id_type=MESH)

        landed = pltpu.make_async_copy(sib_ref.at[u], rbuf, lsem)

        @pl.when(t == 0)
        def _():
            exchange.start()

        @pl.when(t >= 1)
        def _():
            to_sibling(u).wait_recv()
            landed.start()

        @pl.when(t < nblk)
        def _():
            res[t % 2] = jnp.dot(l_ref[...], r_ref[...], preferred_element_type=F32).reshape(groups, rg, cw)
            to_sibling(t).start()

        @pl.when(t >= 1)
        def _():
            landed.wait()
            p = res[u % 2, :, pl.ds(c * hh, hh), :] + rbuf[...]
            p32_ref[...] = p.reshape(p32_ref.shape)
            pbf_ref[...] = p.reshape(pbf_ref.shape).astype(BF16)
            to_sibling(u).wait_send()

        @pl.when(t == nblk)
        def _():
            exchange.finish()

    any_spec = pl.BlockSpec(memory_space=pl.ANY)
    last = nblk - 1
    out_spec = pl.BlockSpec(out_block, lambda t: out_index(jnp.maximum(t - 1, 0)))
    extra = [] if carried is None else [carried]
    outs = pl.pallas_call(
        body, name=name, grid=(nblk + 1,),
        in_specs=[pl.BlockSpec((None, d, tp), lambda t: (jnp.minimum(t, last) // per_a, 0, 0)),
                  pl.BlockSpec((None, tp, cw), lambda t: (jnp.minimum(t, last) // per_a, 0,
                                                          jnp.minimum(t, last) % per_a))]
        + [any_spec] * len(extra),
        out_specs=[out_spec, out_spec, any_spec] + [any_spec] * len(extra),
        out_shape=[jax.ShapeDtypeStruct(out_dims, F32), jax.ShapeDtypeStruct(out_dims, BF16),
                   jax.ShapeDtypeStruct((nblk, groups, hh, cw), F32)]
        + [jax.ShapeDtypeStruct(e.shape, e.dtype) for e in extra],
        scratch_shapes=[pltpu.VMEM((2, groups, rg, cw), F32), pltpu.VMEM((groups, hh, cw), F32),
                        pltpu.SemaphoreType.DMA, pltpu.SemaphoreType.DMA((nblk,)), pltpu.SemaphoreType.DMA((nblk,))]
        + [pltpu.SemaphoreType.DMA((3,)), pltpu.SemaphoreType.DMA((3,))] * len(extra),
        compiler_params=_params(("arbitrary",)),
    )(lhs_t, rhs, *extra)
    return outs[0], outs[1], (outs[3] if extra else None)


def _conv_a_taps(first_lag, last_lag):
    out = []
    for r in range(8):
        taps = [(q, 8 * q + r) for q in range(5) if first_lag <= 8 * q + r <= last_lag]
        if taps:
            out.append((r, taps))
    return out


def _mix_fwd(s_pad, proj, target, w3, wa, wb, conv_a_b, ln_g, ln_b, b_a_out, final_g, norm_g):
    tp, d = s_pad.shape
    nt = tp // TM
    nrb = TM // RB
    shl = TM + SHIFT_ROWS

    def body(s_ref, proj_ref, tgt_ref, w3_ref, wa_ref, wb_ref, cab_ref, lng_ref, lnb_ref, bao_ref, fg_ref, ng_ref,
             ca_ref, cb_ref, ya_ref, yb_ref, abmt_ref, ds2_ref, ht_ref, loss_ref, dfg_ref,
             abm_ref, ext_a, ext_b, sh, s2_s, lacc, gacc):
        i = pl.program_id(0)

        def split(k, rows):
            return proj_ref[rows, k * d:(k + 1) * d].astype(F32)

        s_in = s_ref[...]
        h = s_in * lax.rsqrt(jnp.mean(s_in * s_in, axis=-1, keepdims=True) + EPS) * ng_ref[...]
        ht_ref[...] = h.T.astype(BF16)

        @pl.when(i == 0)
        def _():
            ext_a[0:HALO_A, :] = jnp.zeros((HALO_A, d), F32)
            ext_b[0:HALO_B, :] = jnp.zeros((HALO_B, d), F32)
            lacc[...] = jnp.zeros_like(lacc)
            gacc[...] = jnp.zeros_like(gacc)

        def conv_in(rb, carry):
            rows = _rows(rb)
            ua0 = split(0, rows) * _sigmoid(split(1, rows))
            ext_a[pl.ds(pl.multiple_of(HALO_A + rb * RB, 8), RB), :] = ua0
            ext_b[pl.ds(pl.multiple_of(HALO_B + rb * RB, 8), RB), :] = split(4, rows) * split(5, rows)
            ca_ref[rows, :] = jnp.broadcast_to(cab_ref[...], (RB, d))
            return carry
        lax.fori_loop(0, nrb, conv_in, 0)

        for r, taps in _conv_a_taps(HALO_A - CONV_A + 1, HALO_A):
            if r == 0:
                src = ext_a
            else:
                sh[...] = ext_a[r:r + shl, :]
                src = sh

            def conv_acc(rb, carry, src=src, taps=taps):
                rows = _rows(rb)
                acc = ca_ref[rows, :]
                for q, lag in taps:
                    k = lag - (HALO_A - CONV_A + 1)
                    acc = acc + src[pl.ds(pl.multiple_of(rb * RB + 8 * q, 8), RB), :] * wa_ref[k:k + 1, :]
                ca_ref[rows, :] = acc
                return carry
            lax.fori_loop(0, nrb, conv_acc, 0)
        ext_a[0:HALO_A, :] = ext_a[TM:TM + HALO_A, :]

        cb_ref[...] = ext_b[HALO_B:HALO_B + TM, :] * wb_ref[2:3, :]
        for k in range(CONV_B - 1):
            off = HALO_B - CONV_B + 1 + k
            sh[0:TM, :] = ext_b[off:off + TM, :]
            cb_ref[...] += sh[0:TM, :] * wb_ref[k:k + 1, :]
        ext_b[0:HALO_B, :] = ext_b[TM:TM + HALO_B, :]

        def branches(rb, carry):
            rows = _rows(rb)
            ca = ca_ref[rows, :]
            mu = jnp.mean(ca, axis=-1, keepdims=True)
            xc = ca - mu
            rstd = lax.rsqrt(jnp.mean(xc * xc, axis=-1, keepdims=True) + EPS)
            ln = xc * rstd * lng_ref[...] + lnb_ref[...]
            ua = ln * _sigmoid(ln)
            a_z = split(2, rows)
            abm_ref[0, rows, :] = (ua * (a_z * _sigmoid(a_z))).astype(BF16)
            b_z = split(6, rows)
            ub = split(3, rows) * cb_ref[rows, :]
            abm_ref[1, rows, :] = (ub * (b_z * _sigmoid(b_z))).astype(BF16)
            return carry
        lax.fori_loop(0, nrb, branches, 0)

        ya_ref[...] = jnp.dot(abm_ref[0], w3_ref[0], preferred_element_type=F32) + bao_ref[...]
        yb_ref[...] = jnp.dot(abm_ref[1], w3_ref[1], preferred_element_type=F32)

        def merge(rb, carry):
            rows = _rows(rb)
            m = _sigmoid(split(7, rows)) * ya_ref[rows, :] + _sigmoid(split(8, rows)) * yb_ref[rows, :]
            abm_ref[2, rows, :] = m.astype(BF16)
            return carry
        lax.fori_loop(0, nrb, merge, 0)

        s2_s[...] = s_ref[...] + jnp.dot(abm_ref[2], w3_ref[2], preferred_element_type=F32)
        for k in range(3):
            abmt_ref[k] = abm_ref[k].astype(F32).T.astype(BF16)
        live = (i > 0).astype(F32)

        def head(rb, carry):
            rows = _rows(rb)
            s2 = s2_s[rows, :]
            r2 = lax.rsqrt(jnp.mean(s2 * s2, axis=-1, keepdims=True) + EPS)
            diff = (s2 * r2 * fg_ref[...] - tgt_ref[rows, :]) * live
            lacc[...] += diff * diff
            dy = diff * (1.0 / d)
            gacc[...] += (dy * s2 * r2).reshape(RB // 8, 8, d).sum(axis=0)
            t = dy * fg_ref[...]
            ds2_ref[rows, :] = r2 * t - s2 * (r2 * r2 * r2) * jnp.mean(t * s2, axis=-1, keepdims=True)
            return carry
        lax.fori_loop(0, nrb, head, 0)

        @pl.when(i == nt - 1)
        def _():
            loss_ref[...] = jnp.broadcast_to(0.5 * jnp.sum(lacc[...]) * (1.0 / d), (8, 128))
            dfg_ref[...] = jnp.broadcast_to(jnp.sum(gacc[...], axis=0, keepdims=True), (8, d))

    row_f32 = pl.BlockSpec((TM, d), lambda i: (i, 0))
    const = lambda shape: pl.BlockSpec(shape, lambda i: (0,) * len(shape))
    return pl.pallas_call(
        body, name="f2_mix", grid=(nt,),
        in_specs=[row_f32,
                  pl.BlockSpec((TM, N_SPLIT * d), lambda i: (i, 0)),
                  pl.BlockSpec((TM, d), lambda i: (jnp.maximum(i - 1, 0), 0)),
                  const((3, d, d)), const(wa.shape), const(wb.shape)] + [const((1, d))] * 6,
        out_specs=[row_f32, row_f32, row_f32, row_f32,
                   pl.BlockSpec((3, d, TM), lambda i: (0, 0, i)),
                   row_f32, pl.BlockSpec((d, TM), lambda i: (0, i)), const((8, 128)), const((8, d))],
        out_shape=[jax.ShapeDtypeStruct((tp, d), F32)] * 4
        + [jax.ShapeDtypeStruct((3, d, tp), BF16), jax.ShapeDtypeStruct((tp, d), F32),
           jax.ShapeDtypeStruct((d, tp), BF16),
           jax.ShapeDtypeStruct((8, 128), F32), jax.ShapeDtypeStruct((8, d), F32)],
        scratch_shapes=[pltpu.VMEM((3, TM, d), BF16),
                        pltpu.VMEM((HALO_A + TM, d), F32), pltpu.VMEM((HALO_B + TM, d), F32),
                        pltpu.VMEM((shl, d), F32), pltpu.VMEM((TM, d), F32),
                        pltpu.VMEM((RB, d), F32), pltpu.VMEM((8, d), F32)],
        compiler_params=_params(("arbitrary",)),
    )(s_pad, proj, target, w3, wa, wb, conv_a_b, ln_g, ln_b, b_a_out, final_g, norm_g)


def _mix_bwd(ds2, proj, ca, cb, ya, yb, w3, wa, wb, ln_g, ln_b):
    tp, d = ds2.shape
    nt = tp // TM
    nrb = TM // RB
    shl = TM + SHIFT_ROWS
    nt_dims = (((1,), (1,)), ((), ()))

    def body(ds2_ref, proj_ref, ca_ref, cb_ref, ya_ref, yb_ref, w3_ref, wa_ref, wb_ref, lng_ref, lnb_ref,
             dproj_ref, d3_ref, sm_ref, ext_d, ext_e, sh, dm_s, dpa_s, dpb_s, dua0_s, acc):
        step = pl.program_id(0)

        def split(k, rows):
            return proj_ref[rows, k * d:(k + 1) * d].astype(F32)

        def put(k, rows, val):
            dproj_ref[rows, k * d:(k + 1) * d] = val.astype(BF16)

        def accum(row, val):
            acc[row] += val.reshape(RB // 8, 8, d).sum(axis=0)

        @pl.when(step == 0)
        def _():
            ext_d[TM:TM + HALO_A, :] = jnp.zeros((HALO_A, d), F32)
            ext_e[TM:TM + HALO_B, :] = jnp.zeros((HALO_B, d), F32)
            acc[...] = jnp.zeros_like(acc)

        d3_ref[2] = ds2_ref[...].astype(BF16)
        dm_s[...] = lax.dot_general(d3_ref[2], w3_ref[2], nt_dims, preferred_element_type=F32)

        def gates(rb, carry):
            rows = _rows(rb)
            dm = dm_s[rows, :]
            sa = _sigmoid(split(7, rows))
            sb = _sigmoid(split(8, rows))
            ya_v = ya_ref[rows, :]
            yb_v = yb_ref[rows, :]
            put(7, rows, dm * ya_v * sa * (1.0 - sa))
            put(8, rows, dm * yb_v * sb * (1.0 - sb))
            dya = dm * sa
            accum(ROW_DBAO, dya)
            d3_ref[0, rows, :] = dya.astype(BF16)
            d3_ref[1, rows, :] = (dm * sb).astype(BF16)
            return carry
        lax.fori_loop(0, nrb, gates, 0)

        dpa_s[...] = lax.dot_general(d3_ref[0], w3_ref[0], nt_dims, preferred_element_type=F32)
        dpb_s[...] = lax.dot_general(d3_ref[1], w3_ref[1], nt_dims, preferred_element_type=F32)

        def branches(rb, carry):
            rows = _rows(rb)
            ca_v = ca_ref[rows, :]
            mu = jnp.mean(ca_v, axis=-1, keepdims=True)
            xc = ca_v - mu
            rstd = lax.rsqrt(jnp.mean(xc * xc, axis=-1, keepdims=True) + EPS)
            xhat = xc * rstd
            ln = xhat * lng_ref[...] + lnb_ref[...]
            sl = _sigmoid(ln)
            ua = ln * sl
            a_z = split(2, rows)
            sz = _sigmoid(a_z)
            dpa = dpa_s[rows, :]
            put(2, rows, dpa * ua * (sz * (1.0 + a_z * (1.0 - sz))))
            dln = dpa * (a_z * sz) * (sl * (1.0 + ln * (1.0 - sl)))
            accum(ROW_DLNG, dln * xhat)
            accum(ROW_DLNB, dln)
            dxh = dln * lng_ref[...]
            dca = rstd * (dxh - jnp.mean(dxh, axis=-1, keepdims=True)
                          - xhat * jnp.mean(dxh * xhat, axis=-1, keepdims=True))
            accum(ROW_DCAB, dca)
            ext_d[rows, :] = dca
            dua0_s[rows, :] = jnp.zeros((RB, d), F32)
            dm_s[rows, :] = split(0, rows) * _sigmoid(split(1, rows))
            b_z = split(6, rows)
            szb = _sigmoid(b_z)
            dpb = dpb_s[rows, :]
            b_b = split(3, rows)
            cb_v = cb_ref[rows, :]
            put(6, rows, dpb * (b_b * cb_v) * (szb * (1.0 + b_z * (1.0 - szb))))
            dub = dpb * (b_z * szb)
            put(3, rows, dub * cb_v)
            ext_e[rows, :] = dub * b_b
            return carry
        lax.fori_loop(0, nrb, branches, 0)

        for r, taps in _conv_a_taps(0, CONV_A - 1):
            if r == 0:
                src = ext_d
            else:
                sh[...] = ext_d[r:r + shl, :]
                src = sh

            def conv_t(rb, carry, src=src, taps=taps):
                rows = _rows(rb)
                ua0 = dm_s[rows, :]
                dua0 = dua0_s[rows, :]
                for q, lag in taps:
                    k = CONV_A - 1 - lag
                    slab = src[pl.ds(pl.multiple_of(rb * RB + 8 * q, 8), RB), :]
                    dua0 = dua0 + slab * wa_ref[k:k + 1, :]
                    accum(ROW_DWA + k, slab * ua0)
                dua0_s[rows, :] = dua0
                return carry
            lax.fori_loop(0, nrb, conv_t, 0)
        ext_d[TM:TM + HALO_A, :] = ext_d[0:HALO_A, :]

        dpb_s[...] = ext_e[0:TM, :] * wb_ref[CONV_B - 1:CONV_B, :]
        for lag in range(CONV_B):
            k = CONV_B - 1 - lag
            if lag > 0:
                sh[0:TM, :] = ext_e[lag:lag + TM, :]
                dpb_s[...] += sh[0:TM, :] * wb_ref[k:k + 1, :]
            src = ext_e if lag == 0 else sh

            def conv_b_w(rb, carry, src=src, k=k):
                rows = _rows(rb)
                accum(ROW_DWB + k, src[rows, :] * (split(4, rows) * split(5, rows)))
                return carry
            lax.fori_loop(0, nrb, conv_b_w, 0)
        ext_e[TM:TM + HALO_B, :] = ext_e[0:HALO_B, :]

        def inputs(rb, carry):
            rows = _rows(rb)
            dua0 = dua0_s[rows, :]
            a_val = split(0, rows)
            sg = _sigmoid(split(1, rows))
            put(0, rows, dua0 * sg)
            put(1, rows, dua0 * a_val * sg * (1.0 - sg))
            dcbin = dpb_s[rows, :]
            put(4, rows, dcbin * split(5, rows))
            put(5, rows, dcbin * split(4, rows))
            return carry
        lax.fori_loop(0, nrb, inputs, 0)

        @pl.when(step == nt - 1)
        def _():
            for row in range(SM_ROWS):
                sm_ref[row:row + 1, :] = jnp.sum(acc[row], axis=0, keepdims=True)

    rev = lambda i: (nt - 1 - i, 0)
    row_f32 = pl.BlockSpec((TM, d), rev)
    const = lambda shape: pl.BlockSpec(shape, lambda i: (0,) * len(shape))
    return pl.pallas_call(
        body, name="b1_mix", grid=(nt,),
        in_specs=[row_f32, pl.BlockSpec((TM, N_SPLIT * d), rev), row_f32, row_f32, row_f32, row_f32,
                  const((3, d, d)), const(wa.shape), const(wb.shape), const((1, d)), const((1, d))],
        out_specs=[pl.BlockSpec((TM, N_SPLIT * d), rev),
                   pl.BlockSpec((3, TM, d), lambda i: (0, nt - 1 - i, 0)),
                   const((SM_ROWS, d))],
        out_shape=[jax.ShapeDtypeStruct((tp, N_SPLIT * d), BF16), jax.ShapeDtypeStruct((3, tp, d), BF16),
                   jax.ShapeDtypeStruct((SM_ROWS, d), F32)],
        scratch_shapes=[pltpu.VMEM((TM + HALO_A, d), F32), pltpu.VMEM((TM + HALO_B, d), F32),
                        pltpu.VMEM((shl, d), F32), pltpu.VMEM((TM, d), F32), pltpu.VMEM((TM, d), F32),
                        pltpu.VMEM((TM, d), F32), pltpu.VMEM((TM, d), F32),
                        pltpu.VMEM((SM_ROWS, 8, d), F32)],
        compiler_params=_params(("arbitrary",)),
    )(ds2, proj, ca, cb, ya, yb, w3, wa, wb, ln_g, ln_b)


def kernel(x, meta_tokens, norm_g, w_in, conv_a_w, conv_a_b, ln_a_g, ln_a_b, w_a_out, b_a_out, conv_b_w, w_b_out, w_out, final_g, loss_target, m_meta_tokens, m_norm_g, m_w_in, m_conv_a_w, m_conv_a_b, m_ln_a_g, m_ln_a_b, m_w_a_out, m_b_a_out, m_conv_b_w, m_w_b_out, m_w_out, m_final_g, v_meta_tokens, v_norm_g, v_w_in, v_conv_a_w, v_conv_a_b, v_ln_a_g, v_ln_a_b, v_w_a_out, v_b_a_out, v_conv_b_w, v_w_b_out, v_w_out, v_final_g):
    seq, d = x.shape[1], x.shape[2]
    dc = meta_tokens.shape[1]
    sw = w_in.shape[2]
    rsh = w_a_out.shape[1]
    xi, yi, ci = _mesh_pos()
    me = 2 * xi + yi
    pos = jnp.stack([ci, me]).astype(jnp.int32)

    conv_rows = HALO_A + HALO_B + 8
    convs = jnp.concatenate([
        jnp.pad(conv_a_w[0], ((0, HALO_A - CONV_A), (0, 0))),
        jnp.pad(conv_b_w[0], ((0, HALO_B - CONV_B), (0, 0))), jnp.zeros((8, dc), F32)], axis=0)[None]
    w3_own = jnp.stack([w_a_out[0], w_b_out[0], w_out[0]])
    (metag,) = _all_gather([_place_own(meta_tokens[None], pos, F32, "place_meta")])
    meta_full = jnp.transpose(metag[0], (1, 0, 2)).reshape(N_META, N_CHIPS * dc)
    fg2 = final_g.reshape(1, d)

    first_tile = jnp.concatenate([jnp.zeros((TM - N_META, d), F32), meta_full], axis=0)
    s_pad = jnp.concatenate([first_tile, x[0]], axis=0)

    proj, (wg_in, wg3, convg) = _proj_fwd(s_pad, norm_g, [_place_own(w_in, pos, BF16, "place_in"),
                                                          _place_own(w3_own, pos, BF16, "place_sq"),
                                                          _place_own(convs, pos, F32, "place_conv")], pos)
    w3 = wg3.reshape(3, N_CHIPS * rsh, d)
    convg = jnp.transpose(convg[0], (1, 0, 2)).reshape(conv_rows, N_CHIPS * dc)
    wa_full = convg[0:HALO_A]
    wb_full = convg[HALO_A:HALO_A + HALO_B]
    ca, cb, ya, yb, abm_t, ds2, h_t, loss8, dfg8 = _mix_fwd(
        s_pad, proj, loss_target[0], w3, wa_full, wb_full, conv_a_b, ln_a_g, ln_a_b, b_a_out, fg2, norm_g)
    dproj, d3, sm = _mix_bwd(ds2, proj, ca, cb, ya, yb, w3, wa_full, wb_full, ln_a_g, ln_a_b)
    cw_sq = _col_block(d, 512)
    p32_sq, pbf_sq, _ = _dw_reduced(
        abm_t, d3, cw_sq, N_CHIPS, (3, N_CHIPS, rsh // 2, d), (None, N_CHIPS, rsh // 2, cw_sq),
        lambda u: (u // (d // cw_sq), 0, 0, u % (d // cw_sq)), None, "dw_square")
    cw_in = _col_block(sw, 768)
    p32_in, pbf_in, l_sq = _dw_reduced(
        h_t[None], dproj[None], cw_in, 1, (1, N_CHIPS, d // 2, sw), (None, None, d // 2, cw_in),
        lambda u: (0, u // (sw // cw_in), 0, u % (sw // cw_in)), pbf_sq, "dw_in")
    ds, dng8, l_in = _dh_bwd(dproj, wg_in, s_pad, ds2, norm_g, pbf_in)
    half_in = _sum_chips(p32_in, l_in, pos, "rs_sum_in")
    half_sq = _sum_chips(p32_sq, l_sq, pos, "rs_sum_sq")
    other_in, other_sq = _sibling_swap([half_in, half_sq])

    tail_row = lax.broadcasted_iota(jnp.int32, (8, d), 0)
    tail = jnp.where(tail_row == 0, dng8, jnp.where(tail_row == 1, dfg8,
                     jnp.where(tail_row == 2, loss8[0, 0], 0.0)))
    block = jnp.concatenate([sm, ds[TM - N_META:TM], tail], axis=0)
    red = _all_reduce_small(block)
    col = lax.dynamic_slice(red, (0, me * dc), (AR_ROWS, dc))
    g_small = {
        "meta_tokens": col[ROW_DMETA:ROW_DMETA + N_META],
        "norm_g": red[ROW_DNG:ROW_DNG + 1],
        "conv_a_w": col[ROW_DWA:ROW_DWA + CONV_A][None],
        "conv_a_b": red[ROW_DCAB:ROW_DCAB + 1],
        "ln_a_g": red[ROW_DLNG:ROW_DLNG + 1],
        "ln_a_b": red[ROW_DLNB:ROW_DLNB + 1],
        "b_a_out": red[ROW_DBAO:ROW_DBAO + 1],
        "conv_b_w": col[ROW_DWB:ROW_DWB + CONV_B][None],
        "final_g": red[ROW_DFG],
    }

    upd_in = _adam_halves([w_in], [m_w_in], [v_w_in], half_in, other_in, pos, "adam_in")
    upd_sq = _adam_halves([w_a_out, w_b_out, w_out], [m_w_a_out, m_w_b_out, m_w_out],
                          [v_w_a_out, v_w_b_out, v_w_out], half_sq, other_sq, pos, "adam_sq")
    small_w = {"meta_tokens": (meta_tokens, m_meta_tokens, v_meta_tokens), "norm_g": (norm_g, m_norm_g, v_norm_g),
               "conv_a_w": (conv_a_w, m_conv_a_w, v_conv_a_w), "conv_a_b": (conv_a_b, m_conv_a_b, v_conv_a_b),
               "ln_a_g": (ln_a_g, m_ln_a_g, v_ln_a_g), "ln_a_b": (ln_a_b, m_ln_a_b, v_ln_a_b),
               "b_a_out": (b_a_out, m_b_a_out, v_b_a_out), "conv_b_w": (conv_b_w, m_conv_b_w, v_conv_b_w),
               "final_g": (final_g, m_final_g, v_final_g)}
    names_small = list(small_w)
    as2d = lambda t: t.reshape(-1, t.shape[-1])
    upd_small = _adam_small([(as2d(small_w[k][0]), as2d(g_small[k]), as2d(small_w[k][1]), as2d(small_w[k][2]))
                             for k in names_small])

    grads, deltas, new_m, new_v = dict(g_small), {}, {}, {}
    for k, upd in zip(names_small, upd_small):
        deltas[k], new_m[k], new_v[k] = [t.reshape(small_w[k][0].shape) for t in upd]
    grads["w_in"], deltas["w_in"], new_m["w_in"], new_v["w_in"] = upd_in
    for idx, k in enumerate(["w_a_out", "w_b_out", "w_out"]):
        grads[k], deltas[k], new_m[k], new_v[k] = upd_sq[4 * idx:4 * idx + 4]

    loss = red[ROW_LOSS, 0]
    grad_x = ds[TM:][None]
    order = ["meta_tokens", "norm_g", "w_in", "conv_a_w", "conv_a_b", "ln_a_g", "ln_a_b", "w_a_out", "b_a_out",
             "conv_b_w", "w_b_out", "w_out", "final_g"]
    return (loss, grad_x, *[grads[k] for k in order], *[deltas[k] for k in order],
            *[new_m[k] for k in order], *[new_v[k] for k in order])
```

```python
import functools

import jax
import jax.numpy as jnp
from jax import lax
from jax.experimental import pallas as pl
from jax.experimental.pallas import tpu as pltpu

F32 = jnp.float32
BF16 = jnp.bfloat16
MESH = pl.DeviceIdType.MESH

EPS = 1e-6
N_META = 16
N_SPLIT = 9
CONV_A = 31
CONV_B = 3
HALO_A = 32
HALO_B = 8
SHIFT_ROWS = 24
TM = 256
RB = 64
N_ROW_TILES_BIG = 8
ROW_BLOCK = 256
N_CHIPS = 4
VMEM_LIMIT = 56 * 1024 * 1024

ADAM_LR = 0.001
ADAM_B1 = 0.9
ADAM_B2 = 0.999
ADAM_EPS = 1e-08
ADAM_WD = 0.01
ADAM_STEP = 10

ROW_DWA = 0
ROW_DWB = 32
ROW_DCAB = 40
ROW_DLNG = 41
ROW_DLNB = 42
ROW_DBAO = 43
SM_ROWS = 48
ROW_DMETA = 48
ROW_DNG = 64
ROW_DFG = 65
ROW_LOSS = 66
AR_ROWS = 72


def _sigmoid(v):
    return 0.5 * jnp.tanh(0.5 * v) + 0.5


def _params(sem, **kw):
    return pltpu.CompilerParams(dimension_semantics=sem, vmem_limit_bytes=VMEM_LIMIT, **kw)


def _rows(rb):
    return pl.ds(pl.multiple_of(rb * RB, RB), RB)


def _mesh_pos():
    x, y, c = lax.axis_index("x"), lax.axis_index("y"), lax.axis_index("c")
    return x, y, c


def _half(ref, j, c):
    h = ref.shape[2] // 2
    return ref.at[:, j, pl.ds(c * h, h), :]


def _place_own(shard, pos, dtype, name):
    s, r, c = shard.shape
    rb = ROW_BLOCK if r % ROW_BLOCK == 0 else r

    def body(pos_ref, x_ref, o_ref):
        o_ref[...] = x_ref[...].astype(dtype)

    return pl.pallas_call(
        body, name=name,
        grid_spec=pltpu.PrefetchScalarGridSpec(
            num_scalar_prefetch=1, grid=(s, r // rb),
            in_specs=[pl.BlockSpec((None, rb, c), lambda si, b, pos_ref: (si, b, 0))],
            out_specs=pl.BlockSpec((None, None, rb, c), lambda si, b, pos_ref: (si, pos_ref[1], b, 0))),
        out_shape=jax.ShapeDtypeStruct((s, N_CHIPS, r, c), dtype),
        compiler_params=_params(("arbitrary",) * 2),
    )(pos, shard)


def _all_gather(bufs):
    n = len(bufs)

    def body(*refs):
        outs = refs[n:2 * n]
        send_sems, recv_sems = refs[2 * n:]
        x, y, c = _mesh_pos()
        me = 2 * x + y
        sibling = (x, y, 1 - c)
        chips = [(1 - x, y), (x, 1 - y), (1 - x, 1 - y)]

        def remote(a, k, piece_src, piece_dst, to):
            return pltpu.make_async_remote_copy(
                src_ref=piece_src, dst_ref=piece_dst, send_sem=send_sems.at[6 * a + k],
                recv_sem=recv_sems.at[6 * a + k], device_id=to, device_id_type=MESH)

        sends = []
        for a in range(n):
            mine = _half(outs[a], me, c)
            for k, (px, py) in enumerate(chips):
                sends.append(remote(a, k, mine, mine, (px, py, c)))
        for cp in sends:
            cp.start()
        for a in range(n):
            for k, (px, py) in enumerate(chips):
                piece = _half(outs[a], 2 * px + py, c)
                remote(a, k, piece, piece, (px, py, c)).wait_recv()
                fwd = remote(a, 3 + k, piece, piece, sibling)
                fwd.start()
                sends.append(fwd)
        for a in range(n):
            for k, (px, py) in enumerate(chips):
                piece = _half(outs[a], 2 * px + py, 1 - c)
                remote(a, 3 + k, piece, piece, sibling).wait_recv()
        for cp in sends:
            cp.wait_send()

    any_spec = pl.BlockSpec(memory_space=pl.ANY)
    return pl.pallas_call(
        body, name="ag_weights",
        in_specs=[any_spec] * n, out_specs=[any_spec] * n,
        out_shape=[jax.ShapeDtypeStruct(b.shape, b.dtype) for b in bufs],
        input_output_aliases={a: a for a in range(n)},
        scratch_shapes=[pltpu.SemaphoreType.DMA((6 * n,)), pltpu.SemaphoreType.DMA((6 * n,))],
    )(*bufs)


class _Exchange:
    def __init__(self, sends, recvs):
        self.sends, self.recvs = sends, recvs

    def start(self):
        for cp in self.sends:
            cp.start()

    def finish(self):
        for cp in self.recvs:
            cp.wait_recv()
        for cp in self.sends:
            cp.wait_send()


def _chip_exchange(part_ref, land_ref, send_sems, recv_sems):
    x, y, c = _mesh_pos()
    me = 2 * x + y
    sends, recvs = [], []
    for k, (px, py) in enumerate([(1 - x, y), (x, 1 - y), (1 - x, 1 - y)]):
        sems = dict(send_sem=send_sems.at[k], recv_sem=recv_sems.at[k], device_id=(px, py, c), device_id_type=MESH)
        sends.append(pltpu.make_async_remote_copy(
            src_ref=part_ref.at[:, 2 * px + py], dst_ref=land_ref.at[:, me], **sems))
        landed = land_ref.at[:, 2 * px + py]
        recvs.append(pltpu.make_async_remote_copy(src_ref=landed, dst_ref=landed, **sems))
    return _Exchange(sends, recvs)


def _sibling_swap(halves):
    n = len(halves)

    def body(*refs):
        ins, outs = refs[:n], refs[n:2 * n]
        send_sems, recv_sems = refs[2 * n:]
        x, y, c = _mesh_pos()
        copies = [pltpu.make_async_remote_copy(
            src_ref=ins[a], dst_ref=outs[a], send_sem=send_sems.at[a], recv_sem=recv_sems.at[a],
            device_id=(x, y, 1 - c), device_id_type=MESH) for a in range(n)]
        for cp in copies:
            cp.start()
        for cp in copies:
            cp.wait()

    any_spec = pl.BlockSpec(memory_space=pl.ANY)
    return pl.pallas_call(
        body, name="rs_swap",
        in_specs=[any_spec] * n, out_specs=[any_spec] * n,
        out_shape=[jax.ShapeDtypeStruct(h.shape, h.dtype) for h in halves],
        scratch_shapes=[pltpu.SemaphoreType.DMA((n,)), pltpu.SemaphoreType.DMA((n,))],
    )(*halves)


def _all_reduce_small(block):
    rows, d = block.shape

    def body(x_ref, out_ref, sib_ref, part_ref, peers_ref, send_sems, recv_sems):
        x, y, c = _mesh_pos()
        me = 2 * x + y
        chips = [(1 - x, y), (x, 1 - y), (1 - x, 1 - y)]
        swap = pltpu.make_async_remote_copy(
            src_ref=x_ref, dst_ref=sib_ref, send_sem=send_sems.at[0], recv_sem=recv_sems.at[0],
            device_id=(x, y, 1 - c), device_id_type=MESH)
        swap.start()
        swap.wait()
        part_ref[...] = x_ref[...] + sib_ref[...]
        peers_ref[me] = part_ref[...]
        sends = [pltpu.make_async_remote_copy(
            src_ref=part_ref, dst_ref=peers_ref.at[me], send_sem=send_sems.at[1 + k], recv_sem=recv_sems.at[1 + k],
            device_id=(px, py, c), device_id_type=MESH) for k, (px, py) in enumerate(chips)]
        for cp in sends:
            cp.start()
        for k, (px, py) in enumerate(chips):
            landed = peers_ref.at[2 * px + py]
            pltpu.make_async_remote_copy(
                src_ref=landed, dst_ref=landed, send_sem=send_sems.at[1 + k], recv_sem=recv_sems.at[1 + k],
                device_id=(px, py, c), device_id_type=MESH).wait_recv()
        for cp in sends:
            cp.wait_send()
        out_ref[...] = ((peers_ref[0] + peers_ref[1]) + peers_ref[2]) + peers_ref[3]

    vm = pl.BlockSpec(memory_space=pltpu.VMEM)
    return pl.pallas_call(
        body, name="ar_small",
        in_specs=[vm], out_specs=vm,
        out_shape=jax.ShapeDtypeStruct((rows, d), F32),
        scratch_shapes=[pltpu.VMEM((rows, d), F32), pltpu.VMEM((rows, d), F32),
                        pltpu.VMEM((N_CHIPS, rows, d), F32),
                        pltpu.SemaphoreType.DMA((4,)), pltpu.SemaphoreType.DMA((4,))],
    )(block)


def _sum_chips(p32, landed, pos, name):
    s, nch, h, c = p32.shape
    hb = min(h, ROW_BLOCK)

    def body(pos_ref, p_ref, l1_ref, l2_ref, l3_ref, out_ref):
        out_ref[...] = ((p_ref[...] + l1_ref[...].astype(F32)) + l2_ref[...].astype(F32)) + l3_ref[...].astype(F32)

    def slot(k):
        return pl.BlockSpec((None, None, hb, c), lambda si, b, pos_ref: (si, (pos_ref[1] + k) % N_CHIPS, b, 0))

    return pl.pallas_call(
        body, name=name,
        grid_spec=pltpu.PrefetchScalarGridSpec(
            num_scalar_prefetch=1, grid=(s, h // hb),
            in_specs=[slot(0), slot(1), slot(2), slot(3)],
            out_specs=pl.BlockSpec((None, hb, c), lambda si, b, pos_ref: (si, b, 0))),
        out_shape=jax.ShapeDtypeStruct((s, h, c), F32),
        compiler_params=_params(("arbitrary",) * 2),
    )(pos, p32, landed, landed, landed)


def _adamw(w, g, m, v):
    m = ADAM_B1 * m + (1.0 - ADAM_B1) * g
    v = ADAM_B2 * v + (1.0 - ADAM_B2) * (g * g)
    m_hat = m / (1.0 - ADAM_B1 ** ADAM_STEP)
    v_hat = v / (1.0 - ADAM_B2 ** ADAM_STEP)
    delta = -ADAM_LR * (m_hat / (jnp.sqrt(v_hat) + ADAM_EPS) + ADAM_WD * w)
    return delta, m, v


def _adam_halves(ws, ms, vs, g_own, g_recv, pos, name):
    n = len(ws)
    _, r, c = ws[0].shape
    h = r // 2
    rb = min(h, ROW_BLOCK)
    nb = h // rb

    def body(pos_ref, *refs):
        w_refs, m_refs, v_refs = refs[:n], refs[n:2 * n], refs[2 * n:3 * n]
        go_ref, gr_ref = refs[3 * n:3 * n + 2]
        outs = refs[3 * n + 2:]
        mine = pl.program_id(0) == pos_ref[0]
        for a in range(n):
            g = jnp.where(mine, go_ref[a], gr_ref[a])
            delta, m, v = _adamw(w_refs[a][...], g, m_refs[a][...], v_refs[a][...])
            outs[4 * a][...], outs[4 * a + 1][...], outs[4 * a + 2][...], outs[4 * a + 3][...] = g, delta, m, v

    spec_w = pl.BlockSpec((None, rb, c), lambda hf, b, pos_ref: (0, hf * nb + b, 0))
    spec_g = pl.BlockSpec((n, rb, c), lambda hf, b, pos_ref: (0, b, 0))
    return pl.pallas_call(
        body, name=name,
        grid_spec=pltpu.PrefetchScalarGridSpec(
            num_scalar_prefetch=1, grid=(2, nb), in_specs=[spec_w] * (3 * n) + [spec_g] * 2,
            out_specs=[spec_w] * (4 * n)),
        out_shape=[jax.ShapeDtypeStruct((1, r, c), F32)] * (4 * n),
        compiler_params=_params(("arbitrary",) * 2),
    )(pos, *ws, *ms, *vs, g_own, g_recv)


def _adam_small(items):
    n = len(items)

    def body(*refs):
        ins, outs = refs[:4 * n], refs[4 * n:]
        for a in range(n):
            w_ref, g_ref, m_ref, v_ref = ins[4 * a:4 * a + 4]
            d, m, v = _adamw(w_ref[...], g_ref[...], m_ref[...], v_ref[...])
            outs[3 * a][...] = d
            outs[3 * a + 1][...] = m
            outs[3 * a + 2][...] = v

    vm = pl.BlockSpec(memory_space=pltpu.VMEM)
    flat = [t for it in items for t in it]
    outs = pl.pallas_call(
        body, name="adam_small", in_specs=[vm] * (4 * n), out_specs=[vm] * (3 * n),
        out_shape=[jax.ShapeDtypeStruct(it[0].shape, F32) for it in items for _ in range(3)],
    )(*flat)
    return [tuple(outs[3 * a:3 * a + 3]) for a in range(n)]


def _shard_of_step(js, me):
    flip = jnp.where(js == 1, 2, jnp.where(js == 2, 1, jnp.where(js == 3, 3, 0)))
    return lax.bitwise_xor(me, flip)


def _proj_fwd(s_pad, norm_g, bufs, pos):
    tp, d = s_pad.shape
    _, nsh, _, sw = bufs[0].shape
    tmb = tp // N_ROW_TILES_BIG
    n = len(bufs)

    def body(pos_ref, s_ref, g_ref, *refs):
        proj_ref = refs[n]
        gbufs = refs[n + 1:2 * n + 1]
        wbuf, wsem, send_sems, recv_sems = refs[2 * n + 1:]
        x, y, c = _mesh_pos()
        me = 2 * x + y
        sibling = (x, y, 1 - c)
        chips = [(1 - x, y), (x, 1 - y), (1 - x, 1 - y)]
        js, i = pl.program_id(0), pl.program_id(1)

        def remote(a, k, piece, to):
            return pltpu.make_async_remote_copy(
                src_ref=piece, dst_ref=piece, send_sem=send_sems.at[6 * a + k],
                recv_sem=recv_sems.at[6 * a + k], device_id=to, device_id_type=MESH)

        def fetch(chip):
            cp = pltpu.make_async_copy(gbufs[0].at[0, chip], wbuf, wsem)
            cp.start()
            cp.wait()

        def take(a, k):
            px, py = chips[k]
            remote(a, k, _half(gbufs[a], 2 * px + py, c), (px, py, c)).wait_recv()
            remote(a, 3 + k, _half(gbufs[a], 2 * px + py, c), sibling).start()
            remote(a, 3 + k, _half(gbufs[a], 2 * px + py, 1 - c), sibling).wait_recv()

        @pl.when((js == 0) & (i == 0))
        def _():
            for a in range(n):
                for k, (px, py) in enumerate(chips):
                    remote(a, k, _half(gbufs[a], me, c), (px, py, c)).start()
            fetch(me)

        for k, (px, py) in enumerate(chips):
            @pl.when((js == k + 1) & (i == 0))
            def _(k=k, px=px, py=py):
                take(0, k)
                fetch(2 * px + py)

        s = s_ref[...]
        r = lax.rsqrt(jnp.mean(s * s, axis=-1, keepdims=True) + EPS)
        h = (s * r * g_ref[...]).astype(BF16)
        proj_ref[...] = jnp.dot(h, wbuf[...], preferred_element_type=F32).astype(BF16)

        @pl.when((js == nsh - 1) & (i == N_ROW_TILES_BIG - 1))
        def _():
            for a in range(1, n):
                for k in range(len(chips)):
                    take(a, k)
            for a in range(n):
                for k, (px, py) in enumerate(chips):
                    remote(a, k, _half(gbufs[a], me, c), (px, py, c)).wait_send()
                    remote(a, 3 + k, _half(gbufs[a], 2 * px + py, c), sibling).wait_send()

    any_spec = pl.BlockSpec(memory_space=pl.ANY)
    outs = pl.pallas_call(
        body, name="f1_proj",
        grid_spec=pltpu.PrefetchScalarGridSpec(
            num_scalar_prefetch=1, grid=(nsh, N_ROW_TILES_BIG),
            in_specs=[pl.BlockSpec((tmb, d), lambda js, i, pos_ref: (i, 0)),
                      pl.BlockSpec((1, d), lambda js, i, pos_ref: (0, 0))] + [any_spec] * n,
            out_specs=[pl.BlockSpec((tmb, sw), lambda js, i, pos_ref: (i, _shard_of_step(js, pos_ref[1])))]
            + [any_spec] * n,
            scratch_shapes=[pltpu.VMEM((d, sw), BF16), pltpu.SemaphoreType.DMA,
                            pltpu.SemaphoreType.DMA((6 * n,)), pltpu.SemaphoreType.DMA((6 * n,))]),
        out_shape=[jax.ShapeDtypeStruct((tp, nsh * sw), BF16)]
        + [jax.ShapeDtypeStruct(b.shape, b.dtype) for b in bufs],
        input_output_aliases={3 + a: 1 + a for a in range(n)},
        compiler_params=_params(("arbitrary", "arbitrary")),
    )(pos, s_pad, norm_g, *bufs)
    return outs[0], outs[1:]


def _dh_bwd(dproj, wg_in, s_pad, ds2, norm_g, part):
    tp, d = s_pad.shape
    _, nsh, _, sw = wg_in.shape
    tmb = tp // N_ROW_TILES_BIG

    def body(dp_ref, w_ref, s_ref, ds2_ref, g_ref, part_ref, ds_ref, dng_ref, land_ref, acc, gacc,
             send_sems, recv_sems):
        exchange = _chip_exchange(part_ref, land_ref, send_sems, recv_sems)
        i, j = pl.program_id(0), pl.program_id(1)

        @pl.when((i == 0) & (j == 0))
        def _():
            exchange.start()
            gacc[...] = jnp.zeros_like(gacc)

        part = lax.dot_general(dp_ref[...], w_ref[...], (((1,), (1,)), ((), ())), preferred_element_type=F32)

        @pl.when(j == 0)
        def _():
            acc[...] = part

        @pl.when(j > 0)
        def _():
            acc[...] += part

        @pl.when(j == nsh - 1)
        def _():
            dh = acc[...]
            s = s_ref[...]
            r = lax.rsqrt(jnp.mean(s * s, axis=-1, keepdims=True) + EPS)
            gacc[...] += (dh * s * r).reshape(tmb // 8, 8, d).sum(axis=0)
            t = dh * g_ref[...]
            ds_ref[...] = ds2_ref[...] + r * t - s * (r * r * r) * jnp.mean(t * s, axis=-1, keepdims=True)

        @pl.when((i == N_ROW_TILES_BIG - 1) & (j == nsh - 1))
        def _():
            dng_ref[...] = jnp.broadcast_to(jnp.sum(gacc[...], axis=0, keepdims=True), (8, d))
            exchange.finish()

    any_spec = pl.BlockSpec(memory_space=pl.ANY)
    return pl.pallas_call(
        body, name="b2_dh", grid=(N_ROW_TILES_BIG, nsh),
        in_specs=[pl.BlockSpec((tmb, sw), lambda i, j: (i, j)),
                  pl.BlockSpec((None, None, d, sw), lambda i, j: (0, j, 0, 0)),
                  pl.BlockSpec((tmb, d), lambda i, j: (i, 0)),
                  pl.BlockSpec((tmb, d), lambda i, j: (i, 0)),
                  pl.BlockSpec((1, d), lambda i, j: (0, 0)), any_spec],
        out_specs=[pl.BlockSpec((tmb, d), lambda i, j: (i, 0)),
                   pl.BlockSpec((8, d), lambda i, j: (0, 0)), any_spec],
        out_shape=[jax.ShapeDtypeStruct((tp, d), F32), jax.ShapeDtypeStruct((8, d), F32),
                   jax.ShapeDtypeStruct(part.shape, part.dtype)],
        scratch_shapes=[pltpu.VMEM((tmb, d), F32), pltpu.VMEM((8, d), F32),
                        pltpu.SemaphoreType.DMA((3,)), pltpu.SemaphoreType.DMA((3,))],
        compiler_params=_params(("arbitrary", "arbitrary")),
    )(dproj, wg_in, s_pad, ds2, norm_g, part)


def _col_block(width, cap):
    return max(b for b in range(128, cap + 1, 128) if width % b == 0)


def _dw_reduced(lhs_t, rhs, cw, groups, out_dims, out_block, out_index, carried, name):
    na, d, tp = lhs_t.shape
    per_a = rhs.shape[2] // cw
    nblk = na * per_a
    rg = d // groups
    hh = rg // 2

    def body(*refs):
        if carried is None:
            l_ref, r_ref, p32_ref, pbf_ref, res, rbuf, send_sems, recv_sems = refs
            exchange = _Exchange([], [])
        else:
            (l_ref, r_ref, part_ref, p32_ref, pbf_ref, land_ref, res, rbuf, send_sems, recv_sems,
             xsend, xrecv) = refs
            exchange = _chip_exchange(part_ref, land_ref, xsend, xrecv)
        x, y, c = _mesh_pos()
        t = pl.program_id(0)
        u = jnp.maximum(t - 1, 0)

        def to_sibling(blk):
            return pltpu.make_async_remote_copy(
                src_ref=res.at[blk % 2, :, pl.ds((1 - c) * hh, hh), :], dst_ref=rbuf.at[blk % 2],
                send_sem=send_sems.at[blk], recv_sem=recv_sems.at[blk],
                device_id=(x, y, 1 - c), device_id_type=MESH)

        @pl.when(t == 0)
        def _():
            exchange.start()

        @pl.when(t < nblk)
        def _():
            res[t % 2] = jnp.dot(l_ref[...], r_ref[...], preferred_element_type=F32).reshape(groups, rg, cw)

        @pl.when(t >= 1)
        def _():
            to_sibling(u).wait_recv()
            p = res[u % 2, :, pl.ds(c * hh, hh), :] + rbuf[u % 2]
            p32_ref[...] = p.reshape(p32_ref.shape)
            pbf_ref[...] = p.reshape(pbf_ref.shape).astype(BF16)

        @pl.when(t < nblk)
        def _():
            to_sibling(t).start()

        @pl.when(t >= 1)
        def _():
            to_sibling(u).wait_send()

        @pl.when(t == nblk)
        def _():
            exchange.finish()

    any_spec = pl.BlockSpec(memory_space=pl.ANY)
    last = nblk - 1
    out_spec = pl.BlockSpec(out_block, lambda t: out_index(jnp.maximum(t - 1, 0)))
    extra = [] if carried is None else [carried]
    outs = pl.pallas_call(
        body, name=name, grid=(nblk + 1,),
        in_specs=[pl.BlockSpec((None, d, tp), lambda t: (jnp.minimum(t, last) // per_a, 0, 0)),
                  pl.BlockSpec((None, tp, cw), lambda t: (jnp.minimum(t, last) // per_a, 0,
                                                          jnp.minimum(t, last) % per_a))]
        + [any_spec] * len(extra),
        out_specs=[out_spec, out_spec] + [any_spec] * len(extra),
        out_shape=[jax.ShapeDtypeStruct(out_dims, F32), jax.ShapeDtypeStruct(out_dims, BF16)]
        + [jax.ShapeDtypeStruct(e.shape, e.dtype) for e in extra],
        scratch_shapes=[pltpu.VMEM((2, groups, rg, cw), F32), pltpu.VMEM((2, groups, hh, cw), F32),
                        pltpu.SemaphoreType.DMA((nblk,)), pltpu.SemaphoreType.DMA((nblk,))]
        + [pltpu.SemaphoreType.DMA((3,)), pltpu.SemaphoreType.DMA((3,))] * len(extra),
        compiler_params=_params(("arbitrary",)),
    )(lhs_t, rhs, *extra)
    return outs[0], outs[1], (outs[2] if extra else None)


def _conv_a_taps(first_lag, last_lag):
    out = []
    for r in range(8):
        taps = [(q, 8 * q + r) for q in range(5) if first_lag <= 8 * q + r <= last_lag]
        if taps:
            out.append((r, taps))
    return out


def _mix_fwd(s_pad, proj, target, w3, wa, wb, conv_a_b, ln_g, ln_b, b_a_out, final_g, norm_g):
    tp, d = s_pad.shape
    nt = tp // TM
    nrb = TM // RB
    shl = TM + SHIFT_ROWS

    def body(s_ref, proj_ref, tgt_ref, w3_ref, wa_ref, wb_ref, cab_ref, lng_ref, lnb_ref, bao_ref, fg_ref, ng_ref,
             ca_ref, cb_ref, ya_ref, yb_ref, abmt_ref, ds2_ref, ht_ref, loss_ref, dfg_ref,
             abm_ref, ext_a, ext_b, sh, s2_s, lacc, gacc):
        i = pl.program_id(0)

        def split(k, rows):
            return proj_ref[rows, k * d:(k + 1) * d].astype(F32)

        s_in = s_ref[...]
        h = s_in * lax.rsqrt(jnp.mean(s_in * s_in, axis=-1, keepdims=True) + EPS) * ng_ref[...]
        ht_ref[...] = h.T.astype(BF16)

        @pl.when(i == 0)
        def _():
            ext_a[0:HALO_A, :] = jnp.zeros((HALO_A, d), F32)
            ext_b[0:HALO_B, :] = jnp.zeros((HALO_B, d), F32)
            lacc[...] = jnp.zeros_like(lacc)
            gacc[...] = jnp.zeros_like(gacc)

        def conv_in(rb, carry):
            rows = _rows(rb)
            ua0 = split(0, rows) * _sigmoid(split(1, rows))
            ext_a[pl.ds(pl.multiple_of(HALO_A + rb * RB, 8), RB), :] = ua0
            ext_b[pl.ds(pl.multiple_of(HALO_B + rb * RB, 8), RB), :] = split(4, rows) * split(5, rows)
            ca_ref[rows, :] = jnp.broadcast_to(cab_ref[...], (RB, d))
            return carry
        lax.fori_loop(0, nrb, conv_in, 0)

        for r, taps in _conv_a_taps(HALO_A - CONV_A + 1, HALO_A):
            if r == 0:
                src = ext_a
            else:
                sh[...] = ext_a[r:r + shl, :]
                src = sh

            def conv_acc(rb, carry, src=src, taps=taps):
                rows = _rows(rb)
                acc = ca_ref[rows, :]
                for q, lag in taps:
                    k = lag - (HALO_A - CONV_A + 1)
                    acc = acc + src[pl.ds(pl.multiple_of(rb * RB + 8 * q, 8), RB), :] * wa_ref[k:k + 1, :]
                ca_ref[rows, :] = acc
                return carry
            lax.fori_loop(0, nrb, conv_acc, 0)
        ext_a[0:HALO_A, :] = ext_a[TM:TM + HALO_A, :]

        cb_ref[...] = ext_b[HALO_B:HALO_B + TM, :] * wb_ref[2:3, :]
        for k in range(CONV_B - 1):
            off = HALO_B - CONV_B + 1 + k
            sh[0:TM, :] = ext_b[off:off + TM, :]
            cb_ref[...] += sh[0:TM, :] * wb_ref[k:k + 1, :]
        ext_b[0:HALO_B, :] = ext_b[TM:TM + HALO_B, :]

        def branches(rb, carry):
            rows = _rows(rb)
            ca = ca_ref[rows, :]
            mu = jnp.mean(ca, axis=-1, keepdims=True)
            xc = ca - mu
            rstd = lax.rsqrt(jnp.mean(xc * xc, axis=-1, keepdims=True) + EPS)
            ln = xc * rstd * lng_ref[...] + lnb_ref[...]
            ua = ln * _sigmoid(ln)
            a_z = split(2, rows)
            abm_ref[0, rows, :] = (ua * (a_z * _sigmoid(a_z))).astype(BF16)
            b_z = split(6, rows)
            ub = split(3, rows) * cb_ref[rows, :]
            abm_ref[1, rows, :] = (ub * (b_z * _sigmoid(b_z))).astype(BF16)
            return carry
        lax.fori_loop(0, nrb, branches, 0)

        ya_ref[...] = jnp.dot(abm_ref[0], w3_ref[0], preferred_element_type=F32) + bao_ref[...]
        yb_ref[...] = jnp.dot(abm_ref[1], w3_ref[1], preferred_element_type=F32)

        def merge(rb, carry):
            rows = _rows(rb)
            m = _sigmoid(split(7, rows)) * ya_ref[rows, :] + _sigmoid(split(8, rows)) * yb_ref[rows, :]
            abm_ref[2, rows, :] = m.astype(BF16)
            return carry
        lax.fori_loop(0, nrb, merge, 0)

        s2_s[...] = s_ref[...] + jnp.dot(abm_ref[2], w3_ref[2], preferred_element_type=F32)
        for k in range(3):
            abmt_ref[k] = abm_ref[k].astype(F32).T.astype(BF16)
        live = (i > 0).astype(F32)

        def head(rb, carry):
            rows = _rows(rb)
            s2 = s2_s[rows, :]
            r2 = lax.rsqrt(jnp.mean(s2 * s2, axis=-1, keepdims=True) + EPS)
            diff = (s2 * r2 * fg_ref[...] - tgt_ref[rows, :]) * live
            lacc[...] += diff * diff
            dy = diff * (1.0 / d)
            gacc[...] += (dy * s2 * r2).reshape(RB // 8, 8, d).sum(axis=0)
            t = dy * fg_ref[...]
            ds2_ref[rows, :] = r2 * t - s2 * (r2 * r2 * r2) * jnp.mean(t * s2, axis=-1, keepdims=True)
            return carry
        lax.fori_loop(0, nrb, head, 0)

        @pl.when(i == nt - 1)
        def _():
            loss_ref[...] = jnp.broadcast_to(0.5 * jnp.sum(lacc[...]) * (1.0 / d), (8, 128))
            dfg_ref[...] = jnp.broadcast_to(jnp.sum(gacc[...], axis=0, keepdims=True), (8, d))

    row_f32 = pl.BlockSpec((TM, d), lambda i: (i, 0))
    const = lambda shape: pl.BlockSpec(shape, lambda i: (0,) * len(shape))
    return pl.pallas_call(
        body, name="f2_mix", grid=(nt,),
        in_specs=[row_f32,
                  pl.BlockSpec((TM, N_SPLIT * d), lambda i: (i, 0)),
                  pl.BlockSpec((TM, d), lambda i: (jnp.maximum(i - 1, 0), 0)),
                  const((3, d, d)), const(wa.shape), const(wb.shape)] + [const((1, d))] * 6,
        out_specs=[row_f32, row_f32, row_f32, row_f32,
                   pl.BlockSpec((3, d, TM), lambda i: (0, 0, i)),
                   row_f32, pl.BlockSpec((d, TM), lambda i: (0, i)), const((8, 128)), const((8, d))],
        out_shape=[jax.ShapeDtypeStruct((tp, d), F32)] * 4
        + [jax.ShapeDtypeStruct((3, d, tp), BF16), jax.ShapeDtypeStruct((tp, d), F32),
           jax.ShapeDtypeStruct((d, tp), BF16),
           jax.ShapeDtypeStruct((8, 128), F32), jax.ShapeDtypeStruct((8, d), F32)],
        scratch_shapes=[pltpu.VMEM((3, TM, d), BF16),
                        pltpu.VMEM((HALO_A + TM, d), F32), pltpu.VMEM((HALO_B + TM, d), F32),
                        pltpu.VMEM((shl, d), F32), pltpu.VMEM((TM, d), F32),
                        pltpu.VMEM((RB, d), F32), pltpu.VMEM((8, d), F32)],
        compiler_params=_params(("arbitrary",)),
    )(s_pad, proj, target, w3, wa, wb, conv_a_b, ln_g, ln_b, b_a_out, final_g, norm_g)


def _mix_bwd(ds2, proj, ca, cb, ya, yb, w3, wa, wb, ln_g, ln_b):
    tp, d = ds2.shape
    nt = tp // TM
    nrb = TM // RB
    shl = TM + SHIFT_ROWS
    nt_dims = (((1,), (1,)), ((), ()))

    def body(ds2_ref, proj_ref, ca_ref, cb_ref, ya_ref, yb_ref, w3_ref, wa_ref, wb_ref, lng_ref, lnb_ref,
             dproj_ref, d3_ref, sm_ref, ext_d, ext_e, sh, dm_s, dpa_s, dpb_s, dua0_s, acc):
        step = pl.program_id(0)

        def split(k, rows):
            return proj_ref[rows, k * d:(k + 1) * d].astype(F32)

        def put(k, rows, val):
            dproj_ref[rows, k * d:(k + 1) * d] = val.astype(BF16)

        def accum(row, val):
            acc[row] += val.reshape(RB // 8, 8, d).sum(axis=0)

        @pl.when(step == 0)
        def _():
            ext_d[TM:TM + HALO_A, :] = jnp.zeros((HALO_A, d), F32)
            ext_e[TM:TM + HALO_B, :] = jnp.zeros((HALO_B, d), F32)
            acc[...] = jnp.zeros_like(acc)

        d3_ref[2] = ds2_ref[...].astype(BF16)
        dm_s[...] = lax.dot_general(d3_ref[2], w3_ref[2], nt_dims, preferred_element_type=F32)

        def gates(rb, carry):
            rows = _rows(rb)
            dm = dm_s[rows, :]
            sa = _sigmoid(split(7, rows))
            sb = _sigmoid(split(8, rows))
            ya_v = ya_ref[rows, :]
            yb_v = yb_ref[rows, :]
            put(7, rows, dm * ya_v * sa * (1.0 - sa))
            put(8, rows, dm * yb_v * sb * (1.0 - sb))
            dya = dm * sa
            accum(ROW_DBAO, dya)
            d3_ref[0, rows, :] = dya.astype(BF16)
            d3_ref[1, rows, :] = (dm * sb).astype(BF16)
            return carry
        lax.fori_loop(0, nrb, gates, 0)

        dpa_s[...] = lax.dot_general(d3_ref[0], w3_ref[0], nt_dims, preferred_element_type=F32)
        dpb_s[...] = lax.dot_general(d3_ref[1], w3_ref[1], nt_dims, preferred_element_type=F32)

        def branches(rb, carry):
            rows = _rows(rb)
            ca_v = ca_ref[rows, :]
            mu = jnp.mean(ca_v, axis=-1, keepdims=True)
            xc = ca_v - mu
            rstd = lax.rsqrt(jnp.mean(xc * xc, axis=-1, keepdims=True) + EPS)
            xhat = xc * rstd
            ln = xhat * lng_ref[...] + lnb_ref[...]
            sl = _sigmoid(ln)
            ua = ln * sl
            a_z = split(2, rows)
            sz = _sigmoid(a_z)
            dpa = dpa_s[rows, :]
            put(2, rows, dpa * ua * (sz * (1.0 + a_z * (1.0 - sz))))
            dln = dpa * (a_z * sz) * (sl * (1.0 + ln * (1.0 - sl)))
            accum(ROW_DLNG, dln * xhat)
            accum(ROW_DLNB, dln)
            dxh = dln * lng_ref[...]
            dca = rstd * (dxh - jnp.mean(dxh, axis=-1, keepdims=True)
                          - xhat * jnp.mean(dxh * xhat, axis=-1, keepdims=True))
            accum(ROW_DCAB, dca)
            ext_d[rows, :] = dca
            dua0_s[rows, :] = jnp.zeros((RB, d), F32)
            dm_s[rows, :] = split(0, rows) * _sigmoid(split(1, rows))
            b_z = split(6, rows)
            szb = _sigmoid(b_z)
            dpb = dpb_s[rows, :]
            b_b = split(3, rows)
            cb_v = cb_ref[rows, :]
            put(6, rows, dpb * (b_b * cb_v) * (szb * (1.0 + b_z * (1.0 - szb))))
            dub = dpb * (b_z * szb)
            put(3, rows, dub * cb_v)
            ext_e[rows, :] = dub * b_b
            return carry
        lax.fori_loop(0, nrb, branches, 0)

        for r, taps in _conv_a_taps(0, CONV_A - 1):
            if r == 0:
                src = ext_d
            else:
                sh[...] = ext_d[r:r + shl, :]
                src = sh

            def conv_t(rb, carry, src=src, taps=taps):
                rows = _rows(rb)
                ua0 = dm_s[rows, :]
                dua0 = dua0_s[rows, :]
                for q, lag in taps:
                    k = CONV_A - 1 - lag
                    slab = src[pl.ds(pl.multiple_of(rb * RB + 8 * q, 8), RB), :]
                    dua0 = dua0 + slab * wa_ref[k:k + 1, :]
                    accum(ROW_DWA + k, slab * ua0)
                dua0_s[rows, :] = dua0
                return carry
            lax.fori_loop(0, nrb, conv_t, 0)
        ext_d[TM:TM + HALO_A, :] = ext_d[0:HALO_A, :]

        dpb_s[...] = ext_e[0:TM, :] * wb_ref[CONV_B - 1:CONV_B, :]
        for lag in range(CONV_B):
            k = CONV_B - 1 - lag
            if lag > 0:
                sh[0:TM, :] = ext_e[lag:lag + TM, :]
                dpb_s[...] += sh[0:TM, :] * wb_ref[k:k + 1, :]
            src = ext_e if lag == 0 else sh

            def conv_b_w(rb, carry, src=src, k=k):
                rows = _rows(rb)
                accum(ROW_DWB + k, src[rows, :] * (split(4, rows) * split(5, rows)))
                return carry
            lax.fori_loop(0, nrb, conv_b_w, 0)
        ext_e[TM:TM + HALO_B, :] = ext_e[0:HALO_B, :]

        def inputs(rb, carry):
            rows = _rows(rb)
            dua0 = dua0_s[rows, :]
            a_val = split(0, rows)
            sg = _sigmoid(split(1, rows))
            put(0, rows, dua0 * sg)
            put(1, rows, dua0 * a_val * sg * (1.0 - sg))
            dcbin = dpb_s[rows, :]
            put(4, rows, dcbin * split(5, rows))
            put(5, rows, dcbin * split(4, rows))
            return carry
        lax.fori_loop(0, nrb, inputs, 0)

        @pl.when(step == nt - 1)
        def _():
            for row in range(SM_ROWS):
                sm_ref[row:row + 1, :] = jnp.sum(acc[row], axis=0, keepdims=True)

    rev = lambda i: (nt - 1 - i, 0)
    row_f32 = pl.BlockSpec((TM, d), rev)
    const = lambda shape: pl.BlockSpec(shape, lambda i: (0,) * len(shape))
    return pl.pallas_call(
        body, name="b1_mix", grid=(nt,),
        in_specs=[row_f32, pl.BlockSpec((TM, N_SPLIT * d), rev), row_f32, row_f32, row_f32, row_f32,
                  const((3, d, d)), const(wa.shape), const(wb.shape), const((1, d)), const((1, d))],
        out_specs=[pl.BlockSpec((TM, N_SPLIT * d), rev),
                   pl.BlockSpec((3, TM, d), lambda i: (0, nt - 1 - i, 0)),
                   const((SM_ROWS, d))],
        out_shape=[jax.ShapeDtypeStruct((tp, N_SPLIT * d), BF16), jax.ShapeDtypeStruct((3, tp, d), BF16),
                   jax.ShapeDtypeStruct((SM_ROWS, d), F32)],
        scratch_shapes=[pltpu.VMEM((TM + HALO_A, d), F32), pltpu.VMEM((TM + HALO_B, d), F32),
                        pltpu.VMEM((shl, d), F32), pltpu.VMEM((TM, d), F32), pltpu.VMEM((TM, d), F32),
                        pltpu.VMEM((TM, d), F32), pltpu.VMEM((TM, d), F32),
                        pltpu.VMEM((SM_ROWS, 8, d), F32)],
        compiler_params=_params(("arbitrary",)),
    )(ds2, proj, ca, cb, ya, yb, w3, wa, wb, ln_g, ln_b)


def kernel(x, meta_tokens, norm_g, w_in, conv_a_w, conv_a_b, ln_a_g, ln_a_b, w_a_out, b_a_out, conv_b_w, w_b_out, w_out, final_g, loss_target, m_meta_tokens, m_norm_g, m_w_in, m_conv_a_w, m_conv_a_b, m_ln_a_g, m_ln_a_b, m_w_a_out, m_b_a_out, m_conv_b_w, m_w_b_out, m_w_out, m_final_g, v_meta_tokens, v_norm_g, v_w_in, v_conv_a_w, v_conv_a_b, v_ln_a_g, v_ln_a_b, v_w_a_out, v_b_a_out, v_conv_b_w, v_w_b_out, v_w_out, v_final_g):
    seq, d = x.shape[1], x.shape[2]
    dc = meta_tokens.shape[1]
    sw = w_in.shape[2]
    rsh = w_a_out.shape[1]
    xi, yi, ci = _mesh_pos()
    me = 2 * xi + yi
    pos = jnp.stack([ci, me]).astype(jnp.int32)

    conv_rows = HALO_A + HALO_B + 8
    convs = jnp.concatenate([
        jnp.pad(conv_a_w[0], ((0, HALO_A - CONV_A), (0, 0))),
        jnp.pad(conv_b_w[0], ((0, HALO_B - CONV_B), (0, 0))), jnp.zeros((8, dc), F32)], axis=0)[None]
    w3_own = jnp.stack([w_a_out[0], w_b_out[0], w_out[0]])
    (metag,) = _all_gather([_place_own(meta_tokens[None], pos, F32, "place_meta")])
    meta_full = jnp.transpose(metag[0], (1, 0, 2)).reshape(N_META, N_CHIPS * dc)
    fg2 = final_g.reshape(1, d)

    first_tile = jnp.concatenate([jnp.zeros((TM - N_META, d), F32), meta_full], axis=0)
    s_pad = jnp.concatenate([first_tile, x[0]], axis=0)

    proj, (wg_in, wg3, convg) = _proj_fwd(s_pad, norm_g, [_place_own(w_in, pos, BF16, "place_in"),
                                                          _place_own(w3_own, pos, BF16, "place_sq"),
                                                          _place_own(convs, pos, F32, "place_conv")], pos)
    w3 = wg3.reshape(3, N_CHIPS * rsh, d)
    convg = jnp.transpose(convg[0], (1, 0, 2)).reshape(conv_rows, N_CHIPS * dc)
    wa_full = convg[0:HALO_A]
    wb_full = convg[HALO_A:HALO_A + HALO_B]
    ca, cb, ya, yb, abm_t, ds2, h_t, loss8, dfg8 = _mix_fwd(
        s_pad, proj, loss_target[0], w3, wa_full, wb_full, conv_a_b, ln_a_g, ln_a_b, b_a_out, fg2, norm_g)
    dproj, d3, sm = _mix_bwd(ds2, proj, ca, cb, ya, yb, w3, wa_full, wb_full, ln_a_g, ln_a_b)
    cw_sq = _col_block(d, 512)
    p32_sq, pbf_sq, _ = _dw_reduced(
        abm_t, d3, cw_sq, N_CHIPS, (3, N_CHIPS, rsh // 2, d), (None, N_CHIPS, rsh // 2, cw_sq),
        lambda u: (u // (d // cw_sq), 0, 0, u % (d // cw_sq)), None, "dw_square")
    cw_in = _col_block(sw, 768)
    p32_in, pbf_in, l_sq = _dw_reduced(
        h_t[None], dproj[None], cw_in, 1, (1, N_CHIPS, d // 2, sw), (None, None, d // 2, cw_in),
        lambda u: (0, u // (sw // cw_in), 0, u % (sw // cw_in)), pbf_sq, "dw_in")
    ds, dng8, l_in = _dh_bwd(dproj, wg_in, s_pad, ds2, norm_g, pbf_in)
    half_in = _sum_chips(p32_in, l_in, pos, "rs_sum_in")
    half_sq = _sum_chips(p32_sq, l_sq, pos, "rs_sum_sq")
    other_in, other_sq = _sibling_swap([half_in, half_sq])

    tail_row = lax.broadcasted_iota(jnp.int32, (8, d), 0)
    tail = jnp.where(tail_row == 0, dng8, jnp.where(tail_row == 1, dfg8,
                     jnp.where(tail_row == 2, loss8[0, 0], 0.0)))
    block = jnp.concatenate([sm, ds[TM - N_META:TM], tail], axis=0)
    red = _all_reduce_small(block)
    col = lax.dynamic_slice(red, (0, me * dc), (AR_ROWS, dc))
    g_small = {
        "meta_tokens": col[ROW_DMETA:ROW_DMETA + N_META],
        "norm_g": red[ROW_DNG:ROW_DNG + 1],
        "conv_a_w": col[ROW_DWA:ROW_DWA + CONV_A][None],
        "conv_a_b": red[ROW_DCAB:ROW_DCAB + 1],
        "ln_a_g": red[ROW_DLNG:ROW_DLNG + 1],
        "ln_a_b": red[ROW_DLNB:ROW_DLNB + 1],
        "b_a_out": red[ROW_DBAO:ROW_DBAO + 1],
        "conv_b_w": col[ROW_DWB:ROW_DWB + CONV_B][None],
        "final_g": red[ROW_DFG],
    }

    upd_in = _adam_halves([w_in], [m_w_in], [v_w_in], half_in, other_in, pos, "adam_in")
    upd_sq = _adam_halves([w_a_out, w_b_out, w_out], [m_w_a_out, m_w_b_out, m_w_out],
                          [v_w_a_out, v_w_b_out, v_w_out], half_sq, other_sq, pos, "adam_sq")
    small_w = {"meta_tokens": (meta_tokens, m_meta_tokens, v_meta_tokens), "norm_g": (norm_g, m_norm_g, v_norm_g),
               "conv_a_w": (conv_a_w, m_conv_a_w, v_conv_a_w), "conv_a_b": (conv_a_b, m_conv_a_b, v_conv_a_b),
               "ln_a_g": (ln_a_g, m_ln_a_g, v_ln_a_g), "ln_a_b": (ln_a_b, m_ln_a_b, v_ln_a_b),
               "b_a_out": (b_a_out, m_b_a_out, v_b_a_out), "conv_b_w": (conv_b_w, m_conv_b_w, v_conv_b_w),
               "final_g": (final_g, m_final_g, v_final_g)}
    names_small = list(small_w)
    as2d = lambda t: t.reshape(-1, t.shape[-1])
    upd_small = _adam_small([(as2d(small_w[k][0]), as2d(g_small[k]), as2d(small_w[k][1]), as2d(small_w[k][2]))
                             for k in names_small])

    grads, deltas, new_m, new_v = dict(g_small), {}, {}, {}
    for k, upd in zip(names_small, upd_small):
        deltas[k], new_m[k], new_v[k] = [t.reshape(small_w[k][0].shape) for t in upd]
    grads["w_in"], deltas["w_in"], new_m["w_in"], new_v["w_in"] = upd_in
    for idx, k in enumerate(["w_a_out", "w_b_out", "w_out"]):
        grads[k], deltas[k], new_m[k], new_v[k] = upd_sq[4 * idx:4 * idx + 4]

    loss = red[ROW_LOSS, 0]
    grad_x = ds[TM:][None]
    order = ["meta_tokens", "norm_g", "w_in", "conv_a_w", "conv_a_b", "ln_a_g", "ln_a_b", "w_a_out", "b_a_out",
             "conv_b_w", "w_b_out", "w_out", "final_g"]
    return (loss, grad_x, *[grads[k] for k in order], *[deltas[k] for k in order],
            *[new_m[k] for k in order], *[new_v[k] for k in order])
```

```python
import functools

import jax
import jax.numpy as jnp
from jax import lax
from jax.experimental import pallas as pl
from jax.experimental.pallas import tpu as pltpu

F32 = jnp.float32
BF16 = jnp.bfloat16
MESH = pl.DeviceIdType.MESH

EPS = 1e-6
N_META = 16
N_SPLIT = 9
CONV_A = 31
CONV_B = 3
HALO_A = 32
HALO_B = 8
SHIFT_ROWS = 24
TM = 256
RB = 64
N_ROW_TILES_BIG = 8
ROW_BLOCK = 256
N_CHIPS = 4
VMEM_LIMIT = 56 * 1024 * 1024
VMEM_LIMIT_BIG = 62 * 1024 * 1024

ADAM_LR = 0.001
ADAM_B1 = 0.9
ADAM_B2 = 0.999
ADAM_EPS = 1e-08
ADAM_WD = 0.01
ADAM_STEP = 10

ROW_DWA = 0
ROW_DWB = 32
ROW_DCAB = 40
ROW_DLNG = 41
ROW_DLNB = 42
ROW_DBAO = 43
SM_ROWS = 48
ROW_DMETA = 48
ROW_DNG = 64
ROW_DFG = 65
ROW_LOSS = 66
AR_ROWS = 72


def _sigmoid(v):
    return 0.5 * jnp.tanh(0.5 * v) + 0.5


def _params(sem, **kw):
    return pltpu.CompilerParams(dimension_semantics=sem, vmem_limit_bytes=VMEM_LIMIT, **kw)


def _rows(rb):
    return pl.ds(pl.multiple_of(rb * RB, RB), RB)


def _mesh_pos():
    x, y, c = lax.axis_index("x"), lax.axis_index("y"), lax.axis_index("c")
    return x, y, c


def _half(ref, j, c):
    h = ref.shape[2] // 2
    return ref.at[:, j, pl.ds(c * h, h), :]


def _place_own(shard, pos, dtype, name):
    s, r, c = shard.shape
    rb = ROW_BLOCK if r % ROW_BLOCK == 0 else r

    def body(pos_ref, x_ref, o_ref):
        o_ref[...] = x_ref[...].astype(dtype)

    return pl.pallas_call(
        body, name=name,
        grid_spec=pltpu.PrefetchScalarGridSpec(
            num_scalar_prefetch=1, grid=(s, r // rb),
            in_specs=[pl.BlockSpec((None, rb, c), lambda si, b, pos_ref: (si, b, 0))],
            out_specs=pl.BlockSpec((None, None, rb, c), lambda si, b, pos_ref: (si, pos_ref[1], b, 0))),
        out_shape=jax.ShapeDtypeStruct((s, N_CHIPS, r, c), dtype),
        compiler_params=_params(("arbitrary",) * 2),
    )(pos, shard)


def _all_gather(bufs):
    n = len(bufs)

    def body(*refs):
        outs = refs[n:2 * n]
        send_sems, recv_sems = refs[2 * n:]
        x, y, c = _mesh_pos()
        me = 2 * x + y
        sibling = (x, y, 1 - c)
        chips = [(1 - x, y), (x, 1 - y), (1 - x, 1 - y)]

        def remote(a, k, piece_src, piece_dst, to):
            return pltpu.make_async_remote_copy(
                src_ref=piece_src, dst_ref=piece_dst, send_sem=send_sems.at[6 * a + k],
                recv_sem=recv_sems.at[6 * a + k], device_id=to, device_id_type=MESH)

        sends = []
        for a in range(n):
            mine = _half(outs[a], me, c)
            for k, (px, py) in enumerate(chips):
                sends.append(remote(a, k, mine, mine, (px, py, c)))
        for cp in sends:
            cp.start()
        for a in range(n):
            for k, (px, py) in enumerate(chips):
                piece = _half(outs[a], 2 * px + py, c)
                remote(a, k, piece, piece, (px, py, c)).wait_recv()
                fwd = remote(a, 3 + k, piece, piece, sibling)
                fwd.start()
                sends.append(fwd)
        for a in range(n):
            for k, (px, py) in enumerate(chips):
                piece = _half(outs[a], 2 * px + py, 1 - c)
                remote(a, 3 + k, piece, piece, sibling).wait_recv()
        for cp in sends:
            cp.wait_send()

    any_spec = pl.BlockSpec(memory_space=pl.ANY)
    return pl.pallas_call(
        body, name="ag_weights",
        in_specs=[any_spec] * n, out_specs=[any_spec] * n,
        out_shape=[jax.ShapeDtypeStruct(b.shape, b.dtype) for b in bufs],
        input_output_aliases={a: a for a in range(n)},
        scratch_shapes=[pltpu.SemaphoreType.DMA((6 * n,)), pltpu.SemaphoreType.DMA((6 * n,))],
    )(*bufs)


class _Exchange:
    def __init__(self, sends, recvs):
        self.sends, self.recvs = sends, recvs

    def start(self):
        for cp in self.sends:
            cp.start()

    def finish(self):
        for cp in self.recvs:
            cp.wait_recv()
        for cp in self.sends:
            cp.wait_send()


def _chip_exchange(part_ref, land_ref, send_sems, recv_sems):
    x, y, c = _mesh_pos()
    me = 2 * x + y
    sends, recvs = [], []
    for k, (px, py) in enumerate([(1 - x, y), (x, 1 - y), (1 - x, 1 - y)]):
        sems = dict(send_sem=send_sems.at[k], recv_sem=recv_sems.at[k], device_id=(px, py, c), device_id_type=MESH)
        sends.append(pltpu.make_async_remote_copy(
            src_ref=part_ref.at[:, 2 * px + py], dst_ref=land_ref.at[:, me], **sems))
        landed = land_ref.at[:, 2 * px + py]
        recvs.append(pltpu.make_async_remote_copy(src_ref=landed, dst_ref=landed, **sems))
    return _Exchange(sends, recvs)


def _sibling_swap(halves):
    n = len(halves)

    def body(*refs):
        ins, outs = refs[:n], refs[n:2 * n]
        send_sems, recv_sems = refs[2 * n:]
        x, y, c = _mesh_pos()
        copies = [pltpu.make_async_remote_copy(
            src_ref=ins[a], dst_ref=outs[a], send_sem=send_sems.at[a], recv_sem=recv_sems.at[a],
            device_id=(x, y, 1 - c), device_id_type=MESH) for a in range(n)]
        for cp in copies:
            cp.start()
        for cp in copies:
            cp.wait()

    any_spec = pl.BlockSpec(memory_space=pl.ANY)
    return pl.pallas_call(
        body, name="rs_swap",
        in_specs=[any_spec] * n, out_specs=[any_spec] * n,
        out_shape=[jax.ShapeDtypeStruct(h.shape, h.dtype) for h in halves],
        scratch_shapes=[pltpu.SemaphoreType.DMA((n,)), pltpu.SemaphoreType.DMA((n,))],
    )(*halves)


class _SmallAllReduce:
    def __init__(self, x_ref, out_ref, sib_ref, part_ref, peers_ref, send_sems, recv_sems):
        self.x_ref, self.out_ref, self.sib_ref, self.part_ref, self.peers_ref = x_ref, out_ref, sib_ref, part_ref, peers_ref
        x, y, c = _mesh_pos()
        self.me = 2 * x + y
        self.swap = pltpu.make_async_remote_copy(
            src_ref=x_ref, dst_ref=sib_ref, send_sem=send_sems.at[0], recv_sem=recv_sems.at[0],
            device_id=(x, y, 1 - c), device_id_type=MESH)
        self.sends, self.recvs = [], []
        for k, (px, py) in enumerate([(1 - x, y), (x, 1 - y), (1 - x, 1 - y)]):
            sems = dict(send_sem=send_sems.at[1 + k], recv_sem=recv_sems.at[1 + k],
                        device_id=(px, py, c), device_id_type=MESH)
            self.sends.append(pltpu.make_async_remote_copy(src_ref=part_ref, dst_ref=peers_ref.at[self.me], **sems))
            landed = peers_ref.at[2 * px + py]
            self.recvs.append(pltpu.make_async_remote_copy(src_ref=landed, dst_ref=landed, **sems))

    @staticmethod
    def scratch(rows, d):
        return [pltpu.VMEM((rows, d), F32), pltpu.VMEM((rows, d), F32), pltpu.VMEM((N_CHIPS, rows, d), F32),
                pltpu.SemaphoreType.DMA((4,)), pltpu.SemaphoreType.DMA((4,))]

    def start(self):
        self.swap.start()

    def between_chips(self):
        self.swap.wait()
        self.part_ref[...] = self.x_ref[...] + self.sib_ref[...]
        self.peers_ref[self.me] = self.part_ref[...]
        for cp in self.sends:
            cp.start()

    def finish(self):
        for cp in self.recvs:
            cp.wait_recv()
        for cp in self.sends:
            cp.wait_send()
        p = self.peers_ref
        self.out_ref[...] = ((p[0] + p[1]) + p[2]) + p[3]


def _sum_chips(p32, landed, pos, name):
    s, nch, h, c = p32.shape
    hb = min(h, ROW_BLOCK)

    def body(pos_ref, p_ref, l1_ref, l2_ref, l3_ref, out_ref):
        out_ref[...] = ((p_ref[...] + l1_ref[...].astype(F32)) + l2_ref[...].astype(F32)) + l3_ref[...].astype(F32)

    def slot(k):
        return pl.BlockSpec((None, None, hb, c), lambda si, b, pos_ref: (si, (pos_ref[1] + k) % N_CHIPS, b, 0))

    return pl.pallas_call(
        body, name=name,
        grid_spec=pltpu.PrefetchScalarGridSpec(
            num_scalar_prefetch=1, grid=(s, h // hb),
            in_specs=[slot(0), slot(1), slot(2), slot(3)],
            out_specs=pl.BlockSpec((None, hb, c), lambda si, b, pos_ref: (si, b, 0))),
        out_shape=jax.ShapeDtypeStruct((s, h, c), F32),
        compiler_params=_params(("arbitrary",) * 2),
    )(pos, p32, landed, landed, landed)


def _adamw(w, g, m, v):
    m = ADAM_B1 * m + (1.0 - ADAM_B1) * g
    v = ADAM_B2 * v + (1.0 - ADAM_B2) * (g * g)
    m_hat = m / (1.0 - ADAM_B1 ** ADAM_STEP)
    v_hat = v / (1.0 - ADAM_B2 ** ADAM_STEP)
    delta = -ADAM_LR * (m_hat / (jnp.sqrt(v_hat) + ADAM_EPS) + ADAM_WD * w)
    return delta, m, v


def _adam_halves(ws, ms, vs, g_own, g_recv, pos, name, small=None):
    n = len(ws)
    _, r, c = ws[0].shape
    h = r // 2
    rb = min(h, ROW_BLOCK)
    nb = h // rb
    n_steps = 2 * nb
    assert n_steps >= 2

    def body(pos_ref, *refs):
        w_refs, m_refs, v_refs = refs[:n], refs[n:2 * n], refs[2 * n:3 * n]
        go_ref, gr_ref = refs[3 * n:3 * n + 2]
        outs = refs[3 * n + 2 + (small is not None):7 * n + 2 + (small is not None)]
        step = pl.program_id(0) * nb + pl.program_id(1)
        if small is not None:
            reduce = _SmallAllReduce(refs[3 * n + 2], *refs[7 * n + 3:])
            pl.when(step == 0)(reduce.start)
            pl.when(step == 1)(reduce.between_chips)
        mine = pl.program_id(0) == pos_ref[0]
        for a in range(n):
            g = jnp.where(mine, go_ref[a], gr_ref[a])
            delta, m, v = _adamw(w_refs[a][...], g, m_refs[a][...], v_refs[a][...])
            outs[4 * a][...], outs[4 * a + 1][...], outs[4 * a + 2][...], outs[4 * a + 3][...] = g, delta, m, v
        if small is not None:
            pl.when(step == n_steps - 1)(reduce.finish)

    spec_w = pl.BlockSpec((None, rb, c), lambda hf, b, pos_ref: (0, hf * nb + b, 0))
    spec_g = pl.BlockSpec((n, rb, c), lambda hf, b, pos_ref: (0, b, 0))
    extra, extra_specs, extra_scratch = [], [], []
    if small is not None:
        extra = [small]
        extra_specs = [pl.BlockSpec(small.shape, lambda hf, b, pos_ref: (0, 0))]
        extra_scratch = _SmallAllReduce.scratch(*small.shape)
    outs = pl.pallas_call(
        body, name=name,
        grid_spec=pltpu.PrefetchScalarGridSpec(
            num_scalar_prefetch=1, grid=(2, nb), in_specs=[spec_w] * (3 * n) + [spec_g] * 2 + extra_specs,
            out_specs=[spec_w] * (4 * n) + extra_specs, scratch_shapes=extra_scratch),
        out_shape=[jax.ShapeDtypeStruct((1, r, c), F32)] * (4 * n)
        + [jax.ShapeDtypeStruct(e.shape, F32) for e in extra],
        compiler_params=_params(("arbitrary",) * 2),
    )(pos, *ws, *ms, *vs, g_own, g_recv, *extra)
    return outs


def _adam_small(items):
    n = len(items)

    def body(*refs):
        ins, outs = refs[:4 * n], refs[4 * n:]
        for a in range(n):
            w_ref, g_ref, m_ref, v_ref = ins[4 * a:4 * a + 4]
            d, m, v = _adamw(w_ref[...], g_ref[...], m_ref[...], v_ref[...])
            outs[3 * a][...] = d
            outs[3 * a + 1][...] = m
            outs[3 * a + 2][...] = v

    vm = pl.BlockSpec(memory_space=pltpu.VMEM)
    flat = [t for it in items for t in it]
    outs = pl.pallas_call(
        body, name="adam_small", in_specs=[vm] * (4 * n), out_specs=[vm] * (3 * n),
        out_shape=[jax.ShapeDtypeStruct(it[0].shape, F32) for it in items for _ in range(3)],
    )(*flat)
    return [tuple(outs[3 * a:3 * a + 3]) for a in range(n)]


def _shard_of_step(js, me):
    flip = jnp.where(js == 1, 2, jnp.where(js == 2, 1, jnp.where(js == 3, 3, 0)))
    return lax.bitwise_xor(me, flip)


def _proj_fwd(s_pad, norm_g, bufs, pos):
    tp, d = s_pad.shape
    _, nsh, _, sw = bufs[0].shape
    tmb = tp // N_ROW_TILES_BIG
    n = len(bufs)

    def body(pos_ref, s_ref, g_ref, *refs):
        proj_ref = refs[n]
        gbufs = refs[n + 1:2 * n + 1]
        wbuf, wsem, send_sems, recv_sems = refs[2 * n + 1:]
        x, y, c = _mesh_pos()
        me = 2 * x + y
        sibling = (x, y, 1 - c)
        chips = [(1 - x, y), (x, 1 - y), (1 - x, 1 - y)]
        js, i = pl.program_id(0), pl.program_id(1)

        def remote(a, k, piece, to):
            return pltpu.make_async_remote_copy(
                src_ref=piece, dst_ref=piece, send_sem=send_sems.at[6 * a + k],
                recv_sem=recv_sems.at[6 * a + k], device_id=to, device_id_type=MESH)

        def fetch(chip):
            cp = pltpu.make_async_copy(gbufs[0].at[0, chip], wbuf, wsem)
            cp.start()
            cp.wait()

        def take(a, k):
            px, py = chips[k]
            remote(a, k, _half(gbufs[a], 2 * px + py, c), (px, py, c)).wait_recv()
            remote(a, 3 + k, _half(gbufs[a], 2 * px + py, c), sibling).start()
            remote(a, 3 + k, _half(gbufs[a], 2 * px + py, 1 - c), sibling).wait_recv()

        @pl.when((js == 0) & (i == 0))
        def _():
            for a in range(n):
                for k, (px, py) in enumerate(chips):
                    remote(a, k, _half(gbufs[a], me, c), (px, py, c)).start()
            fetch(me)

        for k, (px, py) in enumerate(chips):
            @pl.when((js == k + 1) & (i == 0))
            def _(k=k, px=px, py=py):
                take(0, k)
                fetch(2 * px + py)

        s = s_ref[...]
        r = lax.rsqrt(jnp.mean(s * s, axis=-1, keepdims=True) + EPS)
        h = (s * r * g_ref[...]).astype(BF16)
        proj_ref[...] = jnp.dot(h, wbuf[...], preferred_element_type=F32).astype(BF16)

        @pl.when((js == nsh - 1) & (i == N_ROW_TILES_BIG - 1))
        def _():
            for a in range(1, n):
                for k in range(len(chips)):
                    take(a, k)
            for a in range(n):
                for k, (px, py) in enumerate(chips):
                    remote(a, k, _half(gbufs[a], me, c), (px, py, c)).wait_send()
                    remote(a, 3 + k, _half(gbufs[a], 2 * px + py, c), sibling).wait_send()

    any_spec = pl.BlockSpec(memory_space=pl.ANY)
    outs = pl.pallas_call(
        body, name="f1_proj",
        grid_spec=pltpu.PrefetchScalarGridSpec(
            num_scalar_prefetch=1, grid=(nsh, N_ROW_TILES_BIG),
            in_specs=[pl.BlockSpec((tmb, d), lambda js, i, pos_ref: (i, 0)),
                      pl.BlockSpec((1, d), lambda js, i, pos_ref: (0, 0))] + [any_spec] * n,
            out_specs=[pl.BlockSpec((tmb, sw), lambda js, i, pos_ref: (i, _shard_of_step(js, pos_ref[1])))]
            + [any_spec] * n,
            scratch_shapes=[pltpu.VMEM((d, sw), BF16), pltpu.SemaphoreType.DMA,
                            pltpu.SemaphoreType.DMA((6 * n,)), pltpu.SemaphoreType.DMA((6 * n,))]),
        out_shape=[jax.ShapeDtypeStruct((tp, nsh * sw), BF16)]
        + [jax.ShapeDtypeStruct(b.shape, b.dtype) for b in bufs],
        input_output_aliases={3 + a: 1 + a for a in range(n)},
        compiler_params=_params(("arbitrary", "arbitrary")),
    )(pos, s_pad, norm_g, *bufs)
    return outs[0], outs[1:]


def _dh_bwd(dproj, wg_in, s_pad, ds2, norm_g, part):
    tp, d = s_pad.shape
    _, nsh, _, sw = wg_in.shape
    tmb = tp // N_ROW_TILES_BIG

    def body(dp_ref, w_hbm, s_ref, ds2_ref, g_ref, part_ref, ds_ref, dng_ref, land_ref, wbuf, gacc,
             wsem, send_sems, recv_sems):
        exchange = _chip_exchange(part_ref, land_ref, send_sems, recv_sems)
        i = pl.program_id(0)

        @pl.when(i == 0)
        def _():
            exchange.start()
            gacc[...] = jnp.zeros_like(gacc)
            whole = pltpu.make_async_copy(w_hbm.at[0], wbuf, wsem)
            whole.start()
            whole.wait()

        dh = None
        for j in range(nsh):
            part = lax.dot_general(dp_ref[:, j * sw:(j + 1) * sw], wbuf[j], (((1,), (1,)), ((), ())),
                                   preferred_element_type=F32)
            dh = part if dh is None else dh + part
        s = s_ref[...]
        r = lax.rsqrt(jnp.mean(s * s, axis=-1, keepdims=True) + EPS)
        gacc[...] += (dh * s * r).reshape(tmb // 8, 8, d).sum(axis=0)
        t = dh * g_ref[...]
        ds_ref[...] = ds2_ref[...] + r * t - s * (r * r * r) * jnp.mean(t * s, axis=-1, keepdims=True)

        @pl.when(i == N_ROW_TILES_BIG - 1)
        def _():
            dng_ref[...] = jnp.broadcast_to(jnp.sum(gacc[...], axis=0, keepdims=True), (8, d))
            exchange.finish()

    any_spec = pl.BlockSpec(memory_space=pl.ANY)
    return pl.pallas_call(
        body, name="b2_dh", grid=(N_ROW_TILES_BIG,),
        in_specs=[pl.BlockSpec((tmb, nsh * sw), lambda i: (i, 0)), any_spec,
                  pl.BlockSpec((tmb, d), lambda i: (i, 0)),
                  pl.BlockSpec((tmb, d), lambda i: (i, 0)),
                  pl.BlockSpec((1, d), lambda i: (0, 0)), any_spec],
        out_specs=[pl.BlockSpec((tmb, d), lambda i: (i, 0)),
                   pl.BlockSpec((8, d), lambda i: (0, 0)), any_spec],
        out_shape=[jax.ShapeDtypeStruct((tp, d), F32), jax.ShapeDtypeStruct((8, d), F32),
                   jax.ShapeDtypeStruct(part.shape, part.dtype)],
        scratch_shapes=[pltpu.VMEM((nsh, d, sw), BF16), pltpu.VMEM((8, d), F32), pltpu.SemaphoreType.DMA,
                        pltpu.SemaphoreType.DMA((3,)), pltpu.SemaphoreType.DMA((3,))],
        compiler_params=pltpu.CompilerParams(dimension_semantics=("arbitrary",),
                                             vmem_limit_bytes=VMEM_LIMIT_BIG),
    )(dproj, wg_in, s_pad, ds2, norm_g, part)


def _col_block(width, cap):
    return max(b for b in range(128, cap + 1, 128) if width % b == 0)


def _dw_reduced(lhs_t, rhs, cw, groups, out_dims, out_block, out_index, carried, name):
    na, d, tp = lhs_t.shape
    per_a = rhs.shape[2] // cw
    nblk = na * per_a
    rg = d // groups
    hh = rg // 2

    def body(*refs):
        if carried is None:
            l_ref, r_ref, p32_ref, pbf_ref, res, rbuf, send_sems, recv_sems = refs
            exchange = _Exchange([], [])
        else:
            (l_ref, r_ref, part_ref, p32_ref, pbf_ref, land_ref, res, rbuf, send_sems, recv_sems,
             xsend, xrecv) = refs
            exchange = _chip_exchange(part_ref, land_ref, xsend, xrecv)
        x, y, c = _mesh_pos()
        t = pl.program_id(0)
        u = jnp.maximum(t - 1, 0)

        def to_sibling(blk):
            return pltpu.make_async_remote_copy(
                src_ref=res.at[blk % 2, :, pl.ds((1 - c) * hh, hh), :], dst_ref=rbuf.at[blk % 2],
                send_sem=send_sems.at[blk], recv_sem=recv_sems.at[blk],
                device_id=(x, y, 1 - c), device_id_type=MESH)

        @pl.when(t == 0)
        def _():
            exchange.start()

        @pl.when(t < nblk)
        def _():
            res[t % 2] = jnp.dot(l_ref[...], r_ref[...], preferred_element_type=F32).reshape(groups, rg, cw)

        @pl.when(t >= 1)
        def _():
            to_sibling(u).wait_recv()
            p = res[u % 2, :, pl.ds(c * hh, hh), :] + rbuf[u % 2]
            p32_ref[...] = p.reshape(p32_ref.shape)
            pbf_ref[...] = p.reshape(pbf_ref.shape).astype(BF16)

        @pl.when(t < nblk)
        def _():
            to_sibling(t).start()

        @pl.when(t >= 1)
        def _():
            to_sibling(u).wait_send()

        @pl.when(t == nblk)
        def _():
            exchange.finish()

    any_spec = pl.BlockSpec(memory_space=pl.ANY)
    last = nblk - 1
    out_spec = pl.BlockSpec(out_block, lambda t: out_index(jnp.maximum(t - 1, 0)))
    extra = [] if carried is None else [carried]
    outs = pl.pallas_call(
        body, name=name, grid=(nblk + 1,),
        in_specs=[pl.BlockSpec((None, d, tp), lambda t: (jnp.minimum(t, last) // per_a, 0, 0)),
                  pl.BlockSpec((None, tp, cw), lambda t: (jnp.minimum(t, last) // per_a, 0,
                                                          jnp.minimum(t, last) % per_a))]
        + [any_spec] * len(extra),
        out_specs=[out_spec, out_spec] + [any_spec] * len(extra),
        out_shape=[jax.ShapeDtypeStruct(out_dims, F32), jax.ShapeDtypeStruct(out_dims, BF16)]
        + [jax.ShapeDtypeStruct(e.shape, e.dtype) for e in extra],
        scratch_shapes=[pltpu.VMEM((2, groups, rg, cw), F32), pltpu.VMEM((2, groups, hh, cw), F32),
                        pltpu.SemaphoreType.DMA((nblk,)), pltpu.SemaphoreType.DMA((nblk,))]
        + [pltpu.SemaphoreType.DMA((3,)), pltpu.SemaphoreType.DMA((3,))] * len(extra),
        compiler_params=_params(("arbitrary",)),
    )(lhs_t, rhs, *extra)
    return outs[0], outs[1], (outs[2] if extra else None)


def _conv_a_taps(first_lag, last_lag):
    out = []
    for r in range(8):
        taps = [(q, 8 * q + r) for q in range(5) if first_lag <= 8 * q + r <= last_lag]
        if taps:
            out.append((r, taps))
    return out


def _mix_fwd(s_pad, proj, target, w3, wa, wb, conv_a_b, ln_g, ln_b, b_a_out, final_g, norm_g):
    tp, d = s_pad.shape
    nt = tp // TM
    nrb = TM // RB
    shl = TM + SHIFT_ROWS

    def body(s_ref, proj_ref, tgt_ref, w3_ref, wa_ref, wb_ref, cab_ref, lng_ref, lnb_ref, bao_ref, fg_ref, ng_ref,
             ca_ref, cb_ref, ya_ref, yb_ref, abmt_ref, ds2_ref, ht_ref, loss_ref, dfg_ref,
             abm_ref, ext_a, ext_b, sh, s2_s, lacc, gacc):
        i = pl.program_id(0)

        def split(k, rows):
            return proj_ref[rows, k * d:(k + 1) * d].astype(F32)

        s_in = s_ref[...]
        h = s_in * lax.rsqrt(jnp.mean(s_in * s_in, axis=-1, keepdims=True) + EPS) * ng_ref[...]
        ht_ref[...] = h.T.astype(BF16)

        @pl.when(i == 0)
        def _():
            ext_a[0:HALO_A, :] = jnp.zeros((HALO_A, d), F32)
            ext_b[0:HALO_B, :] = jnp.zeros((HALO_B, d), F32)
            lacc[...] = jnp.zeros_like(lacc)
            gacc[...] = jnp.zeros_like(gacc)

        def conv_in(rb, carry):
            rows = _rows(rb)
            ua0 = split(0, rows) * _sigmoid(split(1, rows))
            ext_a[pl.ds(pl.multiple_of(HALO_A + rb * RB, 8), RB), :] = ua0
            ext_b[pl.ds(pl.multiple_of(HALO_B + rb * RB, 8), RB), :] = split(4, rows) * split(5, rows)
            ca_ref[rows, :] = jnp.broadcast_to(cab_ref[...], (RB, d))
            return carry
        lax.fori_loop(0, nrb, conv_in, 0)

        for r, taps in _conv_a_taps(HALO_A - CONV_A + 1, HALO_A):
            if r == 0:
                src = ext_a
            else:
                sh[...] = ext_a[r:r + shl, :]
                src = sh

            def conv_acc(rb, carry, src=src, taps=taps):
                rows = _rows(rb)
                acc = ca_ref[rows, :]
                for q, lag in taps:
                    k = lag - (HALO_A - CONV_A + 1)
                    acc = acc + src[pl.ds(pl.multiple_of(rb * RB + 8 * q, 8), RB), :] * wa_ref[k:k + 1, :]
                ca_ref[rows, :] = acc
                return carry
            lax.fori_loop(0, nrb, conv_acc, 0)
        ext_a[0:HALO_A, :] = ext_a[TM:TM + HALO_A, :]

        cb_ref[...] = ext_b[HALO_B:HALO_B + TM, :] * wb_ref[2:3, :]
        for k in range(CONV_B - 1):
            off = HALO_B - CONV_B + 1 + k
            sh[0:TM, :] = ext_b[off:off + TM, :]
            cb_ref[...] += sh[0:TM, :] * wb_ref[k:k + 1, :]
        ext_b[0:HALO_B, :] = ext_b[TM:TM + HALO_B, :]

        def branches(rb, carry):
            rows = _rows(rb)
            ca = ca_ref[rows, :]
            mu = jnp.mean(ca, axis=-1, keepdims=True)
            xc = ca - mu
            rstd = lax.rsqrt(jnp.mean(xc * xc, axis=-1, keepdims=True) + EPS)
            ln = xc * rstd * lng_ref[...] + lnb_ref[...]
            ua = ln * _sigmoid(ln)
            a_z = split(2, rows)
            abm_ref[0, rows, :] = (ua * (a_z * _sigmoid(a_z))).astype(BF16)
            b_z = split(6, rows)
            ub = split(3, rows) * cb_ref[rows, :]
            abm_ref[1, rows, :] = (ub * (b_z * _sigmoid(b_z))).astype(BF16)
            return carry
        lax.fori_loop(0, nrb, branches, 0)

        ya_ref[...] = jnp.dot(abm_ref[0], w3_ref[0], preferred_element_type=F32) + bao_ref[...]
        yb_ref[...] = jnp.dot(abm_ref[1], w3_ref[1], preferred_element_type=F32)

        def merge(rb, carry):
            rows = _rows(rb)
            m = _sigmoid(split(7, rows)) * ya_ref[rows, :] + _sigmoid(split(8, rows)) * yb_ref[rows, :]
            abm_ref[2, rows, :] = m.astype(BF16)
            return carry
        lax.fori_loop(0, nrb, merge, 0)

        s2_s[...] = s_ref[...] + jnp.dot(abm_ref[2], w3_ref[2], preferred_element_type=F32)
        for k in range(3):
            abmt_ref[k] = abm_ref[k].astype(F32).T.astype(BF16)
        live = (i > 0).astype(F32)

        def head(rb, carry):
            rows = _rows(rb)
            s2 = s2_s[rows, :]
            r2 = lax.rsqrt(jnp.mean(s2 * s2, axis=-1, keepdims=True) + EPS)
            diff = (s2 * r2 * fg_ref[...] - tgt_ref[rows, :]) * live
            lacc[...] += diff * diff
            dy = diff * (1.0 / d)
            gacc[...] += (dy * s2 * r2).reshape(RB // 8, 8, d).sum(axis=0)
            t = dy * fg_ref[...]
            ds2_ref[rows, :] = r2 * t - s2 * (r2 * r2 * r2) * jnp.mean(t * s2, axis=-1, keepdims=True)
            return carry
        lax.fori_loop(0, nrb, head, 0)

        @pl.when(i == nt - 1)
        def _():
            loss_ref[...] = jnp.broadcast_to(0.5 * jnp.sum(lacc[...]) * (1.0 / d), (8, 128))
            dfg_ref[...] = jnp.broadcast_to(jnp.sum(gacc[...], axis=0, keepdims=True), (8, d))

    row_f32 = pl.BlockSpec((TM, d), lambda i: (i, 0))
    const = lambda shape: pl.BlockSpec(shape, lambda i: (0,) * len(shape))
    return pl.pallas_call(
        body, name="f2_mix", grid=(nt,),
        in_specs=[row_f32,
                  pl.BlockSpec((TM, N_SPLIT * d), lambda i: (i, 0)),
                  pl.BlockSpec((TM, d), lambda i: (jnp.maximum(i - 1, 0), 0)),
                  const((3, d, d)), const(wa.shape), const(wb.shape)] + [const((1, d))] * 6,
        out_specs=[row_f32, row_f32, row_f32, row_f32,
                   pl.BlockSpec((3, d, TM), lambda i: (0, 0, i)),
                   row_f32, pl.BlockSpec((d, TM), lambda i: (0, i)), const((8, 128)), const((8, d))],
        out_shape=[jax.ShapeDtypeStruct((tp, d), F32)] * 4
        + [jax.ShapeDtypeStruct((3, d, tp), BF16), jax.ShapeDtypeStruct((tp, d), F32),
           jax.ShapeDtypeStruct((d, tp), BF16),
           jax.ShapeDtypeStruct((8, 128), F32), jax.ShapeDtypeStruct((8, d), F32)],
        scratch_shapes=[pltpu.VMEM((3, TM, d), BF16),
                        pltpu.VMEM((HALO_A + TM, d), F32), pltpu.VMEM((HALO_B + TM, d), F32),
                        pltpu.VMEM((shl, d), F32), pltpu.VMEM((TM, d), F32),
                        pltpu.VMEM((RB, d), F32), pltpu.VMEM((8, d), F32)],
        compiler_params=_params(("arbitrary",)),
    )(s_pad, proj, target, w3, wa, wb, conv_a_b, ln_g, ln_b, b_a_out, final_g, norm_g)


def _mix_bwd(ds2, proj, ca, cb, ya, yb, w3, wa, wb, ln_g, ln_b):
    tp, d = ds2.shape
    nt = tp // TM
    nrb = TM // RB
    shl = TM + SHIFT_ROWS
    nt_dims = (((1,), (1,)), ((), ()))

    def body(ds2_ref, proj_ref, ca_ref, cb_ref, ya_ref, yb_ref, w3_ref, wa_ref, wb_ref, lng_ref, lnb_ref,
             dproj_ref, d3_ref, sm_ref, ext_d, ext_e, sh, dm_s, dpa_s, dpb_s, dua0_s, acc):
        step = pl.program_id(0)

        def split(k, rows):
            return proj_ref[rows, k * d:(k + 1) * d].astype(F32)

        def put(k, rows, val):
            dproj_ref[rows, k * d:(k + 1) * d] = val.astype(BF16)

        def accum(row, val):
            acc[row] += val.reshape(RB // 8, 8, d).sum(axis=0)

        @pl.when(step == 0)
        def _():
            ext_d[TM:TM + HALO_A, :] = jnp.zeros((HALO_A, d), F32)
            ext_e[TM:TM + HALO_B, :] = jnp.zeros((HALO_B, d), F32)
            acc[...] = jnp.zeros_like(acc)

        d3_ref[2] = ds2_ref[...].astype(BF16)
        dm_s[...] = lax.dot_general(d3_ref[2], w3_ref[2], nt_dims, preferred_element_type=F32)

        def gates(rb, carry):
            rows = _rows(rb)
            dm = dm_s[rows, :]
            sa = _sigmoid(split(7, rows))
            sb = _sigmoid(split(8, rows))
            ya_v = ya_ref[rows, :]
            yb_v = yb_ref[rows, :]
            put(7, rows, dm * ya_v * sa * (1.0 - sa))
            put(8, rows, dm * yb_v * sb * (1.0 - sb))
            dya = dm * sa
            accum(ROW_DBAO, dya)
            d3_ref[0, rows, :] = dya.astype(BF16)
            d3_ref[1, rows, :] = (dm * sb).astype(BF16)
            return carry
        lax.fori_loop(0, nrb, gates, 0)

        dpa_s[...] = lax.dot_general(d3_ref[0], w3_ref[0], nt_dims, preferred_element_type=F32)
        dpb_s[...] = lax.dot_general(d3_ref[1], w3_ref[1], nt_dims, preferred_element_type=F32)

        def branches(rb, carry):
            rows = _rows(rb)
            ca_v = ca_ref[rows, :]
            mu = jnp.mean(ca_v, axis=-1, keepdims=True)
            xc = ca_v - mu
            rstd = lax.rsqrt(jnp.mean(xc * xc, axis=-1, keepdims=True) + EPS)
            xhat = xc * rstd
            ln = xhat * lng_ref[...] + lnb_ref[...]
            sl = _sigmoid(ln)
            ua = ln * sl
            a_z = split(2, rows)
            sz = _sigmoid(a_z)
            dpa = dpa_s[rows, :]
            put(2, rows, dpa * ua * (sz * (1.0 + a_z * (1.0 - sz))))
            dln = dpa * (a_z * sz) * (sl * (1.0 + ln * (1.0 - sl)))
            accum(ROW_DLNG, dln * xhat)
            accum(ROW_DLNB, dln)
            dxh = dln * lng_ref[...]
            dca = rstd * (dxh - jnp.mean(dxh, axis=-1, keepdims=True)
                          - xhat * jnp.mean(dxh * xhat, axis=-1, keepdims=True))
            accum(ROW_DCAB, dca)
            ext_d[rows, :] = dca
            dua0_s[rows, :] = jnp.zeros((RB, d), F32)
            dm_s[rows, :] = split(0, rows) * _sigmoid(split(1, rows))
            b_z = split(6, rows)
            szb = _sigmoid(b_z)
            dpb = dpb_s[rows, :]
            b_b = split(3, rows)
            cb_v = cb_ref[rows, :]
            put(6, rows, dpb * (b_b * cb_v) * (szb * (1.0 + b_z * (1.0 - szb))))
            dub = dpb * (b_z * szb)
            put(3, rows, dub * cb_v)
            ext_e[rows, :] = dub * b_b
            return carry
        lax.fori_loop(0, nrb, branches, 0)

        for r, taps in _conv_a_taps(0, CONV_A - 1):
            if r == 0:
                src = ext_d
            else:
                sh[...] = ext_d[r:r + shl, :]
                src = sh

            def conv_t(rb, carry, src=src, taps=taps):
                rows = _rows(rb)
                ua0 = dm_s[rows, :]
                dua0 = dua0_s[rows, :]
                for q, lag in taps:
                    k = CONV_A - 1 - lag
                    slab = src[pl.ds(pl.multiple_of(rb * RB + 8 * q, 8), RB), :]
                    dua0 = dua0 + slab * wa_ref[k:k + 1, :]
                    accum(ROW_DWA + k, slab * ua0)
                dua0_s[rows, :] = dua0
                return carry
            lax.fori_loop(0, nrb, conv_t, 0)
        ext_d[TM:TM + HALO_A, :] = ext_d[0:HALO_A, :]

        dpb_s[...] = ext_e[0:TM, :] * wb_ref[CONV_B - 1:CONV_B, :]
        for lag in range(CONV_B):
            k = CONV_B - 1 - lag
            if lag > 0:
                sh[0:TM, :] = ext_e[lag:lag + TM, :]
                dpb_s[...] += sh[0:TM, :] * wb_ref[k:k + 1, :]
            src = ext_e if lag == 0 else sh

            def conv_b_w(rb, carry, src=src, k=k):
                rows = _rows(rb)
                accum(ROW_DWB + k, src[rows, :] * (split(4, rows) * split(5, rows)))
                return carry
            lax.fori_loop(0, nrb, conv_b_w, 0)
        ext_e[TM:TM + HALO_B, :] = ext_e[0:HALO_B, :]

        def inputs(rb, carry):
            rows = _rows(rb)
            dua0 = dua0_s[rows, :]
            a_val = split(0, rows)
            sg = _sigmoid(split(1, rows))
            put(0, rows, dua0 * sg)
            put(1, rows, dua0 * a_val * sg * (1.0 - sg))
            dcbin = dpb_s[rows, :]
            put(4, rows, dcbin * split(5, rows))
            put(5, rows, dcbin * split(4, rows))
            return carry
        lax.fori_loop(0, nrb, inputs, 0)

        @pl.when(step == nt - 1)
        def _():
            for row in range(SM_ROWS):
                sm_ref[row:row + 1, :] = jnp.sum(acc[row], axis=0, keepdims=True)

    rev = lambda i: (nt - 1 - i, 0)
    row_f32 = pl.BlockSpec((TM, d), rev)
    const = lambda shape: pl.BlockSpec(shape, lambda i: (0,) * len(shape))
    return pl.pallas_call(
        body, name="b1_mix", grid=(nt,),
        in_specs=[row_f32, pl.BlockSpec((TM, N_SPLIT * d), rev), row_f32, row_f32, row_f32, row_f32,
                  const((3, d, d)), const(wa.shape), const(wb.shape), const((1, d)), const((1, d))],
        out_specs=[pl.BlockSpec((TM, N_SPLIT * d), rev),
                   pl.BlockSpec((3, TM, d), lambda i: (0, nt - 1 - i, 0)),
                   const((SM_ROWS, d))],
        out_shape=[jax.ShapeDtypeStruct((tp, N_SPLIT * d), BF16), jax.ShapeDtypeStruct((3, tp, d), BF16),
                   jax.ShapeDtypeStruct((SM_ROWS, d), F32)],
        scratch_shapes=[pltpu.VMEM((TM + HALO_A, d), F32), pltpu.VMEM((TM + HALO_B, d), F32),
                        pltpu.VMEM((shl, d), F32), pltpu.VMEM((TM, d), F32), pltpu.VMEM((TM, d), F32),
                        pltpu.VMEM((TM, d), F32), pltpu.VMEM((TM, d), F32),
                        pltpu.VMEM((SM_ROWS, 8, d), F32)],
        compiler_params=_params(("arbitrary",)),
    )(ds2, proj, ca, cb, ya, yb, w3, wa, wb, ln_g, ln_b)


def kernel(x, meta_tokens, norm_g, w_in, conv_a_w, conv_a_b, ln_a_g, ln_a_b, w_a_out, b_a_out, conv_b_w, w_b_out, w_out, final_g, loss_target, m_meta_tokens, m_norm_g, m_w_in, m_conv_a_w, m_conv_a_b, m_ln_a_g, m_ln_a_b, m_w_a_out, m_b_a_out, m_conv_b_w, m_w_b_out, m_w_out, m_final_g, v_meta_tokens, v_norm_g, v_w_in, v_conv_a_w, v_conv_a_b, v_ln_a_g, v_ln_a_b, v_w_a_out, v_b_a_out, v_conv_b_w, v_w_b_out, v_w_out, v_final_g):
    seq, d = x.shape[1], x.shape[2]
    dc = meta_tokens.shape[1]
    sw = w_in.shape[2]
    rsh = w_a_out.shape[1]
    xi, yi, ci = _mesh_pos()
    me = 2 * xi + yi
    pos = jnp.stack([ci, me]).astype(jnp.int32)

    conv_rows = HALO_A + HALO_B + 8
    convs = jnp.concatenate([
        jnp.pad(conv_a_w[0], ((0, HALO_A - CONV_A), (0, 0))),
        jnp.pad(conv_b_w[0], ((0, HALO_B - CONV_B), (0, 0))), jnp.zeros((8, dc), F32)], axis=0)[None]
    w3_own = jnp.stack([w_a_out[0], w_b_out[0], w_out[0]])
    (metag,) = _all_gather([_place_own(meta_tokens[None], pos, F32, "place_meta")])
    meta_full = jnp.transpose(metag[0], (1, 0, 2)).reshape(N_META, N_CHIPS * dc)
    fg2 = final_g.reshape(1, d)

    first_tile = jnp.concatenate([jnp.zeros((TM - N_META, d), F32), meta_full], axis=0)
    s_pad = jnp.concatenate([first_tile, x[0]], axis=0)

    proj, (wg_in, wg3, convg) = _proj_fwd(s_pad, norm_g, [_place_own(w_in, pos, BF16, "place_in"),
                                                          _place_own(w3_own, pos, BF16, "place_sq"),
                                                          _place_own(convs, pos, F32, "place_conv")], pos)
    w3 = wg3.reshape(3, N_CHIPS * rsh, d)
    convg = jnp.transpose(convg[0], (1, 0, 2)).reshape(conv_rows, N_CHIPS * dc)
    wa_full = convg[0:HALO_A]
    wb_full = convg[HALO_A:HALO_A + HALO_B]
    ca, cb, ya, yb, abm_t, ds2, h_t, loss8, dfg8 = _mix_fwd(
        s_pad, proj, loss_target[0], w3, wa_full, wb_full, conv_a_b, ln_a_g, ln_a_b, b_a_out, fg2, norm_g)
    dproj, d3, sm = _mix_bwd(ds2, proj, ca, cb, ya, yb, w3, wa_full, wb_full, ln_a_g, ln_a_b)
    cw_sq = _col_block(d, 512)
    p32_sq, pbf_sq, _ = _dw_reduced(
        abm_t, d3, cw_sq, N_CHIPS, (3, N_CHIPS, rsh // 2, d), (None, N_CHIPS, rsh // 2, cw_sq),
        lambda u: (u // (d // cw_sq), 0, 0, u % (d // cw_sq)), None, "dw_square")
    cw_in = _col_block(sw, 768)
    p32_in, pbf_in, l_sq = _dw_reduced(
        h_t[None], dproj[None], cw_in, 1, (1, N_CHIPS, d // 2, sw), (None, None, d // 2, cw_in),
        lambda u: (0, u // (sw // cw_in), 0, u % (sw // cw_in)), pbf_sq, "dw_in")
    ds, dng8, l_in = _dh_bwd(dproj, wg_in, s_pad, ds2, norm_g, pbf_in)
    half_in = _sum_chips(p32_in, l_in, pos, "rs_sum_in")
    half_sq = _sum_chips(p32_sq, l_sq, pos, "rs_sum_sq")
    other_in, other_sq = _sibling_swap([half_in, half_sq])

    tail_row = lax.broadcasted_iota(jnp.int32, (8, d), 0)
    tail = jnp.where(tail_row == 0, dng8, jnp.where(tail_row == 1, dfg8,
                     jnp.where(tail_row == 2, loss8[0, 0], 0.0)))
    block = jnp.concatenate([sm, ds[TM - N_META:TM], tail], axis=0)
    *upd_in, red = _adam_halves([w_in], [m_w_in], [v_w_in], half_in, other_in, pos, "adam_in", small=block)
    col = lax.dynamic_slice(red, (0, me * dc), (AR_ROWS, dc))
    g_small = {
        "meta_tokens": col[ROW_DMETA:ROW_DMETA + N_META],
        "norm_g": red[ROW_DNG:ROW_DNG + 1],
        "conv_a_w": col[ROW_DWA:ROW_DWA + CONV_A][None],
        "conv_a_b": red[ROW_DCAB:ROW_DCAB + 1],
        "ln_a_g": red[ROW_DLNG:ROW_DLNG + 1],
        "ln_a_b": red[ROW_DLNB:ROW_DLNB + 1],
        "b_a_out": red[ROW_DBAO:ROW_DBAO + 1],
        "conv_b_w": col[ROW_DWB:ROW_DWB + CONV_B][None],
        "final_g": red[ROW_DFG],
    }

    upd_sq = _adam_halves([w_a_out, w_b_out, w_out], [m_w_a_out, m_w_b_out, m_w_out],
                          [v_w_a_out, v_w_b_out, v_w_out], half_sq, other_sq, pos, "adam_sq")
    small_w = {"meta_tokens": (meta_tokens, m_meta_tokens, v_meta_tokens), "norm_g": (norm_g, m_norm_g, v_norm_g),
               "conv_a_w": (conv_a_w, m_conv_a_w, v_conv_a_w), "conv_a_b": (conv_a_b, m_conv_a_b, v_conv_a_b),
               "ln_a_g": (ln_a_g, m_ln_a_g, v_ln_a_g), "ln_a_b": (ln_a_b, m_ln_a_b, v_ln_a_b),
               "b_a_out": (b_a_out, m_b_a_out, v_b_a_out), "conv_b_w": (conv_b_w, m_conv_b_w, v_conv_b_w),
               "final_g": (final_g, m_final_g, v_final_g)}
    names_small = list(small_w)
    as2d = lambda t: t.reshape(-1, t.shape[-1])
    upd_small = _adam_small([(as2d(small_w[k][0]), as2d(g_small[k]), as2d(small_w[k][1]), as2d(small_w[k][2]))
                             for k in names_small])

    grads, deltas, new_m, new_v = dict(g_small), {}, {}, {}
    for k, upd in zip(names_small, upd_small):
        deltas[k], new_m[k], new_v[k] = [t.reshape(small_w[k][0].shape) for t in upd]
    grads["w_in"], deltas["w_in"], new_m["w_in"], new_v["w_in"] = upd_in
    for idx, k in enumerate(["w_a_out", "w_b_out", "w_out"]):
        grads[k], deltas[k], new_m[k], new_v[k] = upd_sq[4 * idx:4 * idx + 4]

    loss = red[ROW_LOSS, 0]
    grad_x = ds[TM:][None]
    order = ["meta_tokens", "norm_g", "w_in", "conv_a_w", "conv_a_b", "ln_a_g", "ln_a_b", "w_a_out", "b_a_out",
             "conv_b_w", "w_b_out", "w_out", "final_g"]
    return (loss, grad_x, *[grads[k] for k in order], *[deltas[k] for k in order],
            *[new_m[k] for k in order], *[new_v[k] for k in order])
```

```python
import functools

import jax
import jax.numpy as jnp
from jax import lax
from jax.experimental import pallas as pl
from jax.experimental.pallas import tpu as pltpu

F32 = jnp.float32
BF16 = jnp.bfloat16
MESH = pl.DeviceIdType.MESH

EPS = 1e-6
N_META = 16
N_SPLIT = 9
CONV_A = 31
CONV_B = 3
HALO_A = 32
HALO_B = 8
SHIFT_ROWS = 24
TM = 256
RB = 64
N_ROW_TILES_BIG = 8
ROW_BLOCK = 256
N_CHIPS = 4
VMEM_LIMIT = 56 * 1024 * 1024
VMEM_LIMIT_BIG = 62 * 1024 * 1024

ADAM_LR = 0.001
ADAM_B1 = 0.9
ADAM_B2 = 0.999
ADAM_EPS = 1e-08
ADAM_WD = 0.01
ADAM_STEP = 10

ROW_DWA = 0
ROW_DWB = 32
ROW_DCAB = 40
ROW_DLNG = 41
ROW_DLNB = 42
ROW_DBAO = 43
SM_ROWS = 48
ROW_DMETA = 48
ROW_DNG = 64
ROW_DFG = 65
ROW_LOSS = 66
AR_ROWS = 72


def _sigmoid(v):
    return 0.5 * jnp.tanh(0.5 * v) + 0.5


def _params(sem, **kw):
    return pltpu.CompilerParams(dimension_semantics=sem, vmem_limit_bytes=VMEM_LIMIT, **kw)


def _rows(rb):
    return pl.ds(pl.multiple_of(rb * RB, RB), RB)


def _mesh_pos():
    x, y, c = lax.axis_index("x"), lax.axis_index("y"), lax.axis_index("c")
    return x, y, c


def _half(ref, j, c):
    h = ref.shape[2] // 2
    return ref.at[:, j, pl.ds(c * h, h), :]


def _place_own(shard, pos, dtype, name):
    s, r, c = shard.shape
    rb = ROW_BLOCK if r % ROW_BLOCK == 0 else r

    def body(pos_ref, x_ref, o_ref):
        o_ref[...] = x_ref[...].astype(dtype)

    return pl.pallas_call(
        body, name=name,
        grid_spec=pltpu.PrefetchScalarGridSpec(
            num_scalar_prefetch=1, grid=(s, r // rb),
            in_specs=[pl.BlockSpec((None, rb, c), lambda si, b, pos_ref: (si, b, 0))],
            out_specs=pl.BlockSpec((None, None, rb, c), lambda si, b, pos_ref: (si, pos_ref[1], b, 0))),
        out_shape=jax.ShapeDtypeStruct((s, N_CHIPS, r, c), dtype),
        compiler_params=_params(("arbitrary",) * 2),
    )(pos, shard)


def _all_gather(bufs):
    n = len(bufs)

    def body(*refs):
        outs = refs[n:2 * n]
        send_sems, recv_sems = refs[2 * n:]
        x, y, c = _mesh_pos()
        me = 2 * x + y
        sibling = (x, y, 1 - c)
        chips = [(1 - x, y), (x, 1 - y), (1 - x, 1 - y)]

        def remote(a, k, piece_src, piece_dst, to):
            return pltpu.make_async_remote_copy(
                src_ref=piece_src, dst_ref=piece_dst, send_sem=send_sems.at[6 * a + k],
                recv_sem=recv_sems.at[6 * a + k], device_id=to, device_id_type=MESH)

        sends = []
        for a in range(n):
            mine = _half(outs[a], me, c)
            for k, (px, py) in enumerate(chips):
                sends.append(remote(a, k, mine, mine, (px, py, c)))
        for cp in sends:
            cp.start()
        for a in range(n):
            for k, (px, py) in enumerate(chips):
                piece = _half(outs[a], 2 * px + py, c)
                remote(a, k, piece, piece, (px, py, c)).wait_recv()
                fwd = remote(a, 3 + k, piece, piece, sibling)
                fwd.start()
                sends.append(fwd)
        for a in range(n):
            for k, (px, py) in enumerate(chips):
                piece = _half(outs[a], 2 * px + py, 1 - c)
                remote(a, 3 + k, piece, piece, sibling).wait_recv()
        for cp in sends:
            cp.wait_send()

    any_spec = pl.BlockSpec(memory_space=pl.ANY)
    return pl.pallas_call(
        body, name="ag_weights",
        in_specs=[any_spec] * n, out_specs=[any_spec] * n,
        out_shape=[jax.ShapeDtypeStruct(b.shape, b.dtype) for b in bufs],
        input_output_aliases={a: a for a in range(n)},
        scratch_shapes=[pltpu.SemaphoreType.DMA((6 * n,)), pltpu.SemaphoreType.DMA((6 * n,))],
    )(*bufs)


class _Exchange:
    def __init__(self, sends, recvs):
        self.sends, self.recvs = sends, recvs

    def start(self):
        for cp in self.sends:
            cp.start()

    def finish(self):
        for cp in self.recvs:
            cp.wait_recv()
        for cp in self.sends:
            cp.wait_send()


def _chip_exchange(part_ref, land_ref, send_sems, recv_sems):
    x, y, c = _mesh_pos()
    me = 2 * x + y
    sends, recvs = [], []
    for k, (px, py) in enumerate([(1 - x, y), (x, 1 - y), (1 - x, 1 - y)]):
        sems = dict(send_sem=send_sems.at[k], recv_sem=recv_sems.at[k], device_id=(px, py, c), device_id_type=MESH)
        sends.append(pltpu.make_async_remote_copy(
            src_ref=part_ref.at[:, 2 * px + py], dst_ref=land_ref.at[:, me], **sems))
        landed = land_ref.at[:, 2 * px + py]
        recvs.append(pltpu.make_async_remote_copy(src_ref=landed, dst_ref=landed, **sems))
    return _Exchange(sends, recvs)

def _sibling_swap(halves, small):
    n = len(halves)

    def body(*refs):
        ins, small_ref, outs, red_ref = refs[:n], refs[n], refs[n + 1:2 * n + 1], refs[2 * n + 1]
        send_sems, recv_sems = refs[2 * n + 2:2 * n + 4]
        reduce = _SmallAllReduce(small_ref, red_ref, *refs[2 * n + 4:])
        x, y, c = _mesh_pos()
        copies = [pltpu.make_async_remote_copy(
            src_ref=ins[a], dst_ref=outs[a], send_sem=send_sems.at[a], recv_sem=recv_sems.at[a],
            device_id=(x, y, 1 - c), device_id_type=MESH) for a in range(n)]
        reduce.start()
        for cp in copies:
            cp.start()
        reduce.between_chips()
        reduce.finish()
        for cp in copies:
            cp.wait()

    any_spec = pl.BlockSpec(memory_space=pl.ANY)
    vm = pl.BlockSpec(memory_space=pltpu.VMEM)
    outs = pl.pallas_call(
        body, name="rs_swap",
        in_specs=[any_spec] * n + [vm], out_specs=[any_spec] * n + [vm],
        out_shape=[jax.ShapeDtypeStruct(h.shape, h.dtype) for h in halves]
        + [jax.ShapeDtypeStruct(small.shape, F32)],
        scratch_shapes=[pltpu.SemaphoreType.DMA((n,)), pltpu.SemaphoreType.DMA((n,))]
        + _SmallAllReduce.scratch(*small.shape),
    )(*halves, small)
    return outs[:n], outs[n]


class _SmallAllReduce:
    def __init__(self, x_ref, out_ref, sib_ref, part_ref, peers_ref, send_sems, recv_sems):
        self.x_ref, self.out_ref, self.sib_ref, self.part_ref, self.peers_ref = x_ref, out_ref, sib_ref, part_ref, peers_ref
        x, y, c = _mesh_pos()
        self.me = 2 * x + y
        self.swap = pltpu.make_async_remote_copy(
            src_ref=x_ref, dst_ref=sib_ref, send_sem=send_sems.at[0], recv_sem=recv_sems.at[0],
            device_id=(x, y, 1 - c), device_id_type=MESH)
        self.sends, self.recvs = [], []
        for k, (px, py) in enumerate([(1 - x, y), (x, 1 - y), (1 - x, 1 - y)]):
            sems = dict(send_sem=send_sems.at[1 + k], recv_sem=recv_sems.at[1 + k],
                        device_id=(px, py, c), device_id_type=MESH)
            self.sends.append(pltpu.make_async_remote_copy(src_ref=part_ref, dst_ref=peers_ref.at[self.me], **sems))
            landed = peers_ref.at[2 * px + py]
            self.recvs.append(pltpu.make_async_remote_copy(src_ref=landed, dst_ref=landed, **sems))

    @staticmethod
    def scratch(rows, d):
        return [pltpu.VMEM((rows, d), F32), pltpu.VMEM((rows, d), F32), pltpu.VMEM((N_CHIPS, rows, d), F32),
                pltpu.SemaphoreType.DMA((4,)), pltpu.SemaphoreType.DMA((4,))]

    def start(self):
        self.swap.start()

    def between_chips(self):
        self.swap.wait()
        self.part_ref[...] = self.x_ref[...] + self.sib_ref[...]
        self.peers_ref[self.me] = self.part_ref[...]
        for cp in self.sends:
            cp.start()

    def finish(self):
        for cp in self.recvs:
            cp.wait_recv()
        for cp in self.sends:
            cp.wait_send()
        p = self.peers_ref
        self.out_ref[...] = ((p[0] + p[1]) + p[2]) + p[3]


def _sum_chips(p32, landed, pos, name):
    s, nch, h, c = p32.shape
    hb = min(h, ROW_BLOCK)

    def body(pos_ref, p_ref, l1_ref, l2_ref, l3_ref, out_ref):
        out_ref[...] = ((p_ref[...] + l1_ref[...].astype(F32)) + l2_ref[...].astype(F32)) + l3_ref[...].astype(F32)

    def slot(k):
        return pl.BlockSpec((None, None, hb, c), lambda si, b, pos_ref: (si, (pos_ref[1] + k) % N_CHIPS, b, 0))

    return pl.pallas_call(
        body, name=name,
        grid_spec=pltpu.PrefetchScalarGridSpec(
            num_scalar_prefetch=1, grid=(s, h // hb),
            in_specs=[slot(0), slot(1), slot(2), slot(3)],
            out_specs=pl.BlockSpec((None, hb, c), lambda si, b, pos_ref: (si, b, 0))),
        out_shape=jax.ShapeDtypeStruct((s, h, c), F32),
        compiler_params=_params(("arbitrary",) * 2),
    )(pos, p32, landed, landed, landed)


def _adamw(w, g, m, v):
    m = ADAM_B1 * m + (1.0 - ADAM_B1) * g
    v = ADAM_B2 * v + (1.0 - ADAM_B2) * (g * g)
    m_hat = m / (1.0 - ADAM_B1 ** ADAM_STEP)
    v_hat = v / (1.0 - ADAM_B2 ** ADAM_STEP)
    delta = -ADAM_LR * (m_hat / (jnp.sqrt(v_hat) + ADAM_EPS) + ADAM_WD * w)
    return delta, m, v


def _adam_halves(ws, ms, vs, g_own, g_recv, pos, name):
    n = len(ws)
    _, r, c = ws[0].shape
    h = r // 2
    rb = min(h, ROW_BLOCK)
    nb = h // rb

    def body(pos_ref, *refs):
        w_refs, m_refs, v_refs = refs[:n], refs[n:2 * n], refs[2 * n:3 * n]
        go_ref, gr_ref = refs[3 * n:3 * n + 2]
        outs = refs[3 * n + 2:]
        mine = pl.program_id(0) == pos_ref[0]
        for a in range(n):
            g = jnp.where(mine, go_ref[a], gr_ref[a])
            delta, m, v = _adamw(w_refs[a][...], g, m_refs[a][...], v_refs[a][...])
            outs[4 * a][...], outs[4 * a + 1][...], outs[4 * a + 2][...], outs[4 * a + 3][...] = g, delta, m, v

    spec_w = pl.BlockSpec((None, rb, c), lambda hf, b, pos_ref: (0, hf * nb + b, 0))
    spec_g = pl.BlockSpec((n, rb, c), lambda hf, b, pos_ref: (0, b, 0))
    return pl.pallas_call(
        body, name=name,
        grid_spec=pltpu.PrefetchScalarGridSpec(
            num_scalar_prefetch=1, grid=(2, nb), in_specs=[spec_w] * (3 * n) + [spec_g] * 2,
            out_specs=[spec_w] * (4 * n)),
        out_shape=[jax.ShapeDtypeStruct((1, r, c), F32)] * (4 * n),
        compiler_params=_params(("arbitrary",) * 2),
    )(pos, *ws, *ms, *vs, g_own, g_recv)


def _adam_small(items):
    n = len(items)

    def body(*refs):
        ins, outs = refs[:4 * n], refs[4 * n:]
        for a in range(n):
            w_ref, g_ref, m_ref, v_ref = ins[4 * a:4 * a + 4]
            d, m, v = _adamw(w_ref[...], g_ref[...], m_ref[...], v_ref[...])
            outs[3 * a][...] = d
            outs[3 * a + 1][...] = m
            outs[3 * a + 2][...] = v

    vm = pl.BlockSpec(memory_space=pltpu.VMEM)
    flat = [t for it in items for t in it]
    outs = pl.pallas_call(
        body, name="adam_small", in_specs=[vm] * (4 * n), out_specs=[vm] * (3 * n),
        out_shape=[jax.ShapeDtypeStruct(it[0].shape, F32) for it in items for _ in range(3)],
    )(*flat)
    return [tuple(outs[3 * a:3 * a + 3]) for a in range(n)]


def _shard_of_step(js, me):
    flip = jnp.where(js == 1, 2, jnp.where(js == 2, 1, jnp.where(js == 3, 3, 0)))
    return lax.bitwise_xor(me, flip)


def _proj_fwd(s_pad, norm_g, bufs, pos):
    tp, d = s_pad.shape
    _, nsh, _, sw = bufs[0].shape
    tmb = tp // N_ROW_TILES_BIG
    n = len(bufs)

    def body(pos_ref, s_ref, g_ref, *refs):
        proj_ref = refs[n]
        gbufs = refs[n + 1:2 * n + 1]
        wbuf, wsem, send_sems, recv_sems = refs[2 * n + 1:]
        x, y, c = _mesh_pos()
        me = 2 * x + y
        sibling = (x, y, 1 - c)
        chips = [(1 - x, y), (x, 1 - y), (1 - x, 1 - y)]
        js, i = pl.program_id(0), pl.program_id(1)

        def remote(a, k, piece, to):
            return pltpu.make_async_remote_copy(
                src_ref=piece, dst_ref=piece, send_sem=send_sems.at[6 * a + k],
                recv_sem=recv_sems.at[6 * a + k], device_id=to, device_id_type=MESH)

        def fetch(chip):
            cp = pltpu.make_async_copy(gbufs[0].at[0, chip], wbuf, wsem)
            cp.start()
            cp.wait()

        def take(a, k):
            px, py = chips[k]
            remote(a, k, _half(gbufs[a], 2 * px + py, c), (px, py, c)).wait_recv()
            remote(a, 3 + k, _half(gbufs[a], 2 * px + py, c), sibling).start()
            remote(a, 3 + k, _half(gbufs[a], 2 * px + py, 1 - c), sibling).wait_recv()

        @pl.when((js == 0) & (i == 0))
        def _():
            for a in range(n):
                for k, (px, py) in enumerate(chips):
                    remote(a, k, _half(gbufs[a], me, c), (px, py, c)).start()
            fetch(me)

        for k, (px, py) in enumerate(chips):
            @pl.when((js == k + 1) & (i == 0))
            def _(k=k, px=px, py=py):
                take(0, k)
                fetch(2 * px + py)

        s = s_ref[...]
        r = lax.rsqrt(jnp.mean(s * s, axis=-1, keepdims=True) + EPS)
        h = (s * r * g_ref[...]).astype(BF16)
        proj_ref[...] = jnp.dot(h, wbuf[...], preferred_element_type=F32).astype(BF16)

        @pl.when((js == nsh - 1) & (i == N_ROW_TILES_BIG - 1))
        def _():
            for a in range(1, n):
                for k in range(len(chips)):
                    take(a, k)
            for a in range(n):
                for k, (px, py) in enumerate(chips):
                    remote(a, k, _half(gbufs[a], me, c), (px, py, c)).wait_send()
                    remote(a, 3 + k, _half(gbufs[a], 2 * px + py, c), sibling).wait_send()

    any_spec = pl.BlockSpec(memory_space=pl.ANY)
    outs = pl.pallas_call(
        body, name="f1_proj",
        grid_spec=pltpu.PrefetchScalarGridSpec(
            num_scalar_prefetch=1, grid=(nsh, N_ROW_TILES_BIG),
            in_specs=[pl.BlockSpec((tmb, d), lambda js, i, pos_ref: (i, 0)),
                      pl.BlockSpec((1, d), lambda js, i, pos_ref: (0, 0))] + [any_spec] * n,
            out_specs=[pl.BlockSpec((tmb, sw), lambda js, i, pos_ref: (i, _shard_of_step(js, pos_ref[1])))]
            + [any_spec] * n,
            scratch_shapes=[pltpu.VMEM((d, sw), BF16), pltpu.SemaphoreType.DMA,
                            pltpu.SemaphoreType.DMA((6 * n,)), pltpu.SemaphoreType.DMA((6 * n,))]),
        out_shape=[jax.ShapeDtypeStruct((tp, nsh * sw), BF16)]
        + [jax.ShapeDtypeStruct(b.shape, b.dtype) for b in bufs],
        input_output_aliases={3 + a: 1 + a for a in range(n)},
        compiler_params=_params(("arbitrary", "arbitrary")),
    )(pos, s_pad, norm_g, *bufs)
    return outs[0], outs[1:]


def _dh_bwd(dproj, wg_in, s_pad, ds2, norm_g, part):
    tp, d = s_pad.shape
    _, nsh, _, sw = wg_in.shape
    tmb = tp // N_ROW_TILES_BIG

    def body(dp_ref, w_hbm, s_ref, ds2_ref, g_ref, part_ref, ds_ref, dng_ref, land_ref, wbuf, gacc,
             wsem, send_sems, recv_sems):
        exchange = _chip_exchange(part_ref, land_ref, send_sems, recv_sems)
        i = pl.program_id(0)

        @pl.when(i == 0)
        def _():
            exchange.start()
            gacc[...] = jnp.zeros_like(gacc)
            whole = pltpu.make_async_copy(w_hbm.at[0], wbuf, wsem)
            whole.start()
            whole.wait()

        dh = None
        for j in range(nsh):
            part = lax.dot_general(dp_ref[:, j * sw:(j + 1) * sw], wbuf[j], (((1,), (1,)), ((), ())),
                                   preferred_element_type=F32)
            dh = part if dh is None else dh + part
        s = s_ref[...]
        r = lax.rsqrt(jnp.mean(s * s, axis=-1, keepdims=True) + EPS)
        gacc[...] += (dh * s * r).reshape(tmb // 8, 8, d).sum(axis=0)
        t = dh * g_ref[...]
        ds_ref[...] = ds2_ref[...] + r * t - s * (r * r * r) * jnp.mean(t * s, axis=-1, keepdims=True)

        @pl.when(i == N_ROW_TILES_BIG - 1)
        def _():
            dng_ref[...] = jnp.broadcast_to(jnp.sum(gacc[...], axis=0, keepdims=True), (8, d))
            exchange.finish()

    any_spec = pl.BlockSpec(memory_space=pl.ANY)
    return pl.pallas_call(
        body, name="b2_dh", grid=(N_ROW_TILES_BIG,),
        in_specs=[pl.BlockSpec((tmb, nsh * sw), lambda i: (i, 0)), any_spec,
                  pl.BlockSpec((tmb, d), lambda i: (i, 0)),
                  pl.BlockSpec((tmb, d), lambda i: (i, 0)),
                  pl.BlockSpec((1, d), lambda i: (0, 0)), any_spec],
        out_specs=[pl.BlockSpec((tmb, d), lambda i: (i, 0)),
                   pl.BlockSpec((8, d), lambda i: (0, 0)), any_spec],
        out_shape=[jax.ShapeDtypeStruct((tp, d), F32), jax.ShapeDtypeStruct((8, d), F32),
                   jax.ShapeDtypeStruct(part.shape, part.dtype)],
        scratch_shapes=[pltpu.VMEM((nsh, d, sw), BF16), pltpu.VMEM((8, d), F32), pltpu.SemaphoreType.DMA,
                        pltpu.SemaphoreType.DMA((3,)), pltpu.SemaphoreType.DMA((3,))],
        compiler_params=pltpu.CompilerParams(dimension_semantics=("arbitrary",),
                                             vmem_limit_bytes=VMEM_LIMIT_BIG),
    )(dproj, wg_in, s_pad, ds2, norm_g, part)


def _col_block(width, cap):
    return max(b for b in range(128, cap + 1, 128) if width % b == 0)


def _dw_reduced(lhs_t, rhs, cw, groups, out_dims, out_block, out_index, carried, name):
    na, d, tp = lhs_t.shape
    per_a = rhs.shape[2] // cw
    nblk = na * per_a
    rg = d // groups
    hh = rg // 2

    def body(*refs):
        if carried is None:
            l_ref, r_ref, p32_ref, pbf_ref, res, rbuf, send_sems, recv_sems = refs
            exchange = _Exchange([], [])
        else:
            (l_ref, r_ref, part_ref, p32_ref, pbf_ref, land_ref, res, rbuf, send_sems, recv_sems,
             xsend, xrecv) = refs
            exchange = _chip_exchange(part_ref, land_ref, xsend, xrecv)
        x, y, c = _mesh_pos()
        t = pl.program_id(0)
        u = jnp.maximum(t - 1, 0)

        def to_sibling(blk):
            return pltpu.make_async_remote_copy(
                src_ref=res.at[blk % 2, :, pl.ds((1 - c) * hh, hh), :], dst_ref=rbuf.at[blk % 2],
                send_sem=send_sems.at[blk], recv_sem=recv_sems.at[blk],
                device_id=(x, y, 1 - c), device_id_type=MESH)

        @pl.when(t == 0)
        def _():
            exchange.start()

        @pl.when(t < nblk)
        def _():
            res[t % 2] = jnp.dot(l_ref[...], r_ref[...], preferred_element_type=F32).reshape(groups, rg, cw)

        @pl.when(t >= 1)
        def _():
            to_sibling(u).wait_recv()
            p = res[u % 2, :, pl.ds(c * hh, hh), :] + rbuf[u % 2]
            p32_ref[...] = p.reshape(p32_ref.shape)
            pbf_ref[...] = p.reshape(pbf_ref.shape).astype(BF16)

        @pl.when(t < nblk)
        def _():
            to_sibling(t).start()

        @pl.when(t >= 1)
        def _():
            to_sibling(u).wait_send()

        @pl.when(t == nblk)
        def _():
            exchange.finish()

    any_spec = pl.BlockSpec(memory_space=pl.ANY)
    last = nblk - 1
    out_spec = pl.BlockSpec(out_block, lambda t: out_index(jnp.maximum(t - 1, 0)))
    extra = [] if carried is None else [carried]
    outs = pl.pallas_call(
        body, name=name, grid=(nblk + 1,),
        in_specs=[pl.BlockSpec((None, d, tp), lambda t: (jnp.minimum(t, last) // per_a, 0, 0)),
                  pl.BlockSpec((None, tp, cw), lambda t: (jnp.minimum(t, last) // per_a, 0,
                                                          jnp.minimum(t, last) % per_a))]
        + [any_spec] * len(extra),
        out_specs=[out_spec, out_spec] + [any_spec] * len(extra),
        out_shape=[jax.ShapeDtypeStruct(out_dims, F32), jax.ShapeDtypeStruct(out_dims, BF16)]
        + [jax.ShapeDtypeStruct(e.shape, e.dtype) for e in extra],
        scratch_shapes=[pltpu.VMEM((2, groups, rg, cw), F32), pltpu.VMEM((2, groups, hh, cw), F32),
                        pltpu.SemaphoreType.DMA((nblk,)), pltpu.SemaphoreType.DMA((nblk,))]
        + [pltpu.SemaphoreType.DMA((3,)), pltpu.SemaphoreType.DMA((3,))] * len(extra),
        compiler_params=_params(("arbitrary",)),
    )(lhs_t, rhs, *extra)
    return outs[0], outs[1], (outs[2] if extra else None)


def _conv_a_taps(first_lag, last_lag):
    out = []
    for r in range(8):
        taps = [(q, 8 * q + r) for q in range(5) if first_lag <= 8 * q + r <= last_lag]
        if taps:
            out.append((r, taps))
    return out


def _mix_fwd(s_pad, proj, target, w3, wa, wb, conv_a_b, ln_g, ln_b, b_a_out, final_g, norm_g):
    tp, d = s_pad.shape
    nt = tp // TM
    nrb = TM // RB
    shl = TM + SHIFT_ROWS

    def body(s_ref, proj_ref, tgt_ref, w3_ref, wa_ref, wb_ref, cab_ref, lng_ref, lnb_ref, bao_ref, fg_ref, ng_ref,
             ca_ref, cb_ref, ya_ref, yb_ref, abmt_ref, ds2_ref, ht_ref, loss_ref, dfg_ref,
             abm_ref, ext_a, ext_b, sh, s2_s, lacc, gacc):
        i = pl.program_id(0)

        def split(k, rows):
            return proj_ref[rows, k * d:(k + 1) * d].astype(F32)

        s_in = s_ref[...]
        h = s_in * lax.rsqrt(jnp.mean(s_in * s_in, axis=-1, keepdims=True) + EPS) * ng_ref[...]
        ht_ref[...] = h.T.astype(BF16)

        @pl.when(i == 0)
        def _():
            ext_a[0:HALO_A, :] = jnp.zeros((HALO_A, d), F32)
            ext_b[0:HALO_B, :] = jnp.zeros((HALO_B, d), F32)
            lacc[...] = jnp.zeros_like(lacc)
            gacc[...] = jnp.zeros_like(gacc)

        def conv_in(rb, carry):
            rows = _rows(rb)
            ua0 = split(0, rows) * _sigmoid(split(1, rows))
            ext_a[pl.ds(pl.multiple_of(HALO_A + rb * RB, 8), RB), :] = ua0
            ext_b[pl.ds(pl.multiple_of(HALO_B + rb * RB, 8), RB), :] = split(4, rows) * split(5, rows)
            ca_ref[rows, :] = jnp.broadcast_to(cab_ref[...], (RB, d))
            return carry
        lax.fori_loop(0, nrb, conv_in, 0)

        for r, taps in _conv_a_taps(HALO_A - CONV_A + 1, HALO_A):
            if r == 0:
                src = ext_a
            else:
                sh[...] = ext_a[r:r + shl, :]
                src = sh

            def conv_acc(rb, carry, src=src, taps=taps):
                rows = _rows(rb)
                acc = ca_ref[rows, :]
                for q, lag in taps:
                    k = lag - (HALO_A - CONV_A + 1)
                    acc = acc + src[pl.ds(pl.multiple_of(rb * RB + 8 * q, 8), RB), :] * wa_ref[k:k + 1, :]
                ca_ref[rows, :] = acc
                return carry
            lax.fori_loop(0, nrb, conv_acc, 0)
        ext_a[0:HALO_A, :] = ext_a[TM:TM + HALO_A, :]

        cb_ref[...] = ext_b[HALO_B:HALO_B + TM, :] * wb_ref[2:3, :]
        for k in range(CONV_B - 1):
            off = HALO_B - CONV_B + 1 + k
            sh[0:TM, :] = ext_b[off:off + TM, :]
            cb_ref[...] += sh[0:TM, :] * wb_ref[k:k + 1, :]
        ext_b[0:HALO_B, :] = ext_b[TM:TM + HALO_B, :]

        def branches(rb, carry):
            rows = _rows(rb)
            ca = ca_ref[rows, :]
            mu = jnp.mean(ca, axis=-1, keepdims=True)
            xc = ca - mu
            rstd = lax.rsqrt(jnp.mean(xc * xc, axis=-1, keepdims=True) + EPS)
            ln = xc * rstd * lng_ref[...] + lnb_ref[...]
            ua = ln * _sigmoid(ln)
            a_z = split(2, rows)
            abm_ref[0, rows, :] = (ua * (a_z * _sigmoid(a_z))).astype(BF16)
            b_z = split(6, rows)
            ub = split(3, rows) * cb_ref[rows, :]
            abm_ref[1, rows, :] = (ub * (b_z * _sigmoid(b_z))).astype(BF16)
            return carry
        lax.fori_loop(0, nrb, branches, 0)

        ya_ref[...] = jnp.dot(abm_ref[0], w3_ref[0], preferred_element_type=F32) + bao_ref[...]
        yb_ref[...] = jnp.dot(abm_ref[1], w3_ref[1], preferred_element_type=F32)

        def merge(rb, carry):
            rows = _rows(rb)
            m = _sigmoid(split(7, rows)) * ya_ref[rows, :] + _sigmoid(split(8, rows)) * yb_ref[rows, :]
            abm_ref[2, rows, :] = m.astype(BF16)
            return carry
        lax.fori_loop(0, nrb, merge, 0)

        s2_s[...] = s_ref[...] + jnp.dot(abm_ref[2], w3_ref[2], preferred_element_type=F32)
        for k in range(3):
            abmt_ref[k] = abm_ref[k].astype(F32).T.astype(BF16)
        live = (i > 0).astype(F32)

        def head(rb, carry):
            rows = _rows(rb)
            s2 = s2_s[rows, :]
            r2 = lax.rsqrt(jnp.mean(s2 * s2, axis=-1, keepdims=True) + EPS)
            diff = (s2 * r2 * fg_ref[...] - tgt_ref[rows, :]) * live
            lacc[...] += diff * diff
            dy = diff * (1.0 / d)
            gacc[...] += (dy * s2 * r2).reshape(RB // 8, 8, d).sum(axis=0)
            t = dy * fg_ref[...]
            ds2_ref[rows, :] = r2 * t - s2 * (r2 * r2 * r2) * jnp.mean(t * s2, axis=-1, keepdims=True)
            return carry
        lax.fori_loop(0, nrb, head, 0)

        @pl.when(i == nt - 1)
        def _():
            loss_ref[...] = jnp.broadcast_to(0.5 * jnp.sum(lacc[...]) * (1.0 / d), (8, 128))
            dfg_ref[...] = jnp.broadcast_to(jnp.sum(gacc[...], axis=0, keepdims=True), (8, d))

    row_f32 = pl.BlockSpec((TM, d), lambda i: (i, 0))
    const = lambda shape: pl.BlockSpec(shape, lambda i: (0,) * len(shape))
    return pl.pallas_call(
        body, name="f2_mix", grid=(nt,),
        in_specs=[row_f32,
                  pl.BlockSpec((TM, N_SPLIT * d), lambda i: (i, 0)),
                  pl.BlockSpec((TM, d), lambda i: (jnp.maximum(i - 1, 0), 0)),
                  const((3, d, d)), const(wa.shape), const(wb.shape)] + [const((1, d))] * 6,
        out_specs=[row_f32, row_f32, row_f32, row_f32,
                   pl.BlockSpec((3, d, TM), lambda i: (0, 0, i)),
                   row_f32, pl.BlockSpec((d, TM), lambda i: (0, i)), const((8, 128)), const((8, d))],
        out_shape=[jax.ShapeDtypeStruct((tp, d), F32)] * 4
        + [jax.ShapeDtypeStruct((3, d, tp), BF16), jax.ShapeDtypeStruct((tp, d), F32),
           jax.ShapeDtypeStruct((d, tp), BF16),
           jax.ShapeDtypeStruct((8, 128), F32), jax.ShapeDtypeStruct((8, d), F32)],
        scratch_shapes=[pltpu.VMEM((3, TM, d), BF16),
                        pltpu.VMEM((HALO_A + TM, d), F32), pltpu.VMEM((HALO_B + TM, d), F32),
                        pltpu.VMEM((shl, d), F32), pltpu.VMEM((TM, d), F32),
                        pltpu.VMEM((RB, d), F32), pltpu.VMEM((8, d), F32)],
        compiler_params=_params(("arbitrary",)),
    )(s_pad, proj, target, w3, wa, wb, conv_a_b, ln_g, ln_b, b_a_out, final_g, norm_g)


def _mix_bwd(ds2, proj, ca, cb, ya, yb, w3, wa, wb, ln_g, ln_b):
    tp, d = ds2.shape
    nt = tp // TM
    nrb = TM // RB
    shl = TM + SHIFT_ROWS
    nt_dims = (((1,), (1,)), ((), ()))

    def body(ds2_ref, proj_ref, ca_ref, cb_ref, ya_ref, yb_ref, w3_ref, wa_ref, wb_ref, lng_ref, lnb_ref,
             dproj_ref, d3_ref, sm_ref, ext_d, ext_e, sh, dm_s, dpa_s, dpb_s, dua0_s, acc):
        step = pl.program_id(0)

        def split(k, rows):
            return proj_ref[rows, k * d:(k + 1) * d].astype(F32)

        def put(k, rows, val):
            dproj_ref[rows, k * d:(k + 1) * d] = val.astype(BF16)

        def accum(row, val):
            acc[row] += val.reshape(RB // 8, 8, d).sum(axis=0)

        @pl.when(step == 0)
        def _():
            ext_d[TM:TM + HALO_A, :] = jnp.zeros((HALO_A, d), F32)
            ext_e[TM:TM + HALO_B, :] = jnp.zeros((HALO_B, d), F32)
            acc[...] = jnp.zeros_like(acc)

        d3_ref[2] = ds2_ref[...].astype(BF16)
        dm_s[...] = lax.dot_general(d3_ref[2], w3_ref[2], nt_dims, preferred_element_type=F32)

        def gates(rb, carry):
            rows = _rows(rb)
            dm = dm_s[rows, :]
            sa = _sigmoid(split(7, rows))
            sb = _sigmoid(split(8, rows))
            ya_v = ya_ref[rows, :]
            yb_v = yb_ref[rows, :]
            put(7, rows, dm * ya_v * sa * (1.0 - sa))
            put(8, rows, dm * yb_v * sb * (1.0 - sb))
            dya = dm * sa
            accum(ROW_DBAO, dya)
            d3_ref[0, rows, :] = dya.astype(BF16)
            d3_ref[1, rows, :] = (dm * sb).astype(BF16)
            return carry
        lax.fori_loop(0, nrb, gates, 0)

        dpa_s[...] = lax.dot_general(d3_ref[0], w3_ref[0], nt_dims, preferred_element_type=F32)
        dpb_s[...] = lax.dot_general(d3_ref[1], w3_ref[1], nt_dims, preferred_element_type=F32)

        def branches(rb, carry):
            rows = _rows(rb)
            ca_v = ca_ref[rows, :]
            mu = jnp.mean(ca_v, axis=-1, keepdims=True)
            xc = ca_v - mu
            rstd = lax.rsqrt(jnp.mean(xc * xc, axis=-1, keepdims=True) + EPS)
            xhat = xc * rstd
            ln = xhat * lng_ref[...] + lnb_ref[...]
            sl = _sigmoid(ln)
            ua = ln * sl
            a_z = split(2, rows)
            sz = _sigmoid(a_z)
            dpa = dpa_s[rows, :]
            put(2, rows, dpa * ua * (sz * (1.0 + a_z * (1.0 - sz))))
            dln = dpa * (a_z * sz) * (sl * (1.0 + ln * (1.0 - sl)))
            accum(ROW_DLNG, dln * xhat)
            accum(ROW_DLNB, dln)
            dxh = dln * lng_ref[...]
            dca = rstd * (dxh - jnp.mean(dxh, axis=-1, keepdims=True)
                          - xhat * jnp.mean(dxh * xhat, axis=-1, keepdims=True))
            accum(ROW_DCAB, dca)
            ext_d[rows, :] = dca
            dua0_s[rows, :] = jnp.zeros((RB, d), F32)
            dm_s[rows, :] = split(0, rows) * _sigmoid(split(1, rows))
            b_z = split(6, rows)
            szb = _sigmoid(b_z)
            dpb = dpb_s[rows, :]
            b_b = split(3, rows)
            cb_v = cb_ref[rows, :]
            put(6, rows, dpb * (b_b * cb_v) * (szb * (1.0 + b_z * (1.0 - szb))))
            dub = dpb * (b_z * szb)
            put(3, rows, dub * cb_v)
            ext_e[rows, :] = dub * b_b
            return carry
        lax.fori_loop(0, nrb, branches, 0)

        for r, taps in _conv_a_taps(0, CONV_A - 1):
            if r == 0:
                src = ext_d
            else:
                sh[...] = ext_d[r:r + shl, :]
                src = sh

            def conv_t(rb, carry, src=src, taps=taps):
                rows = _rows(rb)
                ua0 = dm_s[rows, :]
                dua0 = dua0_s[rows, :]
                for q, lag in taps:
                    k = CONV_A - 1 - lag
                    slab = src[pl.ds(pl.multiple_of(rb * RB + 8 * q, 8), RB), :]
                    dua0 = dua0 + slab * wa_ref[k:k + 1, :]
                    accum(ROW_DWA + k, slab * ua0)
                dua0_s[rows, :] = dua0
                return carry
            lax.fori_loop(0, nrb, conv_t, 0)
        ext_d[TM:TM + HALO_A, :] = ext_d[0:HALO_A, :]

        dpb_s[...] = ext_e[0:TM, :] * wb_ref[CONV_B - 1:CONV_B, :]
        for lag in range(CONV_B):
            k = CONV_B - 1 - lag
            if lag > 0:
                sh[0:TM, :] = ext_e[lag:lag + TM, :]
                dpb_s[...] += sh[0:TM, :] * wb_ref[k:k + 1, :]
            src = ext_e if lag == 0 else sh

            def conv_b_w(rb, carry, src=src, k=k):
                rows = _rows(rb)
                accum(ROW_DWB + k, src[rows, :] * (split(4, rows) * split(5, rows)))
                return carry
            lax.fori_loop(0, nrb, conv_b_w, 0)
        ext_e[TM:TM + HALO_B, :] = ext_e[0:HALO_B, :]

        def inputs(rb, carry):
            rows = _rows(rb)
            dua0 = dua0_s[rows, :]
            a_val = split(0, rows)
            sg = _sigmoid(split(1, rows))
            put(0, rows, dua0 * sg)
            put(1, rows, dua0 * a_val * sg * (1.0 - sg))
            dcbin = dpb_s[rows, :]
            put(4, rows, dcbin * split(5, rows))
            put(5, rows, dcbin * split(4, rows))
            return carry
        lax.fori_loop(0, nrb, inputs, 0)

        @pl.when(step == nt - 1)
        def _():
            for row in range(SM_ROWS):
                sm_ref[row:row + 1, :] = jnp.sum(acc[row], axis=0, keepdims=True)

    rev = lambda i: (nt - 1 - i, 0)
    row_f32 = pl.BlockSpec((TM, d), rev)
    const = lambda shape: pl.BlockSpec(shape, lambda i: (0,) * len(shape))
    return pl.pallas_call(
        body, name="b1_mix", grid=(nt,),
        in_specs=[row_f32, pl.BlockSpec((TM, N_SPLIT * d), rev), row_f32, row_f32, row_f32, row_f32,
                  const((3, d, d)), const(wa.shape), const(wb.shape), const((1, d)), const((1, d))],
        out_specs=[pl.BlockSpec((TM, N_SPLIT * d), rev),
                   pl.BlockSpec((3, TM, d), lambda i: (0, nt - 1 - i, 0)),
                   const((SM_ROWS, d))],
        out_shape=[jax.ShapeDtypeStruct((tp, N_SPLIT * d), BF16), jax.ShapeDtypeStruct((3, tp, d), BF16),
                   jax.ShapeDtypeStruct((SM_ROWS, d), F32)],
        scratch_shapes=[pltpu.VMEM((TM + HALO_A, d), F32), pltpu.VMEM((TM + HALO_B, d), F32),
                        pltpu.VMEM((shl, d), F32), pltpu.VMEM((TM, d), F32), pltpu.VMEM((TM, d), F32),
                        pltpu.VMEM((TM, d), F32), pltpu.VMEM((TM, d), F32),
                        pltpu.VMEM((SM_ROWS, 8, d), F32)],
        compiler_params=_params(("arbitrary",)),
    )(ds2, proj, ca, cb, ya, yb, w3, wa, wb, ln_g, ln_b)


def kernel(x, meta_tokens, norm_g, w_in, conv_a_w, conv_a_b, ln_a_g, ln_a_b, w_a_out, b_a_out, conv_b_w, w_b_out, w_out, final_g, loss_target, m_meta_tokens, m_norm_g, m_w_in, m_conv_a_w, m_conv_a_b, m_ln_a_g, m_ln_a_b, m_w_a_out, m_b_a_out, m_conv_b_w, m_w_b_out, m_w_out, m_final_g, v_meta_tokens, v_norm_g, v_w_in, v_conv_a_w, v_conv_a_b, v_ln_a_g, v_ln_a_b, v_w_a_out, v_b_a_out, v_conv_b_w, v_w_b_out, v_w_out, v_final_g):
    seq, d = x.shape[1], x.shape[2]
    dc = meta_tokens.shape[1]
    sw = w_in.shape[2]
    rsh = w_a_out.shape[1]
    xi, yi, ci = _mesh_pos()
    me = 2 * xi + yi
    pos = jnp.stack([ci, me]).astype(jnp.int32)

    conv_rows = HALO_A + HALO_B + 8
    convs = jnp.concatenate([
        jnp.pad(conv_a_w[0], ((0, HALO_A - CONV_A), (0, 0))),
        jnp.pad(conv_b_w[0], ((0, HALO_B - CONV_B), (0, 0))), jnp.zeros((8, dc), F32)], axis=0)[None]
    w3_own = jnp.stack([w_a_out[0], w_b_out[0], w_out[0]])
    (metag,) = _all_gather([_place_own(meta_tokens[None], pos, F32, "place_meta")])
    meta_full = jnp.transpose(metag[0], (1, 0, 2)).reshape(N_META, N_CHIPS * dc)
    fg2 = final_g.reshape(1, d)

    first_tile = jnp.concatenate([jnp.zeros((TM - N_META, d), F32), meta_full], axis=0)
    s_pad = jnp.concatenate([first_tile, x[0]], axis=0)

    proj, (wg_in, wg3, convg) = _proj_fwd(s_pad, norm_g, [_place_own(w_in, pos, BF16, "place_in"),
                                                          _place_own(w3_own, pos, BF16, "place_sq"),
                                                          _place_own(convs, pos, F32, "place_conv")], pos)
    w3 = wg3.reshape(3, N_CHIPS * rsh, d)
    convg = jnp.transpose(convg[0], (1, 0, 2)).reshape(conv_rows, N_CHIPS * dc)
    wa_full = convg[0:HALO_A]
    wb_full = convg[HALO_A:HALO_A + HALO_B]
    ca, cb, ya, yb, abm_t, ds2, h_t, loss8, dfg8 = _mix_fwd(
        s_pad, proj, loss_target[0], w3, wa_full, wb_full, conv_a_b, ln_a_g, ln_a_b, b_a_out, fg2, norm_g)
    dproj, d3, sm = _mix_bwd(ds2, proj, ca, cb, ya, yb, w3, wa_full, wb_full, ln_a_g, ln_a_b)
    cw_sq = _col_block(d, 512)
    p32_sq, pbf_sq, _ = _dw_reduced(
        abm_t, d3, cw_sq, N_CHIPS, (3, N_CHIPS, rsh // 2, d), (None, N_CHIPS, rsh // 2, cw_sq),
        lambda u: (u // (d // cw_sq), 0, 0, u % (d // cw_sq)), None, "dw_square")
    cw_in = _col_block(sw, 768)
    p32_in, pbf_in, l_sq = _dw_reduced(
        h_t[None], dproj[None], cw_in, 1, (1, N_CHIPS, d // 2, sw), (None, None, d // 2, cw_in),
        lambda u: (0, u // (sw // cw_in), 0, u % (sw // cw_in)), pbf_sq, "dw_in")
    ds, dng8, l_in = _dh_bwd(dproj, wg_in, s_pad, ds2, norm_g, pbf_in)
    half_in = _sum_chips(p32_in, l_in, pos, "rs_sum_in")
    half_sq = _sum_chips(p32_sq, l_sq, pos, "rs_sum_sq")
    tail_row = lax.broadcasted_iota(jnp.int32, (8, d), 0)
    tail = jnp.where(tail_row == 0, dng8, jnp.where(tail_row == 1, dfg8,
                     jnp.where(tail_row == 2, loss8[0, 0], 0.0)))
    block = jnp.concatenate([sm, ds[TM - N_META:TM], tail], axis=0)
    (other_in, other_sq), red = _sibling_swap([half_in, half_sq], block)
    col = lax.dynamic_slice(red, (0, me * dc), (AR_ROWS, dc))
    g_small = {
        "meta_tokens": col[ROW_DMETA:ROW_DMETA + N_META],
        "norm_g": red[ROW_DNG:ROW_DNG + 1],
        "conv_a_w": col[ROW_DWA:ROW_DWA + CONV_A][None],
        "conv_a_b": red[ROW_DCAB:ROW_DCAB + 1],
        "ln_a_g": red[ROW_DLNG:ROW_DLNG + 1],
        "ln_a_b": red[ROW_DLNB:ROW_DLNB + 1],
        "b_a_out": red[ROW_DBAO:ROW_DBAO + 1],
        "conv_b_w": col[ROW_DWB:ROW_DWB + CONV_B][None],
        "final_g": red[ROW_DFG],
    }

    upd_in = _adam_halves([w_in], [m_w_in], [v_w_in], half_in, other_in, pos, "adam_in")
    upd_sq = _adam_halves([w_a_out, w_b_out, w_out], [m_w_a_out, m_w_b_out, m_w_out],
                          [v_w_a_out, v_w_b_out, v_w_out], half_sq, other_sq, pos, "adam_sq")
    small_w = {"meta_tokens": (meta_tokens, m_meta_tokens, v_meta_tokens), "norm_g": (norm_g, m_norm_g, v_norm_g),
               "conv_a_w": (conv_a_w, m_conv_a_w, v_conv_a_w), "conv_a_b": (conv_a_b, m_conv_a_b, v_conv_a_b),
               "ln_a_g": (ln_a_g, m_ln_a_g, v_ln_a_g), "ln_a_b": (ln_a_b, m_ln_a_b, v_ln_a_b),
               "b_a_out": (b_a_out, m_b_a_out, v_b_a_out), "conv_b_w": (conv_b_w, m_conv_b_w, v_conv_b_w),
               "final_g": (final_g, m_final_g, v_final_g)}
    names_small = list(small_w)
    as2d = lambda t: t.reshape(-1, t.shape[-1])
    upd_small = _adam_small([(as2d(small_w[k][0]), as2d(g_small[k]), as2d(small_w[k][1]), as2d(small_w[k][2]))
                             for k in names_small])

    grads, deltas, new_m, new_v = dict(g_small), {}, {}, {}
    for k, upd in zip(names_small, upd_small):
        deltas[k], new_m[k], new_v[k] = [t.reshape(small_w[k][0].shape) for t in upd]
    grads["w_in"], deltas["w_in"], new_m["w_in"], new_v["w_in"] = upd_in
    for idx, k in enumerate(["w_a_out", "w_b_out", "w_out"]):
        grads[k], deltas[k], new_m[k], new_v[k] = upd_sq[4 * idx:4 * idx + 4]

    loss = red[ROW_LOSS, 0]
    grad_x = ds[TM:][None]
    order = ["meta_tokens", "norm_g", "w_in", "conv_a_w", "conv_a_b", "ln_a_g", "ln_a_b", "w_a_out", "b_a_out",
             "conv_b_w", "w_b_out", "w_out", "final_g"]
    return (loss, grad_x, *[grads[k] for k in order], *[deltas[k] for k in order],
            *[new_m[k] for k in order], *[new_v[k] for k in order])
```

```python
import functools

import jax
import jax.numpy as jnp
from jax import lax
from jax.experimental import pallas as pl
from jax.experimental.pallas import tpu as pltpu

F32 = jnp.float32
BF16 = jnp.bfloat16
MESH = pl.DeviceIdType.MESH

EPS = 1e-6
N_META = 16
N_SPLIT = 9
CONV_A = 31
CONV_B = 3
HALO_A = 32
HALO_B = 8
SHIFT_ROWS = 24
TM = 256
RB = 64
N_ROW_TILES_BIG = 8
ROW_BLOCK = 256
N_CHIPS = 4
VMEM_LIMIT = 56 * 1024 * 1024
VMEM_LIMIT_BIG = 62 * 1024 * 1024

ADAM_LR = 0.001
ADAM_B1 = 0.9
ADAM_B2 = 0.999
ADAM_EPS = 1e-08
ADAM_WD = 0.01
ADAM_STEP = 10

ROW_DWA = 0
ROW_DWB = 32
ROW_DCAB = 40
ROW_DLNG = 41
ROW_DLNB = 42
ROW_DBAO = 43
SM_ROWS = 48
ROW_DMETA = 48
ROW_DNG = 64
ROW_DFG = 65
ROW_LOSS = 66
AR_ROWS = 72


def _sigmoid(v):
    return 0.5 * jnp.tanh(0.5 * v) + 0.5


def _params(sem, **kw):
    return pltpu.CompilerParams(dimension_semantics=sem, vmem_limit_bytes=VMEM_LIMIT, **kw)


def _rows(rb):
    return pl.ds(pl.multiple_of(rb * RB, RB), RB)


def _mesh_pos():
    x, y, c = lax.axis_index("x"), lax.axis_index("y"), lax.axis_index("c")
    return x, y, c


def _half(ref, j, c):
    h = ref.shape[2] // 2
    return ref.at[:, j, pl.ds(c * h, h), :]


def _place_own(shard, pos, dtype, name):
    s, r, c = shard.shape
    rb = ROW_BLOCK if r % ROW_BLOCK == 0 else r

    def body(pos_ref, x_ref, o_ref):
        o_ref[...] = x_ref[...].astype(dtype)

    return pl.pallas_call(
        body, name=name,
        grid_spec=pltpu.PrefetchScalarGridSpec(
            num_scalar_prefetch=1, grid=(s, r // rb),
            in_specs=[pl.BlockSpec((None, rb, c), lambda si, b, pos_ref: (si, b, 0))],
            out_specs=pl.BlockSpec((None, None, rb, c), lambda si, b, pos_ref: (si, pos_ref[1], b, 0))),
        out_shape=jax.ShapeDtypeStruct((s, N_CHIPS, r, c), dtype),
        compiler_params=_params(("arbitrary",) * 2),
    )(pos, shard)


def _all_gather(bufs):
    n = len(bufs)

    def body(*refs):
        outs = refs[n:2 * n]
        send_sems, recv_sems = refs[2 * n:]
        x, y, c = _mesh_pos()
        me = 2 * x + y
        sibling = (x, y, 1 - c)
        chips = [(1 - x, y), (x, 1 - y), (1 - x, 1 - y)]

        def remote(a, k, piece_src, piece_dst, to):
            return pltpu.make_async_remote_copy(
                src_ref=piece_src, dst_ref=piece_dst, send_sem=send_sems.at[6 * a + k],
                recv_sem=recv_sems.at[6 * a + k], device_id=to, device_id_type=MESH)

        sends = []
        for a in range(n):
            mine = _half(outs[a], me, c)
            for k, (px, py) in enumerate(chips):
                sends.append(remote(a, k, mine, mine, (px, py, c)))
        for cp in sends:
            cp.start()
        for a in range(n):
            for k, (px, py) in enumerate(chips):
                piece = _half(outs[a], 2 * px + py, c)
                remote(a, k, piece, piece, (px, py, c)).wait_recv()
                fwd = remote(a, 3 + k, piece, piece, sibling)
                fwd.start()
                sends.append(fwd)
        for a in range(n):
            for k, (px, py) in enumerate(chips):
                piece = _half(outs[a], 2 * px + py, 1 - c)
                remote(a, 3 + k, piece, piece, sibling).wait_recv()
        for cp in sends:
            cp.wait_send()

    any_spec = pl.BlockSpec(memory_space=pl.ANY)
    return pl.pallas_call(
        body, name="ag_weights",
        in_specs=[any_spec] * n, out_specs=[any_spec] * n,
        out_shape=[jax.ShapeDtypeStruct(b.shape, b.dtype) for b in bufs],
        input_output_aliases={a: a for a in range(n)},
        scratch_shapes=[pltpu.SemaphoreType.DMA((6 * n,)), pltpu.SemaphoreType.DMA((6 * n,))],
    )(*bufs)


class _Exchange:
    def __init__(self, sends, recvs):
        self.sends, self.recvs = sends, recvs

    @staticmethod
    def _each(pairs, act):
        for cond, cp in pairs:
            if cond is None:
                act(cp)
            else:
                pl.when(cond)(functools.partial(act, cp))

    def start(self):
        self._each(self.sends, lambda cp: cp.start())

    def finish(self):
        self._each(self.recvs, lambda cp: cp.wait_recv())
        self._each(self.sends, lambda cp: cp.wait_send())


def _chip_exchange(part_ref, land_ref, send_sems, recv_sems, half=None):
    x, y, c = _mesh_pos()
    me = 2 * x + y
    sends, recvs = [], []
    for k, (px, py) in enumerate([(1 - x, y), (x, 1 - y), (1 - x, 1 - y)]):
        sems = dict(send_sem=send_sems.at[k], recv_sem=recv_sems.at[k], device_id=(px, py, c), device_id_type=MESH)
        slot = 2 * px + py if half is None else py
        sends.append((None if half is None else px == half, pltpu.make_async_remote_copy(
            src_ref=part_ref.at[:, slot], dst_ref=land_ref.at[:, me], **sems)))
        landed = land_ref.at[:, 2 * px + py]
        recvs.append((None if half is None else x == half,
                      pltpu.make_async_remote_copy(src_ref=landed, dst_ref=landed, **sems)))
    return _Exchange(sends, recvs)

def _sibling_swap(halves, small):
    n = len(halves)

    def body(*refs):
        ins, small_ref, outs, red_ref = refs[:n], refs[n], refs[n + 1:2 * n + 1], refs[2 * n + 1]
        send_sems, recv_sems = refs[2 * n + 2:2 * n + 4]
        reduce = _SmallAllReduce(small_ref, red_ref, *refs[2 * n + 4:])
        x, y, c = _mesh_pos()
        copies = [pltpu.make_async_remote_copy(
            src_ref=ins[a], dst_ref=outs[a], send_sem=send_sems.at[a], recv_sem=recv_sems.at[a],
            device_id=(x, y, 1 - c), device_id_type=MESH) for a in range(n)]
        reduce.start()
        for cp in copies:
            cp.start()
        reduce.between_chips()
        reduce.finish()
        for cp in copies:
            cp.wait()

    any_spec = pl.BlockSpec(memory_space=pl.ANY)
    vm = pl.BlockSpec(memory_space=pltpu.VMEM)
    outs = pl.pallas_call(
        body, name="rs_swap",
        in_specs=[any_spec] * n + [vm], out_specs=[any_spec] * n + [vm],
        out_shape=[jax.ShapeDtypeStruct(h.shape, h.dtype) for h in halves]
        + [jax.ShapeDtypeStruct(small.shape, F32)],
        scratch_shapes=[pltpu.SemaphoreType.DMA((n,)), pltpu.SemaphoreType.DMA((n,))]
        + _SmallAllReduce.scratch(*small.shape),
    )(*halves, small)
    return outs[:n], outs[n]


class _SmallAllReduce:
    def __init__(self, x_ref, out_ref, sib_ref, part_ref, peers_ref, send_sems, recv_sems):
        self.x_ref, self.out_ref, self.sib_ref, self.part_ref, self.peers_ref = x_ref, out_ref, sib_ref, part_ref, peers_ref
        x, y, c = _mesh_pos()
        self.me = 2 * x + y
        self.swap = pltpu.make_async_remote_copy(
            src_ref=x_ref, dst_ref=sib_ref, send_sem=send_sems.at[0], recv_sem=recv_sems.at[0],
            device_id=(x, y, 1 - c), device_id_type=MESH)
        self.sends, self.recvs = [], []
        for k, (px, py) in enumerate([(1 - x, y), (x, 1 - y), (1 - x, 1 - y)]):
            sems = dict(send_sem=send_sems.at[1 + k], recv_sem=recv_sems.at[1 + k],
                        device_id=(px, py, c), device_id_type=MESH)
            self.sends.append(pltpu.make_async_remote_copy(src_ref=part_ref, dst_ref=peers_ref.at[self.me], **sems))
            landed = peers_ref.at[2 * px + py]
            self.recvs.append(pltpu.make_async_remote_copy(src_ref=landed, dst_ref=landed, **sems))

    @staticmethod
    def scratch(rows, d):
        return [pltpu.VMEM((rows, d), F32), pltpu.VMEM((rows, d), F32), pltpu.VMEM((N_CHIPS, rows, d), F32),
                pltpu.SemaphoreType.DMA((4,)), pltpu.SemaphoreType.DMA((4,))]

    def start(self):
        self.swap.start()

    def between_chips(self):
        self.swap.wait()
        self.part_ref[...] = self.x_ref[...] + self.sib_ref[...]
        self.peers_ref[self.me] = self.part_ref[...]
        for cp in self.sends:
            cp.start()

    def finish(self):
        for cp in self.recvs:
            cp.wait_recv()
        for cp in self.sends:
            cp.wait_send()
        p = self.peers_ref
        self.out_ref[...] = ((p[0] + p[1]) + p[2]) + p[3]


def _sum_chips(p32s, landed, pos, name):
    s, _, h, c = p32s[0].shape
    hb = min(h, ROW_BLOCK)
    per = N_CHIPS // len(p32s)

    def body(pos_ref, *refs):
        own_refs, (l1_ref, l2_ref, l3_ref, out_ref) = refs[:len(p32s)], refs[len(p32s):]
        own = own_refs[0][...]
        for k in range(1, len(p32s)):
            own = jnp.where(pos_ref[1] // per == k, own_refs[k][...], own)
        out_ref[...] = ((own + l1_ref[...].astype(F32)) + l2_ref[...].astype(F32)) + l3_ref[...].astype(F32)

    own_spec = pl.BlockSpec((None, None, hb, c), lambda si, b, pos_ref: (si, pos_ref[1] % per, b, 0))

    def slot(k):
        return pl.BlockSpec((None, None, hb, c), lambda si, b, pos_ref: (si, (pos_ref[1] + k) % N_CHIPS, b, 0))

    return pl.pallas_call(
        body, name=name,
        grid_spec=pltpu.PrefetchScalarGridSpec(
            num_scalar_prefetch=1, grid=(s, h // hb),
            in_specs=[own_spec] * len(p32s) + [slot(1), slot(2), slot(3)],
            out_specs=pl.BlockSpec((None, hb, c), lambda si, b, pos_ref: (si, b, 0))),
        out_shape=jax.ShapeDtypeStruct((s, h, c), F32),
        compiler_params=_params(("arbitrary",) * 2),
    )(pos, *p32s, landed, landed, landed)


def _adamw(w, g, m, v):
    m = ADAM_B1 * m + (1.0 - ADAM_B1) * g
    v = ADAM_B2 * v + (1.0 - ADAM_B2) * (g * g)
    m_hat = m / (1.0 - ADAM_B1 ** ADAM_STEP)
    v_hat = v / (1.0 - ADAM_B2 ** ADAM_STEP)
    delta = -ADAM_LR * (m_hat / (jnp.sqrt(v_hat) + ADAM_EPS) + ADAM_WD * w)
    return delta, m, v


def _adam_halves(ws, ms, vs, g_own, g_recv, pos, name):
    n = len(ws)
    _, r, c = ws[0].shape
    h = r // 2
    rb = min(h, ROW_BLOCK)
    nb = h // rb

    def body(pos_ref, *refs):
        w_refs, m_refs, v_refs = refs[:n], refs[n:2 * n], refs[2 * n:3 * n]
        go_ref, gr_ref = refs[3 * n:3 * n + 2]
        outs = refs[3 * n + 2:]
        mine = pl.program_id(0) == pos_ref[0]
        for a in range(n):
            g = jnp.where(mine, go_ref[a], gr_ref[a])
            delta, m, v = _adamw(w_refs[a][...], g, m_refs[a][...], v_refs[a][...])
            outs[4 * a][...], outs[4 * a + 1][...], outs[4 * a + 2][...], outs[4 * a + 3][...] = g, delta, m, v

    spec_w = pl.BlockSpec((None, rb, c), lambda hf, b, pos_ref: (0, hf * nb + b, 0))
    spec_g = pl.BlockSpec((n, rb, c), lambda hf, b, pos_ref: (0, b, 0))
    return pl.pallas_call(
        body, name=name,
        grid_spec=pltpu.PrefetchScalarGridSpec(
            num_scalar_prefetch=1, grid=(2, nb), in_specs=[spec_w] * (3 * n) + [spec_g] * 2,
            out_specs=[spec_w] * (4 * n)),
        out_shape=[jax.ShapeDtypeStruct((1, r, c), F32)] * (4 * n),
        compiler_params=_params(("arbitrary",) * 2),
    )(pos, *ws, *ms, *vs, g_own, g_recv)


def _adam_small(items):
    n = len(items)

    def body(*refs):
        ins, outs = refs[:4 * n], refs[4 * n:]
        for a in range(n):
            w_ref, g_ref, m_ref, v_ref = ins[4 * a:4 * a + 4]
            d, m, v = _adamw(w_ref[...], g_ref[...], m_ref[...], v_ref[...])
            outs[3 * a][...] = d
            outs[3 * a + 1][...] = m
            outs[3 * a + 2][...] = v

    vm = pl.BlockSpec(memory_space=pltpu.VMEM)
    flat = [t for it in items for t in it]
    outs = pl.pallas_call(
        body, name="adam_small", in_specs=[vm] * (4 * n), out_specs=[vm] * (3 * n),
        out_shape=[jax.ShapeDtypeStruct(it[0].shape, F32) for it in items for _ in range(3)],
    )(*flat)
    return [tuple(outs[3 * a:3 * a + 3]) for a in range(n)]


def _shard_of_step(js, me):
    flip = jnp.where(js == 1, 2, jnp.where(js == 2, 1, jnp.where(js == 3, 3, 0)))
    return lax.bitwise_xor(me, flip)


def _proj_fwd(s_pad, norm_g, bufs, pos):
    tp, d = s_pad.shape
    _, nsh, _, sw = bufs[0].shape
    tmb = tp // N_ROW_TILES_BIG
    n = len(bufs)

    def body(pos_ref, s_ref, g_ref, *refs):
        proj_ref = refs[n]
        gbufs = refs[n + 1:2 * n + 1]
        wbuf, wsem, send_sems, recv_sems = refs[2 * n + 1:]
        x, y, c = _mesh_pos()
        me = 2 * x + y
        sibling = (x, y, 1 - c)
        chips = [(1 - x, y), (x, 1 - y), (1 - x, 1 - y)]
        js, i = pl.program_id(0), pl.program_id(1)

        def remote(a, k, piece, to):
            return pltpu.make_async_remote_copy(
                src_ref=piece, dst_ref=piece, send_sem=send_sems.at[6 * a + k],
                recv_sem=recv_sems.at[6 * a + k], device_id=to, device_id_type=MESH)

        def fetch(chip):
            cp = pltpu.make_async_copy(gbufs[0].at[0, chip], wbuf, wsem)
            cp.start()
            cp.wait()

        def take(a, k):
            px, py = chips[k]
            remote(a, k, _half(gbufs[a], 2 * px + py, c), (px, py, c)).wait_recv()
            remote(a, 3 + k, _half(gbufs[a], 2 * px + py, c), sibling).start()
            remote(a, 3 + k, _half(gbufs[a], 2 * px + py, 1 - c), sibling).wait_recv()

        @pl.when((js == 0) & (i == 0))
        def _():
            for a in range(n):
                for k, (px, py) in enumerate(chips):
                    remote(a, k, _half(gbufs[a], me, c), (px, py, c)).start()
            fetch(me)

        for k, (px, py) in enumerate(chips):
            @pl.when((js == k + 1) & (i == 0))
            def _(k=k, px=px, py=py):
                take(0, k)
                fetch(2 * px + py)

        s = s_ref[...]
        r = lax.rsqrt(jnp.mean(s * s, axis=-1, keepdims=True) + EPS)
        h = (s * r * g_ref[...]).astype(BF16)
        proj_ref[...] = jnp.dot(h, wbuf[...], preferred_element_type=F32).astype(BF16)

        @pl.when((js == nsh - 1) & (i == N_ROW_TILES_BIG - 1))
        def _():
            for a in range(1, n):
                for k in range(len(chips)):
                    take(a, k)
            for a in range(n):
                for k, (px, py) in enumerate(chips):
                    remote(a, k, _half(gbufs[a], me, c), (px, py, c)).wait_send()
                    remote(a, 3 + k, _half(gbufs[a], 2 * px + py, c), sibling).wait_send()

    any_spec = pl.BlockSpec(memory_space=pl.ANY)
    outs = pl.pallas_call(
        body, name="f1_proj",
        grid_spec=pltpu.PrefetchScalarGridSpec(
            num_scalar_prefetch=1, grid=(nsh, N_ROW_TILES_BIG),
            in_specs=[pl.BlockSpec((tmb, d), lambda js, i, pos_ref: (i, 0)),
                      pl.BlockSpec((1, d), lambda js, i, pos_ref: (0, 0))] + [any_spec] * n,
            out_specs=[pl.BlockSpec((tmb, sw), lambda js, i, pos_ref: (i, _shard_of_step(js, pos_ref[1])))]
            + [any_spec] * n,
            scratch_shapes=[pltpu.VMEM((d, sw), BF16), pltpu.SemaphoreType.DMA,
                            pltpu.SemaphoreType.DMA((6 * n,)), pltpu.SemaphoreType.DMA((6 * n,))]),
        out_shape=[jax.ShapeDtypeStruct((tp, nsh * sw), BF16)]
        + [jax.ShapeDtypeStruct(b.shape, b.dtype) for b in bufs],
        input_output_aliases={3 + a: 1 + a for a in range(n)},
        compiler_params=_params(("arbitrary", "arbitrary")),
    )(pos, s_pad, norm_g, *bufs)
    return outs[0], outs[1:]


def _dh_bwd(dproj, wg_in, s_pad, ds2, norm_g, part, land, half):
    tp, d = s_pad.shape
    _, nsh, _, sw = wg_in.shape
    tmb = tp // N_ROW_TILES_BIG

    def body(dp_ref, w_hbm, s_ref, ds2_ref, g_ref, part_ref, _, ds_ref, dng_ref, land_ref, wbuf, gacc,
             wsem, send_sems, recv_sems):
        exchange = _chip_exchange(part_ref, land_ref, send_sems, recv_sems, half)
        i = pl.program_id(0)

        @pl.when(i == 0)
        def _():
            exchange.start()
            gacc[...] = jnp.zeros_like(gacc)
            whole = pltpu.make_async_copy(w_hbm.at[0], wbuf, wsem)
            whole.start()
            whole.wait()

        dh = None
        for j in range(nsh):
            part = lax.dot_general(dp_ref[:, j * sw:(j + 1) * sw], wbuf[j], (((1,), (1,)), ((), ())),
                                   preferred_element_type=F32)
            dh = part if dh is None else dh + part
        s = s_ref[...]
        r = lax.rsqrt(jnp.mean(s * s, axis=-1, keepdims=True) + EPS)
        gacc[...] += (dh * s * r).reshape(tmb // 8, 8, d).sum(axis=0)
        t = dh * g_ref[...]
        ds_ref[...] = ds2_ref[...] + r * t - s * (r * r * r) * jnp.mean(t * s, axis=-1, keepdims=True)

        @pl.when(i == N_ROW_TILES_BIG - 1)
        def _():
            dng_ref[...] = jnp.broadcast_to(jnp.sum(gacc[...], axis=0, keepdims=True), (8, d))
            exchange.finish()

    any_spec = pl.BlockSpec(memory_space=pl.ANY)
    return pl.pallas_call(
        body, name="b2_dh", grid=(N_ROW_TILES_BIG,),
        in_specs=[pl.BlockSpec((tmb, nsh * sw), lambda i: (i, 0)), any_spec,
                  pl.BlockSpec((tmb, d), lambda i: (i, 0)),
                  pl.BlockSpec((tmb, d), lambda i: (i, 0)),
                  pl.BlockSpec((1, d), lambda i: (0, 0)), any_spec, any_spec],
        out_specs=[pl.BlockSpec((tmb, d), lambda i: (i, 0)),
                   pl.BlockSpec((8, d), lambda i: (0, 0)), any_spec],
        out_shape=[jax.ShapeDtypeStruct((tp, d), F32), jax.ShapeDtypeStruct((8, d), F32),
                   jax.ShapeDtypeStruct(land.shape, land.dtype)],
        input_output_aliases={6: 2},
        scratch_shapes=[pltpu.VMEM((nsh, d, sw), BF16), pltpu.VMEM((8, d), F32), pltpu.SemaphoreType.DMA,
                        pltpu.SemaphoreType.DMA((3,)), pltpu.SemaphoreType.DMA((3,))],
        compiler_params=pltpu.CompilerParams(dimension_semantics=("arbitrary",),
                                             vmem_limit_bytes=VMEM_LIMIT_BIG),
    )(dproj, wg_in, s_pad, ds2, norm_g, part, land)


def _col_block(width, cap):
    return max(b for b in range(128, cap + 1, 128) if width % b == 0)


def _dw_reduced(lhs_t, rhs, cw, cols, groups, out_dims, out_block, out_index, carried, name):
    na, d, tp = lhs_t.shape
    col0, per_a = cols
    nblk = na * per_a
    rg = d // groups
    hh = rg // 2

    def body(*refs):
        if carried is None:
            l_ref, r_ref, p32_ref, pbf_ref, res, rbuf, send_sems, recv_sems = refs
            exchange = _Exchange([], [])
        else:
            (l_ref, r_ref, part_ref, p32_ref, pbf_ref, land_ref, res, rbuf, send_sems, recv_sems,
             xsend, xrecv) = refs
            exchange = _chip_exchange(part_ref, land_ref, xsend, xrecv, carried[1])
        x, y, c = _mesh_pos()
        t = pl.program_id(0)
        u = jnp.maximum(t - 1, 0)

        def to_sibling(blk):
            return pltpu.make_async_remote_copy(
                src_ref=res.at[blk % 2, :, pl.ds((1 - c) * hh, hh), :], dst_ref=rbuf.at[blk % 2],
                send_sem=send_sems.at[blk], recv_sem=recv_sems.at[blk],
                device_id=(x, y, 1 - c), device_id_type=MESH)

        @pl.when(t == 0)
        def _():
            exchange.start()

        @pl.when(t < nblk)
        def _():
            res[t % 2] = jnp.dot(l_ref[...], r_ref[...], preferred_element_type=F32).reshape(groups, rg, cw)

        @pl.when(t >= 1)
        def _():
            to_sibling(u).wait_recv()
            p = res[u % 2, :, pl.ds(c * hh, hh), :] + rbuf[u % 2]
            p32_ref[...] = p.reshape(p32_ref.shape)
            pbf_ref[...] = p.reshape(pbf_ref.shape).astype(BF16)

        @pl.when(t < nblk)
        def _():
            to_sibling(t).start()

        @pl.when(t >= 1)
        def _():
            to_sibling(u).wait_send()

        @pl.when(t == nblk)
        def _():
            exchange.finish()

    any_spec = pl.BlockSpec(memory_space=pl.ANY)
    last = nblk - 1
    out_spec = pl.BlockSpec(out_block, lambda t: out_index(jnp.maximum(t - 1, 0)))
    extra = [] if carried is None else [carried[0]]
    outs = pl.pallas_call(
        body, name=name, grid=(nblk + 1,),
        in_specs=[pl.BlockSpec((None, d, tp), lambda t: (jnp.minimum(t, last) // per_a, 0, 0)),
                  pl.BlockSpec((None, tp, cw), lambda t: (jnp.minimum(t, last) // per_a, 0,
                                                          col0 + jnp.minimum(t, last) % per_a))]
        + [any_spec] * len(extra),
        out_specs=[out_spec, out_spec] + [any_spec] * len(extra),
        out_shape=[jax.ShapeDtypeStruct(out_dims, F32), jax.ShapeDtypeStruct(out_dims, BF16)]
        + [jax.ShapeDtypeStruct(carried[2], e.dtype) for e in extra],
        scratch_shapes=[pltpu.VMEM((2, groups, rg, cw), F32), pltpu.VMEM((2, groups, hh, cw), F32),
                        pltpu.SemaphoreType.DMA((nblk,)), pltpu.SemaphoreType.DMA((nblk,))]
        + [pltpu.SemaphoreType.DMA((3,)), pltpu.SemaphoreType.DMA((3,))] * len(extra),
        compiler_params=_params(("arbitrary",)),
    )(lhs_t, rhs, *extra)
    return outs[0], outs[1], (outs[2] if extra else None)


def _conv_a_taps(first_lag, last_lag):
    out = []
    for r in range(8):
        taps = [(q, 8 * q + r) for q in range(5) if first_lag <= 8 * q + r <= last_lag]
        if taps:
            out.append((r, taps))
    return out


def _mix_fwd(s_pad, proj, target, w3, wa, wb, conv_a_b, ln_g, ln_b, b_a_out, final_g, norm_g):
    tp, d = s_pad.shape
    nt = tp // TM
    nrb = TM // RB
    shl = TM + SHIFT_ROWS

    def body(s_ref, proj_ref, tgt_ref, w3_ref, wa_ref, wb_ref, cab_ref, lng_ref, lnb_ref, bao_ref, fg_ref, ng_ref,
             ca_ref, cb_ref, ya_ref, yb_ref, abmt_ref, ds2_ref, ht_ref, loss_ref, dfg_ref,
             abm_ref, ext_a, ext_b, sh, s2_s, lacc, gacc):
        i = pl.program_id(0)

        def split(k, rows):
            return proj_ref[rows, k * d:(k + 1) * d].astype(F32)

        s_in = s_ref[...]
        h = s_in * lax.rsqrt(jnp.mean(s_in * s_in, axis=-1, keepdims=True) + EPS) * ng_ref[...]
        ht_ref[...] = h.T.astype(BF16)

        @pl.when(i == 0)
        def _():
            ext_a[0:HALO_A, :] = jnp.zeros((HALO_A, d), F32)
            ext_b[0:HALO_B, :] = jnp.zeros((HALO_B, d), F32)
            lacc[...] = jnp.zeros_like(lacc)
            gacc[...] = jnp.zeros_like(gacc)

        def conv_in(rb, carry):
            rows = _rows(rb)
            ua0 = split(0, rows) * _sigmoid(split(1, rows))
            ext_a[pl.ds(pl.multiple_of(HALO_A + rb * RB, 8), RB), :] = ua0
            ext_b[pl.ds(pl.multiple_of(HALO_B + rb * RB, 8), RB), :] = split(4, rows) * split(5, rows)
            ca_ref[rows, :] = jnp.broadcast_to(cab_ref[...], (RB, d))
            return carry
        lax.fori_loop(0, nrb, conv_in, 0)

        for r, taps in _conv_a_taps(HALO_A - CONV_A + 1, HALO_A):
            if r == 0:
                src = ext_a
            else:
                sh[...] = ext_a[r:r + shl, :]
                src = sh

            def conv_acc(rb, carry, src=src, taps=taps):
                rows = _rows(rb)
                acc = ca_ref[rows, :]
                for q, lag in taps:
                    k = lag - (HALO_A - CONV_A + 1)
                    acc = acc + src[pl.ds(pl.multiple_of(rb * RB + 8 * q, 8), RB), :] * wa_ref[k:k + 1, :]
                ca_ref[rows, :] = acc
                return carry
            lax.fori_loop(0, nrb, conv_acc, 0)
        ext_a[0:HALO_A, :] = ext_a[TM:TM + HALO_A, :]

        cb_ref[...] = ext_b[HALO_B:HALO_B + TM, :] * wb_ref[2:3, :]
        for k in range(CONV_B - 1):
            off = HALO_B - CONV_B + 1 + k
            sh[0:TM, :] = ext_b[off:off + TM, :]
            cb_ref[...] += sh[0:TM, :] * wb_ref[k:k + 1, :]
        ext_b[0:HALO_B, :] = ext_b[TM:TM + HALO_B, :]

        def branches(rb, carry):
            rows = _rows(rb)
            ca = ca_ref[rows, :]
            mu = jnp.mean(ca, axis=-1, keepdims=True)
            xc = ca - mu
            rstd = lax.rsqrt(jnp.mean(xc * xc, axis=-1, keepdims=True) + EPS)
            ln = xc * rstd * lng_ref[...] + lnb_ref[...]
            ua = ln * _sigmoid(ln)
            a_z = split(2, rows)
            abm_ref[0, rows, :] = (ua * (a_z * _sigmoid(a_z))).astype(BF16)
            b_z = split(6, rows)
            ub = split(3, rows) * cb_ref[rows, :]
            abm_ref[1, rows, :] = (ub * (b_z * _sigmoid(b_z))).astype(BF16)
            return carry
        lax.fori_loop(0, nrb, branches, 0)

        ya_ref[...] = jnp.dot(abm_ref[0], w3_ref[0], preferred_element_type=F32) + bao_ref[...]
        yb_ref[...] = jnp.dot(abm_ref[1], w3_ref[1], preferred_element_type=F32)

        def merge(rb, carry):
            rows = _rows(rb)
            m = _sigmoid(split(7, rows)) * ya_ref[rows, :] + _sigmoid(split(8, rows)) * yb_ref[rows, :]
            abm_ref[2, rows, :] = m.astype(BF16)
            return carry
        lax.fori_loop(0, nrb, merge, 0)

        s2_s[...] = s_ref[...] + jnp.dot(abm_ref[2], w3_ref[2], preferred_element_type=F32)
        for k in range(3):
            abmt_ref[k] = abm_ref[k].astype(F32).T.astype(BF16)
        live = (i > 0).astype(F32)

        def head(rb, carry):
            rows = _rows(rb)
            s2 = s2_s[rows, :]
            r2 = lax.rsqrt(jnp.mean(s2 * s2, axis=-1, keepdims=True) + EPS)
            diff = (s2 * r2 * fg_ref[...] - tgt_ref[rows, :]) * live
            lacc[...] += diff * diff
            dy = diff * (1.0 / d)
            gacc[...] += (dy * s2 * r2).reshape(RB // 8, 8, d).sum(axis=0)
            t = dy * fg_ref[...]
            ds2_ref[rows, :] = r2 * t - s2 * (r2 * r2 * r2) * jnp.mean(t * s2, axis=-1, keepdims=True)
            return carry
        lax.fori_loop(0, nrb, head, 0)

        @pl.when(i == nt - 1)
        def _():
            loss_ref[...] = jnp.broadcast_to(0.5 * jnp.sum(lacc[...]) * (1.0 / d), (8, 128))
            dfg_ref[...] = jnp.broadcast_to(jnp.sum(gacc[...], axis=0, keepdims=True), (8, d))

    row_f32 = pl.BlockSpec((TM, d), lambda i: (i, 0))
    const = lambda shape: pl.BlockSpec(shape, lambda i: (0,) * len(shape))
    return pl.pallas_call(
        body, name="f2_mix", grid=(nt,),
        in_specs=[row_f32,
                  pl.BlockSpec((TM, N_SPLIT * d), lambda i: (i, 0)),
                  pl.BlockSpec((TM, d), lambda i: (jnp.maximum(i - 1, 0), 0)),
                  const((3, d, d)), const(wa.shape), const(wb.shape)] + [const((1, d))] * 6,
        out_specs=[row_f32, row_f32, row_f32, row_f32,
                   pl.BlockSpec((3, d, TM), lambda i: (0, 0, i)),
                   row_f32, pl.BlockSpec((d, TM), lambda i: (0, i)), const((8, 128)), const((8, d))],
        out_shape=[jax.ShapeDtypeStruct((tp, d), F32)] * 4
        + [jax.ShapeDtypeStruct((3, d, tp), BF16), jax.ShapeDtypeStruct((tp, d), F32),
           jax.ShapeDtypeStruct((d, tp), BF16),
           jax.ShapeDtypeStruct((8, 128), F32), jax.ShapeDtypeStruct((8, d), F32)],
        scratch_shapes=[pltpu.VMEM((3, TM, d), BF16),
                        pltpu.VMEM((HALO_A + TM, d), F32), pltpu.VMEM((HALO_B + TM, d), F32),
                        pltpu.VMEM((shl, d), F32), pltpu.VMEM((TM, d), F32),
                        pltpu.VMEM((RB, d), F32), pltpu.VMEM((8, d), F32)],
        compiler_params=_params(("arbitrary",)),
    )(s_pad, proj, target, w3, wa, wb, conv_a_b, ln_g, ln_b, b_a_out, final_g, norm_g)


def _mix_bwd(ds2, proj, ca, cb, ya, yb, w3, wa, wb, ln_g, ln_b):
    tp, d = ds2.shape
    nt = tp // TM
    nrb = TM // RB
    shl = TM + SHIFT_ROWS
    nt_dims = (((1,), (1,)), ((), ()))

    def body(ds2_ref, proj_ref, ca_ref, cb_ref, ya_ref, yb_ref, w3_ref, wa_ref, wb_ref, lng_ref, lnb_ref,
             dproj_ref, d3_ref, sm_ref, ext_d, ext_e, sh, dm_s, dpa_s, dpb_s, dua0_s, acc):
        step = pl.program_id(0)

        def split(k, rows):
            return proj_ref[rows, k * d:(k + 1) * d].astype(F32)

        def put(k, rows, val):
            dproj_ref[rows, k * d:(k + 1) * d] = val.astype(BF16)

        def accum(row, val):
            acc[row] += val.reshape(RB // 8, 8, d).sum(axis=0)

        @pl.when(step == 0)
        def _():
            ext_d[TM:TM + HALO_A, :] = jnp.zeros((HALO_A, d), F32)
            ext_e[TM:TM + HALO_B, :] = jnp.zeros((HALO_B, d), F32)
            acc[...] = jnp.zeros_like(acc)

        d3_ref[2] = ds2_ref[...].astype(BF16)
        dm_s[...] = lax.dot_general(d3_ref[2], w3_ref[2], nt_dims, preferred_element_type=F32)

        def gates(rb, carry):
            rows = _rows(rb)
            dm = dm_s[rows, :]
            sa = _sigmoid(split(7, rows))
            sb = _sigmoid(split(8, rows))
            ya_v = ya_ref[rows, :]
            yb_v = yb_ref[rows, :]
            put(7, rows, dm * ya_v * sa * (1.0 - sa))
            put(8, rows, dm * yb_v * sb * (1.0 - sb))
            dya = dm * sa
            accum(ROW_DBAO, dya)
            d3_ref[0, rows, :] = dya.astype(BF16)
            d3_ref[1, rows, :] = (dm * sb).astype(BF16)
            return carry
        lax.fori_loop(0, nrb, gates, 0)

        dpa_s[...] = lax.dot_general(d3_ref[0], w3_ref[0], nt_dims, preferred_element_type=F32)
        dpb_s[...] = lax.dot_general(d3_ref[1], w3_ref[1], nt_dims, preferred_element_type=F32)

        def branches(rb, carry):
            rows = _rows(rb)
            ca_v = ca_ref[rows, :]
            mu = jnp.mean(ca_v, axis=-1, keepdims=True)
            xc = ca_v - mu
            rstd = lax.rsqrt(jnp.mean(xc * xc, axis=-1, keepdims=True) + EPS)
            xhat = xc * rstd
            ln = xhat * lng_ref[...] + lnb_ref[...]
            sl = _sigmoid(ln)
            ua = ln * sl
            a_z = split(2, rows)
            sz = _sigmoid(a_z)
            dpa = dpa_s[rows, :]
            put(2, rows, dpa * ua * (sz * (1.0 + a_z * (1.0 - sz))))
            dln = dpa * (a_z * sz) * (sl * (1.0 + ln * (1.0 - sl)))
            accum(ROW_DLNG, dln * xhat)
            accum(ROW_DLNB, dln)
            dxh = dln * lng_ref[...]
            dca = rstd * (dxh - jnp.mean(dxh, axis=-1, keepdims=True)
                          - xhat * jnp.mean(dxh * xhat, axis=-1, keepdims=True))
            accum(ROW_DCAB, dca)
            ext_d[rows, :] = dca
            dua0_s[rows, :] = jnp.zeros((RB, d), F32)
            dm_s[rows, :] = split(0, rows) * _sigmoid(split(1, rows))
            b_z = split(6, rows)
            szb = _sigmoid(b_z)
            dpb = dpb_s[rows, :]
            b_b = split(3, rows)
            cb_v = cb_ref[rows, :]
            put(6, rows, dpb * (b_b * cb_v) * (szb * (1.0 + b_z * (1.0 - szb))))
            dub = dpb * (b_z * szb)
            put(3, rows, dub * cb_v)
            ext_e[rows, :] = dub * b_b
            return carry
        lax.fori_loop(0, nrb, branches, 0)

        for r, taps in _conv_a_taps(0, CONV_A - 1):
            if r == 0:
                src = ext_d
            else:
                sh[...] = ext_d[r:r + shl, :]
                src = sh

            def conv_t(rb, carry, src=src, taps=taps):
                rows = _rows(rb)
                ua0 = dm_s[rows, :]
                dua0 = dua0_s[rows, :]
                for q, lag in taps:
                    k = CONV_A - 1 - lag
                    slab = src[pl.ds(pl.multiple_of(rb * RB + 8 * q, 8), RB), :]
                    dua0 = dua0 + slab * wa_ref[k:k + 1, :]
                    accum(ROW_DWA + k, slab * ua0)
                dua0_s[rows, :] = dua0
                return carry
            lax.fori_loop(0, nrb, conv_t, 0)
        ext_d[TM:TM + HALO_A, :] = ext_d[0:HALO_A, :]

        dpb_s[...] = ext_e[0:TM, :] * wb_ref[CONV_B - 1:CONV_B, :]
        for lag in range(CONV_B):
            k = CONV_B - 1 - lag
            if lag > 0:
                sh[0:TM, :] = ext_e[lag:lag + TM, :]
                dpb_s[...] += sh[0:TM, :] * wb_ref[k:k + 1, :]
            src = ext_e if lag == 0 else sh

            def conv_b_w(rb, carry, src=src, k=k):
                rows = _rows(rb)
                accum(ROW_DWB + k, src[rows, :] * (split(4, rows) * split(5, rows)))
                return carry
            lax.fori_loop(0, nrb, conv_b_w, 0)
        ext_e[TM:TM + HALO_B, :] = ext_e[0:HALO_B, :]

        def inputs(rb, carry):
            rows = _rows(rb)
            dua0 = dua0_s[rows, :]
            a_val = split(0, rows)
            sg = _sigmoid(split(1, rows))
            put(0, rows, dua0 * sg)
            put(1, rows, dua0 * a_val * sg * (1.0 - sg))
            dcbin = dpb_s[rows, :]
            put(4, rows, dcbin * split(5, rows))
            put(5, rows, dcbin * split(4, rows))
            return carry
        lax.fori_loop(0, nrb, inputs, 0)

        @pl.when(step == nt - 1)
        def _():
            for row in range(SM_ROWS):
                sm_ref[row:row + 1, :] = jnp.sum(acc[row], axis=0, keepdims=True)

    rev = lambda i: (nt - 1 - i, 0)
    row_f32 = pl.BlockSpec((TM, d), rev)
    const = lambda shape: pl.BlockSpec(shape, lambda i: (0,) * len(shape))
    return pl.pallas_call(
        body, name="b1_mix", grid=(nt,),
        in_specs=[row_f32, pl.BlockSpec((TM, N_SPLIT * d), rev), row_f32, row_f32, row_f32, row_f32,
                  const((3, d, d)), const(wa.shape), const(wb.shape), const((1, d)), const((1, d))],
        out_specs=[pl.BlockSpec((TM, N_SPLIT * d), rev),
                   pl.BlockSpec((3, TM, d), lambda i: (0, nt - 1 - i, 0)),
                   const((SM_ROWS, d))],
        out_shape=[jax.ShapeDtypeStruct((tp, N_SPLIT * d), BF16), jax.ShapeDtypeStruct((3, tp, d), BF16),
                   jax.ShapeDtypeStruct((SM_ROWS, d), F32)],
        scratch_shapes=[pltpu.VMEM((TM + HALO_A, d), F32), pltpu.VMEM((TM + HALO_B, d), F32),
                        pltpu.VMEM((shl, d), F32), pltpu.VMEM((TM, d), F32), pltpu.VMEM((TM, d), F32),
                        pltpu.VMEM((TM, d), F32), pltpu.VMEM((TM, d), F32),
                        pltpu.VMEM((SM_ROWS, 8, d), F32)],
        compiler_params=_params(("arbitrary",)),
    )(ds2, proj, ca, cb, ya, yb, w3, wa, wb, ln_g, ln_b)


def kernel(x, meta_tokens, norm_g, w_in, conv_a_w, conv_a_b, ln_a_g, ln_a_b, w_a_out, b_a_out, conv_b_w, w_b_out, w_out, final_g, loss_target, m_meta_tokens, m_norm_g, m_w_in, m_conv_a_w, m_conv_a_b, m_ln_a_g, m_ln_a_b, m_w_a_out, m_b_a_out, m_conv_b_w, m_w_b_out, m_w_out, m_final_g, v_meta_tokens, v_norm_g, v_w_in, v_conv_a_w, v_conv_a_b, v_ln_a_g, v_ln_a_b, v_w_a_out, v_b_a_out, v_conv_b_w, v_w_b_out, v_w_out, v_final_g):
    seq, d = x.shape[1], x.shape[2]
    dc = meta_tokens.shape[1]
    sw = w_in.shape[2]
    rsh = w_a_out.shape[1]
    xi, yi, ci = _mesh_pos()
    me = 2 * xi + yi
    pos = jnp.stack([ci, me]).astype(jnp.int32)

    conv_rows = HALO_A + HALO_B + 8
    convs = jnp.concatenate([
        jnp.pad(conv_a_w[0], ((0, HALO_A - CONV_A), (0, 0))),
        jnp.pad(conv_b_w[0], ((0, HALO_B - CONV_B), (0, 0))), jnp.zeros((8, dc), F32)], axis=0)[None]
    w3_own = jnp.stack([w_a_out[0], w_b_out[0], w_out[0]])
    (metag,) = _all_gather([_place_own(meta_tokens[None], pos, F32, "place_meta")])
    meta_full = jnp.transpose(metag[0], (1, 0, 2)).reshape(N_META, N_CHIPS * dc)
    fg2 = final_g.reshape(1, d)

    first_tile = jnp.concatenate([jnp.zeros((TM - N_META, d), F32), meta_full], axis=0)
    s_pad = jnp.concatenate([first_tile, x[0]], axis=0)

    proj, (wg_in, wg3, convg) = _proj_fwd(s_pad, norm_g, [_place_own(w_in, pos, BF16, "place_in"),
                                                          _place_own(w3_own, pos, BF16, "place_sq"),
                                                          _place_own(convs, pos, F32, "place_conv")], pos)
    w3 = wg3.reshape(3, N_CHIPS * rsh, d)
    convg = jnp.transpose(convg[0], (1, 0, 2)).reshape(conv_rows, N_CHIPS * dc)
    wa_full = convg[0:HALO_A]
    wb_full = convg[HALO_A:HALO_A + HALO_B]
    ca, cb, ya, yb, abm_t, ds2, h_t, loss8, dfg8 = _mix_fwd(
        s_pad, proj, loss_target[0], w3, wa_full, wb_full, conv_a_b, ln_a_g, ln_a_b, b_a_out, fg2, norm_g)
    dproj, d3, sm = _mix_bwd(ds2, proj, ca, cb, ya, yb, w3, wa_full, wb_full, ln_a_g, ln_a_b)
    cw_sq = _col_block(d, 512)
    p32_sq, pbf_sq, _ = _dw_reduced(
        abm_t, d3, cw_sq, (0, d // cw_sq), N_CHIPS, (3, N_CHIPS, rsh // 2, d), (None, N_CHIPS, rsh // 2, cw_sq),
        lambda u: (u // (d // cw_sq), 0, 0, u % (d // cw_sq)), None, "dw_square")
    cw_in = _col_block(sw, 768)
    ncol = sw // cw_in
    in_dims, in_block = (1, N_CHIPS // 2, d // 2, sw), (None, None, d // 2, cw_in)
    in_index = lambda u: (0, u // ncol, 0, u % ncol)
    land_dims = (1, N_CHIPS, d // 2, sw)
    p32_lo, pbf_lo, l_sq = _dw_reduced(
        h_t[None], dproj[None], cw_in, (0, 2 * ncol), 1, in_dims, in_block, in_index,
        (pbf_sq, None, pbf_sq.shape), "dw_in_lo")
    p32_hi, pbf_hi, l_in = _dw_reduced(
        h_t[None], dproj[None], cw_in, (2 * ncol, 2 * ncol), 1, in_dims, in_block, in_index,
        (pbf_lo, 0, land_dims), "dw_in_hi")
    ds, dng8, l_in = _dh_bwd(dproj, wg_in, s_pad, ds2, norm_g, pbf_hi, l_in, 1)
    half_in = _sum_chips([p32_lo, p32_hi], l_in, pos, "rs_sum_in")
    half_sq = _sum_chips([p32_sq], l_sq, pos, "rs_sum_sq")
    tail_row = lax.broadcasted_iota(jnp.int32, (8, d), 0)
    tail = jnp.where(tail_row == 0, dng8, jnp.where(tail_row == 1, dfg8,
                     jnp.where(tail_row == 2, loss8[0, 0], 0.0)))
    block = jnp.concatenate([sm, ds[TM - N_META:TM], tail], axis=0)
    (other_in, other_sq), red = _sibling_swap([half_in, half_sq], block)
    col = lax.dynamic_slice(red, (0, me * dc), (AR_ROWS, dc))
    g_small = {
        "meta_tokens": col[ROW_DMETA:ROW_DMETA + N_META],
        "norm_g": red[ROW_DNG:ROW_DNG + 1],
        "conv_a_w": col[ROW_DWA:ROW_DWA + CONV_A][None],
        "conv_a_b": red[ROW_DCAB:ROW_DCAB + 1],
        "ln_a_g": red[ROW_DLNG:ROW_DLNG + 1],
        "ln_a_b": red[ROW_DLNB:ROW_DLNB + 1],
        "b_a_out": red[ROW_DBAO:ROW_DBAO + 1],
        "conv_b_w": col[ROW_DWB:ROW_DWB + CONV_B][None],
        "final_g": red[ROW_DFG],
    }

    upd_in = _adam_halves([w_in], [m_w_in], [v_w_in], half_in, other_in, pos, "adam_in")
    upd_sq = _adam_halves([w_a_out, w_b_out, w_out], [m_w_a_out, m_w_b_out, m_w_out],
                          [v_w_a_out, v_w_b_out, v_w_out], half_sq, other_sq, pos, "adam_sq")
    small_w = {"meta_tokens": (meta_tokens, m_meta_tokens, v_meta_tokens), "norm_g": (norm_g, m_norm_g, v_norm_g),
               "conv_a_w": (conv_a_w, m_conv_a_w, v_conv_a_w), "conv_a_b": (conv_a_b, m_conv_a_b, v_conv_a_b),
               "ln_a_g": (ln_a_g, m_ln_a_g, v_ln_a_g), "ln_a_b": (ln_a_b, m_ln_a_b, v_ln_a_b),
               "b_a_out": (b_a_out, m_b_a_out, v_b_a_out), "conv_b_w": (conv_b_w, m_conv_b_w, v_conv_b_w),
               "final_g": (final_g, m_final_g, v_final_g)}
    names_small = list(small_w)
    as2d = lambda t: t.reshape(-1, t.shape[-1])
    upd_small = _adam_small([(as2d(small_w[k][0]), as2d(g_small[k]), as2d(small_w[k][1]), as2d(small_w[k][2]))
                             for k in names_small])

    grads, deltas, new_m, new_v = dict(g_small), {}, {}, {}
    for k, upd in zip(names_small, upd_small):
        deltas[k], new_m[k], new_v[k] = [t.reshape(small_w[k][0].shape) for t in upd]
    grads["w_in"], deltas["w_in"], new_m["w_in"], new_v["w_in"] = upd_in
    for idx, k in enumerate(["w_a_out", "w_b_out", "w_out"]):
        grads[k], deltas[k], new_m[k], new_v[k] = upd_sq[4 * idx:4 * idx + 4]

    loss = red[ROW_LOSS, 0]
    grad_x = ds[TM:][None]
    order = ["meta_tokens", "norm_g", "w_in", "conv_a_w", "conv_a_b", "ln_a_g", "ln_a_b", "w_a_out", "b_a_out",
             "conv_b_w", "w_b_out", "w_out", "final_g"]
    return (loss, grad_x, *[grads[k] for k in order], *[deltas[k] for k in order],
            *[new_m[k] for k in order], *[new_v[k] for k in order])
```

```python
import functools

import jax
import jax.numpy as jnp
from jax import lax
from jax.experimental import pallas as pl
from jax.experimental.pallas import tpu as pltpu

F32 = jnp.float32
BF16 = jnp.bfloat16
MESH = pl.DeviceIdType.MESH

EPS = 1e-6
N_META = 16
N_SPLIT = 9
CONV_A = 31
CONV_B = 3
HALO_A = 32
HALO_B = 8
SHIFT_ROWS = 24
TM = 256
RB = 64
N_ROW_TILES_BIG = 8
ROW_BLOCK = 256
N_CHIPS = 4
VMEM_LIMIT = 56 * 1024 * 1024
VMEM_LIMIT_BIG = 62 * 1024 * 1024

ADAM_LR = 0.001
ADAM_B1 = 0.9
ADAM_B2 = 0.999
ADAM_EPS = 1e-08
ADAM_WD = 0.01
ADAM_STEP = 10

ROW_DWA = 0
ROW_DWB = 32
ROW_DCAB = 40
ROW_DLNG = 41
ROW_DLNB = 42
ROW_DBAO = 43
SM_ROWS = 48
ROW_DMETA = 48
ROW_DNG = 64
ROW_DFG = 65
ROW_LOSS = 66
AR_ROWS = 72


def _sigmoid(v):
    return 0.5 * jnp.tanh(0.5 * v) + 0.5


def _params(sem, **kw):
    return pltpu.CompilerParams(dimension_semantics=sem, vmem_limit_bytes=VMEM_LIMIT, **kw)


def _rows(rb):
    return pl.ds(pl.multiple_of(rb * RB, RB), RB)


def _mesh_pos():
    x, y, c = lax.axis_index("x"), lax.axis_index("y"), lax.axis_index("c")
    return x, y, c


def _half(ref, j, c):
    h = ref.shape[2] // 2
    return ref.at[:, j, pl.ds(c * h, h), :]


def _place_own(shard, pos, dtype, name):
    s, r, c = shard.shape
    rb = ROW_BLOCK if r % ROW_BLOCK == 0 else r

    def body(pos_ref, x_ref, o_ref):
        o_ref[...] = x_ref[...].astype(dtype)

    return pl.pallas_call(
        body, name=name,
        grid_spec=pltpu.PrefetchScalarGridSpec(
            num_scalar_prefetch=1, grid=(s, r // rb),
            in_specs=[pl.BlockSpec((None, rb, c), lambda si, b, pos_ref: (si, b, 0))],
            out_specs=pl.BlockSpec((None, None, rb, c), lambda si, b, pos_ref: (si, pos_ref[1], b, 0))),
        out_shape=jax.ShapeDtypeStruct((s, N_CHIPS, r, c), dtype),
        compiler_params=_params(("arbitrary",) * 2),
    )(pos, shard)


def _all_gather(bufs):
    n = len(bufs)

    def body(*refs):
        outs = refs[n:2 * n]
        send_sems, recv_sems = refs[2 * n:]
        x, y, c = _mesh_pos()
        me = 2 * x + y
        sibling = (x, y, 1 - c)
        chips = [(1 - x, y), (x, 1 - y), (1 - x, 1 - y)]

        def remote(a, k, piece_src, piece_dst, to):
            return pltpu.make_async_remote_copy(
                src_ref=piece_src, dst_ref=piece_dst, send_sem=send_sems.at[6 * a + k],
                recv_sem=recv_sems.at[6 * a + k], device_id=to, device_id_type=MESH)

        sends = []
        for a in range(n):
            mine = _half(outs[a], me, c)
            for k, (px, py) in enumerate(chips):
                sends.append(remote(a, k, mine, mine, (px, py, c)))
        for cp in sends:
            cp.start()
        for a in range(n):
            for k, (px, py) in enumerate(chips):
                piece = _half(outs[a], 2 * px + py, c)
                remote(a, k, piece, piece, (px, py, c)).wait_recv()
                fwd = remote(a, 3 + k, piece, piece, sibling)
                fwd.start()
                sends.append(fwd)
        for a in range(n):
            for k, (px, py) in enumerate(chips):
                piece = _half(outs[a], 2 * px + py, 1 - c)
                remote(a, 3 + k, piece, piece, sibling).wait_recv()
        for cp in sends:
            cp.wait_send()

    any_spec = pl.BlockSpec(memory_space=pl.ANY)
    return pl.pallas_call(
        body, name="ag_weights",
        in_specs=[any_spec] * n, out_specs=[any_spec] * n,
        out_shape=[jax.ShapeDtypeStruct(b.shape, b.dtype) for b in bufs],
        input_output_aliases={a: a for a in range(n)},
        scratch_shapes=[pltpu.SemaphoreType.DMA((6 * n,)), pltpu.SemaphoreType.DMA((6 * n,))],
    )(*bufs)


class _Exchange:
    def __init__(self, sends, recvs):
        self.sends, self.recvs = sends, recvs

    @staticmethod
    def _each(pairs, act):
        for cond, cp in pairs:
            if cond is None:
                act(cp)
            else:
                pl.when(cond)(functools.partial(act, cp))

    def start(self):
        self._each(self.sends, lambda cp: cp.start())

    def finish(self):
        self._each(self.recvs, lambda cp: cp.wait_recv())
        self._each(self.sends, lambda cp: cp.wait_send())


def _chip_exchange(part_ref, land_ref, send_sems, recv_sems, half=None):
    x, y, c = _mesh_pos()
    me = 2 * x + y
    sends, recvs = [], []
    for k, (px, py) in enumerate([(1 - x, y), (x, 1 - y), (1 - x, 1 - y)]):
        sems = dict(send_sem=send_sems.at[k], recv_sem=recv_sems.at[k], device_id=(px, py, c), device_id_type=MESH)
        slot = 2 * px + py if half is None else py
        sends.append((None if half is None else px == half, pltpu.make_async_remote_copy(
            src_ref=part_ref.at[:, slot], dst_ref=land_ref.at[:, me], **sems)))
        landed = land_ref.at[:, 2 * px + py]
        recvs.append((None if half is None else x == half,
                      pltpu.make_async_remote_copy(src_ref=landed, dst_ref=landed, **sems)))
    return _Exchange(sends, recvs)

def _sibling_swap(halves, small):
    n = len(halves)

    def body(*refs):
        ins, small_ref, outs, red_ref = refs[:n], refs[n], refs[n + 1:2 * n + 1], refs[2 * n + 1]
        send_sems, recv_sems = refs[2 * n + 2:2 * n + 4]
        reduce = _SmallAllReduce(small_ref, red_ref, *refs[2 * n + 4:])
        x, y, c = _mesh_pos()
        copies = [pltpu.make_async_remote_copy(
            src_ref=ins[a], dst_ref=outs[a], send_sem=send_sems.at[a], recv_sem=recv_sems.at[a],
            device_id=(x, y, 1 - c), device_id_type=MESH) for a in range(n)]
        reduce.start()
        for cp in copies:
            cp.start()
        reduce.between_chips()
        reduce.finish()
        for cp in copies:
            cp.wait()

    any_spec = pl.BlockSpec(memory_space=pl.ANY)
    vm = pl.BlockSpec(memory_space=pltpu.VMEM)
    outs = pl.pallas_call(
        body, name="rs_swap",
        in_specs=[any_spec] * n + [vm], out_specs=[any_spec] * n + [vm],
        out_shape=[jax.ShapeDtypeStruct(h.shape, h.dtype) for h in halves]
        + [jax.ShapeDtypeStruct(small.shape, F32)],
        scratch_shapes=[pltpu.SemaphoreType.DMA((n,)), pltpu.SemaphoreType.DMA((n,))]
        + _SmallAllReduce.scratch(*small.shape),
    )(*halves, small)
    return outs[:n], outs[n]


class _SmallAllReduce:
    def __init__(self, x_ref, out_ref, sib_ref, part_ref, peers_ref, send_sems, recv_sems):
        self.x_ref, self.out_ref, self.sib_ref, self.part_ref, self.peers_ref = x_ref, out_ref, sib_ref, part_ref, peers_ref
        x, y, c = _mesh_pos()
        self.me = 2 * x + y
        self.swap = pltpu.make_async_remote_copy(
            src_ref=x_ref, dst_ref=sib_ref, send_sem=send_sems.at[0], recv_sem=recv_sems.at[0],
            device_id=(x, y, 1 - c), device_id_type=MESH)
        self.sends, self.recvs = [], []
        for k, (px, py) in enumerate([(1 - x, y), (x, 1 - y), (1 - x, 1 - y)]):
            sems = dict(send_sem=send_sems.at[1 + k], recv_sem=recv_sems.at[1 + k],
                        device_id=(px, py, c), device_id_type=MESH)
            self.sends.append(pltpu.make_async_remote_copy(src_ref=part_ref, dst_ref=peers_ref.at[self.me], **sems))
            landed = peers_ref.at[2 * px + py]
            self.recvs.append(pltpu.make_async_remote_copy(src_ref=landed, dst_ref=landed, **sems))

    @staticmethod
    def scratch(rows, d):
        return [pltpu.VMEM((rows, d), F32), pltpu.VMEM((rows, d), F32), pltpu.VMEM((N_CHIPS, rows, d), F32),
                pltpu.SemaphoreType.DMA((4,)), pltpu.SemaphoreType.DMA((4,))]

    def start(self):
        self.swap.start()

    def between_chips(self):
        self.swap.wait()
        self.part_ref[...] = self.x_ref[...] + self.sib_ref[...]
        self.peers_ref[self.me] = self.part_ref[...]
        for cp in self.sends:
            cp.start()

    def finish(self):
        for cp in self.recvs:
            cp.wait_recv()
        for cp in self.sends:
            cp.wait_send()
        p = self.peers_ref
        self.out_ref[...] = ((p[0] + p[1]) + p[2]) + p[3]


def _sum_chips(p32s, landed, pos, name):
    s, _, h, c = p32s[0].shape
    hb = min(h, ROW_BLOCK)
    per = N_CHIPS // len(p32s)

    def body(pos_ref, *refs):
        own_refs, (l1_ref, l2_ref, l3_ref, out_ref) = refs[:len(p32s)], refs[len(p32s):]
        own = own_refs[0][...]
        for k in range(1, len(p32s)):
            own = jnp.where(pos_ref[1] // per == k, own_refs[k][...], own)
        out_ref[...] = ((own + l1_ref[...].astype(F32)) + l2_ref[...].astype(F32)) + l3_ref[...].astype(F32)

    own_spec = pl.BlockSpec((None, None, hb, c), lambda si, b, pos_ref: (si, pos_ref[1] % per, b, 0))

    def slot(k):
        return pl.BlockSpec((None, None, hb, c), lambda si, b, pos_ref: (si, (pos_ref[1] + k) % N_CHIPS, b, 0))

    return pl.pallas_call(
        body, name=name,
        grid_spec=pltpu.PrefetchScalarGridSpec(
            num_scalar_prefetch=1, grid=(s, h // hb),
            in_specs=[own_spec] * len(p32s) + [slot(1), slot(2), slot(3)],
            out_specs=pl.BlockSpec((None, hb, c), lambda si, b, pos_ref: (si, b, 0))),
        out_shape=jax.ShapeDtypeStruct((s, h, c), F32),
        compiler_params=_params(("arbitrary",) * 2),
    )(pos, *p32s, landed, landed, landed)


def _adamw(w, g, m, v):
    m = ADAM_B1 * m + (1.0 - ADAM_B1) * g
    v = ADAM_B2 * v + (1.0 - ADAM_B2) * (g * g)
    m_hat = m / (1.0 - ADAM_B1 ** ADAM_STEP)
    v_hat = v / (1.0 - ADAM_B2 ** ADAM_STEP)
    delta = -ADAM_LR * (m_hat / (jnp.sqrt(v_hat) + ADAM_EPS) + ADAM_WD * w)
    return delta, m, v


def _adam_halves(ws, ms, vs, g_own, g_recv, pos, name):
    n = len(ws)
    _, r, c = ws[0].shape
    h = r // 2
    rb = min(h, ROW_BLOCK)
    nb = h // rb

    def body(pos_ref, *refs):
        w_refs, m_refs, v_refs = refs[:n], refs[n:2 * n], refs[2 * n:3 * n]
        go_ref, gr_ref = refs[3 * n:3 * n + 2]
        outs = refs[3 * n + 2:]
        mine = pl.program_id(0) == pos_ref[0]
        for a in range(n):
            g = jnp.where(mine, go_ref[a], gr_ref[a])
            delta, m, v = _adamw(w_refs[a][...], g, m_refs[a][...], v_refs[a][...])
            outs[4 * a][...], outs[4 * a + 1][...], outs[4 * a + 2][...], outs[4 * a + 3][...] = g, delta, m, v

    spec_w = pl.BlockSpec((None, rb, c), lambda hf, b, pos_ref: (0, hf * nb + b, 0))
    spec_g = pl.BlockSpec((n, rb, c), lambda hf, b, pos_ref: (0, b, 0))
    return pl.pallas_call(
        body, name=name,
        grid_spec=pltpu.PrefetchScalarGridSpec(
            num_scalar_prefetch=1, grid=(2, nb), in_specs=[spec_w] * (3 * n) + [spec_g] * 2,
            out_specs=[spec_w] * (4 * n)),
        out_shape=[jax.ShapeDtypeStruct((1, r, c), F32)] * (4 * n),
        compiler_params=_params(("arbitrary",) * 2),
    )(pos, *ws, *ms, *vs, g_own, g_recv)


def _adam_small(items):
    n = len(items)

    def body(*refs):
        ins, outs = refs[:4 * n], refs[4 * n:]
        for a in range(n):
            w_ref, g_ref, m_ref, v_ref = ins[4 * a:4 * a + 4]
            d, m, v = _adamw(w_ref[...], g_ref[...], m_ref[...], v_ref[...])
            outs[3 * a][...] = d
            outs[3 * a + 1][...] = m
            outs[3 * a + 2][...] = v

    vm = pl.BlockSpec(memory_space=pltpu.VMEM)
    flat = [t for it in items for t in it]
    outs = pl.pallas_call(
        body, name="adam_small", in_specs=[vm] * (4 * n), out_specs=[vm] * (3 * n),
        out_shape=[jax.ShapeDtypeStruct(it[0].shape, F32) for it in items for _ in range(3)],
    )(*flat)
    return [tuple(outs[3 * a:3 * a + 3]) for a in range(n)]


def _shard_of_step(js, me):
    flip = jnp.where(js == 1, 2, jnp.where(js == 2, 1, jnp.where(js == 3, 3, 0)))
    return lax.bitwise_xor(me, flip)


def _proj_fwd(s_pad, norm_g, bufs, pos):
    tp, d = s_pad.shape
    _, nsh, _, sw = bufs[0].shape
    tmb = tp // N_ROW_TILES_BIG
    n = len(bufs)

    def body(pos_ref, s_ref, g_ref, *refs):
        proj_ref = refs[n]
        gbufs = refs[n + 1:2 * n + 1]
        wbuf, wsems, send_sems, recv_sems = refs[2 * n + 1:]
        x, y, c = _mesh_pos()
        me = 2 * x + y
        sibling = (x, y, 1 - c)
        chips = [(1 - x, y), (x, 1 - y), (1 - x, 1 - y)]
        js, i = pl.program_id(0), pl.program_id(1)

        def remote(a, k, piece, to):
            return pltpu.make_async_remote_copy(
                src_ref=piece, dst_ref=piece, send_sem=send_sems.at[6 * a + k],
                recv_sem=recv_sems.at[6 * a + k], device_id=to, device_id_type=MESH)

        def fetch(chip, step):
            return pltpu.make_async_copy(gbufs[0].at[0, chip], wbuf.at[step % 2], wsems.at[step % 2])

        def take(a, k):
            px, py = chips[k]
            remote(a, k, _half(gbufs[a], 2 * px + py, c), (px, py, c)).wait_recv()
            remote(a, 3 + k, _half(gbufs[a], 2 * px + py, c), sibling).start()
            remote(a, 3 + k, _half(gbufs[a], 2 * px + py, 1 - c), sibling).wait_recv()

        @pl.when((js == 0) & (i == 0))
        def _():
            for a in range(n):
                for k, (px, py) in enumerate(chips):
                    remote(a, k, _half(gbufs[a], me, c), (px, py, c)).start()
            fetch(me, 0).start()
            fetch(me, 0).wait()

        for k, (px, py) in enumerate(chips):
            ahead = k > 0

            @pl.when((js == k) & (i == N_ROW_TILES_BIG - 2) if ahead else (js == k + 1) & (i == 0))
            def _(k=k, px=px, py=py):
                take(0, k)
                fetch(2 * px + py, k + 1).start()

            @pl.when((js == k + 1) & (i == 0))
            def _(k=k, px=px, py=py):
                fetch(2 * px + py, k + 1).wait()

        s = s_ref[...]
        r = lax.rsqrt(jnp.mean(s * s, axis=-1, keepdims=True) + EPS)
        h = (s * r * g_ref[...]).astype(BF16)
        proj_ref[...] = jnp.dot(h, wbuf[js % 2], preferred_element_type=F32).astype(BF16)

        @pl.when((js == nsh - 1) & (i == N_ROW_TILES_BIG - 1))
        def _():
            for a in range(1, n):
                for k in range(len(chips)):
                    take(a, k)
            for a in range(n):
                for k, (px, py) in enumerate(chips):
                    remote(a, k, _half(gbufs[a], me, c), (px, py, c)).wait_send()
                    remote(a, 3 + k, _half(gbufs[a], 2 * px + py, c), sibling).wait_send()

    any_spec = pl.BlockSpec(memory_space=pl.ANY)
    outs = pl.pallas_call(
        body, name="f1_proj",
        grid_spec=pltpu.PrefetchScalarGridSpec(
            num_scalar_prefetch=1, grid=(nsh, N_ROW_TILES_BIG),
            in_specs=[pl.BlockSpec((tmb, d), lambda js, i, pos_ref: (i, 0)),
                      pl.BlockSpec((1, d), lambda js, i, pos_ref: (0, 0))] + [any_spec] * n,
            out_specs=[pl.BlockSpec((tmb, sw), lambda js, i, pos_ref: (i, _shard_of_step(js, pos_ref[1])))]
            + [any_spec] * n,
            scratch_shapes=[pltpu.VMEM((2, d, sw), BF16), pltpu.SemaphoreType.DMA((2,)),
                            pltpu.SemaphoreType.DMA((6 * n,)), pltpu.SemaphoreType.DMA((6 * n,))]),
        out_shape=[jax.ShapeDtypeStruct((tp, nsh * sw), BF16)]
        + [jax.ShapeDtypeStruct(b.shape, b.dtype) for b in bufs],
        input_output_aliases={3 + a: 1 + a for a in range(n)},
        compiler_params=_params(("arbitrary", "arbitrary")),
    )(pos, s_pad, norm_g, *bufs)
    return outs[0], outs[1:]


def _dh_bwd(dproj, wg_in, s_pad, ds2, norm_g, part, land, half):
    tp, d = s_pad.shape
    _, nsh, _, sw = wg_in.shape
    tmb = tp // N_ROW_TILES_BIG

    def body(dp_ref, w_hbm, s_ref, ds2_ref, g_ref, part_ref, _, ds_ref, dng_ref, land_ref, wbuf, gacc,
             wsem, send_sems, recv_sems):
        exchange = _chip_exchange(part_ref, land_ref, send_sems, recv_sems, half)
        i = pl.program_id(0)

        @pl.when(i == 0)
        def _():
            exchange.start()
            gacc[...] = jnp.zeros_like(gacc)
            whole = pltpu.make_async_copy(w_hbm.at[0], wbuf, wsem)
            whole.start()
            whole.wait()

        dh = None
        for j in range(nsh):
            part = lax.dot_general(dp_ref[:, j * sw:(j + 1) * sw], wbuf[j], (((1,), (1,)), ((), ())),
                                   preferred_element_type=F32)
            dh = part if dh is None else dh + part
        s = s_ref[...]
        r = lax.rsqrt(jnp.mean(s * s, axis=-1, keepdims=True) + EPS)
        gacc[...] += (dh * s * r).reshape(tmb // 8, 8, d).sum(axis=0)
        t = dh * g_ref[...]
        ds_ref[...] = ds2_ref[...] + r * t - s * (r * r * r) * jnp.mean(t * s, axis=-1, keepdims=True)

        @pl.when(i == N_ROW_TILES_BIG - 1)
        def _():
            dng_ref[...] = jnp.broadcast_to(jnp.sum(gacc[...], axis=0, keepdims=True), (8, d))
            exchange.finish()

    any_spec = pl.BlockSpec(memory_space=pl.ANY)
    return pl.pallas_call(
        body, name="b2_dh", grid=(N_ROW_TILES_BIG,),
        in_specs=[pl.BlockSpec((tmb, nsh * sw), lambda i: (i, 0)), any_spec,
                  pl.BlockSpec((tmb, d), lambda i: (i, 0)),
                  pl.BlockSpec((tmb, d), lambda i: (i, 0)),
                  pl.BlockSpec((1, d), lambda i: (0, 0)), any_spec, any_spec],
        out_specs=[pl.BlockSpec((tmb, d), lambda i: (i, 0)),
                   pl.BlockSpec((8, d), lambda i: (0, 0)), any_spec],
        out_shape=[jax.ShapeDtypeStruct((tp, d), F32), jax.ShapeDtypeStruct((8, d), F32),
                   jax.ShapeDtypeStruct(land.shape, land.dtype)],
        input_output_aliases={6: 2},
        scratch_shapes=[pltpu.VMEM((nsh, d, sw), BF16), pltpu.VMEM((8, d), F32), pltpu.SemaphoreType.DMA,
                        pltpu.SemaphoreType.DMA((3,)), pltpu.SemaphoreType.DMA((3,))],
        compiler_params=pltpu.CompilerParams(dimension_semantics=("arbitrary",),
                                             vmem_limit_bytes=VMEM_LIMIT_BIG),
    )(dproj, wg_in, s_pad, ds2, norm_g, part, land)


def _col_block(width, cap):
    return max(b for b in range(128, cap + 1, 128) if width % b == 0)


def _dw_reduced(lhs_t, rhs, cw, cols, groups, out_dims, out_block, out_index, carried, name):
    na, d, tp = lhs_t.shape
    col0, per_a = cols
    nblk = na * per_a
    rg = d // groups
    hh = rg // 2

    def body(*refs):
        if carried is None:
            l_ref, r_ref, p32_ref, pbf_ref, res, rbuf, send_sems, recv_sems = refs
            exchange = _Exchange([], [])
        else:
            (l_ref, r_ref, part_ref, p32_ref, pbf_ref, land_ref, res, rbuf, send_sems, recv_sems,
             xsend, xrecv) = refs
            exchange = _chip_exchange(part_ref, land_ref, xsend, xrecv, carried[1])
        x, y, c = _mesh_pos()
        t = pl.program_id(0)
        u = jnp.maximum(t - 1, 0)

        def to_sibling(blk):
            return pltpu.make_async_remote_copy(
                src_ref=res.at[blk % 2, :, pl.ds((1 - c) * hh, hh), :], dst_ref=rbuf.at[blk % 2],
                send_sem=send_sems.at[blk], recv_sem=recv_sems.at[blk],
                device_id=(x, y, 1 - c), device_id_type=MESH)

        @pl.when(t == 0)
        def _():
            exchange.start()

        @pl.when(t < nblk)
        def _():
            res[t % 2] = jnp.dot(l_ref[...], r_ref[...], preferred_element_type=F32).reshape(groups, rg, cw)

        @pl.when(t >= 1)
        def _():
            to_sibling(u).wait_recv()
            p = res[u % 2, :, pl.ds(c * hh, hh), :] + rbuf[u % 2]
            p32_ref[...] = p.reshape(p32_ref.shape)
            pbf_ref[...] = p.reshape(pbf_ref.shape).astype(BF16)

        @pl.when(t < nblk)
        def _():
            to_sibling(t).start()

        @pl.when(t >= 1)
        def _():
            to_sibling(u).wait_send()

        @pl.when(t == nblk)
        def _():
            exchange.finish()

    any_spec = pl.BlockSpec(memory_space=pl.ANY)
    last = nblk - 1
    out_spec = pl.BlockSpec(out_block, lambda t: out_index(jnp.maximum(t - 1, 0)))
    extra = [] if carried is None else [carried[0]]
    outs = pl.pallas_call(
        body, name=name, grid=(nblk + 1,),
        in_specs=[pl.BlockSpec((None, d, tp), lambda t: (jnp.minimum(t, last) // per_a, 0, 0)),
                  pl.BlockSpec((None, tp, cw), lambda t: (jnp.minimum(t, last) // per_a, 0,
                                                          col0 + jnp.minimum(t, last) % per_a))]
        + [any_spec] * len(extra),
        out_specs=[out_spec, out_spec] + [any_spec] * len(extra),
        out_shape=[jax.ShapeDtypeStruct(out_dims, F32), jax.ShapeDtypeStruct(out_dims, BF16)]
        + [jax.ShapeDtypeStruct(carried[2], e.dtype) for e in extra],
        scratch_shapes=[pltpu.VMEM((2, groups, rg, cw), F32), pltpu.VMEM((2, groups, hh, cw), F32),
                        pltpu.SemaphoreType.DMA((nblk,)), pltpu.SemaphoreType.DMA((nblk,))]
        + [pltpu.SemaphoreType.DMA((3,)), pltpu.SemaphoreType.DMA((3,))] * len(extra),
        compiler_params=_params(("arbitrary",)),
    )(lhs_t, rhs, *extra)
    return outs[0], outs[1], (outs[2] if extra else None)


def _conv_a_taps(first_lag, last_lag):
    out = []
    for r in range(8):
        taps = [(q, 8 * q + r) for q in range(5) if first_lag <= 8 * q + r <= last_lag]
        if taps:
            out.append((r, taps))
    return out


def _mix_fwd(s_pad, proj, target, w3, wa, wb, conv_a_b, ln_g, ln_b, b_a_out, final_g, norm_g):
    tp, d = s_pad.shape
    nt = tp // TM
    nrb = TM // RB
    shl = TM + SHIFT_ROWS

    def body(s_ref, proj_ref, tgt_ref, w3_ref, wa_ref, wb_ref, cab_ref, lng_ref, lnb_ref, bao_ref, fg_ref, ng_ref,
             ca_ref, cb_ref, ya_ref, yb_ref, abmt_ref, ds2_ref, ht_ref, loss_ref, dfg_ref,
             abm_ref, ext_a, ext_b, sh, s2_s, lacc, gacc):
        i = pl.program_id(0)

        def split(k, rows):
            return proj_ref[rows, k * d:(k + 1) * d].astype(F32)

        s_in = s_ref[...]
        h = s_in * lax.rsqrt(jnp.mean(s_in * s_in, axis=-1, keepdims=True) + EPS) * ng_ref[...]
        ht_ref[...] = h.T.astype(BF16)

        @pl.when(i == 0)
        def _():
            ext_a[0:HALO_A, :] = jnp.zeros((HALO_A, d), F32)
            ext_b[0:HALO_B, :] = jnp.zeros((HALO_B, d), F32)
            lacc[...] = jnp.zeros_like(lacc)
            gacc[...] = jnp.zeros_like(gacc)

        def conv_in(rb, carry):
            rows = _rows(rb)
            ua0 = split(0, rows) * _sigmoid(split(1, rows))
            ext_a[pl.ds(pl.multiple_of(HALO_A + rb * RB, 8), RB), :] = ua0
            ext_b[pl.ds(pl.multiple_of(HALO_B + rb * RB, 8), RB), :] = split(4, rows) * split(5, rows)
            ca_ref[rows, :] = jnp.broadcast_to(cab_ref[...], (RB, d))
            return carry
        lax.fori_loop(0, nrb, conv_in, 0)

        for r, taps in _conv_a_taps(HALO_A - CONV_A + 1, HALO_A):
            if r == 0:
                src = ext_a
            else:
                sh[...] = ext_a[r:r + shl, :]
                src = sh

            def conv_acc(rb, carry, src=src, taps=taps):
                rows = _rows(rb)
                acc = ca_ref[rows, :]
                for q, lag in taps:
                    k = lag - (HALO_A - CONV_A + 1)
                    acc = acc + src[pl.ds(pl.multiple_of(rb * RB + 8 * q, 8), RB), :] * wa_ref[k:k + 1, :]
                ca_ref[rows, :] = acc
                return carry
            lax.fori_loop(0, nrb, conv_acc, 0)
        ext_a[0:HALO_A, :] = ext_a[TM:TM + HALO_A, :]

        cb_ref[...] = ext_b[HALO_B:HALO_B + TM, :] * wb_ref[2:3, :]
        for k in range(CONV_B - 1):
            off = HALO_B - CONV_B + 1 + k
            sh[0:TM, :] = ext_b[off:off + TM, :]
            cb_ref[...] += sh[0:TM, :] * wb_ref[k:k + 1, :]
        ext_b[0:HALO_B, :] = ext_b[TM:TM + HALO_B, :]

        def branches(rb, carry):
            rows = _rows(rb)
            ca = ca_ref[rows, :]
            mu = jnp.mean(ca, axis=-1, keepdims=True)
            xc = ca - mu
            rstd = lax.rsqrt(jnp.mean(xc * xc, axis=-1, keepdims=True) + EPS)
            ln = xc * rstd * lng_ref[...] + lnb_ref[...]
            ua = ln * _sigmoid(ln)
            a_z = split(2, rows)
            abm_ref[0, rows, :] = (ua * (a_z * _sigmoid(a_z))).astype(BF16)
            b_z = split(6, rows)
            ub = split(3, rows) * cb_ref[rows, :]
            abm_ref[1, rows, :] = (ub * (b_z * _sigmoid(b_z))).astype(BF16)
            return carry
        lax.fori_loop(0, nrb, branches, 0)

        ya_ref[...] = jnp.dot(abm_ref[0], w3_ref[0], preferred_element_type=F32) + bao_ref[...]
        yb_ref[...] = jnp.dot(abm_ref[1], w3_ref[1], preferred_element_type=F32)

        def merge(rb, carry):
            rows = _rows(rb)
            m = _sigmoid(split(7, rows)) * ya_ref[rows, :] + _sigmoid(split(8, rows)) * yb_ref[rows, :]
            abm_ref[2, rows, :] = m.astype(BF16)
            return carry
        lax.fori_loop(0, nrb, merge, 0)

        s2_s[...] = s_ref[...] + jnp.dot(abm_ref[2], w3_ref[2], preferred_element_type=F32)
        for k in range(3):
            abmt_ref[k] = abm_ref[k].astype(F32).T.astype(BF16)
        live = (i > 0).astype(F32)

        def head(rb, carry):
            rows = _rows(rb)
            s2 = s2_s[rows, :]
            r2 = lax.rsqrt(jnp.mean(s2 * s2, axis=-1, keepdims=True) + EPS)
            diff = (s2 * r2 * fg_ref[...] - tgt_ref[rows, :]) * live
            lacc[...] += diff * diff
            dy = diff * (1.0 / d)
            gacc[...] += (dy * s2 * r2).reshape(RB // 8, 8, d).sum(axis=0)
            t = dy * fg_ref[...]
            ds2_ref[rows, :] = r2 * t - s2 * (r2 * r2 * r2) * jnp.mean(t * s2, axis=-1, keepdims=True)
            return carry
        lax.fori_loop(0, nrb, head, 0)

        @pl.when(i == nt - 1)
        def _():
            loss_ref[...] = jnp.broadcast_to(0.5 * jnp.sum(lacc[...]) * (1.0 / d), (8, 128))
            dfg_ref[...] = jnp.broadcast_to(jnp.sum(gacc[...], axis=0, keepdims=True), (8, d))

    row_f32 = pl.BlockSpec((TM, d), lambda i: (i, 0))
    const = lambda shape: pl.BlockSpec(shape, lambda i: (0,) * len(shape))
    return pl.pallas_call(
        body, name="f2_mix", grid=(nt,),
        in_specs=[row_f32,
                  pl.BlockSpec((TM, N_SPLIT * d), lambda i: (i, 0)),
                  pl.BlockSpec((TM, d), lambda i: (jnp.maximum(i - 1, 0), 0)),
                  const((3, d, d)), const(wa.shape), const(wb.shape)] + [const((1, d))] * 6,
        out_specs=[row_f32, row_f32, row_f32, row_f32,
                   pl.BlockSpec((3, d, TM), lambda i: (0, 0, i)),
                   row_f32, pl.BlockSpec((d, TM), lambda i: (0, i)), const((8, 128)), const((8, d))],
        out_shape=[jax.ShapeDtypeStruct((tp, d), F32)] * 4
        + [jax.ShapeDtypeStruct((3, d, tp), BF16), jax.ShapeDtypeStruct((tp, d), F32),
           jax.ShapeDtypeStruct((d, tp), BF16),
           jax.ShapeDtypeStruct((8, 128), F32), jax.ShapeDtypeStruct((8, d), F32)],
        scratch_shapes=[pltpu.VMEM((3, TM, d), BF16),
                        pltpu.VMEM((HALO_A + TM, d), F32), pltpu.VMEM((HALO_B + TM, d), F32),
                        pltpu.VMEM((shl, d), F32), pltpu.VMEM((TM, d), F32),
                        pltpu.VMEM((RB, d), F32), pltpu.VMEM((8, d), F32)],
        compiler_params=_params(("arbitrary",)),
    )(s_pad, proj, target, w3, wa, wb, conv_a_b, ln_g, ln_b, b_a_out, final_g, norm_g)


def _mix_bwd(ds2, proj, ca, cb, ya, yb, w3, wa, wb, ln_g, ln_b):
    tp, d = ds2.shape
    nt = tp // TM
    nrb = TM // RB
    shl = TM + SHIFT_ROWS
    nt_dims = (((1,), (1,)), ((), ()))

    def body(ds2_ref, proj_ref, ca_ref, cb_ref, ya_ref, yb_ref, w3_ref, wa_ref, wb_ref, lng_ref, lnb_ref,
             dproj_ref, d3_ref, sm_ref, ext_d, ext_e, sh, dm_s, dpa_s, dpb_s, dua0_s, acc):
        step = pl.program_id(0)

        def split(k, rows):
            return proj_ref[rows, k * d:(k + 1) * d].astype(F32)

        def put(k, rows, val):
            dproj_ref[rows, k * d:(k + 1) * d] = val.astype(BF16)

        def accum(row, val):
            acc[row] += val.reshape(RB // 8, 8, d).sum(axis=0)

        @pl.when(step == 0)
        def _():
            ext_d[TM:TM + HALO_A, :] = jnp.zeros((HALO_A, d), F32)
            ext_e[TM:TM + HALO_B, :] = jnp.zeros((HALO_B, d), F32)
            acc[...] = jnp.zeros_like(acc)

        d3_ref[2] = ds2_ref[...].astype(BF16)
        dm_s[...] = lax.dot_general(d3_ref[2], w3_ref[2], nt_dims, preferred_element_type=F32)

        def gates(rb, carry):
            rows = _rows(rb)
            dm = dm_s[rows, :]
            sa = _sigmoid(split(7, rows))
            sb = _sigmoid(split(8, rows))
            ya_v = ya_ref[rows, :]
            yb_v = yb_ref[rows, :]
            put(7, rows, dm * ya_v * sa * (1.0 - sa))
            put(8, rows, dm * yb_v * sb * (1.0 - sb))
            dya = dm * sa
            accum(ROW_DBAO, dya)
            d3_ref[0, rows, :] = dya.astype(BF16)
            d3_ref[1, rows, :] = (dm * sb).astype(BF16)
            return carry
        lax.fori_loop(0, nrb, gates, 0)

        dpa_s[...] = lax.dot_general(d3_ref[0], w3_ref[0], nt_dims, preferred_element_type=F32)
        dpb_s[...] = lax.dot_general(d3_ref[1], w3_ref[1], nt_dims, preferred_element_type=F32)

        def branches(rb, carry):
            rows = _rows(rb)
            ca_v = ca_ref[rows, :]
            mu = jnp.mean(ca_v, axis=-1, keepdims=True)
            xc = ca_v - mu
            rstd = lax.rsqrt(jnp.mean(xc * xc, axis=-1, keepdims=True) + EPS)
            xhat = xc * rstd
            ln = xhat * lng_ref[...] + lnb_ref[...]
            sl = _sigmoid(ln)
            ua = ln * sl
            a_z = split(2, rows)
            sz = _sigmoid(a_z)
            dpa = dpa_s[rows, :]
            put(2, rows, dpa * ua * (sz * (1.0 + a_z * (1.0 - sz))))
            dln = dpa * (a_z * sz) * (sl * (1.0 + ln * (1.0 - sl)))
            accum(ROW_DLNG, dln * xhat)
            accum(ROW_DLNB, dln)
            dxh = dln * lng_ref[...]
            dca = rstd * (dxh - jnp.mean(dxh, axis=-1, keepdims=True)
                          - xhat * jnp.mean(dxh * xhat, axis=-1, keepdims=True))
            accum(ROW_DCAB, dca)
            ext_d[rows, :] = dca
            dua0_s[rows, :] = jnp.zeros((RB, d), F32)
            dm_s[rows, :] = split(0, rows) * _sigmoid(split(1, rows))
            b_z = split(6, rows)
            szb = _sigmoid(b_z)
            dpb = dpb_s[rows, :]
            b_b = split(3, rows)
            cb_v = cb_ref[rows, :]
            put(6, rows, dpb * (b_b * cb_v) * (szb * (1.0 + b_z * (1.0 - szb))))
            dub = dpb * (b_z * szb)
            put(3, rows, dub * cb_v)
            ext_e[rows, :] = dub * b_b
            return carry
        lax.fori_loop(0, nrb, branches, 0)

        for r, taps in _conv_a_taps(0, CONV_A - 1):
            if r == 0:
                src = ext_d
            else:
                sh[...] = ext_d[r:r + shl, :]
                src = sh

            def conv_t(rb, carry, src=src, taps=taps):
                rows = _rows(rb)
                ua0 = dm_s[rows, :]
                dua0 = dua0_s[rows, :]
                for q, lag in taps:
                    k = CONV_A - 1 - lag
                    slab = src[pl.ds(pl.multiple_of(rb * RB + 8 * q, 8), RB), :]
                    dua0 = dua0 + slab * wa_ref[k:k + 1, :]
                    accum(ROW_DWA + k, slab * ua0)
                dua0_s[rows, :] = dua0
                return carry
            lax.fori_loop(0, nrb, conv_t, 0)
        ext_d[TM:TM + HALO_A, :] = ext_d[0:HALO_A, :]

        dpb_s[...] = ext_e[0:TM, :] * wb_ref[CONV_B - 1:CONV_B, :]
        for lag in range(CONV_B):
            k = CONV_B - 1 - lag
            if lag > 0:
                sh[0:TM, :] = ext_e[lag:lag + TM, :]
                dpb_s[...] += sh[0:TM, :] * wb_ref[k:k + 1, :]
            src = ext_e if lag == 0 else sh

            def conv_b_w(rb, carry, src=src, k=k):
                rows = _rows(rb)
                accum(ROW_DWB + k, src[rows, :] * (split(4, rows) * split(5, rows)))
                return carry
            lax.fori_loop(0, nrb, conv_b_w, 0)
        ext_e[TM:TM + HALO_B, :] = ext_e[0:HALO_B, :]

        def inputs(rb, carry):
            rows = _rows(rb)
            dua0 = dua0_s[rows, :]
            a_val = split(0, rows)
            sg = _sigmoid(split(1, rows))
            put(0, rows, dua0 * sg)
            put(1, rows, dua0 * a_val * sg * (1.0 - sg))
            dcbin = dpb_s[rows, :]
            put(4, rows, dcbin * split(5, rows))
            put(5, rows, dcbin * split(4, rows))
            return carry
        lax.fori_loop(0, nrb, inputs, 0)

        @pl.when(step == nt - 1)
        def _():
            for row in range(SM_ROWS):
                sm_ref[row:row + 1, :] = jnp.sum(acc[row], axis=0, keepdims=True)

    rev = lambda i: (nt - 1 - i, 0)
    row_f32 = pl.BlockSpec((TM, d), rev)
    const = lambda shape: pl.BlockSpec(shape, lambda i: (0,) * len(shape))
    return pl.pallas_call(
        body, name="b1_mix", grid=(nt,),
        in_specs=[row_f32, pl.BlockSpec((TM, N_SPLIT * d), rev), row_f32, row_f32, row_f32, row_f32,
                  const((3, d, d)), const(wa.shape), const(wb.shape), const((1, d)), const((1, d))],
        out_specs=[pl.BlockSpec((TM, N_SPLIT * d), rev),
                   pl.BlockSpec((3, TM, d), lambda i: (0, nt - 1 - i, 0)),
                   const((SM_ROWS, d))],
        out_shape=[jax.ShapeDtypeStruct((tp, N_SPLIT * d), BF16), jax.ShapeDtypeStruct((3, tp, d), BF16),
                   jax.ShapeDtypeStruct((SM_ROWS, d), F32)],
        scratch_shapes=[pltpu.VMEM((TM + HALO_A, d), F32), pltpu.VMEM((TM + HALO_B, d), F32),
                        pltpu.VMEM((shl, d), F32), pltpu.VMEM((TM, d), F32), pltpu.VMEM((TM, d), F32),
                        pltpu.VMEM((TM, d), F32), pltpu.VMEM((TM, d), F32),
                        pltpu.VMEM((SM_ROWS, 8, d), F32)],
        compiler_params=_params(("arbitrary",)),
    )(ds2, proj, ca, cb, ya, yb, w3, wa, wb, ln_g, ln_b)


def kernel(x, meta_tokens, norm_g, w_in, conv_a_w, conv_a_b, ln_a_g, ln_a_b, w_a_out, b_a_out, conv_b_w, w_b_out, w_out, final_g, loss_target, m_meta_tokens, m_norm_g, m_w_in, m_conv_a_w, m_conv_a_b, m_ln_a_g, m_ln_a_b, m_w_a_out, m_b_a_out, m_conv_b_w, m_w_b_out, m_w_out, m_final_g, v_meta_tokens, v_norm_g, v_w_in, v_conv_a_w, v_conv_a_b, v_ln_a_g, v_ln_a_b, v_w_a_out, v_b_a_out, v_conv_b_w, v_w_b_out, v_w_out, v_final_g):
    seq, d = x.shape[1], x.shape[2]
    dc = meta_tokens.shape[1]
    sw = w_in.shape[2]
    rsh = w_a_out.shape[1]
    xi, yi, ci = _mesh_pos()
    me = 2 * xi + yi
    pos = jnp.stack([ci, me]).astype(jnp.int32)

    conv_rows = HALO_A + HALO_B + 8
    convs = jnp.concatenate([
        jnp.pad(conv_a_w[0], ((0, HALO_A - CONV_A), (0, 0))),
        jnp.pad(conv_b_w[0], ((0, HALO_B - CONV_B), (0, 0))), jnp.zeros((8, dc), F32)], axis=0)[None]
    w3_own = jnp.stack([w_a_out[0], w_b_out[0], w_out[0]])
    (metag,) = _all_gather([_place_own(meta_tokens[None], pos, F32, "place_meta")])
    meta_full = jnp.transpose(metag[0], (1, 0, 2)).reshape(N_META, N_CHIPS * dc)
    fg2 = final_g.reshape(1, d)

    first_tile = jnp.concatenate([jnp.zeros((TM - N_META, d), F32), meta_full], axis=0)
    s_pad = jnp.concatenate([first_tile, x[0]], axis=0)

    proj, (wg_in, wg3, convg) = _proj_fwd(s_pad, norm_g, [_place_own(w_in, pos, BF16, "place_in"),
                                                          _place_own(w3_own, pos, BF16, "place_sq"),
                                                          _place_own(convs, pos, F32, "place_conv")], pos)
    w3 = wg3.reshape(3, N_CHIPS * rsh, d)
    convg = jnp.transpose(convg[0], (1, 0, 2)).reshape(conv_rows, N_CHIPS * dc)
    wa_full = convg[0:HALO_A]
    wb_full = convg[HALO_A:HALO_A + HALO_B]
    ca, cb, ya, yb, abm_t, ds2, h_t, loss8, dfg8 = _mix_fwd(
        s_pad, proj, loss_target[0], w3, wa_full, wb_full, conv_a_b, ln_a_g, ln_a_b, b_a_out, fg2, norm_g)
    dproj, d3, sm = _mix_bwd(ds2, proj, ca, cb, ya, yb, w3, wa_full, wb_full, ln_a_g, ln_a_b)
    cw_sq = _col_block(d, 512)
    p32_sq, pbf_sq, _ = _dw_reduced(
        abm_t, d3, cw_sq, (0, d // cw_sq), N_CHIPS, (3, N_CHIPS, rsh // 2, d), (None, N_CHIPS, rsh // 2, cw_sq),
        lambda u: (u // (d // cw_sq), 0, 0, u % (d // cw_sq)), None, "dw_square")
    cw_in = _col_block(sw, 768)
    ncol = sw // cw_in
    p32_in, pbf_in, l_sq = _dw_reduced(
        h_t[None], dproj[None], cw_in, (0, N_CHIPS * ncol), 1, (1, N_CHIPS, d // 2, sw), (None, None, d // 2, cw_in),
        lambda u: (0, u // ncol, 0, u % ncol), (pbf_sq, None, pbf_sq.shape), "dw_in")
    ds, dng8, l_in = _dh_bwd(dproj, wg_in, s_pad, ds2, norm_g, pbf_in, lax.empty(pbf_in.shape, BF16), None)
    half_in = _sum_chips([p32_in], l_in, pos, "rs_sum_in")
    half_sq = _sum_chips([p32_sq], l_sq, pos, "rs_sum_sq")
    tail_row = lax.broadcasted_iota(jnp.int32, (8, d), 0)
    tail = jnp.where(tail_row == 0, dng8, jnp.where(tail_row == 1, dfg8,
                     jnp.where(tail_row == 2, loss8[0, 0], 0.0)))
    block = jnp.concatenate([sm, ds[TM - N_META:TM], tail], axis=0)
    (other_in, other_sq), red = _sibling_swap([half_in, half_sq], block)
    col = lax.dynamic_slice(red, (0, me * dc), (AR_ROWS, dc))
    g_small = {
        "meta_tokens": col[ROW_DMETA:ROW_DMETA + N_META],
        "norm_g": red[ROW_DNG:ROW_DNG + 1],
        "conv_a_w": col[ROW_DWA:ROW_DWA + CONV_A][None],
        "conv_a_b": red[ROW_DCAB:ROW_DCAB + 1],
        "ln_a_g": red[ROW_DLNG:ROW_DLNG + 1],
        "ln_a_b": red[ROW_DLNB:ROW_DLNB + 1],
        "b_a_out": red[ROW_DBAO:ROW_DBAO + 1],
        "conv_b_w": col[ROW_DWB:ROW_DWB + CONV_B][None],
        "final_g": red[ROW_DFG],
    }

    upd_in = _adam_halves([w_in], [m_w_in], [v_w_in], half_in, other_in, pos, "adam_in")
    upd_sq = _adam_halves([w_a_out, w_b_out, w_out], [m_w_a_out, m_w_b_out, m_w_out],
                          [v_w_a_out, v_w_b_out, v_w_out], half_sq, other_sq, pos, "adam_sq")
    small_w = {"meta_tokens": (meta_tokens, m_meta_tokens, v_meta_tokens), "norm_g": (norm_g, m_norm_g, v_norm_g),
               "conv_a_w": (conv_a_w, m_conv_a_w, v_conv_a_w), "conv_a_b": (conv_a_b, m_conv_a_b, v_conv_a_b),
               "ln_a_g": (ln_a_g, m_ln_a_g, v_ln_a_g), "ln_a_b": (ln_a_b, m_ln_a_b, v_ln_a_b),
               "b_a_out": (b_a_out, m_b_a_out, v_b_a_out), "conv_b_w": (conv_b_w, m_conv_b_w, v_conv_b_w),
               "final_g": (final_g, m_final_g, v_final_g)}
    names_small = list(small_w)
    as2d = lambda t: t.reshape(-1, t.shape[-1])
    upd_small = _adam_small([(as2d(small_w[k][0]), as2d(g_small[k]), as2d(small_w[k][1]), as2d(small_w[k][2]))
                             for k in names_small])

    grads, deltas, new_m, new_v = dict(g_small), {}, {}, {}
    for k, upd in zip(names_small, upd_small):
        deltas[k], new_m[k], new_v[k] = [t.reshape(small_w[k][0].shape) for t in upd]
    grads["w_in"], deltas["w_in"], new_m["w_in"], new_v["w_in"] = upd_in
    for idx, k in enumerate(["w_a_out", "w_b_out", "w_out"]):
        grads[k], deltas[k], new_m[k], new_v[k] = upd_sq[4 * idx:4 * idx + 4]

    loss = red[ROW_LOSS, 0]
    grad_x = ds[TM:][None]
    order = ["meta_tokens", "norm_g", "w_in", "conv_a_w", "conv_a_b", "ln_a_g", "ln_a_b", "w_a_out", "b_a_out",
             "conv_b_w", "w_b_out", "w_out", "final_g"]
    return (loss, grad_x, *[grads[k] for k in order], *[deltas[k] for k in order],
            *[new_m[k] for k in order], *[new_v[k] for k in order])
```

```python
import functools

import jax
import jax.numpy as jnp
from jax import lax
from jax.experimental import pallas as pl
from jax.experimental.pallas import tpu as pltpu

F32 = jnp.float32
BF16 = jnp.bfloat16
MESH = pl.DeviceIdType.MESH

EPS = 1e-6
N_META = 16
N_SPLIT = 9
CONV_A = 31
CONV_B = 3
HALO_A = 32
HALO_B = 8
SHIFT_ROWS = 24
TM = 256
RB = 64
N_ROW_TILES_BIG = 8
ROW_BLOCK = 256
N_CHIPS = 4
VMEM_LIMIT = 56 * 1024 * 1024
VMEM_LIMIT_BIG = 62 * 1024 * 1024

ADAM_LR = 0.001
ADAM_B1 = 0.9
ADAM_B2 = 0.999
ADAM_EPS = 1e-08
ADAM_WD = 0.01
ADAM_STEP = 10

ROW_DWA = 0
ROW_DWB = 32
ROW_DCAB = 40
ROW_DLNG = 41
ROW_DLNB = 42
ROW_DBAO = 43
SM_ROWS = 48
ROW_DMETA = 48
ROW_DNG = 64
ROW_DFG = 65
ROW_LOSS = 66
AR_ROWS = 72


def _sigmoid(v):
    return 0.5 * jnp.tanh(0.5 * v) + 0.5


def _params(sem, **kw):
    return pltpu.CompilerParams(dimension_semantics=sem, vmem_limit_bytes=VMEM_LIMIT, **kw)


def _rows(rb):
    return pl.ds(pl.multiple_of(rb * RB, RB), RB)


def _mesh_pos():
    x, y, c = lax.axis_index("x"), lax.axis_index("y"), lax.axis_index("c")
    return x, y, c


def _half(ref, j, c):
    h = ref.shape[2] // 2
    return ref.at[:, j, pl.ds(c * h, h), :]


def _place_own(shard, pos, dtype, name):
    s, r, c = shard.shape
    rb = ROW_BLOCK if r % ROW_BLOCK == 0 else r

    def body(pos_ref, x_ref, o_ref):
        o_ref[...] = x_ref[...].astype(dtype)

    return pl.pallas_call(
        body, name=name,
        grid_spec=pltpu.PrefetchScalarGridSpec(
            num_scalar_prefetch=1, grid=(s, r // rb),
            in_specs=[pl.BlockSpec((None, rb, c), lambda si, b, pos_ref: (si, b, 0))],
            out_specs=pl.BlockSpec((None, None, rb, c), lambda si, b, pos_ref: (si, pos_ref[1], b, 0))),
        out_shape=jax.ShapeDtypeStruct((s, N_CHIPS, r, c), dtype),
        compiler_params=_params(("arbitrary",) * 2),
    )(pos, shard)


def _all_gather(bufs):
    n = len(bufs)

    def body(*refs):
        outs = refs[n:2 * n]
        send_sems, recv_sems = refs[2 * n:]
        x, y, c = _mesh_pos()
        me = 2 * x + y
        sibling = (x, y, 1 - c)
        chips = [(1 - x, y), (x, 1 - y), (1 - x, 1 - y)]

        def remote(a, k, piece_src, piece_dst, to):
            return pltpu.make_async_remote_copy(
                src_ref=piece_src, dst_ref=piece_dst, send_sem=send_sems.at[6 * a + k],
                recv_sem=recv_sems.at[6 * a + k], device_id=to, device_id_type=MESH)

        sends = []
        for a in range(n):
            mine = _half(outs[a], me, c)
            for k, (px, py) in enumerate(chips):
                sends.append(remote(a, k, mine, mine, (px, py, c)))
        for cp in sends:
            cp.start()
        for a in range(n):
            for k, (px, py) in enumerate(chips):
                piece = _half(outs[a], 2 * px + py, c)
                remote(a, k, piece, piece, (px, py, c)).wait_recv()
                fwd = remote(a, 3 + k, piece, piece, sibling)
                fwd.start()
                sends.append(fwd)
        for a in range(n):
            for k, (px, py) in enumerate(chips):
                piece = _half(outs[a], 2 * px + py, 1 - c)
                remote(a, 3 + k, piece, piece, sibling).wait_recv()
        for cp in sends:
            cp.wait_send()

    any_spec = pl.BlockSpec(memory_space=pl.ANY)
    return pl.pallas_call(
        body, name="ag_weights",
        in_specs=[any_spec] * n, out_specs=[any_spec] * n,
        out_shape=[jax.ShapeDtypeStruct(b.shape, b.dtype) for b in bufs],
        input_output_aliases={a: a for a in range(n)},
        scratch_shapes=[pltpu.SemaphoreType.DMA((6 * n,)), pltpu.SemaphoreType.DMA((6 * n,))],
    )(*bufs)


class _Exchange:
    def __init__(self, sends, recvs):
        self.sends, self.recvs = sends, recvs

    @staticmethod
    def _each(pairs, act):
        for cond, cp in pairs:
            if cond is None:
                act(cp)
            else:
                pl.when(cond)(functools.partial(act, cp))

    def start(self):
        self._each(self.sends, lambda cp: cp.start())

    def finish(self):
        self._each(self.recvs, lambda cp: cp.wait_recv())
        self._each(self.sends, lambda cp: cp.wait_send())


def _chip_exchange(part_ref, land_ref, send_sems, recv_sems, half=None):
    x, y, c = _mesh_pos()
    me = 2 * x + y
    sends, recvs = [], []
    for k, (px, py) in enumerate([(1 - x, y), (x, 1 - y), (1 - x, 1 - y)]):
        sems = dict(send_sem=send_sems.at[k], recv_sem=recv_sems.at[k], device_id=(px, py, c), device_id_type=MESH)
        slot = 2 * px + py if half is None else py
        sends.append((None if half is None else px == half, pltpu.make_async_remote_copy(
            src_ref=part_ref.at[:, slot], dst_ref=land_ref.at[:, me], **sems)))
        landed = land_ref.at[:, 2 * px + py]
        recvs.append((None if half is None else x == half,
                      pltpu.make_async_remote_copy(src_ref=landed, dst_ref=landed, **sems)))
    return _Exchange(sends, recvs)

def _sibling_swap(halves, small):
    n = len(halves)

    def body(*refs):
        ins, small_ref, outs, red_ref = refs[:n], refs[n], refs[n + 1:2 * n + 1], refs[2 * n + 1]
        send_sems, recv_sems = refs[2 * n + 2:2 * n + 4]
        reduce = _SmallAllReduce(small_ref, red_ref, *refs[2 * n + 4:])
        x, y, c = _mesh_pos()
        copies = [pltpu.make_async_remote_copy(
            src_ref=ins[a], dst_ref=outs[a], send_sem=send_sems.at[a], recv_sem=recv_sems.at[a],
            device_id=(x, y, 1 - c), device_id_type=MESH) for a in range(n)]
        reduce.start()
        for cp in copies:
            cp.start()
        reduce.between_chips()
        reduce.finish()
        for cp in copies:
            cp.wait()

    any_spec = pl.BlockSpec(memory_space=pl.ANY)
    vm = pl.BlockSpec(memory_space=pltpu.VMEM)
    outs = pl.pallas_call(
        body, name="rs_swap",
        in_specs=[any_spec] * n + [vm], out_specs=[any_spec] * n + [vm],
        out_shape=[jax.ShapeDtypeStruct(h.shape, h.dtype) for h in halves]
        + [jax.ShapeDtypeStruct(small.shape, F32)],
        scratch_shapes=[pltpu.SemaphoreType.DMA((n,)), pltpu.SemaphoreType.DMA((n,))]
        + _SmallAllReduce.scratch(*small.shape),
    )(*halves, small)
    return outs[:n], outs[n]


class _SmallAllReduce:
    def __init__(self, x_ref, out_ref, sib_ref, part_ref, peers_ref, send_sems, recv_sems):
        self.x_ref, self.out_ref, self.sib_ref, self.part_ref, self.peers_ref = x_ref, out_ref, sib_ref, part_ref, peers_ref
        x, y, c = _mesh_pos()
        self.me = 2 * x + y
        self.swap = pltpu.make_async_remote_copy(
            src_ref=x_ref, dst_ref=sib_ref, send_sem=send_sems.at[0], recv_sem=recv_sems.at[0],
            device_id=(x, y, 1 - c), device_id_type=MESH)
        self.sends, self.recvs = [], []
        for k, (px, py) in enumerate([(1 - x, y), (x, 1 - y), (1 - x, 1 - y)]):
            sems = dict(send_sem=send_sems.at[1 + k], recv_sem=recv_sems.at[1 + k],
                        device_id=(px, py, c), device_id_type=MESH)
            self.sends.append(pltpu.make_async_remote_copy(src_ref=part_ref, dst_ref=peers_ref.at[self.me], **sems))
            landed = peers_ref.at[2 * px + py]
            self.recvs.append(pltpu.make_async_remote_copy(src_ref=landed, dst_ref=landed, **sems))

    @staticmethod
    def scratch(rows, d):
        return [pltpu.VMEM((rows, d), F32), pltpu.VMEM((rows, d), F32), pltpu.VMEM((N_CHIPS, rows, d), F32),
                pltpu.SemaphoreType.DMA((4,)), pltpu.SemaphoreType.DMA((4,))]

    def start(self):
        self.swap.start()

    def between_chips(self):
        self.swap.wait()
        self.part_ref[...] = self.x_ref[...] + self.sib_ref[...]
        self.peers_ref[self.me] = self.part_ref[...]
        for cp in self.sends:
            cp.start()

    def finish(self):
        for cp in self.recvs:
            cp.wait_recv()
        for cp in self.sends:
            cp.wait_send()
        p = self.peers_ref
        self.out_ref[...] = ((p[0] + p[1]) + p[2]) + p[3]


def _sum_chips(p32s, landed, pos, name):
    s, _, h, c = p32s[0].shape
    hb = min(h, ROW_BLOCK)
    per = N_CHIPS // len(p32s)

    def body(pos_ref, *refs):
        own_refs, (l1_ref, l2_ref, l3_ref, out_ref) = refs[:len(p32s)], refs[len(p32s):]
        own = own_refs[0][...]
        for k in range(1, len(p32s)):
            own = jnp.where(pos_ref[1] // per == k, own_refs[k][...], own)
        out_ref[...] = ((own + l1_ref[...].astype(F32)) + l2_ref[...].astype(F32)) + l3_ref[...].astype(F32)

    own_spec = pl.BlockSpec((None, None, hb, c), lambda si, b, pos_ref: (si, pos_ref[1] % per, b, 0))

    def slot(k):
        return pl.BlockSpec((None, None, hb, c), lambda si, b, pos_ref: (si, (pos_ref[1] + k) % N_CHIPS, b, 0))

    return pl.pallas_call(
        body, name=name,
        grid_spec=pltpu.PrefetchScalarGridSpec(
            num_scalar_prefetch=1, grid=(s, h // hb),
            in_specs=[own_spec] * len(p32s) + [slot(1), slot(2), slot(3)],
            out_specs=pl.BlockSpec((None, hb, c), lambda si, b, pos_ref: (si, b, 0))),
        out_shape=jax.ShapeDtypeStruct((s, h, c), F32),
        compiler_params=_params(("arbitrary",) * 2),
    )(pos, *p32s, landed, landed, landed)


def _adamw(w, g, m, v):
    m = ADAM_B1 * m + (1.0 - ADAM_B1) * g
    v = ADAM_B2 * v + (1.0 - ADAM_B2) * (g * g)
    m_hat = m / (1.0 - ADAM_B1 ** ADAM_STEP)
    v_hat = v / (1.0 - ADAM_B2 ** ADAM_STEP)
    delta = -ADAM_LR * (m_hat / (jnp.sqrt(v_hat) + ADAM_EPS) + ADAM_WD * w)
    return delta, m, v


def _adam_halves(ws, ms, vs, g_own, g_recv, pos, name):
    n = len(ws)
    _, r, c = ws[0].shape
    h = r // 2
    rb = min(h, ROW_BLOCK)
    nb = h // rb

    def body(pos_ref, *refs):
        w_refs, m_refs, v_refs = refs[:n], refs[n:2 * n], refs[2 * n:3 * n]
        go_ref, gr_ref = refs[3 * n:3 * n + 2]
        outs = refs[3 * n + 2:]
        mine = pl.program_id(0) == pos_ref[0]
        for a in range(n):
            g = jnp.where(mine, go_ref[a], gr_ref[a])
            delta, m, v = _adamw(w_refs[a][...], g, m_refs[a][...], v_refs[a][...])
            outs[4 * a][...], outs[4 * a + 1][...], outs[4 * a + 2][...], outs[4 * a + 3][...] = g, delta, m, v

    spec_w = pl.BlockSpec((None, rb, c), lambda hf, b, pos_ref: (0, hf * nb + b, 0))
    spec_g = pl.BlockSpec((n, rb, c), lambda hf, b, pos_ref: (0, b, 0))
    return pl.pallas_call(
        body, name=name,
        grid_spec=pltpu.PrefetchScalarGridSpec(
            num_scalar_prefetch=1, grid=(2, nb), in_specs=[spec_w] * (3 * n) + [spec_g] * 2,
            out_specs=[spec_w] * (4 * n)),
        out_shape=[jax.ShapeDtypeStruct((1, r, c), F32)] * (4 * n),
        compiler_params=_params(("arbitrary",) * 2),
    )(pos, *ws, *ms, *vs, g_own, g_recv)


def _adam_small(items):
    n = len(items)

    def body(*refs):
        ins, outs = refs[:4 * n], refs[4 * n:]
        for a in range(n):
            w_ref, g_ref, m_ref, v_ref = ins[4 * a:4 * a + 4]
            d, m, v = _adamw(w_ref[...], g_ref[...], m_ref[...], v_ref[...])
            outs[3 * a][...] = d
            outs[3 * a + 1][...] = m
            outs[3 * a + 2][...] = v

    vm = pl.BlockSpec(memory_space=pltpu.VMEM)
    flat = [t for it in items for t in it]
    outs = pl.pallas_call(
        body, name="adam_small", in_specs=[vm] * (4 * n), out_specs=[vm] * (3 * n),
        out_shape=[jax.ShapeDtypeStruct(it[0].shape, F32) for it in items for _ in range(3)],
    )(*flat)
    return [tuple(outs[3 * a:3 * a + 3]) for a in range(n)]


def _shard_of_step(js, me):
    flip = jnp.where(js == 1, 2, jnp.where(js == 2, 1, jnp.where(js == 3, 3, 0)))
    return lax.bitwise_xor(me, flip)


def _proj_fwd(s_pad, norm_g, bufs, pos):
    tp, d = s_pad.shape
    _, nsh, _, sw = bufs[0].shape
    tmb = tp // N_ROW_TILES_BIG
    n = len(bufs)

    def body(pos_ref, s_ref, g_ref, *refs):
        proj_ref = refs[n]
        gbufs = refs[n + 1:2 * n + 1]
        wbuf, wsems, send_sems, recv_sems = refs[2 * n + 1:]
        x, y, c = _mesh_pos()
        me = 2 * x + y
        sibling = (x, y, 1 - c)
        chips = [(1 - x, y), (x, 1 - y), (1 - x, 1 - y)]
        js, i = pl.program_id(0), pl.program_id(1)

        def remote(a, k, piece, to):
            return pltpu.make_async_remote_copy(
                src_ref=piece, dst_ref=piece, send_sem=send_sems.at[6 * a + k],
                recv_sem=recv_sems.at[6 * a + k], device_id=to, device_id_type=MESH)

        def fetch(chip, step):
            return pltpu.make_async_copy(gbufs[0].at[0, chip], wbuf.at[step % 2], wsems.at[step % 2])

        chip_ids = [2 * px + py for px, py in chips]
        relayed_chip = jnp.where(c == 0, chip_ids[0], chip_ids[1])
        relay_to = (jnp.where(c == 0, x, 1 - x), jnp.where(c == 0, 1 - y, y), c)

        def own_piece(a, k):
            return remote(a, k, _half(gbufs[a], me, c), (*chips[k], c))

        def relay(a):
            return remote(a, 2, _half(gbufs[a], relayed_chip, c), relay_to)

        def to_sibling(a, k, core):
            return remote(a, 3 + k, _half(gbufs[a], chip_ids[k], core), sibling)

        def landed(a, k):
            return remote(a, k, _half(gbufs[a], chip_ids[k], c), (*chips[k], c))

        def take_neighbours(a):
            landed(a, 0).wait_recv()
            landed(a, 1).wait_recv()
            relay(a).start()
            for k in range(2):
                to_sibling(a, k, c).start()
            for k in range(2):
                to_sibling(a, k, 1 - c).wait_recv()

        def take_diagonal(a):
            landed(a, 2).wait_recv()
            to_sibling(a, 2, c).start()
            to_sibling(a, 2, 1 - c).wait_recv()

        @pl.when((js == 0) & (i == 0))
        def _():
            for a in range(n):
                for k in range(2):
                    own_piece(a, k).start()
            fetch(me, 0).start()
            fetch(me, 0).wait()

        @pl.when((js == 1) & (i == 0))
        def _():
            take_neighbours(0)
            fetch(chip_ids[0], 1).start()
            fetch(chip_ids[0], 1).wait()

        @pl.when((js == 1) & (i == N_ROW_TILES_BIG - 2))
        def _():
            fetch(chip_ids[1], 2).start()

        @pl.when((js == 2) & (i == 0))
        def _():
            fetch(chip_ids[1], 2).wait()

        @pl.when((js == 2) & (i == N_ROW_TILES_BIG - 2))
        def _():
            take_diagonal(0)
            fetch(chip_ids[2], 3).start()

        @pl.when((js == 3) & (i == 0))
        def _():
            fetch(chip_ids[2], 3).wait()

        s = s_ref[...]
        r = lax.rsqrt(jnp.mean(s * s, axis=-1, keepdims=True) + EPS)
        h = (s * r * g_ref[...]).astype(BF16)
        proj_ref[...] = jnp.dot(h, wbuf[js % 2], preferred_element_type=F32).astype(BF16)

        @pl.when((js == nsh - 1) & (i == N_ROW_TILES_BIG - 1))
        def _():
            for a in range(1, n):
                take_neighbours(a)
            for a in range(1, n):
                take_diagonal(a)
            for a in range(n):
                for k in range(2):
                    own_piece(a, k).wait_send()
                relay(a).wait_send()
                for k in range(3):
                    to_sibling(a, k, c).wait_send()

    any_spec = pl.BlockSpec(memory_space=pl.ANY)
    outs = pl.pallas_call(
        body, name="f1_proj",
        grid_spec=pltpu.PrefetchScalarGridSpec(
            num_scalar_prefetch=1, grid=(nsh, N_ROW_TILES_BIG),
            in_specs=[pl.BlockSpec((tmb, d), lambda js, i, pos_ref: (i, 0)),
                      pl.BlockSpec((1, d), lambda js, i, pos_ref: (0, 0))] + [any_spec] * n,
            out_specs=[pl.BlockSpec((tmb, sw), lambda js, i, pos_ref: (i, _shard_of_step(js, pos_ref[1])))]
            + [any_spec] * n,
            scratch_shapes=[pltpu.VMEM((2, d, sw), BF16), pltpu.SemaphoreType.DMA((2,)),
                            pltpu.SemaphoreType.DMA((6 * n,)), pltpu.SemaphoreType.DMA((6 * n,))]),
        out_shape=[jax.ShapeDtypeStruct((tp, nsh * sw), BF16)]
        + [jax.ShapeDtypeStruct(b.shape, b.dtype) for b in bufs],
        input_output_aliases={3 + a: 1 + a for a in range(n)},
        compiler_params=_params(("arbitrary", "arbitrary")),
    )(pos, s_pad, norm_g, *bufs)
    return outs[0], outs[1:]


def _dh_bwd(dproj, wg_in, s_pad, ds2, norm_g, part, land, half):
    tp, d = s_pad.shape
    _, nsh, _, sw = wg_in.shape
    tmb = tp // N_ROW_TILES_BIG

    def body(dp_ref, w_hbm, s_ref, ds2_ref, g_ref, part_ref, _, ds_ref, dng_ref, land_ref, wbuf, gacc,
             wsem, send_sems, recv_sems):
        exchange = _chip_exchange(part_ref, land_ref, send_sems, recv_sems, half)
        i = pl.program_id(0)

        @pl.when(i == 0)
        def _():
            exchange.start()
            gacc[...] = jnp.zeros_like(gacc)
            whole = pltpu.make_async_copy(w_hbm.at[0], wbuf, wsem)
            whole.start()
            whole.wait()

        dh = None
        for j in range(nsh):
            part = lax.dot_general(dp_ref[:, j * sw:(j + 1) * sw], wbuf[j], (((1,), (1,)), ((), ())),
                                   preferred_element_type=F32)
            dh = part if dh is None else dh + part
        s = s_ref[...]
        r = lax.rsqrt(jnp.mean(s * s, axis=-1, keepdims=True) + EPS)
        gacc[...] += (dh * s * r).reshape(tmb // 8, 8, d).sum(axis=0)
        t = dh * g_ref[...]
        ds_ref[...] = ds2_ref[...] + r * t - s * (r * r * r) * jnp.mean(t * s, axis=-1, keepdims=True)

        @pl.when(i == N_ROW_TILES_BIG - 1)
        def _():
            dng_ref[...] = jnp.broadcast_to(jnp.sum(gacc[...], axis=0, keepdims=True), (8, d))
            exchange.finish()

    any_spec = pl.BlockSpec(memory_space=pl.ANY)
    return pl.pallas_call(
        body, name="b2_dh", grid=(N_ROW_TILES_BIG,),
        in_specs=[pl.BlockSpec((tmb, nsh * sw), lambda i: (i, 0)), any_spec,
                  pl.BlockSpec((tmb, d), lambda i: (i, 0)),
                  pl.BlockSpec((tmb, d), lambda i: (i, 0)),
                  pl.BlockSpec((1, d), lambda i: (0, 0)), any_spec, any_spec],
        out_specs=[pl.BlockSpec((tmb, d), lambda i: (i, 0)),
                   pl.BlockSpec((8, d), lambda i: (0, 0)), any_spec],
        out_shape=[jax.ShapeDtypeStruct((tp, d), F32), jax.ShapeDtypeStruct((8, d), F32),
                   jax.ShapeDtypeStruct(land.shape, land.dtype)],
        input_output_aliases={6: 2},
        scratch_shapes=[pltpu.VMEM((nsh, d, sw), BF16), pltpu.VMEM((8, d), F32), pltpu.SemaphoreType.DMA,
                        pltpu.SemaphoreType.DMA((3,)), pltpu.SemaphoreType.DMA((3,))],
        compiler_params=pltpu.CompilerParams(dimension_semantics=("arbitrary",),
                                             vmem_limit_bytes=VMEM_LIMIT_BIG),
    )(dproj, wg_in, s_pad, ds2, norm_g, part, land)


def _col_block(width, cap):
    return max(b for b in range(128, cap + 1, 128) if width % b == 0)


def _dw_reduced(lhs_t, rhs, cw, cols, groups, out_dims, out_block, out_index, carried, name):
    na, d, tp = lhs_t.shape
    col0, per_a = cols
    nblk = na * per_a
    rg = d // groups
    hh = rg // 2

    def body(*refs):
        if carried is None:
            l_ref, r_ref, p32_ref, pbf_ref, res, rbuf, send_sems, recv_sems = refs
            exchange = _Exchange([], [])
        else:
            (l_ref, r_ref, part_ref, p32_ref, pbf_ref, land_ref, res, rbuf, send_sems, recv_sems,
             xsend, xrecv) = refs
            exchange = _chip_exchange(part_ref, land_ref, xsend, xrecv, carried[1])
        x, y, c = _mesh_pos()
        t = pl.program_id(0)
        u = jnp.maximum(t - 1, 0)

        def to_sibling(blk):
            return pltpu.make_async_remote_copy(
                src_ref=res.at[blk % 2, :, pl.ds((1 - c) * hh, hh), :], dst_ref=rbuf.at[blk % 2],
                send_sem=send_sems.at[blk], recv_sem=recv_sems.at[blk],
                device_id=(x, y, 1 - c), device_id_type=MESH)

        @pl.when(t == 0)
        def _():
            exchange.start()

        @pl.when(t < nblk)
        def _():
            res[t % 2] = jnp.dot(l_ref[...], r_ref[...], preferred_element_type=F32).reshape(groups, rg, cw)

        @pl.when(t >= 1)
        def _():
            to_sibling(u).wait_recv()
            p = res[u % 2, :, pl.ds(c * hh, hh), :] + rbuf[u % 2]
            p32_ref[...] = p.reshape(p32_ref.shape)
            pbf_ref[...] = p.reshape(pbf_ref.shape).astype(BF16)

        @pl.when(t < nblk)
        def _():
            to_sibling(t).start()

        @pl.when(t >= 1)
        def _():
            to_sibling(u).wait_send()

        @pl.when(t == nblk)
        def _():
            exchange.finish()

    any_spec = pl.BlockSpec(memory_space=pl.ANY)
    last = nblk - 1
    out_spec = pl.BlockSpec(out_block, lambda t: out_index(jnp.maximum(t - 1, 0)))
    extra = [] if carried is None else [carried[0]]
    outs = pl.pallas_call(
        body, name=name, grid=(nblk + 1,),
        in_specs=[pl.BlockSpec((None, d, tp), lambda t: (jnp.minimum(t, last) // per_a, 0, 0)),
                  pl.BlockSpec((None, tp, cw), lambda t: (jnp.minimum(t, last) // per_a, 0,
                                                          col0 + jnp.minimum(t, last) % per_a))]
        + [any_spec] * len(extra),
        out_specs=[out_spec, out_spec] + [any_spec] * len(extra),
        out_shape=[jax.ShapeDtypeStruct(out_dims, F32), jax.ShapeDtypeStruct(out_dims, BF16)]
        + [jax.ShapeDtypeStruct(carried[2], e.dtype) for e in extra],
        scratch_shapes=[pltpu.VMEM((2, groups, rg, cw), F32), pltpu.VMEM((2, groups, hh, cw), F32),
                        pltpu.SemaphoreType.DMA((nblk,)), pltpu.SemaphoreType.DMA((nblk,))]
        + [pltpu.SemaphoreType.DMA((3,)), pltpu.SemaphoreType.DMA((3,))] * len(extra),
        compiler_params=_params(("arbitrary",)),
    )(lhs_t, rhs, *extra)
    return outs[0], outs[1], (outs[2] if extra else None)


def _conv_a_taps(first_lag, last_lag):
    out = []
    for r in range(8):
        taps = [(q, 8 * q + r) for q in range(5) if first_lag <= 8 * q + r <= last_lag]
        if taps:
            out.append((r, taps))
    return out


def _mix_fwd(s_pad, proj, target, w3, wa, wb, conv_a_b, ln_g, ln_b, b_a_out, final_g, norm_g):
    tp, d = s_pad.shape
    nt = tp // TM
    nrb = TM // RB
    shl = TM + SHIFT_ROWS

    def body(s_ref, proj_ref, tgt_ref, w3_ref, wa_ref, wb_ref, cab_ref, lng_ref, lnb_ref, bao_ref, fg_ref, ng_ref,
             ca_ref, cb_ref, ya_ref, yb_ref, abmt_ref, ds2_ref, ht_ref, loss_ref, dfg_ref,
             abm_ref, ext_a, ext_b, sh, s2_s, lacc, gacc):
        i = pl.program_id(0)

        def split(k, rows):
            return proj_ref[rows, k * d:(k + 1) * d].astype(F32)

        s_in = s_ref[...]
        h = s_in * lax.rsqrt(jnp.mean(s_in * s_in, axis=-1, keepdims=True) + EPS) * ng_ref[...]
        ht_ref[...] = h.T.astype(BF16)

        @pl.when(i == 0)
        def _():
            ext_a[0:HALO_A, :] = jnp.zeros((HALO_A, d), F32)
            ext_b[0:HALO_B, :] = jnp.zeros((HALO_B, d), F32)
            lacc[...] = jnp.zeros_like(lacc)
            gacc[...] = jnp.zeros_like(gacc)

        def conv_in(rb, carry):
            rows = _rows(rb)
            ua0 = split(0, rows) * _sigmoid(split(1, rows))
            ext_a[pl.ds(pl.multiple_of(HALO_A + rb * RB, 8), RB), :] = ua0
            ext_b[pl.ds(pl.multiple_of(HALO_B + rb * RB, 8), RB), :] = split(4, rows) * split(5, rows)
            ca_ref[rows, :] = jnp.broadcast_to(cab_ref[...], (RB, d))
            return carry
        lax.fori_loop(0, nrb, conv_in, 0)

        for r, taps in _conv_a_taps(HALO_A - CONV_A + 1, HALO_A):
            if r == 0:
                src = ext_a
            else:
                sh[...] = ext_a[r:r + shl, :]
                src = sh

            def conv_acc(rb, carry, src=src, taps=taps):
                rows = _rows(rb)
                acc = ca_ref[rows, :]
                for q, lag in taps:
                    k = lag - (HALO_A - CONV_A + 1)
                    acc = acc + src[pl.ds(pl.multiple_of(rb * RB + 8 * q, 8), RB), :] * wa_ref[k:k + 1, :]
                ca_ref[rows, :] = acc
                return carry
            lax.fori_loop(0, nrb, conv_acc, 0)
        ext_a[0:HALO_A, :] = ext_a[TM:TM + HALO_A, :]

        cb_ref[...] = ext_b[HALO_B:HALO_B + TM, :] * wb_ref[2:3, :]
        for k in range(CONV_B - 1):
            off = HALO_B - CONV_B + 1 + k
            sh[0:TM, :] = ext_b[off:off + TM, :]
            cb_ref[...] += sh[0:TM, :] * wb_ref[k:k + 1, :]
        ext_b[0:HALO_B, :] = ext_b[TM:TM + HALO_B, :]

        def branches(rb, carry):
            rows = _rows(rb)
            ca = ca_ref[rows, :]
            mu = jnp.mean(ca, axis=-1, keepdims=True)
            xc = ca - mu
            rstd = lax.rsqrt(jnp.mean(xc * xc, axis=-1, keepdims=True) + EPS)
            ln = xc * rstd * lng_ref[...] + lnb_ref[...]
            ua = ln * _sigmoid(ln)
            a_z = split(2, rows)
            abm_ref[0, rows, :] = (ua * (a_z * _sigmoid(a_z))).astype(BF16)
            b_z = split(6, rows)
            ub = split(3, rows) * cb_ref[rows, :]
            abm_ref[1, rows, :] = (ub * (b_z * _sigmoid(b_z))).astype(BF16)
            return carry
        lax.fori_loop(0, nrb, branches, 0)

        ya_ref[...] = jnp.dot(abm_ref[0], w3_ref[0], preferred_element_type=F32) + bao_ref[...]
        yb_ref[...] = jnp.dot(abm_ref[1], w3_ref[1], preferred_element_type=F32)

        def merge(rb, carry):
            rows = _rows(rb)
            m = _sigmoid(split(7, rows)) * ya_ref[rows, :] + _sigmoid(split(8, rows)) * yb_ref[rows, :]
            abm_ref[2, rows, :] = m.astype(BF16)
            return carry
        lax.fori_loop(0, nrb, merge, 0)

        s2_s[...] = s_ref[...] + jnp.dot(abm_ref[2], w3_ref[2], preferred_element_type=F32)
        for k in range(3):
            abmt_ref[k] = abm_ref[k].astype(F32).T.astype(BF16)
        live = (i > 0).astype(F32)

        def head(rb, carry):
            rows = _rows(rb)
            s2 = s2_s[rows, :]
            r2 = lax.rsqrt(jnp.mean(s2 * s2, axis=-1, keepdims=True) + EPS)
            diff = (s2 * r2 * fg_ref[...] - tgt_ref[rows, :]) * live
            lacc[...] += diff * diff
            dy = diff * (1.0 / d)
            gacc[...] += (dy * s2 * r2).reshape(RB // 8, 8, d).sum(axis=0)
            t = dy * fg_ref[...]
            ds2_ref[rows, :] = r2 * t - s2 * (r2 * r2 * r2) * jnp.mean(t * s2, axis=-1, keepdims=True)
            return carry
        lax.fori_loop(0, nrb, head, 0)

        @pl.when(i == nt - 1)
        def _():
            loss_ref[...] = jnp.broadcast_to(0.5 * jnp.sum(lacc[...]) * (1.0 / d), (8, 128))
            dfg_ref[...] = jnp.broadcast_to(jnp.sum(gacc[...], axis=0, keepdims=True), (8, d))

    row_f32 = pl.BlockSpec((TM, d), lambda i: (i, 0))
    const = lambda shape: pl.BlockSpec(shape, lambda i: (0,) * len(shape))
    return pl.pallas_call(
        body, name="f2_mix", grid=(nt,),
        in_specs=[row_f32,
                  pl.BlockSpec((TM, N_SPLIT * d), lambda i: (i, 0)),
                  pl.BlockSpec((TM, d), lambda i: (jnp.maximum(i - 1, 0), 0)),
                  const((3, d, d)), const(wa.shape), const(wb.shape)] + [const((1, d))] * 6,
        out_specs=[row_f32, row_f32, row_f32, row_f32,
                   pl.BlockSpec((3, d, TM), lambda i: (0, 0, i)),
                   row_f32, pl.BlockSpec((d, TM), lambda i: (0, i)), const((8, 128)), const((8, d))],
        out_shape=[jax.ShapeDtypeStruct((tp, d), F32)] * 4
        + [jax.ShapeDtypeStruct((3, d, tp), BF16), jax.ShapeDtypeStruct((tp, d), F32),
           jax.ShapeDtypeStruct((d, tp), BF16),
           jax.ShapeDtypeStruct((8, 128), F32), jax.ShapeDtypeStruct((8, d), F32)],
        scratch_shapes=[pltpu.VMEM((3, TM, d), BF16),
                        pltpu.VMEM((HALO_A + TM, d), F32), pltpu.VMEM((HALO_B + TM, d), F32),
                        pltpu.VMEM((shl, d), F32), pltpu.VMEM((TM, d), F32),
                        pltpu.VMEM((RB, d), F32), pltpu.VMEM((8, d), F32)],
        compiler_params=_params(("arbitrary",)),
    )(s_pad, proj, target, w3, wa, wb, conv_a_b, ln_g, ln_b, b_a_out, final_g, norm_g)


def _mix_bwd(ds2, proj, ca, cb, ya, yb, w3, wa, wb, ln_g, ln_b):
    tp, d = ds2.shape
    nt = tp // TM
    nrb = TM // RB
    shl = TM + SHIFT_ROWS
    nt_dims = (((1,), (1,)), ((), ()))

    def body(ds2_ref, proj_ref, ca_ref, cb_ref, ya_ref, yb_ref, w3_ref, wa_ref, wb_ref, lng_ref, lnb_ref,
             dproj_ref, d3_ref, sm_ref, ext_d, ext_e, sh, dm_s, dpa_s, dpb_s, dua0_s, acc):
        step = pl.program_id(0)

        def split(k, rows):
            return proj_ref[rows, k * d:(k + 1) * d].astype(F32)

        def put(k, rows, val):
            dproj_ref[rows, k * d:(k + 1) * d] = val.astype(BF16)

        def accum(row, val):
            acc[row] += val.reshape(RB // 8, 8, d).sum(axis=0)

        @pl.when(step == 0)
        def _():
            ext_d[TM:TM + HALO_A, :] = jnp.zeros((HALO_A, d), F32)
            ext_e[TM:TM + HALO_B, :] = jnp.zeros((HALO_B, d), F32)
            acc[...] = jnp.zeros_like(acc)

        d3_ref[2] = ds2_ref[...].astype(BF16)
        dm_s[...] = lax.dot_general(d3_ref[2], w3_ref[2], nt_dims, preferred_element_type=F32)

        def gates(rb, carry):
            rows = _rows(rb)
            dm = dm_s[rows, :]
            sa = _sigmoid(split(7, rows))
            sb = _sigmoid(split(8, rows))
            ya_v = ya_ref[rows, :]
            yb_v = yb_ref[rows, :]
            put(7, rows, dm * ya_v * sa * (1.0 - sa))
            put(8, rows, dm * yb_v * sb * (1.0 - sb))
            dya = dm * sa
            accum(ROW_DBAO, dya)
            d3_ref[0, rows, :] = dya.astype(BF16)
            d3_ref[1, rows, :] = (dm * sb).astype(BF16)
            return carry
        lax.fori_loop(0, nrb, gates, 0)

        dpa_s[...] = lax.dot_general(d3_ref[0], w3_ref[0], nt_dims, preferred_element_type=F32)
        dpb_s[...] = lax.dot_general(d3_ref[1], w3_ref[1], nt_dims, preferred_element_type=F32)

        def branches(rb, carry):
            rows = _rows(rb)
            ca_v = ca_ref[rows, :]
            mu = jnp.mean(ca_v, axis=-1, keepdims=True)
            xc = ca_v - mu
            rstd = lax.rsqrt(jnp.mean(xc * xc, axis=-1, keepdims=True) + EPS)
            xhat = xc * rstd
            ln = xhat * lng_ref[...] + lnb_ref[...]
            sl = _sigmoid(ln)
            ua = ln * sl
            a_z = split(2, rows)
            sz = _sigmoid(a_z)
            dpa = dpa_s[rows, :]
            put(2, rows, dpa * ua * (sz * (1.0 + a_z * (1.0 - sz))))
            dln = dpa * (a_z * sz) * (sl * (1.0 + ln * (1.0 - sl)))
            accum(ROW_DLNG, dln * xhat)
            accum(ROW_DLNB, dln)
            dxh = dln * lng_ref[...]
            dca = rstd * (dxh - jnp.mean(dxh, axis=-1, keepdims=True)
                          - xhat * jnp.mean(dxh * xhat, axis=-1, keepdims=True))
            accum(ROW_DCAB, dca)
            ext_d[rows, :] = dca
            dua0_s[rows, :] = jnp.zeros((RB, d), F32)
            dm_s[rows, :] = split(0, rows) * _sigmoid(split(1, rows))
            b_z = split(6, rows)
            szb = _sigmoid(b_z)
            dpb = dpb_s[rows, :]
            b_b = split(3, rows)
            cb_v = cb_ref[rows, :]
            put(6, rows, dpb * (b_b * cb_v) * (szb * (1.0 + b_z * (1.0 - szb))))
            dub = dpb * (b_z * szb)
            put(3, rows, dub * cb_v)
            ext_e[rows, :] = dub * b_b
            return carry
        lax.fori_loop(0, nrb, branches, 0)

        for r, taps in _conv_a_taps(0, CONV_A - 1):
            if r == 0:
                src = ext_d
            else:
                sh[...] = ext_d[r:r + shl, :]
                src = sh

            def conv_t(rb, carry, src=src, taps=taps):
                rows = _rows(rb)
                ua0 = dm_s[rows, :]
                dua0 = dua0_s[rows, :]
                for q, lag in taps:
                    k = CONV_A - 1 - lag
                    slab = src[pl.ds(pl.multiple_of(rb * RB + 8 * q, 8), RB), :]
                    dua0 = dua0 + slab * wa_ref[k:k + 1, :]
                    accum(ROW_DWA + k, slab * ua0)
                dua0_s[rows, :] = dua0
                return carry
            lax.fori_loop(0, nrb, conv_t, 0)
        ext_d[TM:TM + HALO_A, :] = ext_d[0:HALO_A, :]

        dpb_s[...] = ext_e[0:TM, :] * wb_ref[CONV_B - 1:CONV_B, :]
        for lag in range(CONV_B):
            k = CONV_B - 1 - lag
            if lag > 0:
                sh[0:TM, :] = ext_e[lag:lag + TM, :]
                dpb_s[...] += sh[0:TM, :] * wb_ref[k:k + 1, :]
            src = ext_e if lag == 0 else sh

            def conv_b_w(rb, carry, src=src, k=k):
                rows = _rows(rb)
                accum(ROW_DWB + k, src[rows, :] * (split(4, rows) * split(5, rows)))
                return carry
            lax.fori_loop(0, nrb, conv_b_w, 0)
        ext_e[TM:TM + HALO_B, :] = ext_e[0:HALO_B, :]

        def inputs(rb, carry):
            rows = _rows(rb)
            dua0 = dua0_s[rows, :]
            a_val = split(0, rows)
            sg = _sigmoid(split(1, rows))
            put(0, rows, dua0 * sg)
            put(1, rows, dua0 * a_val * sg * (1.0 - sg))
            dcbin = dpb_s[rows, :]
            put(4, rows, dcbin * split(5, rows))
            put(5, rows, dcbin * split(4, rows))
            return carry
        lax.fori_loop(0, nrb, inputs, 0)

        @pl.when(step == nt - 1)
        def _():
            for row in range(SM_ROWS):
                sm_ref[row:row + 1, :] = jnp.sum(acc[row], axis=0, keepdims=True)

    rev = lambda i: (nt - 1 - i, 0)
    row_f32 = pl.BlockSpec((TM, d), rev)
    const = lambda shape: pl.BlockSpec(shape, lambda i: (0,) * len(shape))
    return pl.pallas_call(
        body, name="b1_mix", grid=(nt,),
        in_specs=[row_f32, pl.BlockSpec((TM, N_SPLIT * d), rev), row_f32, row_f32, row_f32, row_f32,
                  const((3, d, d)), const(wa.shape), const(wb.shape), const((1, d)), const((1, d))],
        out_specs=[pl.BlockSpec((TM, N_SPLIT * d), rev),
                   pl.BlockSpec((3, TM, d), lambda i: (0, nt - 1 - i, 0)),
                   const((SM_ROWS, d))],
        out_shape=[jax.ShapeDtypeStruct((tp, N_SPLIT * d), BF16), jax.ShapeDtypeStruct((3, tp, d), BF16),
                   jax.ShapeDtypeStruct((SM_ROWS, d), F32)],
        scratch_shapes=[pltpu.VMEM((TM + HALO_A, d), F32), pltpu.VMEM((TM + HALO_B, d), F32),
                        pltpu.VMEM((shl, d), F32), pltpu.VMEM((TM, d), F32), pltpu.VMEM((TM, d), F32),
                        pltpu.VMEM((TM, d), F32), pltpu.VMEM((TM, d), F32),
                        pltpu.VMEM((SM_ROWS, 8, d), F32)],
        compiler_params=_params(("arbitrary",)),
    )(ds2, proj, ca, cb, ya, yb, w3, wa, wb, ln_g, ln_b)


def kernel(x, meta_tokens, norm_g, w_in, conv_a_w, conv_a_b, ln_a_g, ln_a_b, w_a_out, b_a_out, conv_b_w, w_b_out, w_out, final_g, loss_target, m_meta_tokens, m_norm_g, m_w_in, m_conv_a_w, m_conv_a_b, m_ln_a_g, m_ln_a_b, m_w_a_out, m_b_a_out, m_conv_b_w, m_w_b_out, m_w_out, m_final_g, v_meta_tokens, v_norm_g, v_w_in, v_conv_a_w, v_conv_a_b, v_ln_a_g, v_ln_a_b, v_w_a_out, v_b_a_out, v_conv_b_w, v_w_b_out, v_w_out, v_final_g):
    seq, d = x.shape[1], x.shape[2]
    dc = meta_tokens.shape[1]
    sw = w_in.shape[2]
    rsh = w_a_out.shape[1]
    xi, yi, ci = _mesh_pos()
    me = 2 * xi + yi
    pos = jnp.stack([ci, me]).astype(jnp.int32)

    conv_rows = HALO_A + HALO_B + 8
    convs = jnp.concatenate([
        jnp.pad(conv_a_w[0], ((0, HALO_A - CONV_A), (0, 0))),
        jnp.pad(conv_b_w[0], ((0, HALO_B - CONV_B), (0, 0))), jnp.zeros((8, dc), F32)], axis=0)[None]
    w3_own = jnp.stack([w_a_out[0], w_b_out[0], w_out[0]])
    (metag,) = _all_gather([_place_own(meta_tokens[None], pos, F32, "place_meta")])
    meta_full = jnp.transpose(metag[0], (1, 0, 2)).reshape(N_META, N_CHIPS * dc)
    fg2 = final_g.reshape(1, d)

    first_tile = jnp.concatenate([jnp.zeros((TM - N_META, d), F32), meta_full], axis=0)
    s_pad = jnp.concatenate([first_tile, x[0]], axis=0)

    proj, (wg_in, wg3, convg) = _proj_fwd(s_pad, norm_g, [_place_own(w_in, pos, BF16, "place_in"),
                                                          _place_own(w3_own, pos, BF16, "place_sq"),
                                                          _place_own(convs, pos, F32, "place_conv")], pos)
    w3 = wg3.reshape(3, N_CHIPS * rsh, d)
    convg = jnp.transpose(convg[0], (1, 0, 2)).reshape(conv_rows, N_CHIPS * dc)
    wa_full = convg[0:HALO_A]
    wb_full = convg[HALO_A:HALO_A + HALO_B]
    ca, cb, ya, yb, abm_t, ds2, h_t, loss8, dfg8 = _mix_fwd(
        s_pad, proj, loss_target[0], w3, wa_full, wb_full, conv_a_b, ln_a_g, ln_a_b, b_a_out, fg2, norm_g)
    dproj, d3, sm = _mix_bwd(ds2, proj, ca, cb, ya, yb, w3, wa_full, wb_full, ln_a_g, ln_a_b)
    cw_sq = _col_block(d, 512)
    p32_sq, pbf_sq, _ = _dw_reduced(
        abm_t, d3, cw_sq, (0, d // cw_sq), N_CHIPS, (3, N_CHIPS, rsh // 2, d), (None, N_CHIPS, rsh // 2, cw_sq),
        lambda u: (u // (d // cw_sq), 0, 0, u % (d // cw_sq)), None, "dw_square")
    cw_in = _col_block(sw, 768)
    ncol = sw // cw_in
    p32_in, pbf_in, l_sq = _dw_reduced(
        h_t[None], dproj[None], cw_in, (0, N_CHIPS * ncol), 1, (1, N_CHIPS, d // 2, sw), (None, None, d // 2, cw_in),
        lambda u: (0, u // ncol, 0, u % ncol), (pbf_sq, None, pbf_sq.shape), "dw_in")
    ds, dng8, l_in = _dh_bwd(dproj, wg_in, s_pad, ds2, norm_g, pbf_in, lax.empty(pbf_in.shape, BF16), None)
    half_in = _sum_chips([p32_in], l_in, pos, "rs_sum_in")
    half_sq = _sum_chips([p32_sq], l_sq, pos, "rs_sum_sq")
    tail_row = lax.broadcasted_iota(jnp.int32, (8, d), 0)
    tail = jnp.where(tail_row == 0, dng8, jnp.where(tail_row == 1, dfg8,
                     jnp.where(tail_row == 2, loss8[0, 0], 0.0)))
    block = jnp.concatenate([sm, ds[TM - N_META:TM], tail], axis=0)
    (other_in, other_sq), red = _sibling_swap([half_in, half_sq], block)
    col = lax.dynamic_slice(red, (0, me * dc), (AR_ROWS, dc))
    g_small = {
        "meta_tokens": col[ROW_DMETA:ROW_DMETA + N_META],
        "norm_g": red[ROW_DNG:ROW_DNG + 1],
        "conv_a_w": col[ROW_DWA:ROW_DWA + CONV_A][None],
        "conv_a_b": red[ROW_DCAB:ROW_DCAB + 1],
        "ln_a_g": red[ROW_DLNG:ROW_DLNG + 1],
        "ln_a_b": red[ROW_DLNB:ROW_DLNB + 1],
        "b_a_out": red[ROW_DBAO:ROW_DBAO + 1],
        "conv_b_w": col[ROW_DWB:ROW_DWB + CONV_B][None],
        "final_g": red[ROW_DFG],
    }

    upd_in = _adam_halves([w_in], [m_w_in], [v_w_in], half_in, other_in, pos, "adam_in")
    upd_sq = _adam_halves([w_a_out, w_b_out, w_out], [m_w_a_out, m_w_b_out, m_w_out],
                          [v_w_a_out, v_w_b_out, v_w_out], half_sq, other_sq, pos, "adam_sq")
    small_w = {"meta_tokens": (meta_tokens, m_meta_tokens, v_meta_tokens), "norm_g": (norm_g, m_norm_g, v_norm_g),
               "conv_a_w": (conv_a_w, m_conv_a_w, v_conv_a_w), "conv_a_b": (conv_a_b, m_conv_a_b, v_conv_a_b),
               "ln_a_g": (ln_a_g, m_ln_a_g, v_ln_a_g), "ln_a_b": (ln_a_b, m_ln_a_b, v_ln_a_b),
               "b_a_out": (b_a_out, m_b_a_out, v_b_a_out), "conv_b_w": (conv_b_w, m_conv_b_w, v_conv_b_w),
               "final_g": (final_g, m_final_g, v_final_g)}
    names_small = list(small_w)
    as2d = lambda t: t.reshape(-1, t.shape[-1])
    upd_small = _adam_small([(as2d(small_w[k][0]), as2d(g_small[k]), as2d(small_w[k][1]), as2d(small_w[k][2]))
                             for k in names_small])

    grads, deltas, new_m, new_v = dict(g_small), {}, {}, {}
    for k, upd in zip(names_small, upd_small):
        deltas[k], new_m[k], new_v[k] = [t.reshape(small_w[k][0].shape) for t in upd]
    grads["w_in"], deltas["w_in"], new_m["w_in"], new_v["w_in"] = upd_in
    for idx, k in enumerate(["w_a_out", "w_b_out", "w_out"]):
        grads[k], deltas[k], new_m[k], new_v[k] = upd_sq[4 * idx:4 * idx + 4]

    loss = red[ROW_LOSS, 0]
    grad_x = ds[TM:][None]
    order = ["meta_tokens", "norm_g", "w_in", "conv_a_w", "conv_a_b", "ln_a_g", "ln_a_b", "w_a_out", "b_a_out",
             "conv_b_w", "w_b_out", "w_out", "final_g"]
    return (loss, grad_x, *[grads[k] for k in order], *[deltas[k] for k in order],
            *[new_m[k] for k in order], *[new_v[k] for k in order])
```

```python
import functools

import jax
import jax.numpy as jnp
from jax import lax
from jax.experimental import pallas as pl
from jax.experimental.pallas import tpu as pltpu

F32 = jnp.float32
BF16 = jnp.bfloat16
MESH = pl.DeviceIdType.MESH

EPS = 1e-6
N_META = 16
N_SPLIT = 9
CONV_A = 31
CONV_B = 3
HALO_A = 32
HALO_B = 8
SHIFT_ROWS = 24
TM = 256
RB = 64
N_ROW_TILES_BIG = 8
ROW_BLOCK = 256
N_CHIPS = 4
VMEM_LIMIT = 56 * 1024 * 1024
VMEM_LIMIT_BIG = 62 * 1024 * 1024

ADAM_LR = 0.001
ADAM_B1 = 0.9
ADAM_B2 = 0.999
ADAM_EPS = 1e-08
ADAM_WD = 0.01
ADAM_STEP = 10

ROW_DWA = 0
ROW_DWB = 32
ROW_DCAB = 40
ROW_DLNG = 41
ROW_DLNB = 42
ROW_DBAO = 43
SM_ROWS = 48
ROW_DMETA = 48
ROW_DNG = 64
ROW_DFG = 65
ROW_LOSS = 66
AR_ROWS = 72


def _sigmoid(v):
    return 0.5 * jnp.tanh(0.5 * v) + 0.5


def _params(sem, **kw):
    return pltpu.CompilerParams(dimension_semantics=sem, vmem_limit_bytes=VMEM_LIMIT, **kw)


def _rows(rb):
    return pl.ds(pl.multiple_of(rb * RB, RB), RB)


def _mesh_pos():
    x, y, c = lax.axis_index("x"), lax.axis_index("y"), lax.axis_index("c")
    return x, y, c


def _half(ref, j, c):
    h = ref.shape[2] // 2
    return ref.at[:, j, pl.ds(c * h, h), :]


def _place_own(shard, pos, dtype, name):
    s, r, c = shard.shape
    rb = ROW_BLOCK if r % ROW_BLOCK == 0 else r

    def body(pos_ref, x_ref, o_ref):
        o_ref[...] = x_ref[...].astype(dtype)

    return pl.pallas_call(
        body, name=name,
        grid_spec=pltpu.PrefetchScalarGridSpec(
            num_scalar_prefetch=1, grid=(s, r // rb),
            in_specs=[pl.BlockSpec((None, rb, c), lambda si, b, pos_ref: (si, b, 0))],
            out_specs=pl.BlockSpec((None, None, rb, c), lambda si, b, pos_ref: (si, pos_ref[1], b, 0))),
        out_shape=jax.ShapeDtypeStruct((s, N_CHIPS, r, c), dtype),
        compiler_params=_params(("arbitrary",) * 2),
    )(pos, shard)


def _all_gather(bufs):
    n = len(bufs)

    def body(*refs):
        outs = refs[n:2 * n]
        send_sems, recv_sems = refs[2 * n:]
        x, y, c = _mesh_pos()
        me = 2 * x + y
        sibling = (x, y, 1 - c)
        chips = [(1 - x, y), (x, 1 - y), (1 - x, 1 - y)]

        def remote(a, k, piece_src, piece_dst, to):
            return pltpu.make_async_remote_copy(
                src_ref=piece_src, dst_ref=piece_dst, send_sem=send_sems.at[6 * a + k],
                recv_sem=recv_sems.at[6 * a + k], device_id=to, device_id_type=MESH)

        sends = []
        for a in range(n):
            mine = _half(outs[a], me, c)
            for k, (px, py) in enumerate(chips):
                sends.append(remote(a, k, mine, mine, (px, py, c)))
        for cp in sends:
            cp.start()
        for a in range(n):
            for k, (px, py) in enumerate(chips):
                piece = _half(outs[a], 2 * px + py, c)
                remote(a, k, piece, piece, (px, py, c)).wait_recv()
                fwd = remote(a, 3 + k, piece, piece, sibling)
                fwd.start()
                sends.append(fwd)
        for a in range(n):
            for k, (px, py) in enumerate(chips):
                piece = _half(outs[a], 2 * px + py, 1 - c)
                remote(a, 3 + k, piece, piece, sibling).wait_recv()
        for cp in sends:
            cp.wait_send()

    any_spec = pl.BlockSpec(memory_space=pl.ANY)
    return pl.pallas_call(
        body, name="ag_weights",
        in_specs=[any_spec] * n, out_specs=[any_spec] * n,
        out_shape=[jax.ShapeDtypeStruct(b.shape, b.dtype) for b in bufs],
        input_output_aliases={a: a for a in range(n)},
        scratch_shapes=[pltpu.SemaphoreType.DMA((6 * n,)), pltpu.SemaphoreType.DMA((6 * n,))],
    )(*bufs)


class _Exchange:
    def __init__(self, sends, recvs):
        self.sends, self.recvs = sends, recvs

    @staticmethod
    def _each(pairs, act):
        for cond, cp in pairs:
            if cond is None:
                act(cp)
            else:
                pl.when(cond)(functools.partial(act, cp))

    def start(self):
        self._each(self.sends, lambda cp: cp.start())

    def finish(self):
        self._each(self.recvs, lambda cp: cp.wait_recv())
        self._each(self.sends, lambda cp: cp.wait_send())


def _chip_exchange(part_ref, land_ref, send_sems, recv_sems, half=None):
    x, y, c = _mesh_pos()
    me = 2 * x + y
    sends, recvs = [], []
    for k, (px, py) in enumerate([(1 - x, y), (x, 1 - y), (1 - x, 1 - y)]):
        sems = dict(send_sem=send_sems.at[k], recv_sem=recv_sems.at[k], device_id=(px, py, c), device_id_type=MESH)
        slot = 2 * px + py if half is None else py
        sends.append((None if half is None else px == half, pltpu.make_async_remote_copy(
            src_ref=part_ref.at[:, slot], dst_ref=land_ref.at[:, me], **sems)))
        landed = land_ref.at[:, 2 * px + py]
        recvs.append((None if half is None else x == half,
                      pltpu.make_async_remote_copy(src_ref=landed, dst_ref=landed, **sems)))
    return _Exchange(sends, recvs)

def _sibling_swap(halves, small):
    n = len(halves)

    def body(*refs):
        ins, small_ref, outs, red_ref = refs[:n], refs[n], refs[n + 1:2 * n + 1], refs[2 * n + 1]
        send_sems, recv_sems = refs[2 * n + 2:2 * n + 4]
        reduce = _SmallAllReduce(small_ref, red_ref, *refs[2 * n + 4:])
        x, y, c = _mesh_pos()
        copies = [pltpu.make_async_remote_copy(
            src_ref=ins[a], dst_ref=outs[a], send_sem=send_sems.at[a], recv_sem=recv_sems.at[a],
            device_id=(x, y, 1 - c), device_id_type=MESH) for a in range(n)]
        reduce.start()
        for cp in copies:
            cp.start()
        reduce.between_chips()
        reduce.finish()
        for cp in copies:
            cp.wait()

    any_spec = pl.BlockSpec(memory_space=pl.ANY)
    vm = pl.BlockSpec(memory_space=pltpu.VMEM)
    outs = pl.pallas_call(
        body, name="rs_swap",
        in_specs=[any_spec] * n + [vm], out_specs=[any_spec] * n + [vm],
        out_shape=[jax.ShapeDtypeStruct(h.shape, h.dtype) for h in halves]
        + [jax.ShapeDtypeStruct(small.shape, F32)],
        scratch_shapes=[pltpu.SemaphoreType.DMA((n,)), pltpu.SemaphoreType.DMA((n,))]
        + _SmallAllReduce.scratch(*small.shape),
    )(*halves, small)
    return outs[:n], outs[n]


class _SmallAllReduce:
    def __init__(self, x_ref, out_ref, sib_ref, part_ref, peers_ref, send_sems, recv_sems):
        self.x_ref, self.out_ref, self.sib_ref, self.part_ref, self.peers_ref = x_ref, out_ref, sib_ref, part_ref, peers_ref
        x, y, c = _mesh_pos()
        self.me = 2 * x + y
        self.swap = pltpu.make_async_remote_copy(
            src_ref=x_ref, dst_ref=sib_ref, send_sem=send_sems.at[0], recv_sem=recv_sems.at[0],
            device_id=(x, y, 1 - c), device_id_type=MESH)
        self.sends, self.recvs = [], []
        for k, (px, py) in enumerate([(1 - x, y), (x, 1 - y), (1 - x, 1 - y)]):
            sems = dict(send_sem=send_sems.at[1 + k], recv_sem=recv_sems.at[1 + k],
                        device_id=(px, py, c), device_id_type=MESH)
            self.sends.append(pltpu.make_async_remote_copy(src_ref=part_ref, dst_ref=peers_ref.at[self.me], **sems))
            landed = peers_ref.at[2 * px + py]
            self.recvs.append(pltpu.make_async_remote_copy(src_ref=landed, dst_ref=landed, **sems))

    @staticmethod
    def scratch(rows, d):
        return [pltpu.VMEM((rows, d), F32), pltpu.VMEM((rows, d), F32), pltpu.VMEM((N_CHIPS, rows, d), F32),
                pltpu.SemaphoreType.DMA((4,)), pltpu.SemaphoreType.DMA((4,))]

    def start(self):
        self.swap.start()

    def between_chips(self):
        self.swap.wait()
        self.part_ref[...] = self.x_ref[...] + self.sib_ref[...]
        self.peers_ref[self.me] = self.part_ref[...]
        for cp in self.sends:
            cp.start()

    def finish(self):
        for cp in self.recvs:
            cp.wait_recv()
        for cp in self.sends:
            cp.wait_send()
        p = self.peers_ref
        self.out_ref[...] = ((p[0] + p[1]) + p[2]) + p[3]


def _sum_chips(p32s, landed, pos, name):
    s, _, h, c = p32s[0].shape
    hb = min(h, ROW_BLOCK)
    per = N_CHIPS // len(p32s)

    def body(pos_ref, *refs):
        own_refs, (l1_ref, l2_ref, l3_ref, out_ref) = refs[:len(p32s)], refs[len(p32s):]
        own = own_refs[0][...]
        for k in range(1, len(p32s)):
            own = jnp.where(pos_ref[1] // per == k, own_refs[k][...], own)
        out_ref[...] = ((own + l1_ref[...].astype(F32)) + l2_ref[...].astype(F32)) + l3_ref[...].astype(F32)

    own_spec = pl.BlockSpec((None, None, hb, c), lambda si, b, pos_ref: (si, pos_ref[1] % per, b, 0))

    def slot(k):
        return pl.BlockSpec((None, None, hb, c), lambda si, b, pos_ref: (si, (pos_ref[1] + k) % N_CHIPS, b, 0))

    return pl.pallas_call(
        body, name=name,
        grid_spec=pltpu.PrefetchScalarGridSpec(
            num_scalar_prefetch=1, grid=(s, h // hb),
            in_specs=[own_spec] * len(p32s) + [slot(1), slot(2), slot(3)],
            out_specs=pl.BlockSpec((None, hb, c), lambda si, b, pos_ref: (si, b, 0))),
        out_shape=jax.ShapeDtypeStruct((s, h, c), F32),
        compiler_params=_params(("arbitrary",) * 2),
    )(pos, *p32s, landed, landed, landed)


def _adamw(w, g, m, v):
    m = ADAM_B1 * m + (1.0 - ADAM_B1) * g
    v = ADAM_B2 * v + (1.0 - ADAM_B2) * (g * g)
    m_hat = m / (1.0 - ADAM_B1 ** ADAM_STEP)
    v_hat = v / (1.0 - ADAM_B2 ** ADAM_STEP)
    delta = -ADAM_LR * (m_hat / (jnp.sqrt(v_hat) + ADAM_EPS) + ADAM_WD * w)
    return delta, m, v


def _adam_halves(ws, ms, vs, g_own, g_recv, pos, name):
    n = len(ws)
    _, r, c = ws[0].shape
    h = r // 2
    rb = min(h, ROW_BLOCK)
    nb = h // rb

    def body(pos_ref, *refs):
        w_refs, m_refs, v_refs = refs[:n], refs[n:2 * n], refs[2 * n:3 * n]
        go_ref, gr_ref = refs[3 * n:3 * n + 2]
        outs = refs[3 * n + 2:]
        mine = pl.program_id(0) == pos_ref[0]
        for a in range(n):
            g = jnp.where(mine, go_ref[a], gr_ref[a])
            delta, m, v = _adamw(w_refs[a][...], g, m_refs[a][...], v_refs[a][...])
            outs[4 * a][...], outs[4 * a + 1][...], outs[4 * a + 2][...], outs[4 * a + 3][...] = g, delta, m, v

    spec_w = pl.BlockSpec((None, rb, c), lambda hf, b, pos_ref: (0, hf * nb + b, 0))
    spec_g = pl.BlockSpec((n, rb, c), lambda hf, b, pos_ref: (0, b, 0))
    return pl.pallas_call(
        body, name=name,
        grid_spec=pltpu.PrefetchScalarGridSpec(
            num_scalar_prefetch=1, grid=(2, nb), in_specs=[spec_w] * (3 * n) + [spec_g] * 2,
            out_specs=[spec_w] * (4 * n)),
        out_shape=[jax.ShapeDtypeStruct((1, r, c), F32)] * (4 * n),
        compiler_params=_params(("arbitrary",) * 2),
    )(pos, *ws, *ms, *vs, g_own, g_recv)


def _adam_small(items):
    n = len(items)

    def body(*refs):
        ins, outs = refs[:4 * n], refs[4 * n:]
        for a in range(n):
            w_ref, g_ref, m_ref, v_ref = ins[4 * a:4 * a + 4]
            d, m, v = _adamw(w_ref[...], g_ref[...], m_ref[...], v_ref[...])
            outs[3 * a][...] = d
            outs[3 * a + 1][...] = m
            outs[3 * a + 2][...] = v

    vm = pl.BlockSpec(memory_space=pltpu.VMEM)
    flat = [t for it in items for t in it]
    outs = pl.pallas_call(
        body, name="adam_small", in_specs=[vm] * (4 * n), out_specs=[vm] * (3 * n),
        out_shape=[jax.ShapeDtypeStruct(it[0].shape, F32) for it in items for _ in range(3)],
    )(*flat)
    return [tuple(outs[3 * a:3 * a + 3]) for a in range(n)]


def _shard_of_step(js, me):
    flip = jnp.where(js == 1, 2, jnp.where(js == 2, 1, jnp.where(js == 3, 3, 0)))
    return lax.bitwise_xor(me, flip)


def _proj_fwd(s_pad, norm_g, bufs, pos):
    tp, d = s_pad.shape
    _, nsh, _, sw = bufs[0].shape
    tmb = tp // N_ROW_TILES_BIG
    n = len(bufs)

    def body(pos_ref, s_ref, g_ref, *refs):
        proj_ref = refs[n]
        gbufs = refs[n + 1:2 * n + 1]
        wbuf, wsems, send_sems, recv_sems = refs[2 * n + 1:]
        x, y, c = _mesh_pos()
        me = 2 * x + y
        sibling = (x, y, 1 - c)
        chips = [(1 - x, y), (x, 1 - y), (1 - x, 1 - y)]
        js, i = pl.program_id(0), pl.program_id(1)

        def remote(a, k, piece, to):
            return pltpu.make_async_remote_copy(
                src_ref=piece, dst_ref=piece, send_sem=send_sems.at[6 * a + k],
                recv_sem=recv_sems.at[6 * a + k], device_id=to, device_id_type=MESH)

        def fetch(chip, step):
            return pltpu.make_async_copy(gbufs[0].at[0, chip], wbuf.at[step % 2], wsems.at[step % 2])

        chip_ids = [2 * px + py for px, py in chips]
        relayed_chip = jnp.where(c == 0, chip_ids[0], chip_ids[1])
        relay_to = (jnp.where(c == 0, x, 1 - x), jnp.where(c == 0, 1 - y, y), c)

        def own_piece(a, k):
            return remote(a, k, _half(gbufs[a], me, c), (*chips[k], c))

        def relay(a):
            return remote(a, 2, _half(gbufs[a], relayed_chip, c), relay_to)

        def to_sibling(a, k, core):
            return remote(a, 3 + k, _half(gbufs[a], chip_ids[k], core), sibling)

        def landed(a, k):
            return remote(a, k, _half(gbufs[a], chip_ids[k], c), (*chips[k], c))

        def take_neighbours(a):
            landed(a, 0).wait_recv()
            landed(a, 1).wait_recv()
            relay(a).start()
            for k in range(2):
                to_sibling(a, k, c).start()
            for k in range(2):
                to_sibling(a, k, 1 - c).wait_recv()

        def take_diagonal(a):
            landed(a, 2).wait_recv()
            to_sibling(a, 2, c).start()
            to_sibling(a, 2, 1 - c).wait_recv()

        @pl.when((js == 0) & (i == 0))
        def _():
            for a in range(n):
                for k in range(2):
                    own_piece(a, k).start()
            fetch(me, 0).start()
            fetch(me, 0).wait()

        @pl.when((js == 1) & (i == 0))
        def _():
            take_neighbours(0)
            fetch(chip_ids[0], 1).start()
            fetch(chip_ids[0], 1).wait()

        @pl.when((js == 1) & (i == N_ROW_TILES_BIG - 2))
        def _():
            fetch(chip_ids[1], 2).start()

        @pl.when((js == 2) & (i == 0))
        def _():
            fetch(chip_ids[1], 2).wait()

        @pl.when((js == 2) & (i == 1))
        def _():
            for a in range(1, n):
                take_neighbours(a)

        @pl.when((js == 2) & (i == N_ROW_TILES_BIG - 2))
        def _():
            take_diagonal(0)
            fetch(chip_ids[2], 3).start()

        @pl.when((js == 3) & (i == 0))
        def _():
            fetch(chip_ids[2], 3).wait()

        s = s_ref[...]
        r = lax.rsqrt(jnp.mean(s * s, axis=-1, keepdims=True) + EPS)
        h = (s * r * g_ref[...]).astype(BF16)
        proj_ref[...] = jnp.dot(h, wbuf[js % 2], preferred_element_type=F32).astype(BF16)

        @pl.when((js == nsh - 1) & (i == N_ROW_TILES_BIG - 1))
        def _():
            for a in range(1, n):
                take_diagonal(a)
            for a in range(n):
                for k in range(2):
                    own_piece(a, k).wait_send()
                relay(a).wait_send()
                for k in range(3):
                    to_sibling(a, k, c).wait_send()

    any_spec = pl.BlockSpec(memory_space=pl.ANY)
    outs = pl.pallas_call(
        body, name="f1_proj",
        grid_spec=pltpu.PrefetchScalarGridSpec(
            num_scalar_prefetch=1, grid=(nsh, N_ROW_TILES_BIG),
            in_specs=[pl.BlockSpec((tmb, d), lambda js, i, pos_ref: (i, 0)),
                      pl.BlockSpec((1, d), lambda js, i, pos_ref: (0, 0))] + [any_spec] * n,
            out_specs=[pl.BlockSpec((tmb, sw), lambda js, i, pos_ref: (i, _shard_of_step(js, pos_ref[1])))]
            + [any_spec] * n,
            scratch_shapes=[pltpu.VMEM((2, d, sw), BF16), pltpu.SemaphoreType.DMA((2,)),
                            pltpu.SemaphoreType.DMA((6 * n,)), pltpu.SemaphoreType.DMA((6 * n,))]),
        out_shape=[jax.ShapeDtypeStruct((tp, nsh * sw), BF16)]
        + [jax.ShapeDtypeStruct(b.shape, b.dtype) for b in bufs],
        input_output_aliases={3 + a: 1 + a for a in range(n)},
        compiler_params=_params(("arbitrary", "arbitrary")),
    )(pos, s_pad, norm_g, *bufs)
    return outs[0], outs[1:]


def _dh_bwd(dproj, wg_in, s_pad, ds2, norm_g, part, land, half):
    tp, d = s_pad.shape
    _, nsh, _, sw = wg_in.shape
    tmb = tp // N_ROW_TILES_BIG

    def body(dp_ref, w_hbm, s_ref, ds2_ref, g_ref, part_ref, _, ds_ref, dng_ref, land_ref, wbuf, gacc,
             wsem, send_sems, recv_sems):
        exchange = _chip_exchange(part_ref, land_ref, send_sems, recv_sems, half)
        i = pl.program_id(0)

        @pl.when(i == 0)
        def _():
            exchange.start()
            gacc[...] = jnp.zeros_like(gacc)
            whole = pltpu.make_async_copy(w_hbm.at[0], wbuf, wsem)
            whole.start()
            whole.wait()

        dh = None
        for j in range(nsh):
            part = lax.dot_general(dp_ref[:, j * sw:(j + 1) * sw], wbuf[j], (((1,), (1,)), ((), ())),
                                   preferred_element_type=F32)
            dh = part if dh is None else dh + part
        s = s_ref[...]
        r = lax.rsqrt(jnp.mean(s * s, axis=-1, keepdims=True) + EPS)
        gacc[...] += (dh * s * r).reshape(tmb // 8, 8, d).sum(axis=0)
        t = dh * g_ref[...]
        ds_ref[...] = ds2_ref[...] + r * t - s * (r * r * r) * jnp.mean(t * s, axis=-1, keepdims=True)

        @pl.when(i == N_ROW_TILES_BIG - 1)
        def _():
            dng_ref[...] = jnp.broadcast_to(jnp.sum(gacc[...], axis=0, keepdims=True), (8, d))
            exchange.finish()

    any_spec = pl.BlockSpec(memory_space=pl.ANY)
    return pl.pallas_call(
        body, name="b2_dh", grid=(N_ROW_TILES_BIG,),
        in_specs=[pl.BlockSpec((tmb, nsh * sw), lambda i: (i, 0)), any_spec,
                  pl.BlockSpec((tmb, d), lambda i: (i, 0)),
                  pl.BlockSpec((tmb, d), lambda i: (i, 0)),
                  pl.BlockSpec((1, d), lambda i: (0, 0)), any_spec, any_spec],
        out_specs=[pl.BlockSpec((tmb, d), lambda i: (i, 0)),
                   pl.BlockSpec((8, d), lambda i: (0, 0)), any_spec],
        out_shape=[jax.ShapeDtypeStruct((tp, d), F32), jax.ShapeDtypeStruct((8, d), F32),
                   jax.ShapeDtypeStruct(land.shape, land.dtype)],
        input_output_aliases={6: 2},
        scratch_shapes=[pltpu.VMEM((nsh, d, sw), BF16), pltpu.VMEM((8, d), F32), pltpu.SemaphoreType.DMA,
                        pltpu.SemaphoreType.DMA((3,)), pltpu.SemaphoreType.DMA((3,))],
        compiler_params=pltpu.CompilerParams(dimension_semantics=("arbitrary",),
                                             vmem_limit_bytes=VMEM_LIMIT_BIG),
    )(dproj, wg_in, s_pad, ds2, norm_g, part, land)


def _col_block(width, cap):
    return max(b for b in range(128, cap + 1, 128) if width % b == 0)


def _dw_reduced(lhs_t, rhs, cw, cols, groups, out_dims, out_block, out_index, carried, name):
    na, d, tp = lhs_t.shape
    col0, per_a = cols
    nblk = na * per_a
    rg = d // groups
    hh = rg // 2

    def body(*refs):
        if carried is None:
            l_ref, r_ref, p32_ref, pbf_ref, res, rbuf, send_sems, recv_sems = refs
            exchange = _Exchange([], [])
        else:
            (l_ref, r_ref, part_ref, p32_ref, pbf_ref, land_ref, res, rbuf, send_sems, recv_sems,
             xsend, xrecv) = refs
            exchange = _chip_exchange(part_ref, land_ref, xsend, xrecv, carried[1])
        x, y, c = _mesh_pos()
        t = pl.program_id(0)
        u = jnp.maximum(t - 1, 0)

        def to_sibling(blk):
            return pltpu.make_async_remote_copy(
                src_ref=res.at[blk % 2, :, pl.ds((1 - c) * hh, hh), :], dst_ref=rbuf.at[blk % 2],
                send_sem=send_sems.at[blk], recv_sem=recv_sems.at[blk],
                device_id=(x, y, 1 - c), device_id_type=MESH)

        @pl.when(t == 0)
        def _():
            exchange.start()

        @pl.when(t < nblk)
        def _():
            res[t % 2] = jnp.dot(l_ref[...], r_ref[...], preferred_element_type=F32).reshape(groups, rg, cw)

        @pl.when(t >= 1)
        def _():
            to_sibling(u).wait_recv()
            p = res[u % 2, :, pl.ds(c * hh, hh), :] + rbuf[u % 2]
            p32_ref[...] = p.reshape(p32_ref.shape)
            pbf_ref[...] = p.reshape(pbf_ref.shape).astype(BF16)

        @pl.when(t < nblk)
        def _():
            to_sibling(t).start()

        @pl.when(t >= 1)
        def _():
            to_sibling(u).wait_send()

        @pl.when(t == nblk)
        def _():
            exchange.finish()

    any_spec = pl.BlockSpec(memory_space=pl.ANY)
    last = nblk - 1
    out_spec = pl.BlockSpec(out_block, lambda t: out_index(jnp.maximum(t - 1, 0)))
    extra = [] if carried is None else [carried[0]]
    outs = pl.pallas_call(
        body, name=name, grid=(nblk + 1,),
        in_specs=[pl.BlockSpec((None, d, tp), lambda t: (jnp.minimum(t, last) // per_a, 0, 0)),
                  pl.BlockSpec((None, tp, cw), lambda t: (jnp.minimum(t, last) // per_a, 0,
                                                          col0 + jnp.minimum(t, last) % per_a))]
        + [any_spec] * len(extra),
        out_specs=[out_spec, out_spec] + [any_spec] * len(extra),
        out_shape=[jax.ShapeDtypeStruct(out_dims, F32), jax.ShapeDtypeStruct(out_dims, BF16)]
        + [jax.ShapeDtypeStruct(carried[2], e.dtype) for e in extra],
        scratch_shapes=[pltpu.VMEM((2, groups, rg, cw), F32), pltpu.VMEM((2, groups, hh, cw), F32),
                        pltpu.SemaphoreType.DMA((nblk,)), pltpu.SemaphoreType.DMA((nblk,))]
        + [pltpu.SemaphoreType.DMA((3,)), pltpu.SemaphoreType.DMA((3,))] * len(extra),
        compiler_params=_params(("arbitrary",)),
    )(lhs_t, rhs, *extra)
    return outs[0], outs[1], (outs[2] if extra else None)


def _conv_a_taps(first_lag, last_lag):
    out = []
    for r in range(8):
        taps = [(q, 8 * q + r) for q in range(5) if first_lag <= 8 * q + r <= last_lag]
        if taps:
            out.append((r, taps))
    return out


def _mix_fwd(s_pad, proj, target, w3, wa, wb, conv_a_b, ln_g, ln_b, b_a_out, final_g, norm_g):
    tp, d = s_pad.shape
    nt = tp // TM
    nrb = TM // RB
    shl = TM + SHIFT_ROWS

    def body(s_ref, proj_ref, tgt_ref, w3_ref, wa_ref, wb_ref, cab_ref, lng_ref, lnb_ref, bao_ref, fg_ref, ng_ref,
             ca_ref, cb_ref, ya_ref, yb_ref, abmt_ref, ds2_ref, ht_ref, loss_ref, dfg_ref,
             abm_ref, ext_a, ext_b, sh, s2_s, lacc, gacc):
        i = pl.program_id(0)

        def split(k, rows):
            return proj_ref[rows, k * d:(k + 1) * d].astype(F32)

        s_in = s_ref[...]
        h = s_in * lax.rsqrt(jnp.mean(s_in * s_in, axis=-1, keepdims=True) + EPS) * ng_ref[...]
        ht_ref[...] = h.T.astype(BF16)

        @pl.when(i == 0)
        def _():
            ext_a[0:HALO_A, :] = jnp.zeros((HALO_A, d), F32)
            ext_b[0:HALO_B, :] = jnp.zeros((HALO_B, d), F32)
            lacc[...] = jnp.zeros_like(lacc)
            gacc[...] = jnp.zeros_like(gacc)

        def conv_in(rb, carry):
            rows = _rows(rb)
            ua0 = split(0, rows) * _sigmoid(split(1, rows))
            ext_a[pl.ds(pl.multiple_of(HALO_A + rb * RB, 8), RB), :] = ua0
            ext_b[pl.ds(pl.multiple_of(HALO_B + rb * RB, 8), RB), :] = split(4, rows) * split(5, rows)
            ca_ref[rows, :] = jnp.broadcast_to(cab_ref[...], (RB, d))
            return carry
        lax.fori_loop(0, nrb, conv_in, 0)

        for r, taps in _conv_a_taps(HALO_A - CONV_A + 1, HALO_A):
            if r == 0:
                src = ext_a
            else:
                sh[...] = ext_a[r:r + shl, :]
                src = sh

            def conv_acc(rb, carry, src=src, taps=taps):
                rows = _rows(rb)
                acc = ca_ref[rows, :]
                for q, lag in taps:
                    k = lag - (HALO_A - CONV_A + 1)
                    acc = acc + src[pl.ds(pl.multiple_of(rb * RB + 8 * q, 8), RB), :] * wa_ref[k:k + 1, :]
                ca_ref[rows, :] = acc
                return carry
            lax.fori_loop(0, nrb, conv_acc, 0)
        ext_a[0:HALO_A, :] = ext_a[TM:TM + HALO_A, :]

        cb_ref[...] = ext_b[HALO_B:HALO_B + TM, :] * wb_ref[2:3, :]
        for k in range(CONV_B - 1):
            off = HALO_B - CONV_B + 1 + k
            sh[0:TM, :] = ext_b[off:off + TM, :]
            cb_ref[...] += sh[0:TM, :] * wb_ref[k:k + 1, :]
        ext_b[0:HALO_B, :] = ext_b[TM:TM + HALO_B, :]

        def branches(rb, carry):
            rows = _rows(rb)
            ca = ca_ref[rows, :]
            mu = jnp.mean(ca, axis=-1, keepdims=True)
            xc = ca - mu
            rstd = lax.rsqrt(jnp.mean(xc * xc, axis=-1, keepdims=True) + EPS)
            ln = xc * rstd * lng_ref[...] + lnb_ref[...]
            ua = ln * _sigmoid(ln)
            a_z = split(2, rows)
            abm_ref[0, rows, :] = (ua * (a_z * _sigmoid(a_z))).astype(BF16)
            b_z = split(6, rows)
            ub = split(3, rows) * cb_ref[rows, :]
            abm_ref[1, rows, :] = (ub * (b_z * _sigmoid(b_z))).astype(BF16)
            return carry
        lax.fori_loop(0, nrb, branches, 0)

        ya_ref[...] = jnp.dot(abm_ref[0], w3_ref[0], preferred_element_type=F32) + bao_ref[...]
        yb_ref[...] = jnp.dot(abm_ref[1], w3_ref[1], preferred_element_type=F32)

        def merge(rb, carry):
            rows = _rows(rb)
            m = _sigmoid(split(7, rows)) * ya_ref[rows, :] + _sigmoid(split(8, rows)) * yb_ref[rows, :]
            abm_ref[2, rows, :] = m.astype(BF16)
            return carry
        lax.fori_loop(0, nrb, merge, 0)

        s2_s[...] = s_ref[...] + jnp.dot(abm_ref[2], w3_ref[2], preferred_element_type=F32)
        for k in range(3):
            abmt_ref[k] = abm_ref[k].astype(F32).T.astype(BF16)
        live = (i > 0).astype(F32)

        def head(rb, carry):
            rows = _rows(rb)
            s2 = s2_s[rows, :]
            r2 = lax.rsqrt(jnp.mean(s2 * s2, axis=-1, keepdims=True) + EPS)
            diff = (s2 * r2 * fg_ref[...] - tgt_ref[rows, :]) * live
            lacc[...] += diff * diff
            dy = diff * (1.0 / d)
            gacc[...] += (dy * s2 * r2).reshape(RB // 8, 8, d).sum(axis=0)
            t = dy * fg_ref[...]
            ds2_ref[rows, :] = r2 * t - s2 * (r2 * r2 * r2) * jnp.mean(t * s2, axis=-1, keepdims=True)
            return carry
        lax.fori_loop(0, nrb, head, 0)

        @pl.when(i == nt - 1)
        def _():
            loss_ref[...] = jnp.broadcast_to(0.5 * jnp.sum(lacc[...]) * (1.0 / d), (8, 128))
            dfg_ref[...] = jnp.broadcast_to(jnp.sum(gacc[...], axis=0, keepdims=True), (8, d))

    row_f32 = pl.BlockSpec((TM, d), lambda i: (i, 0))
    const = lambda shape: pl.BlockSpec(shape, lambda i: (0,) * len(shape))
    return pl.pallas_call(
        body, name="f2_mix", grid=(nt,),
        in_specs=[row_f32,
                  pl.BlockSpec((TM, N_SPLIT * d), lambda i: (i, 0)),
                  pl.BlockSpec((TM, d), lambda i: (jnp.maximum(i - 1, 0), 0)),
                  const((3, d, d)), const(wa.shape), const(wb.shape)] + [const((1, d))] * 6,
        out_specs=[row_f32, row_f32, row_f32, row_f32,
                   pl.BlockSpec((3, d, TM), lambda i: (0, 0, i)),
                   row_f32, pl.BlockSpec((d, TM), lambda i: (0, i)), const((8, 128)), const((8, d))],
        out_shape=[jax.ShapeDtypeStruct((tp, d), F32)] * 4
        + [jax.ShapeDtypeStruct((3, d, tp), BF16), jax.ShapeDtypeStruct((tp, d), F32),
           jax.ShapeDtypeStruct((d, tp), BF16),
           jax.ShapeDtypeStruct((8, 128), F32), jax.ShapeDtypeStruct((8, d), F32)],
        scratch_shapes=[pltpu.VMEM((3, TM, d), BF16),
                        pltpu.VMEM((HALO_A + TM, d), F32), pltpu.VMEM((HALO_B + TM, d), F32),
                        pltpu.VMEM((shl, d), F32), pltpu.VMEM((TM, d), F32),
                        pltpu.VMEM((RB, d), F32), pltpu.VMEM((8, d), F32)],
        compiler_params=_params(("arbitrary",)),
    )(s_pad, proj, target, w3, wa, wb, conv_a_b, ln_g, ln_b, b_a_out, final_g, norm_g)


def _mix_bwd(ds2, proj, ca, cb, ya, yb, w3, wa, wb, ln_g, ln_b):
    tp, d = ds2.shape
    nt = tp // TM
    nrb = TM // RB
    shl = TM + SHIFT_ROWS
    nt_dims = (((1,), (1,)), ((), ()))

    def body(ds2_ref, proj_ref, ca_ref, cb_ref, ya_ref, yb_ref, w3_ref, wa_ref, wb_ref, lng_ref, lnb_ref,
             dproj_ref, d3_ref, sm_ref, ext_d, ext_e, sh, dm_s, dpa_s, dpb_s, dua0_s, acc):
        step = pl.program_id(0)

        def split(k, rows):
            return proj_ref[rows, k * d:(k + 1) * d].astype(F32)

        def put(k, rows, val):
            dproj_ref[rows, k * d:(k + 1) * d] = val.astype(BF16)

        def accum(row, val):
            acc[row] += val.reshape(RB // 8, 8, d).sum(axis=0)

        @pl.when(step == 0)
        def _():
            ext_d[TM:TM + HALO_A, :] = jnp.zeros((HALO_A, d), F32)
            ext_e[TM:TM + HALO_B, :] = jnp.zeros((HALO_B, d), F32)
            acc[...] = jnp.zeros_like(acc)

        d3_ref[2] = ds2_ref[...].astype(BF16)
        dm_s[...] = lax.dot_general(d3_ref[2], w3_ref[2], nt_dims, preferred_element_type=F32)

        def gates(rb, carry):
            rows = _rows(rb)
            dm = dm_s[rows, :]
            sa = _sigmoid(split(7, rows))
            sb = _sigmoid(split(8, rows))
            ya_v = ya_ref[rows, :]
            yb_v = yb_ref[rows, :]
            put(7, rows, dm * ya_v * sa * (1.0 - sa))
            put(8, rows, dm * yb_v * sb * (1.0 - sb))
            dya = dm * sa
            accum(ROW_DBAO, dya)
            d3_ref[0, rows, :] = dya.astype(BF16)
            d3_ref[1, rows, :] = (dm * sb).astype(BF16)
            return carry
        lax.fori_loop(0, nrb, gates, 0)

        dpa_s[...] = lax.dot_general(d3_ref[0], w3_ref[0], nt_dims, preferred_element_type=F32)
        dpb_s[...] = lax.dot_general(d3_ref[1], w3_ref[1], nt_dims, preferred_element_type=F32)

        def branches(rb, carry):
            rows = _rows(rb)
            ca_v = ca_ref[rows, :]
            mu = jnp.mean(ca_v, axis=-1, keepdims=True)
            xc = ca_v - mu
            rstd = lax.rsqrt(jnp.mean(xc * xc, axis=-1, keepdims=True) + EPS)
            xhat = xc * rstd
            ln = xhat * lng_ref[...] + lnb_ref[...]
            sl = _sigmoid(ln)
            ua = ln * sl
            a_z = split(2, rows)
            sz = _sigmoid(a_z)
            dpa = dpa_s[rows, :]
            put(2, rows, dpa * ua * (sz * (1.0 + a_z * (1.0 - sz))))
            dln = dpa * (a_z * sz) * (sl * (1.0 + ln * (1.0 - sl)))
            accum(ROW_DLNG, dln * xhat)
            accum(ROW_DLNB, dln)
            dxh = dln * lng_ref[...]
            dca = rstd * (dxh - jnp.mean(dxh, axis=-1, keepdims=True)
                          - xhat * jnp.mean(dxh * xhat, axis=-1, keepdims=True))
            accum(ROW_DCAB, dca)
            ext_d[rows, :] = dca
            dua0_s[rows, :] = jnp.zeros((RB, d), F32)
            dm_s[rows, :] = split(0, rows) * _sigmoid(split(1, rows))
            b_z = split(6, rows)
            szb = _sigmoid(b_z)
            dpb = dpb_s[rows, :]
            b_b = split(3, rows)
            cb_v = cb_ref[rows, :]
            put(6, rows, dpb * (b_b * cb_v) * (szb * (1.0 + b_z * (1.0 - szb))))
            dub = dpb * (b_z * szb)
            put(3, rows, dub * cb_v)
            ext_e[rows, :] = dub * b_b
            return carry
        lax.fori_loop(0, nrb, branches, 0)

        for r, taps in _conv_a_taps(0, CONV_A - 1):
            if r == 0:
                src = ext_d
            else:
                sh[...] = ext_d[r:r + shl, :]
                src = sh

            def conv_t(rb, carry, src=src, taps=taps):
                rows = _rows(rb)
                ua0 = dm_s[rows, :]
                dua0 = dua0_s[rows, :]
                for q, lag in taps:
                    k = CONV_A - 1 - lag
                    slab = src[pl.ds(pl.multiple_of(rb * RB + 8 * q, 8), RB), :]
                    dua0 = dua0 + slab * wa_ref[k:k + 1, :]
                    accum(ROW_DWA + k, slab * ua0)
                dua0_s[rows, :] = dua0
                return carry
            lax.fori_loop(0, nrb, conv_t, 0)
        ext_d[TM:TM + HALO_A, :] = ext_d[0:HALO_A, :]

        dpb_s[...] = ext_e[0:TM, :] * wb_ref[CONV_B - 1:CONV_B, :]
        for lag in range(CONV_B):
            k = CONV_B - 1 - lag
            if lag > 0:
                sh[0:TM, :] = ext_e[lag:lag + TM, :]
                dpb_s[...] += sh[0:TM, :] * wb_ref[k:k + 1, :]
            src = ext_e if lag == 0 else sh

            def conv_b_w(rb, carry, src=src, k=k):
                rows = _rows(rb)
                accum(ROW_DWB + k, src[rows, :] * (split(4, rows) * split(5, rows)))
                return carry
            lax.fori_loop(0, nrb, conv_b_w, 0)
        ext_e[TM:TM + HALO_B, :] = ext_e[0:HALO_B, :]

        def inputs(rb, carry):
            rows = _rows(rb)
            dua0 = dua0_s[rows, :]
            a_val = split(0, rows)
            sg = _sigmoid(split(1, rows))
            put(0, rows, dua0 * sg)
            put(1, rows, dua0 * a_val * sg * (1.0 - sg))
            dcbin = dpb_s[rows, :]
            put(4, rows, dcbin * split(5, rows))
            put(5, rows, dcbin * split(4, rows))
            return carry
        lax.fori_loop(0, nrb, inputs, 0)

        @pl.when(step == nt - 1)
        def _():
            for row in range(SM_ROWS):
                sm_ref[row:row + 1, :] = jnp.sum(acc[row], axis=0, keepdims=True)

    rev = lambda i: (nt - 1 - i, 0)
    row_f32 = pl.BlockSpec((TM, d), rev)
    const = lambda shape: pl.BlockSpec(shape, lambda i: (0,) * len(shape))
    return pl.pallas_call(
        body, name="b1_mix", grid=(nt,),
        in_specs=[row_f32, pl.BlockSpec((TM, N_SPLIT * d), rev), row_f32, row_f32, row_f32, row_f32,
                  const((3, d, d)), const(wa.shape), const(wb.shape), const((1, d)), const((1, d))],
        out_specs=[pl.BlockSpec((TM, N_SPLIT * d), rev),
                   pl.BlockSpec((3, TM, d), lambda i: (0, nt - 1 - i, 0)),
                   const((SM_ROWS, d))],
        out_shape=[jax.ShapeDtypeStruct((tp, N_SPLIT * d), BF16), jax.ShapeDtypeStruct((3, tp, d), BF16),
                   jax.ShapeDtypeStruct((SM_ROWS, d), F32)],
        scratch_shapes=[pltpu.VMEM((TM + HALO_A, d), F32), pltpu.VMEM((TM + HALO_B, d), F32),
                        pltpu.VMEM((shl, d), F32), pltpu.VMEM((TM, d), F32), pltpu.VMEM((TM, d), F32),
                        pltpu.VMEM((TM, d), F32), pltpu.VMEM((TM, d), F32),
                        pltpu.VMEM((SM_ROWS, 8, d), F32)],
        compiler_params=_params(("arbitrary",)),
    )(ds2, proj, ca, cb, ya, yb, w3, wa, wb, ln_g, ln_b)


def kernel(x, meta_tokens, norm_g, w_in, conv_a_w, conv_a_b, ln_a_g, ln_a_b, w_a_out, b_a_out, conv_b_w, w_b_out, w_out, final_g, loss_target, m_meta_tokens, m_norm_g, m_w_in, m_conv_a_w, m_conv_a_b, m_ln_a_g, m_ln_a_b, m_w_a_out, m_b_a_out, m_conv_b_w, m_w_b_out, m_w_out, m_final_g, v_meta_tokens, v_norm_g, v_w_in, v_conv_a_w, v_conv_a_b, v_ln_a_g, v_ln_a_b, v_w_a_out, v_b_a_out, v_conv_b_w, v_w_b_out, v_w_out, v_final_g):
    seq, d = x.shape[1], x.shape[2]
    dc = meta_tokens.shape[1]
    sw = w_in.shape[2]
    rsh = w_a_out.shape[1]
    xi, yi, ci = _mesh_pos()
    me = 2 * xi + yi
    pos = jnp.stack([ci, me]).astype(jnp.int32)

    conv_rows = HALO_A + HALO_B + 8
    convs = jnp.concatenate([
        jnp.pad(conv_a_w[0], ((0, HALO_A - CONV_A), (0, 0))),
        jnp.pad(conv_b_w[0], ((0, HALO_B - CONV_B), (0, 0))), jnp.zeros((8, dc), F32)], axis=0)[None]
    w3_own = jnp.stack([w_a_out[0], w_b_out[0], w_out[0]])
    (metag,) = _all_gather([_place_own(meta_tokens[None], pos, F32, "place_meta")])
    meta_full = jnp.transpose(metag[0], (1, 0, 2)).reshape(N_META, N_CHIPS * dc)
    fg2 = final_g.reshape(1, d)

    first_tile = jnp.concatenate([jnp.zeros((TM - N_META, d), F32), meta_full], axis=0)
    s_pad = jnp.concatenate([first_tile, x[0]], axis=0)

    proj, (wg_in, wg3, convg) = _proj_fwd(s_pad, norm_g, [_place_own(w_in, pos, BF16, "place_in"),
                                                          _place_own(w3_own, pos, BF16, "place_sq"),
                                                          _place_own(convs, pos, F32, "place_conv")], pos)
    w3 = wg3.reshape(3, N_CHIPS * rsh, d)
    convg = jnp.transpose(convg[0], (1, 0, 2)).reshape(conv_rows, N_CHIPS * dc)
    wa_full = convg[0:HALO_A]
    wb_full = convg[HALO_A:HALO_A + HALO_B]
    ca, cb, ya, yb, abm_t, ds2, h_t, loss8, dfg8 = _mix_fwd(
        s_pad, proj, loss_target[0], w3, wa_full, wb_full, conv_a_b, ln_a_g, ln_a_b, b_a_out, fg2, norm_g)
    dproj, d3, sm = _mix_bwd(ds2, proj, ca, cb, ya, yb, w3, wa_full, wb_full, ln_a_g, ln_a_b)
    cw_sq = _col_block(d, 512)
    p32_sq, pbf_sq, _ = _dw_reduced(
        abm_t, d3, cw_sq, (0, d // cw_sq), N_CHIPS, (3, N_CHIPS, rsh // 2, d), (None, N_CHIPS, rsh // 2, cw_sq),
        lambda u: (u // (d // cw_sq), 0, 0, u % (d // cw_sq)), None, "dw_square")
    cw_in = _col_block(sw, 768)
    ncol = sw // cw_in
    p32_in, pbf_in, l_sq = _dw_reduced(
        h_t[None], dproj[None], cw_in, (0, N_CHIPS * ncol), 1, (1, N_CHIPS, d // 2, sw), (None, None, d // 2, cw_in),
        lambda u: (0, u // ncol, 0, u % ncol), (pbf_sq, None, pbf_sq.shape), "dw_in")
    ds, dng8, l_in = _dh_bwd(dproj, wg_in, s_pad, ds2, norm_g, pbf_in, lax.empty(pbf_in.shape, BF16), None)
    half_in = _sum_chips([p32_in], l_in, pos, "rs_sum_in")
    half_sq = _sum_chips([p32_sq], l_sq, pos, "rs_sum_sq")
    tail_row = lax.broadcasted_iota(jnp.int32, (8, d), 0)
    tail = jnp.where(tail_row == 0, dng8, jnp.where(tail_row == 1, dfg8,
                     jnp.where(tail_row == 2, loss8[0, 0], 0.0)))
    block = jnp.concatenate([sm, ds[TM - N_META:TM], tail], axis=0)
    (other_in, other_sq), red = _sibling_swap([half_in, half_sq], block)
    col = lax.dynamic_slice(red, (0, me * dc), (AR_ROWS, dc))
    g_small = {
        "meta_tokens": col[ROW_DMETA:ROW_DMETA + N_META],
        "norm_g": red[ROW_DNG:ROW_DNG + 1],
        "conv_a_w": col[ROW_DWA:ROW_DWA + CONV_A][None],
        "conv_a_b": red[ROW_DCAB:ROW_DCAB + 1],
        "ln_a_g": red[ROW_DLNG:ROW_DLNG + 1],
        "ln_a_b": red[ROW_DLNB:ROW_DLNB + 1],
        "b_a_out": red[ROW_DBAO:ROW_DBAO + 1],
        "conv_b_w": col[ROW_DWB:ROW_DWB + CONV_B][None],
        "final_g": red[ROW_DFG],
    }

    upd_in = _adam_halves([w_in], [m_w_in], [v_w_in], half_in, other_in, pos, "adam_in")
    upd_sq = _adam_halves([w_a_out, w_b_out, w_out], [m_w_a_out, m_w_b_out, m_w_out],
                          [v_w_a_out, v_w_b_out, v_w_out], half_sq, other_sq, pos, "adam_sq")
    small_w = {"meta_tokens": (meta_tokens, m_meta_tokens, v_meta_tokens), "norm_g": (norm_g, m_norm_g, v_norm_g),
               "conv_a_w": (conv_a_w, m_conv_a_w, v_conv_a_w), "conv_a_b": (conv_a_b, m_conv_a_b, v_conv_a_b),
               "ln_a_g": (ln_a_g, m_ln_a_g, v_ln_a_g), "ln_a_b": (ln_a_b, m_ln_a_b, v_ln_a_b),
               "b_a_out": (b_a_out, m_b_a_out, v_b_a_out), "conv_b_w": (conv_b_w, m_conv_b_w, v_conv_b_w),
               "final_g": (final_g, m_final_g, v_final_g)}
    names_small = list(small_w)
    as2d = lambda t: t.reshape(-1, t.shape[-1])
    upd_small = _adam_small([(as2d(small_w[k][0]), as2d(g_small[k]), as2d(small_w[k][1]), as2d(small_w[k][2]))
                             for k in names_small])

    grads, deltas, new_m, new_v = dict(g_small), {}, {}, {}
    for k, upd in zip(names_small, upd_small):
        deltas[k], new_m[k], new_v[k] = [t.reshape(small_w[k][0].shape) for t in upd]
    grads["w_in"], deltas["w_in"], new_m["w_in"], new_v["w_in"] = upd_in
    for idx, k in enumerate(["w_a_out", "w_b_out", "w_out"]):
        grads[k], deltas[k], new_m[k], new_v[k] = upd_sq[4 * idx:4 * idx + 4]

    loss = red[ROW_LOSS, 0]
    grad_x = ds[TM:][None]
    order = ["meta_tokens", "norm_g", "w_in", "conv_a_w", "conv_a_b", "ln_a_g", "ln_a_b", "w_a_out", "b_a_out",
             "conv_b_w", "w_b_out", "w_out", "final_g"]
    return (loss, grad_x, *[grads[k] for k in order], *[deltas[k] for k in order],
            *[new_m[k] for k in order], *[new_v[k] for k in order])
```

```python
import functools

import jax
import jax.numpy as jnp
from jax import lax
from jax.experimental import pallas as pl
from jax.experimental.pallas import tpu as pltpu

F32 = jnp.float32
BF16 = jnp.bfloat16
MESH = pl.DeviceIdType.MESH

EPS = 1e-6
N_META = 16
N_SPLIT = 9
CONV_A = 31
CONV_B = 3
HALO_A = 32
HALO_B = 8
SHIFT_ROWS = 24
TM = 256
RB = 64
N_ROW_TILES_BIG = 8
ROW_BLOCK = 256
N_CHIPS = 4
VMEM_LIMIT = 56 * 1024 * 1024
VMEM_LIMIT_BIG = 62 * 1024 * 1024

ADAM_LR = 0.001
ADAM_B1 = 0.9
ADAM_B2 = 0.999
ADAM_EPS = 1e-08
ADAM_WD = 0.01
ADAM_STEP = 10

ROW_DWA = 0
ROW_DWB = 32
ROW_DCAB = 40
ROW_DLNG = 41
ROW_DLNB = 42
ROW_DBAO = 43
SM_ROWS = 48
ROW_DMETA = 48
ROW_DNG = 64
ROW_DFG = 65
ROW_LOSS = 66
AR_ROWS = 72


def _sigmoid(v):
    return 0.5 * jnp.tanh(0.5 * v) + 0.5


def _params(sem, **kw):
    return pltpu.CompilerParams(dimension_semantics=sem, vmem_limit_bytes=VMEM_LIMIT, **kw)


def _rows(rb):
    return pl.ds(pl.multiple_of(rb * RB, RB), RB)


def _mesh_pos():
    x, y, c = lax.axis_index("x"), lax.axis_index("y"), lax.axis_index("c")
    return x, y, c


def _half(ref, j, c):
    h = ref.shape[2] // 2
    return ref.at[:, j, pl.ds(c * h, h), :]


def _place_own(shard, pos, dtype, name):
    s, r, c = shard.shape
    rb = ROW_BLOCK if r % ROW_BLOCK == 0 else r

    def body(pos_ref, x_ref, o_ref):
        o_ref[...] = x_ref[...].astype(dtype)

    return pl.pallas_call(
        body, name=name,
        grid_spec=pltpu.PrefetchScalarGridSpec(
            num_scalar_prefetch=1, grid=(s, r // rb),
            in_specs=[pl.BlockSpec((None, rb, c), lambda si, b, pos_ref: (si, b, 0))],
            out_specs=pl.BlockSpec((None, None, rb, c), lambda si, b, pos_ref: (si, pos_ref[1], b, 0))),
        out_shape=jax.ShapeDtypeStruct((s, N_CHIPS, r, c), dtype),
        compiler_params=_params(("arbitrary",) * 2),
    )(pos, shard)


def _all_gather(bufs):
    n = len(bufs)

    def body(*refs):
        outs = refs[n:2 * n]
        send_sems, recv_sems = refs[2 * n:]
        x, y, c = _mesh_pos()
        me = 2 * x + y
        sibling = (x, y, 1 - c)
        chips = [(1 - x, y), (x, 1 - y), (1 - x, 1 - y)]

        def remote(a, k, piece_src, piece_dst, to):
            return pltpu.make_async_remote_copy(
                src_ref=piece_src, dst_ref=piece_dst, send_sem=send_sems.at[6 * a + k],
                recv_sem=recv_sems.at[6 * a + k], device_id=to, device_id_type=MESH)

        sends = []
        for a in range(n):
            mine = _half(outs[a], me, c)
            for k, (px, py) in enumerate(chips):
                sends.append(remote(a, k, mine, mine, (px, py, c)))
        for cp in sends:
            cp.start()
        for a in range(n):
            for k, (px, py) in enumerate(chips):
                piece = _half(outs[a], 2 * px + py, c)
                remote(a, k, piece, piece, (px, py, c)).wait_recv()
                fwd = remote(a, 3 + k, piece, piece, sibling)
                fwd.start()
                sends.append(fwd)
        for a in range(n):
            for k, (px, py) in enumerate(chips):
                piece = _half(outs[a], 2 * px + py, 1 - c)
                remote(a, 3 + k, piece, piece, sibling).wait_recv()
        for cp in sends:
            cp.wait_send()

    any_spec = pl.BlockSpec(memory_space=pl.ANY)
    return pl.pallas_call(
        body, name="ag_weights",
        in_specs=[any_spec] * n, out_specs=[any_spec] * n,
        out_shape=[jax.ShapeDtypeStruct(b.shape, b.dtype) for b in bufs],
        input_output_aliases={a: a for a in range(n)},
        scratch_shapes=[pltpu.SemaphoreType.DMA((6 * n,)), pltpu.SemaphoreType.DMA((6 * n,))],
    )(*bufs)


class _Exchange:
    def __init__(self, sends, recvs):
        self.sends, self.recvs = sends, recvs

    @staticmethod
    def _each(pairs, act):
        for cond, cp in pairs:
            if cond is None:
                act(cp)
            else:
                pl.when(cond)(functools.partial(act, cp))

    def start(self):
        self._each(self.sends, lambda cp: cp.start())

    def finish(self):
        self._each(self.recvs, lambda cp: cp.wait_recv())
        self._each(self.sends, lambda cp: cp.wait_send())


def _chip_exchange(part_ref, land_ref, send_sems, recv_sems, half=None):
    x, y, c = _mesh_pos()
    me = 2 * x + y
    sends, recvs = [], []
    for k, (px, py) in enumerate([(1 - x, y), (x, 1 - y), (1 - x, 1 - y)]):
        sems = dict(send_sem=send_sems.at[k], recv_sem=recv_sems.at[k], device_id=(px, py, c), device_id_type=MESH)
        slot = 2 * px + py if half is None else py
        sends.append((None if half is None else px == half, pltpu.make_async_remote_copy(
            src_ref=part_ref.at[:, slot], dst_ref=land_ref.at[:, me], **sems)))
        landed = land_ref.at[:, 2 * px + py]
        recvs.append((None if half is None else x == half,
                      pltpu.make_async_remote_copy(src_ref=landed, dst_ref=landed, **sems)))
    return _Exchange(sends, recvs)

def _sibling_swap(halves, small):
    n = len(halves)

    def body(*refs):
        ins, small_ref, outs, red_ref = refs[:n], refs[n], refs[n + 1:2 * n + 1], refs[2 * n + 1]
        send_sems, recv_sems = refs[2 * n + 2:2 * n + 4]
        reduce = _SmallAllReduce(small_ref, red_ref, *refs[2 * n + 4:])
        x, y, c = _mesh_pos()
        copies = [pltpu.make_async_remote_copy(
            src_ref=ins[a], dst_ref=outs[a], send_sem=send_sems.at[a], recv_sem=recv_sems.at[a],
            device_id=(x, y, 1 - c), device_id_type=MESH) for a in range(n)]
        reduce.start()
        for cp in copies:
            cp.start()
        reduce.between_chips()
        reduce.finish()
        for cp in copies:
            cp.wait()

    any_spec = pl.BlockSpec(memory_space=pl.ANY)
    vm = pl.BlockSpec(memory_space=pltpu.VMEM)
    outs = pl.pallas_call(
        body, name="rs_swap",
        in_specs=[any_spec] * n + [vm], out_specs=[any_spec] * n + [vm],
        out_shape=[jax.ShapeDtypeStruct(h.shape, h.dtype) for h in halves]
        + [jax.ShapeDtypeStruct(small.shape, F32)],
        scratch_shapes=[pltpu.SemaphoreType.DMA((n,)), pltpu.SemaphoreType.DMA((n,))]
        + _SmallAllReduce.scratch(*small.shape),
    )(*halves, small)
    return outs[:n], outs[n]


class _SmallAllReduce:
    def __init__(self, x_ref, out_ref, sib_ref, part_ref, peers_ref, send_sems, recv_sems):
        self.x_ref, self.out_ref, self.sib_ref, self.part_ref, self.peers_ref = x_ref, out_ref, sib_ref, part_ref, peers_ref
        x, y, c = _mesh_pos()
        self.me = 2 * x + y
        self.swap = pltpu.make_async_remote_copy(
            src_ref=x_ref, dst_ref=sib_ref, send_sem=send_sems.at[0], recv_sem=recv_sems.at[0],
            device_id=(x, y, 1 - c), device_id_type=MESH)
        self.sends, self.recvs = [], []
        for k, (px, py) in enumerate([(1 - x, y), (x, 1 - y), (1 - x, 1 - y)]):
            sems = dict(send_sem=send_sems.at[1 + k], recv_sem=recv_sems.at[1 + k],
                        device_id=(px, py, c), device_id_type=MESH)
            self.sends.append(pltpu.make_async_remote_copy(src_ref=part_ref, dst_ref=peers_ref.at[self.me], **sems))
            landed = peers_ref.at[2 * px + py]
            self.recvs.append(pltpu.make_async_remote_copy(src_ref=landed, dst_ref=landed, **sems))

    @staticmethod
    def scratch(rows, d):
        return [pltpu.VMEM((rows, d), F32), pltpu.VMEM((rows, d), F32), pltpu.VMEM((N_CHIPS, rows, d), F32),
                pltpu.SemaphoreType.DMA((4,)), pltpu.SemaphoreType.DMA((4,))]

    def start(self):
        self.swap.start()

    def between_chips(self):
        self.swap.wait()
        self.part_ref[...] = self.x_ref[...] + self.sib_ref[...]
        self.peers_ref[self.me] = self.part_ref[...]
        for cp in self.sends:
            cp.start()

    def finish(self):
        for cp in self.recvs:
            cp.wait_recv()
        for cp in self.sends:
            cp.wait_send()
        p = self.peers_ref
        self.out_ref[...] = ((p[0] + p[1]) + p[2]) + p[3]


def _sum_chips(parts, cw, pos, name):
    s, _, h, _ = parts[0][0].shape
    hb = min(h, ROW_BLOCK)
    widths = [own.shape[3] // cw for own, _ in parts]
    starts = [sum(widths[:a]) for a in range(len(parts))]

    def body(pos_ref, *refs):
        out_ref = refs[-1]
        n = pl.program_id(2)
        total = None
        for a in range(len(parts)):
            own, l1, l2, l3 = refs[4 * a:4 * a + 4]
            val = ((own[...] + l1[...].astype(F32)) + l2[...].astype(F32)) + l3[...].astype(F32)
            total = val if total is None else jnp.where(n >= starts[a], val, total)
        out_ref[...] = total

    def slot(a, k):
        col = lambda n: jnp.clip(n - starts[a], 0, widths[a] - 1)
        return pl.BlockSpec((None, None, hb, cw),
                            lambda si, b, n, pos_ref: (si, (pos_ref[1] + k) % N_CHIPS, b, col(n)))

    operands, specs = [], []
    for a, (own, landed) in enumerate(parts):
        operands += [own, landed, landed, landed]
        specs += [slot(a, 0), slot(a, 1), slot(a, 2), slot(a, 3)]
    return pl.pallas_call(
        body, name=name,
        grid_spec=pltpu.PrefetchScalarGridSpec(
            num_scalar_prefetch=1, grid=(s, h // hb, sum(widths)), in_specs=specs,
            out_specs=pl.BlockSpec((None, hb, cw), lambda si, b, n, pos_ref: (si, b, n))),
        out_shape=jax.ShapeDtypeStruct((s, h, sum(widths) * cw), F32),
        compiler_params=_params(("arbitrary",) * 3),
    )(pos, *operands)


def _adamw(w, g, m, v):
    m = ADAM_B1 * m + (1.0 - ADAM_B1) * g
    v = ADAM_B2 * v + (1.0 - ADAM_B2) * (g * g)
    m_hat = m / (1.0 - ADAM_B1 ** ADAM_STEP)
    v_hat = v / (1.0 - ADAM_B2 ** ADAM_STEP)
    delta = -ADAM_LR * (m_hat / (jnp.sqrt(v_hat) + ADAM_EPS) + ADAM_WD * w)
    return delta, m, v


def _adam_halves(ws, ms, vs, g_own, g_recv, pos, name):
    n = len(ws)
    _, r, c = ws[0].shape
    h = r // 2
    rb = min(h, ROW_BLOCK)
    nb = h // rb

    def body(pos_ref, *refs):
        w_refs, m_refs, v_refs = refs[:n], refs[n:2 * n], refs[2 * n:3 * n]
        go_ref, gr_ref = refs[3 * n:3 * n + 2]
        outs = refs[3 * n + 2:]
        mine = pl.program_id(0) == pos_ref[0]
        for a in range(n):
            g = jnp.where(mine, go_ref[a], gr_ref[a])
            delta, m, v = _adamw(w_refs[a][...], g, m_refs[a][...], v_refs[a][...])
            outs[4 * a][...], outs[4 * a + 1][...], outs[4 * a + 2][...], outs[4 * a + 3][...] = g, delta, m, v

    spec_w = pl.BlockSpec((None, rb, c), lambda hf, b, pos_ref: (0, hf * nb + b, 0))
    spec_g = pl.BlockSpec((n, rb, c), lambda hf, b, pos_ref: (0, b, 0))
    return pl.pallas_call(
        body, name=name,
        grid_spec=pltpu.PrefetchScalarGridSpec(
            num_scalar_prefetch=1, grid=(2, nb), in_specs=[spec_w] * (3 * n) + [spec_g] * 2,
            out_specs=[spec_w] * (4 * n)),
        out_shape=[jax.ShapeDtypeStruct((1, r, c), F32)] * (4 * n),
        compiler_params=_params(("arbitrary",) * 2),
    )(pos, *ws, *ms, *vs, g_own, g_recv)


def _adam_small(items):
    n = len(items)

    def body(*refs):
        ins, outs = refs[:4 * n], refs[4 * n:]
        for a in range(n):
            w_ref, g_ref, m_ref, v_ref = ins[4 * a:4 * a + 4]
            d, m, v = _adamw(w_ref[...], g_ref[...], m_ref[...], v_ref[...])
            outs[3 * a][...] = d
            outs[3 * a + 1][...] = m
            outs[3 * a + 2][...] = v

    vm = pl.BlockSpec(memory_space=pltpu.VMEM)
    flat = [t for it in items for t in it]
    outs = pl.pallas_call(
        body, name="adam_small", in_specs=[vm] * (4 * n), out_specs=[vm] * (3 * n),
        out_shape=[jax.ShapeDtypeStruct(it[0].shape, F32) for it in items for _ in range(3)],
    )(*flat)
    return [tuple(outs[3 * a:3 * a + 3]) for a in range(n)]


def _shard_of_step(js, me):
    flip = jnp.where(js == 1, 2, jnp.where(js == 2, 1, jnp.where(js == 3, 3, 0)))
    return lax.bitwise_xor(me, flip)


def _proj_fwd(s_pad, norm_g, bufs, pos):
    tp, d = s_pad.shape
    _, nsh, _, sw = bufs[0].shape
    tmb = tp // N_ROW_TILES_BIG
    n = len(bufs)

    def body(pos_ref, s_ref, g_ref, *refs):
        proj_ref = refs[n]
        gbufs = refs[n + 1:2 * n + 1]
        wbuf, wsems, send_sems, recv_sems = refs[2 * n + 1:]
        x, y, c = _mesh_pos()
        me = 2 * x + y
        sibling = (x, y, 1 - c)
        chips = [(1 - x, y), (x, 1 - y), (1 - x, 1 - y)]
        js, i = pl.program_id(0), pl.program_id(1)

        def remote(a, k, piece, to):
            return pltpu.make_async_remote_copy(
                src_ref=piece, dst_ref=piece, send_sem=send_sems.at[6 * a + k],
                recv_sem=recv_sems.at[6 * a + k], device_id=to, device_id_type=MESH)

        def fetch(chip, step):
            return pltpu.make_async_copy(gbufs[0].at[0, chip], wbuf.at[step % 2], wsems.at[step % 2])

        chip_ids = [2 * px + py for px, py in chips]
        relayed_chip = jnp.where(c == 0, chip_ids[0], chip_ids[1])
        relay_to = (jnp.where(c == 0, x, 1 - x), jnp.where(c == 0, 1 - y, y), c)

        def own_piece(a, k):
            return remote(a, k, _half(gbufs[a], me, c), (*chips[k], c))

        def relay(a):
            return remote(a, 2, _half(gbufs[a], relayed_chip, c), relay_to)

        def to_sibling(a, k, core):
            return remote(a, 3 + k, _half(gbufs[a], chip_ids[k], core), sibling)

        def landed(a, k):
            return remote(a, k, _half(gbufs[a], chip_ids[k], c), (*chips[k], c))

        def take_neighbours(a):
            landed(a, 0).wait_recv()
            landed(a, 1).wait_recv()
            relay(a).start()
            for k in range(2):
                to_sibling(a, k, c).start()
            for k in range(2):
                to_sibling(a, k, 1 - c).wait_recv()

        def take_diagonal(a):
            landed(a, 2).wait_recv()
            to_sibling(a, 2, c).start()
            to_sibling(a, 2, 1 - c).wait_recv()

        @pl.when((js == 0) & (i == 0))
        def _():
            for a in range(n):
                for k in range(2):
                    own_piece(a, k).start()
            fetch(me, 0).start()
            fetch(me, 0).wait()

        @pl.when((js == 1) & (i == 0))
        def _():
            take_neighbours(0)
            fetch(chip_ids[0], 1).start()
            fetch(chip_ids[0], 1).wait()

        @pl.when((js == 1) & (i == N_ROW_TILES_BIG - 2))
        def _():
            fetch(chip_ids[1], 2).start()

        @pl.when((js == 2) & (i == 0))
        def _():
            fetch(chip_ids[1], 2).wait()

        @pl.when((js == 2) & (i == 1))
        def _():
            for a in range(1, n):
                take_neighbours(a)

        @pl.when((js == 2) & (i == N_ROW_TILES_BIG - 2))
        def _():
            take_diagonal(0)
            fetch(chip_ids[2], 3).start()

        @pl.when((js == 3) & (i == 0))
        def _():
            fetch(chip_ids[2], 3).wait()

        s = s_ref[...]
        r = lax.rsqrt(jnp.mean(s * s, axis=-1, keepdims=True) + EPS)
        h = (s * r * g_ref[...]).astype(BF16)
        proj_ref[...] = jnp.dot(h, wbuf[js % 2], preferred_element_type=F32).astype(BF16)

        @pl.when((js == nsh - 1) & (i == N_ROW_TILES_BIG - 1))
        def _():
            for a in range(1, n):
                take_diagonal(a)
            for a in range(n):
                for k in range(2):
                    own_piece(a, k).wait_send()
                relay(a).wait_send()
                for k in range(3):
                    to_sibling(a, k, c).wait_send()

    any_spec = pl.BlockSpec(memory_space=pl.ANY)
    outs = pl.pallas_call(
        body, name="f1_proj",
        grid_spec=pltpu.PrefetchScalarGridSpec(
            num_scalar_prefetch=1, grid=(nsh, N_ROW_TILES_BIG),
            in_specs=[pl.BlockSpec((tmb, d), lambda js, i, pos_ref: (i, 0)),
                      pl.BlockSpec((1, d), lambda js, i, pos_ref: (0, 0))] + [any_spec] * n,
            out_specs=[pl.BlockSpec((tmb, sw), lambda js, i, pos_ref: (i, _shard_of_step(js, pos_ref[1])))]
            + [any_spec] * n,
            scratch_shapes=[pltpu.VMEM((2, d, sw), BF16), pltpu.SemaphoreType.DMA((2,)),
                            pltpu.SemaphoreType.DMA((6 * n,)), pltpu.SemaphoreType.DMA((6 * n,))]),
        out_shape=[jax.ShapeDtypeStruct((tp, nsh * sw), BF16)]
        + [jax.ShapeDtypeStruct(b.shape, b.dtype) for b in bufs],
        input_output_aliases={3 + a: 1 + a for a in range(n)},
        compiler_params=_params(("arbitrary", "arbitrary")),
    )(pos, s_pad, norm_g, *bufs)
    return outs[0], outs[1:]


def _dh_bwd(dproj, wg_in, s_pad, ds2, norm_g, part, land, half):
    tp, d = s_pad.shape
    _, nsh, _, sw = wg_in.shape
    tmb = tp // N_ROW_TILES_BIG

    def body(dp_ref, w_hbm, s_ref, ds2_ref, g_ref, part_ref, _, ds_ref, dng_ref, land_ref, wbuf, gacc,
             wsem, send_sems, recv_sems):
        exchange = _chip_exchange(part_ref, land_ref, send_sems, recv_sems, half)
        i = pl.program_id(0)

        @pl.when(i == 0)
        def _():
            exchange.start()
            gacc[...] = jnp.zeros_like(gacc)
            whole = pltpu.make_async_copy(w_hbm.at[0], wbuf, wsem)
            whole.start()
            whole.wait()

        dh = None
        for j in range(nsh):
            part = lax.dot_general(dp_ref[:, j * sw:(j + 1) * sw], wbuf[j], (((1,), (1,)), ((), ())),
                                   preferred_element_type=F32)
            dh = part if dh is None else dh + part
        s = s_ref[...]
        r = lax.rsqrt(jnp.mean(s * s, axis=-1, keepdims=True) + EPS)
        gacc[...] += (dh * s * r).reshape(tmb // 8, 8, d).sum(axis=0)
        t = dh * g_ref[...]
        ds_ref[...] = ds2_ref[...] + r * t - s * (r * r * r) * jnp.mean(t * s, axis=-1, keepdims=True)

        @pl.when(i == N_ROW_TILES_BIG - 1)
        def _():
            dng_ref[...] = jnp.broadcast_to(jnp.sum(gacc[...], axis=0, keepdims=True), (8, d))
            exchange.finish()

    any_spec = pl.BlockSpec(memory_space=pl.ANY)
    return pl.pallas_call(
        body, name="b2_dh", grid=(N_ROW_TILES_BIG,),
        in_specs=[pl.BlockSpec((tmb, nsh * sw), lambda i: (i, 0)), any_spec,
                  pl.BlockSpec((tmb, d), lambda i: (i, 0)),
                  pl.BlockSpec((tmb, d), lambda i: (i, 0)),
                  pl.BlockSpec((1, d), lambda i: (0, 0)), any_spec, any_spec],
        out_specs=[pl.BlockSpec((tmb, d), lambda i: (i, 0)),
                   pl.BlockSpec((8, d), lambda i: (0, 0)), any_spec],
        out_shape=[jax.ShapeDtypeStruct((tp, d), F32), jax.ShapeDtypeStruct((8, d), F32),
                   jax.ShapeDtypeStruct(land.shape, land.dtype)],
        input_output_aliases={6: 2},
        scratch_shapes=[pltpu.VMEM((nsh, d, sw), BF16), pltpu.VMEM((8, d), F32), pltpu.SemaphoreType.DMA,
                        pltpu.SemaphoreType.DMA((3,)), pltpu.SemaphoreType.DMA((3,))],
        compiler_params=pltpu.CompilerParams(dimension_semantics=("arbitrary",),
                                             vmem_limit_bytes=VMEM_LIMIT_BIG),
    )(dproj, wg_in, s_pad, ds2, norm_g, part, land)


def _col_block(width, cap):
    return max(b for b in range(128, cap + 1, 128) if width % b == 0)


def _dw_reduced(lhs_t, rhs, cw, nblk, operands, groups, out_dims, out_block, out_index, carried, name):
    na, d, tp = lhs_t.shape
    rg = d // groups
    hh = rg // 2
    nc = len(carried)

    def body(*refs):
        l_ref, r_ref = refs[:2]
        part_refs = refs[2:2 + nc]
        p32_ref, pbf_ref = refs[2 + nc:4 + nc]
        land_refs = refs[4 + nc:4 + 2 * nc]
        res, rbuf, send_sems, recv_sems = refs[4 + 2 * nc:8 + 2 * nc]
        xsems = refs[8 + 2 * nc:]
        exchanges = [_chip_exchange(part_refs[e], land_refs[e], xsems[2 * e], xsems[2 * e + 1]) for e in range(nc)]
        exchange = _Exchange([s for ex in exchanges for s in ex.sends], [r for ex in exchanges for r in ex.recvs])
        x, y, c = _mesh_pos()
        t = pl.program_id(0)
        u = jnp.maximum(t - 1, 0)

        def to_sibling(blk):
            return pltpu.make_async_remote_copy(
                src_ref=res.at[blk % 2, :, pl.ds((1 - c) * hh, hh), :], dst_ref=rbuf.at[blk % 2],
                send_sem=send_sems.at[blk], recv_sem=recv_sems.at[blk],
                device_id=(x, y, 1 - c), device_id_type=MESH)

        @pl.when(t == 0)
        def _():
            exchange.start()

        @pl.when(t < nblk)
        def _():
            res[t % 2] = jnp.dot(l_ref[...], r_ref[...], preferred_element_type=F32).reshape(groups, rg, cw)

        @pl.when(t >= 1)
        def _():
            to_sibling(u).wait_recv()
            p = res[u % 2, :, pl.ds(c * hh, hh), :] + rbuf[u % 2]
            p32_ref[...] = p.reshape(p32_ref.shape)
            pbf_ref[...] = p.reshape(pbf_ref.shape).astype(BF16)

        @pl.when(t < nblk)
        def _():
            to_sibling(t).start()

        @pl.when(t >= 1)
        def _():
            to_sibling(u).wait_send()

        @pl.when(t == nblk)
        def _():
            exchange.finish()

    any_spec = pl.BlockSpec(memory_space=pl.ANY)
    last = nblk - 1
    out_spec = pl.BlockSpec(out_block, lambda t: out_index(jnp.maximum(t - 1, 0)))
    outs = pl.pallas_call(
        body, name=name, grid=(nblk + 1,),
        in_specs=[pl.BlockSpec((None, d, tp), lambda t: (operands(jnp.minimum(t, last))[0], 0, 0)),
                  pl.BlockSpec((None, tp, cw), lambda t: (operands(jnp.minimum(t, last))[0], 0,
                                                          operands(jnp.minimum(t, last))[1]))]
        + [any_spec] * nc,
        out_specs=[out_spec, out_spec] + [any_spec] * nc,
        out_shape=[jax.ShapeDtypeStruct(out_dims, F32), jax.ShapeDtypeStruct(out_dims, BF16)]
        + [jax.ShapeDtypeStruct(e.shape, e.dtype) for e in carried],
        scratch_shapes=[pltpu.VMEM((2, groups, rg, cw), F32), pltpu.VMEM((2, groups, hh, cw), F32),
                        pltpu.SemaphoreType.DMA((nblk,)), pltpu.SemaphoreType.DMA((nblk,))]
        + [pltpu.SemaphoreType.DMA((3,)), pltpu.SemaphoreType.DMA((3,))] * nc,
        compiler_params=_params(("arbitrary",)),
    )(lhs_t, rhs, *carried)
    return outs[0], outs[1], outs[2:]


def _conv_a_taps(first_lag, last_lag):
    out = []
    for r in range(8):
        taps = [(q, 8 * q + r) for q in range(5) if first_lag <= 8 * q + r <= last_lag]
        if taps:
            out.append((r, taps))
    return out


def _mix_fwd(s_pad, proj, target, w3, wa, wb, conv_a_b, ln_g, ln_b, b_a_out, final_g, norm_g):
    tp, d = s_pad.shape
    nt = tp // TM
    nrb = TM // RB
    shl = TM + SHIFT_ROWS

    def body(s_ref, proj_ref, tgt_ref, w3_ref, wa_ref, wb_ref, cab_ref, lng_ref, lnb_ref, bao_ref, fg_ref, ng_ref,
             ca_ref, cb_ref, ya_ref, yb_ref, abmt_ref, ds2_ref, ht_ref, loss_ref, dfg_ref,
             abm_ref, ext_a, ext_b, sh, s2_s, lacc, gacc):
        i = pl.program_id(0)

        def split(k, rows):
            return proj_ref[rows, k * d:(k + 1) * d].astype(F32)

        s_in = s_ref[...]
        h = s_in * lax.rsqrt(jnp.mean(s_in * s_in, axis=-1, keepdims=True) + EPS) * ng_ref[...]
        ht_ref[...] = h.T.astype(BF16)

        @pl.when(i == 0)
        def _():
            ext_a[0:HALO_A, :] = jnp.zeros((HALO_A, d), F32)
            ext_b[0:HALO_B, :] = jnp.zeros((HALO_B, d), F32)
            lacc[...] = jnp.zeros_like(lacc)
            gacc[...] = jnp.zeros_like(gacc)

        def conv_in(rb, carry):
            rows = _rows(rb)
            ua0 = split(0, rows) * _sigmoid(split(1, rows))
            ext_a[pl.ds(pl.multiple_of(HALO_A + rb * RB, 8), RB), :] = ua0
            ext_b[pl.ds(pl.multiple_of(HALO_B + rb * RB, 8), RB), :] = split(4, rows) * split(5, rows)
            ca_ref[rows, :] = jnp.broadcast_to(cab_ref[...], (RB, d))
            return carry
        lax.fori_loop(0, nrb, conv_in, 0)

        for r, taps in _conv_a_taps(HALO_A - CONV_A + 1, HALO_A):
            if r == 0:
                src = ext_a
            else:
                sh[...] = ext_a[r:r + shl, :]
                src = sh

            def conv_acc(rb, carry, src=src, taps=taps):
                rows = _rows(rb)
                acc = ca_ref[rows, :]
                for q, lag in taps:
                    k = lag - (HALO_A - CONV_A + 1)
                    acc = acc + src[pl.ds(pl.multiple_of(rb * RB + 8 * q, 8), RB), :] * wa_ref[k:k + 1, :]
                ca_ref[rows, :] = acc
                return carry
            lax.fori_loop(0, nrb, conv_acc, 0)
        ext_a[0:HALO_A, :] = ext_a[TM:TM + HALO_A, :]

        cb_ref[...] = ext_b[HALO_B:HALO_B + TM, :] * wb_ref[2:3, :]
        for k in range(CONV_B - 1):
            off = HALO_B - CONV_B + 1 + k
            sh[0:TM, :] = ext_b[off:off + TM, :]
            cb_ref[...] += sh[0:TM, :] * wb_ref[k:k + 1, :]
        ext_b[0:HALO_B, :] = ext_b[TM:TM + HALO_B, :]

        def branches(rb, carry):
            rows = _rows(rb)
            ca = ca_ref[rows, :]
            mu = jnp.mean(ca, axis=-1, keepdims=True)
            xc = ca - mu
            rstd = lax.rsqrt(jnp.mean(xc * xc, axis=-1, keepdims=True) + EPS)
            ln = xc * rstd * lng_ref[...] + lnb_ref[...]
            ua = ln * _sigmoid(ln)
            a_z = split(2, rows)
            abm_ref[0, rows, :] = (ua * (a_z * _sigmoid(a_z))).astype(BF16)
            b_z = split(6, rows)
            ub = split(3, rows) * cb_ref[rows, :]
            abm_ref[1, rows, :] = (ub * (b_z * _sigmoid(b_z))).astype(BF16)
            return carry
        lax.fori_loop(0, nrb, branches, 0)

        ya_ref[...] = jnp.dot(abm_ref[0], w3_ref[0], preferred_element_type=F32) + bao_ref[...]
        yb_ref[...] = jnp.dot(abm_ref[1], w3_ref[1], preferred_element_type=F32)

        def merge(rb, carry):
            rows = _rows(rb)
            m = _sigmoid(split(7, rows)) * ya_ref[rows, :] + _sigmoid(split(8, rows)) * yb_ref[rows, :]
            abm_ref[2, rows, :] = m.astype(BF16)
            return carry
        lax.fori_loop(0, nrb, merge, 0)

        s2_s[...] = s_ref[...] + jnp.dot(abm_ref[2], w3_ref[2], preferred_element_type=F32)
        for k in range(3):
            abmt_ref[k] = abm_ref[k].astype(F32).T.astype(BF16)
        live = (i > 0).astype(F32)

        def head(rb, carry):
            rows = _rows(rb)
            s2 = s2_s[rows, :]
            r2 = lax.rsqrt(jnp.mean(s2 * s2, axis=-1, keepdims=True) + EPS)
            diff = (s2 * r2 * fg_ref[...] - tgt_ref[rows, :]) * live
            lacc[...] += diff * diff
            dy = diff * (1.0 / d)
            gacc[...] += (dy * s2 * r2).reshape(RB // 8, 8, d).sum(axis=0)
            t = dy * fg_ref[...]
            ds2_ref[rows, :] = r2 * t - s2 * (r2 * r2 * r2) * jnp.mean(t * s2, axis=-1, keepdims=True)
            return carry
        lax.fori_loop(0, nrb, head, 0)

        @pl.when(i == nt - 1)
        def _():
            loss_ref[...] = jnp.broadcast_to(0.5 * jnp.sum(lacc[...]) * (1.0 / d), (8, 128))
            dfg_ref[...] = jnp.broadcast_to(jnp.sum(gacc[...], axis=0, keepdims=True), (8, d))

    row_f32 = pl.BlockSpec((TM, d), lambda i: (i, 0))
    const = lambda shape: pl.BlockSpec(shape, lambda i: (0,) * len(shape))
    return pl.pallas_call(
        body, name="f2_mix", grid=(nt,),
        in_specs=[row_f32,
                  pl.BlockSpec((TM, N_SPLIT * d), lambda i: (i, 0)),
                  pl.BlockSpec((TM, d), lambda i: (jnp.maximum(i - 1, 0), 0)),
                  const((3, d, d)), const(wa.shape), const(wb.shape)] + [const((1, d))] * 6,
        out_specs=[row_f32, row_f32, row_f32, row_f32,
                   pl.BlockSpec((3, d, TM), lambda i: (0, 0, i)),
                   row_f32, pl.BlockSpec((d, TM), lambda i: (0, i)), const((8, 128)), const((8, d))],
        out_shape=[jax.ShapeDtypeStruct((tp, d), F32)] * 4
        + [jax.ShapeDtypeStruct((3, d, tp), BF16), jax.ShapeDtypeStruct((tp, d), F32),
           jax.ShapeDtypeStruct((d, tp), BF16),
           jax.ShapeDtypeStruct((8, 128), F32), jax.ShapeDtypeStruct((8, d), F32)],
        scratch_shapes=[pltpu.VMEM((3, TM, d), BF16),
                        pltpu.VMEM((HALO_A + TM, d), F32), pltpu.VMEM((HALO_B + TM, d), F32),
                        pltpu.VMEM((shl, d), F32), pltpu.VMEM((TM, d), F32),
                        pltpu.VMEM((RB, d), F32), pltpu.VMEM((8, d), F32)],
        compiler_params=_params(("arbitrary",)),
    )(s_pad, proj, target, w3, wa, wb, conv_a_b, ln_g, ln_b, b_a_out, final_g, norm_g)


def _mix_bwd(ds2, proj, ca, cb, ya, yb, w3, wa, wb, ln_g, ln_b):
    tp, d = ds2.shape
    nt = tp // TM
    nrb = TM // RB
    shl = TM + SHIFT_ROWS
    nt_dims = (((1,), (1,)), ((), ()))

    def body(ds2_ref, proj_ref, ca_ref, cb_ref, ya_ref, yb_ref, w3_ref, wa_ref, wb_ref, lng_ref, lnb_ref,
             dproj_ref, d3_ref, sm_ref, ext_d, ext_e, sh, dm_s, dpa_s, dpb_s, dua0_s, acc):
        step = pl.program_id(0)

        def split(k, rows):
            return proj_ref[rows, k * d:(k + 1) * d].astype(F32)

        def put(k, rows, val):
            dproj_ref[rows, k * d:(k + 1) * d] = val.astype(BF16)

        def accum(row, val):
            acc[row] += val.reshape(RB // 8, 8, d).sum(axis=0)

        @pl.when(step == 0)
        def _():
            ext_d[TM:TM + HALO_A, :] = jnp.zeros((HALO_A, d), F32)
            ext_e[TM:TM + HALO_B, :] = jnp.zeros((HALO_B, d), F32)
            acc[...] = jnp.zeros_like(acc)

        d3_ref[2] = ds2_ref[...].astype(BF16)
        dm_s[...] = lax.dot_general(d3_ref[2], w3_ref[2], nt_dims, preferred_element_type=F32)

        def gates(rb, carry):
            rows = _rows(rb)
            dm = dm_s[rows, :]
            sa = _sigmoid(split(7, rows))
            sb = _sigmoid(split(8, rows))
            ya_v = ya_ref[rows, :]
            yb_v = yb_ref[rows, :]
            put(7, rows, dm * ya_v * sa * (1.0 - sa))
            put(8, rows, dm * yb_v * sb * (1.0 - sb))
            dya = dm * sa
            accum(ROW_DBAO, dya)
            d3_ref[0, rows, :] = dya.astype(BF16)
            d3_ref[1, rows, :] = (dm * sb).astype(BF16)
            return carry
        lax.fori_loop(0, nrb, gates, 0)

        dpa_s[...] = lax.dot_general(d3_ref[0], w3_ref[0], nt_dims, preferred_element_type=F32)
        dpb_s[...] = lax.dot_general(d3_ref[1], w3_ref[1], nt_dims, preferred_element_type=F32)

        def branches(rb, carry):
            rows = _rows(rb)
            ca_v = ca_ref[rows, :]
            mu = jnp.mean(ca_v, axis=-1, keepdims=True)
            xc = ca_v - mu
            rstd = lax.rsqrt(jnp.mean(xc * xc, axis=-1, keepdims=True) + EPS)
            xhat = xc * rstd
            ln = xhat * lng_ref[...] + lnb_ref[...]
            sl = _sigmoid(ln)
            ua = ln * sl
            a_z = split(2, rows)
            sz = _sigmoid(a_z)
            dpa = dpa_s[rows, :]
            put(2, rows, dpa * ua * (sz * (1.0 + a_z * (1.0 - sz))))
            dln = dpa * (a_z * sz) * (sl * (1.0 + ln * (1.0 - sl)))
            accum(ROW_DLNG, dln * xhat)
            accum(ROW_DLNB, dln)
            dxh = dln * lng_ref[...]
            dca = rstd * (dxh - jnp.mean(dxh, axis=-1, keepdims=True)
                          - xhat * jnp.mean(dxh * xhat, axis=-1, keepdims=True))
            accum(ROW_DCAB, dca)
            ext_d[rows, :] = dca
            dua0_s[rows, :] = jnp.zeros((RB, d), F32)
            dm_s[rows, :] = split(0, rows) * _sigmoid(split(1, rows))
            b_z = split(6, rows)
            szb = _sigmoid(b_z)
            dpb = dpb_s[rows, :]
            b_b = split(3, rows)
            cb_v = cb_ref[rows, :]
            put(6, rows, dpb * (b_b * cb_v) * (szb * (1.0 + b_z * (1.0 - szb))))
            dub = dpb * (b_z * szb)
            put(3, rows, dub * cb_v)
            ext_e[rows, :] = dub * b_b
            return carry
        lax.fori_loop(0, nrb, branches, 0)

        for r, taps in _conv_a_taps(0, CONV_A - 1):
            if r == 0:
                src = ext_d
            else:
                sh[...] = ext_d[r:r + shl, :]
                src = sh

            def conv_t(rb, carry, src=src, taps=taps):
                rows = _rows(rb)
                ua0 = dm_s[rows, :]
                dua0 = dua0_s[rows, :]
                for q, lag in taps:
                    k = CONV_A - 1 - lag
                    slab = src[pl.ds(pl.multiple_of(rb * RB + 8 * q, 8), RB), :]
                    dua0 = dua0 + slab * wa_ref[k:k + 1, :]
                    accum(ROW_DWA + k, slab * ua0)
                dua0_s[rows, :] = dua0
                return carry
            lax.fori_loop(0, nrb, conv_t, 0)
        ext_d[TM:TM + HALO_A, :] = ext_d[0:HALO_A, :]

        dpb_s[...] = ext_e[0:TM, :] * wb_ref[CONV_B - 1:CONV_B, :]
        for lag in range(CONV_B):
            k = CONV_B - 1 - lag
            if lag > 0:
                sh[0:TM, :] = ext_e[lag:lag + TM, :]
                dpb_s[...] += sh[0:TM, :] * wb_ref[k:k + 1, :]
            src = ext_e if lag == 0 else sh

            def conv_b_w(rb, carry, src=src, k=k):
                rows = _rows(rb)
                accum(ROW_DWB + k, src[rows, :] * (split(4, rows) * split(5, rows)))
                return carry
            lax.fori_loop(0, nrb, conv_b_w, 0)
        ext_e[TM:TM + HALO_B, :] = ext_e[0:HALO_B, :]

        def inputs(rb, carry):
            rows = _rows(rb)
            dua0 = dua0_s[rows, :]
            a_val = split(0, rows)
            sg = _sigmoid(split(1, rows))
            put(0, rows, dua0 * sg)
            put(1, rows, dua0 * a_val * sg * (1.0 - sg))
            dcbin = dpb_s[rows, :]
            put(4, rows, dcbin * split(5, rows))
            put(5, rows, dcbin * split(4, rows))
            return carry
        lax.fori_loop(0, nrb, inputs, 0)

        @pl.when(step == nt - 1)
        def _():
            for row in range(SM_ROWS):
                sm_ref[row:row + 1, :] = jnp.sum(acc[row], axis=0, keepdims=True)

    rev = lambda i: (nt - 1 - i, 0)
    row_f32 = pl.BlockSpec((TM, d), rev)
    const = lambda shape: pl.BlockSpec(shape, lambda i: (0,) * len(shape))
    return pl.pallas_call(
        body, name="b1_mix", grid=(nt,),
        in_specs=[row_f32, pl.BlockSpec((TM, N_SPLIT * d), rev), row_f32, row_f32, row_f32, row_f32,
                  const((3, d, d)), const(wa.shape), const(wb.shape), const((1, d)), const((1, d))],
        out_specs=[pl.BlockSpec((TM, N_SPLIT * d), rev),
                   pl.BlockSpec((3, TM, d), lambda i: (0, nt - 1 - i, 0)),
                   const((SM_ROWS, d))],
        out_shape=[jax.ShapeDtypeStruct((tp, N_SPLIT * d), BF16), jax.ShapeDtypeStruct((3, tp, d), BF16),
                   jax.ShapeDtypeStruct((SM_ROWS, d), F32)],
        scratch_shapes=[pltpu.VMEM((TM + HALO_A, d), F32), pltpu.VMEM((TM + HALO_B, d), F32),
                        pltpu.VMEM((shl, d), F32), pltpu.VMEM((TM, d), F32), pltpu.VMEM((TM, d), F32),
                        pltpu.VMEM((TM, d), F32), pltpu.VMEM((TM, d), F32),
                        pltpu.VMEM((SM_ROWS, 8, d), F32)],
        compiler_params=_params(("arbitrary",)),
    )(ds2, proj, ca, cb, ya, yb, w3, wa, wb, ln_g, ln_b)


def kernel(x, meta_tokens, norm_g, w_in, conv_a_w, conv_a_b, ln_a_g, ln_a_b, w_a_out, b_a_out, conv_b_w, w_b_out, w_out, final_g, loss_target, m_meta_tokens, m_norm_g, m_w_in, m_conv_a_w, m_conv_a_b, m_ln_a_g, m_ln_a_b, m_w_a_out, m_b_a_out, m_conv_b_w, m_w_b_out, m_w_out, m_final_g, v_meta_tokens, v_norm_g, v_w_in, v_conv_a_w, v_conv_a_b, v_ln_a_g, v_ln_a_b, v_w_a_out, v_b_a_out, v_conv_b_w, v_w_b_out, v_w_out, v_final_g):
    seq, d = x.shape[1], x.shape[2]
    dc = meta_tokens.shape[1]
    sw = w_in.shape[2]
    rsh = w_a_out.shape[1]
    xi, yi, ci = _mesh_pos()
    me = 2 * xi + yi
    pos = jnp.stack([ci, me]).astype(jnp.int32)

    conv_rows = HALO_A + HALO_B + 8
    convs = jnp.concatenate([
        jnp.pad(conv_a_w[0], ((0, HALO_A - CONV_A), (0, 0))),
        jnp.pad(conv_b_w[0], ((0, HALO_B - CONV_B), (0, 0))), jnp.zeros((8, dc), F32)], axis=0)[None]
    w3_own = jnp.stack([w_a_out[0], w_b_out[0], w_out[0]])
    (metag,) = _all_gather([_place_own(meta_tokens[None], pos, F32, "place_meta")])
    meta_full = jnp.transpose(metag[0], (1, 0, 2)).reshape(N_META, N_CHIPS * dc)
    fg2 = final_g.reshape(1, d)

    first_tile = jnp.concatenate([jnp.zeros((TM - N_META, d), F32), meta_full], axis=0)
    s_pad = jnp.concatenate([first_tile, x[0]], axis=0)

    proj, (wg_in, wg3, convg) = _proj_fwd(s_pad, norm_g, [_place_own(w_in, pos, BF16, "place_in"),
                                                          _place_own(w3_own, pos, BF16, "place_sq"),
                                                          _place_own(convs, pos, F32, "place_conv")], pos)
    w3 = wg3.reshape(3, N_CHIPS * rsh, d)
    convg = jnp.transpose(convg[0], (1, 0, 2)).reshape(conv_rows, N_CHIPS * dc)
    wa_full = convg[0:HALO_A]
    wb_full = convg[HALO_A:HALO_A + HALO_B]
    ca, cb, ya, yb, abm_t, ds2, h_t, loss8, dfg8 = _mix_fwd(
        s_pad, proj, loss_target[0], w3, wa_full, wb_full, conv_a_b, ln_a_g, ln_a_b, b_a_out, fg2, norm_g)
    dproj, d3, sm = _mix_bwd(ds2, proj, ca, cb, ya, yb, w3, wa_full, wb_full, ln_a_g, ln_a_b)
    cw_sq = _col_block(d, 512)
    per_sq = d // cw_sq
    p32_sq, pbf_sq, _ = _dw_reduced(
        abm_t, d3, cw_sq, 3 * per_sq, lambda t: (t // per_sq, t % per_sq), N_CHIPS,
        (3, N_CHIPS, rsh // 2, d), (None, N_CHIPS, rsh // 2, cw_sq),
        lambda u: (u // per_sq, 0, 0, u % per_sq), [], "dw_square")
    cw_in = _col_block(sw, 768)
    ncol = sw // cw_in
    assert ncol >= 2
    in_block = (None, None, d // 2, cw_in)
    p32_a, pbf_a, _ = _dw_reduced(
        h_t[None], dproj[None], cw_in, N_CHIPS, lambda t: (0, t * ncol), 1,
        (1, N_CHIPS, d // 2, cw_in), in_block, lambda u: (0, u, 0, 0), [], "dw_in_a")
    p32_b, pbf_b, (l_sq, l_a) = _dw_reduced(
        h_t[None], dproj[None], cw_in, N_CHIPS * (ncol - 1),
        lambda t: (0, (t % N_CHIPS) * ncol + 1 + t // N_CHIPS), 1,
        (1, N_CHIPS, d // 2, sw - cw_in), in_block, lambda u: (0, u % N_CHIPS, 0, u // N_CHIPS),
        [pbf_sq, pbf_a], "dw_in_b")
    ds, dng8, l_b = _dh_bwd(dproj, wg_in, s_pad, ds2, norm_g, pbf_b, lax.empty(pbf_b.shape, BF16), None)
    half_in = _sum_chips([(p32_a, l_a), (p32_b, l_b)], cw_in, pos, "rs_sum_in")
    half_sq = _sum_chips([(p32_sq, l_sq)], d, pos, "rs_sum_sq")
    tail_row = lax.broadcasted_iota(jnp.int32, (8, d), 0)
    tail = jnp.where(tail_row == 0, dng8, jnp.where(tail_row == 1, dfg8,
                     jnp.where(tail_row == 2, loss8[0, 0], 0.0)))
    block = jnp.concatenate([sm, ds[TM - N_META:TM], tail], axis=0)
    (other_in, other_sq), red = _sibling_swap([half_in, half_sq], block)
    col = lax.dynamic_slice(red, (0, me * dc), (AR_ROWS, dc))
    g_small = {
        "meta_tokens": col[ROW_DMETA:ROW_DMETA + N_META],
        "norm_g": red[ROW_DNG:ROW_DNG + 1],
        "conv_a_w": col[ROW_DWA:ROW_DWA + CONV_A][None],
        "conv_a_b": red[ROW_DCAB:ROW_DCAB + 1],
        "ln_a_g": red[ROW_DLNG:ROW_DLNG + 1],
        "ln_a_b": red[ROW_DLNB:ROW_DLNB + 1],
        "b_a_out": red[ROW_DBAO:ROW_DBAO + 1],
        "conv_b_w": col[ROW_DWB:ROW_DWB + CONV_B][None],
        "final_g": red[ROW_DFG],
    }

    upd_in = _adam_halves([w_in], [m_w_in], [v_w_in], half_in, other_in, pos, "adam_in")
    upd_sq = _adam_halves([w_a_out, w_b_out, w_out], [m_w_a_out, m_w_b_out, m_w_out],
                          [v_w_a_out, v_w_b_out, v_w_out], half_sq, other_sq, pos, "adam_sq")
    small_w = {"meta_tokens": (meta_tokens, m_meta_tokens, v_meta_tokens), "norm_g": (norm_g, m_norm_g, v_norm_g),
               "conv_a_w": (conv_a_w, m_conv_a_w, v_conv_a_w), "conv_a_b": (conv_a_b, m_conv_a_b, v_conv_a_b),
               "ln_a_g": (ln_a_g, m_ln_a_g, v_ln_a_g), "ln_a_b": (ln_a_b, m_ln_a_b, v_ln_a_b),
               "b_a_out": (b_a_out, m_b_a_out, v_b_a_out), "conv_b_w": (conv_b_w, m_conv_b_w, v_conv_b_w),
               "final_g": (final_g, m_final_g, v_final_g)}
    names_small = list(small_w)
    as2d = lambda t: t.reshape(-1, t.shape[-1])
    upd_small = _adam_small([(as2d(small_w[k][0]), as2d(g_small[k]), as2d(small_w[k][1]), as2d(small_w[k][2]))
                             for k in names_small])

    grads, deltas, new_m, new_v = dict(g_small), {}, {}, {}
    for k, upd in zip(names_small, upd_small):
        deltas[k], new_m[k], new_v[k] = [t.reshape(small_w[k][0].shape) for t in upd]
    grads["w_in"], deltas["w_in"], new_m["w_in"], new_v["w_in"] = upd_in
    for idx, k in enumerate(["w_a_out", "w_b_out", "w_out"]):
        grads[k], deltas[k], new_m[k], new_v[k] = upd_sq[4 * idx:4 * idx + 4]

    loss = red[ROW_LOSS, 0]
    grad_x = ds[TM:][None]
    order = ["meta_tokens", "norm_g", "w_in", "conv_a_w", "conv_a_b", "ln_a_g", "ln_a_b", "w_a_out", "b_a_out",
             "conv_b_w", "w_b_out", "w_out", "final_g"]
    return (loss, grad_x, *[grads[k] for k in order], *[deltas[k] for k in order],
            *[new_m[k] for k in order], *[new_v[k] for k in order])
```

```python
import functools

import jax
import jax.numpy as jnp
from jax import lax
from jax.experimental import pallas as pl
from jax.experimental.pallas import tpu as pltpu

F32 = jnp.float32
BF16 = jnp.bfloat16
MESH = pl.DeviceIdType.MESH

EPS = 1e-6
N_META = 16
N_SPLIT = 9
CONV_A = 31
CONV_B = 3
HALO_A = 32
HALO_B = 8
SHIFT_ROWS = 24
TM = 256
RB = 64
N_ROW_TILES_BIG = 8
ROW_BLOCK = 256
N_CHIPS = 4
VMEM_LIMIT = 56 * 1024 * 1024
VMEM_LIMIT_BIG = 62 * 1024 * 1024

ADAM_LR = 0.001
ADAM_B1 = 0.9
ADAM_B2 = 0.999
ADAM_EPS = 1e-08
ADAM_WD = 0.01
ADAM_STEP = 10

ROW_DWA = 0
ROW_DWB = 32
ROW_DCAB = 40
ROW_DLNG = 41
ROW_DLNB = 42
ROW_DBAO = 43
SM_ROWS = 48
ROW_DMETA = 48
ROW_DNG = 64
ROW_DFG = 65
ROW_LOSS = 66
AR_ROWS = 72


def _sigmoid(v):
    return 0.5 * jnp.tanh(0.5 * v) + 0.5


def _params(sem, **kw):
    return pltpu.CompilerParams(dimension_semantics=sem, vmem_limit_bytes=VMEM_LIMIT, **kw)


def _rows(rb):
    return pl.ds(pl.multiple_of(rb * RB, RB), RB)


def _mesh_pos():
    x, y, c = lax.axis_index("x"), lax.axis_index("y"), lax.axis_index("c")
    return x, y, c


def _half(ref, j, c):
    h = ref.shape[2] // 2
    return ref.at[:, j, pl.ds(c * h, h), :]


def _place_own(shard, pos, dtype, name):
    s, r, c = shard.shape
    rb = ROW_BLOCK if r % ROW_BLOCK == 0 else r

    def body(pos_ref, x_ref, o_ref):
        o_ref[...] = x_ref[...].astype(dtype)

    return pl.pallas_call(
        body, name=name,
        grid_spec=pltpu.PrefetchScalarGridSpec(
            num_scalar_prefetch=1, grid=(s, r // rb),
            in_specs=[pl.BlockSpec((None, rb, c), lambda si, b, pos_ref: (si, b, 0))],
            out_specs=pl.BlockSpec((None, None, rb, c), lambda si, b, pos_ref: (si, pos_ref[1], b, 0))),
        out_shape=jax.ShapeDtypeStruct((s, N_CHIPS, r, c), dtype),
        compiler_params=_params(("arbitrary",) * 2),
    )(pos, shard)


def _all_gather(bufs):
    n = len(bufs)

    def body(*refs):
        outs = refs[n:2 * n]
        send_sems, recv_sems = refs[2 * n:]
        x, y, c = _mesh_pos()
        me = 2 * x + y
        sibling = (x, y, 1 - c)
        chips = [(1 - x, y), (x, 1 - y), (1 - x, 1 - y)]

        def remote(a, k, piece_src, piece_dst, to):
            return pltpu.make_async_remote_copy(
                src_ref=piece_src, dst_ref=piece_dst, send_sem=send_sems.at[6 * a + k],
                recv_sem=recv_sems.at[6 * a + k], device_id=to, device_id_type=MESH)

        sends = []
        for a in range(n):
            mine = _half(outs[a], me, c)
            for k, (px, py) in enumerate(chips):
                sends.append(remote(a, k, mine, mine, (px, py, c)))
        for cp in sends:
            cp.start()
        for a in range(n):
            for k, (px, py) in enumerate(chips):
                piece = _half(outs[a], 2 * px + py, c)
                remote(a, k, piece, piece, (px, py, c)).wait_recv()
                fwd = remote(a, 3 + k, piece, piece, sibling)
                fwd.start()
                sends.append(fwd)
        for a in range(n):
            for k, (px, py) in enumerate(chips):
                piece = _half(outs[a], 2 * px + py, 1 - c)
                remote(a, 3 + k, piece, piece, sibling).wait_recv()
        for cp in sends:
            cp.wait_send()

    any_spec = pl.BlockSpec(memory_space=pl.ANY)
    return pl.pallas_call(
        body, name="ag_weights",
        in_specs=[any_spec] * n, out_specs=[any_spec] * n,
        out_shape=[jax.ShapeDtypeStruct(b.shape, b.dtype) for b in bufs],
        input_output_aliases={a: a for a in range(n)},
        scratch_shapes=[pltpu.SemaphoreType.DMA((6 * n,)), pltpu.SemaphoreType.DMA((6 * n,))],
    )(*bufs)


class _Exchange:
    def __init__(self, sends, recvs):
        self.sends, self.recvs = sends, recvs

    @staticmethod
    def _each(pairs, act):
        for cond, cp in pairs:
            if cond is None:
                act(cp)
            else:
                pl.when(cond)(functools.partial(act, cp))

    def start(self):
        self._each(self.sends, lambda cp: cp.start())

    def finish(self):
        self._each(self.recvs, lambda cp: cp.wait_recv())
        self._each(self.sends, lambda cp: cp.wait_send())


def _chip_exchange(part_ref, land_ref, send_sems, recv_sems, half=None):
    x, y, c = _mesh_pos()
    me = 2 * x + y
    sends, recvs = [], []
    for k, (px, py) in enumerate([(1 - x, y), (x, 1 - y), (1 - x, 1 - y)]):
        sems = dict(send_sem=send_sems.at[k], recv_sem=recv_sems.at[k], device_id=(px, py, c), device_id_type=MESH)
        slot = 2 * px + py if half is None else py
        sends.append((None if half is None else px == half, pltpu.make_async_remote_copy(
            src_ref=part_ref.at[:, slot], dst_ref=land_ref.at[:, me], **sems)))
        landed = land_ref.at[:, 2 * px + py]
        recvs.append((None if half is None else x == half,
                      pltpu.make_async_remote_copy(src_ref=landed, dst_ref=landed, **sems)))
    return _Exchange(sends, recvs)

def _sibling_swap(halves, small):
    n = len(halves)

    def body(*refs):
        ins, small_ref, outs, red_ref = refs[:n], refs[n], refs[n + 1:2 * n + 1], refs[2 * n + 1]
        send_sems, recv_sems = refs[2 * n + 2:2 * n + 4]
        reduce = _SmallAllReduce(small_ref, red_ref, *refs[2 * n + 4:])
        x, y, c = _mesh_pos()
        copies = [pltpu.make_async_remote_copy(
            src_ref=ins[a], dst_ref=outs[a], send_sem=send_sems.at[a], recv_sem=recv_sems.at[a],
            device_id=(x, y, 1 - c), device_id_type=MESH) for a in range(n)]
        reduce.start()
        for cp in copies:
            cp.start()
        reduce.between_chips()
        reduce.finish()
        for cp in copies:
            cp.wait()

    any_spec = pl.BlockSpec(memory_space=pl.ANY)
    vm = pl.BlockSpec(memory_space=pltpu.VMEM)
    outs = pl.pallas_call(
        body, name="rs_swap",
        in_specs=[any_spec] * n + [vm], out_specs=[any_spec] * n + [vm],
        out_shape=[jax.ShapeDtypeStruct(h.shape, h.dtype) for h in halves]
        + [jax.ShapeDtypeStruct(small.shape, F32)],
        scratch_shapes=[pltpu.SemaphoreType.DMA((n,)), pltpu.SemaphoreType.DMA((n,))]
        + _SmallAllReduce.scratch(*small.shape),
    )(*halves, small)
    return outs[:n], outs[n]


class _SmallAllReduce:
    def __init__(self, x_ref, out_ref, sib_ref, part_ref, peers_ref, send_sems, recv_sems):
        self.x_ref, self.out_ref, self.sib_ref, self.part_ref, self.peers_ref = x_ref, out_ref, sib_ref, part_ref, peers_ref
        x, y, c = _mesh_pos()
        self.me = 2 * x + y
        self.swap = pltpu.make_async_remote_copy(
            src_ref=x_ref, dst_ref=sib_ref, send_sem=send_sems.at[0], recv_sem=recv_sems.at[0],
            device_id=(x, y, 1 - c), device_id_type=MESH)
        self.sends, self.recvs = [], []
        for k, (px, py) in enumerate([(1 - x, y), (x, 1 - y), (1 - x, 1 - y)]):
            sems = dict(send_sem=send_sems.at[1 + k], recv_sem=recv_sems.at[1 + k],
                        device_id=(px, py, c), device_id_type=MESH)
            self.sends.append(pltpu.make_async_remote_copy(src_ref=part_ref, dst_ref=peers_ref.at[self.me], **sems))
            landed = peers_ref.at[2 * px + py]
            self.recvs.append(pltpu.make_async_remote_copy(src_ref=landed, dst_ref=landed, **sems))

    @staticmethod
    def scratch(rows, d):
        return [pltpu.VMEM((rows, d), F32), pltpu.VMEM((rows, d), F32), pltpu.VMEM((N_CHIPS, rows, d), F32),
                pltpu.SemaphoreType.DMA((4,)), pltpu.SemaphoreType.DMA((4,))]

    def start(self):
        self.swap.start()

    def between_chips(self):
        self.swap.wait()
        self.part_ref[...] = self.x_ref[...] + self.sib_ref[...]
        self.peers_ref[self.me] = self.part_ref[...]
        for cp in self.sends:
            cp.start()

    def finish(self):
        for cp in self.recvs:
            cp.wait_recv()
        for cp in self.sends:
            cp.wait_send()
        p = self.peers_ref
        self.out_ref[...] = ((p[0] + p[1]) + p[2]) + p[3]


def _sum_chips(parts, cw, pos, name):
    s, _, h, _ = parts[0][0].shape
    hb = min(h, ROW_BLOCK)
    widths = [own.shape[3] // cw for own, _ in parts]
    starts = [sum(widths[:a]) for a in range(len(parts))]

    def body(pos_ref, *refs):
        out_ref = refs[-1]
        n = pl.program_id(2)
        total = None
        for a in range(len(parts)):
            own, l1, l2, l3 = refs[4 * a:4 * a + 4]
            val = ((own[...] + l1[...].astype(F32)) + l2[...].astype(F32)) + l3[...].astype(F32)
            total = val if total is None else jnp.where(n >= starts[a], val, total)
        out_ref[...] = total

    def slot(a, k):
        col = lambda n: jnp.clip(n - starts[a], 0, widths[a] - 1)
        return pl.BlockSpec((None, None, hb, cw),
                            lambda si, b, n, pos_ref: (si, (pos_ref[1] + k) % N_CHIPS, b, col(n)))

    operands, specs = [], []
    for a, (own, landed) in enumerate(parts):
        operands += [own, landed, landed, landed]
        specs += [slot(a, 0), slot(a, 1), slot(a, 2), slot(a, 3)]
    return pl.pallas_call(
        body, name=name,
        grid_spec=pltpu.PrefetchScalarGridSpec(
            num_scalar_prefetch=1, grid=(s, h // hb, sum(widths)), in_specs=specs,
            out_specs=pl.BlockSpec((None, hb, cw), lambda si, b, n, pos_ref: (si, b, n))),
        out_shape=jax.ShapeDtypeStruct((s, h, sum(widths) * cw), F32),
        compiler_params=_params(("arbitrary",) * 3),
    )(pos, *operands)


def _adamw(w, g, m, v):
    m = ADAM_B1 * m + (1.0 - ADAM_B1) * g
    v = ADAM_B2 * v + (1.0 - ADAM_B2) * (g * g)
    m_hat = m / (1.0 - ADAM_B1 ** ADAM_STEP)
    v_hat = v / (1.0 - ADAM_B2 ** ADAM_STEP)
    delta = -ADAM_LR * (m_hat / (jnp.sqrt(v_hat) + ADAM_EPS) + ADAM_WD * w)
    return delta, m, v


def _adam_halves(ws, ms, vs, g_own, g_recv, pos, name):
    n = len(ws)
    _, r, c = ws[0].shape
    h = r // 2
    rb = min(h, ROW_BLOCK)
    nb = h // rb

    def body(pos_ref, *refs):
        w_refs, m_refs, v_refs = refs[:n], refs[n:2 * n], refs[2 * n:3 * n]
        go_ref, gr_ref = refs[3 * n:3 * n + 2]
        outs = refs[3 * n + 2:]
        mine = pl.program_id(0) == pos_ref[0]
        for a in range(n):
            g = jnp.where(mine, go_ref[a], gr_ref[a])
            delta, m, v = _adamw(w_refs[a][...], g, m_refs[a][...], v_refs[a][...])
            outs[4 * a][...], outs[4 * a + 1][...], outs[4 * a + 2][...], outs[4 * a + 3][...] = g, delta, m, v

    spec_w = pl.BlockSpec((None, rb, c), lambda hf, b, pos_ref: (0, hf * nb + b, 0))
    spec_g = pl.BlockSpec((n, rb, c), lambda hf, b, pos_ref: (0, b, 0))
    return pl.pallas_call(
        body, name=name,
        grid_spec=pltpu.PrefetchScalarGridSpec(
            num_scalar_prefetch=1, grid=(2, nb), in_specs=[spec_w] * (3 * n) + [spec_g] * 2,
            out_specs=[spec_w] * (4 * n)),
        out_shape=[jax.ShapeDtypeStruct((1, r, c), F32)] * (4 * n),
        compiler_params=_params(("arbitrary",) * 2),
    )(pos, *ws, *ms, *vs, g_own, g_recv)


def _adam_small(items):
    n = len(items)

    def body(*refs):
        ins, outs = refs[:4 * n], refs[4 * n:]
        for a in range(n):
            w_ref, g_ref, m_ref, v_ref = ins[4 * a:4 * a + 4]
            d, m, v = _adamw(w_ref[...], g_ref[...], m_ref[...], v_ref[...])
            outs[3 * a][...] = d
            outs[3 * a + 1][...] = m
            outs[3 * a + 2][...] = v

    vm = pl.BlockSpec(memory_space=pltpu.VMEM)
    flat = [t for it in items for t in it]
    outs = pl.pallas_call(
        body, name="adam_small", in_specs=[vm] * (4 * n), out_specs=[vm] * (3 * n),
        out_shape=[jax.ShapeDtypeStruct(it[0].shape, F32) for it in items for _ in range(3)],
    )(*flat)
    return [tuple(outs[3 * a:3 * a + 3]) for a in range(n)]


def _shard_of_step(js, me):
    flip = jnp.where(js == 1, 2, jnp.where(js == 2, 1, jnp.where(js == 3, 3, 0)))
    return lax.bitwise_xor(me, flip)


def _big_row_spec(seq, tmb, d, tile_of):
    return pl.BlockSpec((pl.Element(tmb), pl.Element(d)),
                        lambda *args: (pl.multiple_of(jnp.minimum(tile_of(*args) * tmb, seq - tmb), 8), 0))


def _big_row_tile(x_ref, front_ref, i):
    rows = x_ref[...]
    last = jnp.concatenate([rows[TM:], front_ref[...]], axis=0)
    return jnp.where(i == N_ROW_TILES_BIG - 1, last, rows)


def _proj_fwd(x, front, norm_g, bufs, pos):
    seq, d = x.shape
    tp = seq + TM
    _, nsh, _, sw = bufs[0].shape
    tmb = tp // N_ROW_TILES_BIG
    assert tmb >= TM and tp == tmb * N_ROW_TILES_BIG
    n = len(bufs)

    def body(pos_ref, x_ref, front_ref, g_ref, *refs):
        proj_ref = refs[n]
        gbufs = refs[n + 1:2 * n + 1]
        wbuf, wsems, send_sems, recv_sems = refs[2 * n + 1:]
        x, y, c = _mesh_pos()
        me = 2 * x + y
        sibling = (x, y, 1 - c)
        chips = [(1 - x, y), (x, 1 - y), (1 - x, 1 - y)]
        js, i = pl.program_id(0), pl.program_id(1)

        def remote(a, k, piece, to):
            return pltpu.make_async_remote_copy(
                src_ref=piece, dst_ref=piece, send_sem=send_sems.at[6 * a + k],
                recv_sem=recv_sems.at[6 * a + k], device_id=to, device_id_type=MESH)

        def fetch(chip, step):
            return pltpu.make_async_copy(gbufs[0].at[0, chip], wbuf.at[step % 2], wsems.at[step % 2])

        chip_ids = [2 * px + py for px, py in chips]
        relayed_chip = jnp.where(c == 0, chip_ids[0], chip_ids[1])
        relay_to = (jnp.where(c == 0, x, 1 - x), jnp.where(c == 0, 1 - y, y), c)

        def own_piece(a, k):
            return remote(a, k, _half(gbufs[a], me, c), (*chips[k], c))

        def relay(a):
            return remote(a, 2, _half(gbufs[a], relayed_chip, c), relay_to)

        def to_sibling(a, k, core):
            return remote(a, 3 + k, _half(gbufs[a], chip_ids[k], core), sibling)

        def landed(a, k):
            return remote(a, k, _half(gbufs[a], chip_ids[k], c), (*chips[k], c))

        def take_neighbours(a):
            landed(a, 0).wait_recv()
            landed(a, 1).wait_recv()
            relay(a).start()
            for k in range(2):
                to_sibling(a, k, c).start()
            for k in range(2):
                to_sibling(a, k, 1 - c).wait_recv()

        def take_diagonal(a):
            landed(a, 2).wait_recv()
            to_sibling(a, 2, c).start()
            to_sibling(a, 2, 1 - c).wait_recv()

        @pl.when((js == 0) & (i == 0))
        def _():
            for a in range(n):
                for k in range(2):
                    own_piece(a, k).start()
            fetch(me, 0).start()
            fetch(me, 0).wait()

        @pl.when((js == 1) & (i == 0))
        def _():
            take_neighbours(0)
            fetch(chip_ids[0], 1).start()
            fetch(chip_ids[0], 1).wait()

        @pl.when((js == 1) & (i == N_ROW_TILES_BIG - 2))
        def _():
            fetch(chip_ids[1], 2).start()

        @pl.when((js == 2) & (i == 0))
        def _():
            fetch(chip_ids[1], 2).wait()

        @pl.when((js == 2) & (i == 1))
        def _():
            for a in range(1, n):
                take_neighbours(a)

        @pl.when((js == 2) & (i == N_ROW_TILES_BIG - 2))
        def _():
            take_diagonal(0)
            fetch(chip_ids[2], 3).start()

        @pl.when((js == 3) & (i == 0))
        def _():
            fetch(chip_ids[2], 3).wait()

        s = _big_row_tile(x_ref, front_ref, i)
        r = lax.rsqrt(jnp.mean(s * s, axis=-1, keepdims=True) + EPS)
        h = (s * r * g_ref[...]).astype(BF16)
        proj_ref[...] = jnp.dot(h, wbuf[js % 2], preferred_element_type=F32).astype(BF16)

        @pl.when((js == nsh - 1) & (i == N_ROW_TILES_BIG - 1))
        def _():
            for a in range(1, n):
                take_diagonal(a)
            for a in range(n):
                for k in range(2):
                    own_piece(a, k).wait_send()
                relay(a).wait_send()
                for k in range(3):
                    to_sibling(a, k, c).wait_send()

    any_spec = pl.BlockSpec(memory_space=pl.ANY)
    outs = pl.pallas_call(
        body, name="f1_proj",
        grid_spec=pltpu.PrefetchScalarGridSpec(
            num_scalar_prefetch=1, grid=(nsh, N_ROW_TILES_BIG),
            in_specs=[_big_row_spec(seq, tmb, d, lambda js, i, pos_ref: i),
                      pl.BlockSpec((TM, d), lambda js, i, pos_ref: (0, 0)),
                      pl.BlockSpec((1, d), lambda js, i, pos_ref: (0, 0))] + [any_spec] * n,
            out_specs=[pl.BlockSpec((tmb, sw), lambda js, i, pos_ref: (i, _shard_of_step(js, pos_ref[1])))]
            + [any_spec] * n,
            scratch_shapes=[pltpu.VMEM((2, d, sw), BF16), pltpu.SemaphoreType.DMA((2,)),
                            pltpu.SemaphoreType.DMA((6 * n,)), pltpu.SemaphoreType.DMA((6 * n,))]),
        out_shape=[jax.ShapeDtypeStruct((tp, nsh * sw), BF16)]
        + [jax.ShapeDtypeStruct(b.shape, b.dtype) for b in bufs],
        input_output_aliases={4 + a: 1 + a for a in range(n)},
        compiler_params=_params(("arbitrary", "arbitrary")),
    )(pos, x, front, norm_g, *bufs)
    return outs[0], outs[1:]


def _dh_bwd(dproj, wg_in, x, front, ds2, norm_g, part, land, half):
    seq, d = x.shape
    tp = seq + TM
    _, nsh, _, sw = wg_in.shape
    tmb = tp // N_ROW_TILES_BIG
    tail = tmb - TM
    last = N_ROW_TILES_BIG - 1

    def body(dp_ref, w_hbm, x_ref, front_ref, ds2_ref, g_ref, part_ref, _, gx_hbm, dfront_ref, dng_ref, land_ref,
             wbuf, gacc, dsbuf, wsem, osems, send_sems, recv_sems):
        exchange = _chip_exchange(part_ref, land_ref, send_sems, recv_sems, half)
        i = pl.program_id(0)

        def x_rows_out(step):
            return pltpu.make_async_copy(dsbuf.at[step % 2], gx_hbm.at[pl.ds(step * tmb, tmb), :], osems.at[step % 2])

        last_out = pltpu.make_async_copy(dsbuf.at[last % 2, pl.ds(0, tail), :],
                                         gx_hbm.at[pl.ds(last * tmb, tail), :], osems.at[last % 2])

        @pl.when(i == 0)
        def _():
            exchange.start()
            gacc[...] = jnp.zeros_like(gacc)
            whole = pltpu.make_async_copy(w_hbm.at[0], wbuf, wsem)
            whole.start()
            whole.wait()

        dh = None
        for j in range(nsh):
            part = lax.dot_general(dp_ref[:, j * sw:(j + 1) * sw], wbuf[j], (((1,), (1,)), ((), ())),
                                   preferred_element_type=F32)
            dh = part if dh is None else dh + part
        s = _big_row_tile(x_ref, front_ref, i)
        r = lax.rsqrt(jnp.mean(s * s, axis=-1, keepdims=True) + EPS)
        gacc[...] += (dh * s * r).reshape(tmb // 8, 8, d).sum(axis=0)
        t = dh * g_ref[...]

        @pl.when(i >= 2)
        def _():
            x_rows_out(i - 2).wait()

        dsbuf[i % 2] = ds2_ref[...] + r * t - s * (r * r * r) * jnp.mean(t * s, axis=-1, keepdims=True)

        @pl.when(i < last)
        def _():
            x_rows_out(i).start()

        @pl.when(i == last)
        def _():
            last_out.start()
            dfront_ref[...] = dsbuf[last % 2, tail:, :]
            dng_ref[...] = jnp.broadcast_to(jnp.sum(gacc[...], axis=0, keepdims=True), (8, d))
            exchange.finish()
            x_rows_out(last - 1).wait()
            last_out.wait()

    any_spec = pl.BlockSpec(memory_space=pl.ANY)
    return pl.pallas_call(
        body, name="b2_dh", grid=(N_ROW_TILES_BIG,),
        in_specs=[pl.BlockSpec((tmb, nsh * sw), lambda i: (i, 0)), any_spec,
                  _big_row_spec(seq, tmb, d, lambda i: i),
                  pl.BlockSpec((TM, d), lambda i: (0, 0)),
                  pl.BlockSpec((tmb, d), lambda i: (i, 0)),
                  pl.BlockSpec((1, d), lambda i: (0, 0)), any_spec, any_spec],
        out_specs=[any_spec, pl.BlockSpec((TM, d), lambda i: (0, 0)),
                   pl.BlockSpec((8, d), lambda i: (0, 0)), any_spec],
        out_shape=[jax.ShapeDtypeStruct((seq, d), F32), jax.ShapeDtypeStruct((TM, d), F32),
                   jax.ShapeDtypeStruct((8, d), F32), jax.ShapeDtypeStruct(land.shape, land.dtype)],
        input_output_aliases={7: 3},
        scratch_shapes=[pltpu.VMEM((nsh, d, sw), BF16), pltpu.VMEM((8, d), F32), pltpu.VMEM((2, tmb, d), F32),
                        pltpu.SemaphoreType.DMA, pltpu.SemaphoreType.DMA((2,)),
                        pltpu.SemaphoreType.DMA((3,)), pltpu.SemaphoreType.DMA((3,))],
        compiler_params=pltpu.CompilerParams(dimension_semantics=("arbitrary",),
                                             vmem_limit_bytes=VMEM_LIMIT_BIG),
    )(dproj, wg_in, x, front, ds2, norm_g, part, land)


def _col_block(width, cap):
    return max(b for b in range(128, cap + 1, 128) if width % b == 0)


def _dw_reduced(lhs_t, rhs, cw, nblk, operands, groups, out_dims, out_block, out_index, carried, name):
    na, d, tp = lhs_t.shape
    rg = d // groups
    hh = rg // 2
    nc = len(carried)

    def body(*refs):
        l_ref, r_ref = refs[:2]
        part_refs = refs[2:2 + nc]
        p32_ref, pbf_ref = refs[2 + nc:4 + nc]
        land_refs = refs[4 + nc:4 + 2 * nc]
        res, rbuf, send_sems, recv_sems = refs[4 + 2 * nc:8 + 2 * nc]
        xsems = refs[8 + 2 * nc:]
        exchanges = [_chip_exchange(part_refs[e], land_refs[e], xsems[2 * e], xsems[2 * e + 1]) for e in range(nc)]
        exchange = _Exchange([s for ex in exchanges for s in ex.sends], [r for ex in exchanges for r in ex.recvs])
        x, y, c = _mesh_pos()
        t = pl.program_id(0)
        u = jnp.maximum(t - 1, 0)

        def to_sibling(blk):
            return pltpu.make_async_remote_copy(
                src_ref=res.at[blk % 2, :, pl.ds((1 - c) * hh, hh), :], dst_ref=rbuf.at[blk % 2],
                send_sem=send_sems.at[blk], recv_sem=recv_sems.at[blk],
                device_id=(x, y, 1 - c), device_id_type=MESH)

        @pl.when(t == 0)
        def _():
            exchange.start()

        @pl.when(t < nblk)
        def _():
            res[t % 2] = jnp.dot(l_ref[...], r_ref[...], preferred_element_type=F32).reshape(groups, rg, cw)

        @pl.when(t >= 1)
        def _():
            to_sibling(u).wait_recv()
            p = res[u % 2, :, pl.ds(c * hh, hh), :] + rbuf[u % 2]
            p32_ref[...] = p.reshape(p32_ref.shape)
            pbf_ref[...] = p.reshape(pbf_ref.shape).astype(BF16)

        @pl.when(t < nblk)
        def _():
            to_sibling(t).start()

        @pl.when(t >= 1)
        def _():
            to_sibling(u).wait_send()

        @pl.when(t == nblk)
        def _():
            exchange.finish()

    any_spec = pl.BlockSpec(memory_space=pl.ANY)
    last = nblk - 1
    out_spec = pl.BlockSpec(out_block, lambda t: out_index(jnp.maximum(t - 1, 0)))
    outs = pl.pallas_call(
        body, name=name, grid=(nblk + 1,),
        in_specs=[pl.BlockSpec((None, d, tp), lambda t: (operands(jnp.minimum(t, last))[0], 0, 0)),
                  pl.BlockSpec((None, tp, cw), lambda t: (operands(jnp.minimum(t, last))[0], 0,
                                                          operands(jnp.minimum(t, last))[1]))]
        + [any_spec] * nc,
        out_specs=[out_spec, out_spec] + [any_spec] * nc,
        out_shape=[jax.ShapeDtypeStruct(out_dims, F32), jax.ShapeDtypeStruct(out_dims, BF16)]
        + [jax.ShapeDtypeStruct(e.shape, e.dtype) for e in carried],
        scratch_shapes=[pltpu.VMEM((2, groups, rg, cw), F32), pltpu.VMEM((2, groups, hh, cw), F32),
                        pltpu.SemaphoreType.DMA((nblk,)), pltpu.SemaphoreType.DMA((nblk,))]
        + [pltpu.SemaphoreType.DMA((3,)), pltpu.SemaphoreType.DMA((3,))] * nc,
        compiler_params=_params(("arbitrary",)),
    )(lhs_t, rhs, *carried)
    return outs[0], outs[1], outs[2:]


def _conv_a_taps(first_lag, last_lag):
    out = []
    for r in range(8):
        taps = [(q, 8 * q + r) for q in range(5) if first_lag <= 8 * q + r <= last_lag]
        if taps:
            out.append((r, taps))
    return out


def _tile_block(i, nt):
    return jnp.where(i == 0, nt - 1, i - 1)


def _mix_fwd(x, front, proj, target, w3, wa, wb, conv_a_b, ln_g, ln_b, b_a_out, final_g, norm_g):
    seq, d = x.shape
    tp = seq + TM
    nt = tp // TM
    nrb = TM // RB
    shl = TM + SHIFT_ROWS

    def body(x_ref, front_ref, proj_ref, tgt_ref, w3_ref, wa_ref, wb_ref, cab_ref, lng_ref, lnb_ref, bao_ref, fg_ref,
             ng_ref, ca_ref, cb_ref, ya_ref, yb_ref, abmt_ref, ds2_ref, ht_ref, loss_ref, dfg_ref,
             abm_ref, ext_a, ext_b, sh, s2_s, lacc, gacc):
        i = pl.program_id(0)

        def split(k, rows):
            return proj_ref[rows, k * d:(k + 1) * d].astype(F32)

        def s_tile():
            return jnp.where(i == 0, front_ref[...], x_ref[...])

        s_in = s_tile()
        h = s_in * lax.rsqrt(jnp.mean(s_in * s_in, axis=-1, keepdims=True) + EPS) * ng_ref[...]
        ht_ref[...] = h.astype(BF16).T

        @pl.when(i == 0)
        def _():
            ext_a[0:HALO_A, :] = jnp.zeros((HALO_A, d), F32)
            ext_b[0:HALO_B, :] = jnp.zeros((HALO_B, d), F32)
            lacc[...] = jnp.zeros_like(lacc)
            gacc[...] = jnp.zeros_like(gacc)

        def conv_in(rb, carry):
            rows = _rows(rb)
            ua0 = split(0, rows) * _sigmoid(split(1, rows))
            ext_a[pl.ds(pl.multiple_of(HALO_A + rb * RB, 8), RB), :] = ua0
            ext_b[pl.ds(pl.multiple_of(HALO_B + rb * RB, 8), RB), :] = split(4, rows) * split(5, rows)
            ca_ref[rows, :] = jnp.broadcast_to(cab_ref[...], (RB, d))
            return carry
        lax.fori_loop(0, nrb, conv_in, 0)

        for r, taps in _conv_a_taps(HALO_A - CONV_A + 1, HALO_A):
            if r == 0:
                src = ext_a
            else:
                sh[...] = ext_a[r:r + shl, :]
                src = sh

            def conv_acc(rb, carry, src=src, taps=taps):
                rows = _rows(rb)
                acc = ca_ref[rows, :]
                for q, lag in taps:
                    k = lag - (HALO_A - CONV_A + 1)
                    acc = acc + src[pl.ds(pl.multiple_of(rb * RB + 8 * q, 8), RB), :] * wa_ref[k:k + 1, :]
                ca_ref[rows, :] = acc
                return carry
            lax.fori_loop(0, nrb, conv_acc, 0)
        ext_a[0:HALO_A, :] = ext_a[TM:TM + HALO_A, :]

        cb_ref[...] = ext_b[HALO_B:HALO_B + TM, :] * wb_ref[2:3, :]
        for k in range(CONV_B - 1):
            off = HALO_B - CONV_B + 1 + k
            sh[0:TM, :] = ext_b[off:off + TM, :]
            cb_ref[...] += sh[0:TM, :] * wb_ref[k:k + 1, :]
        ext_b[0:HALO_B, :] = ext_b[TM:TM + HALO_B, :]

        def branches(rb, carry):
            rows = _rows(rb)
            ca = ca_ref[rows, :]
            mu = jnp.mean(ca, axis=-1, keepdims=True)
            xc = ca - mu
            rstd = lax.rsqrt(jnp.mean(xc * xc, axis=-1, keepdims=True) + EPS)
            ln = xc * rstd * lng_ref[...] + lnb_ref[...]
            ua = ln * _sigmoid(ln)
            a_z = split(2, rows)
            abm_ref[0, rows, :] = (ua * (a_z * _sigmoid(a_z))).astype(BF16)
            b_z = split(6, rows)
            ub = split(3, rows) * cb_ref[rows, :]
            abm_ref[1, rows, :] = (ub * (b_z * _sigmoid(b_z))).astype(BF16)
            return carry
        lax.fori_loop(0, nrb, branches, 0)

        ya_ref[...] = jnp.dot(abm_ref[0], w3_ref[0], preferred_element_type=F32) + bao_ref[...]
        yb_ref[...] = jnp.dot(abm_ref[1], w3_ref[1], preferred_element_type=F32)

        def merge(rb, carry):
            rows = _rows(rb)
            m = _sigmoid(split(7, rows)) * ya_ref[rows, :] + _sigmoid(split(8, rows)) * yb_ref[rows, :]
            abm_ref[2, rows, :] = m.astype(BF16)
            return carry
        lax.fori_loop(0, nrb, merge, 0)

        s2_s[...] = s_tile() + jnp.dot(abm_ref[2], w3_ref[2], preferred_element_type=F32)
        for k in range(3):
            abmt_ref[k] = abm_ref[k].T
        live = (i > 0).astype(F32)

        def head(rb, carry):
            rows = _rows(rb)
            s2 = s2_s[rows, :]
            r2 = lax.rsqrt(jnp.mean(s2 * s2, axis=-1, keepdims=True) + EPS)
            diff = (s2 * r2 * fg_ref[...] - tgt_ref[rows, :]) * live
            lacc[...] += diff * diff
            dy = diff * (1.0 / d)
            gacc[...] += (dy * s2 * r2).reshape(RB // 8, 8, d).sum(axis=0)
            t = dy * fg_ref[...]
            ds2_ref[rows, :] = r2 * t - s2 * (r2 * r2 * r2) * jnp.mean(t * s2, axis=-1, keepdims=True)
            return carry
        lax.fori_loop(0, nrb, head, 0)

        @pl.when(i == nt - 1)
        def _():
            loss_ref[...] = jnp.broadcast_to(0.5 * jnp.sum(lacc[...]) * (1.0 / d), (8, 128))
            dfg_ref[...] = jnp.broadcast_to(jnp.sum(gacc[...], axis=0, keepdims=True), (8, d))

    row_f32 = pl.BlockSpec((TM, d), lambda i: (_tile_block(i, nt), 0))
    x_rows = pl.BlockSpec((TM, d), lambda i: (jnp.maximum(i - 1, 0), 0))
    const = lambda shape: pl.BlockSpec(shape, lambda i: (0,) * len(shape))
    return pl.pallas_call(
        body, name="f2_mix", grid=(nt,),
        in_specs=[x_rows, const((TM, d)),
                  pl.BlockSpec((TM, N_SPLIT * d), lambda i: (_tile_block(i, nt), 0)),
                  x_rows,
                  const((3, d, d)), const(wa.shape), const(wb.shape)] + [const((1, d))] * 6,
        out_specs=[row_f32, row_f32, row_f32, row_f32,
                   pl.BlockSpec((3, d, TM), lambda i: (0, 0, _tile_block(i, nt))),
                   row_f32, pl.BlockSpec((d, TM), lambda i: (0, _tile_block(i, nt))),
                   const((8, 128)), const((8, d))],
        out_shape=[jax.ShapeDtypeStruct((tp, d), F32)] * 4
        + [jax.ShapeDtypeStruct((3, d, tp), BF16), jax.ShapeDtypeStruct((tp, d), F32),
           jax.ShapeDtypeStruct((d, tp), BF16),
           jax.ShapeDtypeStruct((8, 128), F32), jax.ShapeDtypeStruct((8, d), F32)],
        scratch_shapes=[pltpu.VMEM((3, TM, d), BF16),
                        pltpu.VMEM((HALO_A + TM, d), F32), pltpu.VMEM((HALO_B + TM, d), F32),
                        pltpu.VMEM((shl, d), F32), pltpu.VMEM((TM, d), F32),
                        pltpu.VMEM((RB, d), F32), pltpu.VMEM((8, d), F32)],
        compiler_params=_params(("arbitrary",)),
    )(x, front, proj, target, w3, wa, wb, conv_a_b, ln_g, ln_b, b_a_out, final_g, norm_g)


def _mix_bwd(ds2, proj, ca, cb, ya, yb, w3, wa, wb, ln_g, ln_b):
    tp, d = ds2.shape
    nt = tp // TM
    nrb = TM // RB
    shl = TM + SHIFT_ROWS
    nt_dims = (((1,), (1,)), ((), ()))

    def body(ds2_ref, proj_ref, ca_ref, cb_ref, ya_ref, yb_ref, w3_ref, wa_ref, wb_ref, lng_ref, lnb_ref,
             dproj_ref, d3_ref, sm_ref, ext_d, ext_e, sh, dm_s, dpa_s, dpb_s, dua0_s, acc):
        step = pl.program_id(0)

        def split(k, rows):
            return proj_ref[rows, k * d:(k + 1) * d].astype(F32)

        def put(k, rows, val):
            dproj_ref[rows, k * d:(k + 1) * d] = val.astype(BF16)

        def accum(row, val):
            acc[row] += val.reshape(RB // 8, 8, d).sum(axis=0)

        @pl.when(step == 0)
        def _():
            ext_d[TM:TM + HALO_A, :] = jnp.zeros((HALO_A, d), F32)
            ext_e[TM:TM + HALO_B, :] = jnp.zeros((HALO_B, d), F32)
            acc[...] = jnp.zeros_like(acc)

        d3_ref[2] = ds2_ref[...].astype(BF16)
        dm_s[...] = lax.dot_general(d3_ref[2], w3_ref[2], nt_dims, preferred_element_type=F32)

        def gates(rb, carry):
            rows = _rows(rb)
            dm = dm_s[rows, :]
            sa = _sigmoid(split(7, rows))
            sb = _sigmoid(split(8, rows))
            ya_v = ya_ref[rows, :]
            yb_v = yb_ref[rows, :]
            put(7, rows, dm * ya_v * sa * (1.0 - sa))
            put(8, rows, dm * yb_v * sb * (1.0 - sb))
            dya = dm * sa
            accum(ROW_DBAO, dya)
            d3_ref[0, rows, :] = dya.astype(BF16)
            d3_ref[1, rows, :] = (dm * sb).astype(BF16)
            return carry
        lax.fori_loop(0, nrb, gates, 0)

        dpa_s[...] = lax.dot_general(d3_ref[0], w3_ref[0], nt_dims, preferred_element_type=F32)
        dpb_s[...] = lax.dot_general(d3_ref[1], w3_ref[1], nt_dims, preferred_element_type=F32)

        def branches(rb, carry):
            rows = _rows(rb)
            ca_v = ca_ref[rows, :]
            mu = jnp.mean(ca_v, axis=-1, keepdims=True)
            xc = ca_v - mu
            rstd = lax.rsqrt(jnp.mean(xc * xc, axis=-1, keepdims=True) + EPS)
            xhat = xc * rstd
            ln = xhat * lng_ref[...] + lnb_ref[...]
            sl = _sigmoid(ln)
            ua = ln * sl
            a_z = split(2, rows)
            sz = _sigmoid(a_z)
            dpa = dpa_s[rows, :]
            put(2, rows, dpa * ua * (sz * (1.0 + a_z * (1.0 - sz))))
            dln = dpa * (a_z * sz) * (sl * (1.0 + ln * (1.0 - sl)))
            accum(ROW_DLNG, dln * xhat)
            accum(ROW_DLNB, dln)
            dxh = dln * lng_ref[...]
            dca = rstd * (dxh - jnp.mean(dxh, axis=-1, keepdims=True)
                          - xhat * jnp.mean(dxh * xhat, axis=-1, keepdims=True))
            accum(ROW_DCAB, dca)
            ext_d[rows, :] = dca
            dua0_s[rows, :] = jnp.zeros((RB, d), F32)
            dm_s[rows, :] = split(0, rows) * _sigmoid(split(1, rows))
            b_z = split(6, rows)
            szb = _sigmoid(b_z)
            dpb = dpb_s[rows, :]
            b_b = split(3, rows)
            cb_v = cb_ref[rows, :]
            put(6, rows, dpb * (b_b * cb_v) * (szb * (1.0 + b_z * (1.0 - szb))))
            dub = dpb * (b_z * szb)
            put(3, rows, dub * cb_v)
            ext_e[rows, :] = dub * b_b
            return carry
        lax.fori_loop(0, nrb, branches, 0)

        for r, taps in _conv_a_taps(0, CONV_A - 1):
            if r == 0:
                src = ext_d
            else:
                sh[...] = ext_d[r:r + shl, :]
                src = sh

            def conv_t(rb, carry, src=src, taps=taps):
                rows = _rows(rb)
                ua0 = dm_s[rows, :]
                dua0 = dua0_s[rows, :]
                for q, lag in taps:
                    k = CONV_A - 1 - lag
                    slab = src[pl.ds(pl.multiple_of(rb * RB + 8 * q, 8), RB), :]
                    dua0 = dua0 + slab * wa_ref[k:k + 1, :]
                    accum(ROW_DWA + k, slab * ua0)
                dua0_s[rows, :] = dua0
                return carry
            lax.fori_loop(0, nrb, conv_t, 0)
        ext_d[TM:TM + HALO_A, :] = ext_d[0:HALO_A, :]

        dpb_s[...] = ext_e[0:TM, :] * wb_ref[CONV_B - 1:CONV_B, :]
        for lag in range(CONV_B):
            k = CONV_B - 1 - lag
            if lag > 0:
                sh[0:TM, :] = ext_e[lag:lag + TM, :]
                dpb_s[...] += sh[0:TM, :] * wb_ref[k:k + 1, :]
            src = ext_e if lag == 0 else sh

            def conv_b_w(rb, carry, src=src, k=k):
                rows = _rows(rb)
                accum(ROW_DWB + k, src[rows, :] * (split(4, rows) * split(5, rows)))
                return carry
            lax.fori_loop(0, nrb, conv_b_w, 0)
        ext_e[TM:TM + HALO_B, :] = ext_e[0:HALO_B, :]

        def inputs(rb, carry):
            rows = _rows(rb)
            dua0 = dua0_s[rows, :]
            a_val = split(0, rows)
            sg = _sigmoid(split(1, rows))
            put(0, rows, dua0 * sg)
            put(1, rows, dua0 * a_val * sg * (1.0 - sg))
            dcbin = dpb_s[rows, :]
            put(4, rows, dcbin * split(5, rows))
            put(5, rows, dcbin * split(4, rows))
            return carry
        lax.fori_loop(0, nrb, inputs, 0)

        @pl.when(step == nt - 1)
        def _():
            for row in range(SM_ROWS):
                sm_ref[row:row + 1, :] = jnp.sum(acc[row], axis=0, keepdims=True)

    rev = lambda i: (_tile_block(nt - 1 - i, nt), 0)
    row_f32 = pl.BlockSpec((TM, d), rev)
    const = lambda shape: pl.BlockSpec(shape, lambda i: (0,) * len(shape))
    return pl.pallas_call(
        body, name="b1_mix", grid=(nt,),
        in_specs=[row_f32, pl.BlockSpec((TM, N_SPLIT * d), rev), row_f32, row_f32, row_f32, row_f32,
                  const((3, d, d)), const(wa.shape), const(wb.shape), const((1, d)), const((1, d))],
        out_specs=[pl.BlockSpec((TM, N_SPLIT * d), rev),
                   pl.BlockSpec((3, TM, d), lambda i: (0, _tile_block(nt - 1 - i, nt), 0)),
                   const((SM_ROWS, d))],
        out_shape=[jax.ShapeDtypeStruct((tp, N_SPLIT * d), BF16), jax.ShapeDtypeStruct((3, tp, d), BF16),
                   jax.ShapeDtypeStruct((SM_ROWS, d), F32)],
        scratch_shapes=[pltpu.VMEM((TM + HALO_A, d), F32), pltpu.VMEM((TM + HALO_B, d), F32),
                        pltpu.VMEM((shl, d), F32), pltpu.VMEM((TM, d), F32), pltpu.VMEM((TM, d), F32),
                        pltpu.VMEM((TM, d), F32), pltpu.VMEM((TM, d), F32),
                        pltpu.VMEM((SM_ROWS, 8, d), F32)],
        compiler_params=_params(("arbitrary",)),
    )(ds2, proj, ca, cb, ya, yb, w3, wa, wb, ln_g, ln_b)


def kernel(x, meta_tokens, norm_g, w_in, conv_a_w, conv_a_b, ln_a_g, ln_a_b, w_a_out, b_a_out, conv_b_w, w_b_out, w_out, final_g, loss_target, m_meta_tokens, m_norm_g, m_w_in, m_conv_a_w, m_conv_a_b, m_ln_a_g, m_ln_a_b, m_w_a_out, m_b_a_out, m_conv_b_w, m_w_b_out, m_w_out, m_final_g, v_meta_tokens, v_norm_g, v_w_in, v_conv_a_w, v_conv_a_b, v_ln_a_g, v_ln_a_b, v_w_a_out, v_b_a_out, v_conv_b_w, v_w_b_out, v_w_out, v_final_g):
    seq, d = x.shape[1], x.shape[2]
    dc = meta_tokens.shape[1]
    sw = w_in.shape[2]
    rsh = w_a_out.shape[1]
    xi, yi, ci = _mesh_pos()
    me = 2 * xi + yi
    pos = jnp.stack([ci, me]).astype(jnp.int32)

    conv_rows = HALO_A + HALO_B + 8
    convs = jnp.concatenate([
        jnp.pad(conv_a_w[0], ((0, HALO_A - CONV_A), (0, 0))),
        jnp.pad(conv_b_w[0], ((0, HALO_B - CONV_B), (0, 0))), jnp.zeros((8, dc), F32)], axis=0)[None]
    w3_own = jnp.stack([w_a_out[0], w_b_out[0], w_out[0]])
    (metag,) = _all_gather([_place_own(meta_tokens[None], pos, F32, "place_meta")])
    meta_full = jnp.transpose(metag[0], (1, 0, 2)).reshape(N_META, N_CHIPS * dc)
    fg2 = final_g.reshape(1, d)

    front = jnp.concatenate([jnp.zeros((TM - N_META, d), F32), meta_full], axis=0)
    xs = x[0]

    proj, (wg_in, wg3, convg) = _proj_fwd(xs, front, norm_g, [_place_own(w_in, pos, BF16, "place_in"),
                                                              _place_own(w3_own, pos, BF16, "place_sq"),
                                                              _place_own(convs, pos, F32, "place_conv")], pos)
    w3 = wg3.reshape(3, N_CHIPS * rsh, d)
    convg = jnp.transpose(convg[0], (1, 0, 2)).reshape(conv_rows, N_CHIPS * dc)
    wa_full = convg[0:HALO_A]
    wb_full = convg[HALO_A:HALO_A + HALO_B]
    ca, cb, ya, yb, abm_t, ds2, h_t, loss8, dfg8 = _mix_fwd(
        xs, front, proj, loss_target[0], w3, wa_full, wb_full, conv_a_b, ln_a_g, ln_a_b, b_a_out, fg2, norm_g)
    dproj, d3, sm = _mix_bwd(ds2, proj, ca, cb, ya, yb, w3, wa_full, wb_full, ln_a_g, ln_a_b)
    cw_sq = _col_block(d, 512)
    per_sq = d // cw_sq
    p32_sq, pbf_sq, _ = _dw_reduced(
        abm_t, d3, cw_sq, 3 * per_sq, lambda t: (t // per_sq, t % per_sq), N_CHIPS,
        (3, N_CHIPS, rsh // 2, d), (None, N_CHIPS, rsh // 2, cw_sq),
        lambda u: (u // per_sq, 0, 0, u % per_sq), [], "dw_square")
    cw_in = _col_block(sw, 768)
    ncol = sw // cw_in
    p32_in, pbf_in, (l_sq,) = _dw_reduced(
        h_t[None], dproj[None], cw_in, N_CHIPS * ncol, lambda t: (0, t), 1,
        (1, N_CHIPS, d // 2, sw), (None, None, d // 2, cw_in), lambda u: (0, u // ncol, 0, u % ncol),
        [pbf_sq], "dw_in")
    grad_x, dfront, dng8, l_in = _dh_bwd(dproj, wg_in, xs, front, ds2, norm_g, pbf_in,
                                         lax.empty(pbf_in.shape, BF16), None)
    half_in = _sum_chips([(p32_in, l_in)], sw, pos, "rs_sum_in")
    half_sq = _sum_chips([(p32_sq, l_sq)], d, pos, "rs_sum_sq")
    tail_row = lax.broadcasted_iota(jnp.int32, (8, d), 0)
    tail = jnp.where(tail_row == 0, dng8, jnp.where(tail_row == 1, dfg8,
                     jnp.where(tail_row == 2, loss8[0, 0], 0.0)))
    block = jnp.concatenate([sm, dfront[TM - N_META:TM], tail], axis=0)
    (other_in, other_sq), red = _sibling_swap([half_in, half_sq], block)
    col = lax.dynamic_slice(red, (0, me * dc), (AR_ROWS, dc))
    g_small = {
        "meta_tokens": col[ROW_DMETA:ROW_DMETA + N_META],
        "norm_g": red[ROW_DNG:ROW_DNG + 1],
        "conv_a_w": col[ROW_DWA:ROW_DWA + CONV_A][None],
        "conv_a_b": red[ROW_DCAB:ROW_DCAB + 1],
        "ln_a_g": red[ROW_DLNG:ROW_DLNG + 1],
        "ln_a_b": red[ROW_DLNB:ROW_DLNB + 1],
        "b_a_out": red[ROW_DBAO:ROW_DBAO + 1],
        "conv_b_w": col[ROW_DWB:ROW_DWB + CONV_B][None],
        "final_g": red[ROW_DFG],
    }

    upd_in = _adam_halves([w_in], [m_w_in], [v_w_in], half_in, other_in, pos, "adam_in")
    upd_sq = _adam_halves([w_a_out, w_b_out, w_out], [m_w_a_out, m_w_b_out, m_w_out],
                          [v_w_a_out, v_w_b_out, v_w_out], half_sq, other_sq, pos, "adam_sq")
    small_w = {"meta_tokens": (meta_tokens, m_meta_tokens, v_meta_tokens), "norm_g": (norm_g, m_norm_g, v_norm_g),
               "conv_a_w": (conv_a_w, m_conv_a_w, v_conv_a_w), "conv_a_b": (conv_a_b, m_conv_a_b, v_conv_a_b),
               "ln_a_g": (ln_a_g, m_ln_a_g, v_ln_a_g), "ln_a_b": (ln_a_b, m_ln_a_b, v_ln_a_b),
               "b_a_out": (b_a_out, m_b_a_out, v_b_a_out), "conv_b_w": (conv_b_w, m_conv_b_w, v_conv_b_w),
               "final_g": (final_g, m_final_g, v_final_g)}
    names_small = list(small_w)
    as2d = lambda t: t.reshape(-1, t.shape[-1])
    upd_small = _adam_small([(as2d(small_w[k][0]), as2d(g_small[k]), as2d(small_w[k][1]), as2d(small_w[k][2]))
                             for k in names_small])

    grads, deltas, new_m, new_v = dict(g_small), {}, {}, {}
    for k, upd in zip(names_small, upd_small):
        deltas[k], new_m[k], new_v[k] = [t.reshape(small_w[k][0].shape) for t in upd]
    grads["w_in"], deltas["w_in"], new_m["w_in"], new_v["w_in"] = upd_in
    for idx, k in enumerate(["w_a_out", "w_b_out", "w_out"]):
        grads[k], deltas[k], new_m[k], new_v[k] = upd_sq[4 * idx:4 * idx + 4]

    loss = red[ROW_LOSS, 0]
    order = ["meta_tokens", "norm_g", "w_in", "conv_a_w", "conv_a_b", "ln_a_g", "ln_a_b", "w_a_out", "b_a_out",
             "conv_b_w", "w_b_out", "w_out", "final_g"]
    return (loss, grad_x[None], *[grads[k] for k in order], *[deltas[k] for k in order],
            *[new_m[k] for k in order], *[new_v[k] for k in order])
```

```python
import functools

import jax
import jax.numpy as jnp
from jax import lax
from jax.experimental import pallas as pl
from jax.experimental.pallas import tpu as pltpu

F32 = jnp.float32
BF16 = jnp.bfloat16
MESH = pl.DeviceIdType.MESH

EPS = 1e-6
N_META = 16
N_SPLIT = 9
CONV_A = 31
CONV_B = 3
HALO_A = 32
HALO_B = 8
SHIFT_ROWS = 24
TM = 256
RB = 64
N_ROW_TILES_BIG = 8
ROW_BLOCK = 256
N_CHIPS = 4
VMEM_LIMIT = 56 * 1024 * 1024
VMEM_LIMIT_BIG = 62 * 1024 * 1024

ADAM_LR = 0.001
ADAM_B1 = 0.9
ADAM_B2 = 0.999
ADAM_EPS = 1e-08
ADAM_WD = 0.01
ADAM_STEP = 10

ROW_DWA = 0
ROW_DWB = 32
ROW_DCAB = 40
ROW_DLNG = 41
ROW_DLNB = 42
ROW_DBAO = 43
SM_ROWS = 48
ROW_DMETA = 48
ROW_DNG = 64
ROW_DFG = 65
ROW_LOSS = 66
AR_ROWS = 72


def _sigmoid(v):
    return 0.5 * jnp.tanh(0.5 * v) + 0.5


def _params(sem, **kw):
    return pltpu.CompilerParams(dimension_semantics=sem, vmem_limit_bytes=VMEM_LIMIT, **kw)


def _rows(rb):
    return pl.ds(pl.multiple_of(rb * RB, RB), RB)


def _mesh_pos():
    x, y, c = lax.axis_index("x"), lax.axis_index("y"), lax.axis_index("c")
    return x, y, c


def _half(ref, j, c):
    h = ref.shape[2] // 2
    return ref.at[:, j, pl.ds(c * h, h), :]


def _place_own(shard, pos, dtype, name):
    s, r, c = shard.shape
    rb = ROW_BLOCK if r % ROW_BLOCK == 0 else r

    def body(pos_ref, x_ref, o_ref):
        o_ref[...] = x_ref[...].astype(dtype)

    return pl.pallas_call(
        body, name=name,
        grid_spec=pltpu.PrefetchScalarGridSpec(
            num_scalar_prefetch=1, grid=(s, r // rb),
            in_specs=[pl.BlockSpec((None, rb, c), lambda si, b, pos_ref: (si, b, 0))],
            out_specs=pl.BlockSpec((None, None, rb, c), lambda si, b, pos_ref: (si, pos_ref[1], b, 0))),
        out_shape=jax.ShapeDtypeStruct((s, N_CHIPS, r, c), dtype),
        compiler_params=_params(("arbitrary",) * 2),
    )(pos, shard)


class _Exchange:
    def __init__(self, sends, recvs):
        self.sends, self.recvs = sends, recvs

    @staticmethod
    def _each(pairs, act):
        for cond, cp in pairs:
            if cond is None:
                act(cp)
            else:
                pl.when(cond)(functools.partial(act, cp))

    def start(self):
        self._each(self.sends, lambda cp: cp.start())

    def finish(self):
        self._each(self.recvs, lambda cp: cp.wait_recv())
        self._each(self.sends, lambda cp: cp.wait_send())


def _chip_exchange(part_ref, land_ref, send_sems, recv_sems, half=None):
    x, y, c = _mesh_pos()
    me = 2 * x + y
    sends, recvs = [], []
    for k, (px, py) in enumerate([(1 - x, y), (x, 1 - y), (1 - x, 1 - y)]):
        sems = dict(send_sem=send_sems.at[k], recv_sem=recv_sems.at[k], device_id=(px, py, c), device_id_type=MESH)
        slot = 2 * px + py if half is None else py
        sends.append((None if half is None else px == half, pltpu.make_async_remote_copy(
            src_ref=part_ref.at[:, slot], dst_ref=land_ref.at[:, me], **sems)))
        landed = land_ref.at[:, 2 * px + py]
        recvs.append((None if half is None else x == half,
                      pltpu.make_async_remote_copy(src_ref=landed, dst_ref=landed, **sems)))
    return _Exchange(sends, recvs)

def _sibling_swap(halves, small):
    n = len(halves)

    def body(*refs):
        ins, small_ref, outs, red_ref = refs[:n], refs[n], refs[n + 1:2 * n + 1], refs[2 * n + 1]
        send_sems, recv_sems = refs[2 * n + 2:2 * n + 4]
        reduce = _SmallAllReduce(small_ref, red_ref, *refs[2 * n + 4:])
        x, y, c = _mesh_pos()
        copies = [pltpu.make_async_remote_copy(
            src_ref=ins[a], dst_ref=outs[a], send_sem=send_sems.at[a], recv_sem=recv_sems.at[a],
            device_id=(x, y, 1 - c), device_id_type=MESH) for a in range(n)]
        reduce.start()
        for cp in copies:
            cp.start()
        reduce.between_chips()
        reduce.finish()
        for cp in copies:
            cp.wait()

    any_spec = pl.BlockSpec(memory_space=pl.ANY)
    vm = pl.BlockSpec(memory_space=pltpu.VMEM)
    outs = pl.pallas_call(
        body, name="rs_swap",
        in_specs=[any_spec] * n + [vm], out_specs=[any_spec] * n + [vm],
        out_shape=[jax.ShapeDtypeStruct(h.shape, h.dtype) for h in halves]
        + [jax.ShapeDtypeStruct(small.shape, F32)],
        scratch_shapes=[pltpu.SemaphoreType.DMA((n,)), pltpu.SemaphoreType.DMA((n,))]
        + _SmallAllReduce.scratch(*small.shape),
    )(*halves, small)
    return outs[:n], outs[n]


class _SmallAllReduce:
    def __init__(self, x_ref, out_ref, sib_ref, part_ref, peers_ref, send_sems, recv_sems):
        self.x_ref, self.out_ref, self.sib_ref, self.part_ref, self.peers_ref = x_ref, out_ref, sib_ref, part_ref, peers_ref
        x, y, c = _mesh_pos()
        self.me = 2 * x + y
        self.swap = pltpu.make_async_remote_copy(
            src_ref=x_ref, dst_ref=sib_ref, send_sem=send_sems.at[0], recv_sem=recv_sems.at[0],
            device_id=(x, y, 1 - c), device_id_type=MESH)
        self.sends, self.recvs = [], []
        for k, (px, py) in enumerate([(1 - x, y), (x, 1 - y), (1 - x, 1 - y)]):
            sems = dict(send_sem=send_sems.at[1 + k], recv_sem=recv_sems.at[1 + k],
                        device_id=(px, py, c), device_id_type=MESH)
            self.sends.append(pltpu.make_async_remote_copy(src_ref=part_ref, dst_ref=peers_ref.at[self.me], **sems))
            landed = peers_ref.at[2 * px + py]
            self.recvs.append(pltpu.make_async_remote_copy(src_ref=landed, dst_ref=landed, **sems))

    @staticmethod
    def scratch(rows, d):
        return [pltpu.VMEM((rows, d), F32), pltpu.VMEM((rows, d), F32), pltpu.VMEM((N_CHIPS, rows, d), F32),
                pltpu.SemaphoreType.DMA((4,)), pltpu.SemaphoreType.DMA((4,))]

    def start(self):
        self.swap.start()

    def between_chips(self):
        self.swap.wait()
        self.part_ref[...] = self.x_ref[...] + self.sib_ref[...]
        self.peers_ref[self.me] = self.part_ref[...]
        for cp in self.sends:
            cp.start()

    def finish(self):
        for cp in self.recvs:
            cp.wait_recv()
        for cp in self.sends:
            cp.wait_send()
        p = self.peers_ref
        self.out_ref[...] = ((p[0] + p[1]) + p[2]) + p[3]


def _sum_chips(parts, cw, pos, name):
    s, _, h, _ = parts[0][0].shape
    hb = min(h, ROW_BLOCK)
    widths = [own.shape[3] // cw for own, _ in parts]
    starts = [sum(widths[:a]) for a in range(len(parts))]

    def body(pos_ref, *refs):
        out_ref = refs[-1]
        n = pl.program_id(2)
        total = None
        for a in range(len(parts)):
            own, l1, l2, l3 = refs[4 * a:4 * a + 4]
            val = ((own[...] + l1[...].astype(F32)) + l2[...].astype(F32)) + l3[...].astype(F32)
            total = val if total is None else jnp.where(n >= starts[a], val, total)
        out_ref[...] = total

    def slot(a, k):
        col = lambda n: jnp.clip(n - starts[a], 0, widths[a] - 1)
        return pl.BlockSpec((None, None, hb, cw),
                            lambda si, b, n, pos_ref: (si, (pos_ref[1] + k) % N_CHIPS, b, col(n)))

    operands, specs = [], []
    for a, (own, landed) in enumerate(parts):
        operands += [own, landed, landed, landed]
        specs += [slot(a, 0), slot(a, 1), slot(a, 2), slot(a, 3)]
    return pl.pallas_call(
        body, name=name,
        grid_spec=pltpu.PrefetchScalarGridSpec(
            num_scalar_prefetch=1, grid=(s, h // hb, sum(widths)), in_specs=specs,
            out_specs=pl.BlockSpec((None, hb, cw), lambda si, b, n, pos_ref: (si, b, n))),
        out_shape=jax.ShapeDtypeStruct((s, h, sum(widths) * cw), F32),
        compiler_params=_params(("arbitrary",) * 3),
    )(pos, *operands)


def _adamw(w, g, m, v):
    m = ADAM_B1 * m + (1.0 - ADAM_B1) * g
    v = ADAM_B2 * v + (1.0 - ADAM_B2) * (g * g)
    m_hat = m / (1.0 - ADAM_B1 ** ADAM_STEP)
    v_hat = v / (1.0 - ADAM_B2 ** ADAM_STEP)
    delta = -ADAM_LR * (m_hat / (jnp.sqrt(v_hat) + ADAM_EPS) + ADAM_WD * w)
    return delta, m, v


def _adam_halves(ws, ms, vs, g_own, g_recv, pos, name):
    n = len(ws)
    _, r, c = ws[0].shape
    h = r // 2
    rb = min(h, ROW_BLOCK)
    nb = h // rb

    def body(pos_ref, *refs):
        w_refs, m_refs, v_refs = refs[:n], refs[n:2 * n], refs[2 * n:3 * n]
        go_ref, gr_ref = refs[3 * n:3 * n + 2]
        outs = refs[3 * n + 2:]
        mine = pl.program_id(0) == pos_ref[0]
        for a in range(n):
            g = jnp.where(mine, go_ref[a], gr_ref[a])
            delta, m, v = _adamw(w_refs[a][...], g, m_refs[a][...], v_refs[a][...])
            outs[4 * a][...], outs[4 * a + 1][...], outs[4 * a + 2][...], outs[4 * a + 3][...] = g, delta, m, v

    spec_w = pl.BlockSpec((None, rb, c), lambda hf, b, pos_ref: (0, hf * nb + b, 0))
    spec_g = pl.BlockSpec((n, rb, c), lambda hf, b, pos_ref: (0, b, 0))
    return pl.pallas_call(
        body, name=name,
        grid_spec=pltpu.PrefetchScalarGridSpec(
            num_scalar_prefetch=1, grid=(2, nb), in_specs=[spec_w] * (3 * n) + [spec_g] * 2,
            out_specs=[spec_w] * (4 * n)),
        out_shape=[jax.ShapeDtypeStruct((1, r, c), F32)] * (4 * n),
        compiler_params=_params(("arbitrary",) * 2),
    )(pos, *ws, *ms, *vs, g_own, g_recv)


def _adam_small(items):
    n = len(items)

    def body(*refs):
        ins, outs = refs[:4 * n], refs[4 * n:]
        for a in range(n):
            w_ref, g_ref, m_ref, v_ref = ins[4 * a:4 * a + 4]
            d, m, v = _adamw(w_ref[...], g_ref[...], m_ref[...], v_ref[...])
            outs[3 * a][...] = d
            outs[3 * a + 1][...] = m
            outs[3 * a + 2][...] = v

    vm = pl.BlockSpec(memory_space=pltpu.VMEM)
    flat = [t for it in items for t in it]
    outs = pl.pallas_call(
        body, name="adam_small", in_specs=[vm] * (4 * n), out_specs=[vm] * (3 * n),
        out_shape=[jax.ShapeDtypeStruct(it[0].shape, F32) for it in items for _ in range(3)],
    )(*flat)
    return [tuple(outs[3 * a:3 * a + 3]) for a in range(n)]


def _shard_of_step(js, me):
    flip = jnp.where(js == 1, 2, jnp.where(js == 2, 1, jnp.where(js == 3, 3, 0)))
    return lax.bitwise_xor(me, flip)


def _big_row_spec(seq, tmb, d, tile_of):
    return pl.BlockSpec((pl.Element(tmb), pl.Element(d)),
                        lambda *args: (pl.multiple_of(jnp.minimum(tile_of(*args) * tmb, seq - tmb), 8), 0))


def _big_row_tile(x_ref, front_ref, i):
    rows = x_ref[...]
    last = jnp.concatenate([rows[TM:], front_ref[...]], axis=0)
    return jnp.where(i == N_ROW_TILES_BIG - 1, last, rows)


def _proj_fwd(x, meta, norm_g, bufs, pos):
    seq, d = x.shape
    tp = seq + TM
    _, nsh, _, sw = bufs[0].shape
    dc = meta.shape[1]
    tmb = tp // N_ROW_TILES_BIG
    assert tmb >= TM and tp == tmb * N_ROW_TILES_BIG
    n = len(bufs)

    def body(pos_ref, x_ref, meta_ref, g_ref, *refs):
        proj_ref, front_ref = refs[n], refs[n + 1]
        gbufs = refs[n + 2:2 * n + 2]
        wbuf, wsems, send_sems, recv_sems, metas, msend, mrecv = refs[2 * n + 2:]
        x, y, c = _mesh_pos()
        me = 2 * x + y
        sibling = (x, y, 1 - c)
        chips = [(1 - x, y), (x, 1 - y), (1 - x, 1 - y)]
        js, i = pl.program_id(0), pl.program_id(1)

        def meta_copy(k, chip):
            return pltpu.make_async_remote_copy(
                src_ref=metas.at[chip], dst_ref=metas.at[chip], send_sem=msend.at[k], recv_sem=mrecv.at[k],
                device_id=(*chips[k], c), device_id_type=MESH)

        @pl.when((js == 0) & (i == 0))
        def _():
            metas[me] = meta_ref[...]
            for k in range(3):
                meta_copy(k, me).start()
            front_ref[...] = jnp.zeros_like(front_ref)

        @pl.when((js == 0) & (i == N_ROW_TILES_BIG - 1))
        def _():
            for k, (px, py) in enumerate(chips):
                meta_copy(k, 2 * px + py).wait_recv()
            for q in range(N_CHIPS):
                front_ref[TM - N_META:TM, q * dc:(q + 1) * dc] = metas[q]

        def remote(a, k, piece, to):
            return pltpu.make_async_remote_copy(
                src_ref=piece, dst_ref=piece, send_sem=send_sems.at[6 * a + k],
                recv_sem=recv_sems.at[6 * a + k], device_id=to, device_id_type=MESH)

        def fetch(chip, step):
            return pltpu.make_async_copy(gbufs[0].at[0, chip], wbuf.at[step % 2], wsems.at[step % 2])

        chip_ids = [2 * px + py for px, py in chips]
        relayed_chip = jnp.where(c == 0, chip_ids[0], chip_ids[1])
        relay_to = (jnp.where(c == 0, x, 1 - x), jnp.where(c == 0, 1 - y, y), c)

        def own_piece(a, k):
            return remote(a, k, _half(gbufs[a], me, c), (*chips[k], c))

        def relay(a):
            return remote(a, 2, _half(gbufs[a], relayed_chip, c), relay_to)

        def to_sibling(a, k, core):
            return remote(a, 3 + k, _half(gbufs[a], chip_ids[k], core), sibling)

        def landed(a, k):
            return remote(a, k, _half(gbufs[a], chip_ids[k], c), (*chips[k], c))

        def take_neighbours(a):
            landed(a, 0).wait_recv()
            landed(a, 1).wait_recv()
            relay(a).start()
            for k in range(2):
                to_sibling(a, k, c).start()
            for k in range(2):
                to_sibling(a, k, 1 - c).wait_recv()

        def take_diagonal(a):
            landed(a, 2).wait_recv()
            to_sibling(a, 2, c).start()
            to_sibling(a, 2, 1 - c).wait_recv()

        @pl.when((js == 0) & (i == 0))
        def _():
            for a in range(n):
                for k in range(2):
                    own_piece(a, k).start()
            fetch(me, 0).start()
            fetch(me, 0).wait()

        @pl.when((js == 1) & (i == 0))
        def _():
            take_neighbours(0)
            fetch(chip_ids[0], 1).start()
            fetch(chip_ids[0], 1).wait()

        @pl.when((js == 1) & (i == N_ROW_TILES_BIG - 2))
        def _():
            fetch(chip_ids[1], 2).start()

        @pl.when((js == 2) & (i == 0))
        def _():
            fetch(chip_ids[1], 2).wait()

        @pl.when((js == 2) & (i == 1))
        def _():
            for a in range(1, n):
                take_neighbours(a)

        @pl.when((js == 2) & (i == N_ROW_TILES_BIG - 2))
        def _():
            take_diagonal(0)
            fetch(chip_ids[2], 3).start()

        @pl.when((js == 3) & (i == 0))
        def _():
            fetch(chip_ids[2], 3).wait()

        s = _big_row_tile(x_ref, front_ref, i)
        r = lax.rsqrt(jnp.mean(s * s, axis=-1, keepdims=True) + EPS)
        h = (s * r * g_ref[...]).astype(BF16)
        proj_ref[...] = jnp.dot(h, wbuf[js % 2], preferred_element_type=F32).astype(BF16)

        @pl.when((js == nsh - 1) & (i == N_ROW_TILES_BIG - 1))
        def _():
            for a in range(1, n):
                take_diagonal(a)
            for a in range(n):
                for k in range(2):
                    own_piece(a, k).wait_send()
                relay(a).wait_send()
                for k in range(3):
                    to_sibling(a, k, c).wait_send()
            for k in range(3):
                meta_copy(k, me).wait_send()

    any_spec = pl.BlockSpec(memory_space=pl.ANY)
    outs = pl.pallas_call(
        body, name="f1_proj",
        grid_spec=pltpu.PrefetchScalarGridSpec(
            num_scalar_prefetch=1, grid=(nsh, N_ROW_TILES_BIG),
            in_specs=[_big_row_spec(seq, tmb, d, lambda js, i, pos_ref: i),
                      pl.BlockSpec(meta.shape, lambda js, i, pos_ref: (0, 0)),
                      pl.BlockSpec((1, d), lambda js, i, pos_ref: (0, 0))] + [any_spec] * n,
            out_specs=[pl.BlockSpec((tmb, sw), lambda js, i, pos_ref: (i, _shard_of_step(js, pos_ref[1]))),
                       pl.BlockSpec((TM, d), lambda js, i, pos_ref: (0, 0))] + [any_spec] * n,
            scratch_shapes=[pltpu.VMEM((2, d, sw), BF16), pltpu.SemaphoreType.DMA((2,)),
                            pltpu.SemaphoreType.DMA((6 * n,)), pltpu.SemaphoreType.DMA((6 * n,)),
                            pltpu.VMEM((N_CHIPS,) + meta.shape, F32),
                            pltpu.SemaphoreType.DMA((3,)), pltpu.SemaphoreType.DMA((3,))]),
        out_shape=[jax.ShapeDtypeStruct((tp, nsh * sw), BF16), jax.ShapeDtypeStruct((TM, d), F32)]
        + [jax.ShapeDtypeStruct(b.shape, b.dtype) for b in bufs],
        input_output_aliases={4 + a: 2 + a for a in range(n)},
        compiler_params=_params(("arbitrary", "arbitrary")),
    )(pos, x, meta, norm_g, *bufs)
    return outs[0], outs[1], outs[2:]


def _dh_bwd(dproj, wg_in, x, front, ds2, norm_g, part, land, half):
    seq, d = x.shape
    tp = seq + TM
    _, nsh, _, sw = wg_in.shape
    tmb = tp // N_ROW_TILES_BIG
    tail = tmb - TM
    last = N_ROW_TILES_BIG - 1

    def body(dp_ref, w_hbm, x_ref, front_ref, ds2_ref, g_ref, part_ref, _, gx_hbm, dfront_ref, dng_ref, land_ref,
             wbuf, gacc, dsbuf, wsem, osems, send_sems, recv_sems):
        exchange = _chip_exchange(part_ref, land_ref, send_sems, recv_sems, half)
        i = pl.program_id(0)

        def x_rows_out(step):
            return pltpu.make_async_copy(dsbuf.at[step % 2], gx_hbm.at[pl.ds(step * tmb, tmb), :], osems.at[step % 2])

        last_out = pltpu.make_async_copy(dsbuf.at[last % 2, pl.ds(0, tail), :],
                                         gx_hbm.at[pl.ds(last * tmb, tail), :], osems.at[last % 2])

        @pl.when(i == 0)
        def _():
            exchange.start()
            gacc[...] = jnp.zeros_like(gacc)
            whole = pltpu.make_async_copy(w_hbm.at[0], wbuf, wsem)
            whole.start()
            whole.wait()

        dh = None
        for j in range(nsh):
            part = lax.dot_general(dp_ref[:, j * sw:(j + 1) * sw], wbuf[j], (((1,), (1,)), ((), ())),
                                   preferred_element_type=F32)
            dh = part if dh is None else dh + part
        s = _big_row_tile(x_ref, front_ref, i)
        r = lax.rsqrt(jnp.mean(s * s, axis=-1, keepdims=True) + EPS)
        gacc[...] += (dh * s * r).reshape(tmb // 8, 8, d).sum(axis=0)
        t = dh * g_ref[...]

        @pl.when(i >= 2)
        def _():
            x_rows_out(i - 2).wait()

        dsbuf[i % 2] = ds2_ref[...] + r * t - s * (r * r * r) * jnp.mean(t * s, axis=-1, keepdims=True)

        @pl.when(i < last)
        def _():
            x_rows_out(i).start()

        @pl.when(i == last)
        def _():
            last_out.start()
            dfront_ref[...] = dsbuf[last % 2, tail:, :]
            dng_ref[...] = jnp.broadcast_to(jnp.sum(gacc[...], axis=0, keepdims=True), (8, d))
            exchange.finish()
            x_rows_out(last - 1).wait()
            last_out.wait()

    any_spec = pl.BlockSpec(memory_space=pl.ANY)
    return pl.pallas_call(
        body, name="b2_dh", grid=(N_ROW_TILES_BIG,),
        in_specs=[pl.BlockSpec((tmb, nsh * sw), lambda i: (i, 0)), any_spec,
                  _big_row_spec(seq, tmb, d, lambda i: i),
                  pl.BlockSpec((TM, d), lambda i: (0, 0)),
                  pl.BlockSpec((tmb, d), lambda i: (i, 0)),
                  pl.BlockSpec((1, d), lambda i: (0, 0)), any_spec, any_spec],
        out_specs=[any_spec, pl.BlockSpec((TM, d), lambda i: (0, 0)),
                   pl.BlockSpec((8, d), lambda i: (0, 0)), any_spec],
        out_shape=[jax.ShapeDtypeStruct((seq, d), F32), jax.ShapeDtypeStruct((TM, d), F32),
                   jax.ShapeDtypeStruct((8, d), F32), jax.ShapeDtypeStruct(land.shape, land.dtype)],
        input_output_aliases={7: 3},
        scratch_shapes=[pltpu.VMEM((nsh, d, sw), BF16), pltpu.VMEM((8, d), F32), pltpu.VMEM((2, tmb, d), F32),
                        pltpu.SemaphoreType.DMA, pltpu.SemaphoreType.DMA((2,)),
                        pltpu.SemaphoreType.DMA((3,)), pltpu.SemaphoreType.DMA((3,))],
        compiler_params=pltpu.CompilerParams(dimension_semantics=("arbitrary",),
                                             vmem_limit_bytes=VMEM_LIMIT_BIG),
    )(dproj, wg_in, x, front, ds2, norm_g, part, land)


def _col_block(width, cap):
    return max(b for b in range(128, cap + 1, 128) if width % b == 0)


def _dw_reduced(lhs_t, rhs, cw, nblk, operands, groups, out_dims, out_block, out_index, carried, name):
    na, d, tp = lhs_t.shape
    rg = d // groups
    hh = rg // 2
    nc = len(carried)

    def body(*refs):
        l_ref, r_ref = refs[:2]
        part_refs = refs[2:2 + nc]
        p32_ref, pbf_ref = refs[2 + nc:4 + nc]
        land_refs = refs[4 + nc:4 + 2 * nc]
        res, rbuf, send_sems, recv_sems = refs[4 + 2 * nc:8 + 2 * nc]
        xsems = refs[8 + 2 * nc:]
        exchanges = [_chip_exchange(part_refs[e], land_refs[e], xsems[2 * e], xsems[2 * e + 1]) for e in range(nc)]
        exchange = _Exchange([s for ex in exchanges for s in ex.sends], [r for ex in exchanges for r in ex.recvs])
        x, y, c = _mesh_pos()
        t = pl.program_id(0)
        u = jnp.maximum(t - 1, 0)

        def to_sibling(blk):
            return pltpu.make_async_remote_copy(
                src_ref=res.at[blk % 2, :, pl.ds((1 - c) * hh, hh), :], dst_ref=rbuf.at[blk % 2],
                send_sem=send_sems.at[blk], recv_sem=recv_sems.at[blk],
                device_id=(x, y, 1 - c), device_id_type=MESH)

        @pl.when(t == 0)
        def _():
            exchange.start()

        @pl.when(t < nblk)
        def _():
            res[t % 2] = jnp.dot(l_ref[...], r_ref[...], preferred_element_type=F32).reshape(groups, rg, cw)

        @pl.when(t >= 1)
        def _():
            to_sibling(u).wait_recv()
            p = res[u % 2, :, pl.ds(c * hh, hh), :] + rbuf[u % 2]
            p32_ref[...] = p.reshape(p32_ref.shape)
            pbf_ref[...] = p.reshape(pbf_ref.shape).astype(BF16)

        @pl.when(t < nblk)
        def _():
            to_sibling(t).start()

        @pl.when(t >= 1)
        def _():
            to_sibling(u).wait_send()

        @pl.when(t == nblk)
        def _():
            exchange.finish()

    any_spec = pl.BlockSpec(memory_space=pl.ANY)
    last = nblk - 1
    out_spec = pl.BlockSpec(out_block, lambda t: out_index(jnp.maximum(t - 1, 0)))
    outs = pl.pallas_call(
        body, name=name, grid=(nblk + 1,),
        in_specs=[pl.BlockSpec((None, d, tp), lambda t: (operands(jnp.minimum(t, last))[0], 0, 0)),
                  pl.BlockSpec((None, tp, cw), lambda t: (operands(jnp.minimum(t, last))[0], 0,
                                                          operands(jnp.minimum(t, last))[1]))]
        + [any_spec] * nc,
        out_specs=[out_spec, out_spec] + [any_spec] * nc,
        out_shape=[jax.ShapeDtypeStruct(out_dims, F32), jax.ShapeDtypeStruct(out_dims, BF16)]
        + [jax.ShapeDtypeStruct(e.shape, e.dtype) for e in carried],
        scratch_shapes=[pltpu.VMEM((2, groups, rg, cw), F32), pltpu.VMEM((2, groups, hh, cw), F32),
                        pltpu.SemaphoreType.DMA((nblk,)), pltpu.SemaphoreType.DMA((nblk,))]
        + [pltpu.SemaphoreType.DMA((3,)), pltpu.SemaphoreType.DMA((3,))] * nc,
        compiler_params=_params(("arbitrary",)),
    )(lhs_t, rhs, *carried)
    return outs[0], outs[1], outs[2:]


def _conv_a_taps(first_lag, last_lag):
    out = []
    for r in range(8):
        taps = [(q, 8 * q + r) for q in range(5) if first_lag <= 8 * q + r <= last_lag]
        if taps:
            out.append((r, taps))
    return out


def _tile_block(i, nt):
    return jnp.where(i == 0, nt - 1, i - 1)


def _mix_fwd(x, front, proj, target, w3, wa, wb, conv_a_b, ln_g, ln_b, b_a_out, final_g, norm_g):
    seq, d = x.shape
    tp = seq + TM
    nt = tp // TM
    nrb = TM // RB
    shl = TM + SHIFT_ROWS

    def body(x_ref, front_ref, proj_ref, tgt_ref, w3_ref, wa_ref, wb_ref, cab_ref, lng_ref, lnb_ref, bao_ref, fg_ref,
             ng_ref, ca_ref, cb_ref, ya_ref, yb_ref, abmt_ref, ds2_ref, ht_ref, loss_ref, dfg_ref,
             abm_ref, ext_a, ext_b, sh, s2_s, lacc, gacc):
        i = pl.program_id(0)

        def split(k, rows):
            return proj_ref[rows, k * d:(k + 1) * d].astype(F32)

        def s_tile():
            return jnp.where(i == 0, front_ref[...], x_ref[...])

        s_in = s_tile()
        h = s_in * lax.rsqrt(jnp.mean(s_in * s_in, axis=-1, keepdims=True) + EPS) * ng_ref[...]
        ht_ref[...] = h.astype(BF16).T

        @pl.when(i == 0)
        def _():
            ext_a[0:HALO_A, :] = jnp.zeros((HALO_A, d), F32)
            ext_b[0:HALO_B, :] = jnp.zeros((HALO_B, d), F32)
            lacc[...] = jnp.zeros_like(lacc)
            gacc[...] = jnp.zeros_like(gacc)

        def conv_in(rb, carry):
            rows = _rows(rb)
            ua0 = split(0, rows) * _sigmoid(split(1, rows))
            ext_a[pl.ds(pl.multiple_of(HALO_A + rb * RB, 8), RB), :] = ua0
            ext_b[pl.ds(pl.multiple_of(HALO_B + rb * RB, 8), RB), :] = split(4, rows) * split(5, rows)
            ca_ref[rows, :] = jnp.broadcast_to(cab_ref[...], (RB, d))
            return carry
        lax.fori_loop(0, nrb, conv_in, 0)

        for r, taps in _conv_a_taps(HALO_A - CONV_A + 1, HALO_A):
            if r == 0:
                src = ext_a
            else:
                sh[...] = ext_a[r:r + shl, :]
                src = sh

            def conv_acc(rb, carry, src=src, taps=taps):
                rows = _rows(rb)
                acc = ca_ref[rows, :]
                for q, lag in taps:
                    k = lag - (HALO_A - CONV_A + 1)
                    acc = acc + src[pl.ds(pl.multiple_of(rb * RB + 8 * q, 8), RB), :] * wa_ref[k:k + 1, :]
                ca_ref[rows, :] = acc
                return carry
            lax.fori_loop(0, nrb, conv_acc, 0)
        ext_a[0:HALO_A, :] = ext_a[TM:TM + HALO_A, :]

        cb_ref[...] = ext_b[HALO_B:HALO_B + TM, :] * wb_ref[2:3, :]
        for k in range(CONV_B - 1):
            off = HALO_B - CONV_B + 1 + k
            sh[0:TM, :] = ext_b[off:off + TM, :]
            cb_ref[...] += sh[0:TM, :] * wb_ref[k:k + 1, :]
        ext_b[0:HALO_B, :] = ext_b[TM:TM + HALO_B, :]

        def branches(rb, carry):
            rows = _rows(rb)
            ca = ca_ref[rows, :]
            mu = jnp.mean(ca, axis=-1, keepdims=True)
            xc = ca - mu
            rstd = lax.rsqrt(jnp.mean(xc * xc, axis=-1, keepdims=True) + EPS)
            ln = xc * rstd * lng_ref[...] + lnb_ref[...]
            ua = ln * _sigmoid(ln)
            a_z = split(2, rows)
            abm_ref[0, rows, :] = (ua * (a_z * _sigmoid(a_z))).astype(BF16)
            b_z = split(6, rows)
            ub = split(3, rows) * cb_ref[rows, :]
            abm_ref[1, rows, :] = (ub * (b_z * _sigmoid(b_z))).astype(BF16)
            return carry
        lax.fori_loop(0, nrb, branches, 0)

        ya_ref[...] = jnp.dot(abm_ref[0], w3_ref[0], preferred_element_type=F32) + bao_ref[...]
        yb_ref[...] = jnp.dot(abm_ref[1], w3_ref[1], preferred_element_type=F32)

        def merge(rb, carry):
            rows = _rows(rb)
            m = _sigmoid(split(7, rows)) * ya_ref[rows, :] + _sigmoid(split(8, rows)) * yb_ref[rows, :]
            abm_ref[2, rows, :] = m.astype(BF16)
            return carry
        lax.fori_loop(0, nrb, merge, 0)

        s2_s[...] = s_tile() + jnp.dot(abm_ref[2], w3_ref[2], preferred_element_type=F32)
        for k in range(3):
            abmt_ref[k] = abm_ref[k].T
        live = (i > 0).astype(F32)

        def head(rb, carry):
            rows = _rows(rb)
            s2 = s2_s[rows, :]
            r2 = lax.rsqrt(jnp.mean(s2 * s2, axis=-1, keepdims=True) + EPS)
            diff = (s2 * r2 * fg_ref[...] - tgt_ref[rows, :]) * live
            lacc[...] += diff * diff
            dy = diff * (1.0 / d)
            gacc[...] += (dy * s2 * r2).reshape(RB // 8, 8, d).sum(axis=0)
            t = dy * fg_ref[...]
            ds2_ref[rows, :] = r2 * t - s2 * (r2 * r2 * r2) * jnp.mean(t * s2, axis=-1, keepdims=True)
            return carry
        lax.fori_loop(0, nrb, head, 0)

        @pl.when(i == nt - 1)
        def _():
            loss_ref[...] = jnp.broadcast_to(0.5 * jnp.sum(lacc[...]) * (1.0 / d), (8, 128))
            dfg_ref[...] = jnp.broadcast_to(jnp.sum(gacc[...], axis=0, keepdims=True), (8, d))

    row_f32 = pl.BlockSpec((TM, d), lambda i: (_tile_block(i, nt), 0))
    x_rows = pl.BlockSpec((TM, d), lambda i: (jnp.maximum(i - 1, 0), 0))
    const = lambda shape: pl.BlockSpec(shape, lambda i: (0,) * len(shape))
    return pl.pallas_call(
        body, name="f2_mix", grid=(nt,),
        in_specs=[x_rows, const((TM, d)),
                  pl.BlockSpec((TM, N_SPLIT * d), lambda i: (_tile_block(i, nt), 0)),
                  x_rows,
                  const((3, d, d)), const(wa.shape), const(wb.shape)] + [const((1, d))] * 6,
        out_specs=[row_f32, row_f32, row_f32, row_f32,
                   pl.BlockSpec((3, d, TM), lambda i: (0, 0, _tile_block(i, nt))),
                   row_f32, pl.BlockSpec((d, TM), lambda i: (0, _tile_block(i, nt))),
                   const((8, 128)), const((8, d))],
        out_shape=[jax.ShapeDtypeStruct((tp, d), F32)] * 4
        + [jax.ShapeDtypeStruct((3, d, tp), BF16), jax.ShapeDtypeStruct((tp, d), F32),
           jax.ShapeDtypeStruct((d, tp), BF16),
           jax.ShapeDtypeStruct((8, 128), F32), jax.ShapeDtypeStruct((8, d), F32)],
        scratch_shapes=[pltpu.VMEM((3, TM, d), BF16),
                        pltpu.VMEM((HALO_A + TM, d), F32), pltpu.VMEM((HALO_B + TM, d), F32),
                        pltpu.VMEM((shl, d), F32), pltpu.VMEM((TM, d), F32),
                        pltpu.VMEM((RB, d), F32), pltpu.VMEM((8, d), F32)],
        compiler_params=_params(("arbitrary",)),
    )(x, front, proj, target, w3, wa, wb, conv_a_b, ln_g, ln_b, b_a_out, final_g, norm_g)


def _mix_bwd(ds2, proj, ca, cb, ya, yb, w3, wa, wb, ln_g, ln_b):
    tp, d = ds2.shape
    nt = tp // TM
    nrb = TM // RB
    shl = TM + SHIFT_ROWS
    nt_dims = (((1,), (1,)), ((), ()))

    def body(ds2_ref, proj_ref, ca_ref, cb_ref, ya_ref, yb_ref, w3_ref, wa_ref, wb_ref, lng_ref, lnb_ref,
             dproj_ref, d3_ref, sm_ref, ext_d, ext_e, sh, dm_s, dpa_s, dpb_s, dua0_s, acc):
        step = pl.program_id(0)

        def split(k, rows):
            return proj_ref[rows, k * d:(k + 1) * d].astype(F32)

        def put(k, rows, val):
            dproj_ref[rows, k * d:(k + 1) * d] = val.astype(BF16)

        def accum(row, val):
            acc[row] += val.reshape(RB // 8, 8, d).sum(axis=0)

        @pl.when(step == 0)
        def _():
            ext_d[TM:TM + HALO_A, :] = jnp.zeros((HALO_A, d), F32)
            ext_e[TM:TM + HALO_B, :] = jnp.zeros((HALO_B, d), F32)
            acc[...] = jnp.zeros_like(acc)

        d3_ref[2] = ds2_ref[...].astype(BF16)
        dm_s[...] = lax.dot_general(d3_ref[2], w3_ref[2], nt_dims, preferred_element_type=F32)

        def gates(rb, carry):
            rows = _rows(rb)
            dm = dm_s[rows, :]
            sa = _sigmoid(split(7, rows))
            sb = _sigmoid(split(8, rows))
            ya_v = ya_ref[rows, :]
            yb_v = yb_ref[rows, :]
            put(7, rows, dm * ya_v * sa * (1.0 - sa))
            put(8, rows, dm * yb_v * sb * (1.0 - sb))
            dya = dm * sa
            accum(ROW_DBAO, dya)
            d3_ref[0, rows, :] = dya.astype(BF16)
            d3_ref[1, rows, :] = (dm * sb).astype(BF16)
            return carry
        lax.fori_loop(0, nrb, gates, 0)

        dpa_s[...] = lax.dot_general(d3_ref[0], w3_ref[0], nt_dims, preferred_element_type=F32)
        dpb_s[...] = lax.dot_general(d3_ref[1], w3_ref[1], nt_dims, preferred_element_type=F32)

        def branches(rb, carry):
            rows = _rows(rb)
            ca_v = ca_ref[rows, :]
            mu = jnp.mean(ca_v, axis=-1, keepdims=True)
            xc = ca_v - mu
            rstd = lax.rsqrt(jnp.mean(xc * xc, axis=-1, keepdims=True) + EPS)
            xhat = xc * rstd
            ln = xhat * lng_ref[...] + lnb_ref[...]
            sl = _sigmoid(ln)
            ua = ln * sl
            a_z = split(2, rows)
            sz = _sigmoid(a_z)
            dpa = dpa_s[rows, :]
            put(2, rows, dpa * ua * (sz * (1.0 + a_z * (1.0 - sz))))
            dln = dpa * (a_z * sz) * (sl * (1.0 + ln * (1.0 - sl)))
            accum(ROW_DLNG, dln * xhat)
            accum(ROW_DLNB, dln)
            dxh = dln * lng_ref[...]
            dca = rstd * (dxh - jnp.mean(dxh, axis=-1, keepdims=True)
                          - xhat * jnp.mean(dxh * xhat, axis=-1, keepdims=True))
            accum(ROW_DCAB, dca)
            ext_d[rows, :] = dca
            dua0_s[rows, :] = jnp.zeros((RB, d), F32)
            dm_s[rows, :] = split(0, rows) * _sigmoid(split(1, rows))
            b_z = split(6, rows)
            szb = _sigmoid(b_z)
            dpb = dpb_s[rows, :]
            b_b = split(3, rows)
            cb_v = cb_ref[rows, :]
            put(6, rows, dpb * (b_b * cb_v) * (szb * (1.0 + b_z * (1.0 - szb))))
            dub = dpb * (b_z * szb)
            put(3, rows, dub * cb_v)
            ext_e[rows, :] = dub * b_b
            return carry
        lax.fori_loop(0, nrb, branches, 0)

        for r, taps in _conv_a_taps(0, CONV_A - 1):
            if r == 0:
                src = ext_d
            else:
                sh[...] = ext_d[r:r + shl, :]
                src = sh

            def conv_t(rb, carry, src=src, taps=taps):
                rows = _rows(rb)
                ua0 = dm_s[rows, :]
                dua0 = dua0_s[rows, :]
                for q, lag in taps:
                    k = CONV_A - 1 - lag
                    slab = src[pl.ds(pl.multiple_of(rb * RB + 8 * q, 8), RB), :]
                    dua0 = dua0 + slab * wa_ref[k:k + 1, :]
                    accum(ROW_DWA + k, slab * ua0)
                dua0_s[rows, :] = dua0
                return carry
            lax.fori_loop(0, nrb, conv_t, 0)
        ext_d[TM:TM + HALO_A, :] = ext_d[0:HALO_A, :]

        dpb_s[...] = ext_e[0:TM, :] * wb_ref[CONV_B - 1:CONV_B, :]
        for lag in range(CONV_B):
            k = CONV_B - 1 - lag
            if lag > 0:
                sh[0:TM, :] = ext_e[lag:lag + TM, :]
                dpb_s[...] += sh[0:TM, :] * wb_ref[k:k + 1, :]
            src = ext_e if lag == 0 else sh

            def conv_b_w(rb, carry, src=src, k=k):
                rows = _rows(rb)
                accum(ROW_DWB + k, src[rows, :] * (split(4, rows) * split(5, rows)))
                return carry
            lax.fori_loop(0, nrb, conv_b_w, 0)
        ext_e[TM:TM + HALO_B, :] = ext_e[0:HALO_B, :]

        def inputs(rb, carry):
            rows = _rows(rb)
            dua0 = dua0_s[rows, :]
            a_val = split(0, rows)
            sg = _sigmoid(split(1, rows))
            put(0, rows, dua0 * sg)
            put(1, rows, dua0 * a_val * sg * (1.0 - sg))
            dcbin = dpb_s[rows, :]
            put(4, rows, dcbin * split(5, rows))
            put(5, rows, dcbin * split(4, rows))
            return carry
        lax.fori_loop(0, nrb, inputs, 0)

        @pl.when(step == nt - 1)
        def _():
            for row in range(SM_ROWS):
                sm_ref[row:row + 1, :] = jnp.sum(acc[row], axis=0, keepdims=True)

    rev = lambda i: (_tile_block(nt - 1 - i, nt), 0)
    row_f32 = pl.BlockSpec((TM, d), rev)
    const = lambda shape: pl.BlockSpec(shape, lambda i: (0,) * len(shape))
    return pl.pallas_call(
        body, name="b1_mix", grid=(nt,),
        in_specs=[row_f32, pl.BlockSpec((TM, N_SPLIT * d), rev), row_f32, row_f32, row_f32, row_f32,
                  const((3, d, d)), const(wa.shape), const(wb.shape), const((1, d)), const((1, d))],
        out_specs=[pl.BlockSpec((TM, N_SPLIT * d), rev),
                   pl.BlockSpec((3, TM, d), lambda i: (0, _tile_block(nt - 1 - i, nt), 0)),
                   const((SM_ROWS, d))],
        out_shape=[jax.ShapeDtypeStruct((tp, N_SPLIT * d), BF16), jax.ShapeDtypeStruct((3, tp, d), BF16),
                   jax.ShapeDtypeStruct((SM_ROWS, d), F32)],
        scratch_shapes=[pltpu.VMEM((TM + HALO_A, d), F32), pltpu.VMEM((TM + HALO_B, d), F32),
                        pltpu.VMEM((shl, d), F32), pltpu.VMEM((TM, d), F32), pltpu.VMEM((TM, d), F32),
                        pltpu.VMEM((TM, d), F32), pltpu.VMEM((TM, d), F32),
                        pltpu.VMEM((SM_ROWS, 8, d), F32)],
        compiler_params=_params(("arbitrary",)),
    )(ds2, proj, ca, cb, ya, yb, w3, wa, wb, ln_g, ln_b)


def kernel(x, meta_tokens, norm_g, w_in, conv_a_w, conv_a_b, ln_a_g, ln_a_b, w_a_out, b_a_out, conv_b_w, w_b_out, w_out, final_g, loss_target, m_meta_tokens, m_norm_g, m_w_in, m_conv_a_w, m_conv_a_b, m_ln_a_g, m_ln_a_b, m_w_a_out, m_b_a_out, m_conv_b_w, m_w_b_out, m_w_out, m_final_g, v_meta_tokens, v_norm_g, v_w_in, v_conv_a_w, v_conv_a_b, v_ln_a_g, v_ln_a_b, v_w_a_out, v_b_a_out, v_conv_b_w, v_w_b_out, v_w_out, v_final_g):
    seq, d = x.shape[1], x.shape[2]
    dc = meta_tokens.shape[1]
    sw = w_in.shape[2]
    rsh = w_a_out.shape[1]
    xi, yi, ci = _mesh_pos()
    me = 2 * xi + yi
    pos = jnp.stack([ci, me]).astype(jnp.int32)

    conv_rows = HALO_A + HALO_B + 8
    convs = jnp.concatenate([
        jnp.pad(conv_a_w[0], ((0, HALO_A - CONV_A), (0, 0))),
        jnp.pad(conv_b_w[0], ((0, HALO_B - CONV_B), (0, 0))), jnp.zeros((8, dc), F32)], axis=0)[None]
    w3_own = jnp.stack([w_a_out[0], w_b_out[0], w_out[0]])
    fg2 = final_g.reshape(1, d)
    xs = x[0]

    proj, front, (wg_in, wg3, convg) = _proj_fwd(xs, meta_tokens, norm_g,
                                                 [_place_own(w_in, pos, BF16, "place_in"),
                                                  _place_own(w3_own, pos, BF16, "place_sq"),
                                                  _place_own(convs, pos, F32, "place_conv")], pos)
    w3 = wg3.reshape(3, N_CHIPS * rsh, d)
    convg = jnp.transpose(convg[0], (1, 0, 2)).reshape(conv_rows, N_CHIPS * dc)
    wa_full = convg[0:HALO_A]
    wb_full = convg[HALO_A:HALO_A + HALO_B]
    ca, cb, ya, yb, abm_t, ds2, h_t, loss8, dfg8 = _mix_fwd(
        xs, front, proj, loss_target[0], w3, wa_full, wb_full, conv_a_b, ln_a_g, ln_a_b, b_a_out, fg2, norm_g)
    dproj, d3, sm = _mix_bwd(ds2, proj, ca, cb, ya, yb, w3, wa_full, wb_full, ln_a_g, ln_a_b)
    cw_sq = _col_block(d, 512)
    per_sq = d // cw_sq
    p32_sq, pbf_sq, _ = _dw_reduced(
        abm_t, d3, cw_sq, 3 * per_sq, lambda t: (t // per_sq, t % per_sq), N_CHIPS,
        (3, N_CHIPS, rsh // 2, d), (None, N_CHIPS, rsh // 2, cw_sq),
        lambda u: (u // per_sq, 0, 0, u % per_sq), [], "dw_square")
    cw_in = _col_block(sw, 768)
    ncol = sw // cw_in
    p32_in, pbf_in, (l_sq,) = _dw_reduced(
        h_t[None], dproj[None], cw_in, N_CHIPS * ncol, lambda t: (0, t), 1,
        (1, N_CHIPS, d // 2, sw), (None, None, d // 2, cw_in), lambda u: (0, u // ncol, 0, u % ncol),
        [pbf_sq], "dw_in")
    grad_x, dfront, dng8, l_in = _dh_bwd(dproj, wg_in, xs, front, ds2, norm_g, pbf_in,
                                         lax.empty(pbf_in.shape, BF16), None)
    half_in = _sum_chips([(p32_in, l_in)], sw, pos, "rs_sum_in")
    half_sq = _sum_chips([(p32_sq, l_sq)], d, pos, "rs_sum_sq")
    tail_row = lax.broadcasted_iota(jnp.int32, (8, d), 0)
    tail = jnp.where(tail_row == 0, dng8, jnp.where(tail_row == 1, dfg8,
                     jnp.where(tail_row == 2, loss8[0, 0], 0.0)))
    block = jnp.concatenate([sm, dfront[TM - N_META:TM], tail], axis=0)
    (other_in, other_sq), red = _sibling_swap([half_in, half_sq], block)
    col = lax.dynamic_slice(red, (0, me * dc), (AR_ROWS, dc))
    g_small = {
        "meta_tokens": col[ROW_DMETA:ROW_DMETA + N_META],
        "norm_g": red[ROW_DNG:ROW_DNG + 1],
        "conv_a_w": col[ROW_DWA:ROW_DWA + CONV_A][None],
        "conv_a_b": red[ROW_DCAB:ROW_DCAB + 1],
        "ln_a_g": red[ROW_DLNG:ROW_DLNG + 1],
        "ln_a_b": red[ROW_DLNB:ROW_DLNB + 1],
        "b_a_out": red[ROW_DBAO:ROW_DBAO + 1],
        "conv_b_w": col[ROW_DWB:ROW_DWB + CONV_B][None],
        "final_g": red[ROW_DFG],
    }

    upd_in = _adam_halves([w_in], [m_w_in], [v_w_in], half_in, other_in, pos, "adam_in")
    upd_sq = _adam_halves([w_a_out, w_b_out, w_out], [m_w_a_out, m_w_b_out, m_w_out],
                          [v_w_a_out, v_w_b_out, v_w_out], half_sq, other_sq, pos, "adam_sq")
    small_w = {"meta_tokens": (meta_tokens, m_meta_tokens, v_meta_tokens), "norm_g": (norm_g, m_norm_g, v_norm_g),
               "conv_a_w": (conv_a_w, m_conv_a_w, v_conv_a_w), "conv_a_b": (conv_a_b, m_conv_a_b, v_conv_a_b),
               "ln_a_g": (ln_a_g, m_ln_a_g, v_ln_a_g), "ln_a_b": (ln_a_b, m_ln_a_b, v_ln_a_b),
               "b_a_out": (b_a_out, m_b_a_out, v_b_a_out), "conv_b_w": (conv_b_w, m_conv_b_w, v_conv_b_w),
               "final_g": (final_g, m_final_g, v_final_g)}
    names_small = list(small_w)
    as2d = lambda t: t.reshape(-1, t.shape[-1])
    upd_small = _adam_small([(as2d(small_w[k][0]), as2d(g_small[k]), as2d(small_w[k][1]), as2d(small_w[k][2]))
                             for k in names_small])

    grads, deltas, new_m, new_v = dict(g_small), {}, {}, {}
    for k, upd in zip(names_small, upd_small):
        deltas[k], new_m[k], new_v[k] = [t.reshape(small_w[k][0].shape) for t in upd]
    grads["w_in"], deltas["w_in"], new_m["w_in"], new_v["w_in"] = upd_in
    for idx, k in enumerate(["w_a_out", "w_b_out", "w_out"]):
        grads[k], deltas[k], new_m[k], new_v[k] = upd_sq[4 * idx:4 * idx + 4]

    loss = red[ROW_LOSS, 0]
    order = ["meta_tokens", "norm_g", "w_in", "conv_a_w", "conv_a_b", "ln_a_g", "ln_a_b", "w_a_out", "b_a_out",
             "conv_b_w", "w_b_out", "w_out", "final_g"]
    return (loss, grad_x[None], *[grads[k] for k in order], *[deltas[k] for k in order],
            *[new_m[k] for k in order], *[new_v[k] for k in order])
```

```python
import functools

import jax
import jax.numpy as jnp
from jax import lax
from jax.experimental import pallas as pl
from jax.experimental.pallas import tpu as pltpu

F32 = jnp.float32
BF16 = jnp.bfloat16
MESH = pl.DeviceIdType.MESH

EPS = 1e-6
N_META = 16
N_SPLIT = 9
CONV_A = 31
CONV_B = 3
HALO_A = 32
HALO_B = 8
SHIFT_ROWS = 24
TM = 256
RB = 64
N_ROW_TILES_BIG = 8
ROW_BLOCK = 256
N_CHIPS = 4
VMEM_LIMIT = 56 * 1024 * 1024
VMEM_LIMIT_BIG = 62 * 1024 * 1024

ADAM_LR = 0.001
ADAM_B1 = 0.9
ADAM_B2 = 0.999
ADAM_EPS = 1e-08
ADAM_WD = 0.01
ADAM_STEP = 10

ROW_DWA = 0
ROW_DWB = 32
ROW_DCAB = 40
ROW_DLNG = 41
ROW_DLNB = 42
ROW_DBAO = 43
SM_ROWS = 48
ROW_DMETA = 48
ROW_DNG = 64
ROW_DFG = 65
ROW_LOSS = 66
AR_ROWS = 72


def _sigmoid(v):
    return 0.5 * jnp.tanh(0.5 * v) + 0.5


def _params(sem, **kw):
    return pltpu.CompilerParams(dimension_semantics=sem, vmem_limit_bytes=VMEM_LIMIT, **kw)


def _rows(rb):
    return pl.ds(pl.multiple_of(rb * RB, RB), RB)


def _mesh_pos():
    x, y, c = lax.axis_index("x"), lax.axis_index("y"), lax.axis_index("c")
    return x, y, c


def _half(ref, j, c):
    h = ref.shape[2] // 2
    return ref.at[:, j, pl.ds(c * h, h), :]


def _place_own(shard, pos, dtype, name):
    s, r, c = shard.shape
    rb = ROW_BLOCK if r % ROW_BLOCK == 0 else r

    def body(pos_ref, x_ref, o_ref):
        o_ref[...] = x_ref[...].astype(dtype)

    return pl.pallas_call(
        body, name=name,
        grid_spec=pltpu.PrefetchScalarGridSpec(
            num_scalar_prefetch=1, grid=(s, r // rb),
            in_specs=[pl.BlockSpec((None, rb, c), lambda si, b, pos_ref: (si, b, 0))],
            out_specs=pl.BlockSpec((None, None, rb, c), lambda si, b, pos_ref: (si, pos_ref[1], b, 0))),
        out_shape=jax.ShapeDtypeStruct((s, N_CHIPS, r, c), dtype),
        compiler_params=_params(("arbitrary",) * 2),
    )(pos, shard)


class _Exchange:
    def __init__(self, sends, recvs):
        self.sends, self.recvs = sends, recvs

    @staticmethod
    def _each(pairs, act):
        for cond, cp in pairs:
            if cond is None:
                act(cp)
            else:
                pl.when(cond)(functools.partial(act, cp))

    def start(self):
        self._each(self.sends, lambda cp: cp.start())

    def finish(self):
        self._each(self.recvs, lambda cp: cp.wait_recv())
        self._each(self.sends, lambda cp: cp.wait_send())


def _chip_exchange(part_ref, land_ref, send_sems, recv_sems, half=None):
    x, y, c = _mesh_pos()
    me = 2 * x + y
    sends, recvs = [], []
    for k, (px, py) in enumerate([(1 - x, y), (x, 1 - y), (1 - x, 1 - y)]):
        sems = dict(send_sem=send_sems.at[k], recv_sem=recv_sems.at[k], device_id=(px, py, c), device_id_type=MESH)
        slot = 2 * px + py if half is None else py
        sends.append((None if half is None else px == half, pltpu.make_async_remote_copy(
            src_ref=part_ref.at[:, slot], dst_ref=land_ref.at[:, me], **sems)))
        landed = land_ref.at[:, 2 * px + py]
        recvs.append((None if half is None else x == half,
                      pltpu.make_async_remote_copy(src_ref=landed, dst_ref=landed, **sems)))
    return _Exchange(sends, recvs)

def _sibling_swap(halves, small):
    n = len(halves)

    def body(*refs):
        ins, small_ref, outs, red_ref = refs[:n], refs[n], refs[n + 1:2 * n + 1], refs[2 * n + 1]
        send_sems, recv_sems = refs[2 * n + 2:2 * n + 4]
        reduce = _SmallAllReduce(small_ref, red_ref, *refs[2 * n + 4:])
        x, y, c = _mesh_pos()
        copies = [pltpu.make_async_remote_copy(
            src_ref=ins[a], dst_ref=outs[a], send_sem=send_sems.at[a], recv_sem=recv_sems.at[a],
            device_id=(x, y, 1 - c), device_id_type=MESH) for a in range(n)]
        reduce.start()
        for cp in copies:
            cp.start()
        reduce.between_chips()
        reduce.finish()
        for cp in copies:
            cp.wait()

    any_spec = pl.BlockSpec(memory_space=pl.ANY)
    vm = pl.BlockSpec(memory_space=pltpu.VMEM)
    outs = pl.pallas_call(
        body, name="rs_swap",
        in_specs=[any_spec] * n + [vm], out_specs=[any_spec] * n + [vm],
        out_shape=[jax.ShapeDtypeStruct(h.shape, h.dtype) for h in halves]
        + [jax.ShapeDtypeStruct(small.shape, F32)],
        scratch_shapes=[pltpu.SemaphoreType.DMA((n,)), pltpu.SemaphoreType.DMA((n,))]
        + _SmallAllReduce.scratch(*small.shape),
    )(*halves, small)
    return outs[:n], outs[n]


class _SmallAllReduce:
    def __init__(self, x_ref, out_ref, sib_ref, part_ref, peers_ref, send_sems, recv_sems):
        self.x_ref, self.out_ref, self.sib_ref, self.part_ref, self.peers_ref = x_ref, out_ref, sib_ref, part_ref, peers_ref
        x, y, c = _mesh_pos()
        self.me = 2 * x + y
        self.swap = pltpu.make_async_remote_copy(
            src_ref=x_ref, dst_ref=sib_ref, send_sem=send_sems.at[0], recv_sem=recv_sems.at[0],
            device_id=(x, y, 1 - c), device_id_type=MESH)
        self.sends, self.recvs = [], []
        for k, (px, py) in enumerate([(1 - x, y), (x, 1 - y), (1 - x, 1 - y)]):
            sems = dict(send_sem=send_sems.at[1 + k], recv_sem=recv_sems.at[1 + k],
                        device_id=(px, py, c), device_id_type=MESH)
            self.sends.append(pltpu.make_async_remote_copy(src_ref=part_ref, dst_ref=peers_ref.at[self.me], **sems))
            landed = peers_ref.at[2 * px + py]
            self.recvs.append(pltpu.make_async_remote_copy(src_ref=landed, dst_ref=landed, **sems))

    @staticmethod
    def scratch(rows, d):
        return [pltpu.VMEM((rows, d), F32), pltpu.VMEM((rows, d), F32), pltpu.VMEM((N_CHIPS, rows, d), F32),
                pltpu.SemaphoreType.DMA((4,)), pltpu.SemaphoreType.DMA((4,))]

    def start(self):
        self.swap.start()

    def between_chips(self):
        self.swap.wait()
        self.part_ref[...] = self.x_ref[...] + self.sib_ref[...]
        self.peers_ref[self.me] = self.part_ref[...]
        for cp in self.sends:
            cp.start()

    def finish(self):
        for cp in self.recvs:
            cp.wait_recv()
        for cp in self.sends:
            cp.wait_send()
        p = self.peers_ref
        self.out_ref[...] = ((p[0] + p[1]) + p[2]) + p[3]


def _sum_chips(parts, cw, pos, name):
    s, _, h, _ = parts[0][0].shape
    hb = min(h, ROW_BLOCK)
    widths = [own.shape[3] // cw for own, _ in parts]
    starts = [sum(widths[:a]) for a in range(len(parts))]

    def body(pos_ref, *refs):
        out_ref = refs[-1]
        n = pl.program_id(2)
        total = None
        for a in range(len(parts)):
            own, l1, l2, l3 = refs[4 * a:4 * a + 4]
            val = ((own[...] + l1[...].astype(F32)) + l2[...].astype(F32)) + l3[...].astype(F32)
            total = val if total is None else jnp.where(n >= starts[a], val, total)
        out_ref[...] = total

    def slot(a, k):
        col = lambda n: jnp.clip(n - starts[a], 0, widths[a] - 1)
        return pl.BlockSpec((None, None, hb, cw),
                            lambda si, b, n, pos_ref: (si, (pos_ref[1] + k) % N_CHIPS, b, col(n)))

    operands, specs = [], []
    for a, (own, landed) in enumerate(parts):
        operands += [own, landed, landed, landed]
        specs += [slot(a, 0), slot(a, 1), slot(a, 2), slot(a, 3)]
    return pl.pallas_call(
        body, name=name,
        grid_spec=pltpu.PrefetchScalarGridSpec(
            num_scalar_prefetch=1, grid=(s, h // hb, sum(widths)), in_specs=specs,
            out_specs=pl.BlockSpec((None, hb, cw), lambda si, b, n, pos_ref: (si, b, n))),
        out_shape=jax.ShapeDtypeStruct((s, h, sum(widths) * cw), F32),
        compiler_params=_params(("arbitrary",) * 3),
    )(pos, *operands)


def _adamw(w, g, m, v):
    m = ADAM_B1 * m + (1.0 - ADAM_B1) * g
    v = ADAM_B2 * v + (1.0 - ADAM_B2) * (g * g)
    m_hat = m / (1.0 - ADAM_B1 ** ADAM_STEP)
    v_hat = v / (1.0 - ADAM_B2 ** ADAM_STEP)
    delta = -ADAM_LR * (m_hat / (jnp.sqrt(v_hat) + ADAM_EPS) + ADAM_WD * w)
    return delta, m, v


def _adam_halves(ws, ms, vs, g_own, g_recv, pos, name):
    n = len(ws)
    _, r, c = ws[0].shape
    h = r // 2
    rb = min(h, ROW_BLOCK)
    nb = h // rb

    def body(pos_ref, *refs):
        w_refs, m_refs, v_refs = refs[:n], refs[n:2 * n], refs[2 * n:3 * n]
        go_ref, gr_ref = refs[3 * n:3 * n + 2]
        outs = refs[3 * n + 2:]
        mine = pl.program_id(0) == pos_ref[0]
        for a in range(n):
            g = jnp.where(mine, go_ref[a], gr_ref[a])
            delta, m, v = _adamw(w_refs[a][...], g, m_refs[a][...], v_refs[a][...])
            outs[4 * a][...], outs[4 * a + 1][...], outs[4 * a + 2][...], outs[4 * a + 3][...] = g, delta, m, v

    spec_w = pl.BlockSpec((None, rb, c), lambda hf, b, pos_ref: (0, hf * nb + b, 0))
    spec_g = pl.BlockSpec((n, rb, c), lambda hf, b, pos_ref: (0, b, 0))
    return pl.pallas_call(
        body, name=name,
        grid_spec=pltpu.PrefetchScalarGridSpec(
            num_scalar_prefetch=1, grid=(2, nb), in_specs=[spec_w] * (3 * n) + [spec_g] * 2,
            out_specs=[spec_w] * (4 * n)),
        out_shape=[jax.ShapeDtypeStruct((1, r, c), F32)] * (4 * n),
        compiler_params=_params(("arbitrary",) * 2),
    )(pos, *ws, *ms, *vs, g_own, g_recv)


def _adam_small(items):
    n = len(items)

    def body(*refs):
        ins, outs = refs[:4 * n], refs[4 * n:]
        for a in range(n):
            w_ref, g_ref, m_ref, v_ref = ins[4 * a:4 * a + 4]
            d, m, v = _adamw(w_ref[...], g_ref[...], m_ref[...], v_ref[...])
            outs[3 * a][...] = d
            outs[3 * a + 1][...] = m
            outs[3 * a + 2][...] = v

    vm = pl.BlockSpec(memory_space=pltpu.VMEM)
    flat = [t for it in items for t in it]
    outs = pl.pallas_call(
        body, name="adam_small", in_specs=[vm] * (4 * n), out_specs=[vm] * (3 * n),
        out_shape=[jax.ShapeDtypeStruct(it[0].shape, F32) for it in items for _ in range(3)],
    )(*flat)
    return [tuple(outs[3 * a:3 * a + 3]) for a in range(n)]


def _shard_of_step(js, me):
    flip = jnp.where(js == 1, 2, jnp.where(js == 2, 1, jnp.where(js == 3, 3, 0)))
    return lax.bitwise_xor(me, flip)


def _big_row_spec(seq, tmb, d, tile_of):
    return pl.BlockSpec((pl.Element(tmb), pl.Element(d)),
                        lambda *args: (pl.multiple_of(jnp.minimum(tile_of(*args) * tmb, seq - tmb), 8), 0))


def _big_row_tile(x_ref, front_ref, i):
    rows = x_ref[...]
    last = jnp.concatenate([rows[TM:], front_ref[...]], axis=0)
    return jnp.where(i == N_ROW_TILES_BIG - 1, last, rows)


def _h_prep(x, meta, norm_g):
    seq, d = x.shape
    tp = seq + TM
    dc = meta.shape[1]
    tmb = tp // N_ROW_TILES_BIG
    assert tmb >= TM and tp == tmb * N_ROW_TILES_BIG
    last = N_ROW_TILES_BIG - 1

    def body(x_ref, meta_ref, g_ref, h_ref, front_ref, metas, msend, mrecv):
        x, y, c = _mesh_pos()
        me = 2 * x + y
        chips = [(1 - x, y), (x, 1 - y), (1 - x, 1 - y)]
        i = pl.program_id(0)

        def meta_copy(k, chip):
            return pltpu.make_async_remote_copy(
                src_ref=metas.at[chip], dst_ref=metas.at[chip], send_sem=msend.at[k], recv_sem=mrecv.at[k],
                device_id=(*chips[k], c), device_id_type=MESH)

        @pl.when(i == 0)
        def _():
            metas[me] = meta_ref[...]
            for k in range(3):
                meta_copy(k, me).start()
            front_ref[...] = jnp.zeros_like(front_ref)

        @pl.when(i == last)
        def _():
            for k, (px, py) in enumerate(chips):
                meta_copy(k, 2 * px + py).wait_recv()
            for q in range(N_CHIPS):
                front_ref[TM - N_META:TM, q * dc:(q + 1) * dc] = metas[q]

        s = _big_row_tile(x_ref, front_ref, i)
        r = lax.rsqrt(jnp.mean(s * s, axis=-1, keepdims=True) + EPS)
        h_ref[...] = (s * r * g_ref[...]).astype(BF16)

        @pl.when(i == last)
        def _():
            for k in range(3):
                meta_copy(k, me).wait_send()

    return pl.pallas_call(
        body, name="f0_norm", grid=(N_ROW_TILES_BIG,),
        in_specs=[_big_row_spec(seq, tmb, d, lambda i: i), pl.BlockSpec(meta.shape, lambda i: (0, 0)),
                  pl.BlockSpec((1, d), lambda i: (0, 0))],
        out_specs=[pl.BlockSpec((tmb, d), lambda i: (i, 0)), pl.BlockSpec((TM, d), lambda i: (0, 0))],
        out_shape=[jax.ShapeDtypeStruct((tp, d), BF16), jax.ShapeDtypeStruct((TM, d), F32)],
        scratch_shapes=[pltpu.VMEM((N_CHIPS,) + meta.shape, F32),
                        pltpu.SemaphoreType.DMA((3,)), pltpu.SemaphoreType.DMA((3,))],
        compiler_params=_params(("arbitrary",)),
    )(x, meta, norm_g)


N_UNITS = 6
N_STEPS_PROJ = N_CHIPS * N_UNITS


def _proj_plan(u):
    v = u - N_UNITS
    if v < 12:
        return v % 2, v // 4, v % 4 < 2
    return 2, (v - 12) // 2, (v - 12) % 2 == 0


def _proj_unit(t, me, c):
    v = t - N_UNITS
    near = v < 12
    rel = jnp.where(near, v % 2, 2)
    chunk = jnp.where(near, v // 4, (v - 12) // 2)
    mine = jnp.where(near, v % 4 < 2, (v - 12) % 2 == 0)
    half = jnp.where(mine, c, 1 - c)
    flip = jnp.where(rel == 0, 2, jnp.where(rel == 1, 1, 3))
    own = t < N_UNITS
    return jnp.where(own, me, lax.bitwise_xor(me, flip)), jnp.where(own, t, 3 * half + chunk)


def _proj_fwd2(h, bufs, pos):
    tp, d = h.shape
    _, nsh, _, sw = bufs[0].shape
    cu = sw // N_UNITS
    assert cu % 128 == 0 and N_UNITS % 2 == 0
    n = len(bufs)
    w_sems = 18
    last = N_STEPS_PROJ - 1
    late = N_STEPS_PROJ - N_UNITS

    def body(pos_ref, h_ref, *refs):
        proj_ref = refs[n]
        gbufs = refs[n + 1:2 * n + 1]
        wbuf, wsems, send_sems, recv_sems = refs[2 * n + 1:]
        x, y, c = _mesh_pos()
        me = 2 * x + y
        sibling = (x, y, 1 - c)
        chips = [(1 - x, y), (x, 1 - y), (1 - x, 1 - y)]
        chip_ids = [2 * px + py for px, py in chips]
        relayed_chip = jnp.where(c == 0, chip_ids[0], chip_ids[1])
        relay_to = (jnp.where(c == 0, x, 1 - x), jnp.where(c == 0, 1 - y, y), c)
        t = pl.program_id(0)

        def remote(idx, piece, to):
            return pltpu.make_async_remote_copy(
                src_ref=piece, dst_ref=piece, send_sem=send_sems.at[idx], recv_sem=recv_sems.at[idx],
                device_id=to, device_id_type=MESH)

        def chunk_of(chip, half, k):
            return gbufs[0].at[0, chip, :, pl.ds(pl.multiple_of((3 * half + k) * cu, 128), cu)]

        def own_chunk(r, k):
            return remote(3 * r + k, chunk_of(me, c, k), (*chips[r], c))

        def landed_chunk(r, k):
            return remote(3 * r + k if r < 2 else 6 + k, chunk_of(chip_ids[r], c, k), (*chips[r], c))

        def relay_chunk(k):
            return remote(6 + k, chunk_of(relayed_chip, c, k), relay_to)

        def sibling_chunk(r, k, half):
            return remote(9 + 3 * r + k, chunk_of(chip_ids[r], half, k), sibling)

        def fetch(u):
            chip, unit = _proj_unit(jnp.int32(u), me, c)
            return pltpu.make_async_copy(gbufs[0].at[0, chip, :, pl.ds(pl.multiple_of(unit * cu, 128), cu)],
                                         wbuf.at[u % 2], wsems.at[u % 2])

        def make_available(u):
            r, k, mine = _proj_plan(u)
            if mine:
                landed_chunk(r, k).wait_recv()
                if r < 2:
                    pl.when(c == r)(lambda: relay_chunk(k).start())
                sibling_chunk(r, k, c).start()
            else:
                sibling_chunk(r, k, 1 - c).wait_recv()

        def own_piece(a, r):
            return remote(w_sems + 6 * (a - 1) + r, _half(gbufs[a], me, c), (*chips[r], c))

        def relay(a):
            return remote(w_sems + 6 * (a - 1) + 2, _half(gbufs[a], relayed_chip, c), relay_to)

        def to_sibling(a, r, core):
            return remote(w_sems + 6 * (a - 1) + 3 + r, _half(gbufs[a], chip_ids[r], core), sibling)

        def landed(a, r):
            return remote(w_sems + 6 * (a - 1) + r, _half(gbufs[a], chip_ids[r], c), (*chips[r], c))

        for u in range(N_STEPS_PROJ):
            @pl.when(t == u)
            def _(u=u):
                if u == 0:
                    for k in range(3):
                        for r in range(2):
                            own_chunk(r, k).start()
                    for a in range(1, n):
                        for r in range(2):
                            own_piece(a, r).start()
                    fetch(0).start()
                if u < last:
                    if u + 1 >= N_UNITS:
                        make_available(u + 1)
                    fetch(u + 1).start()
                if u == late:
                    for a in range(1, n):
                        landed(a, 0).wait_recv()
                        landed(a, 1).wait_recv()
                        relay(a).start()
                        for r in range(2):
                            to_sibling(a, r, c).start()
                        for r in range(2):
                            to_sibling(a, r, 1 - c).wait_recv()
                fetch(u).wait()

        proj_ref[...] = jnp.dot(h_ref[...], wbuf[t % 2], preferred_element_type=F32).astype(BF16)

        @pl.when(t == last)
        def _():
            for a in range(1, n):
                landed(a, 2).wait_recv()
                to_sibling(a, 2, c).start()
                to_sibling(a, 2, 1 - c).wait_recv()
            for k in range(3):
                for r in range(2):
                    own_chunk(r, k).wait_send()
                relay_chunk(k).wait_send()
                for r in range(3):
                    sibling_chunk(r, k, c).wait_send()
            for a in range(1, n):
                for r in range(2):
                    own_piece(a, r).wait_send()
                relay(a).wait_send()
                for r in range(3):
                    to_sibling(a, r, c).wait_send()

    def out_index(t, pos_ref):
        chip, unit = _proj_unit(t, pos_ref[1], pos_ref[0])
        return 0, chip * N_UNITS + unit

    any_spec = pl.BlockSpec(memory_space=pl.ANY)
    outs = pl.pallas_call(
        body, name="f1_proj",
        grid_spec=pltpu.PrefetchScalarGridSpec(
            num_scalar_prefetch=1, grid=(N_STEPS_PROJ,),
            in_specs=[pl.BlockSpec((tp, d), lambda t, pos_ref: (0, 0))] + [any_spec] * n,
            out_specs=[pl.BlockSpec((tp, cu), out_index)] + [any_spec] * n,
            scratch_shapes=[pltpu.VMEM((2, d, cu), BF16), pltpu.SemaphoreType.DMA((2,)),
                            pltpu.SemaphoreType.DMA((w_sems + 6 * (n - 1),)),
                            pltpu.SemaphoreType.DMA((w_sems + 6 * (n - 1),))]),
        out_shape=[jax.ShapeDtypeStruct((tp, nsh * sw), BF16)]
        + [jax.ShapeDtypeStruct(b.shape, b.dtype) for b in bufs],
        input_output_aliases={2 + a: 1 + a for a in range(n)},
        compiler_params=_params(("arbitrary",)),
    )(pos, h, *bufs)
    return outs[0], outs[1:]


def _proj_fwd(x, meta, norm_g, bufs, pos):
    seq, d = x.shape
    tp = seq + TM
    _, nsh, _, sw = bufs[0].shape
    dc = meta.shape[1]
    tmb = tp // N_ROW_TILES_BIG
    assert tmb >= TM and tp == tmb * N_ROW_TILES_BIG
    n = len(bufs)

    def body(pos_ref, x_ref, meta_ref, g_ref, *refs):
        proj_ref, front_ref = refs[n], refs[n + 1]
        gbufs = refs[n + 2:2 * n + 2]
        wbuf, wsems, send_sems, recv_sems, metas, msend, mrecv = refs[2 * n + 2:]
        x, y, c = _mesh_pos()
        me = 2 * x + y
        sibling = (x, y, 1 - c)
        chips = [(1 - x, y), (x, 1 - y), (1 - x, 1 - y)]
        js, i = pl.program_id(0), pl.program_id(1)

        def meta_copy(k, chip):
            return pltpu.make_async_remote_copy(
                src_ref=metas.at[chip], dst_ref=metas.at[chip], send_sem=msend.at[k], recv_sem=mrecv.at[k],
                device_id=(*chips[k], c), device_id_type=MESH)

        @pl.when((js == 0) & (i == 0))
        def _():
            metas[me] = meta_ref[...]
            for k in range(3):
                meta_copy(k, me).start()
            front_ref[...] = jnp.zeros_like(front_ref)

        @pl.when((js == 0) & (i == N_ROW_TILES_BIG - 1))
        def _():
            for k, (px, py) in enumerate(chips):
                meta_copy(k, 2 * px + py).wait_recv()
            for q in range(N_CHIPS):
                front_ref[TM - N_META:TM, q * dc:(q + 1) * dc] = metas[q]

        def remote(a, k, piece, to):
            return pltpu.make_async_remote_copy(
                src_ref=piece, dst_ref=piece, send_sem=send_sems.at[6 * a + k],
                recv_sem=recv_sems.at[6 * a + k], device_id=to, device_id_type=MESH)

        def fetch(chip, step):
            return pltpu.make_async_copy(gbufs[0].at[0, chip], wbuf.at[step % 2], wsems.at[step % 2])

        chip_ids = [2 * px + py for px, py in chips]
        relayed_chip = jnp.where(c == 0, chip_ids[0], chip_ids[1])
        relay_to = (jnp.where(c == 0, x, 1 - x), jnp.where(c == 0, 1 - y, y), c)

        def own_piece(a, k):
            return remote(a, k, _half(gbufs[a], me, c), (*chips[k], c))

        def relay(a):
            return remote(a, 2, _half(gbufs[a], relayed_chip, c), relay_to)

        def to_sibling(a, k, core):
            return remote(a, 3 + k, _half(gbufs[a], chip_ids[k], core), sibling)

        def landed(a, k):
            return remote(a, k, _half(gbufs[a], chip_ids[k], c), (*chips[k], c))

        def take_neighbours(a):
            landed(a, 0).wait_recv()
            landed(a, 1).wait_recv()
            relay(a).start()
            for k in range(2):
                to_sibling(a, k, c).start()
            for k in range(2):
                to_sibling(a, k, 1 - c).wait_recv()

        def take_diagonal(a):
            landed(a, 2).wait_recv()
            to_sibling(a, 2, c).start()
            to_sibling(a, 2, 1 - c).wait_recv()

        @pl.when((js == 0) & (i == 0))
        def _():
            for a in range(n):
                for k in range(2):
                    own_piece(a, k).start()
            fetch(me, 0).start()
            fetch(me, 0).wait()

        @pl.when((js == 1) & (i == 0))
        def _():
            take_neighbours(0)
            fetch(chip_ids[0], 1).start()
            fetch(chip_ids[0], 1).wait()

        @pl.when((js == 1) & (i == N_ROW_TILES_BIG - 2))
        def _():
            fetch(chip_ids[1], 2).start()

        @pl.when((js == 2) & (i == 0))
        def _():
            fetch(chip_ids[1], 2).wait()

        @pl.when((js == 2) & (i == 1))
        def _():
            for a in range(1, n):
                take_neighbours(a)

        @pl.when((js == 2) & (i == N_ROW_TILES_BIG - 2))
        def _():
            take_diagonal(0)
            fetch(chip_ids[2], 3).start()

        @pl.when((js == 3) & (i == 0))
        def _():
            fetch(chip_ids[2], 3).wait()

        s = _big_row_tile(x_ref, front_ref, i)
        r = lax.rsqrt(jnp.mean(s * s, axis=-1, keepdims=True) + EPS)
        h = (s * r * g_ref[...]).astype(BF16)
        proj_ref[...] = jnp.dot(h, wbuf[js % 2], preferred_element_type=F32).astype(BF16)

        @pl.when((js == nsh - 1) & (i == N_ROW_TILES_BIG - 1))
        def _():
            for a in range(1, n):
                take_diagonal(a)
            for a in range(n):
                for k in range(2):
                    own_piece(a, k).wait_send()
                relay(a).wait_send()
                for k in range(3):
                    to_sibling(a, k, c).wait_send()
            for k in range(3):
                meta_copy(k, me).wait_send()

    any_spec = pl.BlockSpec(memory_space=pl.ANY)
    outs = pl.pallas_call(
        body, name="f1_proj",
        grid_spec=pltpu.PrefetchScalarGridSpec(
            num_scalar_prefetch=1, grid=(nsh, N_ROW_TILES_BIG),
            in_specs=[_big_row_spec(seq, tmb, d, lambda js, i, pos_ref: i),
                      pl.BlockSpec(meta.shape, lambda js, i, pos_ref: (0, 0)),
                      pl.BlockSpec((1, d), lambda js, i, pos_ref: (0, 0))] + [any_spec] * n,
            out_specs=[pl.BlockSpec((tmb, sw), lambda js, i, pos_ref: (i, _shard_of_step(js, pos_ref[1]))),
                       pl.BlockSpec((TM, d), lambda js, i, pos_ref: (0, 0))] + [any_spec] * n,
            scratch_shapes=[pltpu.VMEM((2, d, sw), BF16), pltpu.SemaphoreType.DMA((2,)),
                            pltpu.SemaphoreType.DMA((6 * n,)), pltpu.SemaphoreType.DMA((6 * n,)),
                            pltpu.VMEM((N_CHIPS,) + meta.shape, F32),
                            pltpu.SemaphoreType.DMA((3,)), pltpu.SemaphoreType.DMA((3,))]),
        out_shape=[jax.ShapeDtypeStruct((tp, nsh * sw), BF16), jax.ShapeDtypeStruct((TM, d), F32)]
        + [jax.ShapeDtypeStruct(b.shape, b.dtype) for b in bufs],
        input_output_aliases={4 + a: 2 + a for a in range(n)},
        compiler_params=_params(("arbitrary", "arbitrary")),
    )(pos, x, meta, norm_g, *bufs)
    return outs[0], outs[1], outs[2:]


def _dh_bwd(dproj, wg_in, x, front, ds2, norm_g, part, land, half):
    seq, d = x.shape
    tp = seq + TM
    _, nsh, _, sw = wg_in.shape
    tmb = tp // N_ROW_TILES_BIG
    tail = tmb - TM
    last = N_ROW_TILES_BIG - 1

    def body(dp_ref, w_hbm, x_ref, front_ref, ds2_ref, g_ref, part_ref, _, gx_hbm, dfront_ref, dng_ref, land_ref,
             wbuf, gacc, dsbuf, wsem, osems, send_sems, recv_sems):
        exchange = _chip_exchange(part_ref, land_ref, send_sems, recv_sems, half)
        i = pl.program_id(0)

        def x_rows_out(step):
            return pltpu.make_async_copy(dsbuf.at[step % 2], gx_hbm.at[pl.ds(step * tmb, tmb), :], osems.at[step % 2])

        last_out = pltpu.make_async_copy(dsbuf.at[last % 2, pl.ds(0, tail), :],
                                         gx_hbm.at[pl.ds(last * tmb, tail), :], osems.at[last % 2])

        @pl.when(i == 0)
        def _():
            exchange.start()
            gacc[...] = jnp.zeros_like(gacc)
            whole = pltpu.make_async_copy(w_hbm.at[0], wbuf, wsem)
            whole.start()
            whole.wait()

        dh = None
        for j in range(nsh):
            part = lax.dot_general(dp_ref[:, j * sw:(j + 1) * sw], wbuf[j], (((1,), (1,)), ((), ())),
                                   preferred_element_type=F32)
            dh = part if dh is None else dh + part
        s = _big_row_tile(x_ref, front_ref, i)
        r = lax.rsqrt(jnp.mean(s * s, axis=-1, keepdims=True) + EPS)
        gacc[...] += (dh * s * r).reshape(tmb // 8, 8, d).sum(axis=0)
        t = dh * g_ref[...]

        @pl.when(i >= 2)
        def _():
            x_rows_out(i - 2).wait()

        dsbuf[i % 2] = ds2_ref[...] + r * t - s * (r * r * r) * jnp.mean(t * s, axis=-1, keepdims=True)

        @pl.when(i < last)
        def _():
            x_rows_out(i).start()

        @pl.when(i == last)
        def _():
            last_out.start()
            dfront_ref[...] = dsbuf[last % 2, tail:, :]
            dng_ref[...] = jnp.broadcast_to(jnp.sum(gacc[...], axis=0, keepdims=True), (8, d))
            exchange.finish()
            x_rows_out(last - 1).wait()
            last_out.wait()

    any_spec = pl.BlockSpec(memory_space=pl.ANY)
    return pl.pallas_call(
        body, name="b2_dh", grid=(N_ROW_TILES_BIG,),
        in_specs=[pl.BlockSpec((tmb, nsh * sw), lambda i: (i, 0)), any_spec,
                  _big_row_spec(seq, tmb, d, lambda i: i),
                  pl.BlockSpec((TM, d), lambda i: (0, 0)),
                  pl.BlockSpec((tmb, d), lambda i: (i, 0)),
                  pl.BlockSpec((1, d), lambda i: (0, 0)), any_spec, any_spec],
        out_specs=[any_spec, pl.BlockSpec((TM, d), lambda i: (0, 0)),
                   pl.BlockSpec((8, d), lambda i: (0, 0)), any_spec],
        out_shape=[jax.ShapeDtypeStruct((seq, d), F32), jax.ShapeDtypeStruct((TM, d), F32),
                   jax.ShapeDtypeStruct((8, d), F32), jax.ShapeDtypeStruct(land.shape, land.dtype)],
        input_output_aliases={7: 3},
        scratch_shapes=[pltpu.VMEM((nsh, d, sw), BF16), pltpu.VMEM((8, d), F32), pltpu.VMEM((2, tmb, d), F32),
                        pltpu.SemaphoreType.DMA, pltpu.SemaphoreType.DMA((2,)),
                        pltpu.SemaphoreType.DMA((3,)), pltpu.SemaphoreType.DMA((3,))],
        compiler_params=pltpu.CompilerParams(dimension_semantics=("arbitrary",),
                                             vmem_limit_bytes=VMEM_LIMIT_BIG),
    )(dproj, wg_in, x, front, ds2, norm_g, part, land)


def _col_block(width, cap):
    return max(b for b in range(128, cap + 1, 128) if width % b == 0)


def _dw_reduced(lhs_t, rhs, cw, nblk, operands, groups, out_dims, out_block, out_index, carried, name):
    na, d, tp = lhs_t.shape
    rg = d // groups
    hh = rg // 2
    nc = len(carried)

    def body(*refs):
        l_ref, r_ref = refs[:2]
        part_refs = refs[2:2 + nc]
        p32_ref, pbf_ref = refs[2 + nc:4 + nc]
        land_refs = refs[4 + nc:4 + 2 * nc]
        res, rbuf, send_sems, recv_sems = refs[4 + 2 * nc:8 + 2 * nc]
        xsems = refs[8 + 2 * nc:]
        exchanges = [_chip_exchange(part_refs[e], land_refs[e], xsems[2 * e], xsems[2 * e + 1]) for e in range(nc)]
        exchange = _Exchange([s for ex in exchanges for s in ex.sends], [r for ex in exchanges for r in ex.recvs])
        x, y, c = _mesh_pos()
        t = pl.program_id(0)
        u = jnp.maximum(t - 1, 0)

        def to_sibling(blk):
            return pltpu.make_async_remote_copy(
                src_ref=res.at[blk % 2, :, pl.ds((1 - c) * hh, hh), :], dst_ref=rbuf.at[blk % 2],
                send_sem=send_sems.at[blk], recv_sem=recv_sems.at[blk],
                device_id=(x, y, 1 - c), device_id_type=MESH)

        @pl.when(t == 0)
        def _():
            exchange.start()

        @pl.when(t < nblk)
        def _():
            res[t % 2] = jnp.dot(l_ref[...], r_ref[...], preferred_element_type=F32).reshape(groups, rg, cw)

        @pl.when(t >= 1)
        def _():
            to_sibling(u).wait_recv()
            p = res[u % 2, :, pl.ds(c * hh, hh), :] + rbuf[u % 2]
            p32_ref[...] = p.reshape(p32_ref.shape)
            pbf_ref[...] = p.reshape(pbf_ref.shape).astype(BF16)

        @pl.when(t < nblk)
        def _():
            to_sibling(t).start()

        @pl.when(t >= 1)
        def _():
            to_sibling(u).wait_send()

        @pl.when(t == nblk)
        def _():
            exchange.finish()

    any_spec = pl.BlockSpec(memory_space=pl.ANY)
    last = nblk - 1
    out_spec = pl.BlockSpec(out_block, lambda t: out_index(jnp.maximum(t - 1, 0)))
    outs = pl.pallas_call(
        body, name=name, grid=(nblk + 1,),
        in_specs=[pl.BlockSpec((None, d, tp), lambda t: (operands(jnp.minimum(t, last))[0], 0, 0)),
                  pl.BlockSpec((None, tp, cw), lambda t: (operands(jnp.minimum(t, last))[0], 0,
                                                          operands(jnp.minimum(t, last))[1]))]
        + [any_spec] * nc,
        out_specs=[out_spec, out_spec] + [any_spec] * nc,
        out_shape=[jax.ShapeDtypeStruct(out_dims, F32), jax.ShapeDtypeStruct(out_dims, BF16)]
        + [jax.ShapeDtypeStruct(e.shape, e.dtype) for e in carried],
        scratch_shapes=[pltpu.VMEM((2, groups, rg, cw), F32), pltpu.VMEM((2, groups, hh, cw), F32),
                        pltpu.SemaphoreType.DMA((nblk,)), pltpu.SemaphoreType.DMA((nblk,))]
        + [pltpu.SemaphoreType.DMA((3,)), pltpu.SemaphoreType.DMA((3,))] * nc,
        compiler_params=_params(("arbitrary",)),
    )(lhs_t, rhs, *carried)
    return outs[0], outs[1], outs[2:]


def _conv_a_taps(first_lag, last_lag):
    out = []
    for r in range(8):
        taps = [(q, 8 * q + r) for q in range(5) if first_lag <= 8 * q + r <= last_lag]
        if taps:
            out.append((r, taps))
    return out


def _tile_block(i, nt):
    return jnp.where(i == 0, nt - 1, i - 1)


def _mix_fwd(x, front, proj, target, w3, wa, wb, conv_a_b, ln_g, ln_b, b_a_out, final_g, norm_g):
    seq, d = x.shape
    tp = seq + TM
    nt = tp // TM
    nrb = TM // RB
    shl = TM + SHIFT_ROWS

    def body(x_ref, front_ref, proj_ref, tgt_ref, w3_ref, wa_ref, wb_ref, cab_ref, lng_ref, lnb_ref, bao_ref, fg_ref,
             ng_ref, ca_ref, cb_ref, ya_ref, yb_ref, abmt_ref, ds2_ref, ht_ref, loss_ref, dfg_ref,
             abm_ref, ext_a, ext_b, sh, s2_s, lacc, gacc):
        i = pl.program_id(0)

        def split(k, rows):
            return proj_ref[rows, k * d:(k + 1) * d].astype(F32)

        def s_tile():
            return jnp.where(i == 0, front_ref[...], x_ref[...])

        s_in = s_tile()
        h = s_in * lax.rsqrt(jnp.mean(s_in * s_in, axis=-1, keepdims=True) + EPS) * ng_ref[...]
        ht_ref[...] = h.astype(BF16).T

        @pl.when(i == 0)
        def _():
            ext_a[0:HALO_A, :] = jnp.zeros((HALO_A, d), F32)
            ext_b[0:HALO_B, :] = jnp.zeros((HALO_B, d), F32)
            lacc[...] = jnp.zeros_like(lacc)
            gacc[...] = jnp.zeros_like(gacc)

        def conv_in(rb, carry):
            rows = _rows(rb)
            ua0 = split(0, rows) * _sigmoid(split(1, rows))
            ext_a[pl.ds(pl.multiple_of(HALO_A + rb * RB, 8), RB), :] = ua0
            ext_b[pl.ds(pl.multiple_of(HALO_B + rb * RB, 8), RB), :] = split(4, rows) * split(5, rows)
            ca_ref[rows, :] = jnp.broadcast_to(cab_ref[...], (RB, d))
            return carry
        lax.fori_loop(0, nrb, conv_in, 0)

        for r, taps in _conv_a_taps(HALO_A - CONV_A + 1, HALO_A):
            if r == 0:
                src = ext_a
            else:
                sh[...] = ext_a[r:r + shl, :]
                src = sh

            def conv_acc(rb, carry, src=src, taps=taps):
                rows = _rows(rb)
                acc = ca_ref[rows, :]
                for q, lag in taps:
                    k = lag - (HALO_A - CONV_A + 1)
                    acc = acc + src[pl.ds(pl.multiple_of(rb * RB + 8 * q, 8), RB), :] * wa_ref[k:k + 1, :]
                ca_ref[rows, :] = acc
                return carry
            lax.fori_loop(0, nrb, conv_acc, 0)
        ext_a[0:HALO_A, :] = ext_a[TM:TM + HALO_A, :]

        cb_ref[...] = ext_b[HALO_B:HALO_B + TM, :] * wb_ref[2:3, :]
        for k in range(CONV_B - 1):
            off = HALO_B - CONV_B + 1 + k
            sh[0:TM, :] = ext_b[off:off + TM, :]
            cb_ref[...] += sh[0:TM, :] * wb_ref[k:k + 1, :]
        ext_b[0:HALO_B, :] = ext_b[TM:TM + HALO_B, :]

        def branches(rb, carry):
            rows = _rows(rb)
            ca = ca_ref[rows, :]
            mu = jnp.mean(ca, axis=-1, keepdims=True)
            xc = ca - mu
            rstd = lax.rsqrt(jnp.mean(xc * xc, axis=-1, keepdims=True) + EPS)
            ln = xc * rstd * lng_ref[...] + lnb_ref[...]
            ua = ln * _sigmoid(ln)
            a_z = split(2, rows)
            abm_ref[0, rows, :] = (ua * (a_z * _sigmoid(a_z))).astype(BF16)
            b_z = split(6, rows)
            ub = split(3, rows) * cb_ref[rows, :]
            abm_ref[1, rows, :] = (ub * (b_z * _sigmoid(b_z))).astype(BF16)
            return carry
        lax.fori_loop(0, nrb, branches, 0)

        ya_ref[...] = jnp.dot(abm_ref[0], w3_ref[0], preferred_element_type=F32) + bao_ref[...]
        yb_ref[...] = jnp.dot(abm_ref[1], w3_ref[1], preferred_element_type=F32)

        def merge(rb, carry):
            rows = _rows(rb)
            m = _sigmoid(split(7, rows)) * ya_ref[rows, :] + _sigmoid(split(8, rows)) * yb_ref[rows, :]
            abm_ref[2, rows, :] = m.astype(BF16)
            return carry
        lax.fori_loop(0, nrb, merge, 0)

        s2_s[...] = s_tile() + jnp.dot(abm_ref[2], w3_ref[2], preferred_element_type=F32)
        for k in range(3):
            abmt_ref[k] = abm_ref[k].T
        live = (i > 0).astype(F32)

        def head(rb, carry):
            rows = _rows(rb)
            s2 = s2_s[rows, :]
            r2 = lax.rsqrt(jnp.mean(s2 * s2, axis=-1, keepdims=True) + EPS)
            diff = (s2 * r2 * fg_ref[...] - tgt_ref[rows, :]) * live
            lacc[...] += diff * diff
            dy = diff * (1.0 / d)
            gacc[...] += (dy * s2 * r2).reshape(RB // 8, 8, d).sum(axis=0)
            t = dy * fg_ref[...]
            ds2_ref[rows, :] = r2 * t - s2 * (r2 * r2 * r2) * jnp.mean(t * s2, axis=-1, keepdims=True)
            return carry
        lax.fori_loop(0, nrb, head, 0)

        @pl.when(i == nt - 1)
        def _():
            loss_ref[...] = jnp.broadcast_to(0.5 * jnp.sum(lacc[...]) * (1.0 / d), (8, 128))
            dfg_ref[...] = jnp.broadcast_to(jnp.sum(gacc[...], axis=0, keepdims=True), (8, d))

    row_f32 = pl.BlockSpec((TM, d), lambda i: (_tile_block(i, nt), 0))
    x_rows = pl.BlockSpec((TM, d), lambda i: (jnp.maximum(i - 1, 0), 0))
    const = lambda shape: pl.BlockSpec(shape, lambda i: (0,) * len(shape))
    return pl.pallas_call(
        body, name="f2_mix", grid=(nt,),
        in_specs=[x_rows, const((TM, d)),
                  pl.BlockSpec((TM, N_SPLIT * d), lambda i: (_tile_block(i, nt), 0)),
                  x_rows,
                  const((3, d, d)), const(wa.shape), const(wb.shape)] + [const((1, d))] * 6,
        out_specs=[row_f32, row_f32, row_f32, row_f32,
                   pl.BlockSpec((3, d, TM), lambda i: (0, 0, _tile_block(i, nt))),
                   row_f32, pl.BlockSpec((d, TM), lambda i: (0, _tile_block(i, nt))),
                   const((8, 128)), const((8, d))],
        out_shape=[jax.ShapeDtypeStruct((tp, d), F32)] * 4
        + [jax.ShapeDtypeStruct((3, d, tp), BF16), jax.ShapeDtypeStruct((tp, d), F32),
           jax.ShapeDtypeStruct((d, tp), BF16),
           jax.ShapeDtypeStruct((8, 128), F32), jax.ShapeDtypeStruct((8, d), F32)],
        scratch_shapes=[pltpu.VMEM((3, TM, d), BF16),
                        pltpu.VMEM((HALO_A + TM, d), F32), pltpu.VMEM((HALO_B + TM, d), F32),
                        pltpu.VMEM((shl, d), F32), pltpu.VMEM((TM, d), F32),
                        pltpu.VMEM((RB, d), F32), pltpu.VMEM((8, d), F32)],
        compiler_params=_params(("arbitrary",)),
    )(x, front, proj, target, w3, wa, wb, conv_a_b, ln_g, ln_b, b_a_out, final_g, norm_g)


def _mix_bwd(ds2, proj, ca, cb, ya, yb, w3, wa, wb, ln_g, ln_b):
    tp, d = ds2.shape
    nt = tp // TM
    nrb = TM // RB
    shl = TM + SHIFT_ROWS
    nt_dims = (((1,), (1,)), ((), ()))

    def body(ds2_ref, proj_ref, ca_ref, cb_ref, ya_ref, yb_ref, w3_ref, wa_ref, wb_ref, lng_ref, lnb_ref,
             dproj_ref, d3_ref, sm_ref, ext_d, ext_e, sh, dm_s, dpa_s, dpb_s, dua0_s, acc):
        step = pl.program_id(0)

        def split(k, rows):
            return proj_ref[rows, k * d:(k + 1) * d].astype(F32)

        def put(k, rows, val):
            dproj_ref[rows, k * d:(k + 1) * d] = val.astype(BF16)

        def accum(row, val):
            acc[row] += val.reshape(RB // 8, 8, d).sum(axis=0)

        @pl.when(step == 0)
        def _():
            ext_d[TM:TM + HALO_A, :] = jnp.zeros((HALO_A, d), F32)
            ext_e[TM:TM + HALO_B, :] = jnp.zeros((HALO_B, d), F32)
            acc[...] = jnp.zeros_like(acc)

        d3_ref[2] = ds2_ref[...].astype(BF16)
        dm_s[...] = lax.dot_general(d3_ref[2], w3_ref[2], nt_dims, preferred_element_type=F32)

        def gates(rb, carry):
            rows = _rows(rb)
            dm = dm_s[rows, :]
            sa = _sigmoid(split(7, rows))
            sb = _sigmoid(split(8, rows))
            ya_v = ya_ref[rows, :]
            yb_v = yb_ref[rows, :]
            put(7, rows, dm * ya_v * sa * (1.0 - sa))
            put(8, rows, dm * yb_v * sb * (1.0 - sb))
            dya = dm * sa
            accum(ROW_DBAO, dya)
            d3_ref[0, rows, :] = dya.astype(BF16)
            d3_ref[1, rows, :] = (dm * sb).astype(BF16)
            return carry
        lax.fori_loop(0, nrb, gates, 0)

        dpa_s[...] = lax.dot_general(d3_ref[0], w3_ref[0], nt_dims, preferred_element_type=F32)
        dpb_s[...] = lax.dot_general(d3_ref[1], w3_ref[1], nt_dims, preferred_element_type=F32)

        def branches(rb, carry):
            rows = _rows(rb)
            ca_v = ca_ref[rows, :]
            mu = jnp.mean(ca_v, axis=-1, keepdims=True)
            xc = ca_v - mu
            rstd = lax.rsqrt(jnp.mean(xc * xc, axis=-1, keepdims=True) + EPS)
            xhat = xc * rstd
            ln = xhat * lng_ref[...] + lnb_ref[...]
            sl = _sigmoid(ln)
            ua = ln * sl
            a_z = split(2, rows)
            sz = _sigmoid(a_z)
            dpa = dpa_s[rows, :]
            put(2, rows, dpa * ua * (sz * (1.0 + a_z * (1.0 - sz))))
            dln = dpa * (a_z * sz) * (sl * (1.0 + ln * (1.0 - sl)))
            accum(ROW_DLNG, dln * xhat)
            accum(ROW_DLNB, dln)
            dxh = dln * lng_ref[...]
            dca = rstd * (dxh - jnp.mean(dxh, axis=-1, keepdims=True)
                          - xhat * jnp.mean(dxh * xhat, axis=-1, keepdims=True))
            accum(ROW_DCAB, dca)
            ext_d[rows, :] = dca
            dua0_s[rows, :] = jnp.zeros((RB, d), F32)
            dm_s[rows, :] = split(0, rows) * _sigmoid(split(1, rows))
            b_z = split(6, rows)
            szb = _sigmoid(b_z)
            dpb = dpb_s[rows, :]
            b_b = split(3, rows)
            cb_v = cb_ref[rows, :]
            put(6, rows, dpb * (b_b * cb_v) * (szb * (1.0 + b_z * (1.0 - szb))))
            dub = dpb * (b_z * szb)
            put(3, rows, dub * cb_v)
            ext_e[rows, :] = dub * b_b
            return carry
        lax.fori_loop(0, nrb, branches, 0)

        for r, taps in _conv_a_taps(0, CONV_A - 1):
            if r == 0:
                src = ext_d
            else:
                sh[...] = ext_d[r:r + shl, :]
                src = sh

            def conv_t(rb, carry, src=src, taps=taps):
                rows = _rows(rb)
                ua0 = dm_s[rows, :]
                dua0 = dua0_s[rows, :]
                for q, lag in taps:
                    k = CONV_A - 1 - lag
                    slab = src[pl.ds(pl.multiple_of(rb * RB + 8 * q, 8), RB), :]
                    dua0 = dua0 + slab * wa_ref[k:k + 1, :]
                    accum(ROW_DWA + k, slab * ua0)
                dua0_s[rows, :] = dua0
                return carry
            lax.fori_loop(0, nrb, conv_t, 0)
        ext_d[TM:TM + HALO_A, :] = ext_d[0:HALO_A, :]

        dpb_s[...] = ext_e[0:TM, :] * wb_ref[CONV_B - 1:CONV_B, :]
        for lag in range(CONV_B):
            k = CONV_B - 1 - lag
            if lag > 0:
                sh[0:TM, :] = ext_e[lag:lag + TM, :]
                dpb_s[...] += sh[0:TM, :] * wb_ref[k:k + 1, :]
            src = ext_e if lag == 0 else sh

            def conv_b_w(rb, carry, src=src, k=k):
                rows = _rows(rb)
                accum(ROW_DWB + k, src[rows, :] * (split(4, rows) * split(5, rows)))
                return carry
            lax.fori_loop(0, nrb, conv_b_w, 0)
        ext_e[TM:TM + HALO_B, :] = ext_e[0:HALO_B, :]

        def inputs(rb, carry):
            rows = _rows(rb)
            dua0 = dua0_s[rows, :]
            a_val = split(0, rows)
            sg = _sigmoid(split(1, rows))
            put(0, rows, dua0 * sg)
            put(1, rows, dua0 * a_val * sg * (1.0 - sg))
            dcbin = dpb_s[rows, :]
            put(4, rows, dcbin * split(5, rows))
            put(5, rows, dcbin * split(4, rows))
            return carry
        lax.fori_loop(0, nrb, inputs, 0)

        @pl.when(step == nt - 1)
        def _():
            for row in range(SM_ROWS):
                sm_ref[row:row + 1, :] = jnp.sum(acc[row], axis=0, keepdims=True)

    rev = lambda i: (_tile_block(nt - 1 - i, nt), 0)
    row_f32 = pl.BlockSpec((TM, d), rev)
    const = lambda shape: pl.BlockSpec(shape, lambda i: (0,) * len(shape))
    return pl.pallas_call(
        body, name="b1_mix", grid=(nt,),
        in_specs=[row_f32, pl.BlockSpec((TM, N_SPLIT * d), rev), row_f32, row_f32, row_f32, row_f32,
                  const((3, d, d)), const(wa.shape), const(wb.shape), const((1, d)), const((1, d))],
        out_specs=[pl.BlockSpec((TM, N_SPLIT * d), rev),
                   pl.BlockSpec((3, TM, d), lambda i: (0, _tile_block(nt - 1 - i, nt), 0)),
                   const((SM_ROWS, d))],
        out_shape=[jax.ShapeDtypeStruct((tp, N_SPLIT * d), BF16), jax.ShapeDtypeStruct((3, tp, d), BF16),
                   jax.ShapeDtypeStruct((SM_ROWS, d), F32)],
        scratch_shapes=[pltpu.VMEM((TM + HALO_A, d), F32), pltpu.VMEM((TM + HALO_B, d), F32),
                        pltpu.VMEM((shl, d), F32), pltpu.VMEM((TM, d), F32), pltpu.VMEM((TM, d), F32),
                        pltpu.VMEM((TM, d), F32), pltpu.VMEM((TM, d), F32),
                        pltpu.VMEM((SM_ROWS, 8, d), F32)],
        compiler_params=_params(("arbitrary",)),
    )(ds2, proj, ca, cb, ya, yb, w3, wa, wb, ln_g, ln_b)


def kernel(x, meta_tokens, norm_g, w_in, conv_a_w, conv_a_b, ln_a_g, ln_a_b, w_a_out, b_a_out, conv_b_w, w_b_out, w_out, final_g, loss_target, m_meta_tokens, m_norm_g, m_w_in, m_conv_a_w, m_conv_a_b, m_ln_a_g, m_ln_a_b, m_w_a_out, m_b_a_out, m_conv_b_w, m_w_b_out, m_w_out, m_final_g, v_meta_tokens, v_norm_g, v_w_in, v_conv_a_w, v_conv_a_b, v_ln_a_g, v_ln_a_b, v_w_a_out, v_b_a_out, v_conv_b_w, v_w_b_out, v_w_out, v_final_g):
    seq, d = x.shape[1], x.shape[2]
    dc = meta_tokens.shape[1]
    sw = w_in.shape[2]
    rsh = w_a_out.shape[1]
    xi, yi, ci = _mesh_pos()
    me = 2 * xi + yi
    pos = jnp.stack([ci, me]).astype(jnp.int32)

    conv_rows = HALO_A + HALO_B + 8
    convs = jnp.concatenate([
        jnp.pad(conv_a_w[0], ((0, HALO_A - CONV_A), (0, 0))),
        jnp.pad(conv_b_w[0], ((0, HALO_B - CONV_B), (0, 0))), jnp.zeros((8, dc), F32)], axis=0)[None]
    w3_own = jnp.stack([w_a_out[0], w_b_out[0], w_out[0]])
    fg2 = final_g.reshape(1, d)
    xs = x[0]

    h, front = _h_prep(xs, meta_tokens, norm_g)
    proj, (wg_in, wg3, convg) = _proj_fwd2(h, [_place_own(w_in, pos, BF16, "place_in"),
                                               _place_own(w3_own, pos, BF16, "place_sq"),
                                               _place_own(convs, pos, F32, "place_conv")], pos)
    w3 = wg3.reshape(3, N_CHIPS * rsh, d)
    convg = jnp.transpose(convg[0], (1, 0, 2)).reshape(conv_rows, N_CHIPS * dc)
    wa_full = convg[0:HALO_A]
    wb_full = convg[HALO_A:HALO_A + HALO_B]
    ca, cb, ya, yb, abm_t, ds2, h_t, loss8, dfg8 = _mix_fwd(
        xs, front, proj, loss_target[0], w3, wa_full, wb_full, conv_a_b, ln_a_g, ln_a_b, b_a_out, fg2, norm_g)
    dproj, d3, sm = _mix_bwd(ds2, proj, ca, cb, ya, yb, w3, wa_full, wb_full, ln_a_g, ln_a_b)
    cw_sq = _col_block(d, 512)
    per_sq = d // cw_sq
    p32_sq, pbf_sq, _ = _dw_reduced(
        abm_t, d3, cw_sq, 3 * per_sq, lambda t: (t // per_sq, t % per_sq), N_CHIPS,
        (3, N_CHIPS, rsh // 2, d), (None, N_CHIPS, rsh // 2, cw_sq),
        lambda u: (u // per_sq, 0, 0, u % per_sq), [], "dw_square")
    cw_in = _col_block(sw, 768)
    ncol = sw // cw_in
    p32_in, pbf_in, (l_sq,) = _dw_reduced(
        h_t[None], dproj[None], cw_in, N_CHIPS * ncol, lambda t: (0, t), 1,
        (1, N_CHIPS, d // 2, sw), (None, None, d // 2, cw_in), lambda u: (0, u // ncol, 0, u % ncol),
        [pbf_sq], "dw_in")
    grad_x, dfront, dng8, l_in = _dh_bwd(dproj, wg_in, xs, front, ds2, norm_g, pbf_in,
                                         lax.empty(pbf_in.shape, BF16), None)
    half_in = _sum_chips([(p32_in, l_in)], sw, pos, "rs_sum_in")
    half_sq = _sum_chips([(p32_sq, l_sq)], d, pos, "rs_sum_sq")
    tail_row = lax.broadcasted_iota(jnp.int32, (8, d), 0)
    tail = jnp.where(tail_row == 0, dng8, jnp.where(tail_row == 1, dfg8,
                     jnp.where(tail_row == 2, loss8[0, 0], 0.0)))
    block = jnp.concatenate([sm, dfront[TM - N_META:TM], tail], axis=0)
    (other_in, other_sq), red = _sibling_swap([half_in, half_sq], block)
    col = lax.dynamic_slice(red, (0, me * dc), (AR_ROWS, dc))
    g_small = {
        "meta_tokens": col[ROW_DMETA:ROW_DMETA + N_META],
        "norm_g": red[ROW_DNG:ROW_DNG + 1],
        "conv_a_w": col[ROW_DWA:ROW_DWA + CONV_A][None],
        "conv_a_b": red[ROW_DCAB:ROW_DCAB + 1],
        "ln_a_g": red[ROW_DLNG:ROW_DLNG + 1],
        "ln_a_b": red[ROW_DLNB:ROW_DLNB + 1],
        "b_a_out": red[ROW_DBAO:ROW_DBAO + 1],
        "conv_b_w": col[ROW_DWB:ROW_DWB + CONV_B][None],
        "final_g": red[ROW_DFG],
    }

    upd_in = _adam_halves([w_in], [m_w_in], [v_w_in], half_in, other_in, pos, "adam_in")
    upd_sq = _adam_halves([w_a_out, w_b_out, w_out], [m_w_a_out, m_w_b_out, m_w_out],
                          [v_w_a_out, v_w_b_out, v_w_out], half_sq, other_sq, pos, "adam_sq")
    small_w = {"meta_tokens": (meta_tokens, m_meta_tokens, v_meta_tokens), "norm_g": (norm_g, m_norm_g, v_norm_g),
               "conv_a_w": (conv_a_w, m_conv_a_w, v_conv_a_w), "conv_a_b": (conv_a_b, m_conv_a_b, v_conv_a_b),
               "ln_a_g": (ln_a_g, m_ln_a_g, v_ln_a_g), "ln_a_b": (ln_a_b, m_ln_a_b, v_ln_a_b),
               "b_a_out": (b_a_out, m_b_a_out, v_b_a_out), "conv_b_w": (conv_b_w, m_conv_b_w, v_conv_b_w),
               "final_g": (final_g, m_final_g, v_final_g)}
    names_small = list(small_w)
    as2d = lambda t: t.reshape(-1, t.shape[-1])
    upd_small = _adam_small([(as2d(small_w[k][0]), as2d(g_small[k]), as2d(small_w[k][1]), as2d(small_w[k][2]))
                             for k in names_small])

    grads, deltas, new_m, new_v = dict(g_small), {}, {}, {}
    for k, upd in zip(names_small, upd_small):
        deltas[k], new_m[k], new_v[k] = [t.reshape(small_w[k][0].shape) for t in upd]
    grads["w_in"], deltas["w_in"], new_m["w_in"], new_v["w_in"] = upd_in
    for idx, k in enumerate(["w_a_out", "w_b_out", "w_out"]):
        grads[k], deltas[k], new_m[k], new_v[k] = upd_sq[4 * idx:4 * idx + 4]

    loss = red[ROW_LOSS, 0]
    order = ["meta_tokens", "norm_g", "w_in", "conv_a_w", "conv_a_b", "ln_a_g", "ln_a_b", "w_a_out", "b_a_out",
             "conv_b_w", "w_b_out", "w_out", "final_g"]
    return (loss, grad_x[None], *[grads[k] for k in order], *[deltas[k] for k in order],
            *[new_m[k] for k in order], *[new_v[k] for k in order])
```

```python
import functools

import jax
import jax.numpy as jnp
from jax import lax
from jax.experimental import pallas as pl
from jax.experimental.pallas import tpu as pltpu

F32 = jnp.float32
BF16 = jnp.bfloat16
MESH = pl.DeviceIdType.MESH

EPS = 1e-6
N_META = 16
N_SPLIT = 9
CONV_A = 31
CONV_B = 3
HALO_A = 32
HALO_B = 8
SHIFT_ROWS = 24
TM = 256
RB = 64
N_ROW_TILES_BIG = 8
ROW_BLOCK = 256
N_CHIPS = 4
VMEM_LIMIT = 56 * 1024 * 1024
VMEM_LIMIT_BIG = 62 * 1024 * 1024

ADAM_LR = 0.001
ADAM_B1 = 0.9
ADAM_B2 = 0.999
ADAM_EPS = 1e-08
ADAM_WD = 0.01
ADAM_STEP = 10

ROW_DWA = 0
ROW_DWB = 32
ROW_DCAB = 40
ROW_DLNG = 41
ROW_DLNB = 42
ROW_DBAO = 43
SM_ROWS = 48
ROW_DMETA = 48
ROW_DNG = 64
ROW_DFG = 65
ROW_LOSS = 66
AR_ROWS = 72


def _sigmoid(v):
    return 0.5 * jnp.tanh(0.5 * v) + 0.5


def _params(sem, **kw):
    return pltpu.CompilerParams(dimension_semantics=sem, vmem_limit_bytes=VMEM_LIMIT, **kw)


def _rows(rb):
    return pl.ds(pl.multiple_of(rb * RB, RB), RB)


def _mesh_pos():
    x, y, c = lax.axis_index("x"), lax.axis_index("y"), lax.axis_index("c")
    return x, y, c


def _half(ref, j, c):
    h = ref.shape[2] // 2
    return ref.at[:, j, pl.ds(c * h, h), :]


def _place_own(shard, pos, dtype, name):
    s, r, c = shard.shape
    rb = ROW_BLOCK if r % ROW_BLOCK == 0 else r

    def body(pos_ref, x_ref, o_ref):
        o_ref[...] = x_ref[...].astype(dtype)

    return pl.pallas_call(
        body, name=name,
        grid_spec=pltpu.PrefetchScalarGridSpec(
            num_scalar_prefetch=1, grid=(s, r // rb),
            in_specs=[pl.BlockSpec((None, rb, c), lambda si, b, pos_ref: (si, b, 0))],
            out_specs=pl.BlockSpec((None, None, rb, c), lambda si, b, pos_ref: (si, pos_ref[1], b, 0))),
        out_shape=jax.ShapeDtypeStruct((s, N_CHIPS, r, c), dtype),
        compiler_params=_params(("arbitrary",) * 2),
    )(pos, shard)


class _Exchange:
    def __init__(self, sends, recvs):
        self.sends, self.recvs = sends, recvs

    @staticmethod
    def _each(pairs, act):
        for cond, cp in pairs:
            if cond is None:
                act(cp)
            else:
                pl.when(cond)(functools.partial(act, cp))

    def start(self):
        self._each(self.sends, lambda cp: cp.start())

    def finish(self):
        self._each(self.recvs, lambda cp: cp.wait_recv())
        self._each(self.sends, lambda cp: cp.wait_send())


def _chip_exchange(part_ref, land_ref, send_sems, recv_sems, half=None):
    x, y, c = _mesh_pos()
    me = 2 * x + y
    sends, recvs = [], []
    for k, (px, py) in enumerate([(1 - x, y), (x, 1 - y), (1 - x, 1 - y)]):
        sems = dict(send_sem=send_sems.at[k], recv_sem=recv_sems.at[k], device_id=(px, py, c), device_id_type=MESH)
        slot = 2 * px + py if half is None else py
        sends.append((None if half is None else px == half, pltpu.make_async_remote_copy(
            src_ref=part_ref.at[:, slot], dst_ref=land_ref.at[:, me], **sems)))
        landed = land_ref.at[:, 2 * px + py]
        recvs.append((None if half is None else x == half,
                      pltpu.make_async_remote_copy(src_ref=landed, dst_ref=landed, **sems)))
    return _Exchange(sends, recvs)

def _sibling_swap(halves, small):
    n = len(halves)

    def body(*refs):
        ins, small_ref, outs, red_ref = refs[:n], refs[n], refs[n + 1:2 * n + 1], refs[2 * n + 1]
        send_sems, recv_sems = refs[2 * n + 2:2 * n + 4]
        reduce = _SmallAllReduce(small_ref, red_ref, *refs[2 * n + 4:])
        x, y, c = _mesh_pos()
        copies = [pltpu.make_async_remote_copy(
            src_ref=ins[a], dst_ref=outs[a], send_sem=send_sems.at[a], recv_sem=recv_sems.at[a],
            device_id=(x, y, 1 - c), device_id_type=MESH) for a in range(n)]
        reduce.start()
        for cp in copies:
            cp.start()
        reduce.between_chips()
        reduce.finish()
        for cp in copies:
            cp.wait()

    any_spec = pl.BlockSpec(memory_space=pl.ANY)
    vm = pl.BlockSpec(memory_space=pltpu.VMEM)
    outs = pl.pallas_call(
        body, name="rs_swap",
        in_specs=[any_spec] * n + [vm], out_specs=[any_spec] * n + [vm],
        out_shape=[jax.ShapeDtypeStruct(h.shape, h.dtype) for h in halves]
        + [jax.ShapeDtypeStruct(small.shape, F32)],
        scratch_shapes=[pltpu.SemaphoreType.DMA((n,)), pltpu.SemaphoreType.DMA((n,))]
        + _SmallAllReduce.scratch(*small.shape),
    )(*halves, small)
    return outs[:n], outs[n]


class _SmallAllReduce:
    def __init__(self, x_ref, out_ref, sib_ref, part_ref, peers_ref, send_sems, recv_sems):
        self.x_ref, self.out_ref, self.sib_ref, self.part_ref, self.peers_ref = x_ref, out_ref, sib_ref, part_ref, peers_ref
        x, y, c = _mesh_pos()
        self.me = 2 * x + y
        self.swap = pltpu.make_async_remote_copy(
            src_ref=x_ref, dst_ref=sib_ref, send_sem=send_sems.at[0], recv_sem=recv_sems.at[0],
            device_id=(x, y, 1 - c), device_id_type=MESH)
        self.sends, self.recvs = [], []
        for k, (px, py) in enumerate([(1 - x, y), (x, 1 - y), (1 - x, 1 - y)]):
            sems = dict(send_sem=send_sems.at[1 + k], recv_sem=recv_sems.at[1 + k],
                        device_id=(px, py, c), device_id_type=MESH)
            self.sends.append(pltpu.make_async_remote_copy(src_ref=part_ref, dst_ref=peers_ref.at[self.me], **sems))
            landed = peers_ref.at[2 * px + py]
            self.recvs.append(pltpu.make_async_remote_copy(src_ref=landed, dst_ref=landed, **sems))

    @staticmethod
    def scratch(rows, d):
        return [pltpu.VMEM((rows, d), F32), pltpu.VMEM((rows, d), F32), pltpu.VMEM((N_CHIPS, rows, d), F32),
                pltpu.SemaphoreType.DMA((4,)), pltpu.SemaphoreType.DMA((4,))]

    def start(self):
        self.swap.start()

    def between_chips(self):
        self.swap.wait()
        self.part_ref[...] = self.x_ref[...] + self.sib_ref[...]
        self.peers_ref[self.me] = self.part_ref[...]
        for cp in self.sends:
            cp.start()

    def finish(self):
        for cp in self.recvs:
            cp.wait_recv()
        for cp in self.sends:
            cp.wait_send()
        p = self.peers_ref
        self.out_ref[...] = ((p[0] + p[1]) + p[2]) + p[3]


def _sum_chips(parts, cw, pos, name):
    s, _, h, _ = parts[0][0].shape
    hb = min(h, ROW_BLOCK)
    widths = [own.shape[3] // cw for own, _ in parts]
    starts = [sum(widths[:a]) for a in range(len(parts))]

    def body(pos_ref, *refs):
        out_ref = refs[-1]
        n = pl.program_id(2)
        total = None
        for a in range(len(parts)):
            own, l1, l2, l3 = refs[4 * a:4 * a + 4]
            val = ((own[...] + l1[...].astype(F32)) + l2[...].astype(F32)) + l3[...].astype(F32)
            total = val if total is None else jnp.where(n >= starts[a], val, total)
        out_ref[...] = total

    def slot(a, k):
        col = lambda n: jnp.clip(n - starts[a], 0, widths[a] - 1)
        return pl.BlockSpec((None, None, hb, cw),
                            lambda si, b, n, pos_ref: (si, (pos_ref[1] + k) % N_CHIPS, b, col(n)))

    operands, specs = [], []
    for a, (own, landed) in enumerate(parts):
        operands += [own, landed, landed, landed]
        specs += [slot(a, 0), slot(a, 1), slot(a, 2), slot(a, 3)]
    return pl.pallas_call(
        body, name=name,
        grid_spec=pltpu.PrefetchScalarGridSpec(
            num_scalar_prefetch=1, grid=(s, h // hb, sum(widths)), in_specs=specs,
            out_specs=pl.BlockSpec((None, hb, cw), lambda si, b, n, pos_ref: (si, b, n))),
        out_shape=jax.ShapeDtypeStruct((s, h, sum(widths) * cw), F32),
        compiler_params=_params(("arbitrary",) * 3),
    )(pos, *operands)


def _adamw(w, g, m, v):
    m = ADAM_B1 * m + (1.0 - ADAM_B1) * g
    v = ADAM_B2 * v + (1.0 - ADAM_B2) * (g * g)
    m_hat = m / (1.0 - ADAM_B1 ** ADAM_STEP)
    v_hat = v / (1.0 - ADAM_B2 ** ADAM_STEP)
    delta = -ADAM_LR * (m_hat / (jnp.sqrt(v_hat) + ADAM_EPS) + ADAM_WD * w)
    return delta, m, v


def _adam_halves(ws, ms, vs, g_own, g_recv, pos, name):
    n = len(ws)
    _, r, c = ws[0].shape
    h = r // 2
    rb = min(h, ROW_BLOCK)
    nb = h // rb

    def body(pos_ref, *refs):
        w_refs, m_refs, v_refs = refs[:n], refs[n:2 * n], refs[2 * n:3 * n]
        go_ref, gr_ref = refs[3 * n:3 * n + 2]
        outs = refs[3 * n + 2:]
        mine = pl.program_id(0) == pos_ref[0]
        for a in range(n):
            g = jnp.where(mine, go_ref[a], gr_ref[a])
            delta, m, v = _adamw(w_refs[a][...], g, m_refs[a][...], v_refs[a][...])
            outs[4 * a][...], outs[4 * a + 1][...], outs[4 * a + 2][...], outs[4 * a + 3][...] = g, delta, m, v

    spec_w = pl.BlockSpec((None, rb, c), lambda hf, b, pos_ref: (0, hf * nb + b, 0))
    spec_g = pl.BlockSpec((n, rb, c), lambda hf, b, pos_ref: (0, b, 0))
    return pl.pallas_call(
        body, name=name,
        grid_spec=pltpu.PrefetchScalarGridSpec(
            num_scalar_prefetch=1, grid=(2, nb), in_specs=[spec_w] * (3 * n) + [spec_g] * 2,
            out_specs=[spec_w] * (4 * n)),
        out_shape=[jax.ShapeDtypeStruct((1, r, c), F32)] * (4 * n),
        compiler_params=_params(("arbitrary",) * 2),
    )(pos, *ws, *ms, *vs, g_own, g_recv)


def _adam_small(items):
    n = len(items)

    def body(*refs):
        ins, outs = refs[:4 * n], refs[4 * n:]
        for a in range(n):
            w_ref, g_ref, m_ref, v_ref = ins[4 * a:4 * a + 4]
            d, m, v = _adamw(w_ref[...], g_ref[...], m_ref[...], v_ref[...])
            outs[3 * a][...] = d
            outs[3 * a + 1][...] = m
            outs[3 * a + 2][...] = v

    vm = pl.BlockSpec(memory_space=pltpu.VMEM)
    flat = [t for it in items for t in it]
    outs = pl.pallas_call(
        body, name="adam_small", in_specs=[vm] * (4 * n), out_specs=[vm] * (3 * n),
        out_shape=[jax.ShapeDtypeStruct(it[0].shape, F32) for it in items for _ in range(3)],
    )(*flat)
    return [tuple(outs[3 * a:3 * a + 3]) for a in range(n)]


def _shard_of_step(js, me):
    flip = jnp.where(js == 1, 2, jnp.where(js == 2, 1, jnp.where(js == 3, 3, 0)))
    return lax.bitwise_xor(me, flip)


def _big_row_spec(seq, tmb, d, tile_of):
    return pl.BlockSpec((pl.Element(tmb), pl.Element(d)),
                        lambda *args: (pl.multiple_of(jnp.minimum(tile_of(*args) * tmb, seq - tmb), 8), 0))


def _big_row_tile(x_ref, front_ref, i):
    rows = x_ref[...]
    last = jnp.concatenate([rows[TM:], front_ref[...]], axis=0)
    return jnp.where(i == N_ROW_TILES_BIG - 1, last, rows)


def _h_prep(x, meta, norm_g):
    seq, d = x.shape
    tp = seq + TM
    dc = meta.shape[1]
    tmb = tp // N_ROW_TILES_BIG
    assert tmb >= TM and tp == tmb * N_ROW_TILES_BIG
    last = N_ROW_TILES_BIG - 1

    def body(x_ref, meta_ref, g_ref, h_ref, front_ref, metas, msend, mrecv):
        x, y, c = _mesh_pos()
        me = 2 * x + y
        chips = [(1 - x, y), (x, 1 - y), (1 - x, 1 - y)]
        i = pl.program_id(0)

        def meta_copy(k, chip):
            return pltpu.make_async_remote_copy(
                src_ref=metas.at[chip], dst_ref=metas.at[chip], send_sem=msend.at[k], recv_sem=mrecv.at[k],
                device_id=(*chips[k], c), device_id_type=MESH)

        @pl.when(i == 0)
        def _():
            metas[me] = meta_ref[...]
            for k in range(3):
                meta_copy(k, me).start()
            front_ref[...] = jnp.zeros_like(front_ref)

        @pl.when(i == last)
        def _():
            for k, (px, py) in enumerate(chips):
                meta_copy(k, 2 * px + py).wait_recv()
            for q in range(N_CHIPS):
                front_ref[TM - N_META:TM, q * dc:(q + 1) * dc] = metas[q]

        s = _big_row_tile(x_ref, front_ref, i)
        r = lax.rsqrt(jnp.mean(s * s, axis=-1, keepdims=True) + EPS)
        h_ref[...] = (s * r * g_ref[...]).astype(BF16)

        @pl.when(i == last)
        def _():
            for k in range(3):
                meta_copy(k, me).wait_send()

    return pl.pallas_call(
        body, name="f0_norm", grid=(N_ROW_TILES_BIG,),
        in_specs=[_big_row_spec(seq, tmb, d, lambda i: i), pl.BlockSpec(meta.shape, lambda i: (0, 0)),
                  pl.BlockSpec((1, d), lambda i: (0, 0))],
        out_specs=[pl.BlockSpec((tmb, d), lambda i: (i, 0)), pl.BlockSpec((TM, d), lambda i: (0, 0))],
        out_shape=[jax.ShapeDtypeStruct((tp, d), BF16), jax.ShapeDtypeStruct((TM, d), F32)],
        scratch_shapes=[pltpu.VMEM((N_CHIPS,) + meta.shape, F32),
                        pltpu.SemaphoreType.DMA((3,)), pltpu.SemaphoreType.DMA((3,))],
        compiler_params=_params(("arbitrary",)),
    )(x, meta, norm_g)


N_UNITS = 3
N_STEPS_PROJ = N_CHIPS * N_UNITS


def _proj_plan(u):
    v = u - N_UNITS
    if v < 2 * N_UNITS:
        return v % 2, v // 2
    return 2, v - 2 * N_UNITS


def _proj_unit(t, me):
    v = t - N_UNITS
    near = v < 2 * N_UNITS
    rel = jnp.where(near, v % 2, 2)
    unit = jnp.where(near, v // 2, v - 2 * N_UNITS)
    flip = jnp.where(rel == 0, 2, jnp.where(rel == 1, 1, 3))
    own = t < N_UNITS
    return jnp.where(own, me, lax.bitwise_xor(me, flip)), jnp.where(own, t, unit)


def _proj_fwd2(h, bufs, pos):
    tp, d = h.shape
    _, nsh, _, sw = bufs[0].shape
    cu = sw // N_UNITS
    assert cu % 128 == 0
    n = len(bufs)
    w_sems = 6 * N_UNITS
    last = N_STEPS_PROJ - 1
    late = N_STEPS_PROJ - N_UNITS

    def body(pos_ref, h_ref, *refs):
        proj_ref = refs[n]
        gbufs = refs[n + 1:2 * n + 1]
        wbuf, wsems, send_sems, recv_sems = refs[2 * n + 1:]
        x, y, c = _mesh_pos()
        me = 2 * x + y
        sibling = (x, y, 1 - c)
        chips = [(1 - x, y), (x, 1 - y), (1 - x, 1 - y)]
        chip_ids = [2 * px + py for px, py in chips]
        relayed_chip = jnp.where(c == 0, chip_ids[0], chip_ids[1])
        relay_to = (jnp.where(c == 0, x, 1 - x), jnp.where(c == 0, 1 - y, y), c)
        t = pl.program_id(0)

        def remote(idx, piece, to):
            return pltpu.make_async_remote_copy(
                src_ref=piece, dst_ref=piece, send_sem=send_sems.at[idx], recv_sem=recv_sems.at[idx],
                device_id=to, device_id_type=MESH)

        hr = d // 2

        def chunk_of(chip, half, k):
            return gbufs[0].at[0, chip, pl.ds(half * hr, hr), pl.ds(k * cu, cu)]

        def own_chunk(r, k):
            return remote(6 * k + r, chunk_of(me, c, k), (*chips[r], c))

        def landed_chunk(r, k):
            return remote(6 * k + r, chunk_of(chip_ids[r], c, k), (*chips[r], c))

        def relay_chunk(k):
            return remote(6 * k + 2, chunk_of(relayed_chip, c, k), relay_to)

        def sibling_chunk(r, k, half):
            return remote(6 * k + 3 + r, chunk_of(chip_ids[r], half, k), sibling)

        def fetch(u):
            chip, unit = _proj_unit(jnp.int32(u), me)
            return pltpu.make_async_copy(gbufs[0].at[0, chip, :, pl.ds(pl.multiple_of(unit * cu, 128), cu)],
                                         wbuf.at[u % 2], wsems.at[u % 2])

        def make_available(u):
            r, k = _proj_plan(u)
            landed_chunk(r, k).wait_recv()
            if r < 2:
                pl.when(c == r)(lambda: relay_chunk(k).start())
            sibling_chunk(r, k, c).start()
            sibling_chunk(r, k, 1 - c).wait_recv()

        def own_piece(a, r):
            return remote(w_sems + 6 * (a - 1) + r, _half(gbufs[a], me, c), (*chips[r], c))

        def relay(a):
            return remote(w_sems + 6 * (a - 1) + 2, _half(gbufs[a], relayed_chip, c), relay_to)

        def to_sibling(a, r, core):
            return remote(w_sems + 6 * (a - 1) + 3 + r, _half(gbufs[a], chip_ids[r], core), sibling)

        def landed(a, r):
            return remote(w_sems + 6 * (a - 1) + r, _half(gbufs[a], chip_ids[r], c), (*chips[r], c))

        for u in range(N_STEPS_PROJ):
            @pl.when(t == u)
            def _(u=u):
                if u == 0:
                    for k in range(N_UNITS):
                        for r in range(2):
                            own_chunk(r, k).start()
                    for a in range(1, n):
                        for r in range(2):
                            own_piece(a, r).start()
                    fetch(0).start()
                if u < last:
                    if u + 1 >= N_UNITS:
                        make_available(u + 1)
                    fetch(u + 1).start()
                if u == late:
                    for a in range(1, n):
                        landed(a, 0).wait_recv()
                        landed(a, 1).wait_recv()
                        relay(a).start()
                        for r in range(2):
                            to_sibling(a, r, c).start()
                        for r in range(2):
                            to_sibling(a, r, 1 - c).wait_recv()
                fetch(u).wait()

        proj_ref[...] = jnp.dot(h_ref[...], wbuf[t % 2], preferred_element_type=F32).astype(BF16)

        @pl.when(t == last)
        def _():
            for a in range(1, n):
                landed(a, 2).wait_recv()
                to_sibling(a, 2, c).start()
                to_sibling(a, 2, 1 - c).wait_recv()
            for k in range(N_UNITS):
                for r in range(2):
                    own_chunk(r, k).wait_send()
                relay_chunk(k).wait_send()
                for r in range(3):
                    sibling_chunk(r, k, c).wait_send()
            for a in range(1, n):
                for r in range(2):
                    own_piece(a, r).wait_send()
                relay(a).wait_send()
                for r in range(3):
                    to_sibling(a, r, c).wait_send()

    def out_index(t, pos_ref):
        chip, unit = _proj_unit(t, pos_ref[1])
        return 0, chip * N_UNITS + unit

    any_spec = pl.BlockSpec(memory_space=pl.ANY)
    outs = pl.pallas_call(
        body, name="f1_proj",
        grid_spec=pltpu.PrefetchScalarGridSpec(
            num_scalar_prefetch=1, grid=(N_STEPS_PROJ,),
            in_specs=[pl.BlockSpec((tp, d), lambda t, pos_ref: (0, 0))] + [any_spec] * n,
            out_specs=[pl.BlockSpec((tp, cu), out_index)] + [any_spec] * n,
            scratch_shapes=[pltpu.VMEM((2, d, cu), BF16), pltpu.SemaphoreType.DMA((2,)),
                            pltpu.SemaphoreType.DMA((w_sems + 6 * (n - 1),)),
                            pltpu.SemaphoreType.DMA((w_sems + 6 * (n - 1),))]),
        out_shape=[jax.ShapeDtypeStruct((tp, nsh * sw), BF16)]
        + [jax.ShapeDtypeStruct(b.shape, b.dtype) for b in bufs],
        input_output_aliases={2 + a: 1 + a for a in range(n)},
        compiler_params=_params(("arbitrary",)),
    )(pos, h, *bufs)
    return outs[0], outs[1:]


def _proj_fwd(x, meta, norm_g, bufs, pos):
    seq, d = x.shape
    tp = seq + TM
    _, nsh, _, sw = bufs[0].shape
    dc = meta.shape[1]
    tmb = tp // N_ROW_TILES_BIG
    assert tmb >= TM and tp == tmb * N_ROW_TILES_BIG
    n = len(bufs)

    def body(pos_ref, x_ref, meta_ref, g_ref, *refs):
        proj_ref, front_ref = refs[n], refs[n + 1]
        gbufs = refs[n + 2:2 * n + 2]
        wbuf, wsems, send_sems, recv_sems, metas, msend, mrecv = refs[2 * n + 2:]
        x, y, c = _mesh_pos()
        me = 2 * x + y
        sibling = (x, y, 1 - c)
        chips = [(1 - x, y), (x, 1 - y), (1 - x, 1 - y)]
        js, i = pl.program_id(0), pl.program_id(1)

        def meta_copy(k, chip):
            return pltpu.make_async_remote_copy(
                src_ref=metas.at[chip], dst_ref=metas.at[chip], send_sem=msend.at[k], recv_sem=mrecv.at[k],
                device_id=(*chips[k], c), device_id_type=MESH)

        @pl.when((js == 0) & (i == 0))
        def _():
            metas[me] = meta_ref[...]
            for k in range(3):
                meta_copy(k, me).start()
            front_ref[...] = jnp.zeros_like(front_ref)

        @pl.when((js == 0) & (i == N_ROW_TILES_BIG - 1))
        def _():
            for k, (px, py) in enumerate(chips):
                meta_copy(k, 2 * px + py).wait_recv()
            for q in range(N_CHIPS):
                front_ref[TM - N_META:TM, q * dc:(q + 1) * dc] = metas[q]

        def remote(a, k, piece, to):
            return pltpu.make_async_remote_copy(
                src_ref=piece, dst_ref=piece, send_sem=send_sems.at[6 * a + k],
                recv_sem=recv_sems.at[6 * a + k], device_id=to, device_id_type=MESH)

        def fetch(chip, step):
            return pltpu.make_async_copy(gbufs[0].at[0, chip], wbuf.at[step % 2], wsems.at[step % 2])

        chip_ids = [2 * px + py for px, py in chips]
        relayed_chip = jnp.where(c == 0, chip_ids[0], chip_ids[1])
        relay_to = (jnp.where(c == 0, x, 1 - x), jnp.where(c == 0, 1 - y, y), c)

        def own_piece(a, k):
            return remote(a, k, _half(gbufs[a], me, c), (*chips[k], c))

        def relay(a):
            return remote(a, 2, _half(gbufs[a], relayed_chip, c), relay_to)

        def to_sibling(a, k, core):
            return remote(a, 3 + k, _half(gbufs[a], chip_ids[k], core), sibling)

        def landed(a, k):
            return remote(a, k, _half(gbufs[a], chip_ids[k], c), (*chips[k], c))

        def take_neighbours(a):
            landed(a, 0).wait_recv()
            landed(a, 1).wait_recv()
            relay(a).start()
            for k in range(2):
                to_sibling(a, k, c).start()
            for k in range(2):
                to_sibling(a, k, 1 - c).wait_recv()

        def take_diagonal(a):
            landed(a, 2).wait_recv()
            to_sibling(a, 2, c).start()
            to_sibling(a, 2, 1 - c).wait_recv()

        @pl.when((js == 0) & (i == 0))
        def _():
            for a in range(n):
                for k in range(2):
                    own_piece(a, k).start()
            fetch(me, 0).start()
            fetch(me, 0).wait()

        @pl.when((js == 1) & (i == 0))
        def _():
            take_neighbours(0)
            fetch(chip_ids[0], 1).start()
            fetch(chip_ids[0], 1).wait()

        @pl.when((js == 1) & (i == N_ROW_TILES_BIG - 2))
        def _():
            fetch(chip_ids[1], 2).start()

        @pl.when((js == 2) & (i == 0))
        def _():
            fetch(chip_ids[1], 2).wait()

        @pl.when((js == 2) & (i == 1))
        def _():
            for a in range(1, n):
                take_neighbours(a)

        @pl.when((js == 2) & (i == N_ROW_TILES_BIG - 2))
        def _():
            take_diagonal(0)
            fetch(chip_ids[2], 3).start()

        @pl.when((js == 3) & (i == 0))
        def _():
            fetch(chip_ids[2], 3).wait()

        s = _big_row_tile(x_ref, front_ref, i)
        r = lax.rsqrt(jnp.mean(s * s, axis=-1, keepdims=True) + EPS)
        h = (s * r * g_ref[...]).astype(BF16)
        proj_ref[...] = jnp.dot(h, wbuf[js % 2], preferred_element_type=F32).astype(BF16)

        @pl.when((js == nsh - 1) & (i == N_ROW_TILES_BIG - 1))
        def _():
            for a in range(1, n):
                take_diagonal(a)
            for a in range(n):
                for k in range(2):
                    own_piece(a, k).wait_send()
                relay(a).wait_send()
                for k in range(3):
                    to_sibling(a, k, c).wait_send()
            for k in range(3):
                meta_copy(k, me).wait_send()

    any_spec = pl.BlockSpec(memory_space=pl.ANY)
    outs = pl.pallas_call(
        body, name="f1_proj",
        grid_spec=pltpu.PrefetchScalarGridSpec(
            num_scalar_prefetch=1, grid=(nsh, N_ROW_TILES_BIG),
            in_specs=[_big_row_spec(seq, tmb, d, lambda js, i, pos_ref: i),
                      pl.BlockSpec(meta.shape, lambda js, i, pos_ref: (0, 0)),
                      pl.BlockSpec((1, d), lambda js, i, pos_ref: (0, 0))] + [any_spec] * n,
            out_specs=[pl.BlockSpec((tmb, sw), lambda js, i, pos_ref: (i, _shard_of_step(js, pos_ref[1]))),
                       pl.BlockSpec((TM, d), lambda js, i, pos_ref: (0, 0))] + [any_spec] * n,
            scratch_shapes=[pltpu.VMEM((2, d, sw), BF16), pltpu.SemaphoreType.DMA((2,)),
                            pltpu.SemaphoreType.DMA((6 * n,)), pltpu.SemaphoreType.DMA((6 * n,)),
                            pltpu.VMEM((N_CHIPS,) + meta.shape, F32),
                            pltpu.SemaphoreType.DMA((3,)), pltpu.SemaphoreType.DMA((3,))]),
        out_shape=[jax.ShapeDtypeStruct((tp, nsh * sw), BF16), jax.ShapeDtypeStruct((TM, d), F32)]
        + [jax.ShapeDtypeStruct(b.shape, b.dtype) for b in bufs],
        input_output_aliases={4 + a: 2 + a for a in range(n)},
        compiler_params=_params(("arbitrary", "arbitrary")),
    )(pos, x, meta, norm_g, *bufs)
    return outs[0], outs[1], outs[2:]


def _dh_bwd(dproj, wg_in, x, front, ds2, norm_g, part, land, half):
    seq, d = x.shape
    tp = seq + TM
    _, nsh, _, sw = wg_in.shape
    tmb = tp // N_ROW_TILES_BIG
    tail = tmb - TM
    last = N_ROW_TILES_BIG - 1

    def body(dp_ref, w_hbm, x_ref, front_ref, ds2_ref, g_ref, part_ref, _, gx_hbm, dfront_ref, dng_ref, land_ref,
             wbuf, gacc, dsbuf, wsem, osems, send_sems, recv_sems):
        exchange = _chip_exchange(part_ref, land_ref, send_sems, recv_sems, half)
        i = pl.program_id(0)

        def x_rows_out(step):
            return pltpu.make_async_copy(dsbuf.at[step % 2], gx_hbm.at[pl.ds(step * tmb, tmb), :], osems.at[step % 2])

        last_out = pltpu.make_async_copy(dsbuf.at[last % 2, pl.ds(0, tail), :],
                                         gx_hbm.at[pl.ds(last * tmb, tail), :], osems.at[last % 2])

        @pl.when(i == 0)
        def _():
            exchange.start()
            gacc[...] = jnp.zeros_like(gacc)
            whole = pltpu.make_async_copy(w_hbm.at[0], wbuf, wsem)
            whole.start()
            whole.wait()

        dh = None
        for j in range(nsh):
            part = lax.dot_general(dp_ref[:, j * sw:(j + 1) * sw], wbuf[j], (((1,), (1,)), ((), ())),
                                   preferred_element_type=F32)
            dh = part if dh is None else dh + part
        s = _big_row_tile(x_ref, front_ref, i)
        r = lax.rsqrt(jnp.mean(s * s, axis=-1, keepdims=True) + EPS)
        gacc[...] += (dh * s * r).reshape(tmb // 8, 8, d).sum(axis=0)
        t = dh * g_ref[...]

        @pl.when(i >= 2)
        def _():
            x_rows_out(i - 2).wait()

        dsbuf[i % 2] = ds2_ref[...] + r * t - s * (r * r * r) * jnp.mean(t * s, axis=-1, keepdims=True)

        @pl.when(i < last)
        def _():
            x_rows_out(i).start()

        @pl.when(i == last)
        def _():
            last_out.start()
            dfront_ref[...] = dsbuf[last % 2, tail:, :]
            dng_ref[...] = jnp.broadcast_to(jnp.sum(gacc[...], axis=0, keepdims=True), (8, d))
            exchange.finish()
            x_rows_out(last - 1).wait()
            last_out.wait()

    any_spec = pl.BlockSpec(memory_space=pl.ANY)
    return pl.pallas_call(
        body, name="b2_dh", grid=(N_ROW_TILES_BIG,),
        in_specs=[pl.BlockSpec((tmb, nsh * sw), lambda i: (i, 0)), any_spec,
                  _big_row_spec(seq, tmb, d, lambda i: i),
                  pl.BlockSpec((TM, d), lambda i: (0, 0)),
                  pl.BlockSpec((tmb, d), lambda i: (i, 0)),
                  pl.BlockSpec((1, d), lambda i: (0, 0)), any_spec, any_spec],
        out_specs=[any_spec, pl.BlockSpec((TM, d), lambda i: (0, 0)),
                   pl.BlockSpec((8, d), lambda i: (0, 0)), any_spec],
        out_shape=[jax.ShapeDtypeStruct((seq, d), F32), jax.ShapeDtypeStruct((TM, d), F32),
                   jax.ShapeDtypeStruct((8, d), F32), jax.ShapeDtypeStruct(land.shape, land.dtype)],
        input_output_aliases={7: 3},
        scratch_shapes=[pltpu.VMEM((nsh, d, sw), BF16), pltpu.VMEM((8, d), F32), pltpu.VMEM((2, tmb, d), F32),
                        pltpu.SemaphoreType.DMA, pltpu.SemaphoreType.DMA((2,)),
                        pltpu.SemaphoreType.DMA((3,)), pltpu.SemaphoreType.DMA((3,))],
        compiler_params=pltpu.CompilerParams(dimension_semantics=("arbitrary",),
                                             vmem_limit_bytes=VMEM_LIMIT_BIG),
    )(dproj, wg_in, x, front, ds2, norm_g, part, land)


def _col_block(width, cap):
    return max(b for b in range(128, cap + 1, 128) if width % b == 0)


def _dw_reduced(lhs_t, rhs, cw, nblk, operands, groups, out_dims, out_block, out_index, carried, name):
    na, d, tp = lhs_t.shape
    rg = d // groups
    hh = rg // 2
    nc = len(carried)

    def body(*refs):
        l_ref, r_ref = refs[:2]
        part_refs = refs[2:2 + nc]
        p32_ref, pbf_ref = refs[2 + nc:4 + nc]
        land_refs = refs[4 + nc:4 + 2 * nc]
        res, rbuf, send_sems, recv_sems = refs[4 + 2 * nc:8 + 2 * nc]
        xsems = refs[8 + 2 * nc:]
        exchanges = [_chip_exchange(part_refs[e], land_refs[e], xsems[2 * e], xsems[2 * e + 1]) for e in range(nc)]
        exchange = _Exchange([s for ex in exchanges for s in ex.sends], [r for ex in exchanges for r in ex.recvs])
        x, y, c = _mesh_pos()
        t = pl.program_id(0)
        u = jnp.maximum(t - 1, 0)

        def to_sibling(blk):
            return pltpu.make_async_remote_copy(
                src_ref=res.at[blk % 2, :, pl.ds((1 - c) * hh, hh), :], dst_ref=rbuf.at[blk % 2],
                send_sem=send_sems.at[blk], recv_sem=recv_sems.at[blk],
                device_id=(x, y, 1 - c), device_id_type=MESH)

        @pl.when(t == 0)
        def _():
            exchange.start()

        @pl.when(t < nblk)
        def _():
            res[t % 2] = jnp.dot(l_ref[...], r_ref[...], preferred_element_type=F32).reshape(groups, rg, cw)

        @pl.when(t >= 1)
        def _():
            to_sibling(u).wait_recv()
            p = res[u % 2, :, pl.ds(c * hh, hh), :] + rbuf[u % 2]
            p32_ref[...] = p.reshape(p32_ref.shape)
            pbf_ref[...] = p.reshape(pbf_ref.shape).astype(BF16)

        @pl.when(t < nblk)
        def _():
            to_sibling(t).start()

        @pl.when(t >= 1)
        def _():
            to_sibling(u).wait_send()

        @pl.when(t == nblk)
        def _():
            exchange.finish()

    any_spec = pl.BlockSpec(memory_space=pl.ANY)
    last = nblk - 1
    out_spec = pl.BlockSpec(out_block, lambda t: out_index(jnp.maximum(t - 1, 0)))
    outs = pl.pallas_call(
        body, name=name, grid=(nblk + 1,),
        in_specs=[pl.BlockSpec((None, d, tp), lambda t: (operands(jnp.minimum(t, last))[0], 0, 0)),
                  pl.BlockSpec((None, tp, cw), lambda t: (operands(jnp.minimum(t, last))[0], 0,
                                                          operands(jnp.minimum(t, last))[1]))]
        + [any_spec] * nc,
        out_specs=[out_spec, out_spec] + [any_spec] * nc,
        out_shape=[jax.ShapeDtypeStruct(out_dims, F32), jax.ShapeDtypeStruct(out_dims, BF16)]
        + [jax.ShapeDtypeStruct(e.shape, e.dtype) for e in carried],
        scratch_shapes=[pltpu.VMEM((2, groups, rg, cw), F32), pltpu.VMEM((2, groups, hh, cw), F32),
                        pltpu.SemaphoreType.DMA((nblk,)), pltpu.SemaphoreType.DMA((nblk,))]
        + [pltpu.SemaphoreType.DMA((3,)), pltpu.SemaphoreType.DMA((3,))] * nc,
        compiler_params=_params(("arbitrary",)),
    )(lhs_t, rhs, *carried)
    return outs[0], outs[1], outs[2:]


def _conv_a_taps(first_lag, last_lag):
    out = []
    for r in range(8):
        taps = [(q, 8 * q + r) for q in range(5) if first_lag <= 8 * q + r <= last_lag]
        if taps:
            out.append((r, taps))
    return out


def _tile_block(i, nt):
    return jnp.where(i == 0, nt - 1, i - 1)


def _mix_fwd(x, front, proj, target, w3, wa, wb, conv_a_b, ln_g, ln_b, b_a_out, final_g, norm_g):
    seq, d = x.shape
    tp = seq + TM
    nt = tp // TM
    nrb = TM // RB
    shl = TM + SHIFT_ROWS

    def body(x_ref, front_ref, proj_ref, tgt_ref, w3_ref, wa_ref, wb_ref, cab_ref, lng_ref, lnb_ref, bao_ref, fg_ref,
             ng_ref, ca_ref, cb_ref, ya_ref, yb_ref, abmt_ref, ds2_ref, ht_ref, loss_ref, dfg_ref,
             abm_ref, ext_a, ext_b, sh, s2_s, lacc, gacc):
        i = pl.program_id(0)

        def split(k, rows):
            return proj_ref[rows, k * d:(k + 1) * d].astype(F32)

        def s_tile():
            return jnp.where(i == 0, front_ref[...], x_ref[...])

        s_in = s_tile()
        h = s_in * lax.rsqrt(jnp.mean(s_in * s_in, axis=-1, keepdims=True) + EPS) * ng_ref[...]
        ht_ref[...] = h.astype(BF16).T

        @pl.when(i == 0)
        def _():
            ext_a[0:HALO_A, :] = jnp.zeros((HALO_A, d), F32)
            ext_b[0:HALO_B, :] = jnp.zeros((HALO_B, d), F32)
            lacc[...] = jnp.zeros_like(lacc)
            gacc[...] = jnp.zeros_like(gacc)

        def conv_in(rb, carry):
            rows = _rows(rb)
            ua0 = split(0, rows) * _sigmoid(split(1, rows))
            ext_a[pl.ds(pl.multiple_of(HALO_A + rb * RB, 8), RB), :] = ua0
            ext_b[pl.ds(pl.multiple_of(HALO_B + rb * RB, 8), RB), :] = split(4, rows) * split(5, rows)
            ca_ref[rows, :] = jnp.broadcast_to(cab_ref[...], (RB, d))
            return carry
        lax.fori_loop(0, nrb, conv_in, 0)

        for r, taps in _conv_a_taps(HALO_A - CONV_A + 1, HALO_A):
            if r == 0:
                src = ext_a
            else:
                sh[...] = ext_a[r:r + shl, :]
                src = sh

            def conv_acc(rb, carry, src=src, taps=taps):
                rows = _rows(rb)
                acc = ca_ref[rows, :]
                for q, lag in taps:
                    k = lag - (HALO_A - CONV_A + 1)
                    acc = acc + src[pl.ds(pl.multiple_of(rb * RB + 8 * q, 8), RB), :] * wa_ref[k:k + 1, :]
                ca_ref[rows, :] = acc
                return carry
            lax.fori_loop(0, nrb, conv_acc, 0)
        ext_a[0:HALO_A, :] = ext_a[TM:TM + HALO_A, :]

        cb_ref[...] = ext_b[HALO_B:HALO_B + TM, :] * wb_ref[2:3, :]
        for k in range(CONV_B - 1):
            off = HALO_B - CONV_B + 1 + k
            sh[0:TM, :] = ext_b[off:off + TM, :]
            cb_ref[...] += sh[0:TM, :] * wb_ref[k:k + 1, :]
        ext_b[0:HALO_B, :] = ext_b[TM:TM + HALO_B, :]

        def branches(rb, carry):
            rows = _rows(rb)
            ca = ca_ref[rows, :]
            mu = jnp.mean(ca, axis=-1, keepdims=True)
            xc = ca - mu
            rstd = lax.rsqrt(jnp.mean(xc * xc, axis=-1, keepdims=True) + EPS)
            ln = xc * rstd * lng_ref[...] + lnb_ref[...]
            ua = ln * _sigmoid(ln)
            a_z = split(2, rows)
            abm_ref[0, rows, :] = (ua * (a_z * _sigmoid(a_z))).astype(BF16)
            b_z = split(6, rows)
            ub = split(3, rows) * cb_ref[rows, :]
            abm_ref[1, rows, :] = (ub * (b_z * _sigmoid(b_z))).astype(BF16)
            return carry
        lax.fori_loop(0, nrb, branches, 0)

        ya_ref[...] = jnp.dot(abm_ref[0], w3_ref[0], preferred_element_type=F32) + bao_ref[...]
        yb_ref[...] = jnp.dot(abm_ref[1], w3_ref[1], preferred_element_type=F32)

        def merge(rb, carry):
            rows = _rows(rb)
            m = _sigmoid(split(7, rows)) * ya_ref[rows, :] + _sigmoid(split(8, rows)) * yb_ref[rows, :]
            abm_ref[2, rows, :] = m.astype(BF16)
            return carry
        lax.fori_loop(0, nrb, merge, 0)

        s2_s[...] = s_tile() + jnp.dot(abm_ref[2], w3_ref[2], preferred_element_type=F32)
        for k in range(3):
            abmt_ref[k] = abm_ref[k].T
        live = (i > 0).astype(F32)

        def head(rb, carry):
            rows = _rows(rb)
            s2 = s2_s[rows, :]
            r2 = lax.rsqrt(jnp.mean(s2 * s2, axis=-1, keepdims=True) + EPS)
            diff = (s2 * r2 * fg_ref[...] - tgt_ref[rows, :]) * live
            lacc[...] += diff * diff
            dy = diff * (1.0 / d)
            gacc[...] += (dy * s2 * r2).reshape(RB // 8, 8, d).sum(axis=0)
            t = dy * fg_ref[...]
            ds2_ref[rows, :] = r2 * t - s2 * (r2 * r2 * r2) * jnp.mean(t * s2, axis=-1, keepdims=True)
            return carry
        lax.fori_loop(0, nrb, head, 0)

        @pl.when(i == nt - 1)
        def _():
            loss_ref[...] = jnp.broadcast_to(0.5 * jnp.sum(lacc[...]) * (1.0 / d), (8, 128))
            dfg_ref[...] = jnp.broadcast_to(jnp.sum(gacc[...], axis=0, keepdims=True), (8, d))

    row_f32 = pl.BlockSpec((TM, d), lambda i: (_tile_block(i, nt), 0))
    x_rows = pl.BlockSpec((TM, d), lambda i: (jnp.maximum(i - 1, 0), 0))
    const = lambda shape: pl.BlockSpec(shape, lambda i: (0,) * len(shape))
    return pl.pallas_call(
        body, name="f2_mix", grid=(nt,),
        in_specs=[x_rows, const((TM, d)),
                  pl.BlockSpec((TM, N_SPLIT * d), lambda i: (_tile_block(i, nt), 0)),
                  x_rows,
                  const((3, d, d)), const(wa.shape), const(wb.shape)] + [const((1, d))] * 6,
        out_specs=[row_f32, row_f32, row_f32, row_f32,
                   pl.BlockSpec((3, d, TM), lambda i: (0, 0, _tile_block(i, nt))),
                   row_f32, pl.BlockSpec((d, TM), lambda i: (0, _tile_block(i, nt))),
                   const((8, 128)), const((8, d))],
        out_shape=[jax.ShapeDtypeStruct((tp, d), F32)] * 4
        + [jax.ShapeDtypeStruct((3, d, tp), BF16), jax.ShapeDtypeStruct((tp, d), F32),
           jax.ShapeDtypeStruct((d, tp), BF16),
           jax.ShapeDtypeStruct((8, 128), F32), jax.ShapeDtypeStruct((8, d), F32)],
        scratch_shapes=[pltpu.VMEM((3, TM, d), BF16),
                        pltpu.VMEM((HALO_A + TM, d), F32), pltpu.VMEM((HALO_B + TM, d), F32),
                        pltpu.VMEM((shl, d), F32), pltpu.VMEM((TM, d), F32),
                        pltpu.VMEM((RB, d), F32), pltpu.VMEM((8, d), F32)],
        compiler_params=_params(("arbitrary",)),
    )(x, front, proj, target, w3, wa, wb, conv_a_b, ln_g, ln_b, b_a_out, final_g, norm_g)


def _mix_bwd(ds2, proj, ca, cb, ya, yb, w3, wa, wb, ln_g, ln_b):
    tp, d = ds2.shape
    nt = tp // TM
    nrb = TM // RB
    shl = TM + SHIFT_ROWS
    nt_dims = (((1,), (1,)), ((), ()))

    def body(ds2_ref, proj_ref, ca_ref, cb_ref, ya_ref, yb_ref, w3_ref, wa_ref, wb_ref, lng_ref, lnb_ref,
             dproj_ref, d3_ref, sm_ref, ext_d, ext_e, sh, dm_s, dpa_s, dpb_s, dua0_s, acc):
        step = pl.program_id(0)

        def split(k, rows):
            return proj_ref[rows, k * d:(k + 1) * d].astype(F32)

        def put(k, rows, val):
            dproj_ref[rows, k * d:(k + 1) * d] = val.astype(BF16)

        def accum(row, val):
            acc[row] += val.reshape(RB // 8, 8, d).sum(axis=0)

        @pl.when(step == 0)
        def _():
            ext_d[TM:TM + HALO_A, :] = jnp.zeros((HALO_A, d), F32)
            ext_e[TM:TM + HALO_B, :] = jnp.zeros((HALO_B, d), F32)
            acc[...] = jnp.zeros_like(acc)

        d3_ref[2] = ds2_ref[...].astype(BF16)
        dm_s[...] = lax.dot_general(d3_ref[2], w3_ref[2], nt_dims, preferred_element_type=F32)

        def gates(rb, carry):
            rows = _rows(rb)
            dm = dm_s[rows, :]
            sa = _sigmoid(split(7, rows))
            sb = _sigmoid(split(8, rows))
            ya_v = ya_ref[rows, :]
            yb_v = yb_ref[rows, :]
            put(7, rows, dm * ya_v * sa * (1.0 - sa))
            put(8, rows, dm * yb_v * sb * (1.0 - sb))
            dya = dm * sa
            accum(ROW_DBAO, dya)
            d3_ref[0, rows, :] = dya.astype(BF16)
            d3_ref[1, rows, :] = (dm * sb).astype(BF16)
            return carry
        lax.fori_loop(0, nrb, gates, 0)

        dpa_s[...] = lax.dot_general(d3_ref[0], w3_ref[0], nt_dims, preferred_element_type=F32)
        dpb_s[...] = lax.dot_general(d3_ref[1], w3_ref[1], nt_dims, preferred_element_type=F32)

        def branches(rb, carry):
            rows = _rows(rb)
            ca_v = ca_ref[rows, :]
            mu = jnp.mean(ca_v, axis=-1, keepdims=True)
            xc = ca_v - mu
            rstd = lax.rsqrt(jnp.mean(xc * xc, axis=-1, keepdims=True) + EPS)
            xhat = xc * rstd
            ln = xhat * lng_ref[...] + lnb_ref[...]
            sl = _sigmoid(ln)
            ua = ln * sl
            a_z = split(2, rows)
            sz = _sigmoid(a_z)
            dpa = dpa_s[rows, :]
            put(2, rows, dpa * ua * (sz * (1.0 + a_z * (1.0 - sz))))
            dln = dpa * (a_z * sz) * (sl * (1.0 + ln * (1.0 - sl)))
            accum(ROW_DLNG, dln * xhat)
            accum(ROW_DLNB, dln)
            dxh = dln * lng_ref[...]
            dca = rstd * (dxh - jnp.mean(dxh, axis=-1, keepdims=True)
                          - xhat * jnp.mean(dxh * xhat, axis=-1, keepdims=True))
            accum(ROW_DCAB, dca)
            ext_d[rows, :] = dca
            dua0_s[rows, :] = jnp.zeros((RB, d), F32)
            dm_s[rows, :] = split(0, rows) * _sigmoid(split(1, rows))
            b_z = split(6, rows)
            szb = _sigmoid(b_z)
            dpb = dpb_s[rows, :]
            b_b = split(3, rows)
            cb_v = cb_ref[rows, :]
            put(6, rows, dpb * (b_b * cb_v) * (szb * (1.0 + b_z * (1.0 - szb))))
            dub = dpb * (b_z * szb)
            put(3, rows, dub * cb_v)
            ext_e[rows, :] = dub * b_b
            return carry
        lax.fori_loop(0, nrb, branches, 0)

        for r, taps in _conv_a_taps(0, CONV_A - 1):
            if r == 0:
                src = ext_d
            else:
                sh[...] = ext_d[r:r + shl, :]
                src = sh

            def conv_t(rb, carry, src=src, taps=taps):
                rows = _rows(rb)
                ua0 = dm_s[rows, :]
                dua0 = dua0_s[rows, :]
                for q, lag in taps:
                    k = CONV_A - 1 - lag
                    slab = src[pl.ds(pl.multiple_of(rb * RB + 8 * q, 8), RB), :]
                    dua0 = dua0 + slab * wa_ref[k:k + 1, :]
                    accum(ROW_DWA + k, slab * ua0)
                dua0_s[rows, :] = dua0
                return carry
            lax.fori_loop(0, nrb, conv_t, 0)
        ext_d[TM:TM + HALO_A, :] = ext_d[0:HALO_A, :]

        dpb_s[...] = ext_e[0:TM, :] * wb_ref[CONV_B - 1:CONV_B, :]
        for lag in range(CONV_B):
            k = CONV_B - 1 - lag
            if lag > 0:
                sh[0:TM, :] = ext_e[lag:lag + TM, :]
                dpb_s[...] += sh[0:TM, :] * wb_ref[k:k + 1, :]
            src = ext_e if lag == 0 else sh

            def conv_b_w(rb, carry, src=src, k=k):
                rows = _rows(rb)
                accum(ROW_DWB + k, src[rows, :] * (split(4, rows) * split(5, rows)))
                return carry
            lax.fori_loop(0, nrb, conv_b_w, 0)
        ext_e[TM:TM + HALO_B, :] = ext_e[0:HALO_B, :]

        def inputs(rb, carry):
            rows = _rows(rb)
            dua0 = dua0_s[rows, :]
            a_val = split(0, rows)
            sg = _sigmoid(split(1, rows))
            put(0, rows, dua0 * sg)
            put(1, rows, dua0 * a_val * sg * (1.0 - sg))
            dcbin = dpb_s[rows, :]
            put(4, rows, dcbin * split(5, rows))
            put(5, rows, dcbin * split(4, rows))
            return carry
        lax.fori_loop(0, nrb, inputs, 0)

        @pl.when(step == nt - 1)
        def _():
            for row in range(SM_ROWS):
                sm_ref[row:row + 1, :] = jnp.sum(acc[row], axis=0, keepdims=True)

    rev = lambda i: (_tile_block(nt - 1 - i, nt), 0)
    row_f32 = pl.BlockSpec((TM, d), rev)
    const = lambda shape: pl.BlockSpec(shape, lambda i: (0,) * len(shape))
    return pl.pallas_call(
        body, name="b1_mix", grid=(nt,),
        in_specs=[row_f32, pl.BlockSpec((TM, N_SPLIT * d), rev), row_f32, row_f32, row_f32, row_f32,
                  const((3, d, d)), const(wa.shape), const(wb.shape), const((1, d)), const((1, d))],
        out_specs=[pl.BlockSpec((TM, N_SPLIT * d), rev),
                   pl.BlockSpec((3, TM, d), lambda i: (0, _tile_block(nt - 1 - i, nt), 0)),
                   const((SM_ROWS, d))],
        out_shape=[jax.ShapeDtypeStruct((tp, N_SPLIT * d), BF16), jax.ShapeDtypeStruct((3, tp, d), BF16),
                   jax.ShapeDtypeStruct((SM_ROWS, d), F32)],
        scratch_shapes=[pltpu.VMEM((TM + HALO_A, d), F32), pltpu.VMEM((TM + HALO_B, d), F32),
                        pltpu.VMEM((shl, d), F32), pltpu.VMEM((TM, d), F32), pltpu.VMEM((TM, d), F32),
                        pltpu.VMEM((TM, d), F32), pltpu.VMEM((TM, d), F32),
                        pltpu.VMEM((SM_ROWS, 8, d), F32)],
        compiler_params=_params(("arbitrary",)),
    )(ds2, proj, ca, cb, ya, yb, w3, wa, wb, ln_g, ln_b)


def kernel(x, meta_tokens, norm_g, w_in, conv_a_w, conv_a_b, ln_a_g, ln_a_b, w_a_out, b_a_out, conv_b_w, w_b_out, w_out, final_g, loss_target, m_meta_tokens, m_norm_g, m_w_in, m_conv_a_w, m_conv_a_b, m_ln_a_g, m_ln_a_b, m_w_a_out, m_b_a_out, m_conv_b_w, m_w_b_out, m_w_out, m_final_g, v_meta_tokens, v_norm_g, v_w_in, v_conv_a_w, v_conv_a_b, v_ln_a_g, v_ln_a_b, v_w_a_out, v_b_a_out, v_conv_b_w, v_w_b_out, v_w_out, v_final_g):
    seq, d = x.shape[1], x.shape[2]
    dc = meta_tokens.shape[1]
    sw = w_in.shape[2]
    rsh = w_a_out.shape[1]
    xi, yi, ci = _mesh_pos()
    me = 2 * xi + yi
    pos = jnp.stack([ci, me]).astype(jnp.int32)

    conv_rows = HALO_A + HALO_B + 8
    convs = jnp.concatenate([
        jnp.pad(conv_a_w[0], ((0, HALO_A - CONV_A), (0, 0))),
        jnp.pad(conv_b_w[0], ((0, HALO_B - CONV_B), (0, 0))), jnp.zeros((8, dc), F32)], axis=0)[None]
    w3_own = jnp.stack([w_a_out[0], w_b_out[0], w_out[0]])
    fg2 = final_g.reshape(1, d)
    xs = x[0]

    h, front = _h_prep(xs, meta_tokens, norm_g)
    proj, (wg_in, wg3, convg) = _proj_fwd2(h, [_place_own(w_in, pos, BF16, "place_in"),
                                               _place_own(w3_own, pos, BF16, "place_sq"),
                                               _place_own(convs, pos, F32, "place_conv")], pos)
    w3 = wg3.reshape(3, N_CHIPS * rsh, d)
    convg = jnp.transpose(convg[0], (1, 0, 2)).reshape(conv_rows, N_CHIPS * dc)
    wa_full = convg[0:HALO_A]
    wb_full = convg[HALO_A:HALO_A + HALO_B]
    ca, cb, ya, yb, abm_t, ds2, h_t, loss8, dfg8 = _mix_fwd(
        xs, front, proj, loss_target[0], w3, wa_full, wb_full, conv_a_b, ln_a_g, ln_a_b, b_a_out, fg2, norm_g)
    dproj, d3, sm = _mix_bwd(ds2, proj, ca, cb, ya, yb, w3, wa_full, wb_full, ln_a_g, ln_a_b)
    cw_sq = _col_block(d, 512)
    per_sq = d // cw_sq
    p32_sq, pbf_sq, _ = _dw_reduced(
        abm_t, d3, cw_sq, 3 * per_sq, lambda t: (t // per_sq, t % per_sq), N_CHIPS,
        (3, N_CHIPS, rsh // 2, d), (None, N_CHIPS, rsh // 2, cw_sq),
        lambda u: (u // per_sq, 0, 0, u % per_sq), [], "dw_square")
    cw_in = _col_block(sw, 768)
    ncol = sw // cw_in
    p32_in, pbf_in, (l_sq,) = _dw_reduced(
        h_t[None], dproj[None], cw_in, N_CHIPS * ncol, lambda t: (0, t), 1,
        (1, N_CHIPS, d // 2, sw), (None, None, d // 2, cw_in), lambda u: (0, u // ncol, 0, u % ncol),
        [pbf_sq], "dw_in")
    grad_x, dfront, dng8, l_in = _dh_bwd(dproj, wg_in, xs, front, ds2, norm_g, pbf_in,
                                         lax.empty(pbf_in.shape, BF16), None)
    half_in = _sum_chips([(p32_in, l_in)], sw, pos, "rs_sum_in")
    half_sq = _sum_chips([(p32_sq, l_sq)], d, pos, "rs_sum_sq")
    tail_row = lax.broadcasted_iota(jnp.int32, (8, d), 0)
    tail = jnp.where(tail_row == 0, dng8, jnp.where(tail_row == 1, dfg8,
                     jnp.where(tail_row == 2, loss8[0, 0], 0.0)))
    block = jnp.concatenate([sm, dfront[TM - N_META:TM], tail], axis=0)
    (other_in, other_sq), red = _sibling_swap([half_in, half_sq], block)
    col = lax.dynamic_slice(red, (0, me * dc), (AR_ROWS, dc))
    g_small = {
        "meta_tokens": col[ROW_DMETA:ROW_DMETA + N_META],
        "norm_g": red[ROW_DNG:ROW_DNG + 1],
        "conv_a_w": col[ROW_DWA:ROW_DWA + CONV_A][None],
        "conv_a_b": red[ROW_DCAB:ROW_DCAB + 1],
        "ln_a_g": red[ROW_DLNG:ROW_DLNG + 1],
        "ln_a_b": red[ROW_DLNB:ROW_DLNB + 1],
        "b_a_out": red[ROW_DBAO:ROW_DBAO + 1],
        "conv_b_w": col[ROW_DWB:ROW_DWB + CONV_B][None],
        "final_g": red[ROW_DFG],
    }

    upd_in = _adam_halves([w_in], [m_w_in], [v_w_in], half_in, other_in, pos, "adam_in")
    upd_sq = _adam_halves([w_a_out, w_b_out, w_out], [m_w_a_out, m_w_b_out, m_w_out],
                          [v_w_a_out, v_w_b_out, v_w_out], half_sq, other_sq, pos, "adam_sq")
    small_w = {"meta_tokens": (meta_tokens, m_meta_tokens, v_meta_tokens), "norm_g": (norm_g, m_norm_g, v_norm_g),
               "conv_a_w": (conv_a_w, m_conv_a_w, v_conv_a_w), "conv_a_b": (conv_a_b, m_conv_a_b, v_conv_a_b),
               "ln_a_g": (ln_a_g, m_ln_a_g, v_ln_a_g), "ln_a_b": (ln_a_b, m_ln_a_b, v_ln_a_b),
               "b_a_out": (b_a_out, m_b_a_out, v_b_a_out), "conv_b_w": (conv_b_w, m_conv_b_w, v_conv_b_w),
               "final_g": (final_g, m_final_g, v_final_g)}
    names_small = list(small_w)
    as2d = lambda t: t.reshape(-1, t.shape[-1])
    upd_small = _adam_small([(as2d(small_w[k][0]), as2d(g_small[k]), as2d(small_w[k][1]), as2d(small_w[k][2]))
                             for k in names_small])

    grads, deltas, new_m, new_v = dict(g_small), {}, {}, {}
    for k, upd in zip(names_small, upd_small):
        deltas[k], new_m[k], new_v[k] = [t.reshape(small_w[k][0].shape) for t in upd]
    grads["w_in"], deltas["w_in"], new_m["w_in"], new_v["w_in"] = upd_in
    for idx, k in enumerate(["w_a_out", "w_b_out", "w_out"]):
        grads[k], deltas[k], new_m[k], new_v[k] = upd_sq[4 * idx:4 * idx + 4]

    loss = red[ROW_LOSS, 0]
    order = ["meta_tokens", "norm_g", "w_in", "conv_a_w", "conv_a_b", "ln_a_g", "ln_a_b", "w_a_out", "b_a_out",
             "conv_b_w", "w_b_out", "w_out", "final_g"]
    return (loss, grad_x[None], *[grads[k] for k in order], *[deltas[k] for k in order],
            *[new_m[k] for k in order], *[new_v[k] for k in order])
```

```python
import jax
import jax.numpy as jnp
from jax import lax
from jax.experimental import pallas as pl
from jax.experimental.pallas import tpu as pltpu

F32 = jnp.float32
BF16 = jnp.bfloat16
MESH = pl.DeviceIdType.MESH

EPS = 1e-6
N_META = 16
N_SPLIT = 9
CONV_A = 31
CONV_B = 3
HALO_A = 32
HALO_B = 8
SHIFT_ROWS = 24
TM = 256
RB = 64
N_ROW_TILES_BIG = 8
ROW_BLOCK = 256
N_CHIPS = 4
VMEM_LIMIT = 56 * 1024 * 1024
VMEM_LIMIT_BIG = 62 * 1024 * 1024

ADAM_LR = 0.001
ADAM_B1 = 0.9
ADAM_B2 = 0.999
ADAM_EPS = 1e-08
ADAM_WD = 0.01
ADAM_STEP = 10

ROW_DWA = 0
ROW_DWB = 32
ROW_DCAB = 40
ROW_DLNG = 41
ROW_DLNB = 42
ROW_DBAO = 43
SM_ROWS = 48
ROW_DMETA = 48
ROW_DNG = 64
ROW_DFG = 65
ROW_LOSS = 66
AR_ROWS = 72


def _sigmoid(v):
    return 0.5 * jnp.tanh(0.5 * v) + 0.5


def _params(sem, **kw):
    return pltpu.CompilerParams(dimension_semantics=sem, vmem_limit_bytes=VMEM_LIMIT, **kw)


def _rows(rb):
    return pl.ds(pl.multiple_of(rb * RB, RB), RB)


def _mesh_pos():
    x, y, c = lax.axis_index("x"), lax.axis_index("y"), lax.axis_index("c")
    return x, y, c


def _half(ref, j, c):
    h = ref.shape[2] // 2
    return ref.at[:, j, pl.ds(c * h, h), :]


def _place_own(shard, pos, dtype, name):
    s, r, c = shard.shape
    rb = ROW_BLOCK if r % ROW_BLOCK == 0 else r

    def body(pos_ref, x_ref, o_ref):
        o_ref[...] = x_ref[...].astype(dtype)

    return pl.pallas_call(
        body, name=name,
        grid_spec=pltpu.PrefetchScalarGridSpec(
            num_scalar_prefetch=1, grid=(s, r // rb),
            in_specs=[pl.BlockSpec((None, rb, c), lambda si, b, pos_ref: (si, b, 0))],
            out_specs=pl.BlockSpec((None, None, rb, c), lambda si, b, pos_ref: (si, pos_ref[1], b, 0))),
        out_shape=jax.ShapeDtypeStruct((s, N_CHIPS, r, c), dtype),
        compiler_params=_params(("arbitrary",) * 2),
    )(pos, shard)


class _Exchange:
    def __init__(self, sends, recvs):
        self.sends, self.recvs = sends, recvs

    def start(self):
        for cp in self.sends:
            cp.start()

    def finish(self):
        for cp in self.recvs:
            cp.wait_recv()
        for cp in self.sends:
            cp.wait_send()


def _chip_exchange(part_ref, land_ref, send_sems, recv_sems):
    x, y, c = _mesh_pos()
    me = 2 * x + y
    sends, recvs = [], []
    for k, (px, py) in enumerate([(1 - x, y), (x, 1 - y), (1 - x, 1 - y)]):
        sems = dict(send_sem=send_sems.at[k], recv_sem=recv_sems.at[k], device_id=(px, py, c), device_id_type=MESH)
        sends.append(pltpu.make_async_remote_copy(
            src_ref=part_ref.at[:, 2 * px + py], dst_ref=land_ref.at[:, me], **sems))
        landed = land_ref.at[:, 2 * px + py]
        recvs.append(pltpu.make_async_remote_copy(src_ref=landed, dst_ref=landed, **sems))
    return _Exchange(sends, recvs)

def _sibling_swap(halves, small):
    n = len(halves)

    def body(*refs):
        ins, small_ref, outs, red_ref = refs[:n], refs[n], refs[n + 1:2 * n + 1], refs[2 * n + 1]
        send_sems, recv_sems = refs[2 * n + 2:2 * n + 4]
        reduce = _SmallAllReduce(small_ref, red_ref, *refs[2 * n + 4:])
        x, y, c = _mesh_pos()
        copies = [pltpu.make_async_remote_copy(
            src_ref=ins[a], dst_ref=outs[a], send_sem=send_sems.at[a], recv_sem=recv_sems.at[a],
            device_id=(x, y, 1 - c), device_id_type=MESH) for a in range(n)]
        reduce.start()
        for cp in copies:
            cp.start()
        reduce.between_chips()
        reduce.finish()
        for cp in copies:
            cp.wait()

    any_spec = pl.BlockSpec(memory_space=pl.ANY)
    vm = pl.BlockSpec(memory_space=pltpu.VMEM)
    outs = pl.pallas_call(
        body, name="rs_swap",
        in_specs=[any_spec] * n + [vm], out_specs=[any_spec] * n + [vm],
        out_shape=[jax.ShapeDtypeStruct(h.shape, h.dtype) for h in halves]
        + [jax.ShapeDtypeStruct(small.shape, F32)],
        scratch_shapes=[pltpu.SemaphoreType.DMA((n,)), pltpu.SemaphoreType.DMA((n,))]
        + _SmallAllReduce.scratch(*small.shape),
    )(*halves, small)
    return outs[:n], outs[n]


class _SmallAllReduce:
    def __init__(self, x_ref, out_ref, sib_ref, part_ref, peers_ref, send_sems, recv_sems):
        self.x_ref, self.out_ref, self.sib_ref, self.part_ref, self.peers_ref = x_ref, out_ref, sib_ref, part_ref, peers_ref
        x, y, c = _mesh_pos()
        self.me = 2 * x + y
        self.swap = pltpu.make_async_remote_copy(
            src_ref=x_ref, dst_ref=sib_ref, send_sem=send_sems.at[0], recv_sem=recv_sems.at[0],
            device_id=(x, y, 1 - c), device_id_type=MESH)
        self.sends, self.recvs = [], []
        for k, (px, py) in enumerate([(1 - x, y), (x, 1 - y), (1 - x, 1 - y)]):
            sems = dict(send_sem=send_sems.at[1 + k], recv_sem=recv_sems.at[1 + k],
                        device_id=(px, py, c), device_id_type=MESH)
            self.sends.append(pltpu.make_async_remote_copy(src_ref=part_ref, dst_ref=peers_ref.at[self.me], **sems))
            landed = peers_ref.at[2 * px + py]
            self.recvs.append(pltpu.make_async_remote_copy(src_ref=landed, dst_ref=landed, **sems))

    @staticmethod
    def scratch(rows, d):
        return [pltpu.VMEM((rows, d), F32), pltpu.VMEM((rows, d), F32), pltpu.VMEM((N_CHIPS, rows, d), F32),
                pltpu.SemaphoreType.DMA((4,)), pltpu.SemaphoreType.DMA((4,))]

    def start(self):
        self.swap.start()

    def between_chips(self):
        self.swap.wait()
        self.part_ref[...] = self.x_ref[...] + self.sib_ref[...]
        self.peers_ref[self.me] = self.part_ref[...]
        for cp in self.sends:
            cp.start()

    def finish(self):
        for cp in self.recvs:
            cp.wait_recv()
        for cp in self.sends:
            cp.wait_send()
        p = self.peers_ref
        self.out_ref[...] = ((p[0] + p[1]) + p[2]) + p[3]


def _sum_chips(parts, cw, pos, name):
    s, _, h, _ = parts[0][0].shape
    hb = min(h, ROW_BLOCK)
    widths = [own.shape[3] // cw for own, _ in parts]
    starts = [sum(widths[:a]) for a in range(len(parts))]

    def body(pos_ref, *refs):
        out_ref = refs[-1]
        n = pl.program_id(2)
        total = None
        for a in range(len(parts)):
            own, l1, l2, l3 = refs[4 * a:4 * a + 4]
            val = ((own[...] + l1[...].astype(F32)) + l2[...].astype(F32)) + l3[...].astype(F32)
            total = val if total is None else jnp.where(n >= starts[a], val, total)
        out_ref[...] = total

    def slot(a, k):
        col = lambda n: jnp.clip(n - starts[a], 0, widths[a] - 1)
        return pl.BlockSpec((None, None, hb, cw),
                            lambda si, b, n, pos_ref: (si, (pos_ref[1] + k) % N_CHIPS, b, col(n)))

    operands, specs = [], []
    for a, (own, landed) in enumerate(parts):
        operands += [own, landed, landed, landed]
        specs += [slot(a, 0), slot(a, 1), slot(a, 2), slot(a, 3)]
    return pl.pallas_call(
        body, name=name,
        grid_spec=pltpu.PrefetchScalarGridSpec(
            num_scalar_prefetch=1, grid=(s, h // hb, sum(widths)), in_specs=specs,
            out_specs=pl.BlockSpec((None, hb, cw), lambda si, b, n, pos_ref: (si, b, n))),
        out_shape=jax.ShapeDtypeStruct((s, h, sum(widths) * cw), F32),
        compiler_params=_params(("arbitrary",) * 3),
    )(pos, *operands)


def _adamw(w, g, m, v):
    m = ADAM_B1 * m + (1.0 - ADAM_B1) * g
    v = ADAM_B2 * v + (1.0 - ADAM_B2) * (g * g)
    m_hat = m / (1.0 - ADAM_B1 ** ADAM_STEP)
    v_hat = v / (1.0 - ADAM_B2 ** ADAM_STEP)
    delta = -ADAM_LR * (m_hat / (jnp.sqrt(v_hat) + ADAM_EPS) + ADAM_WD * w)
    return delta, m, v


def _adam_halves(ws, ms, vs, g_own, g_recv, pos, name):
    n = len(ws)
    _, r, c = ws[0].shape
    h = r // 2
    rb = min(h, ROW_BLOCK)
    nb = h // rb

    def body(pos_ref, *refs):
        w_refs, m_refs, v_refs = refs[:n], refs[n:2 * n], refs[2 * n:3 * n]
        go_ref, gr_ref = refs[3 * n:3 * n + 2]
        outs = refs[3 * n + 2:]
        mine = pl.program_id(0) == pos_ref[0]
        for a in range(n):
            g = jnp.where(mine, go_ref[a], gr_ref[a])
            delta, m, v = _adamw(w_refs[a][...], g, m_refs[a][...], v_refs[a][...])
            outs[4 * a][...], outs[4 * a + 1][...], outs[4 * a + 2][...], outs[4 * a + 3][...] = g, delta, m, v

    spec_w = pl.BlockSpec((None, rb, c), lambda hf, b, pos_ref: (0, hf * nb + b, 0))
    spec_g = pl.BlockSpec((n, rb, c), lambda hf, b, pos_ref: (0, b, 0))
    return pl.pallas_call(
        body, name=name,
        grid_spec=pltpu.PrefetchScalarGridSpec(
            num_scalar_prefetch=1, grid=(2, nb), in_specs=[spec_w] * (3 * n) + [spec_g] * 2,
            out_specs=[spec_w] * (4 * n)),
        out_shape=[jax.ShapeDtypeStruct((1, r, c), F32)] * (4 * n),
        compiler_params=_params(("arbitrary",) * 2),
    )(pos, *ws, *ms, *vs, g_own, g_recv)


def _adam_small(items):
    n = len(items)

    def body(*refs):
        ins, outs = refs[:4 * n], refs[4 * n:]
        for a in range(n):
            w_ref, g_ref, m_ref, v_ref = ins[4 * a:4 * a + 4]
            d, m, v = _adamw(w_ref[...], g_ref[...], m_ref[...], v_ref[...])
            outs[3 * a][...] = d
            outs[3 * a + 1][...] = m
            outs[3 * a + 2][...] = v

    vm = pl.BlockSpec(memory_space=pltpu.VMEM)
    flat = [t for it in items for t in it]
    outs = pl.pallas_call(
        body, name="adam_small", in_specs=[vm] * (4 * n), out_specs=[vm] * (3 * n),
        out_shape=[jax.ShapeDtypeStruct(it[0].shape, F32) for it in items for _ in range(3)],
    )(*flat)
    return [tuple(outs[3 * a:3 * a + 3]) for a in range(n)]


def _big_row_spec(seq, tmb, d, tile_of):
    return pl.BlockSpec((pl.Element(tmb), pl.Element(d)),
                        lambda *args: (pl.multiple_of(jnp.minimum(tile_of(*args) * tmb, seq - tmb), 8), 0))


def _big_row_tile(x_ref, front_ref, i):
    rows = x_ref[...]
    last = jnp.concatenate([rows[TM:], front_ref[...]], axis=0)
    return jnp.where(i == N_ROW_TILES_BIG - 1, last, rows)


def _h_prep(x, meta, norm_g):
    seq, d = x.shape
    tp = seq + TM
    dc = meta.shape[1]
    tmb = tp // N_ROW_TILES_BIG
    assert tmb >= TM and tp == tmb * N_ROW_TILES_BIG
    last = N_ROW_TILES_BIG - 1

    def body(x_ref, meta_ref, g_ref, h_ref, front_ref, metas, msend, mrecv):
        x, y, c = _mesh_pos()
        me = 2 * x + y
        chips = [(1 - x, y), (x, 1 - y), (1 - x, 1 - y)]
        i = pl.program_id(0)

        def meta_copy(k, chip):
            return pltpu.make_async_remote_copy(
                src_ref=metas.at[chip], dst_ref=metas.at[chip], send_sem=msend.at[k], recv_sem=mrecv.at[k],
                device_id=(*chips[k], c), device_id_type=MESH)

        @pl.when(i == 0)
        def _():
            metas[me] = meta_ref[...]
            for k in range(3):
                meta_copy(k, me).start()
            front_ref[...] = jnp.zeros_like(front_ref)

        @pl.when(i == last)
        def _():
            for k, (px, py) in enumerate(chips):
                meta_copy(k, 2 * px + py).wait_recv()
            for q in range(N_CHIPS):
                front_ref[TM - N_META:TM, q * dc:(q + 1) * dc] = metas[q]

        s = _big_row_tile(x_ref, front_ref, i)
        r = lax.rsqrt(jnp.mean(s * s, axis=-1, keepdims=True) + EPS)
        h_ref[...] = (s * r * g_ref[...]).astype(BF16)

        @pl.when(i == last)
        def _():
            for k in range(3):
                meta_copy(k, me).wait_send()

    return pl.pallas_call(
        body, name="f0_norm", grid=(N_ROW_TILES_BIG,),
        in_specs=[_big_row_spec(seq, tmb, d, lambda i: i), pl.BlockSpec(meta.shape, lambda i: (0, 0)),
                  pl.BlockSpec((1, d), lambda i: (0, 0))],
        out_specs=[pl.BlockSpec((tmb, d), lambda i: (i, 0)), pl.BlockSpec((TM, d), lambda i: (0, 0))],
        out_shape=[jax.ShapeDtypeStruct((tp, d), BF16), jax.ShapeDtypeStruct((TM, d), F32)],
        scratch_shapes=[pltpu.VMEM((N_CHIPS,) + meta.shape, F32),
                        pltpu.SemaphoreType.DMA((3,)), pltpu.SemaphoreType.DMA((3,))],
        compiler_params=_params(("arbitrary",)),
    )(x, meta, norm_g)


N_UNITS = 3
N_STEPS_PROJ = N_CHIPS * N_UNITS


def _proj_plan(u):
    v = u - N_UNITS
    if v < 2 * N_UNITS:
        return v % 2, v // 2
    return 2, v - 2 * N_UNITS


def _proj_unit(t, me):
    v = t - N_UNITS
    near = v < 2 * N_UNITS
    rel = jnp.where(near, v % 2, 2)
    unit = jnp.where(near, v // 2, v - 2 * N_UNITS)
    flip = jnp.where(rel == 0, 2, jnp.where(rel == 1, 1, 3))
    own = t < N_UNITS
    return jnp.where(own, me, lax.bitwise_xor(me, flip)), jnp.where(own, t, unit)


def _proj_fwd(h, bufs, pos):
    tp, d = h.shape
    _, nsh, _, sw = bufs[0].shape
    cu = sw // N_UNITS
    assert cu % 128 == 0
    n = len(bufs)
    w_sems = 6 * N_UNITS
    last = N_STEPS_PROJ - 1
    late = N_STEPS_PROJ - N_UNITS

    def body(pos_ref, h_ref, *refs):
        proj_ref = refs[n]
        gbufs = refs[n + 1:2 * n + 1]
        wbuf, wsems, send_sems, recv_sems = refs[2 * n + 1:]
        x, y, c = _mesh_pos()
        me = 2 * x + y
        sibling = (x, y, 1 - c)
        chips = [(1 - x, y), (x, 1 - y), (1 - x, 1 - y)]
        chip_ids = [2 * px + py for px, py in chips]
        relayed_chip = jnp.where(c == 0, chip_ids[0], chip_ids[1])
        relay_to = (jnp.where(c == 0, x, 1 - x), jnp.where(c == 0, 1 - y, y), c)
        t = pl.program_id(0)

        def remote(idx, piece, to):
            return pltpu.make_async_remote_copy(
                src_ref=piece, dst_ref=piece, send_sem=send_sems.at[idx], recv_sem=recv_sems.at[idx],
                device_id=to, device_id_type=MESH)

        hr = d // 2

        def chunk_of(chip, half, k):
            return gbufs[0].at[0, chip, pl.ds(half * hr, hr), pl.ds(k * cu, cu)]

        def own_chunk(r, k):
            return remote(6 * k + r, chunk_of(me, c, k), (*chips[r], c))

        def landed_chunk(r, k):
            return remote(6 * k + r, chunk_of(chip_ids[r], c, k), (*chips[r], c))

        def relay_chunk(k):
            return remote(6 * k + 2, chunk_of(relayed_chip, c, k), relay_to)

        def sibling_chunk(r, k, half):
            return remote(6 * k + 3 + r, chunk_of(chip_ids[r], half, k), sibling)

        def fetch(u):
            chip, unit = _proj_unit(jnp.int32(u), me)
            return pltpu.make_async_copy(gbufs[0].at[0, chip, :, pl.ds(pl.multiple_of(unit * cu, 128), cu)],
                                         wbuf.at[u % 2], wsems.at[u % 2])

        def make_available(u):
            r, k = _proj_plan(u)
            landed_chunk(r, k).wait_recv()
            if r < 2:
                pl.when(c == r)(lambda: relay_chunk(k).start())
            sibling_chunk(r, k, c).start()
            sibling_chunk(r, k, 1 - c).wait_recv()

        def own_piece(a, r):
            return remote(w_sems + 6 * (a - 1) + r, _half(gbufs[a], me, c), (*chips[r], c))

        def relay(a):
            return remote(w_sems + 6 * (a - 1) + 2, _half(gbufs[a], relayed_chip, c), relay_to)

        def to_sibling(a, r, core):
            return remote(w_sems + 6 * (a - 1) + 3 + r, _half(gbufs[a], chip_ids[r], core), sibling)

        def landed(a, r):
            return remote(w_sems + 6 * (a - 1) + r, _half(gbufs[a], chip_ids[r], c), (*chips[r], c))

        for u in range(N_STEPS_PROJ):
            @pl.when(t == u)
            def _(u=u):
                if u == 0:
                    for k in range(N_UNITS):
                        for r in range(2):
                            own_chunk(r, k).start()
                    for a in range(1, n):
                        for r in range(2):
                            own_piece(a, r).start()
                    fetch(0).start()
                if u < last:
                    if u + 1 >= N_UNITS:
                        make_available(u + 1)
                    fetch(u + 1).start()
                if u == late:
                    for a in range(1, n):
                        landed(a, 0).wait_recv()
                        landed(a, 1).wait_recv()
                        relay(a).start()
                        for r in range(2):
                            to_sibling(a, r, c).start()
                        for r in range(2):
                            to_sibling(a, r, 1 - c).wait_recv()
                fetch(u).wait()

        proj_ref[...] = jnp.dot(h_ref[...], wbuf[t % 2], preferred_element_type=F32).astype(BF16)

        @pl.when(t == last)
        def _():
            for a in range(1, n):
                landed(a, 2).wait_recv()
                to_sibling(a, 2, c).start()
                to_sibling(a, 2, 1 - c).wait_recv()
            for k in range(N_UNITS):
                for r in range(2):
                    own_chunk(r, k).wait_send()
                relay_chunk(k).wait_send()
                for r in range(3):
                    sibling_chunk(r, k, c).wait_send()
            for a in range(1, n):
                for r in range(2):
                    own_piece(a, r).wait_send()
                relay(a).wait_send()
                for r in range(3):
                    to_sibling(a, r, c).wait_send()

    def out_index(t, pos_ref):
        chip, unit = _proj_unit(t, pos_ref[1])
        return 0, chip * N_UNITS + unit

    any_spec = pl.BlockSpec(memory_space=pl.ANY)
    outs = pl.pallas_call(
        body, name="f1_proj",
        grid_spec=pltpu.PrefetchScalarGridSpec(
            num_scalar_prefetch=1, grid=(N_STEPS_PROJ,),
            in_specs=[pl.BlockSpec((tp, d), lambda t, pos_ref: (0, 0))] + [any_spec] * n,
            out_specs=[pl.BlockSpec((tp, cu), out_index)] + [any_spec] * n,
            scratch_shapes=[pltpu.VMEM((2, d, cu), BF16), pltpu.SemaphoreType.DMA((2,)),
                            pltpu.SemaphoreType.DMA((w_sems + 6 * (n - 1),)),
                            pltpu.SemaphoreType.DMA((w_sems + 6 * (n - 1),))]),
        out_shape=[jax.ShapeDtypeStruct((tp, nsh * sw), BF16)]
        + [jax.ShapeDtypeStruct(b.shape, b.dtype) for b in bufs],
        input_output_aliases={2 + a: 1 + a for a in range(n)},
        compiler_params=_params(("arbitrary",)),
    )(pos, h, *bufs)
    return outs[0], outs[1:]


def _dh_bwd(dproj, wg_in, x, front, ds2, norm_g, part):
    seq, d = x.shape
    tp = seq + TM
    _, nsh, _, sw = wg_in.shape
    tmb = tp // N_ROW_TILES_BIG
    tail = tmb - TM
    last = N_ROW_TILES_BIG - 1

    def body(dp_ref, w_hbm, x_ref, front_ref, ds2_ref, g_ref, part_ref, gx_hbm, dfront_ref, dng_ref, land_ref,
             wbuf, gacc, dsbuf, wsem, osems, send_sems, recv_sems):
        exchange = _chip_exchange(part_ref, land_ref, send_sems, recv_sems)
        i = pl.program_id(0)

        def x_rows_out(step):
            return pltpu.make_async_copy(dsbuf.at[step % 2], gx_hbm.at[pl.ds(step * tmb, tmb), :], osems.at[step % 2])

        last_out = pltpu.make_async_copy(dsbuf.at[last % 2, pl.ds(0, tail), :],
                                         gx_hbm.at[pl.ds(last * tmb, tail), :], osems.at[last % 2])

        @pl.when(i == 0)
        def _():
            exchange.start()
            gacc[...] = jnp.zeros_like(gacc)
            whole = pltpu.make_async_copy(w_hbm.at[0], wbuf, wsem)
            whole.start()
            whole.wait()

        dh = None
        for j in range(nsh):
            part = lax.dot_general(dp_ref[:, j * sw:(j + 1) * sw], wbuf[j], (((1,), (1,)), ((), ())),
                                   preferred_element_type=F32)
            dh = part if dh is None else dh + part
        s = _big_row_tile(x_ref, front_ref, i)
        r = lax.rsqrt(jnp.mean(s * s, axis=-1, keepdims=True) + EPS)
        gacc[...] += (dh * s * r).reshape(tmb // 8, 8, d).sum(axis=0)
        t = dh * g_ref[...]

        @pl.when(i >= 2)
        def _():
            x_rows_out(i - 2).wait()

        dsbuf[i % 2] = ds2_ref[...] + r * t - s * (r * r * r) * jnp.mean(t * s, axis=-1, keepdims=True)

        @pl.when(i < last)
        def _():
            x_rows_out(i).start()

        @pl.when(i == last)
        def _():
            last_out.start()
            dfront_ref[...] = dsbuf[last % 2, tail:, :]
            dng_ref[...] = jnp.broadcast_to(jnp.sum(gacc[...], axis=0, keepdims=True), (8, d))
            exchange.finish()
            x_rows_out(last - 1).wait()
            last_out.wait()

    any_spec = pl.BlockSpec(memory_space=pl.ANY)
    return pl.pallas_call(
        body, name="b2_dh", grid=(N_ROW_TILES_BIG,),
        in_specs=[pl.BlockSpec((tmb, nsh * sw), lambda i: (i, 0)), any_spec,
                  _big_row_spec(seq, tmb, d, lambda i: i),
                  pl.BlockSpec((TM, d), lambda i: (0, 0)),
                  pl.BlockSpec((tmb, d), lambda i: (i, 0)),
                  pl.BlockSpec((1, d), lambda i: (0, 0)), any_spec],
        out_specs=[any_spec, pl.BlockSpec((TM, d), lambda i: (0, 0)),
                   pl.BlockSpec((8, d), lambda i: (0, 0)), any_spec],
        out_shape=[jax.ShapeDtypeStruct((seq, d), F32), jax.ShapeDtypeStruct((TM, d), F32),
                   jax.ShapeDtypeStruct((8, d), F32), jax.ShapeDtypeStruct(part.shape, part.dtype)],
        scratch_shapes=[pltpu.VMEM((nsh, d, sw), BF16), pltpu.VMEM((8, d), F32), pltpu.VMEM((2, tmb, d), F32),
                        pltpu.SemaphoreType.DMA, pltpu.SemaphoreType.DMA((2,)),
                        pltpu.SemaphoreType.DMA((3,)), pltpu.SemaphoreType.DMA((3,))],
        compiler_params=pltpu.CompilerParams(dimension_semantics=("arbitrary",),
                                             vmem_limit_bytes=VMEM_LIMIT_BIG),
    )(dproj, wg_in, x, front, ds2, norm_g, part)


def _col_block(width, cap):
    return max(b for b in range(128, cap + 1, 128) if width % b == 0)


def _dw_reduced(lhs_t, rhs, cw, nblk, operands, groups, out_dims, out_block, out_index, carried, name):
    na, d, tp = lhs_t.shape
    rg = d // groups
    hh = rg // 2
    nc = len(carried)

    def body(*refs):
        l_ref, r_ref = refs[:2]
        part_refs = refs[2:2 + nc]
        p32_ref, pbf_ref = refs[2 + nc:4 + nc]
        land_refs = refs[4 + nc:4 + 2 * nc]
        res, rbuf, send_sems, recv_sems = refs[4 + 2 * nc:8 + 2 * nc]
        xsems = refs[8 + 2 * nc:]
        exchanges = [_chip_exchange(part_refs[e], land_refs[e], xsems[2 * e], xsems[2 * e + 1]) for e in range(nc)]
        exchange = _Exchange([s for ex in exchanges for s in ex.sends], [r for ex in exchanges for r in ex.recvs])
        x, y, c = _mesh_pos()
        t = pl.program_id(0)
        u = jnp.maximum(t - 1, 0)

        def to_sibling(blk):
            return pltpu.make_async_remote_copy(
                src_ref=res.at[blk % 2, :, pl.ds((1 - c) * hh, hh), :], dst_ref=rbuf.at[blk % 2],
                send_sem=send_sems.at[blk], recv_sem=recv_sems.at[blk],
                device_id=(x, y, 1 - c), device_id_type=MESH)

        @pl.when(t == 0)
        def _():
            exchange.start()

        @pl.when(t < nblk)
        def _():
            res[t % 2] = jnp.dot(l_ref[...], r_ref[...], preferred_element_type=F32).reshape(groups, rg, cw)

        @pl.when(t >= 1)
        def _():
            to_sibling(u).wait_recv()
            p = res[u % 2, :, pl.ds(c * hh, hh), :] + rbuf[u % 2]
            p32_ref[...] = p.reshape(p32_ref.shape)
            pbf_ref[...] = p.reshape(pbf_ref.shape).astype(BF16)

        @pl.when(t < nblk)
        def _():
            to_sibling(t).start()

        @pl.when(t >= 1)
        def _():
            to_sibling(u).wait_send()

        @pl.when(t == nblk)
        def _():
            exchange.finish()

    any_spec = pl.BlockSpec(memory_space=pl.ANY)
    last = nblk - 1
    out_spec = pl.BlockSpec(out_block, lambda t: out_index(jnp.maximum(t - 1, 0)))
    outs = pl.pallas_call(
        body, name=name, grid=(nblk + 1,),
        in_specs=[pl.BlockSpec((None, d, tp), lambda t: (operands(jnp.minimum(t, last))[0], 0, 0)),
                  pl.BlockSpec((None, tp, cw), lambda t: (operands(jnp.minimum(t, last))[0], 0,
                                                          operands(jnp.minimum(t, last))[1]))]
        + [any_spec] * nc,
        out_specs=[out_spec, out_spec] + [any_spec] * nc,
        out_shape=[jax.ShapeDtypeStruct(out_dims, F32), jax.ShapeDtypeStruct(out_dims, BF16)]
        + [jax.ShapeDtypeStruct(e.shape, e.dtype) for e in carried],
        scratch_shapes=[pltpu.VMEM((2, groups, rg, cw), F32), pltpu.VMEM((2, groups, hh, cw), F32),
                        pltpu.SemaphoreType.DMA((nblk,)), pltpu.SemaphoreType.DMA((nblk,))]
        + [pltpu.SemaphoreType.DMA((3,)), pltpu.SemaphoreType.DMA((3,))] * nc,
        compiler_params=_params(("arbitrary",)),
    )(lhs_t, rhs, *carried)
    return outs[0], outs[1], outs[2:]


def _conv_a_taps(first_lag, last_lag):
    out = []
    for r in range(8):
        taps = [(q, 8 * q + r) for q in range(5) if first_lag <= 8 * q + r <= last_lag]
        if taps:
            out.append((r, taps))
    return out


def _tile_block(i, nt):
    return jnp.where(i == 0, nt - 1, i - 1)


def _mix_fwd(x, front, proj, target, w3, wa, wb, conv_a_b, ln_g, ln_b, b_a_out, final_g, norm_g):
    seq, d = x.shape
    tp = seq + TM
    nt = tp // TM
    nrb = TM // RB
    shl = TM + SHIFT_ROWS

    def body(x_ref, front_ref, proj_ref, tgt_ref, w3_ref, wa_ref, wb_ref, cab_ref, lng_ref, lnb_ref, bao_ref, fg_ref,
             ng_ref, ca_ref, cb_ref, ya_ref, yb_ref, abmt_ref, ds2_ref, ht_ref, loss_ref, dfg_ref,
             abm_ref, ext_a, ext_b, sh, s2_s, lacc, gacc):
        i = pl.program_id(0)

        def split(k, rows):
            return proj_ref[rows, k * d:(k + 1) * d].astype(F32)

        def s_tile():
            return jnp.where(i == 0, front_ref[...], x_ref[...])

        s_in = s_tile()
        h = s_in * lax.rsqrt(jnp.mean(s_in * s_in, axis=-1, keepdims=True) + EPS) * ng_ref[...]
        ht_ref[...] = h.astype(BF16).T

        @pl.when(i == 0)
        def _():
            ext_a[0:HALO_A, :] = jnp.zeros((HALO_A, d), F32)
            ext_b[0:HALO_B, :] = jnp.zeros((HALO_B, d), F32)
            lacc[...] = jnp.zeros_like(lacc)
            gacc[...] = jnp.zeros_like(gacc)

        def conv_in(rb, carry):
            rows = _rows(rb)
            ua0 = split(0, rows) * _sigmoid(split(1, rows))
            ext_a[pl.ds(pl.multiple_of(HALO_A + rb * RB, 8), RB), :] = ua0
            ext_b[pl.ds(pl.multiple_of(HALO_B + rb * RB, 8), RB), :] = split(4, rows) * split(5, rows)
            ca_ref[rows, :] = jnp.broadcast_to(cab_ref[...], (RB, d))
            return carry
        lax.fori_loop(0, nrb, conv_in, 0)

        @pl.when(i == 0)
        def _():
            abmt_ref[...] = jnp.zeros_like(abmt_ref)
            ds2_ref[...] = jnp.zeros_like(ds2_ref)

        @pl.when(i > 0)
        def _():
            tile_after_conv_inputs(split, s_tile, tgt_ref, w3_ref, wa_ref, wb_ref, lng_ref, lnb_ref, bao_ref, fg_ref,
                                   ca_ref, cb_ref, ya_ref, yb_ref, abmt_ref, ds2_ref, abm_ref, ext_a, ext_b, sh,
                                   s2_s, lacc, gacc)

        ext_a[0:HALO_A, :] = ext_a[TM:TM + HALO_A, :]
        ext_b[0:HALO_B, :] = ext_b[TM:TM + HALO_B, :]

        @pl.when(i == nt - 1)
        def _():
            loss_ref[...] = jnp.broadcast_to(0.5 * jnp.sum(lacc[...]) * (1.0 / d), (8, 128))
            dfg_ref[...] = jnp.broadcast_to(jnp.sum(gacc[...], axis=0, keepdims=True), (8, d))

    def tile_after_conv_inputs(split, s_tile, tgt_ref, w3_ref, wa_ref, wb_ref, lng_ref, lnb_ref, bao_ref, fg_ref,
                               ca_ref, cb_ref, ya_ref, yb_ref, abmt_ref, ds2_ref, abm_ref, ext_a, ext_b, sh, s2_s,
                               lacc, gacc):
        for r, taps in _conv_a_taps(HALO_A - CONV_A + 1, HALO_A):
            if r == 0:
                src = ext_a
            else:
                sh[...] = ext_a[r:r + shl, :]
                src = sh

            def conv_acc(rb, carry, src=src, taps=taps):
                rows = _rows(rb)
                acc = ca_ref[rows, :]
                for q, lag in taps:
                    k = lag - (HALO_A - CONV_A + 1)
                    acc = acc + src[pl.ds(pl.multiple_of(rb * RB + 8 * q, 8), RB), :] * wa_ref[k:k + 1, :]
                ca_ref[rows, :] = acc
                return carry
            lax.fori_loop(0, nrb, conv_acc, 0)

        cb_ref[...] = ext_b[HALO_B:HALO_B + TM, :] * wb_ref[2:3, :]
        for k in range(CONV_B - 1):
            off = HALO_B - CONV_B + 1 + k
            sh[0:TM, :] = ext_b[off:off + TM, :]
            cb_ref[...] += sh[0:TM, :] * wb_ref[k:k + 1, :]

        def branches(rb, carry):
            rows = _rows(rb)
            ca = ca_ref[rows, :]
            mu = jnp.mean(ca, axis=-1, keepdims=True)
            xc = ca - mu
            rstd = lax.rsqrt(jnp.mean(xc * xc, axis=-1, keepdims=True) + EPS)
            ln = xc * rstd * lng_ref[...] + lnb_ref[...]
            ua = ln * _sigmoid(ln)
            a_z = split(2, rows)
            abm_ref[0, rows, :] = (ua * (a_z * _sigmoid(a_z))).astype(BF16)
            b_z = split(6, rows)
            ub = split(3, rows) * cb_ref[rows, :]
            abm_ref[1, rows, :] = (ub * (b_z * _sigmoid(b_z))).astype(BF16)
            return carry
        lax.fori_loop(0, nrb, branches, 0)

        ya_ref[...] = jnp.dot(abm_ref[0], w3_ref[0], preferred_element_type=F32) + bao_ref[...]
        yb_ref[...] = jnp.dot(abm_ref[1], w3_ref[1], preferred_element_type=F32)

        def merge(rb, carry):
            rows = _rows(rb)
            m = _sigmoid(split(7, rows)) * ya_ref[rows, :] + _sigmoid(split(8, rows)) * yb_ref[rows, :]
            abm_ref[2, rows, :] = m.astype(BF16)
            return carry
        lax.fori_loop(0, nrb, merge, 0)

        s2_s[...] = s_tile() + jnp.dot(abm_ref[2], w3_ref[2], preferred_element_type=F32)
        for k in range(3):
            abmt_ref[k] = abm_ref[k].T

        def head(rb, carry):
            rows = _rows(rb)
            s2 = s2_s[rows, :]
            r2 = lax.rsqrt(jnp.mean(s2 * s2, axis=-1, keepdims=True) + EPS)
            diff = s2 * r2 * fg_ref[...] - tgt_ref[rows, :]
            lacc[...] += diff * diff
            dy = diff * (1.0 / d)
            gacc[...] += (dy * s2 * r2).reshape(RB // 8, 8, d).sum(axis=0)
            t = dy * fg_ref[...]
            ds2_ref[rows, :] = r2 * t - s2 * (r2 * r2 * r2) * jnp.mean(t * s2, axis=-1, keepdims=True)
            return carry
        lax.fori_loop(0, nrb, head, 0)

    row_f32 = pl.BlockSpec((TM, d), lambda i: (_tile_block(i, nt), 0))
    x_rows = pl.BlockSpec((TM, d), lambda i: (jnp.maximum(i - 1, 0), 0))
    const = lambda shape: pl.BlockSpec(shape, lambda i: (0,) * len(shape))
    return pl.pallas_call(
        body, name="f2_mix", grid=(nt,),
        in_specs=[x_rows, const((TM, d)),
                  pl.BlockSpec((TM, N_SPLIT * d), lambda i: (_tile_block(i, nt), 0)),
                  x_rows,
                  const((3, d, d)), const(wa.shape), const(wb.shape)] + [const((1, d))] * 6,
        out_specs=[row_f32, row_f32, row_f32, row_f32,
                   pl.BlockSpec((3, d, TM), lambda i: (0, 0, _tile_block(i, nt))),
                   row_f32, pl.BlockSpec((d, TM), lambda i: (0, _tile_block(i, nt))),
                   const((8, 128)), const((8, d))],
        out_shape=[jax.ShapeDtypeStruct((tp, d), F32)] * 4
        + [jax.ShapeDtypeStruct((3, d, tp), BF16), jax.ShapeDtypeStruct((tp, d), F32),
           jax.ShapeDtypeStruct((d, tp), BF16),
           jax.ShapeDtypeStruct((8, 128), F32), jax.ShapeDtypeStruct((8, d), F32)],
        scratch_shapes=[pltpu.VMEM((3, TM, d), BF16),
                        pltpu.VMEM((HALO_A + TM, d), F32), pltpu.VMEM((HALO_B + TM, d), F32),
                        pltpu.VMEM((shl, d), F32), pltpu.VMEM((TM, d), F32),
                        pltpu.VMEM((RB, d), F32), pltpu.VMEM((8, d), F32)],
        compiler_params=_params(("arbitrary",)),
    )(x, front, proj, target, w3, wa, wb, conv_a_b, ln_g, ln_b, b_a_out, final_g, norm_g)


def _mix_bwd(ds2, proj, ca, cb, ya, yb, w3, wa, wb, ln_g, ln_b):
    tp, d = ds2.shape
    nt = tp // TM
    nrb = TM // RB
    shl = TM + SHIFT_ROWS
    nt_dims = (((1,), (1,)), ((), ()))

    def body(ds2_ref, proj_ref, ca_ref, cb_ref, ya_ref, yb_ref, w3_ref, wa_ref, wb_ref, lng_ref, lnb_ref,
             dproj_ref, d3_ref, sm_ref, ext_d, ext_e, sh, dm_s, dpa_s, dpb_s, dua0_s, acc):
        step = pl.program_id(0)

        def split(k, rows):
            return proj_ref[rows, k * d:(k + 1) * d].astype(F32)

        def put(k, rows, val):
            dproj_ref[rows, k * d:(k + 1) * d] = val.astype(BF16)

        def accum(row, val):
            acc[row] += val.reshape(RB // 8, 8, d).sum(axis=0)

        @pl.when(step == 0)
        def _():
            ext_d[TM:TM + HALO_A, :] = jnp.zeros((HALO_A, d), F32)
            ext_e[TM:TM + HALO_B, :] = jnp.zeros((HALO_B, d), F32)
            acc[...] = jnp.zeros_like(acc)

        front = step == nt - 1

        @pl.when(front)
        def _():
            d3_ref[...] = jnp.zeros_like(d3_ref)

            def conv_only(rb, carry):
                rows = _rows(rb)
                zeros = jnp.zeros((RB, d), F32)
                for k in (2, 3, 6, 7, 8):
                    put(k, rows, zeros)
                ext_d[rows, :] = zeros
                ext_e[rows, :] = zeros
                dua0_s[rows, :] = zeros
                dm_s[rows, :] = split(0, rows) * _sigmoid(split(1, rows))
                return carry
            lax.fori_loop(0, nrb, conv_only, 0)

        @pl.when(jnp.logical_not(front))
        def _():
            tile_to_conv_outputs(split, put, accum, ds2_ref, ca_ref, cb_ref, ya_ref, yb_ref, w3_ref, lng_ref, lnb_ref,
                                 d3_ref, ext_d, ext_e, dm_s, dpa_s, dpb_s, dua0_s)

        tile_conv_transposes(split, put, accum, wa_ref, wb_ref, ext_d, ext_e, sh, dm_s, dpb_s, dua0_s)

        @pl.when(front)
        def _():
            for row in range(SM_ROWS):
                sm_ref[row:row + 1, :] = jnp.sum(acc[row], axis=0, keepdims=True)

    def tile_to_conv_outputs(split, put, accum, ds2_ref, ca_ref, cb_ref, ya_ref, yb_ref, w3_ref, lng_ref, lnb_ref,
                             d3_ref, ext_d, ext_e, dm_s, dpa_s, dpb_s, dua0_s):
        d3_ref[2] = ds2_ref[...].astype(BF16)
        dm_s[...] = lax.dot_general(d3_ref[2], w3_ref[2], nt_dims, preferred_element_type=F32)

        def gates(rb, carry):
            rows = _rows(rb)
            dm = dm_s[rows, :]
            sa = _sigmoid(split(7, rows))
            sb = _sigmoid(split(8, rows))
            ya_v = ya_ref[rows, :]
            yb_v = yb_ref[rows, :]
            put(7, rows, dm * ya_v * sa * (1.0 - sa))
            put(8, rows, dm * yb_v * sb * (1.0 - sb))
            dya = dm * sa
            accum(ROW_DBAO, dya)
            d3_ref[0, rows, :] = dya.astype(BF16)
            d3_ref[1, rows, :] = (dm * sb).astype(BF16)
            return carry
        lax.fori_loop(0, nrb, gates, 0)

        dpa_s[...] = lax.dot_general(d3_ref[0], w3_ref[0], nt_dims, preferred_element_type=F32)
        dpb_s[...] = lax.dot_general(d3_ref[1], w3_ref[1], nt_dims, preferred_element_type=F32)

        def branches(rb, carry):
            rows = _rows(rb)
            ca_v = ca_ref[rows, :]
            mu = jnp.mean(ca_v, axis=-1, keepdims=True)
            xc = ca_v - mu
            rstd = lax.rsqrt(jnp.mean(xc * xc, axis=-1, keepdims=True) + EPS)
            xhat = xc * rstd
            ln = xhat * lng_ref[...] + lnb_ref[...]
            sl = _sigmoid(ln)
            ua = ln * sl
            a_z = split(2, rows)
            sz = _sigmoid(a_z)
            dpa = dpa_s[rows, :]
            put(2, rows, dpa * ua * (sz * (1.0 + a_z * (1.0 - sz))))
            dln = dpa * (a_z * sz) * (sl * (1.0 + ln * (1.0 - sl)))
            accum(ROW_DLNG, dln * xhat)
            accum(ROW_DLNB, dln)
            dxh = dln * lng_ref[...]
            dca = rstd * (dxh - jnp.mean(dxh, axis=-1, keepdims=True)
                          - xhat * jnp.mean(dxh * xhat, axis=-1, keepdims=True))
            accum(ROW_DCAB, dca)
            ext_d[rows, :] = dca
            dua0_s[rows, :] = jnp.zeros((RB, d), F32)
            dm_s[rows, :] = split(0, rows) * _sigmoid(split(1, rows))
            b_z = split(6, rows)
            szb = _sigmoid(b_z)
            dpb = dpb_s[rows, :]
            b_b = split(3, rows)
            cb_v = cb_ref[rows, :]
            put(6, rows, dpb * (b_b * cb_v) * (szb * (1.0 + b_z * (1.0 - szb))))
            dub = dpb * (b_z * szb)
            put(3, rows, dub * cb_v)
            ext_e[rows, :] = dub * b_b
            return carry
        lax.fori_loop(0, nrb, branches, 0)

    def tile_conv_transposes(split, put, accum, wa_ref, wb_ref, ext_d, ext_e, sh, dm_s, dpb_s, dua0_s):
        for r, taps in _conv_a_taps(0, CONV_A - 1):
            if r == 0:
                src = ext_d
            else:
                sh[...] = ext_d[r:r + shl, :]
                src = sh

            def conv_t(rb, carry, src=src, taps=taps):
                rows = _rows(rb)
                ua0 = dm_s[rows, :]
                dua0 = dua0_s[rows, :]
                for q, lag in taps:
                    k = CONV_A - 1 - lag
                    slab = src[pl.ds(pl.multiple_of(rb * RB + 8 * q, 8), RB), :]
                    dua0 = dua0 + slab * wa_ref[k:k + 1, :]
                    accum(ROW_DWA + k, slab * ua0)
                dua0_s[rows, :] = dua0
                return carry
            lax.fori_loop(0, nrb, conv_t, 0)
        ext_d[TM:TM + HALO_A, :] = ext_d[0:HALO_A, :]

        dpb_s[...] = ext_e[0:TM, :] * wb_ref[CONV_B - 1:CONV_B, :]
        for lag in range(CONV_B):
            k = CONV_B - 1 - lag
            if lag > 0:
                sh[0:TM, :] = ext_e[lag:lag + TM, :]
                dpb_s[...] += sh[0:TM, :] * wb_ref[k:k + 1, :]
            src = ext_e if lag == 0 else sh

            def conv_b_w(rb, carry, src=src, k=k):
                rows = _rows(rb)
                accum(ROW_DWB + k, src[rows, :] * (split(4, rows) * split(5, rows)))
                return carry
            lax.fori_loop(0, nrb, conv_b_w, 0)
        ext_e[TM:TM + HALO_B, :] = ext_e[0:HALO_B, :]

        def inputs(rb, carry):
            rows = _rows(rb)
            dua0 = dua0_s[rows, :]
            a_val = split(0, rows)
            sg = _sigmoid(split(1, rows))
            put(0, rows, dua0 * sg)
            put(1, rows, dua0 * a_val * sg * (1.0 - sg))
            dcbin = dpb_s[rows, :]
            put(4, rows, dcbin * split(5, rows))
            put(5, rows, dcbin * split(4, rows))
            return carry
        lax.fori_loop(0, nrb, inputs, 0)

    rev = lambda i: (_tile_block(nt - 1 - i, nt), 0)
    row_f32 = pl.BlockSpec((TM, d), rev)
    const = lambda shape: pl.BlockSpec(shape, lambda i: (0,) * len(shape))
    return pl.pallas_call(
        body, name="b1_mix", grid=(nt,),
        in_specs=[row_f32, pl.BlockSpec((TM, N_SPLIT * d), rev), row_f32, row_f32, row_f32, row_f32,
                  const((3, d, d)), const(wa.shape), const(wb.shape), const((1, d)), const((1, d))],
        out_specs=[pl.BlockSpec((TM, N_SPLIT * d), rev),
                   pl.BlockSpec((3, TM, d), lambda i: (0, _tile_block(nt - 1 - i, nt), 0)),
                   const((SM_ROWS, d))],
        out_shape=[jax.ShapeDtypeStruct((tp, N_SPLIT * d), BF16), jax.ShapeDtypeStruct((3, tp, d), BF16),
                   jax.ShapeDtypeStruct((SM_ROWS, d), F32)],
        scratch_shapes=[pltpu.VMEM((TM + HALO_A, d), F32), pltpu.VMEM((TM + HALO_B, d), F32),
                        pltpu.VMEM((shl, d), F32), pltpu.VMEM((TM, d), F32), pltpu.VMEM((TM, d), F32),
                        pltpu.VMEM((TM, d), F32), pltpu.VMEM((TM, d), F32),
                        pltpu.VMEM((SM_ROWS, 8, d), F32)],
        compiler_params=_params(("arbitrary",)),
    )(ds2, proj, ca, cb, ya, yb, w3, wa, wb, ln_g, ln_b)


def kernel(x, meta_tokens, norm_g, w_in, conv_a_w, conv_a_b, ln_a_g, ln_a_b, w_a_out, b_a_out, conv_b_w, w_b_out, w_out, final_g, loss_target, m_meta_tokens, m_norm_g, m_w_in, m_conv_a_w, m_conv_a_b, m_ln_a_g, m_ln_a_b, m_w_a_out, m_b_a_out, m_conv_b_w, m_w_b_out, m_w_out, m_final_g, v_meta_tokens, v_norm_g, v_w_in, v_conv_a_w, v_conv_a_b, v_ln_a_g, v_ln_a_b, v_w_a_out, v_b_a_out, v_conv_b_w, v_w_b_out, v_w_out, v_final_g):
    seq, d = x.shape[1], x.shape[2]
    dc = meta_tokens.shape[1]
    sw = w_in.shape[2]
    rsh = w_a_out.shape[1]
    xi, yi, ci = _mesh_pos()
    me = 2 * xi + yi
    pos = jnp.stack([ci, me]).astype(jnp.int32)

    conv_rows = HALO_A + HALO_B + 8
    convs = jnp.concatenate([
        jnp.pad(conv_a_w[0], ((0, HALO_A - CONV_A), (0, 0))),
        jnp.pad(conv_b_w[0], ((0, HALO_B - CONV_B), (0, 0))), jnp.zeros((8, dc), F32)], axis=0)[None]
    w3_own = jnp.stack([w_a_out[0], w_b_out[0], w_out[0]])
    fg2 = final_g.reshape(1, d)
    xs = x[0]

    h, front = _h_prep(xs, meta_tokens, norm_g)
    proj, (wg_in, wg3, convg) = _proj_fwd(h, [_place_own(w_in, pos, BF16, "place_in"),
                                               _place_own(w3_own, pos, BF16, "place_sq"),
                                               _place_own(convs, pos, F32, "place_conv")], pos)
    w3 = wg3.reshape(3, N_CHIPS * rsh, d)
    convg = jnp.transpose(convg[0], (1, 0, 2)).reshape(conv_rows, N_CHIPS * dc)
    wa_full = convg[0:HALO_A]
    wb_full = convg[HALO_A:HALO_A + HALO_B]
    ca, cb, ya, yb, abm_t, ds2, h_t, loss8, dfg8 = _mix_fwd(
        xs, front, proj, loss_target[0], w3, wa_full, wb_full, conv_a_b, ln_a_g, ln_a_b, b_a_out, fg2, norm_g)
    dproj, d3, sm = _mix_bwd(ds2, proj, ca, cb, ya, yb, w3, wa_full, wb_full, ln_a_g, ln_a_b)
    cw_sq = _col_block(d, 512)
    per_sq = d // cw_sq
    p32_sq, pbf_sq, _ = _dw_reduced(
        abm_t, d3, cw_sq, 3 * per_sq, lambda t: (t // per_sq, t % per_sq), N_CHIPS,
        (3, N_CHIPS, rsh // 2, d), (None, N_CHIPS, rsh // 2, cw_sq),
        lambda u: (u // per_sq, 0, 0, u % per_sq), [], "dw_square")
    cw_in = _col_block(sw, 768)
    ncol = sw // cw_in
    p32_in, pbf_in, (l_sq,) = _dw_reduced(
        h_t[None], dproj[None], cw_in, N_CHIPS * ncol, lambda t: (0, t), 1,
        (1, N_CHIPS, d // 2, sw), (None, None, d // 2, cw_in), lambda u: (0, u // ncol, 0, u % ncol),
        [pbf_sq], "dw_in")
    grad_x, dfront, dng8, l_in = _dh_bwd(dproj, wg_in, xs, front, ds2, norm_g, pbf_in)
    half_in = _sum_chips([(p32_in, l_in)], sw, pos, "rs_sum_in")
    half_sq = _sum_chips([(p32_sq, l_sq)], d, pos, "rs_sum_sq")
    tail_row = lax.broadcasted_iota(jnp.int32, (8, d), 0)
    tail = jnp.where(tail_row == 0, dng8, jnp.where(tail_row == 1, dfg8,
                     jnp.where(tail_row == 2, loss8[0, 0], 0.0)))
    block = jnp.concatenate([sm, dfront[TM - N_META:TM], tail], axis=0)
    (other_in, other_sq), red = _sibling_swap([half_in, half_sq], block)
    col = lax.dynamic_slice(red, (0, me * dc), (AR_ROWS, dc))
    g_small = {
        "meta_tokens": col[ROW_DMETA:ROW_DMETA + N_META],
        "norm_g": red[ROW_DNG:ROW_DNG + 1],
        "conv_a_w": col[ROW_DWA:ROW_DWA + CONV_A][None],
        "conv_a_b": red[ROW_DCAB:ROW_DCAB + 1],
        "ln_a_g": red[ROW_DLNG:ROW_DLNG + 1],
        "ln_a_b": red[ROW_DLNB:ROW_DLNB + 1],
        "b_a_out": red[ROW_DBAO:ROW_DBAO + 1],
        "conv_b_w": col[ROW_DWB:ROW_DWB + CONV_B][None],
        "final_g": red[ROW_DFG],
    }

    upd_in = _adam_halves([w_in], [m_w_in], [v_w_in], half_in, other_in, pos, "adam_in")
    upd_sq = _adam_halves([w_a_out, w_b_out, w_out], [m_w_a_out, m_w_b_out, m_w_out],
                          [v_w_a_out, v_w_b_out, v_w_out], half_sq, other_sq, pos, "adam_sq")
    small_w = {"meta_tokens": (meta_tokens, m_meta_tokens, v_meta_tokens), "norm_g": (norm_g, m_norm_g, v_norm_g),
               "conv_a_w": (conv_a_w, m_conv_a_w, v_conv_a_w), "conv_a_b": (conv_a_b, m_conv_a_b, v_conv_a_b),
               "ln_a_g": (ln_a_g, m_ln_a_g, v_ln_a_g), "ln_a_b": (ln_a_b, m_ln_a_b, v_ln_a_b),
               "b_a_out": (b_a_out, m_b_a_out, v_b_a_out), "conv_b_w": (conv_b_w, m_conv_b_w, v_conv_b_w),
               "final_g": (final_g, m_final_g, v_final_g)}
    names_small = list(small_w)
    as2d = lambda t: t.reshape(-1, t.shape[-1])
    upd_small = _adam_small([(as2d(small_w[k][0]), as2d(g_small[k]), as2d(small_w[k][1]), as2d(small_w[k][2]))
                             for k in names_small])

    grads, deltas, new_m, new_v = dict(g_small), {}, {}, {}
    for k, upd in zip(names_small, upd_small):
        deltas[k], new_m[k], new_v[k] = [t.reshape(small_w[k][0].shape) for t in upd]
    grads["w_in"], deltas["w_in"], new_m["w_in"], new_v["w_in"] = upd_in
    for idx, k in enumerate(["w_a_out", "w_b_out", "w_out"]):
        grads[k], deltas[k], new_m[k], new_v[k] = upd_sq[4 * idx:4 * idx + 4]

    loss = red[ROW_LOSS, 0]
    order = ["meta_tokens", "norm_g", "w_in", "conv_a_w", "conv_a_b", "ln_a_g", "ln_a_b", "w_a_out", "b_a_out",
             "conv_b_w", "w_b_out", "w_out", "final_g"]
    return (loss, grad_x[None], *[grads[k] for k in order], *[deltas[k] for k in order],
            *[new_m[k] for k in order], *[new_v[k] for k in order])
```

```python
import jax
import jax.numpy as jnp
from jax import lax
from jax.experimental import pallas as pl
from jax.experimental.pallas import tpu as pltpu

F32 = jnp.float32
BF16 = jnp.bfloat16
MESH = pl.DeviceIdType.MESH

EPS = 1e-6
N_META = 16
N_SPLIT = 9
CONV_A = 31
CONV_B = 3
HALO_A = 32
HALO_B = 8
SHIFT_ROWS = 24
TM = 256
RB_FWD = 128
RB_BWD = 64
N_ROW_TILES_BIG = 8
ROW_BLOCK = 256
N_CHIPS = 4
VMEM_LIMIT = 56 * 1024 * 1024
VMEM_LIMIT_BIG = 62 * 1024 * 1024

ADAM_LR = 0.001
ADAM_B1 = 0.9
ADAM_B2 = 0.999
ADAM_EPS = 1e-08
ADAM_WD = 0.01
ADAM_STEP = 10

ROW_DWA = 0
ROW_DWB = 32
ROW_DCAB = 40
ROW_DLNG = 41
ROW_DLNB = 42
ROW_DBAO = 43
SM_ROWS = 48
ROW_DMETA = 48
ROW_DNG = 64
ROW_DFG = 65
ROW_LOSS = 66
AR_ROWS = 72


def _sigmoid(v):
    return 0.5 * jnp.tanh(0.5 * v) + 0.5


def _params(sem, **kw):
    return pltpu.CompilerParams(dimension_semantics=sem, vmem_limit_bytes=VMEM_LIMIT, **kw)


def _rows(rb, n):
    return pl.ds(pl.multiple_of(rb * n, n), n)


def _mesh_pos():
    x, y, c = lax.axis_index("x"), lax.axis_index("y"), lax.axis_index("c")
    return x, y, c


def _half(ref, j, c):
    h = ref.shape[2] // 2
    return ref.at[:, j, pl.ds(c * h, h), :]


class _Exchange:
    def __init__(self, sends, recvs):
        self.sends, self.recvs = sends, recvs

    def start(self):
        for cp in self.sends:
            cp.start()

    def finish(self):
        for cp in self.recvs:
            cp.wait_recv()
        for cp in self.sends:
            cp.wait_send()


def _chip_exchange(part_ref, land_ref, send_sems, recv_sems):
    x, y, c = _mesh_pos()
    me = 2 * x + y
    sends, recvs = [], []
    for k, (px, py) in enumerate([(1 - x, y), (x, 1 - y), (1 - x, 1 - y)]):
        sems = dict(send_sem=send_sems.at[k], recv_sem=recv_sems.at[k], device_id=(px, py, c), device_id_type=MESH)
        sends.append(pltpu.make_async_remote_copy(
            src_ref=part_ref.at[:, 2 * px + py], dst_ref=land_ref.at[:, me], **sems))
        landed = land_ref.at[:, 2 * px + py]
        recvs.append(pltpu.make_async_remote_copy(src_ref=landed, dst_ref=landed, **sems))
    return _Exchange(sends, recvs)

def _sibling_swap(halves, small):
    n = len(halves)

    def body(*refs):
        ins, small_ref, outs, red_ref = refs[:n], refs[n], refs[n + 1:2 * n + 1], refs[2 * n + 1]
        send_sems, recv_sems = refs[2 * n + 2:2 * n + 4]
        reduce = _SmallAllReduce(small_ref, red_ref, *refs[2 * n + 4:])
        x, y, c = _mesh_pos()
        copies = [pltpu.make_async_remote_copy(
            src_ref=ins[a], dst_ref=outs[a], send_sem=send_sems.at[a], recv_sem=recv_sems.at[a],
            device_id=(x, y, 1 - c), device_id_type=MESH) for a in range(n)]
        reduce.start()
        for cp in copies:
            cp.start()
        reduce.between_chips()
        reduce.finish()
        for cp in copies:
            cp.wait()

    any_spec = pl.BlockSpec(memory_space=pl.ANY)
    vm = pl.BlockSpec(memory_space=pltpu.VMEM)
    outs = pl.pallas_call(
        body, name="rs_swap",
        in_specs=[any_spec] * n + [vm], out_specs=[any_spec] * n + [vm],
        out_shape=[jax.ShapeDtypeStruct(h.shape, h.dtype) for h in halves]
        + [jax.ShapeDtypeStruct(small.shape, F32)],
        scratch_shapes=[pltpu.SemaphoreType.DMA((n,)), pltpu.SemaphoreType.DMA((n,))]
        + _SmallAllReduce.scratch(*small.shape),
    )(*halves, small)
    return outs[:n], outs[n]


class _SmallAllReduce:
    def __init__(self, x_ref, out_ref, sib_ref, part_ref, peers_ref, send_sems, recv_sems):
        self.x_ref, self.out_ref, self.sib_ref, self.part_ref, self.peers_ref = x_ref, out_ref, sib_ref, part_ref, peers_ref
        x, y, c = _mesh_pos()
        self.me = 2 * x + y
        self.swap = pltpu.make_async_remote_copy(
            src_ref=x_ref, dst_ref=sib_ref, send_sem=send_sems.at[0], recv_sem=recv_sems.at[0],
            device_id=(x, y, 1 - c), device_id_type=MESH)
        self.sends, self.recvs = [], []
        for k, (px, py) in enumerate([(1 - x, y), (x, 1 - y), (1 - x, 1 - y)]):
            sems = dict(send_sem=send_sems.at[1 + k], recv_sem=recv_sems.at[1 + k],
                        device_id=(px, py, c), device_id_type=MESH)
            self.sends.append(pltpu.make_async_remote_copy(src_ref=part_ref, dst_ref=peers_ref.at[self.me], **sems))
            landed = peers_ref.at[2 * px + py]
            self.recvs.append(pltpu.make_async_remote_copy(src_ref=landed, dst_ref=landed, **sems))

    @staticmethod
    def scratch(rows, d):
        return [pltpu.VMEM((rows, d), F32), pltpu.VMEM((rows, d), F32), pltpu.VMEM((N_CHIPS, rows, d), F32),
                pltpu.SemaphoreType.DMA((4,)), pltpu.SemaphoreType.DMA((4,))]

    def start(self):
        self.swap.start()

    def between_chips(self):
        self.swap.wait()
        self.part_ref[...] = self.x_ref[...] + self.sib_ref[...]
        self.peers_ref[self.me] = self.part_ref[...]
        for cp in self.sends:
            cp.start()

    def finish(self):
        for cp in self.recvs:
            cp.wait_recv()
        for cp in self.sends:
            cp.wait_send()
        p = self.peers_ref
        self.out_ref[...] = ((p[0] + p[1]) + p[2]) + p[3]


def _sum_chips(parts, cw, pos, name):
    s, _, h, _ = parts[0][0].shape
    hb = min(h, ROW_BLOCK)
    widths = [own.shape[3] // cw for own, _ in parts]
    starts = [sum(widths[:a]) for a in range(len(parts))]

    def body(pos_ref, *refs):
        out_ref = refs[-1]
        n = pl.program_id(2)
        total = None
        for a in range(len(parts)):
            own, l1, l2, l3 = refs[4 * a:4 * a + 4]
            val = ((own[...] + l1[...].astype(F32)) + l2[...].astype(F32)) + l3[...].astype(F32)
            total = val if total is None else jnp.where(n >= starts[a], val, total)
        out_ref[...] = total

    def slot(a, k):
        col = lambda n: jnp.clip(n - starts[a], 0, widths[a] - 1)
        return pl.BlockSpec((None, None, hb, cw),
                            lambda si, b, n, pos_ref: (si, (pos_ref[1] + k) % N_CHIPS, b, col(n)))

    operands, specs = [], []
    for a, (own, landed) in enumerate(parts):
        operands += [own, landed, landed, landed]
        specs += [slot(a, 0), slot(a, 1), slot(a, 2), slot(a, 3)]
    return pl.pallas_call(
        body, name=name,
        grid_spec=pltpu.PrefetchScalarGridSpec(
            num_scalar_prefetch=1, grid=(s, h // hb, sum(widths)), in_specs=specs,
            out_specs=pl.BlockSpec((None, hb, cw), lambda si, b, n, pos_ref: (si, b, n))),
        out_shape=jax.ShapeDtypeStruct((s, h, sum(widths) * cw), F32),
        compiler_params=_params(("arbitrary",) * 3),
    )(pos, *operands)


def _adamw(w, g, m, v):
    m = ADAM_B1 * m + (1.0 - ADAM_B1) * g
    v = ADAM_B2 * v + (1.0 - ADAM_B2) * (g * g)
    m_hat = m / (1.0 - ADAM_B1 ** ADAM_STEP)
    v_hat = v / (1.0 - ADAM_B2 ** ADAM_STEP)
    delta = -ADAM_LR * (m_hat / (jnp.sqrt(v_hat) + ADAM_EPS) + ADAM_WD * w)
    return delta, m, v


def _adam_halves(ws, ms, vs, g_own, g_recv, pos, name):
    n = len(ws)
    _, r, c = ws[0].shape
    h = r // 2
    rb = min(h, ROW_BLOCK)
    nb = h // rb

    def body(pos_ref, *refs):
        w_refs, m_refs, v_refs = refs[:n], refs[n:2 * n], refs[2 * n:3 * n]
        go_ref, gr_ref = refs[3 * n:3 * n + 2]
        outs = refs[3 * n + 2:]
        mine = pl.program_id(0) == pos_ref[0]
        for a in range(n):
            g = jnp.where(mine, go_ref[a], gr_ref[a])
            delta, m, v = _adamw(w_refs[a][...], g, m_refs[a][...], v_refs[a][...])
            outs[4 * a][...], outs[4 * a + 1][...], outs[4 * a + 2][...], outs[4 * a + 3][...] = g, delta, m, v

    spec_w = pl.BlockSpec((None, rb, c), lambda hf, b, pos_ref: (0, hf * nb + b, 0))
    spec_g = pl.BlockSpec((n, rb, c), lambda hf, b, pos_ref: (0, b, 0))
    return pl.pallas_call(
        body, name=name,
        grid_spec=pltpu.PrefetchScalarGridSpec(
            num_scalar_prefetch=1, grid=(2, nb), in_specs=[spec_w] * (3 * n) + [spec_g] * 2,
            out_specs=[spec_w] * (4 * n)),
        out_shape=[jax.ShapeDtypeStruct((1, r, c), F32)] * (4 * n),
        compiler_params=_params(("arbitrary",) * 2),
    )(pos, *ws, *ms, *vs, g_own, g_recv)


def _adam_small(items):
    n = len(items)

    def body(*refs):
        ins, outs = refs[:4 * n], refs[4 * n:]
        for a in range(n):
            w_ref, g_ref, m_ref, v_ref = ins[4 * a:4 * a + 4]
            d, m, v = _adamw(w_ref[...], g_ref[...], m_ref[...], v_ref[...])
            outs[3 * a][...] = d
            outs[3 * a + 1][...] = m
            outs[3 * a + 2][...] = v

    vm = pl.BlockSpec(memory_space=pltpu.VMEM)
    flat = [t for it in items for t in it]
    outs = pl.pallas_call(
        body, name="adam_small", in_specs=[vm] * (4 * n), out_specs=[vm] * (3 * n),
        out_shape=[jax.ShapeDtypeStruct(it[0].shape, F32) for it in items for _ in range(3)],
    )(*flat)
    return [tuple(outs[3 * a:3 * a + 3]) for a in range(n)]


def _big_row_spec(seq, tmb, d, tile_of):
    return pl.BlockSpec((pl.Element(tmb), pl.Element(d)),
                        lambda *args: (pl.multiple_of(jnp.minimum(tile_of(*args) * tmb, seq - tmb), 8), 0))


def _big_row_tile(x_ref, front_ref, i):
    rows = x_ref[...]
    last = jnp.concatenate([rows[TM:], front_ref[...]], axis=0)
    return jnp.where(i == N_ROW_TILES_BIG - 1, last, rows)


def _h_prep(x, meta, norm_g, shards, pos):
    seq, d = x.shape
    tp = seq + TM
    dc = meta.shape[1]
    tmb = tp // N_ROW_TILES_BIG
    assert tmb >= TM and tp == tmb * N_ROW_TILES_BIG
    last = N_ROW_TILES_BIG - 1
    ns = len(shards)

    def body(pos_ref, x_ref, meta_ref, g_ref, *refs):
        shard_refs, (h_ref, front_ref), placed_refs = refs[:ns], refs[ns:ns + 2], refs[ns + 2:2 * ns + 2]
        metas, msend, mrecv = refs[2 * ns + 2:]
        for a in range(ns):
            placed_refs[a][...] = shard_refs[a][...].astype(placed_refs[a].dtype)
        x, y, c = _mesh_pos()
        me = 2 * x + y
        chips = [(1 - x, y), (x, 1 - y), (1 - x, 1 - y)]
        i = pl.program_id(0)

        def meta_copy(k, chip):
            return pltpu.make_async_remote_copy(
                src_ref=metas.at[chip], dst_ref=metas.at[chip], send_sem=msend.at[k], recv_sem=mrecv.at[k],
                device_id=(*chips[k], c), device_id_type=MESH)

        @pl.when(i == 0)
        def _():
            metas[me] = meta_ref[...]
            for k in range(3):
                meta_copy(k, me).start()
            front_ref[...] = jnp.zeros_like(front_ref)

        @pl.when(i == last)
        def _():
            for k, (px, py) in enumerate(chips):
                meta_copy(k, 2 * px + py).wait_recv()
            for q in range(N_CHIPS):
                front_ref[TM - N_META:TM, q * dc:(q + 1) * dc] = metas[q]

        s = _big_row_tile(x_ref, front_ref, i)
        r = lax.rsqrt(jnp.mean(s * s, axis=-1, keepdims=True) + EPS)
        h_ref[...] = (s * r * g_ref[...]).astype(BF16)

        @pl.when(i == last)
        def _():
            for k in range(3):
                meta_copy(k, me).wait_send()

    shard_in, shard_out, shard_shapes = [], [], []
    for arr, dtype in shards:
        s, r, c = arr.shape
        sliced = r % (N_ROW_TILES_BIG * 16) == 0
        rp = r // N_ROW_TILES_BIG if sliced else r
        step = (lambda i: i) if sliced else (lambda i: 0)
        shard_in.append(pl.BlockSpec((s, rp, c), lambda i, pos_ref, step=step: (0, step(i), 0)))
        shard_out.append(pl.BlockSpec((s, None, rp, c), lambda i, pos_ref, step=step: (0, pos_ref[1], step(i), 0)))
        shard_shapes.append(jax.ShapeDtypeStruct((s, N_CHIPS, r, c), dtype))
    outs = pl.pallas_call(
        body, name="f0_norm",
        grid_spec=pltpu.PrefetchScalarGridSpec(
            num_scalar_prefetch=1, grid=(N_ROW_TILES_BIG,),
            in_specs=[_big_row_spec(seq, tmb, d, lambda i, pos_ref: i),
                      pl.BlockSpec(meta.shape, lambda i, pos_ref: (0, 0)),
                      pl.BlockSpec((1, d), lambda i, pos_ref: (0, 0))] + shard_in,
            out_specs=[pl.BlockSpec((tmb, d), lambda i, pos_ref: (i, 0)),
                       pl.BlockSpec((TM, d), lambda i, pos_ref: (0, 0))] + shard_out,
            scratch_shapes=[pltpu.VMEM((N_CHIPS,) + meta.shape, F32),
                            pltpu.SemaphoreType.DMA((3,)), pltpu.SemaphoreType.DMA((3,))]),
        out_shape=[jax.ShapeDtypeStruct((tp, d), BF16), jax.ShapeDtypeStruct((TM, d), F32)] + shard_shapes,
        compiler_params=_params(("arbitrary",)),
    )(pos, x, meta, norm_g, *[arr for arr, _ in shards])
    return outs[0], outs[1], outs[2:]


N_UNITS = 3
N_STEPS_PROJ = N_CHIPS * N_UNITS


def _proj_plan(u):
    v = u - N_UNITS
    if v < 2 * N_UNITS:
        return v % 2, v // 2
    return 2, v - 2 * N_UNITS


def _proj_unit(t, me):
    v = t - N_UNITS
    near = v < 2 * N_UNITS
    rel = jnp.where(near, v % 2, 2)
    unit = jnp.where(near, v // 2, v - 2 * N_UNITS)
    flip = jnp.where(rel == 0, 2, jnp.where(rel == 1, 1, 3))
    own = t < N_UNITS
    return jnp.where(own, me, lax.bitwise_xor(me, flip)), jnp.where(own, t, unit)


def _proj_fwd(h, bufs, pos):
    tp, d = h.shape
    _, nsh, _, sw = bufs[0].shape
    cu = sw // N_UNITS
    assert cu % 128 == 0
    n = len(bufs)
    w_sems = 6 * N_UNITS
    last = N_STEPS_PROJ - 1
    late = N_STEPS_PROJ - N_UNITS

    def body(pos_ref, h_ref, *refs):
        proj_ref = refs[n]
        gbufs = refs[n + 1:2 * n + 1]
        wbuf, wsems, send_sems, recv_sems = refs[2 * n + 1:]
        x, y, c = _mesh_pos()
        me = 2 * x + y
        sibling = (x, y, 1 - c)
        chips = [(1 - x, y), (x, 1 - y), (1 - x, 1 - y)]
        chip_ids = [2 * px + py for px, py in chips]
        relayed_chip = jnp.where(c == 0, chip_ids[0], chip_ids[1])
        relay_to = (jnp.where(c == 0, x, 1 - x), jnp.where(c == 0, 1 - y, y), c)
        t = pl.program_id(0)

        def remote(idx, piece, to):
            return pltpu.make_async_remote_copy(
                src_ref=piece, dst_ref=piece, send_sem=send_sems.at[idx], recv_sem=recv_sems.at[idx],
                device_id=to, device_id_type=MESH)

        hr = d // 2

        def chunk_of(chip, half, k):
            return gbufs[0].at[0, chip, pl.ds(half * hr, hr), pl.ds(k * cu, cu)]

        def own_chunk(r, k):
            return remote(6 * k + r, chunk_of(me, c, k), (*chips[r], c))

        def landed_chunk(r, k):
            return remote(6 * k + r, chunk_of(chip_ids[r], c, k), (*chips[r], c))

        def relay_chunk(k):
            return remote(6 * k + 2, chunk_of(relayed_chip, c, k), relay_to)

        def sibling_chunk(r, k, half):
            return remote(6 * k + 3 + r, chunk_of(chip_ids[r], half, k), sibling)

        def fetch(u):
            chip, unit = _proj_unit(jnp.int32(u), me)
            return pltpu.make_async_copy(gbufs[0].at[0, chip, :, pl.ds(pl.multiple_of(unit * cu, 128), cu)],
                                         wbuf.at[u % 2], wsems.at[u % 2])

        def make_available(u):
            r, k = _proj_plan(u)
            landed_chunk(r, k).wait_recv()
            if r < 2:
                pl.when(c == r)(lambda: relay_chunk(k).start())
            sibling_chunk(r, k, c).start()
            sibling_chunk(r, k, 1 - c).wait_recv()

        def own_piece(a, r):
            return remote(w_sems + 6 * (a - 1) + r, _half(gbufs[a], me, c), (*chips[r], c))

        def relay(a):
            return remote(w_sems + 6 * (a - 1) + 2, _half(gbufs[a], relayed_chip, c), relay_to)

        def to_sibling(a, r, core):
            return remote(w_sems + 6 * (a - 1) + 3 + r, _half(gbufs[a], chip_ids[r], core), sibling)

        def landed(a, r):
            return remote(w_sems + 6 * (a - 1) + r, _half(gbufs[a], chip_ids[r], c), (*chips[r], c))

        for u in range(N_STEPS_PROJ):
            @pl.when(t == u)
            def _(u=u):
                if u == 0:
                    for k in range(N_UNITS):
                        for r in range(2):
                            own_chunk(r, k).start()
                    for a in range(1, n):
                        for r in range(2):
                            own_piece(a, r).start()
                    fetch(0).start()
                if u < last:
                    if u + 1 >= N_UNITS:
                        make_available(u + 1)
                    fetch(u + 1).start()
                if u == late:
                    for a in range(1, n):
                        landed(a, 0).wait_recv()
                        landed(a, 1).wait_recv()
                        relay(a).start()
                        for r in range(2):
                            to_sibling(a, r, c).start()
                        for r in range(2):
                            to_sibling(a, r, 1 - c).wait_recv()
                fetch(u).wait()

        proj_ref[...] = jnp.dot(h_ref[...], wbuf[t % 2], preferred_element_type=F32).astype(BF16)

        @pl.when(t == last)
        def _():
            for a in range(1, n):
                landed(a, 2).wait_recv()
                to_sibling(a, 2, c).start()
                to_sibling(a, 2, 1 - c).wait_recv()
            for k in range(N_UNITS):
                for r in range(2):
                    own_chunk(r, k).wait_send()
                relay_chunk(k).wait_send()
                for r in range(3):
                    sibling_chunk(r, k, c).wait_send()
            for a in range(1, n):
                for r in range(2):
                    own_piece(a, r).wait_send()
                relay(a).wait_send()
                for r in range(3):
                    to_sibling(a, r, c).wait_send()

    def out_index(t, pos_ref):
        chip, unit = _proj_unit(t, pos_ref[1])
        return 0, chip * N_UNITS + unit

    any_spec = pl.BlockSpec(memory_space=pl.ANY)
    outs = pl.pallas_call(
        body, name="f1_proj",
        grid_spec=pltpu.PrefetchScalarGridSpec(
            num_scalar_prefetch=1, grid=(N_STEPS_PROJ,),
            in_specs=[pl.BlockSpec((tp, d), lambda t, pos_ref: (0, 0))] + [any_spec] * n,
            out_specs=[pl.BlockSpec((tp, cu), out_index)] + [any_spec] * n,
            scratch_shapes=[pltpu.VMEM((2, d, cu), BF16), pltpu.SemaphoreType.DMA((2,)),
                            pltpu.SemaphoreType.DMA((w_sems + 6 * (n - 1),)),
                            pltpu.SemaphoreType.DMA((w_sems + 6 * (n - 1),))]),
        out_shape=[jax.ShapeDtypeStruct((tp, nsh * sw), BF16)]
        + [jax.ShapeDtypeStruct(b.shape, b.dtype) for b in bufs],
        input_output_aliases={2 + a: 1 + a for a in range(n)},
        compiler_params=_params(("arbitrary",)),
    )(pos, h, *bufs)
    return outs[0], outs[1:]


def _dh_bwd(dproj, wg_in, x, front, ds2, norm_g, part):
    seq, d = x.shape
    tp = seq + TM
    _, nsh, _, sw = wg_in.shape
    tmb = tp // N_ROW_TILES_BIG
    tail = tmb - TM
    last = N_ROW_TILES_BIG - 1

    def body(dp_ref, w_hbm, x_ref, front_ref, ds2_ref, g_ref, part_ref, gx_hbm, dfront_ref, dng_ref, land_ref,
             wbuf, gacc, dsbuf, wsem, osems, send_sems, recv_sems):
        exchange = _chip_exchange(part_ref, land_ref, send_sems, recv_sems)
        i = pl.program_id(0)

        def x_rows_out(step):
            return pltpu.make_async_copy(dsbuf.at[step % 2], gx_hbm.at[pl.ds(step * tmb, tmb), :], osems.at[step % 2])

        last_out = pltpu.make_async_copy(dsbuf.at[last % 2, pl.ds(0, tail), :],
                                         gx_hbm.at[pl.ds(last * tmb, tail), :], osems.at[last % 2])

        @pl.when(i == 0)
        def _():
            exchange.start()
            gacc[...] = jnp.zeros_like(gacc)
            whole = pltpu.make_async_copy(w_hbm.at[0], wbuf, wsem)
            whole.start()
            whole.wait()

        dh = None
        for j in range(nsh):
            part = lax.dot_general(dp_ref[:, j * sw:(j + 1) * sw], wbuf[j], (((1,), (1,)), ((), ())),
                                   preferred_element_type=F32)
            dh = part if dh is None else dh + part
        s = _big_row_tile(x_ref, front_ref, i)
        r = lax.rsqrt(jnp.mean(s * s, axis=-1, keepdims=True) + EPS)
        gacc[...] += (dh * s * r).reshape(tmb // 8, 8, d).sum(axis=0)
        t = dh * g_ref[...]

        @pl.when(i >= 2)
        def _():
            x_rows_out(i - 2).wait()

        dsbuf[i % 2] = ds2_ref[...] + r * t - s * (r * r * r) * jnp.mean(t * s, axis=-1, keepdims=True)

        @pl.when(i < last)
        def _():
            x_rows_out(i).start()

        @pl.when(i == last)
        def _():
            last_out.start()
            dfront_ref[...] = dsbuf[last % 2, tail:, :]
            dng_ref[...] = jnp.broadcast_to(jnp.sum(gacc[...], axis=0, keepdims=True), (8, d))
            exchange.finish()
            x_rows_out(last - 1).wait()
            last_out.wait()

    any_spec = pl.BlockSpec(memory_space=pl.ANY)
    return pl.pallas_call(
        body, name="b2_dh", grid=(N_ROW_TILES_BIG,),
        in_specs=[pl.BlockSpec((tmb, nsh * sw), lambda i: (i, 0)), any_spec,
                  _big_row_spec(seq, tmb, d, lambda i: i),
                  pl.BlockSpec((TM, d), lambda i: (0, 0)),
                  pl.BlockSpec((tmb, d), lambda i: (i, 0)),
                  pl.BlockSpec((1, d), lambda i: (0, 0)), any_spec],
        out_specs=[any_spec, pl.BlockSpec((TM, d), lambda i: (0, 0)),
                   pl.BlockSpec((8, d), lambda i: (0, 0)), any_spec],
        out_shape=[jax.ShapeDtypeStruct((seq, d), F32), jax.ShapeDtypeStruct((TM, d), F32),
                   jax.ShapeDtypeStruct((8, d), F32), jax.ShapeDtypeStruct(part.shape, part.dtype)],
        scratch_shapes=[pltpu.VMEM((nsh, d, sw), BF16), pltpu.VMEM((8, d), F32), pltpu.VMEM((2, tmb, d), F32),
                        pltpu.SemaphoreType.DMA, pltpu.SemaphoreType.DMA((2,)),
                        pltpu.SemaphoreType.DMA((3,)), pltpu.SemaphoreType.DMA((3,))],
        compiler_params=pltpu.CompilerParams(dimension_semantics=("arbitrary",),
                                             vmem_limit_bytes=VMEM_LIMIT_BIG),
    )(dproj, wg_in, x, front, ds2, norm_g, part)


def _col_block(width, cap):
    return max(b for b in range(128, cap + 1, 128) if width % b == 0)


def _dw_reduced(lhs_t, rhs, cw, nblk, operands, groups, out_dims, out_block, out_index, carried, name):
    na, d, tp = lhs_t.shape
    rg = d // groups
    hh = rg // 2
    nc = len(carried)

    def body(*refs):
        l_ref, r_ref = refs[:2]
        part_refs = refs[2:2 + nc]
        p32_ref, pbf_ref = refs[2 + nc:4 + nc]
        land_refs = refs[4 + nc:4 + 2 * nc]
        res, rbuf, send_sems, recv_sems = refs[4 + 2 * nc:8 + 2 * nc]
        xsems = refs[8 + 2 * nc:]
        exchanges = [_chip_exchange(part_refs[e], land_refs[e], xsems[2 * e], xsems[2 * e + 1]) for e in range(nc)]
        exchange = _Exchange([s for ex in exchanges for s in ex.sends], [r for ex in exchanges for r in ex.recvs])
        x, y, c = _mesh_pos()
        t = pl.program_id(0)
        u = jnp.maximum(t - 1, 0)

        def to_sibling(blk):
            return pltpu.make_async_remote_copy(
                src_ref=res.at[blk % 2, :, pl.ds((1 - c) * hh, hh), :], dst_ref=rbuf.at[blk % 2],
                send_sem=send_sems.at[blk], recv_sem=recv_sems.at[blk],
                device_id=(x, y, 1 - c), device_id_type=MESH)

        @pl.when(t == 0)
        def _():
            exchange.start()

        @pl.when(t < nblk)
        def _():
            res[t % 2] = jnp.dot(l_ref[...], r_ref[...], preferred_element_type=F32).reshape(groups, rg, cw)

        @pl.when(t >= 1)
        def _():
            to_sibling(u).wait_recv()
            p = res[u % 2, :, pl.ds(c * hh, hh), :] + rbuf[u % 2]
            p32_ref[...] = p.reshape(p32_ref.shape)
            pbf_ref[...] = p.reshape(pbf_ref.shape).astype(BF16)

        @pl.when(t < nblk)
        def _():
            to_sibling(t).start()

        @pl.when(t >= 1)
        def _():
            to_sibling(u).wait_send()

        @pl.when(t == nblk)
        def _():
            exchange.finish()

    any_spec = pl.BlockSpec(memory_space=pl.ANY)
    last = nblk - 1
    out_spec = pl.BlockSpec(out_block, lambda t: out_index(jnp.maximum(t - 1, 0)))
    outs = pl.pallas_call(
        body, name=name, grid=(nblk + 1,),
        in_specs=[pl.BlockSpec((None, d, tp), lambda t: (operands(jnp.minimum(t, last))[0], 0, 0)),
                  pl.BlockSpec((None, tp, cw), lambda t: (operands(jnp.minimum(t, last))[0], 0,
                                                          operands(jnp.minimum(t, last))[1]))]
        + [any_spec] * nc,
        out_specs=[out_spec, out_spec] + [any_spec] * nc,
        out_shape=[jax.ShapeDtypeStruct(out_dims, F32), jax.ShapeDtypeStruct(out_dims, BF16)]
        + [jax.ShapeDtypeStruct(e.shape, e.dtype) for e in carried],
        scratch_shapes=[pltpu.VMEM((2, groups, rg, cw), F32), pltpu.VMEM((2, groups, hh, cw), F32),
                        pltpu.SemaphoreType.DMA((nblk,)), pltpu.SemaphoreType.DMA((nblk,))]
        + [pltpu.SemaphoreType.DMA((3,)), pltpu.SemaphoreType.DMA((3,))] * nc,
        compiler_params=_params(("arbitrary",)),
    )(lhs_t, rhs, *carried)
    return outs[0], outs[1], outs[2:]


def _conv_a_taps(first_lag, last_lag):
    out = []
    for r in range(8):
        taps = [(q, 8 * q + r) for q in range(5) if first_lag <= 8 * q + r <= last_lag]
        if taps:
            out.append((r, taps))
    return out


def _tile_block(i, nt):
    return jnp.where(i == 0, nt - 1, i - 1)


def _mix_fwd(x, front, proj, target, w3, wa, wb, conv_a_b, ln_g, ln_b, b_a_out, final_g, norm_g):
    seq, d = x.shape
    tp = seq + TM
    nt = tp // TM
    RB = RB_FWD
    nrb = TM // RB
    shl = TM + SHIFT_ROWS

    def body(x_ref, front_ref, proj_ref, tgt_ref, w3_ref, wa_ref, wb_ref, cab_ref, lng_ref, lnb_ref, bao_ref, fg_ref,
             ng_ref, ca_ref, cb_ref, ya_ref, yb_ref, abmt_ref, ds2_ref, ht_ref, loss_ref, dfg_ref,
             abm_ref, ext_a, ext_b, sh, s2_s, lacc, gacc):
        i = pl.program_id(0)

        def split(k, rows):
            return proj_ref[rows, k * d:(k + 1) * d].astype(F32)

        def s_tile():
            return jnp.where(i == 0, front_ref[...], x_ref[...])

        s_in = s_tile()
        h = s_in * lax.rsqrt(jnp.mean(s_in * s_in, axis=-1, keepdims=True) + EPS) * ng_ref[...]
        ht_ref[...] = h.astype(BF16).T

        @pl.when(i == 0)
        def _():
            ext_a[0:HALO_A, :] = jnp.zeros((HALO_A, d), F32)
            ext_b[0:HALO_B, :] = jnp.zeros((HALO_B, d), F32)
            lacc[...] = jnp.zeros_like(lacc)
            gacc[...] = jnp.zeros_like(gacc)

        def conv_in(rb, carry):
            rows = _rows(rb, RB)
            ua0 = split(0, rows) * _sigmoid(split(1, rows))
            ext_a[pl.ds(pl.multiple_of(HALO_A + rb * RB, 8), RB), :] = ua0
            ext_b[pl.ds(pl.multiple_of(HALO_B + rb * RB, 8), RB), :] = split(4, rows) * split(5, rows)
            ca_ref[rows, :] = jnp.broadcast_to(cab_ref[...], (RB, d))
            return carry
        lax.fori_loop(0, nrb, conv_in, 0)

        @pl.when(i == 0)
        def _():
            abmt_ref[...] = jnp.zeros_like(abmt_ref)
            ds2_ref[...] = jnp.zeros_like(ds2_ref)

        @pl.when(i > 0)
        def _():
            tile_after_conv_inputs(split, s_tile, tgt_ref, w3_ref, wa_ref, wb_ref, lng_ref, lnb_ref, bao_ref, fg_ref,
                                   ca_ref, cb_ref, ya_ref, yb_ref, abmt_ref, ds2_ref, abm_ref, ext_a, ext_b, sh,
                                   s2_s, lacc, gacc)

        ext_a[0:HALO_A, :] = ext_a[TM:TM + HALO_A, :]
        ext_b[0:HALO_B, :] = ext_b[TM:TM + HALO_B, :]

        @pl.when(i == nt - 1)
        def _():
            loss_ref[...] = jnp.broadcast_to(0.5 * jnp.sum(lacc[...]) * (1.0 / d), (8, 128))
            dfg_ref[...] = jnp.broadcast_to(jnp.sum(gacc[...], axis=0, keepdims=True), (8, d))

    def tile_after_conv_inputs(split, s_tile, tgt_ref, w3_ref, wa_ref, wb_ref, lng_ref, lnb_ref, bao_ref, fg_ref,
                               ca_ref, cb_ref, ya_ref, yb_ref, abmt_ref, ds2_ref, abm_ref, ext_a, ext_b, sh, s2_s,
                               lacc, gacc):
        for r, taps in _conv_a_taps(HALO_A - CONV_A + 1, HALO_A):
            if r == 0:
                src = ext_a
            else:
                sh[...] = ext_a[r:r + shl, :]
                src = sh

            def conv_acc(rb, carry, src=src, taps=taps):
                rows = _rows(rb, RB)
                acc = ca_ref[rows, :]
                for q, lag in taps:
                    k = lag - (HALO_A - CONV_A + 1)
                    acc = acc + src[pl.ds(pl.multiple_of(rb * RB + 8 * q, 8), RB), :] * wa_ref[k:k + 1, :]
                ca_ref[rows, :] = acc
                return carry
            lax.fori_loop(0, nrb, conv_acc, 0)

        cb_ref[...] = ext_b[HALO_B:HALO_B + TM, :] * wb_ref[2:3, :]
        for k in range(CONV_B - 1):
            off = HALO_B - CONV_B + 1 + k
            sh[0:TM, :] = ext_b[off:off + TM, :]
            cb_ref[...] += sh[0:TM, :] * wb_ref[k:k + 1, :]

        def branches(rb, carry):
            rows = _rows(rb, RB)
            ca = ca_ref[rows, :]
            mu = jnp.mean(ca, axis=-1, keepdims=True)
            xc = ca - mu
            rstd = lax.rsqrt(jnp.mean(xc * xc, axis=-1, keepdims=True) + EPS)
            ln = xc * rstd * lng_ref[...] + lnb_ref[...]
            ua = ln * _sigmoid(ln)
            a_z = split(2, rows)
            abm_ref[0, rows, :] = (ua * (a_z * _sigmoid(a_z))).astype(BF16)
            b_z = split(6, rows)
            ub = split(3, rows) * cb_ref[rows, :]
            abm_ref[1, rows, :] = (ub * (b_z * _sigmoid(b_z))).astype(BF16)
            return carry
        lax.fori_loop(0, nrb, branches, 0)

        ya_ref[...] = jnp.dot(abm_ref[0], w3_ref[0], preferred_element_type=F32) + bao_ref[...]
        yb_ref[...] = jnp.dot(abm_ref[1], w3_ref[1], preferred_element_type=F32)

        def merge(rb, carry):
            rows = _rows(rb, RB)
            m = _sigmoid(split(7, rows)) * ya_ref[rows, :] + _sigmoid(split(8, rows)) * yb_ref[rows, :]
            abm_ref[2, rows, :] = m.astype(BF16)
            return carry
        lax.fori_loop(0, nrb, merge, 0)

        s2_s[...] = s_tile() + jnp.dot(abm_ref[2], w3_ref[2], preferred_element_type=F32)
        for k in range(3):
            abmt_ref[k] = abm_ref[k].T

        def head(rb, carry):
            rows = _rows(rb, RB)
            s2 = s2_s[rows, :]
            r2 = lax.rsqrt(jnp.mean(s2 * s2, axis=-1, keepdims=True) + EPS)
            diff = s2 * r2 * fg_ref[...] - tgt_ref[rows, :]
            lacc[...] += diff * diff
            dy = diff * (1.0 / d)
            gacc[...] += (dy * s2 * r2).reshape(RB // 8, 8, d).sum(axis=0)
            t = dy * fg_ref[...]
            ds2_ref[rows, :] = r2 * t - s2 * (r2 * r2 * r2) * jnp.mean(t * s2, axis=-1, keepdims=True)
            return carry
        lax.fori_loop(0, nrb, head, 0)

    row_f32 = pl.BlockSpec((TM, d), lambda i: (_tile_block(i, nt), 0))
    x_rows = pl.BlockSpec((TM, d), lambda i: (jnp.maximum(i - 1, 0), 0))
    const = lambda shape: pl.BlockSpec(shape, lambda i: (0,) * len(shape))
    return pl.pallas_call(
        body, name="f2_mix", grid=(nt,),
        in_specs=[x_rows, const((TM, d)),
                  pl.BlockSpec((TM, N_SPLIT * d), lambda i: (_tile_block(i, nt), 0)),
                  x_rows,
                  const((3, d, d)), const(wa.shape), const(wb.shape)] + [const((1, d))] * 6,
        out_specs=[row_f32, row_f32, row_f32, row_f32,
                   pl.BlockSpec((3, d, TM), lambda i: (0, 0, _tile_block(i, nt))),
                   row_f32, pl.BlockSpec((d, TM), lambda i: (0, _tile_block(i, nt))),
                   const((8, 128)), const((8, d))],
        out_shape=[jax.ShapeDtypeStruct((tp, d), F32)] * 4
        + [jax.ShapeDtypeStruct((3, d, tp), BF16), jax.ShapeDtypeStruct((tp, d), F32),
           jax.ShapeDtypeStruct((d, tp), BF16),
           jax.ShapeDtypeStruct((8, 128), F32), jax.ShapeDtypeStruct((8, d), F32)],
        scratch_shapes=[pltpu.VMEM((3, TM, d), BF16),
                        pltpu.VMEM((HALO_A + TM, d), F32), pltpu.VMEM((HALO_B + TM, d), F32),
                        pltpu.VMEM((shl, d), F32), pltpu.VMEM((TM, d), F32),
                        pltpu.VMEM((RB, d), F32), pltpu.VMEM((8, d), F32)],
        compiler_params=_params(("arbitrary",)),
    )(x, front, proj, target, w3, wa, wb, conv_a_b, ln_g, ln_b, b_a_out, final_g, norm_g)


def _mix_bwd(ds2, proj, ca, cb, ya, yb, w3, wa, wb, ln_g, ln_b):
    tp, d = ds2.shape
    nt = tp // TM
    RB = RB_BWD
    nrb = TM // RB
    shl = TM + SHIFT_ROWS
    nt_dims = (((1,), (1,)), ((), ()))

    def body(ds2_ref, proj_ref, ca_ref, cb_ref, ya_ref, yb_ref, w3_ref, wa_ref, wb_ref, lng_ref, lnb_ref,
             dproj_ref, d3_ref, sm_ref, ext_d, ext_e, sh, dm_s, dpa_s, dpb_s, dua0_s, acc):
        step = pl.program_id(0)

        def split(k, rows):
            return proj_ref[rows, k * d:(k + 1) * d].astype(F32)

        def put(k, rows, val):
            dproj_ref[rows, k * d:(k + 1) * d] = val.astype(BF16)

        def accum(row, val):
            acc[row] += val.reshape(RB // 8, 8, d).sum(axis=0)

        @pl.when(step == 0)
        def _():
            ext_d[TM:TM + HALO_A, :] = jnp.zeros((HALO_A, d), F32)
            ext_e[TM:TM + HALO_B, :] = jnp.zeros((HALO_B, d), F32)
            acc[...] = jnp.zeros_like(acc)

        front = step == nt - 1

        @pl.when(front)
        def _():
            d3_ref[...] = jnp.zeros_like(d3_ref)

            def conv_only(rb, carry):
                rows = _rows(rb, RB)
                zeros = jnp.zeros((RB, d), F32)
                for k in (2, 3, 6, 7, 8):
                    put(k, rows, zeros)
                ext_d[rows, :] = zeros
                ext_e[rows, :] = zeros
                dua0_s[rows, :] = zeros
                dm_s[rows, :] = split(0, rows) * _sigmoid(split(1, rows))
                return carry
            lax.fori_loop(0, nrb, conv_only, 0)

        @pl.when(jnp.logical_not(front))
        def _():
            tile_to_conv_outputs(split, put, accum, ds2_ref, ca_ref, cb_ref, ya_ref, yb_ref, w3_ref, lng_ref, lnb_ref,
                                 d3_ref, ext_d, ext_e, dm_s, dpa_s, dpb_s, dua0_s)

        tile_conv_transposes(split, put, accum, wa_ref, wb_ref, ext_d, ext_e, sh, dm_s, dpb_s, dua0_s)

        @pl.when(front)
        def _():
            for row in range(SM_ROWS):
                sm_ref[row:row + 1, :] = jnp.sum(acc[row], axis=0, keepdims=True)

    def tile_to_conv_outputs(split, put, accum, ds2_ref, ca_ref, cb_ref, ya_ref, yb_ref, w3_ref, lng_ref, lnb_ref,
                             d3_ref, ext_d, ext_e, dm_s, dpa_s, dpb_s, dua0_s):
        d3_ref[2] = ds2_ref[...].astype(BF16)
        dm_s[...] = lax.dot_general(d3_ref[2], w3_ref[2], nt_dims, preferred_element_type=F32)

        def gates(rb, carry):
            rows = _rows(rb, RB)
            dm = dm_s[rows, :]
            sa = _sigmoid(split(7, rows))
            sb = _sigmoid(split(8, rows))
            ya_v = ya_ref[rows, :]
            yb_v = yb_ref[rows, :]
            put(7, rows, dm * ya_v * sa * (1.0 - sa))
            put(8, rows, dm * yb_v * sb * (1.0 - sb))
            dya = dm * sa
            accum(ROW_DBAO, dya)
            d3_ref[0, rows, :] = dya.astype(BF16)
            d3_ref[1, rows, :] = (dm * sb).astype(BF16)
            return carry
        lax.fori_loop(0, nrb, gates, 0)

        dpa_s[...] = lax.dot_general(d3_ref[0], w3_ref[0], nt_dims, preferred_element_type=F32)
        dpb_s[...] = lax.dot_general(d3_ref[1], w3_ref[1], nt_dims, preferred_element_type=F32)

        def branches(rb, carry):
            rows = _rows(rb, RB)
            ca_v = ca_ref[rows, :]
            mu = jnp.mean(ca_v, axis=-1, keepdims=True)
            xc = ca_v - mu
            rstd = lax.rsqrt(jnp.mean(xc * xc, axis=-1, keepdims=True) + EPS)
            xhat = xc * rstd
            ln = xhat * lng_ref[...] + lnb_ref[...]
            sl = _sigmoid(ln)
            ua = ln * sl
            a_z = split(2, rows)
            sz = _sigmoid(a_z)
            dpa = dpa_s[rows, :]
            put(2, rows, dpa * ua * (sz * (1.0 + a_z * (1.0 - sz))))
            dln = dpa * (a_z * sz) * (sl * (1.0 + ln * (1.0 - sl)))
            accum(ROW_DLNG, dln * xhat)
            accum(ROW_DLNB, dln)
            dxh = dln * lng_ref[...]
            dca = rstd * (dxh - jnp.mean(dxh, axis=-1, keepdims=True)
                          - xhat * jnp.mean(dxh * xhat, axis=-1, keepdims=True))
            accum(ROW_DCAB, dca)
            ext_d[rows, :] = dca
            dua0_s[rows, :] = jnp.zeros((RB, d), F32)
            dm_s[rows, :] = split(0, rows) * _sigmoid(split(1, rows))
            b_z = split(6, rows)
            szb = _sigmoid(b_z)
            dpb = dpb_s[rows, :]
            b_b = split(3, rows)
            cb_v = cb_ref[rows, :]
            put(6, rows, dpb * (b_b * cb_v) * (szb * (1.0 + b_z * (1.0 - szb))))
            dub = dpb * (b_z * szb)
            put(3, rows, dub * cb_v)
            ext_e[rows, :] = dub * b_b
            return carry
        lax.fori_loop(0, nrb, branches, 0)

    def tile_conv_transposes(split, put, accum, wa_ref, wb_ref, ext_d, ext_e, sh, dm_s, dpb_s, dua0_s):
        for r, taps in _conv_a_taps(0, CONV_A - 1):
            if r == 0:
                src = ext_d
            else:
                sh[...] = ext_d[r:r + shl, :]
                src = sh

            def conv_t(rb, carry, src=src, taps=taps):
                rows = _rows(rb, RB)
                ua0 = dm_s[rows, :]
                dua0 = dua0_s[rows, :]
                for q, lag in taps:
                    k = CONV_A - 1 - lag
                    slab = src[pl.ds(pl.multiple_of(rb * RB + 8 * q, 8), RB), :]
                    dua0 = dua0 + slab * wa_ref[k:k + 1, :]
                    accum(ROW_DWA + k, slab * ua0)
                dua0_s[rows, :] = dua0
                return carry
            lax.fori_loop(0, nrb, conv_t, 0)
        ext_d[TM:TM + HALO_A, :] = ext_d[0:HALO_A, :]

        dpb_s[...] = ext_e[0:TM, :] * wb_ref[CONV_B - 1:CONV_B, :]
        for lag in range(CONV_B):
            k = CONV_B - 1 - lag
            if lag > 0:
                sh[0:TM, :] = ext_e[lag:lag + TM, :]
                dpb_s[...] += sh[0:TM, :] * wb_ref[k:k + 1, :]
            src = ext_e if lag == 0 else sh

            def conv_b_w(rb, carry, src=src, k=k):
                rows = _rows(rb, RB)
                accum(ROW_DWB + k, src[rows, :] * (split(4, rows) * split(5, rows)))
                return carry
            lax.fori_loop(0, nrb, conv_b_w, 0)
        ext_e[TM:TM + HALO_B, :] = ext_e[0:HALO_B, :]

        def inputs(rb, carry):
            rows = _rows(rb, RB)
            dua0 = dua0_s[rows, :]
            a_val = split(0, rows)
            sg = _sigmoid(split(1, rows))
            put(0, rows, dua0 * sg)
            put(1, rows, dua0 * a_val * sg * (1.0 - sg))
            dcbin = dpb_s[rows, :]
            put(4, rows, dcbin * split(5, rows))
            put(5, rows, dcbin * split(4, rows))
            return carry
        lax.fori_loop(0, nrb, inputs, 0)

    rev = lambda i: (_tile_block(nt - 1 - i, nt), 0)
    row_f32 = pl.BlockSpec((TM, d), rev)
    const = lambda shape: pl.BlockSpec(shape, lambda i: (0,) * len(shape))
    return pl.pallas_call(
        body, name="b1_mix", grid=(nt,),
        in_specs=[row_f32, pl.BlockSpec((TM, N_SPLIT * d), rev), row_f32, row_f32, row_f32, row_f32,
                  const((3, d, d)), const(wa.shape), const(wb.shape), const((1, d)), const((1, d))],
        out_specs=[pl.BlockSpec((TM, N_SPLIT * d), rev),
                   pl.BlockSpec((3, TM, d), lambda i: (0, _tile_block(nt - 1 - i, nt), 0)),
                   const((SM_ROWS, d))],
        out_shape=[jax.ShapeDtypeStruct((tp, N_SPLIT * d), BF16), jax.ShapeDtypeStruct((3, tp, d), BF16),
                   jax.ShapeDtypeStruct((SM_ROWS, d), F32)],
        scratch_shapes=[pltpu.VMEM((TM + HALO_A, d), F32), pltpu.VMEM((TM + HALO_B, d), F32),
                        pltpu.VMEM((shl, d), F32), pltpu.VMEM((TM, d), F32), pltpu.VMEM((TM, d), F32),
                        pltpu.VMEM((TM, d), F32), pltpu.VMEM((TM, d), F32),
                        pltpu.VMEM((SM_ROWS, 8, d), F32)],
        compiler_params=_params(("arbitrary",)),
    )(ds2, proj, ca, cb, ya, yb, w3, wa, wb, ln_g, ln_b)


def kernel(x, meta_tokens, norm_g, w_in, conv_a_w, conv_a_b, ln_a_g, ln_a_b, w_a_out, b_a_out, conv_b_w, w_b_out, w_out, final_g, loss_target, m_meta_tokens, m_norm_g, m_w_in, m_conv_a_w, m_conv_a_b, m_ln_a_g, m_ln_a_b, m_w_a_out, m_b_a_out, m_conv_b_w, m_w_b_out, m_w_out, m_final_g, v_meta_tokens, v_norm_g, v_w_in, v_conv_a_w, v_conv_a_b, v_ln_a_g, v_ln_a_b, v_w_a_out, v_b_a_out, v_conv_b_w, v_w_b_out, v_w_out, v_final_g):
    seq, d = x.shape[1], x.shape[2]
    dc = meta_tokens.shape[1]
    sw = w_in.shape[2]
    rsh = w_a_out.shape[1]
    xi, yi, ci = _mesh_pos()
    me = 2 * xi + yi
    pos = jnp.stack([ci, me]).astype(jnp.int32)

    conv_rows = HALO_A + HALO_B + 8
    convs = jnp.concatenate([
        jnp.pad(conv_a_w[0], ((0, HALO_A - CONV_A), (0, 0))),
        jnp.pad(conv_b_w[0], ((0, HALO_B - CONV_B), (0, 0))), jnp.zeros((8, dc), F32)], axis=0)[None]
    w3_own = jnp.stack([w_a_out[0], w_b_out[0], w_out[0]])
    fg2 = final_g.reshape(1, d)
    xs = x[0]

    h, front, placed = _h_prep(xs, meta_tokens, norm_g, [(w_in, BF16), (w3_own, BF16), (convs, F32)], pos)
    proj, (wg_in, wg3, convg) = _proj_fwd(h, placed, pos)
    w3 = wg3.reshape(3, N_CHIPS * rsh, d)
    convg = jnp.transpose(convg[0], (1, 0, 2)).reshape(conv_rows, N_CHIPS * dc)
    wa_full = convg[0:HALO_A]
    wb_full = convg[HALO_A:HALO_A + HALO_B]
    ca, cb, ya, yb, abm_t, ds2, h_t, loss8, dfg8 = _mix_fwd(
        xs, front, proj, loss_target[0], w3, wa_full, wb_full, conv_a_b, ln_a_g, ln_a_b, b_a_out, fg2, norm_g)
    dproj, d3, sm = _mix_bwd(ds2, proj, ca, cb, ya, yb, w3, wa_full, wb_full, ln_a_g, ln_a_b)
    cw_sq = _col_block(d, 512)
    per_sq = d // cw_sq
    p32_sq, pbf_sq, _ = _dw_reduced(
        abm_t, d3, cw_sq, 3 * per_sq, lambda t: (t // per_sq, t % per_sq), N_CHIPS,
        (3, N_CHIPS, rsh // 2, d), (None, N_CHIPS, rsh // 2, cw_sq),
        lambda u: (u // per_sq, 0, 0, u % per_sq), [], "dw_square")
    cw_in = _col_block(sw, 768)
    ncol = sw // cw_in
    p32_in, pbf_in, (l_sq,) = _dw_reduced(
        h_t[None], dproj[None], cw_in, N_CHIPS * ncol, lambda t: (0, t), 1,
        (1, N_CHIPS, d // 2, sw), (None, None, d // 2, cw_in), lambda u: (0, u // ncol, 0, u % ncol),
        [pbf_sq], "dw_in")
    grad_x, dfront, dng8, l_in = _dh_bwd(dproj, wg_in, xs, front, ds2, norm_g, pbf_in)
    half_in = _sum_chips([(p32_in, l_in)], sw, pos, "rs_sum_in")
    half_sq = _sum_chips([(p32_sq, l_sq)], d, pos, "rs_sum_sq")
    tail_row = lax.broadcasted_iota(jnp.int32, (8, d), 0)
    tail = jnp.where(tail_row == 0, dng8, jnp.where(tail_row == 1, dfg8,
                     jnp.where(tail_row == 2, loss8[0, 0], 0.0)))
    block = jnp.concatenate([sm, dfront[TM - N_META:TM], tail], axis=0)
    (other_in, other_sq), red = _sibling_swap([half_in, half_sq], block)
    col = lax.dynamic_slice(red, (0, me * dc), (AR_ROWS, dc))
    g_small = {
        "meta_tokens": col[ROW_DMETA:ROW_DMETA + N_META],
        "norm_g": red[ROW_DNG:ROW_DNG + 1],
        "conv_a_w": col[ROW_DWA:ROW_DWA + CONV_A][None],
        "conv_a_b": red[ROW_DCAB:ROW_DCAB + 1],
        "ln_a_g": red[ROW_DLNG:ROW_DLNG + 1],
        "ln_a_b": red[ROW_DLNB:ROW_DLNB + 1],
        "b_a_out": red[ROW_DBAO:ROW_DBAO + 1],
        "conv_b_w": col[ROW_DWB:ROW_DWB + CONV_B][None],
        "final_g": red[ROW_DFG],
    }

    upd_in = _adam_halves([w_in], [m_w_in], [v_w_in], half_in, other_in, pos, "adam_in")
    upd_sq = _adam_halves([w_a_out, w_b_out, w_out], [m_w_a_out, m_w_b_out, m_w_out],
                          [v_w_a_out, v_w_b_out, v_w_out], half_sq, other_sq, pos, "adam_sq")
    small_w = {"meta_tokens": (meta_tokens, m_meta_tokens, v_meta_tokens), "norm_g": (norm_g, m_norm_g, v_norm_g),
               "conv_a_w": (conv_a_w, m_conv_a_w, v_conv_a_w), "conv_a_b": (conv_a_b, m_conv_a_b, v_conv_a_b),
               "ln_a_g": (ln_a_g, m_ln_a_g, v_ln_a_g), "ln_a_b": (ln_a_b, m_ln_a_b, v_ln_a_b),
               "b_a_out": (b_a_out, m_b_a_out, v_b_a_out), "conv_b_w": (conv_b_w, m_conv_b_w, v_conv_b_w),
               "final_g": (final_g, m_final_g, v_final_g)}
    names_small = list(small_w)
    as2d = lambda t: t.reshape(-1, t.shape[-1])
    upd_small = _adam_small([(as2d(small_w[k][0]), as2d(g_small[k]), as2d(small_w[k][1]), as2d(small_w[k][2]))
                             for k in names_small])

    grads, deltas, new_m, new_v = dict(g_small), {}, {}, {}
    for k, upd in zip(names_small, upd_small):
        deltas[k], new_m[k], new_v[k] = [t.reshape(small_w[k][0].shape) for t in upd]
    grads["w_in"], deltas["w_in"], new_m["w_in"], new_v["w_in"] = upd_in
    for idx, k in enumerate(["w_a_out", "w_b_out", "w_out"]):
        grads[k], deltas[k], new_m[k], new_v[k] = upd_sq[4 * idx:4 * idx + 4]

    loss = red[ROW_LOSS, 0]
    order = ["meta_tokens", "norm_g", "w_in", "conv_a_w", "conv_a_b", "ln_a_g", "ln_a_b", "w_a_out", "b_a_out",
             "conv_b_w", "w_b_out", "w_out", "final_g"]
    return (loss, grad_x[None], *[grads[k] for k in order], *[deltas[k] for k in order],
            *[new_m[k] for k in order], *[new_v[k] for k in order])
```

```python
import jax
import jax.numpy as jnp
from jax import lax
from jax.experimental import pallas as pl
from jax.experimental.pallas import tpu as pltpu

F32 = jnp.float32
BF16 = jnp.bfloat16
MESH = pl.DeviceIdType.MESH

EPS = 1e-6
N_META = 16
N_SPLIT = 9
CONV_A = 31
CONV_B = 3
HALO_A = 32
HALO_B = 8
SHIFT_ROWS = 24
TM = 256
RB_FWD = 128
RB_BWD = 64
N_ROW_TILES_BIG = 8
ROW_BLOCK = 256
N_CHIPS = 4
VMEM_LIMIT = 56 * 1024 * 1024
VMEM_LIMIT_BIG = 62 * 1024 * 1024

ADAM_LR = 0.001
ADAM_B1 = 0.9
ADAM_B2 = 0.999
ADAM_EPS = 1e-08
ADAM_WD = 0.01
ADAM_STEP = 10

ROW_DWA = 0
ROW_DWB = 32
ROW_DCAB = 40
ROW_DLNG = 41
ROW_DLNB = 42
ROW_DBAO = 43
SM_ROWS = 48
ROW_DMETA = 48
ROW_DNG = 64
ROW_DFG = 65
ROW_LOSS = 66
AR_ROWS = 72


def _sigmoid(v):
    return 0.5 * jnp.tanh(0.5 * v) + 0.5


def _params(sem, **kw):
    return pltpu.CompilerParams(dimension_semantics=sem, vmem_limit_bytes=VMEM_LIMIT, **kw)


def _rows(rb, n):
    return pl.ds(pl.multiple_of(rb * n, n), n)


def _mesh_pos():
    x, y, c = lax.axis_index("x"), lax.axis_index("y"), lax.axis_index("c")
    return x, y, c


def _half(ref, j, c):
    h = ref.shape[2] // 2
    return ref.at[:, j, pl.ds(c * h, h), :]


class _Exchange:
    def __init__(self, sends, recvs):
        self.sends, self.recvs = sends, recvs

    def start(self):
        for cp in self.sends:
            cp.start()

    def finish(self):
        for cp in self.recvs:
            cp.wait_recv()
        for cp in self.sends:
            cp.wait_send()


def _chip_exchange(part_ref, land_ref, send_sems, recv_sems):
    x, y, c = _mesh_pos()
    me = 2 * x + y
    sends, recvs = [], []
    for k, (px, py) in enumerate([(1 - x, y), (x, 1 - y), (1 - x, 1 - y)]):
        sems = dict(send_sem=send_sems.at[k], recv_sem=recv_sems.at[k], device_id=(px, py, c), device_id_type=MESH)
        sends.append(pltpu.make_async_remote_copy(
            src_ref=part_ref.at[:, 2 * px + py], dst_ref=land_ref.at[:, me], **sems))
        landed = land_ref.at[:, 2 * px + py]
        recvs.append(pltpu.make_async_remote_copy(src_ref=landed, dst_ref=landed, **sems))
    return _Exchange(sends, recvs)

def _sibling_swap(halves, small):
    n = len(halves)

    def body(*refs):
        ins, small_ref, outs, red_ref = refs[:n], refs[n], refs[n + 1:2 * n + 1], refs[2 * n + 1]
        send_sems, recv_sems = refs[2 * n + 2:2 * n + 4]
        reduce = _SmallAllReduce(small_ref, red_ref, *refs[2 * n + 4:])
        x, y, c = _mesh_pos()
        copies = [pltpu.make_async_remote_copy(
            src_ref=ins[a], dst_ref=outs[a], send_sem=send_sems.at[a], recv_sem=recv_sems.at[a],
            device_id=(x, y, 1 - c), device_id_type=MESH) for a in range(n)]
        reduce.start()
        for cp in copies:
            cp.start()
        reduce.between_chips()
        reduce.finish()
        for cp in copies:
            cp.wait()

    any_spec = pl.BlockSpec(memory_space=pl.ANY)
    vm = pl.BlockSpec(memory_space=pltpu.VMEM)
    outs = pl.pallas_call(
        body, name="rs_swap",
        in_specs=[any_spec] * n + [vm], out_specs=[any_spec] * n + [vm],
        out_shape=[jax.ShapeDtypeStruct(h.shape, h.dtype) for h in halves]
        + [jax.ShapeDtypeStruct(small.shape, F32)],
        scratch_shapes=[pltpu.SemaphoreType.DMA((n,)), pltpu.SemaphoreType.DMA((n,))]
        + _SmallAllReduce.scratch(*small.shape),
    )(*halves, small)
    return outs[:n], outs[n]


class _SmallAllReduce:
    def __init__(self, x_ref, out_ref, sib_ref, part_ref, peers_ref, send_sems, recv_sems):
        self.x_ref, self.out_ref, self.sib_ref, self.part_ref, self.peers_ref = x_ref, out_ref, sib_ref, part_ref, peers_ref
        x, y, c = _mesh_pos()
        self.me = 2 * x + y
        self.swap = pltpu.make_async_remote_copy(
            src_ref=x_ref, dst_ref=sib_ref, send_sem=send_sems.at[0], recv_sem=recv_sems.at[0],
            device_id=(x, y, 1 - c), device_id_type=MESH)
        self.sends, self.recvs = [], []
        for k, (px, py) in enumerate([(1 - x, y), (x, 1 - y), (1 - x, 1 - y)]):
            sems = dict(send_sem=send_sems.at[1 + k], recv_sem=recv_sems.at[1 + k],
                        device_id=(px, py, c), device_id_type=MESH)
            self.sends.append(pltpu.make_async_remote_copy(src_ref=part_ref, dst_ref=peers_ref.at[self.me], **sems))
            landed = peers_ref.at[2 * px + py]
            self.recvs.append(pltpu.make_async_remote_copy(src_ref=landed, dst_ref=landed, **sems))

    @staticmethod
    def scratch(rows, d):
        return [pltpu.VMEM((rows, d), F32), pltpu.VMEM((rows, d), F32), pltpu.VMEM((N_CHIPS, rows, d), F32),
                pltpu.SemaphoreType.DMA((4,)), pltpu.SemaphoreType.DMA((4,))]

    def start(self):
        self.swap.start()

    def between_chips(self):
        self.swap.wait()
        self.part_ref[...] = self.x_ref[...] + self.sib_ref[...]
        self.peers_ref[self.me] = self.part_ref[...]
        for cp in self.sends:
            cp.start()

    def finish(self):
        for cp in self.recvs:
            cp.wait_recv()
        for cp in self.sends:
            cp.wait_send()
        p = self.peers_ref
        self.out_ref[...] = ((p[0] + p[1]) + p[2]) + p[3]


def _sum_chips(parts, cw, pos, name):
    s, _, h, _ = parts[0][0].shape
    hb = min(h, ROW_BLOCK)
    widths = [own.shape[3] // cw for own, _ in parts]
    starts = [sum(widths[:a]) for a in range(len(parts))]

    def body(pos_ref, *refs):
        out_ref = refs[-1]
        n = pl.program_id(2)
        total = None
        for a in range(len(parts)):
            own, l1, l2, l3 = refs[4 * a:4 * a + 4]
            val = ((own[...] + l1[...].astype(F32)) + l2[...].astype(F32)) + l3[...].astype(F32)
            total = val if total is None else jnp.where(n >= starts[a], val, total)
        out_ref[...] = total

    def slot(a, k):
        col = lambda n: jnp.clip(n - starts[a], 0, widths[a] - 1)
        return pl.BlockSpec((None, None, hb, cw),
                            lambda si, b, n, pos_ref: (si, (pos_ref[1] + k) % N_CHIPS, b, col(n)))

    operands, specs = [], []
    for a, (own, landed) in enumerate(parts):
        operands += [own, landed, landed, landed]
        specs += [slot(a, 0), slot(a, 1), slot(a, 2), slot(a, 3)]
    return pl.pallas_call(
        body, name=name,
        grid_spec=pltpu.PrefetchScalarGridSpec(
            num_scalar_prefetch=1, grid=(s, h // hb, sum(widths)), in_specs=specs,
            out_specs=pl.BlockSpec((None, hb, cw), lambda si, b, n, pos_ref: (si, b, n))),
        out_shape=jax.ShapeDtypeStruct((s, h, sum(widths) * cw), F32),
        compiler_params=_params(("arbitrary",) * 3),
    )(pos, *operands)


def _adamw(w, g, m, v):
    m = ADAM_B1 * m + (1.0 - ADAM_B1) * g
    v = ADAM_B2 * v + (1.0 - ADAM_B2) * (g * g)
    m_hat = m / (1.0 - ADAM_B1 ** ADAM_STEP)
    v_hat = v / (1.0 - ADAM_B2 ** ADAM_STEP)
    delta = -ADAM_LR * (m_hat / (jnp.sqrt(v_hat) + ADAM_EPS) + ADAM_WD * w)
    return delta, m, v


def _adam_halves(ws, ms, vs, g_own, g_recv, pos, name):
    n = len(ws)
    _, r, c = ws[0].shape
    h = r // 2
    rb = min(h, ROW_BLOCK)
    nb = h // rb

    def body(pos_ref, *refs):
        w_refs, m_refs, v_refs = refs[:n], refs[n:2 * n], refs[2 * n:3 * n]
        go_ref, gr_ref = refs[3 * n:3 * n + 2]
        outs = refs[3 * n + 2:]
        mine = pl.program_id(0) == pos_ref[0]
        for a in range(n):
            g = jnp.where(mine, go_ref[a], gr_ref[a])
            delta, m, v = _adamw(w_refs[a][...], g, m_refs[a][...], v_refs[a][...])
            outs[4 * a][...], outs[4 * a + 1][...], outs[4 * a + 2][...], outs[4 * a + 3][...] = g, delta, m, v

    spec_w = pl.BlockSpec((None, rb, c), lambda hf, b, pos_ref: (0, hf * nb + b, 0))
    spec_g = pl.BlockSpec((n, rb, c), lambda hf, b, pos_ref: (0, b, 0))
    return pl.pallas_call(
        body, name=name,
        grid_spec=pltpu.PrefetchScalarGridSpec(
            num_scalar_prefetch=1, grid=(2, nb), in_specs=[spec_w] * (3 * n) + [spec_g] * 2,
            out_specs=[spec_w] * (4 * n)),
        out_shape=[jax.ShapeDtypeStruct((1, r, c), F32)] * (4 * n),
        compiler_params=_params(("arbitrary",) * 2),
    )(pos, *ws, *ms, *vs, g_own, g_recv)


def _adam_small(items):
    n = len(items)

    def body(*refs):
        ins, outs = refs[:4 * n], refs[4 * n:]
        for a in range(n):
            w_ref, g_ref, m_ref, v_ref = ins[4 * a:4 * a + 4]
            d, m, v = _adamw(w_ref[...], g_ref[...], m_ref[...], v_ref[...])
            outs[3 * a][...] = d
            outs[3 * a + 1][...] = m
            outs[3 * a + 2][...] = v

    vm = pl.BlockSpec(memory_space=pltpu.VMEM)
    flat = [t for it in items for t in it]
    outs = pl.pallas_call(
        body, name="adam_small", in_specs=[vm] * (4 * n), out_specs=[vm] * (3 * n),
        out_shape=[jax.ShapeDtypeStruct(it[0].shape, F32) for it in items for _ in range(3)],
    )(*flat)
    return [tuple(outs[3 * a:3 * a + 3]) for a in range(n)]


def _big_row_spec(seq, tmb, d, tile_of):
    return pl.BlockSpec((pl.Element(tmb), pl.Element(d)),
                        lambda *args: (pl.multiple_of(jnp.minimum(tile_of(*args) * tmb, seq - tmb), 8), 0))


def _big_row_tile(x_ref, front_ref, i):
    rows = x_ref[...]
    last = jnp.concatenate([rows[TM:], front_ref[...]], axis=0)
    return jnp.where(i == N_ROW_TILES_BIG - 1, last, rows)


def _h_prep(x, meta, norm_g, shards, pos):
    seq, d = x.shape
    tp = seq + TM
    dc = meta.shape[1]
    tmb = tp // N_ROW_TILES_BIG
    assert tmb >= TM and tp == tmb * N_ROW_TILES_BIG
    last = N_ROW_TILES_BIG - 1
    ns = len(shards)

    def body(pos_ref, x_ref, meta_ref, g_ref, *refs):
        shard_refs, (h_ref, front_ref), placed_refs = refs[:ns], refs[ns:ns + 2], refs[ns + 2:2 * ns + 2]
        metas, msend, mrecv = refs[2 * ns + 2:]
        for a in range(ns):
            placed_refs[a][...] = shard_refs[a][...].astype(placed_refs[a].dtype)
        x, y, c = _mesh_pos()
        me = 2 * x + y
        chips = [(1 - x, y), (x, 1 - y), (1 - x, 1 - y)]
        i = pl.program_id(0)

        def meta_copy(k, chip):
            return pltpu.make_async_remote_copy(
                src_ref=metas.at[chip], dst_ref=metas.at[chip], send_sem=msend.at[k], recv_sem=mrecv.at[k],
                device_id=(*chips[k], c), device_id_type=MESH)

        @pl.when(i == 0)
        def _():
            metas[me] = meta_ref[...]
            for k in range(3):
                meta_copy(k, me).start()
            front_ref[...] = jnp.zeros_like(front_ref)

        @pl.when(i == last)
        def _():
            for k, (px, py) in enumerate(chips):
                meta_copy(k, 2 * px + py).wait_recv()
            for q in range(N_CHIPS):
                front_ref[TM - N_META:TM, q * dc:(q + 1) * dc] = metas[q]

        s = _big_row_tile(x_ref, front_ref, i)
        r = lax.rsqrt(jnp.mean(s * s, axis=-1, keepdims=True) + EPS)
        h_ref[...] = (s * r * g_ref[...]).astype(BF16)

        @pl.when(i == last)
        def _():
            for k in range(3):
                meta_copy(k, me).wait_send()

    shard_in, shard_out, shard_shapes = [], [], []
    for arr, dtype in shards:
        s, r, c = arr.shape
        sliced = r % (N_ROW_TILES_BIG * 16) == 0
        rp = r // N_ROW_TILES_BIG if sliced else r
        step = (lambda i: i) if sliced else (lambda i: 0)
        shard_in.append(pl.BlockSpec((s, rp, c), lambda i, pos_ref, step=step: (0, step(i), 0)))
        shard_out.append(pl.BlockSpec((s, None, rp, c), lambda i, pos_ref, step=step: (0, pos_ref[1], step(i), 0)))
        shard_shapes.append(jax.ShapeDtypeStruct((s, N_CHIPS, r, c), dtype))
    outs = pl.pallas_call(
        body, name="f0_norm",
        grid_spec=pltpu.PrefetchScalarGridSpec(
            num_scalar_prefetch=1, grid=(N_ROW_TILES_BIG,),
            in_specs=[_big_row_spec(seq, tmb, d, lambda i, pos_ref: i),
                      pl.BlockSpec(meta.shape, lambda i, pos_ref: (0, 0)),
                      pl.BlockSpec((1, d), lambda i, pos_ref: (0, 0))] + shard_in,
            out_specs=[pl.BlockSpec((tmb, d), lambda i, pos_ref: (i, 0)),
                       pl.BlockSpec((TM, d), lambda i, pos_ref: (0, 0))] + shard_out,
            scratch_shapes=[pltpu.VMEM((N_CHIPS,) + meta.shape, F32),
                            pltpu.SemaphoreType.DMA((3,)), pltpu.SemaphoreType.DMA((3,))]),
        out_shape=[jax.ShapeDtypeStruct((tp, d), BF16), jax.ShapeDtypeStruct((TM, d), F32)] + shard_shapes,
        compiler_params=_params(("arbitrary",)),
    )(pos, x, meta, norm_g, *[arr for arr, _ in shards])
    return outs[0], outs[1], outs[2:]


N_UNITS = 3
N_STEPS_PROJ = N_CHIPS * N_UNITS


def _proj_plan(u):
    v = u - N_UNITS
    if v < 2 * N_UNITS:
        return v % 2, v // 2
    return 2, v - 2 * N_UNITS


def _proj_unit(t, me):
    v = t - N_UNITS
    near = v < 2 * N_UNITS
    rel = jnp.where(near, v % 2, 2)
    unit = jnp.where(near, v // 2, v - 2 * N_UNITS)
    flip = jnp.where(rel == 0, 2, jnp.where(rel == 1, 1, 3))
    own = t < N_UNITS
    return jnp.where(own, me, lax.bitwise_xor(me, flip)), jnp.where(own, t, unit)


def _proj_fwd(h, bufs, pos):
    tp, d = h.shape
    _, nsh, _, sw = bufs[0].shape
    cu = sw // N_UNITS
    assert cu % 128 == 0
    n = len(bufs)
    w_sems = 6 * N_UNITS
    last = N_STEPS_PROJ - 1
    late = N_STEPS_PROJ - N_UNITS

    def body(pos_ref, h_ref, *refs):
        proj_ref = refs[n]
        gbufs = refs[n + 1:2 * n + 1]
        wbuf, wsems, send_sems, recv_sems = refs[2 * n + 1:]
        x, y, c = _mesh_pos()
        me = 2 * x + y
        sibling = (x, y, 1 - c)
        chips = [(1 - x, y), (x, 1 - y), (1 - x, 1 - y)]
        chip_ids = [2 * px + py for px, py in chips]
        relayed_chip = jnp.where(c == 0, chip_ids[0], chip_ids[1])
        relay_to = (jnp.where(c == 0, x, 1 - x), jnp.where(c == 0, 1 - y, y), c)
        t = pl.program_id(0)

        def remote(idx, piece, to):
            return pltpu.make_async_remote_copy(
                src_ref=piece, dst_ref=piece, send_sem=send_sems.at[idx], recv_sem=recv_sems.at[idx],
                device_id=to, device_id_type=MESH)

        hr = d // 2

        def chunk_of(chip, half, k):
            return gbufs[0].at[0, chip, pl.ds(half * hr, hr), pl.ds(k * cu, cu)]

        def own_chunk(r, k):
            return remote(6 * k + r, chunk_of(me, c, k), (*chips[r], c))

        def landed_chunk(r, k):
            return remote(6 * k + r, chunk_of(chip_ids[r], c, k), (*chips[r], c))

        def relay_chunk(k):
            return remote(6 * k + 2, chunk_of(relayed_chip, c, k), relay_to)

        def sibling_chunk(r, k, half):
            return remote(6 * k + 3 + r, chunk_of(chip_ids[r], half, k), sibling)

        def fetch(u):
            chip, unit = _proj_unit(jnp.int32(u), me)
            return pltpu.make_async_copy(gbufs[0].at[0, chip, :, pl.ds(pl.multiple_of(unit * cu, 128), cu)],
                                         wbuf.at[u % 2], wsems.at[u % 2])

        def make_available(u):
            r, k = _proj_plan(u)
            landed_chunk(r, k).wait_recv()
            if r < 2:
                pl.when(c == r)(lambda: relay_chunk(k).start())
            sibling_chunk(r, k, c).start()
            sibling_chunk(r, k, 1 - c).wait_recv()

        def own_piece(a, r):
            return remote(w_sems + 6 * (a - 1) + r, _half(gbufs[a], me, c), (*chips[r], c))

        def relay(a):
            return remote(w_sems + 6 * (a - 1) + 2, _half(gbufs[a], relayed_chip, c), relay_to)

        def to_sibling(a, r, core):
            return remote(w_sems + 6 * (a - 1) + 3 + r, _half(gbufs[a], chip_ids[r], core), sibling)

        def landed(a, r):
            return remote(w_sems + 6 * (a - 1) + r, _half(gbufs[a], chip_ids[r], c), (*chips[r], c))

        for u in range(N_STEPS_PROJ):
            @pl.when(t == u)
            def _(u=u):
                if u == 0:
                    for k in range(N_UNITS):
                        for r in range(2):
                            own_chunk(r, k).start()
                    for a in range(1, n):
                        for r in range(2):
                            own_piece(a, r).start()
                    fetch(0).start()
                if u < last:
                    if u + 1 >= N_UNITS:
                        make_available(u + 1)
                    fetch(u + 1).start()
                if u == late:
                    for a in range(1, n):
                        landed(a, 0).wait_recv()
                        landed(a, 1).wait_recv()
                        relay(a).start()
                        for r in range(2):
                            to_sibling(a, r, c).start()
                        for r in range(2):
                            to_sibling(a, r, 1 - c).wait_recv()
                fetch(u).wait()

        proj_ref[...] = jnp.dot(h_ref[...], wbuf[t % 2], preferred_element_type=F32).astype(BF16)

        @pl.when(t == last)
        def _():
            for a in range(1, n):
                landed(a, 2).wait_recv()
                to_sibling(a, 2, c).start()
                to_sibling(a, 2, 1 - c).wait_recv()
            for k in range(N_UNITS):
                for r in range(2):
                    own_chunk(r, k).wait_send()
                relay_chunk(k).wait_send()
                for r in range(3):
                    sibling_chunk(r, k, c).wait_send()
            for a in range(1, n):
                for r in range(2):
                    own_piece(a, r).wait_send()
                relay(a).wait_send()
                for r in range(3):
                    to_sibling(a, r, c).wait_send()

    def out_index(t, pos_ref):
        chip, unit = _proj_unit(t, pos_ref[1])
        return 0, chip * N_UNITS + unit

    any_spec = pl.BlockSpec(memory_space=pl.ANY)
    outs = pl.pallas_call(
        body, name="f1_proj",
        grid_spec=pltpu.PrefetchScalarGridSpec(
            num_scalar_prefetch=1, grid=(N_STEPS_PROJ,),
            in_specs=[pl.BlockSpec((tp, d), lambda t, pos_ref: (0, 0))] + [any_spec] * n,
            out_specs=[pl.BlockSpec((tp, cu), out_index)] + [any_spec] * n,
            scratch_shapes=[pltpu.VMEM((2, d, cu), BF16), pltpu.SemaphoreType.DMA((2,)),
                            pltpu.SemaphoreType.DMA((w_sems + 6 * (n - 1),)),
                            pltpu.SemaphoreType.DMA((w_sems + 6 * (n - 1),))]),
        out_shape=[jax.ShapeDtypeStruct((tp, nsh * sw), BF16)]
        + [jax.ShapeDtypeStruct(b.shape, b.dtype) for b in bufs],
        input_output_aliases={2 + a: 1 + a for a in range(n)},
        compiler_params=_params(("arbitrary",)),
    )(pos, h, *bufs)
    return outs[0], outs[1:]


def _dh_bwd(dproj, wg_in, x, front, ds2, norm_g, part):
    seq, d = x.shape
    tp = seq + TM
    _, nsh, _, sw = wg_in.shape
    tmb = tp // N_ROW_TILES_BIG
    tail = tmb - TM
    last = N_ROW_TILES_BIG - 1

    def body(dp_ref, w_hbm, x_ref, front_ref, ds2_ref, g_ref, part_ref, gx_hbm, dfront_ref, dng_ref, land_ref,
             wbuf, gacc, dsbuf, wsem, osems, send_sems, recv_sems):
        exchange = _chip_exchange(part_ref, land_ref, send_sems, recv_sems)
        i = pl.program_id(0)

        def x_rows_out(step):
            return pltpu.make_async_copy(dsbuf.at[step % 2], gx_hbm.at[pl.ds(step * tmb, tmb), :], osems.at[step % 2])

        last_out = pltpu.make_async_copy(dsbuf.at[last % 2, pl.ds(0, tail), :],
                                         gx_hbm.at[pl.ds(last * tmb, tail), :], osems.at[last % 2])

        @pl.when(i == 0)
        def _():
            exchange.start()
            gacc[...] = jnp.zeros_like(gacc)
            whole = pltpu.make_async_copy(w_hbm.at[0], wbuf, wsem)
            whole.start()
            whole.wait()

        dh = None
        for j in range(nsh):
            part = lax.dot_general(dp_ref[:, j * sw:(j + 1) * sw], wbuf[j], (((1,), (1,)), ((), ())),
                                   preferred_element_type=F32)
            dh = part if dh is None else dh + part
        s = _big_row_tile(x_ref, front_ref, i)
        r = lax.rsqrt(jnp.mean(s * s, axis=-1, keepdims=True) + EPS)
        gacc[...] += (dh * s * r).reshape(tmb // 8, 8, d).sum(axis=0)
        t = dh * g_ref[...]

        @pl.when(i >= 2)
        def _():
            x_rows_out(i - 2).wait()

        dsbuf[i % 2] = ds2_ref[...] + r * t - s * (r * r * r) * jnp.mean(t * s, axis=-1, keepdims=True)

        @pl.when(i < last)
        def _():
            x_rows_out(i).start()

        @pl.when(i == last)
        def _():
            last_out.start()
            dfront_ref[...] = dsbuf[last % 2, tail:, :]
            dng_ref[...] = jnp.broadcast_to(jnp.sum(gacc[...], axis=0, keepdims=True), (8, d))
            exchange.finish()
            x_rows_out(last - 1).wait()
            last_out.wait()

    any_spec = pl.BlockSpec(memory_space=pl.ANY)
    return pl.pallas_call(
        body, name="b2_dh", grid=(N_ROW_TILES_BIG,),
        in_specs=[pl.BlockSpec((tmb, nsh * sw), lambda i: (i, 0)), any_spec,
                  _big_row_spec(seq, tmb, d, lambda i: i),
                  pl.BlockSpec((TM, d), lambda i: (0, 0)),
                  pl.BlockSpec((tmb, d), lambda i: (i, 0)),
                  pl.BlockSpec((1, d), lambda i: (0, 0)), any_spec],
        out_specs=[any_spec, pl.BlockSpec((TM, d), lambda i: (0, 0)),
                   pl.BlockSpec((8, d), lambda i: (0, 0)), any_spec],
        out_shape=[jax.ShapeDtypeStruct((seq, d), F32), jax.ShapeDtypeStruct((TM, d), F32),
                   jax.ShapeDtypeStruct((8, d), F32), jax.ShapeDtypeStruct(part.shape, part.dtype)],
        scratch_shapes=[pltpu.VMEM((nsh, d, sw), BF16), pltpu.VMEM((8, d), F32), pltpu.VMEM((2, tmb, d), F32),
                        pltpu.SemaphoreType.DMA, pltpu.SemaphoreType.DMA((2,)),
                        pltpu.SemaphoreType.DMA((3,)), pltpu.SemaphoreType.DMA((3,))],
        compiler_params=pltpu.CompilerParams(dimension_semantics=("arbitrary",),
                                             vmem_limit_bytes=VMEM_LIMIT_BIG),
    )(dproj, wg_in, x, front, ds2, norm_g, part)


def _col_block(width, cap):
    return max(b for b in range(128, cap + 1, 128) if width % b == 0)


def _dw_reduced(lhs_t, rhs, cw, nblk, operands, groups, out_dims, out_block, out_index, carried, name):
    na, d, tp = lhs_t.shape
    rg = d // groups
    hh = rg // 2
    nc = len(carried)

    def body(*refs):
        l_ref, r_ref = refs[:2]
        part_refs = refs[2:2 + nc]
        p32_ref, pbf_ref = refs[2 + nc:4 + nc]
        land_refs = refs[4 + nc:4 + 2 * nc]
        res, rbuf, send_sems, recv_sems = refs[4 + 2 * nc:8 + 2 * nc]
        xsems = refs[8 + 2 * nc:]
        exchanges = [_chip_exchange(part_refs[e], land_refs[e], xsems[2 * e], xsems[2 * e + 1]) for e in range(nc)]
        exchange = _Exchange([s for ex in exchanges for s in ex.sends], [r for ex in exchanges for r in ex.recvs])
        x, y, c = _mesh_pos()
        t = pl.program_id(0)
        u = jnp.maximum(t - 1, 0)

        def to_sibling(blk):
            return pltpu.make_async_remote_copy(
                src_ref=res.at[blk % 2, :, pl.ds((1 - c) * hh, hh), :], dst_ref=rbuf.at[blk % 2],
                send_sem=send_sems.at[blk], recv_sem=recv_sems.at[blk],
                device_id=(x, y, 1 - c), device_id_type=MESH)

        @pl.when(t == 0)
        def _():
            exchange.start()

        @pl.when(t < nblk)
        def _():
            res[t % 2] = jnp.dot(l_ref[...], r_ref[...], preferred_element_type=F32).reshape(groups, rg, cw)

        @pl.when(t >= 1)
        def _():
            to_sibling(u).wait_recv()
            p = res[u % 2, :, pl.ds(c * hh, hh), :] + rbuf[u % 2]
            p32_ref[...] = p.reshape(p32_ref.shape)
            pbf_ref[...] = p.reshape(pbf_ref.shape).astype(BF16)

        @pl.when(t < nblk)
        def _():
            to_sibling(t).start()

        @pl.when(t >= 1)
        def _():
            to_sibling(u).wait_send()

        @pl.when(t == nblk)
        def _():
            exchange.finish()

    any_spec = pl.BlockSpec(memory_space=pl.ANY)
    last = nblk - 1
    out_spec = pl.BlockSpec(out_block, lambda t: out_index(jnp.maximum(t - 1, 0)))
    outs = pl.pallas_call(
        body, name=name, grid=(nblk + 1,),
        in_specs=[pl.BlockSpec((None, d, tp), lambda t: (operands(jnp.minimum(t, last))[0], 0, 0)),
                  pl.BlockSpec((None, tp, cw), lambda t: (operands(jnp.minimum(t, last))[0], 0,
                                                          operands(jnp.minimum(t, last))[1]))]
        + [any_spec] * nc,
        out_specs=[out_spec, out_spec] + [any_spec] * nc,
        out_shape=[jax.ShapeDtypeStruct(out_dims, F32), jax.ShapeDtypeStruct(out_dims, BF16)]
        + [jax.ShapeDtypeStruct(e.shape, e.dtype) for e in carried],
        scratch_shapes=[pltpu.VMEM((2, groups, rg, cw), F32), pltpu.VMEM((2, groups, hh, cw), F32),
                        pltpu.SemaphoreType.DMA((nblk,)), pltpu.SemaphoreType.DMA((nblk,))]
        + [pltpu.SemaphoreType.DMA((3,)), pltpu.SemaphoreType.DMA((3,))] * nc,
        compiler_params=_params(("arbitrary",)),
    )(lhs_t, rhs, *carried)
    return outs[0], outs[1], outs[2:]


def _conv_a_taps(first_lag, last_lag):
    out = []
    for r in range(8):
        taps = [(q, 8 * q + r) for q in range(5) if first_lag <= 8 * q + r <= last_lag]
        if taps:
            out.append((r, taps))
    return out


def _tile_block(i, nt):
    return jnp.where(i == 0, nt - 1, i - 1)


def _mix_fwd(x, front, proj, target, w3, wa, wb, conv_a_b, ln_g, ln_b, b_a_out, final_g, norm_g):
    seq, d = x.shape
    tp = seq + TM
    nt = tp // TM
    RB = RB_FWD
    nrb = TM // RB
    shl = TM + SHIFT_ROWS

    def body(x_ref, front_ref, proj_ref, tgt_ref, w3_ref, wa_ref, wb_ref, cab_ref, lng_ref, lnb_ref, bao_ref, fg_ref,
             ng_ref, ca_ref, cb_ref, ya_ref, yb_ref, abmt_ref, ds2_ref, ht_ref, loss_ref, dfg_ref,
             abm_ref, ext_a, ext_b, sh, s2_s, lacc, gacc):
        i = pl.program_id(0)

        def split(k, rows):
            return proj_ref[rows, k * d:(k + 1) * d].astype(F32)

        def s_tile():
            return jnp.where(i == 0, front_ref[...], x_ref[...])

        s_in = s_tile()
        h = s_in * lax.rsqrt(jnp.mean(s_in * s_in, axis=-1, keepdims=True) + EPS) * ng_ref[...]
        ht_ref[...] = h.astype(BF16).T

        @pl.when(i == 0)
        def _():
            ext_a[0:HALO_A, :] = jnp.zeros((HALO_A, d), F32)
            ext_b[0:HALO_B, :] = jnp.zeros((HALO_B, d), F32)
            lacc[...] = jnp.zeros_like(lacc)
            gacc[...] = jnp.zeros_like(gacc)

        def conv_in(rb, carry):
            rows = _rows(rb, RB)
            ua0 = split(0, rows) * _sigmoid(split(1, rows))
            ext_a[pl.ds(pl.multiple_of(HALO_A + rb * RB, 8), RB), :] = ua0
            ext_b[pl.ds(pl.multiple_of(HALO_B + rb * RB, 8), RB), :] = split(4, rows) * split(5, rows)
            ca_ref[rows, :] = jnp.broadcast_to(cab_ref[...], (RB, d))
            return carry
        lax.fori_loop(0, nrb, conv_in, 0)

        @pl.when(i == 0)
        def _():
            abmt_ref[...] = jnp.zeros_like(abmt_ref)
            ds2_ref[...] = jnp.zeros_like(ds2_ref)

        @pl.when(i > 0)
        def _():
            tile_after_conv_inputs(split, s_tile, tgt_ref, w3_ref, wa_ref, wb_ref, lng_ref, lnb_ref, bao_ref, fg_ref,
                                   ca_ref, cb_ref, ya_ref, yb_ref, abmt_ref, ds2_ref, abm_ref, ext_a, ext_b, sh,
                                   s2_s, lacc, gacc)

        ext_a[0:HALO_A, :] = ext_a[TM:TM + HALO_A, :]
        ext_b[0:HALO_B, :] = ext_b[TM:TM + HALO_B, :]

        @pl.when(i == nt - 1)
        def _():
            loss_ref[...] = jnp.broadcast_to(0.5 * jnp.sum(lacc[...]) * (1.0 / d), (8, 128))
            dfg_ref[...] = jnp.broadcast_to(jnp.sum(gacc[...], axis=0, keepdims=True), (8, d))

    def tile_after_conv_inputs(split, s_tile, tgt_ref, w3_ref, wa_ref, wb_ref, lng_ref, lnb_ref, bao_ref, fg_ref,
                               ca_ref, cb_ref, ya_ref, yb_ref, abmt_ref, ds2_ref, abm_ref, ext_a, ext_b, sh, s2_s,
                               lacc, gacc):
        for r, taps in _conv_a_taps(HALO_A - CONV_A + 1, HALO_A):
            if r == 0:
                src = ext_a
            else:
                sh[...] = ext_a[r:r + shl, :]
                src = sh

            def conv_acc(rb, carry, src=src, taps=taps):
                rows = _rows(rb, RB)
                acc = ca_ref[rows, :]
                for q, lag in taps:
                    k = lag - (HALO_A - CONV_A + 1)
                    acc = acc + src[pl.ds(pl.multiple_of(rb * RB + 8 * q, 8), RB), :] * wa_ref[k:k + 1, :]
                ca_ref[rows, :] = acc
                return carry
            lax.fori_loop(0, nrb, conv_acc, 0)

        cb_ref[...] = ext_b[HALO_B:HALO_B + TM, :] * wb_ref[2:3, :]
        for k in range(CONV_B - 1):
            off = HALO_B - CONV_B + 1 + k
            sh[0:TM, :] = ext_b[off:off + TM, :]
            cb_ref[...] += sh[0:TM, :] * wb_ref[k:k + 1, :]

        def branches(rb, carry):
            rows = _rows(rb, RB)
            ca = ca_ref[rows, :]
            mu = jnp.mean(ca, axis=-1, keepdims=True)
            xc = ca - mu
            rstd = lax.rsqrt(jnp.mean(xc * xc, axis=-1, keepdims=True) + EPS)
            ln = xc * rstd * lng_ref[...] + lnb_ref[...]
            ua = ln * _sigmoid(ln)
            a_z = split(2, rows)
            abm_ref[0, rows, :] = (ua * (a_z * _sigmoid(a_z))).astype(BF16)
            return carry
        lax.fori_loop(0, nrb, branches, 0)

        def branch_b(rb, carry):
            rows = _rows(rb, RB)
            b_z = split(6, rows)
            ub = split(3, rows) * cb_ref[rows, :]
            abm_ref[1, rows, :] = (ub * (b_z * _sigmoid(b_z))).astype(BF16)
            return carry
        lax.fori_loop(0, nrb, branch_b, 0)

        ya_ref[...] = jnp.dot(abm_ref[0], w3_ref[0], preferred_element_type=F32) + bao_ref[...]
        yb_ref[...] = jnp.dot(abm_ref[1], w3_ref[1], preferred_element_type=F32)

        def merge(rb, carry):
            rows = _rows(rb, RB)
            m = _sigmoid(split(7, rows)) * ya_ref[rows, :] + _sigmoid(split(8, rows)) * yb_ref[rows, :]
            abm_ref[2, rows, :] = m.astype(BF16)
            return carry
        lax.fori_loop(0, nrb, merge, 0)

        s2_s[...] = s_tile() + jnp.dot(abm_ref[2], w3_ref[2], preferred_element_type=F32)
        for k in range(3):
            abmt_ref[k] = abm_ref[k].T

        def head(rb, carry):
            rows = _rows(rb, RB)
            s2 = s2_s[rows, :]
            r2 = lax.rsqrt(jnp.mean(s2 * s2, axis=-1, keepdims=True) + EPS)
            diff = s2 * r2 * fg_ref[...] - tgt_ref[rows, :]
            lacc[...] += diff * diff
            dy = diff * (1.0 / d)
            gacc[...] += (dy * s2 * r2).reshape(RB // 8, 8, d).sum(axis=0)
            t = dy * fg_ref[...]
            ds2_ref[rows, :] = r2 * t - s2 * (r2 * r2 * r2) * jnp.mean(t * s2, axis=-1, keepdims=True)
            return carry
        lax.fori_loop(0, nrb, head, 0)

    row_f32 = pl.BlockSpec((TM, d), lambda i: (_tile_block(i, nt), 0))
    x_rows = pl.BlockSpec((TM, d), lambda i: (jnp.maximum(i - 1, 0), 0))
    const = lambda shape: pl.BlockSpec(shape, lambda i: (0,) * len(shape))
    return pl.pallas_call(
        body, name="f2_mix", grid=(nt,),
        in_specs=[x_rows, const((TM, d)),
                  pl.BlockSpec((TM, N_SPLIT * d), lambda i: (_tile_block(i, nt), 0)),
                  x_rows,
                  const((3, d, d)), const(wa.shape), const(wb.shape)] + [const((1, d))] * 6,
        out_specs=[row_f32, row_f32, row_f32, row_f32,
                   pl.BlockSpec((3, d, TM), lambda i: (0, 0, _tile_block(i, nt))),
                   row_f32, pl.BlockSpec((d, TM), lambda i: (0, _tile_block(i, nt))),
                   const((8, 128)), const((8, d))],
        out_shape=[jax.ShapeDtypeStruct((tp, d), F32)] * 4
        + [jax.ShapeDtypeStruct((3, d, tp), BF16), jax.ShapeDtypeStruct((tp, d), F32),
           jax.ShapeDtypeStruct((d, tp), BF16),
           jax.ShapeDtypeStruct((8, 128), F32), jax.ShapeDtypeStruct((8, d), F32)],
        scratch_shapes=[pltpu.VMEM((3, TM, d), BF16),
                        pltpu.VMEM((HALO_A + TM, d), F32), pltpu.VMEM((HALO_B + TM, d), F32),
                        pltpu.VMEM((shl, d), F32), pltpu.VMEM((TM, d), F32),
                        pltpu.VMEM((RB, d), F32), pltpu.VMEM((8, d), F32)],
        compiler_params=_params(("arbitrary",)),
    )(x, front, proj, target, w3, wa, wb, conv_a_b, ln_g, ln_b, b_a_out, final_g, norm_g)


def _mix_bwd(ds2, proj, ca, cb, ya, yb, w3, wa, wb, ln_g, ln_b):
    tp, d = ds2.shape
    nt = tp // TM
    RB = RB_BWD
    nrb = TM // RB
    shl = TM + SHIFT_ROWS
    nt_dims = (((1,), (1,)), ((), ()))

    def body(ds2_ref, proj_ref, ca_ref, cb_ref, ya_ref, yb_ref, w3_ref, wa_ref, wb_ref, lng_ref, lnb_ref,
             dproj_ref, d3_ref, sm_ref, ext_d, ext_e, sh, dm_s, dpa_s, dpb_s, dua0_s, acc):
        step = pl.program_id(0)

        def split(k, rows):
            return proj_ref[rows, k * d:(k + 1) * d].astype(F32)

        def put(k, rows, val):
            dproj_ref[rows, k * d:(k + 1) * d] = val.astype(BF16)

        def accum(row, val):
            acc[row] += val.reshape(RB // 8, 8, d).sum(axis=0)

        @pl.when(step == 0)
        def _():
            ext_d[TM:TM + HALO_A, :] = jnp.zeros((HALO_A, d), F32)
            ext_e[TM:TM + HALO_B, :] = jnp.zeros((HALO_B, d), F32)
            acc[...] = jnp.zeros_like(acc)

        front = step == nt - 1

        @pl.when(front)
        def _():
            d3_ref[...] = jnp.zeros_like(d3_ref)

            def conv_only(rb, carry):
                rows = _rows(rb, RB)
                zeros = jnp.zeros((RB, d), F32)
                for k in (2, 3, 6, 7, 8):
                    put(k, rows, zeros)
                ext_d[rows, :] = zeros
                ext_e[rows, :] = zeros
                dua0_s[rows, :] = zeros
                dm_s[rows, :] = split(0, rows) * _sigmoid(split(1, rows))
                return carry
            lax.fori_loop(0, nrb, conv_only, 0)

        @pl.when(jnp.logical_not(front))
        def _():
            tile_to_conv_outputs(split, put, accum, ds2_ref, ca_ref, cb_ref, ya_ref, yb_ref, w3_ref, lng_ref, lnb_ref,
                                 d3_ref, ext_d, ext_e, dm_s, dpa_s, dpb_s, dua0_s)

        tile_conv_transposes(split, put, accum, wa_ref, wb_ref, ext_d, ext_e, sh, dm_s, dpb_s, dua0_s)

        @pl.when(front)
        def _():
            for row in range(SM_ROWS):
                sm_ref[row:row + 1, :] = jnp.sum(acc[row], axis=0, keepdims=True)

    def tile_to_conv_outputs(split, put, accum, ds2_ref, ca_ref, cb_ref, ya_ref, yb_ref, w3_ref, lng_ref, lnb_ref,
                             d3_ref, ext_d, ext_e, dm_s, dpa_s, dpb_s, dua0_s):
        d3_ref[2] = ds2_ref[...].astype(BF16)
        dm_s[...] = lax.dot_general(d3_ref[2], w3_ref[2], nt_dims, preferred_element_type=F32)

        def gates(rb, carry):
            rows = _rows(rb, RB)
            dm = dm_s[rows, :]
            sa = _sigmoid(split(7, rows))
            sb = _sigmoid(split(8, rows))
            ya_v = ya_ref[rows, :]
            yb_v = yb_ref[rows, :]
            put(7, rows, dm * ya_v * sa * (1.0 - sa))
            put(8, rows, dm * yb_v * sb * (1.0 - sb))
            dya = dm * sa
            accum(ROW_DBAO, dya)
            d3_ref[0, rows, :] = dya.astype(BF16)
            d3_ref[1, rows, :] = (dm * sb).astype(BF16)
            return carry
        lax.fori_loop(0, nrb, gates, 0)

        dpa_s[...] = lax.dot_general(d3_ref[0], w3_ref[0], nt_dims, preferred_element_type=F32)
        dpb_s[...] = lax.dot_general(d3_ref[1], w3_ref[1], nt_dims, preferred_element_type=F32)

        def branch_a(rb, carry):
            rows = _rows(rb, RB)
            ca_v = ca_ref[rows, :]
            mu = jnp.mean(ca_v, axis=-1, keepdims=True)
            xc = ca_v - mu
            rstd = lax.rsqrt(jnp.mean(xc * xc, axis=-1, keepdims=True) + EPS)
            xhat = xc * rstd
            ln = xhat * lng_ref[...] + lnb_ref[...]
            sl = _sigmoid(ln)
            ua = ln * sl
            a_z = split(2, rows)
            sz = _sigmoid(a_z)
            dpa = dpa_s[rows, :]
            put(2, rows, dpa * ua * (sz * (1.0 + a_z * (1.0 - sz))))
            dln = dpa * (a_z * sz) * (sl * (1.0 + ln * (1.0 - sl)))
            accum(ROW_DLNG, dln * xhat)
            accum(ROW_DLNB, dln)
            dxh = dln * lng_ref[...]
            dca = rstd * (dxh - jnp.mean(dxh, axis=-1, keepdims=True)
                          - xhat * jnp.mean(dxh * xhat, axis=-1, keepdims=True))
            accum(ROW_DCAB, dca)
            ext_d[rows, :] = dca
            return carry
        lax.fori_loop(0, nrb, branch_a, 0)

        def branch_b(rb, carry):
            rows = _rows(rb, RB)
            dua0_s[rows, :] = jnp.zeros((RB, d), F32)
            dm_s[rows, :] = split(0, rows) * _sigmoid(split(1, rows))
            b_z = split(6, rows)
            szb = _sigmoid(b_z)
            dpb = dpb_s[rows, :]
            b_b = split(3, rows)
            cb_v = cb_ref[rows, :]
            put(6, rows, dpb * (b_b * cb_v) * (szb * (1.0 + b_z * (1.0 - szb))))
            dub = dpb * (b_z * szb)
            put(3, rows, dub * cb_v)
            ext_e[rows, :] = dub * b_b
            return carry
        lax.fori_loop(0, nrb, branch_b, 0)

    def tile_conv_transposes(split, put, accum, wa_ref, wb_ref, ext_d, ext_e, sh, dm_s, dpb_s, dua0_s):
        for r, taps in _conv_a_taps(0, CONV_A - 1):
            if r == 0:
                src = ext_d
            else:
                sh[...] = ext_d[r:r + shl, :]
                src = sh

            def conv_t(rb, carry, src=src, taps=taps):
                rows = _rows(rb, RB)
                ua0 = dm_s[rows, :]
                dua0 = dua0_s[rows, :]
                for q, lag in taps:
                    k = CONV_A - 1 - lag
                    slab = src[pl.ds(pl.multiple_of(rb * RB + 8 * q, 8), RB), :]
                    dua0 = dua0 + slab * wa_ref[k:k + 1, :]
                    accum(ROW_DWA + k, slab * ua0)
                dua0_s[rows, :] = dua0
                return carry
            lax.fori_loop(0, nrb, conv_t, 0)
        ext_d[TM:TM + HALO_A, :] = ext_d[0:HALO_A, :]

        dpb_s[...] = ext_e[0:TM, :] * wb_ref[CONV_B - 1:CONV_B, :]
        for lag in range(CONV_B):
            k = CONV_B - 1 - lag
            if lag > 0:
                sh[0:TM, :] = ext_e[lag:lag + TM, :]
                dpb_s[...] += sh[0:TM, :] * wb_ref[k:k + 1, :]
            src = ext_e if lag == 0 else sh

            def conv_b_w(rb, carry, src=src, k=k):
                rows = _rows(rb, RB)
                accum(ROW_DWB + k, src[rows, :] * (split(4, rows) * split(5, rows)))
                return carry
            lax.fori_loop(0, nrb, conv_b_w, 0)
        ext_e[TM:TM + HALO_B, :] = ext_e[0:HALO_B, :]

        def inputs(rb, carry):
            rows = _rows(rb, RB)
            dua0 = dua0_s[rows, :]
            a_val = split(0, rows)
            sg = _sigmoid(split(1, rows))
            put(0, rows, dua0 * sg)
            put(1, rows, dua0 * a_val * sg * (1.0 - sg))
            dcbin = dpb_s[rows, :]
            put(4, rows, dcbin * split(5, rows))
            put(5, rows, dcbin * split(4, rows))
            return carry
        lax.fori_loop(0, nrb, inputs, 0)

    rev = lambda i: (_tile_block(nt - 1 - i, nt), 0)
    row_f32 = pl.BlockSpec((TM, d), rev)
    const = lambda shape: pl.BlockSpec(shape, lambda i: (0,) * len(shape))
    return pl.pallas_call(
        body, name="b1_mix", grid=(nt,),
        in_specs=[row_f32, pl.BlockSpec((TM, N_SPLIT * d), rev), row_f32, row_f32, row_f32, row_f32,
                  const((3, d, d)), const(wa.shape), const(wb.shape), const((1, d)), const((1, d))],
        out_specs=[pl.BlockSpec((TM, N_SPLIT * d), rev),
                   pl.BlockSpec((3, TM, d), lambda i: (0, _tile_block(nt - 1 - i, nt), 0)),
                   const((SM_ROWS, d))],
        out_shape=[jax.ShapeDtypeStruct((tp, N_SPLIT * d), BF16), jax.ShapeDtypeStruct((3, tp, d), BF16),
                   jax.ShapeDtypeStruct((SM_ROWS, d), F32)],
        scratch_shapes=[pltpu.VMEM((TM + HALO_A, d), F32), pltpu.VMEM((TM + HALO_B, d), F32),
                        pltpu.VMEM((shl, d), F32), pltpu.VMEM((TM, d), F32), pltpu.VMEM((TM, d), F32),
                        pltpu.VMEM((TM, d), F32), pltpu.VMEM((TM, d), F32),
                        pltpu.VMEM((SM_ROWS, 8, d), F32)],
        compiler_params=_params(("arbitrary",)),
    )(ds2, proj, ca, cb, ya, yb, w3, wa, wb, ln_g, ln_b)


def kernel(x, meta_tokens, norm_g, w_in, conv_a_w, conv_a_b, ln_a_g, ln_a_b, w_a_out, b_a_out, conv_b_w, w_b_out, w_out, final_g, loss_target, m_meta_tokens, m_norm_g, m_w_in, m_conv_a_w, m_conv_a_b, m_ln_a_g, m_ln_a_b, m_w_a_out, m_b_a_out, m_conv_b_w, m_w_b_out, m_w_out, m_final_g, v_meta_tokens, v_norm_g, v_w_in, v_conv_a_w, v_conv_a_b, v_ln_a_g, v_ln_a_b, v_w_a_out, v_b_a_out, v_conv_b_w, v_w_b_out, v_w_out, v_final_g):
    seq, d = x.shape[1], x.shape[2]
    dc = meta_tokens.shape[1]
    sw = w_in.shape[2]
    rsh = w_a_out.shape[1]
    xi, yi, ci = _mesh_pos()
    me = 2 * xi + yi
    pos = jnp.stack([ci, me]).astype(jnp.int32)

    conv_rows = HALO_A + HALO_B + 8
    convs = jnp.concatenate([
        jnp.pad(conv_a_w[0], ((0, HALO_A - CONV_A), (0, 0))),
        jnp.pad(conv_b_w[0], ((0, HALO_B - CONV_B), (0, 0))), jnp.zeros((8, dc), F32)], axis=0)[None]
    w3_own = jnp.stack([w_a_out[0], w_b_out[0], w_out[0]])
    fg2 = final_g.reshape(1, d)
    xs = x[0]

    h, front, placed = _h_prep(xs, meta_tokens, norm_g, [(w_in, BF16), (w3_own, BF16), (convs, F32)], pos)
    proj, (wg_in, wg3, convg) = _proj_fwd(h, placed, pos)
    w3 = wg3.reshape(3, N_CHIPS * rsh, d)
    convg = jnp.transpose(convg[0], (1, 0, 2)).reshape(conv_rows, N_CHIPS * dc)
    wa_full = convg[0:HALO_A]
    wb_full = convg[HALO_A:HALO_A + HALO_B]
    ca, cb, ya, yb, abm_t, ds2, h_t, loss8, dfg8 = _mix_fwd(
        xs, front, proj, loss_target[0], w3, wa_full, wb_full, conv_a_b, ln_a_g, ln_a_b, b_a_out, fg2, norm_g)
    dproj, d3, sm = _mix_bwd(ds2, proj, ca, cb, ya, yb, w3, wa_full, wb_full, ln_a_g, ln_a_b)
    cw_sq = _col_block(d, 512)
    per_sq = d // cw_sq
    p32_sq, pbf_sq, _ = _dw_reduced(
        abm_t, d3, cw_sq, 3 * per_sq, lambda t: (t // per_sq, t % per_sq), N_CHIPS,
        (3, N_CHIPS, rsh // 2, d), (None, N_CHIPS, rsh // 2, cw_sq),
        lambda u: (u // per_sq, 0, 0, u % per_sq), [], "dw_square")
    cw_in = _col_block(sw, 768)
    ncol = sw // cw_in
    p32_in, pbf_in, (l_sq,) = _dw_reduced(
        h_t[None], dproj[None], cw_in, N_CHIPS * ncol, lambda t: (0, t), 1,
        (1, N_CHIPS, d // 2, sw), (None, None, d // 2, cw_in), lambda u: (0, u // ncol, 0, u % ncol),
        [pbf_sq], "dw_in")
    grad_x, dfront, dng8, l_in = _dh_bwd(dproj, wg_in, xs, front, ds2, norm_g, pbf_in)
    half_in = _sum_chips([(p32_in, l_in)], sw, pos, "rs_sum_in")
    half_sq = _sum_chips([(p32_sq, l_sq)], d, pos, "rs_sum_sq")
    tail_row = lax.broadcasted_iota(jnp.int32, (8, d), 0)
    tail = jnp.where(tail_row == 0, dng8, jnp.where(tail_row == 1, dfg8,
                     jnp.where(tail_row == 2, loss8[0, 0], 0.0)))
    block = jnp.concatenate([sm, dfront[TM - N_META:TM], tail], axis=0)
    (other_in, other_sq), red = _sibling_swap([half_in, half_sq], block)
    col = lax.dynamic_slice(red, (0, me * dc), (AR_ROWS, dc))
    g_small = {
        "meta_tokens": col[ROW_DMETA:ROW_DMETA + N_META],
        "norm_g": red[ROW_DNG:ROW_DNG + 1],
        "conv_a_w": col[ROW_DWA:ROW_DWA + CONV_A][None],
        "conv_a_b": red[ROW_DCAB:ROW_DCAB + 1],
        "ln_a_g": red[ROW_DLNG:ROW_DLNG + 1],
        "ln_a_b": red[ROW_DLNB:ROW_DLNB + 1],
        "b_a_out": red[ROW_DBAO:ROW_DBAO + 1],
        "conv_b_w": col[ROW_DWB:ROW_DWB + CONV_B][None],
        "final_g": red[ROW_DFG],
    }

    upd_in = _adam_halves([w_in], [m_w_in], [v_w_in], half_in, other_in, pos, "adam_in")
    upd_sq = _adam_halves([w_a_out, w_b_out, w_out], [m_w_a_out, m_w_b_out, m_w_out],
                          [v_w_a_out, v_w_b_out, v_w_out], half_sq, other_sq, pos, "adam_sq")
    small_w = {"meta_tokens": (meta_tokens, m_meta_tokens, v_meta_tokens), "norm_g": (norm_g, m_norm_g, v_norm_g),
               "conv_a_w": (conv_a_w, m_conv_a_w, v_conv_a_w), "conv_a_b": (conv_a_b, m_conv_a_b, v_conv_a_b),
               "ln_a_g": (ln_a_g, m_ln_a_g, v_ln_a_g), "ln_a_b": (ln_a_b, m_ln_a_b, v_ln_a_b),
               "b_a_out": (b_a_out, m_b_a_out, v_b_a_out), "conv_b_w": (conv_b_w, m_conv_b_w, v_conv_b_w),
               "final_g": (final_g, m_final_g, v_final_g)}
    names_small = list(small_w)
    as2d = lambda t: t.reshape(-1, t.shape[-1])
    upd_small = _adam_small([(as2d(small_w[k][0]), as2d(g_small[k]), as2d(small_w[k][1]), as2d(small_w[k][2]))
                             for k in names_small])

    grads, deltas, new_m, new_v = dict(g_small), {}, {}, {}
    for k, upd in zip(names_small, upd_small):
        deltas[k], new_m[k], new_v[k] = [t.reshape(small_w[k][0].shape) for t in upd]
    grads["w_in"], deltas["w_in"], new_m["w_in"], new_v["w_in"] = upd_in
    for idx, k in enumerate(["w_a_out", "w_b_out", "w_out"]):
        grads[k], deltas[k], new_m[k], new_v[k] = upd_sq[4 * idx:4 * idx + 4]

    loss = red[ROW_LOSS, 0]
    order = ["meta_tokens", "norm_g", "w_in", "conv_a_w", "conv_a_b", "ln_a_g", "ln_a_b", "w_a_out", "b_a_out",
             "conv_b_w", "w_b_out", "w_out", "final_g"]
    return (loss, grad_x[None], *[grads[k] for k in order], *[deltas[k] for k in order],
            *[new_m[k] for k in order], *[new_v[k] for k in order])
```

```python
import jax
import jax.numpy as jnp
from jax import lax
from jax.experimental import pallas as pl
from jax.experimental.pallas import tpu as pltpu

F32 = jnp.float32
BF16 = jnp.bfloat16
MESH = pl.DeviceIdType.MESH

EPS = 1e-6
N_META = 16
N_SPLIT = 9
CONV_A = 31
CONV_B = 3
HALO_A = 32
HALO_B = 8
SHIFT_ROWS = 24
TM = 256
RB_FWD = 128
RB_BWD = 64
N_ROW_TILES_BIG = 8
ROW_BLOCK = 256
N_CHIPS = 4
VMEM_LIMIT = 56 * 1024 * 1024
VMEM_LIMIT_BIG = 62 * 1024 * 1024

ADAM_LR = 0.001
ADAM_B1 = 0.9
ADAM_B2 = 0.999
ADAM_EPS = 1e-08
ADAM_WD = 0.01
ADAM_STEP = 10

ROW_DWA = 0
ROW_DWB = 32
ROW_DCAB = 40
ROW_DLNG = 41
ROW_DLNB = 42
ROW_DBAO = 43
SM_ROWS = 48
ROW_DMETA = 48
ROW_DNG = 64
ROW_DFG = 65
ROW_LOSS = 66
AR_ROWS = 72


def _sigmoid(v):
    return 0.5 * jnp.tanh(0.5 * v) + 0.5


def _params(sem, **kw):
    return pltpu.CompilerParams(dimension_semantics=sem, vmem_limit_bytes=VMEM_LIMIT, **kw)


def _rows(rb, n):
    return pl.ds(pl.multiple_of(rb * n, n), n)


def _mesh_pos():
    x, y, c = lax.axis_index("x"), lax.axis_index("y"), lax.axis_index("c")
    return x, y, c


def _half(ref, j, c):
    h = ref.shape[2] // 2
    return ref.at[:, j, pl.ds(c * h, h), :]


class _Exchange:
    def __init__(self, sends, recvs):
        self.sends, self.recvs = sends, recvs

    def start(self):
        for cp in self.sends:
            cp.start()

    def finish(self):
        for cp in self.recvs:
            cp.wait_recv()
        for cp in self.sends:
            cp.wait_send()


def _chip_exchange(part_ref, land_ref, send_sems, recv_sems):
    x, y, c = _mesh_pos()
    me = 2 * x + y
    sends, recvs = [], []
    for k, (px, py) in enumerate([(1 - x, y), (x, 1 - y), (1 - x, 1 - y)]):
        sems = dict(send_sem=send_sems.at[k], recv_sem=recv_sems.at[k], device_id=(px, py, c), device_id_type=MESH)
        sends.append(pltpu.make_async_remote_copy(
            src_ref=part_ref.at[:, 2 * px + py], dst_ref=land_ref.at[:, me], **sems))
        landed = land_ref.at[:, 2 * px + py]
        recvs.append(pltpu.make_async_remote_copy(src_ref=landed, dst_ref=landed, **sems))
    return _Exchange(sends, recvs)

def _sibling_swap(halves, small):
    n = len(halves)

    def body(*refs):
        ins, small_ref, outs, red_ref = refs[:n], refs[n], refs[n + 1:2 * n + 1], refs[2 * n + 1]
        send_sems, recv_sems = refs[2 * n + 2:2 * n + 4]
        reduce = _SmallAllReduce(small_ref, red_ref, *refs[2 * n + 4:])
        x, y, c = _mesh_pos()
        copies = [pltpu.make_async_remote_copy(
            src_ref=ins[a], dst_ref=outs[a], send_sem=send_sems.at[a], recv_sem=recv_sems.at[a],
            device_id=(x, y, 1 - c), device_id_type=MESH) for a in range(n)]
        reduce.start()
        for cp in copies:
            cp.start()
        reduce.between_chips()
        reduce.finish()
        for cp in copies:
            cp.wait()

    any_spec = pl.BlockSpec(memory_space=pl.ANY)
    vm = pl.BlockSpec(memory_space=pltpu.VMEM)
    outs = pl.pallas_call(
        body, name="rs_swap",
        in_specs=[any_spec] * n + [vm], out_specs=[any_spec] * n + [vm],
        out_shape=[jax.ShapeDtypeStruct(h.shape, h.dtype) for h in halves]
        + [jax.ShapeDtypeStruct(small.shape, F32)],
        scratch_shapes=[pltpu.SemaphoreType.DMA((n,)), pltpu.SemaphoreType.DMA((n,))]
        + _SmallAllReduce.scratch(*small.shape),
    )(*halves, small)
    return outs[:n], outs[n]


class _SmallAllReduce:
    def __init__(self, x_ref, out_ref, sib_ref, part_ref, peers_ref, send_sems, recv_sems):
        self.x_ref, self.out_ref, self.sib_ref, self.part_ref, self.peers_ref = x_ref, out_ref, sib_ref, part_ref, peers_ref
        x, y, c = _mesh_pos()
        self.me = 2 * x + y
        self.swap = pltpu.make_async_remote_copy(
            src_ref=x_ref, dst_ref=sib_ref, send_sem=send_sems.at[0], recv_sem=recv_sems.at[0],
            device_id=(x, y, 1 - c), device_id_type=MESH)
        self.sends, self.recvs = [], []
        for k, (px, py) in enumerate([(1 - x, y), (x, 1 - y), (1 - x, 1 - y)]):
            sems = dict(send_sem=send_sems.at[1 + k], recv_sem=recv_sems.at[1 + k],
                        device_id=(px, py, c), device_id_type=MESH)
            self.sends.append(pltpu.make_async_remote_copy(src_ref=part_ref, dst_ref=peers_ref.at[self.me], **sems))
            landed = peers_ref.at[2 * px + py]
            self.recvs.append(pltpu.make_async_remote_copy(src_ref=landed, dst_ref=landed, **sems))

    @staticmethod
    def scratch(rows, d):
        return [pltpu.VMEM((rows, d), F32), pltpu.VMEM((rows, d), F32), pltpu.VMEM((N_CHIPS, rows, d), F32),
                pltpu.SemaphoreType.DMA((4,)), pltpu.SemaphoreType.DMA((4,))]

    def start(self):
        self.swap.start()

    def between_chips(self):
        self.swap.wait()
        self.part_ref[...] = self.x_ref[...] + self.sib_ref[...]
        self.peers_ref[self.me] = self.part_ref[...]
        for cp in self.sends:
            cp.start()

    def finish(self):
        for cp in self.recvs:
            cp.wait_recv()
        for cp in self.sends:
            cp.wait_send()
        p = self.peers_ref
        self.out_ref[...] = ((p[0] + p[1]) + p[2]) + p[3]


def _sum_chips(parts, cw, pos, name):
    s, _, h, _ = parts[0][0].shape
    hb = min(h, ROW_BLOCK)
    widths = [own.shape[3] // cw for own, _ in parts]
    starts = [sum(widths[:a]) for a in range(len(parts))]

    def body(pos_ref, *refs):
        out_ref = refs[-1]
        n = pl.program_id(2)
        total = None
        for a in range(len(parts)):
            own, l1, l2, l3 = refs[4 * a:4 * a + 4]
            val = ((own[...] + l1[...].astype(F32)) + l2[...].astype(F32)) + l3[...].astype(F32)
            total = val if total is None else jnp.where(n >= starts[a], val, total)
        out_ref[...] = total

    def slot(a, k):
        col = lambda n: jnp.clip(n - starts[a], 0, widths[a] - 1)
        return pl.BlockSpec((None, None, hb, cw),
                            lambda si, b, n, pos_ref: (si, (pos_ref[1] + k) % N_CHIPS, b, col(n)))

    operands, specs = [], []
    for a, (own, landed) in enumerate(parts):
        operands += [own, landed, landed, landed]
        specs += [slot(a, 0), slot(a, 1), slot(a, 2), slot(a, 3)]
    return pl.pallas_call(
        body, name=name,
        grid_spec=pltpu.PrefetchScalarGridSpec(
            num_scalar_prefetch=1, grid=(s, h // hb, sum(widths)), in_specs=specs,
            out_specs=pl.BlockSpec((None, hb, cw), lambda si, b, n, pos_ref: (si, b, n))),
        out_shape=jax.ShapeDtypeStruct((s, h, sum(widths) * cw), F32),
        compiler_params=_params(("arbitrary",) * 3),
    )(pos, *operands)


def _adamw(w, g, m, v):
    m = ADAM_B1 * m + (1.0 - ADAM_B1) * g
    v = ADAM_B2 * v + (1.0 - ADAM_B2) * (g * g)
    m_hat = m / (1.0 - ADAM_B1 ** ADAM_STEP)
    v_hat = v / (1.0 - ADAM_B2 ** ADAM_STEP)
    delta = -ADAM_LR * (m_hat / (jnp.sqrt(v_hat) + ADAM_EPS) + ADAM_WD * w)
    return delta, m, v


def _adam_halves(ws, ms, vs, g_own, g_recv, pos, name):
    n = len(ws)
    _, r, c = ws[0].shape
    h = r // 2
    rb = min(h, ROW_BLOCK)
    nb = h // rb

    def body(pos_ref, *refs):
        w_refs, m_refs, v_refs = refs[:n], refs[n:2 * n], refs[2 * n:3 * n]
        go_ref, gr_ref = refs[3 * n:3 * n + 2]
        outs = refs[3 * n + 2:]
        mine = pl.program_id(0) == pos_ref[0]
        for a in range(n):
            g = jnp.where(mine, go_ref[a], gr_ref[a])
            delta, m, v = _adamw(w_refs[a][...], g, m_refs[a][...], v_refs[a][...])
            outs[4 * a][...], outs[4 * a + 1][...], outs[4 * a + 2][...], outs[4 * a + 3][...] = g, delta, m, v

    spec_w = pl.BlockSpec((None, rb, c), lambda hf, b, pos_ref: (0, hf * nb + b, 0))
    spec_g = pl.BlockSpec((n, rb, c), lambda hf, b, pos_ref: (0, b, 0))
    return pl.pallas_call(
        body, name=name,
        grid_spec=pltpu.PrefetchScalarGridSpec(
            num_scalar_prefetch=1, grid=(2, nb), in_specs=[spec_w] * (3 * n) + [spec_g] * 2,
            out_specs=[spec_w] * (4 * n)),
        out_shape=[jax.ShapeDtypeStruct((1, r, c), F32)] * (4 * n),
        compiler_params=_params(("arbitrary",) * 2),
    )(pos, *ws, *ms, *vs, g_own, g_recv)


def _adam_small(items):
    n = len(items)

    def body(*refs):
        ins, outs = refs[:4 * n], refs[4 * n:]
        for a in range(n):
            w_ref, g_ref, m_ref, v_ref = ins[4 * a:4 * a + 4]
            d, m, v = _adamw(w_ref[...], g_ref[...], m_ref[...], v_ref[...])
            outs[3 * a][...] = d
            outs[3 * a + 1][...] = m
            outs[3 * a + 2][...] = v

    vm = pl.BlockSpec(memory_space=pltpu.VMEM)
    flat = [t for it in items for t in it]
    outs = pl.pallas_call(
        body, name="adam_small", in_specs=[vm] * (4 * n), out_specs=[vm] * (3 * n),
        out_shape=[jax.ShapeDtypeStruct(it[0].shape, F32) for it in items for _ in range(3)],
    )(*flat)
    return [tuple(outs[3 * a:3 * a + 3]) for a in range(n)]


def _big_row_spec(seq, tmb, d, tile_of):
    return pl.BlockSpec((pl.Element(tmb), pl.Element(d)),
                        lambda *args: (pl.multiple_of(jnp.minimum(tile_of(*args) * tmb, seq - tmb), 8), 0))


def _big_row_tile(x_ref, front_ref, i):
    rows = x_ref[...]
    last = jnp.concatenate([rows[TM:], front_ref[...]], axis=0)
    return jnp.where(i == N_ROW_TILES_BIG - 1, last, rows)


def _h_prep(x, meta, norm_g, shards, pos):
    seq, d = x.shape
    tp = seq + TM
    dc = meta.shape[1]
    tmb = tp // N_ROW_TILES_BIG
    assert tmb >= TM and tp == tmb * N_ROW_TILES_BIG
    last = N_ROW_TILES_BIG - 1
    ns = len(shards)

    def body(pos_ref, x_ref, meta_ref, g_ref, *refs):
        shard_refs, (h_ref, front_ref), placed_refs = refs[:ns], refs[ns:ns + 2], refs[ns + 2:2 * ns + 2]
        metas, msend, mrecv = refs[2 * ns + 2:]
        for a in range(ns):
            placed_refs[a][...] = shard_refs[a][...].astype(placed_refs[a].dtype)
        x, y, c = _mesh_pos()
        me = 2 * x + y
        chips = [(1 - x, y), (x, 1 - y), (1 - x, 1 - y)]
        i = pl.program_id(0)

        def meta_copy(k, chip):
            return pltpu.make_async_remote_copy(
                src_ref=metas.at[chip], dst_ref=metas.at[chip], send_sem=msend.at[k], recv_sem=mrecv.at[k],
                device_id=(*chips[k], c), device_id_type=MESH)

        @pl.when(i == 0)
        def _():
            metas[me] = meta_ref[...]
            for k in range(3):
                meta_copy(k, me).start()
            front_ref[...] = jnp.zeros_like(front_ref)

        @pl.when(i == last)
        def _():
            for k, (px, py) in enumerate(chips):
                meta_copy(k, 2 * px + py).wait_recv()
            for q in range(N_CHIPS):
                front_ref[TM - N_META:TM, q * dc:(q + 1) * dc] = metas[q]

        s = _big_row_tile(x_ref, front_ref, i)
        r = lax.rsqrt(jnp.mean(s * s, axis=-1, keepdims=True) + EPS)
        h_ref[...] = (s * r * g_ref[...]).astype(BF16)

        @pl.when(i == last)
        def _():
            for k in range(3):
                meta_copy(k, me).wait_send()

    shard_in, shard_out, shard_shapes = [], [], []
    for arr, dtype in shards:
        s, r, c = arr.shape
        sliced = r % (N_ROW_TILES_BIG * 16) == 0
        rp = r // N_ROW_TILES_BIG if sliced else r
        step = (lambda i: i) if sliced else (lambda i: 0)
        shard_in.append(pl.BlockSpec((s, rp, c), lambda i, pos_ref, step=step: (0, step(i), 0)))
        shard_out.append(pl.BlockSpec((s, None, rp, c), lambda i, pos_ref, step=step: (0, pos_ref[1], step(i), 0)))
        shard_shapes.append(jax.ShapeDtypeStruct((s, N_CHIPS, r, c), dtype))
    outs = pl.pallas_call(
        body, name="f0_norm",
        grid_spec=pltpu.PrefetchScalarGridSpec(
            num_scalar_prefetch=1, grid=(N_ROW_TILES_BIG,),
            in_specs=[_big_row_spec(seq, tmb, d, lambda i, pos_ref: i),
                      pl.BlockSpec(meta.shape, lambda i, pos_ref: (0, 0)),
                      pl.BlockSpec((1, d), lambda i, pos_ref: (0, 0))] + shard_in,
            out_specs=[pl.BlockSpec((tmb, d), lambda i, pos_ref: (i, 0)),
                       pl.BlockSpec((TM, d), lambda i, pos_ref: (0, 0))] + shard_out,
            scratch_shapes=[pltpu.VMEM((N_CHIPS,) + meta.shape, F32),
                            pltpu.SemaphoreType.DMA((3,)), pltpu.SemaphoreType.DMA((3,))]),
        out_shape=[jax.ShapeDtypeStruct((tp, d), BF16), jax.ShapeDtypeStruct((TM, d), F32)] + shard_shapes,
        compiler_params=_params(("arbitrary",)),
    )(pos, x, meta, norm_g, *[arr for arr, _ in shards])
    return outs[0], outs[1], outs[2:]


N_UNITS = 3
N_STEPS_PROJ = N_CHIPS * N_UNITS


def _proj_plan(u):
    v = u - N_UNITS
    if v < 2 * N_UNITS:
        return v % 2, v // 2
    return 2, v - 2 * N_UNITS


def _proj_unit(t, me):
    v = t - N_UNITS
    near = v < 2 * N_UNITS
    rel = jnp.where(near, v % 2, 2)
    unit = jnp.where(near, v // 2, v - 2 * N_UNITS)
    flip = jnp.where(rel == 0, 2, jnp.where(rel == 1, 1, 3))
    own = t < N_UNITS
    return jnp.where(own, me, lax.bitwise_xor(me, flip)), jnp.where(own, t, unit)


def _proj_fwd(h, bufs, pos):
    tp, d = h.shape
    _, nsh, _, sw = bufs[0].shape
    cu = sw // N_UNITS
    assert cu % 128 == 0
    n = len(bufs)
    w_sems = 6 * N_UNITS
    last = N_STEPS_PROJ - 1
    late = N_STEPS_PROJ - N_UNITS

    def body(pos_ref, h_ref, *refs):
        proj_ref = refs[n]
        gbufs = refs[n + 1:2 * n + 1]
        wbuf, wsems, send_sems, recv_sems = refs[2 * n + 1:]
        x, y, c = _mesh_pos()
        me = 2 * x + y
        sibling = (x, y, 1 - c)
        chips = [(1 - x, y), (x, 1 - y), (1 - x, 1 - y)]
        chip_ids = [2 * px + py for px, py in chips]
        relayed_chip = jnp.where(c == 0, chip_ids[0], chip_ids[1])
        relay_to = (jnp.where(c == 0, x, 1 - x), jnp.where(c == 0, 1 - y, y), c)
        t = pl.program_id(0)

        def remote(idx, piece, to):
            return pltpu.make_async_remote_copy(
                src_ref=piece, dst_ref=piece, send_sem=send_sems.at[idx], recv_sem=recv_sems.at[idx],
                device_id=to, device_id_type=MESH)

        hr = d // 2

        def chunk_of(chip, half, k):
            return gbufs[0].at[0, chip, pl.ds(half * hr, hr), pl.ds(k * cu, cu)]

        def own_chunk(r, k):
            return remote(6 * k + r, chunk_of(me, c, k), (*chips[r], c))

        def landed_chunk(r, k):
            return remote(6 * k + r, chunk_of(chip_ids[r], c, k), (*chips[r], c))

        def relay_chunk(k):
            return remote(6 * k + 2, chunk_of(relayed_chip, c, k), relay_to)

        def sibling_chunk(r, k, half):
            return remote(6 * k + 3 + r, chunk_of(chip_ids[r], half, k), sibling)

        def fetch(u):
            chip, unit = _proj_unit(jnp.int32(u), me)
            return pltpu.make_async_copy(gbufs[0].at[0, chip, :, pl.ds(pl.multiple_of(unit * cu, 128), cu)],
                                         wbuf.at[u % 2], wsems.at[u % 2])

        def make_available(u):
            r, k = _proj_plan(u)
            landed_chunk(r, k).wait_recv()
            if r < 2:
                pl.when(c == r)(lambda: relay_chunk(k).start())
            sibling_chunk(r, k, c).start()
            sibling_chunk(r, k, 1 - c).wait_recv()

        def own_piece(a, r):
            return remote(w_sems + 6 * (a - 1) + r, _half(gbufs[a], me, c), (*chips[r], c))

        def relay(a):
            return remote(w_sems + 6 * (a - 1) + 2, _half(gbufs[a], relayed_chip, c), relay_to)

        def to_sibling(a, r, core):
            return remote(w_sems + 6 * (a - 1) + 3 + r, _half(gbufs[a], chip_ids[r], core), sibling)

        def landed(a, r):
            return remote(w_sems + 6 * (a - 1) + r, _half(gbufs[a], chip_ids[r], c), (*chips[r], c))

        for u in range(N_STEPS_PROJ):
            @pl.when(t == u)
            def _(u=u):
                if u == 0:
                    for k in range(N_UNITS):
                        for r in range(2):
                            own_chunk(r, k).start()
                    for a in range(1, n):
                        for r in range(2):
                            own_piece(a, r).start()
                    fetch(0).start()
                if u < last:
                    if u + 1 >= N_UNITS:
                        make_available(u + 1)
                    fetch(u + 1).start()
                if u == late:
                    for a in range(1, n):
                        landed(a, 0).wait_recv()
                        landed(a, 1).wait_recv()
                        relay(a).start()
                        for r in range(2):
                            to_sibling(a, r, c).start()
                        for r in range(2):
                            to_sibling(a, r, 1 - c).wait_recv()
                fetch(u).wait()

        proj_ref[...] = jnp.dot(h_ref[...], wbuf[t % 2], preferred_element_type=F32).astype(BF16)

        @pl.when(t == last)
        def _():
            for a in range(1, n):
                landed(a, 2).wait_recv()
                to_sibling(a, 2, c).start()
                to_sibling(a, 2, 1 - c).wait_recv()
            for k in range(N_UNITS):
                for r in range(2):
                    own_chunk(r, k).wait_send()
                relay_chunk(k).wait_send()
                for r in range(3):
                    sibling_chunk(r, k, c).wait_send()
            for a in range(1, n):
                for r in range(2):
                    own_piece(a, r).wait_send()
                relay(a).wait_send()
                for r in range(3):
                    to_sibling(a, r, c).wait_send()

    def out_index(t, pos_ref):
        chip, unit = _proj_unit(t, pos_ref[1])
        return 0, chip * N_UNITS + unit

    any_spec = pl.BlockSpec(memory_space=pl.ANY)
    outs = pl.pallas_call(
        body, name="f1_proj",
        grid_spec=pltpu.PrefetchScalarGridSpec(
            num_scalar_prefetch=1, grid=(N_STEPS_PROJ,),
            in_specs=[pl.BlockSpec((tp, d), lambda t, pos_ref: (0, 0))] + [any_spec] * n,
            out_specs=[pl.BlockSpec((tp, cu), out_index)] + [any_spec] * n,
            scratch_shapes=[pltpu.VMEM((2, d, cu), BF16), pltpu.SemaphoreType.DMA((2,)),
                            pltpu.SemaphoreType.DMA((w_sems + 6 * (n - 1),)),
                            pltpu.SemaphoreType.DMA((w_sems + 6 * (n - 1),))]),
        out_shape=[jax.ShapeDtypeStruct((tp, nsh * sw), BF16)]
        + [jax.ShapeDtypeStruct(b.shape, b.dtype) for b in bufs],
        input_output_aliases={2 + a: 1 + a for a in range(n)},
        compiler_params=_params(("arbitrary",)),
    )(pos, h, *bufs)
    return outs[0], outs[1:]


def _dh_bwd(dproj, wg_in, x, front, ds2, norm_g, part):
    seq, d = x.shape
    tp = seq + TM
    _, nsh, _, sw = wg_in.shape
    tmb = tp // N_ROW_TILES_BIG
    tail = tmb - TM
    last = N_ROW_TILES_BIG - 1

    def body(dp_ref, w_hbm, x_ref, front_ref, ds2_ref, g_ref, part_ref, gx_hbm, dfront_ref, dng_ref, land_ref,
             wbuf, gacc, dsbuf, wsems, osems, send_sems, recv_sems):
        exchange = _chip_exchange(part_ref, land_ref, send_sems, recv_sems)
        i = pl.program_id(0)

        def x_rows_out(step):
            return pltpu.make_async_copy(dsbuf.at[step % 2], gx_hbm.at[pl.ds(step * tmb, tmb), :], osems.at[step % 2])

        last_out = pltpu.make_async_copy(dsbuf.at[last % 2, pl.ds(0, tail), :],
                                         gx_hbm.at[pl.ds(last * tmb, tail), :], osems.at[last % 2])

        def shard_in(j):
            return pltpu.make_async_copy(w_hbm.at[0, j], wbuf.at[j], wsems.at[j])

        @pl.when(i == 0)
        def _():
            exchange.start()
            gacc[...] = jnp.zeros_like(gacc)
            for j in range(nsh):
                shard_in(j).start()

        dh = None
        for j in range(nsh):
            pl.when(i == 0)(lambda j=j: shard_in(j).wait())
            part = lax.dot_general(dp_ref[:, j * sw:(j + 1) * sw], wbuf[j], (((1,), (1,)), ((), ())),
                                   preferred_element_type=F32)
            dh = part if dh is None else dh + part
        s = _big_row_tile(x_ref, front_ref, i)
        r = lax.rsqrt(jnp.mean(s * s, axis=-1, keepdims=True) + EPS)
        gacc[...] += (dh * s * r).reshape(tmb // 8, 8, d).sum(axis=0)
        t = dh * g_ref[...]

        @pl.when(i >= 2)
        def _():
            x_rows_out(i - 2).wait()

        dsbuf[i % 2] = ds2_ref[...] + r * t - s * (r * r * r) * jnp.mean(t * s, axis=-1, keepdims=True)

        @pl.when(i < last)
        def _():
            x_rows_out(i).start()

        @pl.when(i == last)
        def _():
            last_out.start()
            dfront_ref[...] = dsbuf[last % 2, tail:, :]
            dng_ref[...] = jnp.broadcast_to(jnp.sum(gacc[...], axis=0, keepdims=True), (8, d))
            exchange.finish()
            x_rows_out(last - 1).wait()
            last_out.wait()

    any_spec = pl.BlockSpec(memory_space=pl.ANY)
    return pl.pallas_call(
        body, name="b2_dh", grid=(N_ROW_TILES_BIG,),
        in_specs=[pl.BlockSpec((tmb, nsh * sw), lambda i: (i, 0)), any_spec,
                  _big_row_spec(seq, tmb, d, lambda i: i),
                  pl.BlockSpec((TM, d), lambda i: (0, 0)),
                  pl.BlockSpec((tmb, d), lambda i: (i, 0)),
                  pl.BlockSpec((1, d), lambda i: (0, 0)), any_spec],
        out_specs=[any_spec, pl.BlockSpec((TM, d), lambda i: (0, 0)),
                   pl.BlockSpec((8, d), lambda i: (0, 0)), any_spec],
        out_shape=[jax.ShapeDtypeStruct((seq, d), F32), jax.ShapeDtypeStruct((TM, d), F32),
                   jax.ShapeDtypeStruct((8, d), F32), jax.ShapeDtypeStruct(part.shape, part.dtype)],
        scratch_shapes=[pltpu.VMEM((nsh, d, sw), BF16), pltpu.VMEM((8, d), F32), pltpu.VMEM((2, tmb, d), F32),
                        pltpu.SemaphoreType.DMA((nsh,)), pltpu.SemaphoreType.DMA((2,)),
                        pltpu.SemaphoreType.DMA((3,)), pltpu.SemaphoreType.DMA((3,))],
        compiler_params=pltpu.CompilerParams(dimension_semantics=("arbitrary",),
                                             vmem_limit_bytes=VMEM_LIMIT_BIG),
    )(dproj, wg_in, x, front, ds2, norm_g, part)


def _col_block(width, cap):
    return max(b for b in range(128, cap + 1, 128) if width % b == 0)


def _dw_reduced(lhs_t, rhs, cw, nblk, operands, groups, out_dims, out_block, out_index, carried, name):
    na, d, tp = lhs_t.shape
    rg = d // groups
    hh = rg // 2
    nc = len(carried)

    def body(*refs):
        l_ref, r_ref = refs[:2]
        part_refs = refs[2:2 + nc]
        p32_ref, pbf_ref = refs[2 + nc:4 + nc]
        land_refs = refs[4 + nc:4 + 2 * nc]
        res, rbuf, send_sems, recv_sems = refs[4 + 2 * nc:8 + 2 * nc]
        xsems = refs[8 + 2 * nc:]
        exchanges = [_chip_exchange(part_refs[e], land_refs[e], xsems[2 * e], xsems[2 * e + 1]) for e in range(nc)]
        exchange = _Exchange([s for ex in exchanges for s in ex.sends], [r for ex in exchanges for r in ex.recvs])
        x, y, c = _mesh_pos()
        t = pl.program_id(0)
        u = jnp.maximum(t - 1, 0)

        def to_sibling(blk):
            return pltpu.make_async_remote_copy(
                src_ref=res.at[blk % 2, :, pl.ds((1 - c) * hh, hh), :], dst_ref=rbuf.at[blk % 2],
                send_sem=send_sems.at[blk], recv_sem=recv_sems.at[blk],
                device_id=(x, y, 1 - c), device_id_type=MESH)

        @pl.when(t == 0)
        def _():
            exchange.start()

        @pl.when(t < nblk)
        def _():
            res[t % 2] = jnp.dot(l_ref[...], r_ref[...], preferred_element_type=F32).reshape(groups, rg, cw)

        @pl.when(t >= 1)
        def _():
            to_sibling(u).wait_recv()
            p = res[u % 2, :, pl.ds(c * hh, hh), :] + rbuf[u % 2]
            p32_ref[...] = p.reshape(p32_ref.shape)
            pbf_ref[...] = p.reshape(pbf_ref.shape).astype(BF16)

        @pl.when(t < nblk)
        def _():
            to_sibling(t).start()

        @pl.when(t >= 1)
        def _():
            to_sibling(u).wait_send()

        @pl.when(t == nblk)
        def _():
            exchange.finish()

    any_spec = pl.BlockSpec(memory_space=pl.ANY)
    last = nblk - 1
    out_spec = pl.BlockSpec(out_block, lambda t: out_index(jnp.maximum(t - 1, 0)))
    outs = pl.pallas_call(
        body, name=name, grid=(nblk + 1,),
        in_specs=[pl.BlockSpec((None, d, tp), lambda t: (operands(jnp.minimum(t, last))[0], 0, 0)),
                  pl.BlockSpec((None, tp, cw), lambda t: (operands(jnp.minimum(t, last))[0], 0,
                                                          operands(jnp.minimum(t, last))[1]))]
        + [any_spec] * nc,
        out_specs=[out_spec, out_spec] + [any_spec] * nc,
        out_shape=[jax.ShapeDtypeStruct(out_dims, F32), jax.ShapeDtypeStruct(out_dims, BF16)]
        + [jax.ShapeDtypeStruct(e.shape, e.dtype) for e in carried],
        scratch_shapes=[pltpu.VMEM((2, groups, rg, cw), F32), pltpu.VMEM((2, groups, hh, cw), F32),
                        pltpu.SemaphoreType.DMA((nblk,)), pltpu.SemaphoreType.DMA((nblk,))]
        + [pltpu.SemaphoreType.DMA((3,)), pltpu.SemaphoreType.DMA((3,))] * nc,
        compiler_params=_params(("arbitrary",)),
    )(lhs_t, rhs, *carried)
    return outs[0], outs[1], outs[2:]


def _conv_a_taps(first_lag, last_lag):
    out = []
    for r in range(8):
        taps = [(q, 8 * q + r) for q in range(5) if first_lag <= 8 * q + r <= last_lag]
        if taps:
            out.append((r, taps))
    return out


def _tile_block(i, nt):
    return jnp.where(i == 0, nt - 1, i - 1)


def _mix_fwd(x, front, proj, target, w3, wa, wb, conv_a_b, ln_g, ln_b, b_a_out, final_g, norm_g):
    seq, d = x.shape
    tp = seq + TM
    nt = tp // TM
    RB = RB_FWD
    nrb = TM // RB
    shl = TM + SHIFT_ROWS

    def body(x_ref, front_ref, proj_ref, tgt_ref, w3_ref, wa_ref, wb_ref, cab_ref, lng_ref, lnb_ref, bao_ref, fg_ref,
             ng_ref, ca_ref, cb_ref, ya_ref, yb_ref, abmt_ref, ds2_ref, ht_ref, loss_ref, dfg_ref,
             abm_ref, ext_a, ext_b, sh, s2_s, lacc, gacc):
        i = pl.program_id(0)

        def split(k, rows):
            return proj_ref[rows, k * d:(k + 1) * d].astype(F32)

        def s_tile():
            return jnp.where(i == 0, front_ref[...], x_ref[...])

        s_in = s_tile()
        h = s_in * lax.rsqrt(jnp.mean(s_in * s_in, axis=-1, keepdims=True) + EPS) * ng_ref[...]
        ht_ref[...] = h.astype(BF16).T

        @pl.when(i == 0)
        def _():
            ext_a[0:HALO_A, :] = jnp.zeros((HALO_A, d), F32)
            ext_b[0:HALO_B, :] = jnp.zeros((HALO_B, d), F32)
            lacc[...] = jnp.zeros_like(lacc)
            gacc[...] = jnp.zeros_like(gacc)

        def conv_in(rb, carry):
            rows = _rows(rb, RB)
            ua0 = split(0, rows) * _sigmoid(split(1, rows))
            ext_a[pl.ds(pl.multiple_of(HALO_A + rb * RB, 8), RB), :] = ua0
            ext_b[pl.ds(pl.multiple_of(HALO_B + rb * RB, 8), RB), :] = split(4, rows) * split(5, rows)
            ca_ref[rows, :] = jnp.broadcast_to(cab_ref[...], (RB, d))
            return carry
        lax.fori_loop(0, nrb, conv_in, 0)

        @pl.when(i == 0)
        def _():
            abmt_ref[...] = jnp.zeros_like(abmt_ref)
            ds2_ref[...] = jnp.zeros_like(ds2_ref)

        @pl.when(i > 0)
        def _():
            tile_after_conv_inputs(split, s_tile, tgt_ref, w3_ref, wa_ref, wb_ref, lng_ref, lnb_ref, bao_ref, fg_ref,
                                   ca_ref, cb_ref, ya_ref, yb_ref, abmt_ref, ds2_ref, abm_ref, ext_a, ext_b, sh,
                                   s2_s, lacc, gacc)

        ext_a[0:HALO_A, :] = ext_a[TM:TM + HALO_A, :]
        ext_b[0:HALO_B, :] = ext_b[TM:TM + HALO_B, :]

        @pl.when(i == nt - 1)
        def _():
            loss_ref[...] = jnp.broadcast_to(0.5 * jnp.sum(lacc[...]) * (1.0 / d), (8, 128))
            dfg_ref[...] = jnp.broadcast_to(jnp.sum(gacc[...], axis=0, keepdims=True), (8, d))

    def tile_after_conv_inputs(split, s_tile, tgt_ref, w3_ref, wa_ref, wb_ref, lng_ref, lnb_ref, bao_ref, fg_ref,
                               ca_ref, cb_ref, ya_ref, yb_ref, abmt_ref, ds2_ref, abm_ref, ext_a, ext_b, sh, s2_s,
                               lacc, gacc):
        for r, taps in _conv_a_taps(HALO_A - CONV_A + 1, HALO_A):
            if r == 0:
                src = ext_a
            else:
                sh[...] = ext_a[r:r + shl, :]
                src = sh

            def conv_acc(rb, carry, src=src, taps=taps):
                rows = _rows(rb, RB)
                acc = ca_ref[rows, :]
                for q, lag in taps:
                    k = lag - (HALO_A - CONV_A + 1)
                    acc = acc + src[pl.ds(pl.multiple_of(rb * RB + 8 * q, 8), RB), :] * wa_ref[k:k + 1, :]
                ca_ref[rows, :] = acc
                return carry
            lax.fori_loop(0, nrb, conv_acc, 0)

        cb_ref[...] = ext_b[HALO_B:HALO_B + TM, :] * wb_ref[2:3, :]
        for k in range(CONV_B - 1):
            off = HALO_B - CONV_B + 1 + k
            sh[0:TM, :] = ext_b[off:off + TM, :]
            cb_ref[...] += sh[0:TM, :] * wb_ref[k:k + 1, :]

        def branches(rb, carry):
            rows = _rows(rb, RB)
            ca = ca_ref[rows, :]
            mu = jnp.mean(ca, axis=-1, keepdims=True)
            xc = ca - mu
            rstd = lax.rsqrt(jnp.mean(xc * xc, axis=-1, keepdims=True) + EPS)
            ln = xc * rstd * lng_ref[...] + lnb_ref[...]
            ua = ln * _sigmoid(ln)
            a_z = split(2, rows)
            abm_ref[0, rows, :] = (ua * (a_z * _sigmoid(a_z))).astype(BF16)
            return carry
        lax.fori_loop(0, nrb, branches, 0)

        def branch_b(rb, carry):
            rows = _rows(rb, RB)
            b_z = split(6, rows)
            ub = split(3, rows) * cb_ref[rows, :]
            abm_ref[1, rows, :] = (ub * (b_z * _sigmoid(b_z))).astype(BF16)
            return carry
        lax.fori_loop(0, nrb, branch_b, 0)

        ya_ref[...] = jnp.dot(abm_ref[0], w3_ref[0], preferred_element_type=F32) + bao_ref[...]
        yb_ref[...] = jnp.dot(abm_ref[1], w3_ref[1], preferred_element_type=F32)

        def merge(rb, carry):
            rows = _rows(rb, RB)
            m = _sigmoid(split(7, rows)) * ya_ref[rows, :] + _sigmoid(split(8, rows)) * yb_ref[rows, :]
            abm_ref[2, rows, :] = m.astype(BF16)
            return carry
        lax.fori_loop(0, nrb, merge, 0)

        s2_s[...] = s_tile() + jnp.dot(abm_ref[2], w3_ref[2], preferred_element_type=F32)
        for k in range(3):
            abmt_ref[k] = abm_ref[k].T

        def head(rb, carry):
            rows = _rows(rb, RB)
            s2 = s2_s[rows, :]
            r2 = lax.rsqrt(jnp.mean(s2 * s2, axis=-1, keepdims=True) + EPS)
            diff = s2 * r2 * fg_ref[...] - tgt_ref[rows, :]
            lacc[...] += diff * diff
            dy = diff * (1.0 / d)
            gacc[...] += (dy * s2 * r2).reshape(RB // 8, 8, d).sum(axis=0)
            t = dy * fg_ref[...]
            ds2_ref[rows, :] = r2 * t - s2 * (r2 * r2 * r2) * jnp.mean(t * s2, axis=-1, keepdims=True)
            return carry
        lax.fori_loop(0, nrb, head, 0)

    row_f32 = pl.BlockSpec((TM, d), lambda i: (_tile_block(i, nt), 0))
    x_rows = pl.BlockSpec((TM, d), lambda i: (jnp.maximum(i - 1, 0), 0))
    const = lambda shape: pl.BlockSpec(shape, lambda i: (0,) * len(shape))
    return pl.pallas_call(
        body, name="f2_mix", grid=(nt,),
        in_specs=[x_rows, const((TM, d)),
                  pl.BlockSpec((TM, N_SPLIT * d), lambda i: (_tile_block(i, nt), 0)),
                  x_rows,
                  const((3, d, d)), const(wa.shape), const(wb.shape)] + [const((1, d))] * 6,
        out_specs=[row_f32, row_f32, row_f32, row_f32,
                   pl.BlockSpec((3, d, TM), lambda i: (0, 0, _tile_block(i, nt))),
                   row_f32, pl.BlockSpec((d, TM), lambda i: (0, _tile_block(i, nt))),
                   const((8, 128)), const((8, d))],
        out_shape=[jax.ShapeDtypeStruct((tp, d), F32)] * 4
        + [jax.ShapeDtypeStruct((3, d, tp), BF16), jax.ShapeDtypeStruct((tp, d), F32),
           jax.ShapeDtypeStruct((d, tp), BF16),
           jax.ShapeDtypeStruct((8, 128), F32), jax.ShapeDtypeStruct((8, d), F32)],
        scratch_shapes=[pltpu.VMEM((3, TM, d), BF16),
                        pltpu.VMEM((HALO_A + TM, d), F32), pltpu.VMEM((HALO_B + TM, d), F32),
                        pltpu.VMEM((shl, d), F32), pltpu.VMEM((TM, d), F32),
                        pltpu.VMEM((RB, d), F32), pltpu.VMEM((8, d), F32)],
        compiler_params=_params(("arbitrary",)),
    )(x, front, proj, target, w3, wa, wb, conv_a_b, ln_g, ln_b, b_a_out, final_g, norm_g)


def _mix_bwd(ds2, proj, ca, cb, ya, yb, w3, wa, wb, ln_g, ln_b):
    tp, d = ds2.shape
    nt = tp // TM
    RB = RB_BWD
    nrb = TM // RB
    shl = TM + SHIFT_ROWS
    nt_dims = (((1,), (1,)), ((), ()))

    def body(ds2_ref, proj_ref, ca_ref, cb_ref, ya_ref, yb_ref, w3_ref, wa_ref, wb_ref, lng_ref, lnb_ref,
             dproj_ref, d3_ref, sm_ref, ext_d, ext_e, sh, dm_s, dpa_s, dpb_s, dua0_s, acc):
        step = pl.program_id(0)

        def split(k, rows):
            return proj_ref[rows, k * d:(k + 1) * d].astype(F32)

        def put(k, rows, val):
            dproj_ref[rows, k * d:(k + 1) * d] = val.astype(BF16)

        def accum(row, val):
            acc[row] += val.reshape(RB // 8, 8, d).sum(axis=0)

        @pl.when(step == 0)
        def _():
            ext_d[TM:TM + HALO_A, :] = jnp.zeros((HALO_A, d), F32)
            ext_e[TM:TM + HALO_B, :] = jnp.zeros((HALO_B, d), F32)
            acc[...] = jnp.zeros_like(acc)

        front = step == nt - 1

        @pl.when(front)
        def _():
            d3_ref[...] = jnp.zeros_like(d3_ref)

            def conv_only(rb, carry):
                rows = _rows(rb, RB)
                zeros = jnp.zeros((RB, d), F32)
                for k in (2, 3, 6, 7, 8):
                    put(k, rows, zeros)
                ext_d[rows, :] = zeros
                ext_e[rows, :] = zeros
                dua0_s[rows, :] = zeros
                dm_s[rows, :] = split(0, rows) * _sigmoid(split(1, rows))
                return carry
            lax.fori_loop(0, nrb, conv_only, 0)

        @pl.when(jnp.logical_not(front))
        def _():
            tile_to_conv_outputs(split, put, accum, ds2_ref, ca_ref, cb_ref, ya_ref, yb_ref, w3_ref, lng_ref, lnb_ref,
                                 d3_ref, ext_d, ext_e, dm_s, dpa_s, dpb_s, dua0_s)

        tile_conv_transposes(split, put, accum, wa_ref, wb_ref, ext_d, ext_e, sh, dm_s, dpb_s, dua0_s)

        @pl.when(front)
        def _():
            for row in range(SM_ROWS):
                sm_ref[row:row + 1, :] = jnp.sum(acc[row], axis=0, keepdims=True)

    def tile_to_conv_outputs(split, put, accum, ds2_ref, ca_ref, cb_ref, ya_ref, yb_ref, w3_ref, lng_ref, lnb_ref,
                             d3_ref, ext_d, ext_e, dm_s, dpa_s, dpb_s, dua0_s):
        d3_ref[2] = ds2_ref[...].astype(BF16)
        dm_s[...] = lax.dot_general(d3_ref[2], w3_ref[2], nt_dims, preferred_element_type=F32)

        def gates(rb, carry):
            rows = _rows(rb, RB)
            dm = dm_s[rows, :]
            sa = _sigmoid(split(7, rows))
            sb = _sigmoid(split(8, rows))
            ya_v = ya_ref[rows, :]
            yb_v = yb_ref[rows, :]
            put(7, rows, dm * ya_v * sa * (1.0 - sa))
            put(8, rows, dm * yb_v * sb * (1.0 - sb))
            dya = dm * sa
            accum(ROW_DBAO, dya)
            d3_ref[0, rows, :] = dya.astype(BF16)
            d3_ref[1, rows, :] = (dm * sb).astype(BF16)
            return carry
        lax.fori_loop(0, nrb, gates, 0)

        dpa_s[...] = lax.dot_general(d3_ref[0], w3_ref[0], nt_dims, preferred_element_type=F32)
        dpb_s[...] = lax.dot_general(d3_ref[1], w3_ref[1], nt_dims, preferred_element_type=F32)

        def branch_a(rb, carry):
            rows = _rows(rb, RB)
            ca_v = ca_ref[rows, :]
            mu = jnp.mean(ca_v, axis=-1, keepdims=True)
            xc = ca_v - mu
            rstd = lax.rsqrt(jnp.mean(xc * xc, axis=-1, keepdims=True) + EPS)
            xhat = xc * rstd
            ln = xhat * lng_ref[...] + lnb_ref[...]
            sl = _sigmoid(ln)
            ua = ln * sl
            a_z = split(2, rows)
            sz = _sigmoid(a_z)
            dpa = dpa_s[rows, :]
            put(2, rows, dpa * ua * (sz * (1.0 + a_z * (1.0 - sz))))
            dln = dpa * (a_z * sz) * (sl * (1.0 + ln * (1.0 - sl)))
            accum(ROW_DLNG, dln * xhat)
            accum(ROW_DLNB, dln)
            dxh = dln * lng_ref[...]
            dca = rstd * (dxh - jnp.mean(dxh, axis=-1, keepdims=True)
                          - xhat * jnp.mean(dxh * xhat, axis=-1, keepdims=True))
            accum(ROW_DCAB, dca)
            ext_d[rows, :] = dca
            return carry
        lax.fori_loop(0, nrb, branch_a, 0)

        def branch_b(rb, carry):
            rows = _rows(rb, RB)
            dua0_s[rows, :] = jnp.zeros((RB, d), F32)
            dm_s[rows, :] = split(0, rows) * _sigmoid(split(1, rows))
            b_z = split(6, rows)
            szb = _sigmoid(b_z)
            dpb = dpb_s[rows, :]
            b_b = split(3, rows)
            cb_v = cb_ref[rows, :]
            put(6, rows, dpb * (b_b * cb_v) * (szb * (1.0 + b_z * (1.0 - szb))))
            dub = dpb * (b_z * szb)
            put(3, rows, dub * cb_v)
            ext_e[rows, :] = dub * b_b
            return carry
        lax.fori_loop(0, nrb, branch_b, 0)

    def tile_conv_transposes(split, put, accum, wa_ref, wb_ref, ext_d, ext_e, sh, dm_s, dpb_s, dua0_s):
        for r, taps in _conv_a_taps(0, CONV_A - 1):
            if r == 0:
                src = ext_d
            else:
                sh[...] = ext_d[r:r + shl, :]
                src = sh

            def conv_t(rb, carry, src=src, taps=taps):
                rows = _rows(rb, RB)
                ua0 = dm_s[rows, :]
                dua0 = dua0_s[rows, :]
                for q, lag in taps:
                    k = CONV_A - 1 - lag
                    slab = src[pl.ds(pl.multiple_of(rb * RB + 8 * q, 8), RB), :]
                    dua0 = dua0 + slab * wa_ref[k:k + 1, :]
                    accum(ROW_DWA + k, slab * ua0)
                dua0_s[rows, :] = dua0
                return carry
            lax.fori_loop(0, nrb, conv_t, 0)
        ext_d[TM:TM + HALO_A, :] = ext_d[0:HALO_A, :]

        dpb_s[...] = ext_e[0:TM, :] * wb_ref[CONV_B - 1:CONV_B, :]
        for lag in range(CONV_B):
            k = CONV_B - 1 - lag
            if lag > 0:
                sh[0:TM, :] = ext_e[lag:lag + TM, :]
                dpb_s[...] += sh[0:TM, :] * wb_ref[k:k + 1, :]
            src = ext_e if lag == 0 else sh

            def conv_b_w(rb, carry, src=src, k=k):
                rows = _rows(rb, RB)
                accum(ROW_DWB + k, src[rows, :] * (split(4, rows) * split(5, rows)))
                return carry
            lax.fori_loop(0, nrb, conv_b_w, 0)
        ext_e[TM:TM + HALO_B, :] = ext_e[0:HALO_B, :]

        def inputs(rb, carry):
            rows = _rows(rb, RB)
            dua0 = dua0_s[rows, :]
            a_val = split(0, rows)
            sg = _sigmoid(split(1, rows))
            put(0, rows, dua0 * sg)
            put(1, rows, dua0 * a_val * sg * (1.0 - sg))
            dcbin = dpb_s[rows, :]
            put(4, rows, dcbin * split(5, rows))
            put(5, rows, dcbin * split(4, rows))
            return carry
        lax.fori_loop(0, nrb, inputs, 0)

    rev = lambda i: (_tile_block(nt - 1 - i, nt), 0)
    row_f32 = pl.BlockSpec((TM, d), rev)
    const = lambda shape: pl.BlockSpec(shape, lambda i: (0,) * len(shape))
    return pl.pallas_call(
        body, name="b1_mix", grid=(nt,),
        in_specs=[row_f32, pl.BlockSpec((TM, N_SPLIT * d), rev), row_f32, row_f32, row_f32, row_f32,
                  const((3, d, d)), const(wa.shape), const(wb.shape), const((1, d)), const((1, d))],
        out_specs=[pl.BlockSpec((TM, N_SPLIT * d), rev),
                   pl.BlockSpec((3, TM, d), lambda i: (0, _tile_block(nt - 1 - i, nt), 0)),
                   const((SM_ROWS, d))],
        out_shape=[jax.ShapeDtypeStruct((tp, N_SPLIT * d), BF16), jax.ShapeDtypeStruct((3, tp, d), BF16),
                   jax.ShapeDtypeStruct((SM_ROWS, d), F32)],
        scratch_shapes=[pltpu.VMEM((TM + HALO_A, d), F32), pltpu.VMEM((TM + HALO_B, d), F32),
                        pltpu.VMEM((shl, d), F32), pltpu.VMEM((TM, d), F32), pltpu.VMEM((TM, d), F32),
                        pltpu.VMEM((TM, d), F32), pltpu.VMEM((TM, d), F32),
                        pltpu.VMEM((SM_ROWS, 8, d), F32)],
        compiler_params=_params(("arbitrary",)),
    )(ds2, proj, ca, cb, ya, yb, w3, wa, wb, ln_g, ln_b)


def kernel(x, meta_tokens, norm_g, w_in, conv_a_w, conv_a_b, ln_a_g, ln_a_b, w_a_out, b_a_out, conv_b_w, w_b_out, w_out, final_g, loss_target, m_meta_tokens, m_norm_g, m_w_in, m_conv_a_w, m_conv_a_b, m_ln_a_g, m_ln_a_b, m_w_a_out, m_b_a_out, m_conv_b_w, m_w_b_out, m_w_out, m_final_g, v_meta_tokens, v_norm_g, v_w_in, v_conv_a_w, v_conv_a_b, v_ln_a_g, v_ln_a_b, v_w_a_out, v_b_a_out, v_conv_b_w, v_w_b_out, v_w_out, v_final_g):
    seq, d = x.shape[1], x.shape[2]
    dc = meta_tokens.shape[1]
    sw = w_in.shape[2]
    rsh = w_a_out.shape[1]
    xi, yi, ci = _mesh_pos()
    me = 2 * xi + yi
    pos = jnp.stack([ci, me]).astype(jnp.int32)

    conv_rows = HALO_A + HALO_B + 8
    convs = jnp.concatenate([
        jnp.pad(conv_a_w[0], ((0, HALO_A - CONV_A), (0, 0))),
        jnp.pad(conv_b_w[0], ((0, HALO_B - CONV_B), (0, 0))), jnp.zeros((8, dc), F32)], axis=0)[None]
    w3_own = jnp.stack([w_a_out[0], w_b_out[0], w_out[0]])
    fg2 = final_g.reshape(1, d)
    xs = x[0]

    h, front, placed = _h_prep(xs, meta_tokens, norm_g, [(w_in, BF16), (w3_own, BF16), (convs, F32)], pos)
    proj, (wg_in, wg3, convg) = _proj_fwd(h, placed, pos)
    w3 = wg3.reshape(3, N_CHIPS * rsh, d)
    convg = jnp.transpose(convg[0], (1, 0, 2)).reshape(conv_rows, N_CHIPS * dc)
    wa_full = convg[0:HALO_A]
    wb_full = convg[HALO_A:HALO_A + HALO_B]
    ca, cb, ya, yb, abm_t, ds2, h_t, loss8, dfg8 = _mix_fwd(
        xs, front, proj, loss_target[0], w3, wa_full, wb_full, conv_a_b, ln_a_g, ln_a_b, b_a_out, fg2, norm_g)
    dproj, d3, sm = _mix_bwd(ds2, proj, ca, cb, ya, yb, w3, wa_full, wb_full, ln_a_g, ln_a_b)
    cw_sq = _col_block(d, 512)
    per_sq = d // cw_sq
    p32_sq, pbf_sq, _ = _dw_reduced(
        abm_t, d3, cw_sq, 3 * per_sq, lambda t: (t // per_sq, t % per_sq), N_CHIPS,
        (3, N_CHIPS, rsh // 2, d), (None, N_CHIPS, rsh // 2, cw_sq),
        lambda u: (u // per_sq, 0, 0, u % per_sq), [], "dw_square")
    cw_in = _col_block(sw, 768)
    ncol = sw // cw_in
    p32_in, pbf_in, (l_sq,) = _dw_reduced(
        h_t[None], dproj[None], cw_in, N_CHIPS * ncol, lambda t: (0, t), 1,
        (1, N_CHIPS, d // 2, sw), (None, None, d // 2, cw_in), lambda u: (0, u // ncol, 0, u % ncol),
        [pbf_sq], "dw_in")
    grad_x, dfront, dng8, l_in = _dh_bwd(dproj, wg_in, xs, front, ds2, norm_g, pbf_in)
    half_in = _sum_chips([(p32_in, l_in)], sw, pos, "rs_sum_in")
    half_sq = _sum_chips([(p32_sq, l_sq)], d, pos, "rs_sum_sq")
    tail_row = lax.broadcasted_iota(jnp.int32, (8, d), 0)
    tail = jnp.where(tail_row == 0, dng8, jnp.where(tail_row == 1, dfg8,
                     jnp.where(tail_row == 2, loss8[0, 0], 0.0)))
    block = jnp.concatenate([sm, dfront[TM - N_META:TM], tail], axis=0)
    (other_in, other_sq), red = _sibling_swap([half_in, half_sq], block)
    col = lax.dynamic_slice(red, (0, me * dc), (AR_ROWS, dc))
    g_small = {
        "meta_tokens": col[ROW_DMETA:ROW_DMETA + N_META],
        "norm_g": red[ROW_DNG:ROW_DNG + 1],
        "conv_a_w": col[ROW_DWA:ROW_DWA + CONV_A][None],
        "conv_a_b": red[ROW_DCAB:ROW_DCAB + 1],
        "ln_a_g": red[ROW_DLNG:ROW_DLNG + 1],
        "ln_a_b": red[ROW_DLNB:ROW_DLNB + 1],
        "b_a_out": red[ROW_DBAO:ROW_DBAO + 1],
        "conv_b_w": col[ROW_DWB:ROW_DWB + CONV_B][None],
        "final_g": red[ROW_DFG],
    }

    upd_in = _adam_halves([w_in], [m_w_in], [v_w_in], half_in, other_in, pos, "adam_in")
    upd_sq = _adam_halves([w_a_out, w_b_out, w_out], [m_w_a_out, m_w_b_out, m_w_out],
                          [v_w_a_out, v_w_b_out, v_w_out], half_sq, other_sq, pos, "adam_sq")
    small_w = {"meta_tokens": (meta_tokens, m_meta_tokens, v_meta_tokens), "norm_g": (norm_g, m_norm_g, v_norm_g),
               "conv_a_w": (conv_a_w, m_conv_a_w, v_conv_a_w), "conv_a_b": (conv_a_b, m_conv_a_b, v_conv_a_b),
               "ln_a_g": (ln_a_g, m_ln_a_g, v_ln_a_g), "ln_a_b": (ln_a_b, m_ln_a_b, v_ln_a_b),
               "b_a_out": (b_a_out, m_b_a_out, v_b_a_out), "conv_b_w": (conv_b_w, m_conv_b_w, v_conv_b_w),
               "final_g": (final_g, m_final_g, v_final_g)}
    names_small = list(small_w)
    as2d = lambda t: t.reshape(-1, t.shape[-1])
    upd_small = _adam_small([(as2d(small_w[k][0]), as2d(g_small[k]), as2d(small_w[k][1]), as2d(small_w[k][2]))
                             for k in names_small])

    grads, deltas, new_m, new_v = dict(g_small), {}, {}, {}
    for k, upd in zip(names_small, upd_small):
        deltas[k], new_m[k], new_v[k] = [t.reshape(small_w[k][0].shape) for t in upd]
    grads["w_in"], deltas["w_in"], new_m["w_in"], new_v["w_in"] = upd_in
    for idx, k in enumerate(["w_a_out", "w_b_out", "w_out"]):
        grads[k], deltas[k], new_m[k], new_v[k] = upd_sq[4 * idx:4 * idx + 4]

    loss = red[ROW_LOSS, 0]
    order = ["meta_tokens", "norm_g", "w_in", "conv_a_w", "conv_a_b", "ln_a_g", "ln_a_b", "w_a_out", "b_a_out",
             "conv_b_w", "w_b_out", "w_out", "final_g"]
    return (loss, grad_x[None], *[grads[k] for k in order], *[deltas[k] for k in order],
            *[new_m[k] for k in order], *[new_v[k] for k in order])
```

```python
import jax
import jax.numpy as jnp
from jax import lax
from jax.experimental import pallas as pl
from jax.experimental.pallas import tpu as pltpu

F32 = jnp.float32
BF16 = jnp.bfloat16
MESH = pl.DeviceIdType.MESH

EPS = 1e-6
N_META = 16
N_SPLIT = 9
CONV_A = 31
CONV_B = 3
HALO_A = 32
HALO_B = 8
SHIFT_ROWS = 24
TM = 256
RB_FWD = 128
RB_BWD = 64
N_ROW_TILES_BIG = 8
ROW_BLOCK = 256
N_CHIPS = 4
VMEM_LIMIT = 56 * 1024 * 1024
VMEM_LIMIT_BIG = 62 * 1024 * 1024

ADAM_LR = 0.001
ADAM_B1 = 0.9
ADAM_B2 = 0.999
ADAM_EPS = 1e-08
ADAM_WD = 0.01
ADAM_STEP = 10

ROW_DWA = 0
ROW_DWB = 32
ROW_DCAB = 40
ROW_DLNG = 41
ROW_DLNB = 42
ROW_DBAO = 43
SM_ROWS = 48
ROW_DMETA = 48
ROW_DNG = 64
ROW_DFG = 65
ROW_LOSS = 66
AR_ROWS = 72


def _sigmoid(v):
    return 0.5 * jnp.tanh(0.5 * v) + 0.5


def _params(sem, **kw):
    return pltpu.CompilerParams(dimension_semantics=sem, vmem_limit_bytes=VMEM_LIMIT, **kw)


def _rows(rb, n):
    return pl.ds(pl.multiple_of(rb * n, n), n)


def _mesh_pos():
    x, y, c = lax.axis_index("x"), lax.axis_index("y"), lax.axis_index("c")
    return x, y, c


def _half(ref, j, c):
    h = ref.shape[2] // 2
    return ref.at[:, j, pl.ds(c * h, h), :]


class _Exchange:
    def __init__(self, sends, recvs):
        self.sends, self.recvs = sends, recvs

    def start(self):
        for cp in self.sends:
            cp.start()

    def finish(self):
        for cp in self.recvs:
            cp.wait_recv()
        for cp in self.sends:
            cp.wait_send()


def _chip_exchange(part_ref, land_ref, send_sems, recv_sems, diagonal=True):
    x, y, c = _mesh_pos()
    me = 2 * x + y
    sends, recvs = [], []
    for k, (px, py) in enumerate([(1 - x, y), (x, 1 - y), (1 - x, 1 - y)][:3 if diagonal else 2]):
        sems = dict(send_sem=send_sems.at[k], recv_sem=recv_sems.at[k], device_id=(px, py, c), device_id_type=MESH)
        sends.append(pltpu.make_async_remote_copy(
            src_ref=part_ref.at[:, 2 * px + py], dst_ref=land_ref.at[:, me], **sems))
        landed = land_ref.at[:, 2 * px + py]
        recvs.append(pltpu.make_async_remote_copy(src_ref=landed, dst_ref=landed, **sems))
    return _Exchange(sends, recvs)

def _sibling_swap(halves, small):
    n = len(halves)

    def body(*refs):
        ins, small_ref, outs, red_ref = refs[:n], refs[n], refs[n + 1:2 * n + 1], refs[2 * n + 1]
        send_sems, recv_sems = refs[2 * n + 2:2 * n + 4]
        reduce = _SmallAllReduce(small_ref, red_ref, *refs[2 * n + 4:])
        x, y, c = _mesh_pos()
        copies = [pltpu.make_async_remote_copy(
            src_ref=ins[a], dst_ref=outs[a], send_sem=send_sems.at[a], recv_sem=recv_sems.at[a],
            device_id=(x, y, 1 - c), device_id_type=MESH) for a in range(n)]
        reduce.start()
        for cp in copies:
            cp.start()
        reduce.between_chips()
        reduce.finish()
        for cp in copies:
            cp.wait()

    any_spec = pl.BlockSpec(memory_space=pl.ANY)
    vm = pl.BlockSpec(memory_space=pltpu.VMEM)
    outs = pl.pallas_call(
        body, name="rs_swap",
        in_specs=[any_spec] * n + [vm], out_specs=[any_spec] * n + [vm],
        out_shape=[jax.ShapeDtypeStruct(h.shape, h.dtype) for h in halves]
        + [jax.ShapeDtypeStruct(small.shape, F32)],
        scratch_shapes=[pltpu.SemaphoreType.DMA((n,)), pltpu.SemaphoreType.DMA((n,))]
        + _SmallAllReduce.scratch(*small.shape),
    )(*halves, small)
    return outs[:n], outs[n]


class _SmallAllReduce:
    def __init__(self, x_ref, out_ref, sib_ref, part_ref, peers_ref, send_sems, recv_sems):
        self.x_ref, self.out_ref, self.sib_ref, self.part_ref, self.peers_ref = x_ref, out_ref, sib_ref, part_ref, peers_ref
        x, y, c = _mesh_pos()
        self.me = 2 * x + y
        self.swap = pltpu.make_async_remote_copy(
            src_ref=x_ref, dst_ref=sib_ref, send_sem=send_sems.at[0], recv_sem=recv_sems.at[0],
            device_id=(x, y, 1 - c), device_id_type=MESH)
        self.sends, self.recvs = [], []
        for k, (px, py) in enumerate([(1 - x, y), (x, 1 - y), (1 - x, 1 - y)]):
            sems = dict(send_sem=send_sems.at[1 + k], recv_sem=recv_sems.at[1 + k],
                        device_id=(px, py, c), device_id_type=MESH)
            self.sends.append(pltpu.make_async_remote_copy(src_ref=part_ref, dst_ref=peers_ref.at[self.me], **sems))
            landed = peers_ref.at[2 * px + py]
            self.recvs.append(pltpu.make_async_remote_copy(src_ref=landed, dst_ref=landed, **sems))

    @staticmethod
    def scratch(rows, d):
        return [pltpu.VMEM((rows, d), F32), pltpu.VMEM((rows, d), F32), pltpu.VMEM((N_CHIPS, rows, d), F32),
                pltpu.SemaphoreType.DMA((4,)), pltpu.SemaphoreType.DMA((4,))]

    def start(self):
        self.swap.start()

    def between_chips(self):
        self.swap.wait()
        self.part_ref[...] = self.x_ref[...] + self.sib_ref[...]
        self.peers_ref[self.me] = self.part_ref[...]
        for cp in self.sends:
            cp.start()

    def finish(self):
        for cp in self.recvs:
            cp.wait_recv()
        for cp in self.sends:
            cp.wait_send()
        p = self.peers_ref
        self.out_ref[...] = ((p[0] + p[1]) + p[2]) + p[3]


def _sum_chips(parts, cw, pos, name, senders=(2, 1, 3)):
    s, _, h, _ = parts[0][0].shape
    hb = min(h, ROW_BLOCK)
    widths = [own.shape[3] // cw for own, _ in parts]
    starts = [sum(widths[:a]) for a in range(len(parts))]
    per = 1 + len(senders)

    def body(pos_ref, *refs):
        out_ref = refs[-1]
        n = pl.program_id(2)
        total = None
        for a in range(len(parts)):
            val = refs[per * a][...]
            for landed_ref in refs[per * a + 1:per * (a + 1)]:
                val = val + landed_ref[...].astype(F32)
            total = val if total is None else jnp.where(n >= starts[a], val, total)
        out_ref[...] = total

    def slot(a, flip):
        col = lambda n: jnp.clip(n - starts[a], 0, widths[a] - 1)
        return pl.BlockSpec((None, None, hb, cw),
                            lambda si, b, n, pos_ref: (si, lax.bitwise_xor(pos_ref[1], flip), b, col(n)))

    operands, specs = [], []
    for a, (own, landed) in enumerate(parts):
        operands += [own] + [landed] * len(senders)
        specs += [slot(a, 0)] + [slot(a, f) for f in senders]
    return pl.pallas_call(
        body, name=name,
        grid_spec=pltpu.PrefetchScalarGridSpec(
            num_scalar_prefetch=1, grid=(s, h // hb, sum(widths)), in_specs=specs,
            out_specs=pl.BlockSpec((None, hb, cw), lambda si, b, n, pos_ref: (si, b, n))),
        out_shape=jax.ShapeDtypeStruct((s, h, sum(widths) * cw), F32),
        compiler_params=_params(("arbitrary",) * 3),
    )(pos, *operands)


def _adamw(w, g, m, v):
    m = ADAM_B1 * m + (1.0 - ADAM_B1) * g
    v = ADAM_B2 * v + (1.0 - ADAM_B2) * (g * g)
    m_hat = m / (1.0 - ADAM_B1 ** ADAM_STEP)
    v_hat = v / (1.0 - ADAM_B2 ** ADAM_STEP)
    delta = -ADAM_LR * (m_hat / (jnp.sqrt(v_hat) + ADAM_EPS) + ADAM_WD * w)
    return delta, m, v


def _adam_halves(ws, ms, vs, g_own, g_recv, pos, name):
    n = len(ws)
    _, r, c = ws[0].shape
    h = r // 2
    rb = min(h, ROW_BLOCK)
    nb = h // rb

    def body(pos_ref, *refs):
        w_refs, m_refs, v_refs = refs[:n], refs[n:2 * n], refs[2 * n:3 * n]
        go_ref, gr_ref = refs[3 * n:3 * n + 2]
        outs = refs[3 * n + 2:]
        mine = pl.program_id(0) == pos_ref[0]
        for a in range(n):
            g = jnp.where(mine, go_ref[a], gr_ref[a])
            delta, m, v = _adamw(w_refs[a][...], g, m_refs[a][...], v_refs[a][...])
            outs[4 * a][...], outs[4 * a + 1][...], outs[4 * a + 2][...], outs[4 * a + 3][...] = g, delta, m, v

    spec_w = pl.BlockSpec((None, rb, c), lambda hf, b, pos_ref: (0, hf * nb + b, 0))
    spec_g = pl.BlockSpec((n, rb, c), lambda hf, b, pos_ref: (0, b, 0))
    return pl.pallas_call(
        body, name=name,
        grid_spec=pltpu.PrefetchScalarGridSpec(
            num_scalar_prefetch=1, grid=(2, nb), in_specs=[spec_w] * (3 * n) + [spec_g] * 2,
            out_specs=[spec_w] * (4 * n)),
        out_shape=[jax.ShapeDtypeStruct((1, r, c), F32)] * (4 * n),
        compiler_params=_params(("arbitrary",) * 2),
    )(pos, *ws, *ms, *vs, g_own, g_recv)


def _adam_small(items):
    n = len(items)

    def body(*refs):
        ins, outs = refs[:4 * n], refs[4 * n:]
        for a in range(n):
            w_ref, g_ref, m_ref, v_ref = ins[4 * a:4 * a + 4]
            d, m, v = _adamw(w_ref[...], g_ref[...], m_ref[...], v_ref[...])
            outs[3 * a][...] = d
            outs[3 * a + 1][...] = m
            outs[3 * a + 2][...] = v

    vm = pl.BlockSpec(memory_space=pltpu.VMEM)
    flat = [t for it in items for t in it]
    outs = pl.pallas_call(
        body, name="adam_small", in_specs=[vm] * (4 * n), out_specs=[vm] * (3 * n),
        out_shape=[jax.ShapeDtypeStruct(it[0].shape, F32) for it in items for _ in range(3)],
    )(*flat)
    return [tuple(outs[3 * a:3 * a + 3]) for a in range(n)]


def _big_row_spec(seq, tmb, d, tile_of):
    return pl.BlockSpec((pl.Element(tmb), pl.Element(d)),
                        lambda *args: (pl.multiple_of(jnp.minimum(tile_of(*args) * tmb, seq - tmb), 8), 0))


def _big_row_tile(x_ref, front_ref, i):
    rows = x_ref[...]
    last = jnp.concatenate([rows[TM:], front_ref[...]], axis=0)
    return jnp.where(i == N_ROW_TILES_BIG - 1, last, rows)


def _h_prep(x, meta, norm_g, shards, pos):
    seq, d = x.shape
    tp = seq + TM
    dc = meta.shape[1]
    tmb = tp // N_ROW_TILES_BIG
    assert tmb >= TM and tp == tmb * N_ROW_TILES_BIG
    last = N_ROW_TILES_BIG - 1
    ns = len(shards)

    def body(pos_ref, x_ref, meta_ref, g_ref, *refs):
        shard_refs, (h_ref, front_ref), placed_refs = refs[:ns], refs[ns:ns + 2], refs[ns + 2:2 * ns + 2]
        metas, msend, mrecv = refs[2 * ns + 2:]
        for a in range(ns):
            placed_refs[a][...] = shard_refs[a][...].astype(placed_refs[a].dtype)
        x, y, c = _mesh_pos()
        me = 2 * x + y
        chips = [(1 - x, y), (x, 1 - y), (1 - x, 1 - y)]
        i = pl.program_id(0)

        def meta_copy(k, chip):
            return pltpu.make_async_remote_copy(
                src_ref=metas.at[chip], dst_ref=metas.at[chip], send_sem=msend.at[k], recv_sem=mrecv.at[k],
                device_id=(*chips[k], c), device_id_type=MESH)

        @pl.when(i == 0)
        def _():
            metas[me] = meta_ref[...]
            for k in range(3):
                meta_copy(k, me).start()
            front_ref[...] = jnp.zeros_like(front_ref)

        @pl.when(i == last)
        def _():
            for k, (px, py) in enumerate(chips):
                meta_copy(k, 2 * px + py).wait_recv()
            for q in range(N_CHIPS):
                front_ref[TM - N_META:TM, q * dc:(q + 1) * dc] = metas[q]

        s = _big_row_tile(x_ref, front_ref, i)
        r = lax.rsqrt(jnp.mean(s * s, axis=-1, keepdims=True) + EPS)
        h_ref[...] = (s * r * g_ref[...]).astype(BF16)

        @pl.when(i == last)
        def _():
            for k in range(3):
                meta_copy(k, me).wait_send()

    shard_in, shard_out, shard_shapes = [], [], []
    for arr, dtype in shards:
        s, r, c = arr.shape
        sliced = r % (N_ROW_TILES_BIG * 16) == 0
        rp = r // N_ROW_TILES_BIG if sliced else r
        step = (lambda i: i) if sliced else (lambda i: 0)
        shard_in.append(pl.BlockSpec((s, rp, c), lambda i, pos_ref, step=step: (0, step(i), 0)))
        shard_out.append(pl.BlockSpec((s, None, rp, c), lambda i, pos_ref, step=step: (0, pos_ref[1], step(i), 0)))
        shard_shapes.append(jax.ShapeDtypeStruct((s, N_CHIPS, r, c), dtype))
    outs = pl.pallas_call(
        body, name="f0_norm",
        grid_spec=pltpu.PrefetchScalarGridSpec(
            num_scalar_prefetch=1, grid=(N_ROW_TILES_BIG,),
            in_specs=[_big_row_spec(seq, tmb, d, lambda i, pos_ref: i),
                      pl.BlockSpec(meta.shape, lambda i, pos_ref: (0, 0)),
                      pl.BlockSpec((1, d), lambda i, pos_ref: (0, 0))] + shard_in,
            out_specs=[pl.BlockSpec((tmb, d), lambda i, pos_ref: (i, 0)),
                       pl.BlockSpec((TM, d), lambda i, pos_ref: (0, 0))] + shard_out,
            scratch_shapes=[pltpu.VMEM((N_CHIPS,) + meta.shape, F32),
                            pltpu.SemaphoreType.DMA((3,)), pltpu.SemaphoreType.DMA((3,))]),
        out_shape=[jax.ShapeDtypeStruct((tp, d), BF16), jax.ShapeDtypeStruct((TM, d), F32)] + shard_shapes,
        compiler_params=_params(("arbitrary",)),
    )(pos, x, meta, norm_g, *[arr for arr, _ in shards])
    return outs[0], outs[1], outs[2:]


N_UNITS = 3
N_STEPS_PROJ = N_CHIPS * N_UNITS


def _proj_plan(u):
    v = u - N_UNITS
    if v < 2 * N_UNITS:
        return v % 2, v // 2
    return 2, v - 2 * N_UNITS


def _proj_unit(t, me):
    v = t - N_UNITS
    near = v < 2 * N_UNITS
    rel = jnp.where(near, v % 2, 2)
    unit = jnp.where(near, v // 2, v - 2 * N_UNITS)
    flip = jnp.where(rel == 0, 2, jnp.where(rel == 1, 1, 3))
    own = t < N_UNITS
    return jnp.where(own, me, lax.bitwise_xor(me, flip)), jnp.where(own, t, unit)


def _proj_fwd(h, bufs, pos):
    tp, d = h.shape
    _, nsh, _, sw = bufs[0].shape
    cu = sw // N_UNITS
    assert cu % 128 == 0
    n = len(bufs)
    w_sems = 6 * N_UNITS
    last = N_STEPS_PROJ - 1
    late = N_STEPS_PROJ - N_UNITS

    def body(pos_ref, h_ref, *refs):
        proj_ref = refs[n]
        gbufs = refs[n + 1:2 * n + 1]
        wbuf, wsems, send_sems, recv_sems = refs[2 * n + 1:]
        x, y, c = _mesh_pos()
        me = 2 * x + y
        sibling = (x, y, 1 - c)
        chips = [(1 - x, y), (x, 1 - y), (1 - x, 1 - y)]
        chip_ids = [2 * px + py for px, py in chips]
        relayed_chip = jnp.where(c == 0, chip_ids[0], chip_ids[1])
        relay_to = (jnp.where(c == 0, x, 1 - x), jnp.where(c == 0, 1 - y, y), c)
        t = pl.program_id(0)

        def remote(idx, piece, to):
            return pltpu.make_async_remote_copy(
                src_ref=piece, dst_ref=piece, send_sem=send_sems.at[idx], recv_sem=recv_sems.at[idx],
                device_id=to, device_id_type=MESH)

        hr = d // 2

        def chunk_of(chip, half, k):
            return gbufs[0].at[0, chip, pl.ds(half * hr, hr), pl.ds(k * cu, cu)]

        def own_chunk(r, k):
            return remote(6 * k + r, chunk_of(me, c, k), (*chips[r], c))

        def landed_chunk(r, k):
            return remote(6 * k + r, chunk_of(chip_ids[r], c, k), (*chips[r], c))

        def relay_chunk(k):
            return remote(6 * k + 2, chunk_of(relayed_chip, c, k), relay_to)

        def sibling_chunk(r, k, half):
            return remote(6 * k + 3 + r, chunk_of(chip_ids[r], half, k), sibling)

        def fetch(u):
            chip, unit = _proj_unit(jnp.int32(u), me)
            return pltpu.make_async_copy(gbufs[0].at[0, chip, :, pl.ds(pl.multiple_of(unit * cu, 128), cu)],
                                         wbuf.at[u % 2], wsems.at[u % 2])

        def make_available(u):
            r, k = _proj_plan(u)
            landed_chunk(r, k).wait_recv()
            if r < 2:
                pl.when(c == r)(lambda: relay_chunk(k).start())
            sibling_chunk(r, k, c).start()
            sibling_chunk(r, k, 1 - c).wait_recv()

        def own_piece(a, r):
            return remote(w_sems + 6 * (a - 1) + r, _half(gbufs[a], me, c), (*chips[r], c))

        def relay(a):
            return remote(w_sems + 6 * (a - 1) + 2, _half(gbufs[a], relayed_chip, c), relay_to)

        def to_sibling(a, r, core):
            return remote(w_sems + 6 * (a - 1) + 3 + r, _half(gbufs[a], chip_ids[r], core), sibling)

        def landed(a, r):
            return remote(w_sems + 6 * (a - 1) + r, _half(gbufs[a], chip_ids[r], c), (*chips[r], c))

        for u in range(N_STEPS_PROJ):
            @pl.when(t == u)
            def _(u=u):
                if u == 0:
                    for k in range(N_UNITS):
                        for r in range(2):
                            own_chunk(r, k).start()
                    for a in range(1, n):
                        for r in range(2):
                            own_piece(a, r).start()
                    fetch(0).start()
                if u < last:
                    if u + 1 >= N_UNITS:
                        make_available(u + 1)
                    fetch(u + 1).start()
                if u == late:
                    for a in range(1, n):
                        landed(a, 0).wait_recv()
                        landed(a, 1).wait_recv()
                        relay(a).start()
                        for r in range(2):
                            to_sibling(a, r, c).start()
                        for r in range(2):
                            to_sibling(a, r, 1 - c).wait_recv()
                fetch(u).wait()

        proj_ref[...] = jnp.dot(h_ref[...], wbuf[t % 2], preferred_element_type=F32).astype(BF16)

        @pl.when(t == last)
        def _():
            for a in range(1, n):
                landed(a, 2).wait_recv()
                to_sibling(a, 2, c).start()
                to_sibling(a, 2, 1 - c).wait_recv()
            for k in range(N_UNITS):
                for r in range(2):
                    own_chunk(r, k).wait_send()
                relay_chunk(k).wait_send()
                for r in range(3):
                    sibling_chunk(r, k, c).wait_send()
            for a in range(1, n):
                for r in range(2):
                    own_piece(a, r).wait_send()
                relay(a).wait_send()
                for r in range(3):
                    to_sibling(a, r, c).wait_send()

    def out_index(t, pos_ref):
        chip, unit = _proj_unit(t, pos_ref[1])
        return 0, chip * N_UNITS + unit

    any_spec = pl.BlockSpec(memory_space=pl.ANY)
    outs = pl.pallas_call(
        body, name="f1_proj",
        grid_spec=pltpu.PrefetchScalarGridSpec(
            num_scalar_prefetch=1, grid=(N_STEPS_PROJ,),
            in_specs=[pl.BlockSpec((tp, d), lambda t, pos_ref: (0, 0))] + [any_spec] * n,
            out_specs=[pl.BlockSpec((tp, cu), out_index)] + [any_spec] * n,
            scratch_shapes=[pltpu.VMEM((2, d, cu), BF16), pltpu.SemaphoreType.DMA((2,)),
                            pltpu.SemaphoreType.DMA((w_sems + 6 * (n - 1),)),
                            pltpu.SemaphoreType.DMA((w_sems + 6 * (n - 1),))]),
        out_shape=[jax.ShapeDtypeStruct((tp, nsh * sw), BF16)]
        + [jax.ShapeDtypeStruct(b.shape, b.dtype) for b in bufs],
        input_output_aliases={2 + a: 1 + a for a in range(n)},
        compiler_params=_params(("arbitrary",)),
    )(pos, h, *bufs)
    return outs[0], outs[1:]


def _dh_bwd(dproj, wg_in, x, front, ds2, norm_g, part):
    seq, d = x.shape
    tp = seq + TM
    _, nsh, _, sw = wg_in.shape
    tmb = tp // N_ROW_TILES_BIG
    tail = tmb - TM
    last = N_ROW_TILES_BIG - 1

    def body(dp_ref, w_hbm, x_ref, front_ref, ds2_ref, g_ref, part_ref, gx_hbm, dfront_ref, dng_ref, land_ref,
             wbuf, gacc, dsbuf, wsem, osems, send_sems, recv_sems):
        exchange = _chip_exchange(part_ref, land_ref, send_sems, recv_sems, diagonal=False)
        i = pl.program_id(0)

        def x_rows_out(step):
            return pltpu.make_async_copy(dsbuf.at[step % 2], gx_hbm.at[pl.ds(step * tmb, tmb), :], osems.at[step % 2])

        last_out = pltpu.make_async_copy(dsbuf.at[last % 2, pl.ds(0, tail), :],
                                         gx_hbm.at[pl.ds(last * tmb, tail), :], osems.at[last % 2])

        @pl.when(i == 0)
        def _():
            exchange.start()
            gacc[...] = jnp.zeros_like(gacc)
            whole = pltpu.make_async_copy(w_hbm.at[0], wbuf, wsem)
            whole.start()
            whole.wait()

        dh = None
        for j in range(nsh):
            part = lax.dot_general(dp_ref[:, j * sw:(j + 1) * sw], wbuf[j], (((1,), (1,)), ((), ())),
                                   preferred_element_type=F32)
            dh = part if dh is None else dh + part
        s = _big_row_tile(x_ref, front_ref, i)
        r = lax.rsqrt(jnp.mean(s * s, axis=-1, keepdims=True) + EPS)
        gacc[...] += (dh * s * r).reshape(tmb // 8, 8, d).sum(axis=0)
        t = dh * g_ref[...]

        @pl.when(i >= 2)
        def _():
            x_rows_out(i - 2).wait()

        dsbuf[i % 2] = ds2_ref[...] + r * t - s * (r * r * r) * jnp.mean(t * s, axis=-1, keepdims=True)

        @pl.when(i < last)
        def _():
            x_rows_out(i).start()

        @pl.when(i == last)
        def _():
            last_out.start()
            dfront_ref[...] = dsbuf[last % 2, tail:, :]
            dng_ref[...] = jnp.broadcast_to(jnp.sum(gacc[...], axis=0, keepdims=True), (8, d))
            exchange.finish()
            x_rows_out(last - 1).wait()
            last_out.wait()

    any_spec = pl.BlockSpec(memory_space=pl.ANY)
    return pl.pallas_call(
        body, name="b2_dh", grid=(N_ROW_TILES_BIG,),
        in_specs=[pl.BlockSpec((tmb, nsh * sw), lambda i: (i, 0)), any_spec,
                  _big_row_spec(seq, tmb, d, lambda i: i),
                  pl.BlockSpec((TM, d), lambda i: (0, 0)),
                  pl.BlockSpec((tmb, d), lambda i: (i, 0)),
                  pl.BlockSpec((1, d), lambda i: (0, 0)), any_spec],
        out_specs=[any_spec, pl.BlockSpec((TM, d), lambda i: (0, 0)),
                   pl.BlockSpec((8, d), lambda i: (0, 0)), any_spec],
        out_shape=[jax.ShapeDtypeStruct((seq, d), F32), jax.ShapeDtypeStruct((TM, d), F32),
                   jax.ShapeDtypeStruct((8, d), F32), jax.ShapeDtypeStruct(part.shape, part.dtype)],
        scratch_shapes=[pltpu.VMEM((nsh, d, sw), BF16), pltpu.VMEM((8, d), F32), pltpu.VMEM((2, tmb, d), F32),
                        pltpu.SemaphoreType.DMA, pltpu.SemaphoreType.DMA((2,)),
                        pltpu.SemaphoreType.DMA((3,)), pltpu.SemaphoreType.DMA((3,))],
        compiler_params=pltpu.CompilerParams(dimension_semantics=("arbitrary",),
                                             vmem_limit_bytes=VMEM_LIMIT_BIG),
    )(dproj, wg_in, x, front, ds2, norm_g, part)


def _col_block(width, cap):
    return max(b for b in range(128, cap + 1, 128) if width % b == 0)


def _relay_order(k, me):
    flip = jnp.where(k == 0, 3, jnp.where(k == 1, 0, jnp.where(k == 2, 2, 1)))
    return lax.bitwise_xor(me, flip)


def _dw_reduced(lhs_t, rhs, cw, nblk, operands, groups, out_dims, out_block, out_index, carried, pos, name,
                relay=0):
    na, d, tp = lhs_t.shape
    rg = d // groups
    hh = rg // 2
    nc = len(carried)

    def body(pos_ref, *refs):
        l_ref, r_ref = refs[:2]
        part_refs = refs[2:2 + nc]
        p32_ref, pbf_ref = refs[2 + nc:4 + nc]
        land_refs = refs[4 + nc:4 + 2 * nc]
        base = 4 + 2 * nc + (1 if relay else 0)
        res, rbuf, send_sems, recv_sems = refs[base:base + 4]
        xsems = refs[base + 4:base + 4 + 2 * nc]
        exchanges = [_chip_exchange(part_refs[e], land_refs[e], xsems[2 * e], xsems[2 * e + 1]) for e in range(nc)]
        exchange = _Exchange([s for ex in exchanges for s in ex.sends], [r for ex in exchanges for r in ex.recvs])
        x, y, c = _mesh_pos()
        t = pl.program_id(0)
        u = jnp.maximum(t - 1, 0)
        if relay:
            relay_hbm = refs[4 + 2 * nc]
            stage, rin, rsend, rrecv, rlocal = refs[base + 4 + 2 * nc:]
            relay_to = (jnp.where(c == 0, 1 - x, x), jnp.where(c == 0, y, 1 - y), c)

            def relay_copy(n):
                return pltpu.make_async_remote_copy(
                    src_ref=stage.at[n], dst_ref=relay_hbm.at[n], send_sem=rsend.at[n], recv_sem=rrecv.at[n],
                    device_id=relay_to, device_id_type=MESH)

        def to_sibling(blk):
            return pltpu.make_async_remote_copy(
                src_ref=res.at[blk % 2, :, pl.ds((1 - c) * hh, hh), :], dst_ref=rbuf.at[blk % 2],
                send_sem=send_sems.at[blk], recv_sem=recv_sems.at[blk],
                device_id=(x, y, 1 - c), device_id_type=MESH)

        @pl.when(t == 0)
        def _():
            exchange.start()

        @pl.when(t < nblk)
        def _():
            res[t % 2] = jnp.dot(l_ref[...], r_ref[...], preferred_element_type=F32).reshape(groups, rg, cw)

        def chip_partial():
            return res[u % 2, :, pl.ds(c * hh, hh), :] + rbuf[u % 2]

        def write(p):
            p32_ref[...] = p.reshape(p32_ref.shape)
            pbf_ref[...] = p.reshape(pbf_ref.shape).astype(BF16)

        plain = t >= 1
        if relay:
            group_in = jnp.where(c == 0, 3, 2)
            outgoing = (t >= 1) & (u < relay)
            incoming = (t >= 1) & (u // relay == group_in)
            plain = (t >= 1) & jnp.logical_not(outgoing | incoming)

            @pl.when(outgoing)
            def _():
                to_sibling(u).wait_recv()
                p = chip_partial()
                write(p)
                stage[u] = p.astype(BF16)
                relay_copy(u).start()

            @pl.when(incoming)
            def _():
                n = u - group_in * relay
                relay_copy(n).wait_recv()
                landed = pltpu.make_async_copy(relay_hbm.at[n], rin, rlocal)
                landed.start()
                to_sibling(u).wait_recv()
                landed.wait()
                write(chip_partial() + rin[...].astype(F32))

        @pl.when(plain)
        def _():
            to_sibling(u).wait_recv()
            write(chip_partial())

        @pl.when(t < nblk)
        def _():
            to_sibling(t).start()

        @pl.when(t >= 1)
        def _():
            to_sibling(u).wait_send()

        @pl.when(t == nblk)
        def _():
            exchange.finish()
            for n in range(relay):
                relay_copy(n).wait_send()

    any_spec = pl.BlockSpec(memory_space=pl.ANY)
    last = nblk - 1

    def operand_index(which):
        return lambda t, pos_ref: operands(jnp.minimum(t, last), pos_ref)[which]

    out_spec = pl.BlockSpec(out_block, lambda t, pos_ref: out_index(jnp.maximum(t - 1, 0), pos_ref))
    relay_out, relay_shape, relay_scratch = [], [], []
    if relay:
        relay_out = [any_spec]
        relay_shape = [jax.ShapeDtypeStruct((relay, groups, hh, cw), BF16)]
        relay_scratch = [pltpu.VMEM((relay, groups, hh, cw), BF16), pltpu.VMEM((groups, hh, cw), BF16),
                         pltpu.SemaphoreType.DMA((relay,)), pltpu.SemaphoreType.DMA((relay,)),
                         pltpu.SemaphoreType.DMA]
    outs = pl.pallas_call(
        body, name=name,
        grid_spec=pltpu.PrefetchScalarGridSpec(
            num_scalar_prefetch=1, grid=(nblk + 1,),
            in_specs=[pl.BlockSpec((None, d, tp), lambda t, pos_ref: (operand_index(0)(t, pos_ref), 0, 0)),
                      pl.BlockSpec((None, tp, cw), lambda t, pos_ref: (operand_index(0)(t, pos_ref), 0,
                                                                       operand_index(1)(t, pos_ref)))]
            + [any_spec] * nc,
            out_specs=[out_spec, out_spec] + [any_spec] * nc + relay_out,
            scratch_shapes=[pltpu.VMEM((2, groups, rg, cw), F32), pltpu.VMEM((2, groups, hh, cw), F32),
                            pltpu.SemaphoreType.DMA((nblk,)), pltpu.SemaphoreType.DMA((nblk,))]
            + [pltpu.SemaphoreType.DMA((3,)), pltpu.SemaphoreType.DMA((3,))] * nc + relay_scratch),
        out_shape=[jax.ShapeDtypeStruct(out_dims, F32), jax.ShapeDtypeStruct(out_dims, BF16)]
        + [jax.ShapeDtypeStruct(e.shape, e.dtype) for e in carried] + relay_shape,
        compiler_params=_params(("arbitrary",)),
    )(pos, lhs_t, rhs, *carried)
    return outs[0], outs[1], outs[2:2 + nc]


def _conv_a_taps(first_lag, last_lag):
    out = []
    for r in range(8):
        taps = [(q, 8 * q + r) for q in range(5) if first_lag <= 8 * q + r <= last_lag]
        if taps:
            out.append((r, taps))
    return out


def _tile_block(i, nt):
    return jnp.where(i == 0, nt - 1, i - 1)


def _mix_fwd(x, front, proj, target, w3, wa, wb, conv_a_b, ln_g, ln_b, b_a_out, final_g, norm_g):
    seq, d = x.shape
    tp = seq + TM
    nt = tp // TM
    RB = RB_FWD
    nrb = TM // RB
    shl = TM + SHIFT_ROWS

    def body(x_ref, front_ref, proj_ref, tgt_ref, w3_ref, wa_ref, wb_ref, cab_ref, lng_ref, lnb_ref, bao_ref, fg_ref,
             ng_ref, ca_ref, cb_ref, ya_ref, yb_ref, abmt_ref, ds2_ref, ht_ref, loss_ref, dfg_ref,
             abm_ref, ext_a, ext_b, sh, s2_s, lacc, gacc):
        i = pl.program_id(0)

        def split(k, rows):
            return proj_ref[rows, k * d:(k + 1) * d].astype(F32)

        def s_tile():
            return jnp.where(i == 0, front_ref[...], x_ref[...])

        s_in = s_tile()
        h = s_in * lax.rsqrt(jnp.mean(s_in * s_in, axis=-1, keepdims=True) + EPS) * ng_ref[...]
        ht_ref[...] = h.astype(BF16).T

        @pl.when(i == 0)
        def _():
            ext_a[0:HALO_A, :] = jnp.zeros((HALO_A, d), F32)
            ext_b[0:HALO_B, :] = jnp.zeros((HALO_B, d), F32)
            lacc[...] = jnp.zeros_like(lacc)
            gacc[...] = jnp.zeros_like(gacc)

        def conv_in(rb, carry):
            rows = _rows(rb, RB)
            ua0 = split(0, rows) * _sigmoid(split(1, rows))
            ext_a[pl.ds(pl.multiple_of(HALO_A + rb * RB, 8), RB), :] = ua0
            ext_b[pl.ds(pl.multiple_of(HALO_B + rb * RB, 8), RB), :] = split(4, rows) * split(5, rows)
            ca_ref[rows, :] = jnp.broadcast_to(cab_ref[...], (RB, d))
            return carry
        lax.fori_loop(0, nrb, conv_in, 0)

        @pl.when(i == 0)
        def _():
            abmt_ref[...] = jnp.zeros_like(abmt_ref)
            ds2_ref[...] = jnp.zeros_like(ds2_ref)

        @pl.when(i > 0)
        def _():
            tile_after_conv_inputs(split, s_tile, tgt_ref, w3_ref, wa_ref, wb_ref, lng_ref, lnb_ref, bao_ref, fg_ref,
                                   ca_ref, cb_ref, ya_ref, yb_ref, abmt_ref, ds2_ref, abm_ref, ext_a, ext_b, sh,
                                   s2_s, lacc, gacc)

        ext_a[0:HALO_A, :] = ext_a[TM:TM + HALO_A, :]
        ext_b[0:HALO_B, :] = ext_b[TM:TM + HALO_B, :]

        @pl.when(i == nt - 1)
        def _():
            loss_ref[...] = jnp.broadcast_to(0.5 * jnp.sum(lacc[...]) * (1.0 / d), (8, 128))
            dfg_ref[...] = jnp.broadcast_to(jnp.sum(gacc[...], axis=0, keepdims=True), (8, d))

    def tile_after_conv_inputs(split, s_tile, tgt_ref, w3_ref, wa_ref, wb_ref, lng_ref, lnb_ref, bao_ref, fg_ref,
                               ca_ref, cb_ref, ya_ref, yb_ref, abmt_ref, ds2_ref, abm_ref, ext_a, ext_b, sh, s2_s,
                               lacc, gacc):
        for r, taps in _conv_a_taps(HALO_A - CONV_A + 1, HALO_A):
            if r == 0:
                src = ext_a
            else:
                sh[...] = ext_a[r:r + shl, :]
                src = sh

            def conv_acc(rb, carry, src=src, taps=taps):
                rows = _rows(rb, RB)
                acc = ca_ref[rows, :]
                for q, lag in taps:
                    k = lag - (HALO_A - CONV_A + 1)
                    acc = acc + src[pl.ds(pl.multiple_of(rb * RB + 8 * q, 8), RB), :] * wa_ref[k:k + 1, :]
                ca_ref[rows, :] = acc
                return carry
            lax.fori_loop(0, nrb, conv_acc, 0)

        cb_ref[...] = ext_b[HALO_B:HALO_B + TM, :] * wb_ref[2:3, :]
        for k in range(CONV_B - 1):
            off = HALO_B - CONV_B + 1 + k
            sh[0:TM, :] = ext_b[off:off + TM, :]
            cb_ref[...] += sh[0:TM, :] * wb_ref[k:k + 1, :]

        def branches(rb, carry):
            rows = _rows(rb, RB)
            ca = ca_ref[rows, :]
            mu = jnp.mean(ca, axis=-1, keepdims=True)
            xc = ca - mu
            rstd = lax.rsqrt(jnp.mean(xc * xc, axis=-1, keepdims=True) + EPS)
            ln = xc * rstd * lng_ref[...] + lnb_ref[...]
            ua = ln * _sigmoid(ln)
            a_z = split(2, rows)
            abm_ref[0, rows, :] = (ua * (a_z * _sigmoid(a_z))).astype(BF16)
            return carry
        lax.fori_loop(0, nrb, branches, 0)

        def branch_b(rb, carry):
            rows = _rows(rb, RB)
            b_z = split(6, rows)
            ub = split(3, rows) * cb_ref[rows, :]
            abm_ref[1, rows, :] = (ub * (b_z * _sigmoid(b_z))).astype(BF16)
            return carry
        lax.fori_loop(0, nrb, branch_b, 0)

        ya_ref[...] = jnp.dot(abm_ref[0], w3_ref[0], preferred_element_type=F32) + bao_ref[...]
        yb_ref[...] = jnp.dot(abm_ref[1], w3_ref[1], preferred_element_type=F32)

        def merge(rb, carry):
            rows = _rows(rb, RB)
            m = _sigmoid(split(7, rows)) * ya_ref[rows, :] + _sigmoid(split(8, rows)) * yb_ref[rows, :]
            abm_ref[2, rows, :] = m.astype(BF16)
            return carry
        lax.fori_loop(0, nrb, merge, 0)

        s2_s[...] = s_tile() + jnp.dot(abm_ref[2], w3_ref[2], preferred_element_type=F32)
        for k in range(3):
            abmt_ref[k] = abm_ref[k].T

        def head(rb, carry):
            rows = _rows(rb, RB)
            s2 = s2_s[rows, :]
            r2 = lax.rsqrt(jnp.mean(s2 * s2, axis=-1, keepdims=True) + EPS)
            diff = s2 * r2 * fg_ref[...] - tgt_ref[rows, :]
            lacc[...] += diff * diff
            dy = diff * (1.0 / d)
            gacc[...] += (dy * s2 * r2).reshape(RB // 8, 8, d).sum(axis=0)
            t = dy * fg_ref[...]
            ds2_ref[rows, :] = r2 * t - s2 * (r2 * r2 * r2) * jnp.mean(t * s2, axis=-1, keepdims=True)
            return carry
        lax.fori_loop(0, nrb, head, 0)

    row_f32 = pl.BlockSpec((TM, d), lambda i: (_tile_block(i, nt), 0))
    x_rows = pl.BlockSpec((TM, d), lambda i: (jnp.maximum(i - 1, 0), 0))
    const = lambda shape: pl.BlockSpec(shape, lambda i: (0,) * len(shape))
    return pl.pallas_call(
        body, name="f2_mix", grid=(nt,),
        in_specs=[x_rows, const((TM, d)),
                  pl.BlockSpec((TM, N_SPLIT * d), lambda i: (_tile_block(i, nt), 0)),
                  x_rows,
                  const((3, d, d)), const(wa.shape), const(wb.shape)] + [const((1, d))] * 6,
        out_specs=[row_f32, row_f32, row_f32, row_f32,
                   pl.BlockSpec((3, d, TM), lambda i: (0, 0, _tile_block(i, nt))),
                   row_f32, pl.BlockSpec((d, TM), lambda i: (0, _tile_block(i, nt))),
                   const((8, 128)), const((8, d))],
        out_shape=[jax.ShapeDtypeStruct((tp, d), F32)] * 4
        + [jax.ShapeDtypeStruct((3, d, tp), BF16), jax.ShapeDtypeStruct((tp, d), F32),
           jax.ShapeDtypeStruct((d, tp), BF16),
           jax.ShapeDtypeStruct((8, 128), F32), jax.ShapeDtypeStruct((8, d), F32)],
        scratch_shapes=[pltpu.VMEM((3, TM, d), BF16),
                        pltpu.VMEM((HALO_A + TM, d), F32), pltpu.VMEM((HALO_B + TM, d), F32),
                        pltpu.VMEM((shl, d), F32), pltpu.VMEM((TM, d), F32),
                        pltpu.VMEM((RB, d), F32), pltpu.VMEM((8, d), F32)],
        compiler_params=_params(("arbitrary",)),
    )(x, front, proj, target, w3, wa, wb, conv_a_b, ln_g, ln_b, b_a_out, final_g, norm_g)


def _mix_bwd(ds2, proj, ca, cb, ya, yb, w3, wa, wb, ln_g, ln_b):
    tp, d = ds2.shape
    nt = tp // TM
    RB = RB_BWD
    nrb = TM // RB
    shl = TM + SHIFT_ROWS
    nt_dims = (((1,), (1,)), ((), ()))

    def body(ds2_ref, proj_ref, ca_ref, cb_ref, ya_ref, yb_ref, w3_ref, wa_ref, wb_ref, lng_ref, lnb_ref,
             dproj_ref, d3_ref, sm_ref, ext_d, ext_e, sh, dm_s, dpa_s, dpb_s, dua0_s, acc):
        step = pl.program_id(0)

        def split(k, rows):
            return proj_ref[rows, k * d:(k + 1) * d].astype(F32)

        def put(k, rows, val):
            dproj_ref[rows, k * d:(k + 1) * d] = val.astype(BF16)

        def accum(row, val):
            acc[row] += val.reshape(RB // 8, 8, d).sum(axis=0)

        @pl.when(step == 0)
        def _():
            ext_d[TM:TM + HALO_A, :] = jnp.zeros((HALO_A, d), F32)
            ext_e[TM:TM + HALO_B, :] = jnp.zeros((HALO_B, d), F32)
            acc[...] = jnp.zeros_like(acc)

        front = step == nt - 1

        @pl.when(front)
        def _():
            d3_ref[...] = jnp.zeros_like(d3_ref)

            def conv_only(rb, carry):
                rows = _rows(rb, RB)
                zeros = jnp.zeros((RB, d), F32)
                for k in (2, 3, 6, 7, 8):
                    put(k, rows, zeros)
                ext_d[rows, :] = zeros
                ext_e[rows, :] = zeros
                dua0_s[rows, :] = zeros
                dm_s[rows, :] = split(0, rows) * _sigmoid(split(1, rows))
                return carry
            lax.fori_loop(0, nrb, conv_only, 0)

        @pl.when(jnp.logical_not(front))
        def _():
            tile_to_conv_outputs(split, put, accum, ds2_ref, ca_ref, cb_ref, ya_ref, yb_ref, w3_ref, lng_ref, lnb_ref,
                                 d3_ref, ext_d, ext_e, dm_s, dpa_s, dpb_s, dua0_s)

        tile_conv_transposes(split, put, accum, wa_ref, wb_ref, ext_d, ext_e, sh, dm_s, dpb_s, dua0_s)

        @pl.when(front)
        def _():
            for row in range(SM_ROWS):
                sm_ref[row:row + 1, :] = jnp.sum(acc[row], axis=0, keepdims=True)

    def tile_to_conv_outputs(split, put, accum, ds2_ref, ca_ref, cb_ref, ya_ref, yb_ref, w3_ref, lng_ref, lnb_ref,
                             d3_ref, ext_d, ext_e, dm_s, dpa_s, dpb_s, dua0_s):
        d3_ref[2] = ds2_ref[...].astype(BF16)
        dm_s[...] = lax.dot_general(d3_ref[2], w3_ref[2], nt_dims, preferred_element_type=F32)

        def gates(rb, carry):
            rows = _rows(rb, RB)
            dm = dm_s[rows, :]
            sa = _sigmoid(split(7, rows))
            sb = _sigmoid(split(8, rows))
            ya_v = ya_ref[rows, :]
            yb_v = yb_ref[rows, :]
            put(7, rows, dm * ya_v * sa * (1.0 - sa))
            put(8, rows, dm * yb_v * sb * (1.0 - sb))
            dya = dm * sa
            accum(ROW_DBAO, dya)
            d3_ref[0, rows, :] = dya.astype(BF16)
            d3_ref[1, rows, :] = (dm * sb).astype(BF16)
            return carry
        lax.fori_loop(0, nrb, gates, 0)

        dpa_s[...] = lax.dot_general(d3_ref[0], w3_ref[0], nt_dims, preferred_element_type=F32)
        dpb_s[...] = lax.dot_general(d3_ref[1], w3_ref[1], nt_dims, preferred_element_type=F32)

        def branch_a(rb, carry):
            rows = _rows(rb, RB)
            ca_v = ca_ref[rows, :]
            mu = jnp.mean(ca_v, axis=-1, keepdims=True)
            xc = ca_v - mu
            rstd = lax.rsqrt(jnp.mean(xc * xc, axis=-1, keepdims=True) + EPS)
            xhat = xc * rstd
            ln = xhat * lng_ref[...] + lnb_ref[...]
            sl = _sigmoid(ln)
            ua = ln * sl
            a_z = split(2, rows)
            sz = _sigmoid(a_z)
            dpa = dpa_s[rows, :]
            put(2, rows, dpa * ua * (sz * (1.0 + a_z * (1.0 - sz))))
            dln = dpa * (a_z * sz) * (sl * (1.0 + ln * (1.0 - sl)))
            accum(ROW_DLNG, dln * xhat)
            accum(ROW_DLNB, dln)
            dxh = dln * lng_ref[...]
            dca = rstd * (dxh - jnp.mean(dxh, axis=-1, keepdims=True)
                          - xhat * jnp.mean(dxh * xhat, axis=-1, keepdims=True))
            accum(ROW_DCAB, dca)
            ext_d[rows, :] = dca
            return carry
        lax.fori_loop(0, nrb, branch_a, 0)

        def branch_b(rb, carry):
            rows = _rows(rb, RB)
            dua0_s[rows, :] = jnp.zeros((RB, d), F32)
            dm_s[rows, :] = split(0, rows) * _sigmoid(split(1, rows))
            b_z = split(6, rows)
            szb = _sigmoid(b_z)
            dpb = dpb_s[rows, :]
            b_b = split(3, rows)
            cb_v = cb_ref[rows, :]
            put(6, rows, dpb * (b_b * cb_v) * (szb * (1.0 + b_z * (1.0 - szb))))
            dub = dpb * (b_z * szb)
            put(3, rows, dub * cb_v)
            ext_e[rows, :] = dub * b_b
            return carry
        lax.fori_loop(0, nrb, branch_b, 0)

    def tile_conv_transposes(split, put, accum, wa_ref, wb_ref, ext_d, ext_e, sh, dm_s, dpb_s, dua0_s):
        for r, taps in _conv_a_taps(0, CONV_A - 1):
            if r == 0:
                src = ext_d
            else:
                sh[...] = ext_d[r:r + shl, :]
                src = sh

            def conv_t(rb, carry, src=src, taps=taps):
                rows = _rows(rb, RB)
                ua0 = dm_s[rows, :]
                dua0 = dua0_s[rows, :]
                for q, lag in taps:
                    k = CONV_A - 1 - lag
                    slab = src[pl.ds(pl.multiple_of(rb * RB + 8 * q, 8), RB), :]
                    dua0 = dua0 + slab * wa_ref[k:k + 1, :]
                    accum(ROW_DWA + k, slab * ua0)
                dua0_s[rows, :] = dua0
                return carry
            lax.fori_loop(0, nrb, conv_t, 0)
        ext_d[TM:TM + HALO_A, :] = ext_d[0:HALO_A, :]

        dpb_s[...] = ext_e[0:TM, :] * wb_ref[CONV_B - 1:CONV_B, :]
        for lag in range(CONV_B):
            k = CONV_B - 1 - lag
            if lag > 0:
                sh[0:TM, :] = ext_e[lag:lag + TM, :]
                dpb_s[...] += sh[0:TM, :] * wb_ref[k:k + 1, :]
            src = ext_e if lag == 0 else sh

            def conv_b_w(rb, carry, src=src, k=k):
                rows = _rows(rb, RB)
                accum(ROW_DWB + k, src[rows, :] * (split(4, rows) * split(5, rows)))
                return carry
            lax.fori_loop(0, nrb, conv_b_w, 0)
        ext_e[TM:TM + HALO_B, :] = ext_e[0:HALO_B, :]

        def inputs(rb, carry):
            rows = _rows(rb, RB)
            dua0 = dua0_s[rows, :]
            a_val = split(0, rows)
            sg = _sigmoid(split(1, rows))
            put(0, rows, dua0 * sg)
            put(1, rows, dua0 * a_val * sg * (1.0 - sg))
            dcbin = dpb_s[rows, :]
            put(4, rows, dcbin * split(5, rows))
            put(5, rows, dcbin * split(4, rows))
            return carry
        lax.fori_loop(0, nrb, inputs, 0)

    rev = lambda i: (_tile_block(nt - 1 - i, nt), 0)
    row_f32 = pl.BlockSpec((TM, d), rev)
    const = lambda shape: pl.BlockSpec(shape, lambda i: (0,) * len(shape))
    return pl.pallas_call(
        body, name="b1_mix", grid=(nt,),
        in_specs=[row_f32, pl.BlockSpec((TM, N_SPLIT * d), rev), row_f32, row_f32, row_f32, row_f32,
                  const((3, d, d)), const(wa.shape), const(wb.shape), const((1, d)), const((1, d))],
        out_specs=[pl.BlockSpec((TM, N_SPLIT * d), rev),
                   pl.BlockSpec((3, TM, d), lambda i: (0, _tile_block(nt - 1 - i, nt), 0)),
                   const((SM_ROWS, d))],
        out_shape=[jax.ShapeDtypeStruct((tp, N_SPLIT * d), BF16), jax.ShapeDtypeStruct((3, tp, d), BF16),
                   jax.ShapeDtypeStruct((SM_ROWS, d), F32)],
        scratch_shapes=[pltpu.VMEM((TM + HALO_A, d), F32), pltpu.VMEM((TM + HALO_B, d), F32),
                        pltpu.VMEM((shl, d), F32), pltpu.VMEM((TM, d), F32), pltpu.VMEM((TM, d), F32),
                        pltpu.VMEM((TM, d), F32), pltpu.VMEM((TM, d), F32),
                        pltpu.VMEM((SM_ROWS, 8, d), F32)],
        compiler_params=_params(("arbitrary",)),
    )(ds2, proj, ca, cb, ya, yb, w3, wa, wb, ln_g, ln_b)


def kernel(x, meta_tokens, norm_g, w_in, conv_a_w, conv_a_b, ln_a_g, ln_a_b, w_a_out, b_a_out, conv_b_w, w_b_out, w_out, final_g, loss_target, m_meta_tokens, m_norm_g, m_w_in, m_conv_a_w, m_conv_a_b, m_ln_a_g, m_ln_a_b, m_w_a_out, m_b_a_out, m_conv_b_w, m_w_b_out, m_w_out, m_final_g, v_meta_tokens, v_norm_g, v_w_in, v_conv_a_w, v_conv_a_b, v_ln_a_g, v_ln_a_b, v_w_a_out, v_b_a_out, v_conv_b_w, v_w_b_out, v_w_out, v_final_g):
    seq, d = x.shape[1], x.shape[2]
    dc = meta_tokens.shape[1]
    sw = w_in.shape[2]
    rsh = w_a_out.shape[1]
    xi, yi, ci = _mesh_pos()
    me = 2 * xi + yi
    pos = jnp.stack([ci, me]).astype(jnp.int32)

    conv_rows = HALO_A + HALO_B + 8
    convs = jnp.concatenate([
        jnp.pad(conv_a_w[0], ((0, HALO_A - CONV_A), (0, 0))),
        jnp.pad(conv_b_w[0], ((0, HALO_B - CONV_B), (0, 0))), jnp.zeros((8, dc), F32)], axis=0)[None]
    w3_own = jnp.stack([w_a_out[0], w_b_out[0], w_out[0]])
    fg2 = final_g.reshape(1, d)
    xs = x[0]

    h, front, placed = _h_prep(xs, meta_tokens, norm_g, [(w_in, BF16), (w3_own, BF16), (convs, F32)], pos)
    proj, (wg_in, wg3, convg) = _proj_fwd(h, placed, pos)
    w3 = wg3.reshape(3, N_CHIPS * rsh, d)
    convg = jnp.transpose(convg[0], (1, 0, 2)).reshape(conv_rows, N_CHIPS * dc)
    wa_full = convg[0:HALO_A]
    wb_full = convg[HALO_A:HALO_A + HALO_B]
    ca, cb, ya, yb, abm_t, ds2, h_t, loss8, dfg8 = _mix_fwd(
        xs, front, proj, loss_target[0], w3, wa_full, wb_full, conv_a_b, ln_a_g, ln_a_b, b_a_out, fg2, norm_g)
    dproj, d3, sm = _mix_bwd(ds2, proj, ca, cb, ya, yb, w3, wa_full, wb_full, ln_a_g, ln_a_b)
    cw_sq = _col_block(d, 512)
    per_sq = d // cw_sq
    p32_sq, pbf_sq, _ = _dw_reduced(
        abm_t, d3, cw_sq, 3 * per_sq, lambda t, pos_ref: (t // per_sq, t % per_sq), N_CHIPS,
        (3, N_CHIPS, rsh // 2, d), (None, N_CHIPS, rsh // 2, cw_sq),
        lambda u, pos_ref: (u // per_sq, 0, 0, u % per_sq), [], pos, "dw_square")
    cw_in = _col_block(sw, 768)
    ncol = sw // cw_in
    owner = lambda blk, pos_ref: _relay_order(blk // ncol, pos_ref[1])
    p32_in, pbf_in, (l_sq,) = _dw_reduced(
        h_t[None], dproj[None], cw_in, N_CHIPS * ncol,
        lambda t, pos_ref: (0, owner(t, pos_ref) * ncol + t % ncol), 1,
        (1, N_CHIPS, d // 2, sw), (None, None, d // 2, cw_in),
        lambda u, pos_ref: (0, owner(u, pos_ref), 0, u % ncol), [pbf_sq], pos, "dw_in", relay=ncol)
    grad_x, dfront, dng8, l_in = _dh_bwd(dproj, wg_in, xs, front, ds2, norm_g, pbf_in)
    half_in = _sum_chips([(p32_in, l_in)], sw, pos, "rs_sum_in", senders=(2, 1))
    half_sq = _sum_chips([(p32_sq, l_sq)], d, pos, "rs_sum_sq")
    tail_row = lax.broadcasted_iota(jnp.int32, (8, d), 0)
    tail = jnp.where(tail_row == 0, dng8, jnp.where(tail_row == 1, dfg8,
                     jnp.where(tail_row == 2, loss8[0, 0], 0.0)))
    block = jnp.concatenate([sm, dfront[TM - N_META:TM], tail], axis=0)
    (other_in, other_sq), red = _sibling_swap([half_in, half_sq], block)
    col = lax.dynamic_slice(red, (0, me * dc), (AR_ROWS, dc))
    g_small = {
        "meta_tokens": col[ROW_DMETA:ROW_DMETA + N_META],
        "norm_g": red[ROW_DNG:ROW_DNG + 1],
        "conv_a_w": col[ROW_DWA:ROW_DWA + CONV_A][None],
        "conv_a_b": red[ROW_DCAB:ROW_DCAB + 1],
        "ln_a_g": red[ROW_DLNG:ROW_DLNG + 1],
        "ln_a_b": red[ROW_DLNB:ROW_DLNB + 1],
        "b_a_out": red[ROW_DBAO:ROW_DBAO + 1],
        "conv_b_w": col[ROW_DWB:ROW_DWB + CONV_B][None],
        "final_g": red[ROW_DFG],
    }

    upd_in = _adam_halves([w_in], [m_w_in], [v_w_in], half_in, other_in, pos, "adam_in")
    upd_sq = _adam_halves([w_a_out, w_b_out, w_out], [m_w_a_out, m_w_b_out, m_w_out],
                          [v_w_a_out, v_w_b_out, v_w_out], half_sq, other_sq, pos, "adam_sq")
    small_w = {"meta_tokens": (meta_tokens, m_meta_tokens, v_meta_tokens), "norm_g": (norm_g, m_norm_g, v_norm_g),
               "conv_a_w": (conv_a_w, m_conv_a_w, v_conv_a_w), "conv_a_b": (conv_a_b, m_conv_a_b, v_conv_a_b),
               "ln_a_g": (ln_a_g, m_ln_a_g, v_ln_a_g), "ln_a_b": (ln_a_b, m_ln_a_b, v_ln_a_b),
               "b_a_out": (b_a_out, m_b_a_out, v_b_a_out), "conv_b_w": (conv_b_w, m_conv_b_w, v_conv_b_w),
               "final_g": (final_g, m_final_g, v_final_g)}
    names_small = list(small_w)
    as2d = lambda t: t.reshape(-1, t.shape[-1])
    upd_small = _adam_small([(as2d(small_w[k][0]), as2d(g_small[k]), as2d(small_w[k][1]), as2d(small_w[k][2]))
                             for k in names_small])

    grads, deltas, new_m, new_v = dict(g_small), {}, {}, {}
    for k, upd in zip(names_small, upd_small):
        deltas[k], new_m[k], new_v[k] = [t.reshape(small_w[k][0].shape) for t in upd]
    grads["w_in"], deltas["w_in"], new_m["w_in"], new_v["w_in"] = upd_in
    for idx, k in enumerate(["w_a_out", "w_b_out", "w_out"]):
        grads[k], deltas[k], new_m[k], new_v[k] = upd_sq[4 * idx:4 * idx + 4]

    loss = red[ROW_LOSS, 0]
    order = ["meta_tokens", "norm_g", "w_in", "conv_a_w", "conv_a_b", "ln_a_g", "ln_a_b", "w_a_out", "b_a_out",
             "conv_b_w", "w_b_out", "w_out", "final_g"]
    return (loss, grad_x[None], *[grads[k] for k in order], *[deltas[k] for k in order],
            *[new_m[k] for k in order], *[new_v[k] for k in order])
```

```python
import jax
import jax.numpy as jnp
from jax import lax
from jax.experimental import pallas as pl
from jax.experimental.pallas import tpu as pltpu

F32 = jnp.float32
BF16 = jnp.bfloat16
MESH = pl.DeviceIdType.MESH

EPS = 1e-6
N_META = 16
N_SPLIT = 9
CONV_A = 31
CONV_B = 3
HALO_A = 32
HALO_B = 8
SHIFT_ROWS = 24
TM = 256
RB_FWD = 256
RB_BWD = 64
N_ROW_TILES_BIG = 8
ROW_BLOCK = 256
N_CHIPS = 4
VMEM_LIMIT = 56 * 1024 * 1024
VMEM_LIMIT_BIG = 62 * 1024 * 1024

ADAM_LR = 0.001
ADAM_B1 = 0.9
ADAM_B2 = 0.999
ADAM_EPS = 1e-08
ADAM_WD = 0.01
ADAM_STEP = 10

ROW_DWA = 0
ROW_DWB = 32
ROW_DCAB = 40
ROW_DLNG = 41
ROW_DLNB = 42
ROW_DBAO = 43
SM_ROWS = 48
ROW_DMETA = 48
ROW_DNG = 64
ROW_DFG = 65
ROW_LOSS = 66
AR_ROWS = 72


def _sigmoid(v):
    return 0.5 * jnp.tanh(0.5 * v) + 0.5


def _params(sem, **kw):
    return pltpu.CompilerParams(dimension_semantics=sem, vmem_limit_bytes=VMEM_LIMIT, **kw)


def _rows(rb, n):
    return pl.ds(pl.multiple_of(rb * n, n), n)


def _mesh_pos():
    x, y, c = lax.axis_index("x"), lax.axis_index("y"), lax.axis_index("c")
    return x, y, c


def _half(ref, j, c):
    h = ref.shape[2] // 2
    return ref.at[:, j, pl.ds(c * h, h), :]


class _Exchange:
    def __init__(self, sends, recvs):
        self.sends, self.recvs = sends, recvs

    def start(self):
        for cp in self.sends:
            cp.start()

    def finish(self):
        for cp in self.recvs:
            cp.wait_recv()
        for cp in self.sends:
            cp.wait_send()


def _chip_exchange(part_ref, land_ref, send_sems, recv_sems):
    x, y, c = _mesh_pos()
    me = 2 * x + y
    sends, recvs = [], []
    for k, (px, py) in enumerate([(1 - x, y), (x, 1 - y), (1 - x, 1 - y)]):
        sems = dict(send_sem=send_sems.at[k], recv_sem=recv_sems.at[k], device_id=(px, py, c), device_id_type=MESH)
        sends.append(pltpu.make_async_remote_copy(
            src_ref=part_ref.at[:, 2 * px + py], dst_ref=land_ref.at[:, me], **sems))
        landed = land_ref.at[:, 2 * px + py]
        recvs.append(pltpu.make_async_remote_copy(src_ref=landed, dst_ref=landed, **sems))
    return _Exchange(sends, recvs)

def _sibling_swap(halves, small):
    n = len(halves)

    def body(*refs):
        ins, small_ref, outs, red_ref = refs[:n], refs[n], refs[n + 1:2 * n + 1], refs[2 * n + 1]
        send_sems, recv_sems = refs[2 * n + 2:2 * n + 4]
        reduce = _SmallAllReduce(small_ref, red_ref, *refs[2 * n + 4:])
        x, y, c = _mesh_pos()
        copies = [pltpu.make_async_remote_copy(
            src_ref=ins[a], dst_ref=outs[a], send_sem=send_sems.at[a], recv_sem=recv_sems.at[a],
            device_id=(x, y, 1 - c), device_id_type=MESH) for a in range(n)]
        reduce.start()
        for cp in copies:
            cp.start()
        reduce.between_chips()
        reduce.finish()
        for cp in copies:
            cp.wait()

    any_spec = pl.BlockSpec(memory_space=pl.ANY)
    vm = pl.BlockSpec(memory_space=pltpu.VMEM)
    outs = pl.pallas_call(
        body, name="rs_swap",
        in_specs=[any_spec] * n + [vm], out_specs=[any_spec] * n + [vm],
        out_shape=[jax.ShapeDtypeStruct(h.shape, h.dtype) for h in halves]
        + [jax.ShapeDtypeStruct(small.shape, F32)],
        scratch_shapes=[pltpu.SemaphoreType.DMA((n,)), pltpu.SemaphoreType.DMA((n,))]
        + _SmallAllReduce.scratch(*small.shape),
    )(*halves, small)
    return outs[:n], outs[n]


class _SmallAllReduce:
    def __init__(self, x_ref, out_ref, sib_ref, part_ref, peers_ref, send_sems, recv_sems):
        self.x_ref, self.out_ref, self.sib_ref, self.part_ref, self.peers_ref = x_ref, out_ref, sib_ref, part_ref, peers_ref
        x, y, c = _mesh_pos()
        self.me = 2 * x + y
        self.swap = pltpu.make_async_remote_copy(
            src_ref=x_ref, dst_ref=sib_ref, send_sem=send_sems.at[0], recv_sem=recv_sems.at[0],
            device_id=(x, y, 1 - c), device_id_type=MESH)
        self.sends, self.recvs = [], []
        for k, (px, py) in enumerate([(1 - x, y), (x, 1 - y), (1 - x, 1 - y)]):
            sems = dict(send_sem=send_sems.at[1 + k], recv_sem=recv_sems.at[1 + k],
                        device_id=(px, py, c), device_id_type=MESH)
            self.sends.append(pltpu.make_async_remote_copy(src_ref=part_ref, dst_ref=peers_ref.at[self.me], **sems))
            landed = peers_ref.at[2 * px + py]
            self.recvs.append(pltpu.make_async_remote_copy(src_ref=landed, dst_ref=landed, **sems))

    @staticmethod
    def scratch(rows, d):
        return [pltpu.VMEM((rows, d), F32), pltpu.VMEM((rows, d), F32), pltpu.VMEM((N_CHIPS, rows, d), F32),
                pltpu.SemaphoreType.DMA((4,)), pltpu.SemaphoreType.DMA((4,))]

    def start(self):
        self.swap.start()

    def between_chips(self):
        self.swap.wait()
        self.part_ref[...] = self.x_ref[...] + self.sib_ref[...]
        self.peers_ref[self.me] = self.part_ref[...]
        for cp in self.sends:
            cp.start()

    def finish(self):
        for cp in self.recvs:
            cp.wait_recv()
        for cp in self.sends:
            cp.wait_send()
        p = self.peers_ref
        self.out_ref[...] = ((p[0] + p[1]) + p[2]) + p[3]


def _sum_chips(parts, cw, pos, name):
    s, _, h, _ = parts[0][0].shape
    hb = min(h, ROW_BLOCK)
    widths = [own.shape[3] // cw for own, _ in parts]
    starts = [sum(widths[:a]) for a in range(len(parts))]

    def body(pos_ref, *refs):
        out_ref = refs[-1]
        n = pl.program_id(2)
        total = None
        for a in range(len(parts)):
            own, l1, l2, l3 = refs[4 * a:4 * a + 4]
            val = ((own[...] + l1[...].astype(F32)) + l2[...].astype(F32)) + l3[...].astype(F32)
            total = val if total is None else jnp.where(n >= starts[a], val, total)
        out_ref[...] = total

    def slot(a, k):
        col = lambda n: jnp.clip(n - starts[a], 0, widths[a] - 1)
        return pl.BlockSpec((None, None, hb, cw),
                            lambda si, b, n, pos_ref: (si, (pos_ref[1] + k) % N_CHIPS, b, col(n)))

    operands, specs = [], []
    for a, (own, landed) in enumerate(parts):
        operands += [own, landed, landed, landed]
        specs += [slot(a, 0), slot(a, 1), slot(a, 2), slot(a, 3)]
    return pl.pallas_call(
        body, name=name,
        grid_spec=pltpu.PrefetchScalarGridSpec(
            num_scalar_prefetch=1, grid=(s, h // hb, sum(widths)), in_specs=specs,
            out_specs=pl.BlockSpec((None, hb, cw), lambda si, b, n, pos_ref: (si, b, n))),
        out_shape=jax.ShapeDtypeStruct((s, h, sum(widths) * cw), F32),
        compiler_params=_params(("arbitrary",) * 3),
    )(pos, *operands)


def _adamw(w, g, m, v):
    m = ADAM_B1 * m + (1.0 - ADAM_B1) * g
    v = ADAM_B2 * v + (1.0 - ADAM_B2) * (g * g)
    m_hat = m / (1.0 - ADAM_B1 ** ADAM_STEP)
    v_hat = v / (1.0 - ADAM_B2 ** ADAM_STEP)
    delta = -ADAM_LR * (m_hat / (jnp.sqrt(v_hat) + ADAM_EPS) + ADAM_WD * w)
    return delta, m, v


def _adam_halves(ws, ms, vs, g_own, g_recv, pos, name):
    n = len(ws)
    _, r, c = ws[0].shape
    h = r // 2
    rb = min(h, ROW_BLOCK)
    nb = h // rb

    def body(pos_ref, *refs):
        w_refs, m_refs, v_refs = refs[:n], refs[n:2 * n], refs[2 * n:3 * n]
        go_ref, gr_ref = refs[3 * n:3 * n + 2]
        outs = refs[3 * n + 2:]
        mine = pl.program_id(0) == pos_ref[0]
        for a in range(n):
            g = jnp.where(mine, go_ref[a], gr_ref[a])
            delta, m, v = _adamw(w_refs[a][...], g, m_refs[a][...], v_refs[a][...])
            outs[4 * a][...], outs[4 * a + 1][...], outs[4 * a + 2][...], outs[4 * a + 3][...] = g, delta, m, v

    spec_w = pl.BlockSpec((None, rb, c), lambda hf, b, pos_ref: (0, hf * nb + b, 0))
    spec_g = pl.BlockSpec((n, rb, c), lambda hf, b, pos_ref: (0, b, 0))
    return pl.pallas_call(
        body, name=name,
        grid_spec=pltpu.PrefetchScalarGridSpec(
            num_scalar_prefetch=1, grid=(2, nb), in_specs=[spec_w] * (3 * n) + [spec_g] * 2,
            out_specs=[spec_w] * (4 * n)),
        out_shape=[jax.ShapeDtypeStruct((1, r, c), F32)] * (4 * n),
        compiler_params=_params(("arbitrary",) * 2),
    )(pos, *ws, *ms, *vs, g_own, g_recv)


def _adam_small(items):
    n = len(items)

    def body(*refs):
        ins, outs = refs[:4 * n], refs[4 * n:]
        for a in range(n):
            w_ref, g_ref, m_ref, v_ref = ins[4 * a:4 * a + 4]
            d, m, v = _adamw(w_ref[...], g_ref[...], m_ref[...], v_ref[...])
            outs[3 * a][...] = d
            outs[3 * a + 1][...] = m
            outs[3 * a + 2][...] = v

    vm = pl.BlockSpec(memory_space=pltpu.VMEM)
    flat = [t for it in items for t in it]
    outs = pl.pallas_call(
        body, name="adam_small", in_specs=[vm] * (4 * n), out_specs=[vm] * (3 * n),
        out_shape=[jax.ShapeDtypeStruct(it[0].shape, F32) for it in items for _ in range(3)],
    )(*flat)
    return [tuple(outs[3 * a:3 * a + 3]) for a in range(n)]


def _big_row_spec(seq, tmb, d, tile_of):
    return pl.BlockSpec((pl.Element(tmb), pl.Element(d)),
                        lambda *args: (pl.multiple_of(jnp.minimum(tile_of(*args) * tmb, seq - tmb), 8), 0))


def _big_row_tile(x_ref, front_ref, i):
    rows = x_ref[...]
    last = jnp.concatenate([rows[TM:], front_ref[...]], axis=0)
    return jnp.where(i == N_ROW_TILES_BIG - 1, last, rows)


def _h_prep(x, meta, norm_g, shards, pos):
    seq, d = x.shape
    tp = seq + TM
    dc = meta.shape[1]
    tmb = tp // N_ROW_TILES_BIG
    assert tmb >= TM and tp == tmb * N_ROW_TILES_BIG
    last = N_ROW_TILES_BIG - 1
    ns = len(shards)

    def body(pos_ref, x_ref, meta_ref, g_ref, *refs):
        shard_refs, (h_ref, front_ref), placed_refs = refs[:ns], refs[ns:ns + 2], refs[ns + 2:2 * ns + 2]
        metas, msend, mrecv = refs[2 * ns + 2:]
        for a in range(ns):
            placed_refs[a][...] = shard_refs[a][...].astype(placed_refs[a].dtype)
        x, y, c = _mesh_pos()
        me = 2 * x + y
        chips = [(1 - x, y), (x, 1 - y), (1 - x, 1 - y)]
        i = pl.program_id(0)

        def meta_copy(k, chip):
            return pltpu.make_async_remote_copy(
                src_ref=metas.at[chip], dst_ref=metas.at[chip], send_sem=msend.at[k], recv_sem=mrecv.at[k],
                device_id=(*chips[k], c), device_id_type=MESH)

        @pl.when(i == 0)
        def _():
            metas[me] = meta_ref[...]
            for k in range(3):
                meta_copy(k, me).start()
            front_ref[...] = jnp.zeros_like(front_ref)

        @pl.when(i == last)
        def _():
            for k, (px, py) in enumerate(chips):
                meta_copy(k, 2 * px + py).wait_recv()
            for q in range(N_CHIPS):
                front_ref[TM - N_META:TM, q * dc:(q + 1) * dc] = metas[q]

        s = _big_row_tile(x_ref, front_ref, i)
        r = lax.rsqrt(jnp.mean(s * s, axis=-1, keepdims=True) + EPS)
        h_ref[...] = (s * r * g_ref[...]).astype(BF16)

        @pl.when(i == last)
        def _():
            for k in range(3):
                meta_copy(k, me).wait_send()

    shard_in, shard_out, shard_shapes = [], [], []
    for arr, dtype in shards:
        s, r, c = arr.shape
        sliced = r % (N_ROW_TILES_BIG * 16) == 0
        rp = r // N_ROW_TILES_BIG if sliced else r
        step = (lambda i: i) if sliced else (lambda i: 0)
        shard_in.append(pl.BlockSpec((s, rp, c), lambda i, pos_ref, step=step: (0, step(i), 0)))
        shard_out.append(pl.BlockSpec((s, None, rp, c), lambda i, pos_ref, step=step: (0, pos_ref[1], step(i), 0)))
        shard_shapes.append(jax.ShapeDtypeStruct((s, N_CHIPS, r, c), dtype))
    outs = pl.pallas_call(
        body, name="f0_norm",
        grid_spec=pltpu.PrefetchScalarGridSpec(
            num_scalar_prefetch=1, grid=(N_ROW_TILES_BIG,),
            in_specs=[_big_row_spec(seq, tmb, d, lambda i, pos_ref: i),
                      pl.BlockSpec(meta.shape, lambda i, pos_ref: (0, 0)),
                      pl.BlockSpec((1, d), lambda i, pos_ref: (0, 0))] + shard_in,
            out_specs=[pl.BlockSpec((tmb, d), lambda i, pos_ref: (i, 0)),
                       pl.BlockSpec((TM, d), lambda i, pos_ref: (0, 0))] + shard_out,
            scratch_shapes=[pltpu.VMEM((N_CHIPS,) + meta.shape, F32),
                            pltpu.SemaphoreType.DMA((3,)), pltpu.SemaphoreType.DMA((3,))]),
        out_shape=[jax.ShapeDtypeStruct((tp, d), BF16), jax.ShapeDtypeStruct((TM, d), F32)] + shard_shapes,
        compiler_params=_params(("arbitrary",)),
    )(pos, x, meta, norm_g, *[arr for arr, _ in shards])
    return outs[0], outs[1], outs[2:]


N_UNITS = 3
N_STEPS_PROJ = N_CHIPS * N_UNITS


def _proj_plan(u):
    v = u - N_UNITS
    if v < 2 * N_UNITS:
        return v % 2, v // 2
    return 2, v - 2 * N_UNITS


def _proj_unit(t, me):
    v = t - N_UNITS
    near = v < 2 * N_UNITS
    rel = jnp.where(near, v % 2, 2)
    unit = jnp.where(near, v // 2, v - 2 * N_UNITS)
    flip = jnp.where(rel == 0, 2, jnp.where(rel == 1, 1, 3))
    own = t < N_UNITS
    return jnp.where(own, me, lax.bitwise_xor(me, flip)), jnp.where(own, t, unit)


def _proj_fwd(h, bufs, pos):
    tp, d = h.shape
    _, nsh, _, sw = bufs[0].shape
    cu = sw // N_UNITS
    assert cu % 128 == 0
    n = len(bufs)
    w_sems = 6 * N_UNITS
    last = N_STEPS_PROJ - 1
    late = N_STEPS_PROJ - N_UNITS

    def body(pos_ref, h_ref, *refs):
        proj_ref = refs[n]
        gbufs = refs[n + 1:2 * n + 1]
        wbuf, wsems, send_sems, recv_sems = refs[2 * n + 1:]
        x, y, c = _mesh_pos()
        me = 2 * x + y
        sibling = (x, y, 1 - c)
        chips = [(1 - x, y), (x, 1 - y), (1 - x, 1 - y)]
        chip_ids = [2 * px + py for px, py in chips]
        relayed_chip = jnp.where(c == 0, chip_ids[0], chip_ids[1])
        relay_to = (jnp.where(c == 0, x, 1 - x), jnp.where(c == 0, 1 - y, y), c)
        t = pl.program_id(0)

        def remote(idx, piece, to):
            return pltpu.make_async_remote_copy(
                src_ref=piece, dst_ref=piece, send_sem=send_sems.at[idx], recv_sem=recv_sems.at[idx],
                device_id=to, device_id_type=MESH)

        hr = d // 2

        def chunk_of(chip, half, k):
            return gbufs[0].at[0, chip, pl.ds(half * hr, hr), pl.ds(k * cu, cu)]

        def own_chunk(r, k):
            return remote(6 * k + r, chunk_of(me, c, k), (*chips[r], c))

        def landed_chunk(r, k):
            return remote(6 * k + r, chunk_of(chip_ids[r], c, k), (*chips[r], c))

        def relay_chunk(k):
            return remote(6 * k + 2, chunk_of(relayed_chip, c, k), relay_to)

        def sibling_chunk(r, k, half):
            return remote(6 * k + 3 + r, chunk_of(chip_ids[r], half, k), sibling)

        def fetch(u):
            chip, unit = _proj_unit(jnp.int32(u), me)
            return pltpu.make_async_copy(gbufs[0].at[0, chip, :, pl.ds(pl.multiple_of(unit * cu, 128), cu)],
                                         wbuf.at[u % 2], wsems.at[u % 2])

        def make_available(u):
            r, k = _proj_plan(u)
            landed_chunk(r, k).wait_recv()
            if r < 2:
                pl.when(c == r)(lambda: relay_chunk(k).start())
            sibling_chunk(r, k, c).start()
            sibling_chunk(r, k, 1 - c).wait_recv()

        def own_piece(a, r):
            return remote(w_sems + 6 * (a - 1) + r, _half(gbufs[a], me, c), (*chips[r], c))

        def relay(a):
            return remote(w_sems + 6 * (a - 1) + 2, _half(gbufs[a], relayed_chip, c), relay_to)

        def to_sibling(a, r, core):
            return remote(w_sems + 6 * (a - 1) + 3 + r, _half(gbufs[a], chip_ids[r], core), sibling)

        def landed(a, r):
            return remote(w_sems + 6 * (a - 1) + r, _half(gbufs[a], chip_ids[r], c), (*chips[r], c))

        for u in range(N_STEPS_PROJ):
            @pl.when(t == u)
            def _(u=u):
                if u == 0:
                    for k in range(N_UNITS):
                        for r in range(2):
                            own_chunk(r, k).start()
                    for a in range(1, n):
                        for r in range(2):
                            own_piece(a, r).start()
                    fetch(0).start()
                if u < last:
                    if u + 1 >= N_UNITS:
                        make_available(u + 1)
                    fetch(u + 1).start()
                if u == late:
                    for a in range(1, n):
                        landed(a, 0).wait_recv()
                        landed(a, 1).wait_recv()
                        relay(a).start()
                        for r in range(2):
                            to_sibling(a, r, c).start()
                        for r in range(2):
                            to_sibling(a, r, 1 - c).wait_recv()
                fetch(u).wait()

        proj_ref[...] = jnp.dot(h_ref[...], wbuf[t % 2], preferred_element_type=F32).astype(BF16)

        @pl.when(t == last)
        def _():
            for a in range(1, n):
                landed(a, 2).wait_recv()
                to_sibling(a, 2, c).start()
                to_sibling(a, 2, 1 - c).wait_recv()
            for k in range(N_UNITS):
                for r in range(2):
                    own_chunk(r, k).wait_send()
                relay_chunk(k).wait_send()
                for r in range(3):
                    sibling_chunk(r, k, c).wait_send()
            for a in range(1, n):
                for r in range(2):
                    own_piece(a, r).wait_send()
                relay(a).wait_send()
                for r in range(3):
                    to_sibling(a, r, c).wait_send()

    def out_index(t, pos_ref):
        chip, unit = _proj_unit(t, pos_ref[1])
        return 0, chip * N_UNITS + unit

    any_spec = pl.BlockSpec(memory_space=pl.ANY)
    outs = pl.pallas_call(
        body, name="f1_proj",
        grid_spec=pltpu.PrefetchScalarGridSpec(
            num_scalar_prefetch=1, grid=(N_STEPS_PROJ,),
            in_specs=[pl.BlockSpec((tp, d), lambda t, pos_ref: (0, 0))] + [any_spec] * n,
            out_specs=[pl.BlockSpec((tp, cu), out_index)] + [any_spec] * n,
            scratch_shapes=[pltpu.VMEM((2, d, cu), BF16), pltpu.SemaphoreType.DMA((2,)),
                            pltpu.SemaphoreType.DMA((w_sems + 6 * (n - 1),)),
                            pltpu.SemaphoreType.DMA((w_sems + 6 * (n - 1),))]),
        out_shape=[jax.ShapeDtypeStruct((tp, nsh * sw), BF16)]
        + [jax.ShapeDtypeStruct(b.shape, b.dtype) for b in bufs],
        input_output_aliases={2 + a: 1 + a for a in range(n)},
        compiler_params=_params(("arbitrary",)),
    )(pos, h, *bufs)
    return outs[0], outs[1:]


def _dh_bwd(dproj, wg_in, x, front, ds2, norm_g, part):
    seq, d = x.shape
    tp = seq + TM
    _, nsh, _, sw = wg_in.shape
    tmb = tp // N_ROW_TILES_BIG
    tail = tmb - TM
    last = N_ROW_TILES_BIG - 1

    def body(dp_ref, w_hbm, x_ref, front_ref, ds2_ref, g_ref, part_ref, gx_hbm, dfront_ref, dng_ref, land_ref,
             wbuf, gacc, dsbuf, wsem, osems, send_sems, recv_sems):
        exchange = _chip_exchange(part_ref, land_ref, send_sems, recv_sems)
        i = pl.program_id(0)

        def x_rows_out(step):
            return pltpu.make_async_copy(dsbuf.at[step % 2], gx_hbm.at[pl.ds(step * tmb, tmb), :], osems.at[step % 2])

        last_out = pltpu.make_async_copy(dsbuf.at[last % 2, pl.ds(0, tail), :],
                                         gx_hbm.at[pl.ds(last * tmb, tail), :], osems.at[last % 2])

        @pl.when(i == 0)
        def _():
            exchange.start()
            gacc[...] = jnp.zeros_like(gacc)
            whole = pltpu.make_async_copy(w_hbm.at[0], wbuf, wsem)
            whole.start()
            whole.wait()

        dh = None
        for j in range(nsh):
            part = lax.dot_general(dp_ref[:, j * sw:(j + 1) * sw], wbuf[j], (((1,), (1,)), ((), ())),
                                   preferred_element_type=F32)
            dh = part if dh is None else dh + part
        s = _big_row_tile(x_ref, front_ref, i)
        r = lax.rsqrt(jnp.mean(s * s, axis=-1, keepdims=True) + EPS)
        gacc[...] += (dh * s * r).reshape(tmb // 8, 8, d).sum(axis=0)
        t = dh * g_ref[...]

        @pl.when(i >= 2)
        def _():
            x_rows_out(i - 2).wait()

        dsbuf[i % 2] = ds2_ref[...] + r * t - s * (r * r * r) * jnp.mean(t * s, axis=-1, keepdims=True)

        @pl.when(i < last)
        def _():
            x_rows_out(i).start()

        @pl.when(i == last)
        def _():
            last_out.start()
            dfront_ref[...] = dsbuf[last % 2, tail:, :]
            dng_ref[...] = jnp.broadcast_to(jnp.sum(gacc[...], axis=0, keepdims=True), (8, d))
            exchange.finish()
            x_rows_out(last - 1).wait()
            last_out.wait()

    any_spec = pl.BlockSpec(memory_space=pl.ANY)
    return pl.pallas_call(
        body, name="b2_dh", grid=(N_ROW_TILES_BIG,),
        in_specs=[pl.BlockSpec((tmb, nsh * sw), lambda i: (i, 0)), any_spec,
                  _big_row_spec(seq, tmb, d, lambda i: i),
                  pl.BlockSpec((TM, d), lambda i: (0, 0)),
                  pl.BlockSpec((tmb, d), lambda i: (i, 0)),
                  pl.BlockSpec((1, d), lambda i: (0, 0)), any_spec],
        out_specs=[any_spec, pl.BlockSpec((TM, d), lambda i: (0, 0)),
                   pl.BlockSpec((8, d), lambda i: (0, 0)), any_spec],
        out_shape=[jax.ShapeDtypeStruct((seq, d), F32), jax.ShapeDtypeStruct((TM, d), F32),
                   jax.ShapeDtypeStruct((8, d), F32), jax.ShapeDtypeStruct(part.shape, part.dtype)],
        scratch_shapes=[pltpu.VMEM((nsh, d, sw), BF16), pltpu.VMEM((8, d), F32), pltpu.VMEM((2, tmb, d), F32),
                        pltpu.SemaphoreType.DMA, pltpu.SemaphoreType.DMA((2,)),
                        pltpu.SemaphoreType.DMA((3,)), pltpu.SemaphoreType.DMA((3,))],
        compiler_params=pltpu.CompilerParams(dimension_semantics=("arbitrary",),
                                             vmem_limit_bytes=VMEM_LIMIT_BIG),
    )(dproj, wg_in, x, front, ds2, norm_g, part)


def _col_block(width, cap):
    return max(b for b in range(128, cap + 1, 128) if width % b == 0)


def _dw_reduced(lhs_t, rhs, cw, nblk, operands, groups, out_dims, out_block, out_index, carried, name):
    na, d, tp = lhs_t.shape
    rg = d // groups
    hh = rg // 2
    nc = len(carried)

    def body(*refs):
        l_ref, r_ref = refs[:2]
        part_refs = refs[2:2 + nc]
        p32_ref, pbf_ref = refs[2 + nc:4 + nc]
        land_refs = refs[4 + nc:4 + 2 * nc]
        res, rbuf, send_sems, recv_sems = refs[4 + 2 * nc:8 + 2 * nc]
        xsems = refs[8 + 2 * nc:]
        exchanges = [_chip_exchange(part_refs[e], land_refs[e], xsems[2 * e], xsems[2 * e + 1]) for e in range(nc)]
        exchange = _Exchange([s for ex in exchanges for s in ex.sends], [r for ex in exchanges for r in ex.recvs])
        x, y, c = _mesh_pos()
        t = pl.program_id(0)
        u = jnp.maximum(t - 1, 0)

        def to_sibling(blk):
            return pltpu.make_async_remote_copy(
                src_ref=res.at[blk % 2, :, pl.ds((1 - c) * hh, hh), :], dst_ref=rbuf.at[blk % 2],
                send_sem=send_sems.at[blk], recv_sem=recv_sems.at[blk],
                device_id=(x, y, 1 - c), device_id_type=MESH)

        @pl.when(t == 0)
        def _():
            exchange.start()

        @pl.when(t < nblk)
        def _():
            res[t % 2] = jnp.dot(l_ref[...], r_ref[...], preferred_element_type=F32).reshape(groups, rg, cw)

        @pl.when(t >= 1)
        def _():
            to_sibling(u).wait_recv()
            p = res[u % 2, :, pl.ds(c * hh, hh), :] + rbuf[u % 2]
            p32_ref[...] = p.reshape(p32_ref.shape)
            pbf_ref[...] = p.reshape(pbf_ref.shape).astype(BF16)

        @pl.when(t < nblk)
        def _():
            to_sibling(t).start()

        @pl.when(t >= 1)
        def _():
            to_sibling(u).wait_send()

        @pl.when(t == nblk)
        def _():
            exchange.finish()

    any_spec = pl.BlockSpec(memory_space=pl.ANY)
    last = nblk - 1
    out_spec = pl.BlockSpec(out_block, lambda t: out_index(jnp.maximum(t - 1, 0)))
    outs = pl.pallas_call(
        body, name=name, grid=(nblk + 1,),
        in_specs=[pl.BlockSpec((None, d, tp), lambda t: (operands(jnp.minimum(t, last))[0], 0, 0)),
                  pl.BlockSpec((None, tp, cw), lambda t: (operands(jnp.minimum(t, last))[0], 0,
                                                          operands(jnp.minimum(t, last))[1]))]
        + [any_spec] * nc,
        out_specs=[out_spec, out_spec] + [any_spec] * nc,
        out_shape=[jax.ShapeDtypeStruct(out_dims, F32), jax.ShapeDtypeStruct(out_dims, BF16)]
        + [jax.ShapeDtypeStruct(e.shape, e.dtype) for e in carried],
        scratch_shapes=[pltpu.VMEM((2, groups, rg, cw), F32), pltpu.VMEM((2, groups, hh, cw), F32),
                        pltpu.SemaphoreType.DMA((nblk,)), pltpu.SemaphoreType.DMA((nblk,))]
        + [pltpu.SemaphoreType.DMA((3,)), pltpu.SemaphoreType.DMA((3,))] * nc,
        compiler_params=_params(("arbitrary",)),
    )(lhs_t, rhs, *carried)
    return outs[0], outs[1], outs[2:]


def _conv_a_taps(first_lag, last_lag):
    out = []
    for r in range(8):
        taps = [(q, 8 * q + r) for q in range(5) if first_lag <= 8 * q + r <= last_lag]
        if taps:
            out.append((r, taps))
    return out


def _tile_block(i, nt):
    return jnp.where(i == 0, nt - 1, i - 1)


def _mix_fwd(x, front, proj, target, w3, wa, wb, conv_a_b, ln_g, ln_b, b_a_out, final_g, norm_g):
    seq, d = x.shape
    tp = seq + TM
    nt = tp // TM
    RB = RB_FWD
    nrb = TM // RB
    shl = TM + SHIFT_ROWS

    def body(x_ref, front_ref, proj_ref, tgt_ref, w3_ref, wa_ref, wb_ref, cab_ref, lng_ref, lnb_ref, bao_ref, fg_ref,
             ng_ref, ca_ref, cb_ref, ya_ref, yb_ref, abmt_ref, ds2_ref, ht_ref, loss_ref, dfg_ref,
             abm_ref, ext_a, ext_b, sh, s2_s, lacc, gacc):
        i = pl.program_id(0)

        def split(k, rows):
            return proj_ref[rows, k * d:(k + 1) * d].astype(F32)

        def s_tile():
            return jnp.where(i == 0, front_ref[...], x_ref[...])

        s_in = s_tile()
        h = s_in * lax.rsqrt(jnp.mean(s_in * s_in, axis=-1, keepdims=True) + EPS) * ng_ref[...]
        ht_ref[...] = h.astype(BF16).T

        @pl.when(i == 0)
        def _():
            ext_a[0:HALO_A, :] = jnp.zeros((HALO_A, d), F32)
            ext_b[0:HALO_B, :] = jnp.zeros((HALO_B, d), F32)
            lacc[...] = jnp.zeros_like(lacc)
            gacc[...] = jnp.zeros_like(gacc)

        def conv_in(rb, carry):
            rows = _rows(rb, RB)
            ua0 = split(0, rows) * _sigmoid(split(1, rows))
            ext_a[pl.ds(pl.multiple_of(HALO_A + rb * RB, 8), RB), :] = ua0
            ext_b[pl.ds(pl.multiple_of(HALO_B + rb * RB, 8), RB), :] = split(4, rows) * split(5, rows)
            ca_ref[rows, :] = jnp.broadcast_to(cab_ref[...], (RB, d))
            return carry
        lax.fori_loop(0, nrb, conv_in, 0)

        @pl.when(i == 0)
        def _():
            abmt_ref[...] = jnp.zeros_like(abmt_ref)
            ds2_ref[...] = jnp.zeros_like(ds2_ref)

        @pl.when(i > 0)
        def _():
            tile_after_conv_inputs(split, s_tile, tgt_ref, w3_ref, wa_ref, wb_ref, lng_ref, lnb_ref, bao_ref, fg_ref,
                                   ca_ref, cb_ref, ya_ref, yb_ref, abmt_ref, ds2_ref, abm_ref, ext_a, ext_b, sh,
                                   s2_s, lacc, gacc)

        ext_a[0:HALO_A, :] = ext_a[TM:TM + HALO_A, :]
        ext_b[0:HALO_B, :] = ext_b[TM:TM + HALO_B, :]

        @pl.when(i == nt - 1)
        def _():
            loss_ref[...] = jnp.broadcast_to(0.5 * jnp.sum(lacc[...]) * (1.0 / d), (8, 128))
            dfg_ref[...] = jnp.broadcast_to(jnp.sum(gacc[...], axis=0, keepdims=True), (8, d))

    def tile_after_conv_inputs(split, s_tile, tgt_ref, w3_ref, wa_ref, wb_ref, lng_ref, lnb_ref, bao_ref, fg_ref,
                               ca_ref, cb_ref, ya_ref, yb_ref, abmt_ref, ds2_ref, abm_ref, ext_a, ext_b, sh, s2_s,
                               lacc, gacc):
        for r, taps in _conv_a_taps(HALO_A - CONV_A + 1, HALO_A):
            if r == 0:
                src = ext_a
            else:
                sh[...] = ext_a[r:r + shl, :]
                src = sh

            def conv_acc(rb, carry, src=src, taps=taps):
                rows = _rows(rb, RB)
                acc = ca_ref[rows, :]
                for q, lag in taps:
                    k = lag - (HALO_A - CONV_A + 1)
                    acc = acc + src[pl.ds(pl.multiple_of(rb * RB + 8 * q, 8), RB), :] * wa_ref[k:k + 1, :]
                ca_ref[rows, :] = acc
                return carry
            lax.fori_loop(0, nrb, conv_acc, 0)

        cb_ref[...] = ext_b[HALO_B:HALO_B + TM, :] * wb_ref[2:3, :]
        for k in range(CONV_B - 1):
            off = HALO_B - CONV_B + 1 + k
            sh[0:TM, :] = ext_b[off:off + TM, :]
            cb_ref[...] += sh[0:TM, :] * wb_ref[k:k + 1, :]

        def branches(rb, carry):
            rows = _rows(rb, RB)
            ca = ca_ref[rows, :]
            mu = jnp.mean(ca, axis=-1, keepdims=True)
            xc = ca - mu
            rstd = lax.rsqrt(jnp.mean(xc * xc, axis=-1, keepdims=True) + EPS)
            ln = xc * rstd * lng_ref[...] + lnb_ref[...]
            ua = ln * _sigmoid(ln)
            a_z = split(2, rows)
            abm_ref[0, rows, :] = (ua * (a_z * _sigmoid(a_z))).astype(BF16)
            return carry
        lax.fori_loop(0, nrb, branches, 0)

        def branch_b(rb, carry):
            rows = _rows(rb, RB)
            b_z = split(6, rows)
            ub = split(3, rows) * cb_ref[rows, :]
            abm_ref[1, rows, :] = (ub * (b_z * _sigmoid(b_z))).astype(BF16)
            return carry
        lax.fori_loop(0, nrb, branch_b, 0)

        ya_ref[...] = jnp.dot(abm_ref[0], w3_ref[0], preferred_element_type=F32) + bao_ref[...]
        yb_ref[...] = jnp.dot(abm_ref[1], w3_ref[1], preferred_element_type=F32)

        def merge(rb, carry):
            rows = _rows(rb, RB)
            m = _sigmoid(split(7, rows)) * ya_ref[rows, :] + _sigmoid(split(8, rows)) * yb_ref[rows, :]
            abm_ref[2, rows, :] = m.astype(BF16)
            return carry
        lax.fori_loop(0, nrb, merge, 0)

        s2_s[...] = s_tile() + jnp.dot(abm_ref[2], w3_ref[2], preferred_element_type=F32)
        for k in range(3):
            abmt_ref[k] = abm_ref[k].T

        def head(rb, carry):
            rows = _rows(rb, RB)
            s2 = s2_s[rows, :]
            r2 = lax.rsqrt(jnp.mean(s2 * s2, axis=-1, keepdims=True) + EPS)
            diff = s2 * r2 * fg_ref[...] - tgt_ref[rows, :]
            lacc[...] += diff * diff
            dy = diff * (1.0 / d)
            gacc[...] += (dy * s2 * r2).reshape(RB // 8, 8, d).sum(axis=0)
            t = dy * fg_ref[...]
            ds2_ref[rows, :] = r2 * t - s2 * (r2 * r2 * r2) * jnp.mean(t * s2, axis=-1, keepdims=True)
            return carry
        lax.fori_loop(0, nrb, head, 0)

    row_f32 = pl.BlockSpec((TM, d), lambda i: (_tile_block(i, nt), 0))
    x_rows = pl.BlockSpec((TM, d), lambda i: (jnp.maximum(i - 1, 0), 0))
    const = lambda shape: pl.BlockSpec(shape, lambda i: (0,) * len(shape))
    return pl.pallas_call(
        body, name="f2_mix", grid=(nt,),
        in_specs=[x_rows, const((TM, d)),
                  pl.BlockSpec((TM, N_SPLIT * d), lambda i: (_tile_block(i, nt), 0)),
                  x_rows,
                  const((3, d, d)), const(wa.shape), const(wb.shape)] + [const((1, d))] * 6,
        out_specs=[row_f32, row_f32, row_f32, row_f32,
                   pl.BlockSpec((3, d, TM), lambda i: (0, 0, _tile_block(i, nt))),
                   row_f32, pl.BlockSpec((d, TM), lambda i: (0, _tile_block(i, nt))),
                   const((8, 128)), const((8, d))],
        out_shape=[jax.ShapeDtypeStruct((tp, d), F32)] * 4
        + [jax.ShapeDtypeStruct((3, d, tp), BF16), jax.ShapeDtypeStruct((tp, d), F32),
           jax.ShapeDtypeStruct((d, tp), BF16),
           jax.ShapeDtypeStruct((8, 128), F32), jax.ShapeDtypeStruct((8, d), F32)],
        scratch_shapes=[pltpu.VMEM((3, TM, d), BF16),
                        pltpu.VMEM((HALO_A + TM, d), F32), pltpu.VMEM((HALO_B + TM, d), F32),
                        pltpu.VMEM((shl, d), F32), pltpu.VMEM((TM, d), F32),
                        pltpu.VMEM((RB, d), F32), pltpu.VMEM((8, d), F32)],
        compiler_params=_params(("arbitrary",)),
    )(x, front, proj, target, w3, wa, wb, conv_a_b, ln_g, ln_b, b_a_out, final_g, norm_g)


def _mix_bwd(ds2, proj, ca, cb, ya, yb, w3, wa, wb, ln_g, ln_b):
    tp, d = ds2.shape
    nt = tp // TM
    RB = RB_BWD
    nrb = TM // RB
    shl = TM + SHIFT_ROWS
    nt_dims = (((1,), (1,)), ((), ()))

    def body(ds2_ref, proj_ref, ca_ref, cb_ref, ya_ref, yb_ref, w3_ref, wa_ref, wb_ref, lng_ref, lnb_ref,
             dproj_ref, d3_ref, sm_ref, ext_d, ext_e, sh, dm_s, dpa_s, dpb_s, dua0_s, acc):
        step = pl.program_id(0)

        def split(k, rows):
            return proj_ref[rows, k * d:(k + 1) * d].astype(F32)

        def put(k, rows, val):
            dproj_ref[rows, k * d:(k + 1) * d] = val.astype(BF16)

        def accum(row, val):
            acc[row] += val.reshape(RB // 8, 8, d).sum(axis=0)

        @pl.when(step == 0)
        def _():
            ext_d[TM:TM + HALO_A, :] = jnp.zeros((HALO_A, d), F32)
            ext_e[TM:TM + HALO_B, :] = jnp.zeros((HALO_B, d), F32)
            acc[...] = jnp.zeros_like(acc)

        front = step == nt - 1

        @pl.when(front)
        def _():
            d3_ref[...] = jnp.zeros_like(d3_ref)

            def conv_only(rb, carry):
                rows = _rows(rb, RB)
                zeros = jnp.zeros((RB, d), F32)
                for k in (2, 3, 6, 7, 8):
                    put(k, rows, zeros)
                ext_d[rows, :] = zeros
                ext_e[rows, :] = zeros
                dua0_s[rows, :] = zeros
                dm_s[rows, :] = split(0, rows) * _sigmoid(split(1, rows))
                return carry
            lax.fori_loop(0, nrb, conv_only, 0)

        @pl.when(jnp.logical_not(front))
        def _():
            tile_to_conv_outputs(split, put, accum, ds2_ref, ca_ref, cb_ref, ya_ref, yb_ref, w3_ref, lng_ref, lnb_ref,
                                 d3_ref, ext_d, ext_e, dm_s, dpa_s, dpb_s, dua0_s)

        tile_conv_transposes(split, put, accum, wa_ref, wb_ref, ext_d, ext_e, sh, dm_s, dpb_s, dua0_s)

        @pl.when(front)
        def _():
            for row in range(SM_ROWS):
                sm_ref[row:row + 1, :] = jnp.sum(acc[row], axis=0, keepdims=True)

    def tile_to_conv_outputs(split, put, accum, ds2_ref, ca_ref, cb_ref, ya_ref, yb_ref, w3_ref, lng_ref, lnb_ref,
                             d3_ref, ext_d, ext_e, dm_s, dpa_s, dpb_s, dua0_s):
        d3_ref[2] = ds2_ref[...].astype(BF16)
        dm_s[...] = lax.dot_general(d3_ref[2], w3_ref[2], nt_dims, preferred_element_type=F32)

        def gates(rb, carry):
            rows = _rows(rb, RB)
            dm = dm_s[rows, :]
            sa = _sigmoid(split(7, rows))
            sb = _sigmoid(split(8, rows))
            ya_v = ya_ref[rows, :]
            yb_v = yb_ref[rows, :]
            put(7, rows, dm * ya_v * sa * (1.0 - sa))
            put(8, rows, dm * yb_v * sb * (1.0 - sb))
            dya = dm * sa
            accum(ROW_DBAO, dya)
            d3_ref[0, rows, :] = dya.astype(BF16)
            d3_ref[1, rows, :] = (dm * sb).astype(BF16)
            return carry
        lax.fori_loop(0, nrb, gates, 0)

        dpa_s[...] = lax.dot_general(d3_ref[0], w3_ref[0], nt_dims, preferred_element_type=F32)
        dpb_s[...] = lax.dot_general(d3_ref[1], w3_ref[1], nt_dims, preferred_element_type=F32)

        def branch_a(rb, carry):
            rows = _rows(rb, RB)
            ca_v = ca_ref[rows, :]
            mu = jnp.mean(ca_v, axis=-1, keepdims=True)
            xc = ca_v - mu
            rstd = lax.rsqrt(jnp.mean(xc * xc, axis=-1, keepdims=True) + EPS)
            xhat = xc * rstd
            ln = xhat * lng_ref[...] + lnb_ref[...]
            sl = _sigmoid(ln)
            ua = ln * sl
            a_z = split(2, rows)
            sz = _sigmoid(a_z)
            dpa = dpa_s[rows, :]
            put(2, rows, dpa * ua * (sz * (1.0 + a_z * (1.0 - sz))))
            dln = dpa * (a_z * sz) * (sl * (1.0 + ln * (1.0 - sl)))
            accum(ROW_DLNG, dln * xhat)
            accum(ROW_DLNB, dln)
            dxh = dln * lng_ref[...]
            dca = rstd * (dxh - jnp.mean(dxh, axis=-1, keepdims=True)
                          - xhat * jnp.mean(dxh * xhat, axis=-1, keepdims=True))
            accum(ROW_DCAB, dca)
            ext_d[rows, :] = dca
            return carry
        lax.fori_loop(0, nrb, branch_a, 0)

        def branch_b(rb, carry):
            rows = _rows(rb, RB)
            dua0_s[rows, :] = jnp.zeros((RB, d), F32)
            dm_s[rows, :] = split(0, rows) * _sigmoid(split(1, rows))
            b_z = split(6, rows)
            szb = _sigmoid(b_z)
            dpb = dpb_s[rows, :]
            b_b = split(3, rows)
            cb_v = cb_ref[rows, :]
            put(6, rows, dpb * (b_b * cb_v) * (szb * (1.0 + b_z * (1.0 - szb))))
            dub = dpb * (b_z * szb)
            put(3, rows, dub * cb_v)
            ext_e[rows, :] = dub * b_b
            return carry
        lax.fori_loop(0, nrb, branch_b, 0)

    def tile_conv_transposes(split, put, accum, wa_ref, wb_ref, ext_d, ext_e, sh, dm_s, dpb_s, dua0_s):
        for r, taps in _conv_a_taps(0, CONV_A - 1):
            if r == 0:
                src = ext_d
            else:
                sh[...] = ext_d[r:r + shl, :]
                src = sh

            def conv_t(rb, carry, src=src, taps=taps):
                rows = _rows(rb, RB)
                ua0 = dm_s[rows, :]
                dua0 = dua0_s[rows, :]
                for q, lag in taps:
                    k = CONV_A - 1 - lag
                    slab = src[pl.ds(pl.multiple_of(rb * RB + 8 * q, 8), RB), :]
                    dua0 = dua0 + slab * wa_ref[k:k + 1, :]
                    accum(ROW_DWA + k, slab * ua0)
                dua0_s[rows, :] = dua0
                return carry
            lax.fori_loop(0, nrb, conv_t, 0)
        ext_d[TM:TM + HALO_A, :] = ext_d[0:HALO_A, :]

        dpb_s[...] = ext_e[0:TM, :] * wb_ref[CONV_B - 1:CONV_B, :]
        for lag in range(CONV_B):
            k = CONV_B - 1 - lag
            if lag > 0:
                sh[0:TM, :] = ext_e[lag:lag + TM, :]
                dpb_s[...] += sh[0:TM, :] * wb_ref[k:k + 1, :]
            src = ext_e if lag == 0 else sh

            def conv_b_w(rb, carry, src=src, k=k):
                rows = _rows(rb, RB)
                accum(ROW_DWB + k, src[rows, :] * (split(4, rows) * split(5, rows)))
                return carry
            lax.fori_loop(0, nrb, conv_b_w, 0)
        ext_e[TM:TM + HALO_B, :] = ext_e[0:HALO_B, :]

        def inputs(rb, carry):
            rows = _rows(rb, RB)
            dua0 = dua0_s[rows, :]
            a_val = split(0, rows)
            sg = _sigmoid(split(1, rows))
            put(0, rows, dua0 * sg)
            put(1, rows, dua0 * a_val * sg * (1.0 - sg))
            dcbin = dpb_s[rows, :]
            put(4, rows, dcbin * split(5, rows))
            put(5, rows, dcbin * split(4, rows))
            return carry
        lax.fori_loop(0, nrb, inputs, 0)

    rev = lambda i: (_tile_block(nt - 1 - i, nt), 0)
    row_f32 = pl.BlockSpec((TM, d), rev)
    const = lambda shape: pl.BlockSpec(shape, lambda i: (0,) * len(shape))
    return pl.pallas_call(
        body, name="b1_mix", grid=(nt,),
        in_specs=[row_f32, pl.BlockSpec((TM, N_SPLIT * d), rev), row_f32, row_f32, row_f32, row_f32,
                  const((3, d, d)), const(wa.shape), const(wb.shape), const((1, d)), const((1, d))],
        out_specs=[pl.BlockSpec((TM, N_SPLIT * d), rev),
                   pl.BlockSpec((3, TM, d), lambda i: (0, _tile_block(nt - 1 - i, nt), 0)),
                   const((SM_ROWS, d))],
        out_shape=[jax.ShapeDtypeStruct((tp, N_SPLIT * d), BF16), jax.ShapeDtypeStruct((3, tp, d), BF16),
                   jax.ShapeDtypeStruct((SM_ROWS, d), F32)],
        scratch_shapes=[pltpu.VMEM((TM + HALO_A, d), F32), pltpu.VMEM((TM + HALO_B, d), F32),
                        pltpu.VMEM((shl, d), F32), pltpu.VMEM((TM, d), F32), pltpu.VMEM((TM, d), F32),
                        pltpu.VMEM((TM, d), F32), pltpu.VMEM((TM, d), F32),
                        pltpu.VMEM((SM_ROWS, 8, d), F32)],
        compiler_params=_params(("arbitrary",)),
    )(ds2, proj, ca, cb, ya, yb, w3, wa, wb, ln_g, ln_b)


def kernel(x, meta_tokens, norm_g, w_in, conv_a_w, conv_a_b, ln_a_g, ln_a_b, w_a_out, b_a_out, conv_b_w, w_b_out, w_out, final_g, loss_target, m_meta_tokens, m_norm_g, m_w_in, m_conv_a_w, m_conv_a_b, m_ln_a_g, m_ln_a_b, m_w_a_out, m_b_a_out, m_conv_b_w, m_w_b_out, m_w_out, m_final_g, v_meta_tokens, v_norm_g, v_w_in, v_conv_a_w, v_conv_a_b, v_ln_a_g, v_ln_a_b, v_w_a_out, v_b_a_out, v_conv_b_w, v_w_b_out, v_w_out, v_final_g):
    seq, d = x.shape[1], x.shape[2]
    dc = meta_tokens.shape[1]
    sw = w_in.shape[2]
    rsh = w_a_out.shape[1]
    xi, yi, ci = _mesh_pos()
    me = 2 * xi + yi
    pos = jnp.stack([ci, me]).astype(jnp.int32)

    conv_rows = HALO_A + HALO_B + 8
    convs = jnp.concatenate([
        jnp.pad(conv_a_w[0], ((0, HALO_A - CONV_A), (0, 0))),
        jnp.pad(conv_b_w[0], ((0, HALO_B - CONV_B), (0, 0))), jnp.zeros((8, dc), F32)], axis=0)[None]
    w3_own = jnp.stack([w_a_out[0], w_b_out[0], w_out[0]])
    fg2 = final_g.reshape(1, d)
    xs = x[0]

    h, front, placed = _h_prep(xs, meta_tokens, norm_g, [(w_in, BF16), (w3_own, BF16), (convs, F32)], pos)
    proj, (wg_in, wg3, convg) = _proj_fwd(h, placed, pos)
    w3 = wg3.reshape(3, N_CHIPS * rsh, d)
    convg = jnp.transpose(convg[0], (1, 0, 2)).reshape(conv_rows, N_CHIPS * dc)
    wa_full = convg[0:HALO_A]
    wb_full = convg[HALO_A:HALO_A + HALO_B]
    ca, cb, ya, yb, abm_t, ds2, h_t, loss8, dfg8 = _mix_fwd(
        xs, front, proj, loss_target[0], w3, wa_full, wb_full, conv_a_b, ln_a_g, ln_a_b, b_a_out, fg2, norm_g)
    dproj, d3, sm = _mix_bwd(ds2, proj, ca, cb, ya, yb, w3, wa_full, wb_full, ln_a_g, ln_a_b)
    cw_sq = _col_block(d, 512)
    per_sq = d // cw_sq
    p32_sq, pbf_sq, _ = _dw_reduced(
        abm_t, d3, cw_sq, 3 * per_sq, lambda t: (t // per_sq, t % per_sq), N_CHIPS,
        (3, N_CHIPS, rsh // 2, d), (None, N_CHIPS, rsh // 2, cw_sq),
        lambda u: (u // per_sq, 0, 0, u % per_sq), [], "dw_square")
    cw_in = _col_block(sw, 768)
    ncol = sw // cw_in
    p32_in, pbf_in, (l_sq,) = _dw_reduced(
        h_t[None], dproj[None], cw_in, N_CHIPS * ncol, lambda t: (0, t), 1,
        (1, N_CHIPS, d // 2, sw), (None, None, d // 2, cw_in), lambda u: (0, u // ncol, 0, u % ncol),
        [pbf_sq], "dw_in")
    grad_x, dfront, dng8, l_in = _dh_bwd(dproj, wg_in, xs, front, ds2, norm_g, pbf_in)
    half_in = _sum_chips([(p32_in, l_in)], sw, pos, "rs_sum_in")
    half_sq = _sum_chips([(p32_sq, l_sq)], d, pos, "rs_sum_sq")
    tail_row = lax.broadcasted_iota(jnp.int32, (8, d), 0)
    tail = jnp.where(tail_row == 0, dng8, jnp.where(tail_row == 1, dfg8,
                     jnp.where(tail_row == 2, loss8[0, 0], 0.0)))
    block = jnp.concatenate([sm, dfront[TM - N_META:TM], tail], axis=0)
    (other_in, other_sq), red = _sibling_swap([half_in, half_sq], block)
    col = lax.dynamic_slice(red, (0, me * dc), (AR_ROWS, dc))
    g_small = {
        "meta_tokens": col[ROW_DMETA:ROW_DMETA + N_META],
        "norm_g": red[ROW_DNG:ROW_DNG + 1],
        "conv_a_w": col[ROW_DWA:ROW_DWA + CONV_A][None],
        "conv_a_b": red[ROW_DCAB:ROW_DCAB + 1],
        "ln_a_g": red[ROW_DLNG:ROW_DLNG + 1],
        "ln_a_b": red[ROW_DLNB:ROW_DLNB + 1],
        "b_a_out": red[ROW_DBAO:ROW_DBAO + 1],
        "conv_b_w": col[ROW_DWB:ROW_DWB + CONV_B][None],
        "final_g": red[ROW_DFG],
    }

    upd_in = _adam_halves([w_in], [m_w_in], [v_w_in], half_in, other_in, pos, "adam_in")
    upd_sq = _adam_halves([w_a_out, w_b_out, w_out], [m_w_a_out, m_w_b_out, m_w_out],
                          [v_w_a_out, v_w_b_out, v_w_out], half_sq, other_sq, pos, "adam_sq")
    small_w = {"meta_tokens": (meta_tokens, m_meta_tokens, v_meta_tokens), "norm_g": (norm_g, m_norm_g, v_norm_g),
               "conv_a_w": (conv_a_w, m_conv_a_w, v_conv_a_w), "conv_a_b": (conv_a_b, m_conv_a_b, v_conv_a_b),
               "ln_a_g": (ln_a_g, m_ln_a_g, v_ln_a_g), "ln_a_b": (ln_a_b, m_ln_a_b, v_ln_a_b),
               "b_a_out": (b_a_out, m_b_a_out, v_b_a_out), "conv_b_w": (conv_b_w, m_conv_b_w, v_conv_b_w),
               "final_g": (final_g, m_final_g, v_final_g)}
    names_small = list(small_w)
    as2d = lambda t: t.reshape(-1, t.shape[-1])
    upd_small = _adam_small([(as2d(small_w[k][0]), as2d(g_small[k]), as2d(small_w[k][1]), as2d(small_w[k][2]))
                             for k in names_small])

    grads, deltas, new_m, new_v = dict(g_small), {}, {}, {}
    for k, upd in zip(names_small, upd_small):
        deltas[k], new_m[k], new_v[k] = [t.reshape(small_w[k][0].shape) for t in upd]
    grads["w_in"], deltas["w_in"], new_m["w_in"], new_v["w_in"] = upd_in
    for idx, k in enumerate(["w_a_out", "w_b_out", "w_out"]):
        grads[k], deltas[k], new_m[k], new_v[k] = upd_sq[4 * idx:4 * idx + 4]

    loss = red[ROW_LOSS, 0]
    order = ["meta_tokens", "norm_g", "w_in", "conv_a_w", "conv_a_b", "ln_a_g", "ln_a_b", "w_a_out", "b_a_out",
             "conv_b_w", "w_b_out", "w_out", "final_g"]
    return (loss, grad_x[None], *[grads[k] for k in order], *[deltas[k] for k in order],
            *[new_m[k] for k in order], *[new_v[k] for k in order])
```

```python
import jax
import jax.numpy as jnp
from jax import lax
from jax.experimental import pallas as pl
from jax.experimental.pallas import tpu as pltpu

F32 = jnp.float32
BF16 = jnp.bfloat16
MESH = pl.DeviceIdType.MESH

EPS = 1e-6
N_META = 16
N_SPLIT = 9
CONV_A = 31
CONV_B = 3
HALO_A = 32
HALO_B = 8
SHIFT_ROWS = 24
TM = 256
RB_FWD = 256
RB_BWD = 64
STRIP_COLS = 256
N_ROW_TILES_BIG = 8
ROW_BLOCK = 256
N_CHIPS = 4
VMEM_LIMIT = 56 * 1024 * 1024
VMEM_LIMIT_BIG = 62 * 1024 * 1024

ADAM_LR = 0.001
ADAM_B1 = 0.9
ADAM_B2 = 0.999
ADAM_EPS = 1e-08
ADAM_WD = 0.01
ADAM_STEP = 10

ROW_DWA = 0
ROW_DWB = 32
ROW_DCAB = 40
ROW_DLNG = 41
ROW_DLNB = 42
ROW_DBAO = 43
SM_ROWS = 48
ROW_DMETA = 48
ROW_DNG = 64
ROW_DFG = 65
ROW_LOSS = 66
AR_ROWS = 72


def _sigmoid(v):
    return 0.5 * jnp.tanh(0.5 * v) + 0.5


def _params(sem, **kw):
    return pltpu.CompilerParams(dimension_semantics=sem, vmem_limit_bytes=VMEM_LIMIT, **kw)


def _rows(rb, n):
    return pl.ds(pl.multiple_of(rb * n, n), n)


def _mesh_pos():
    x, y, c = lax.axis_index("x"), lax.axis_index("y"), lax.axis_index("c")
    return x, y, c


def _half(ref, j, c):
    h = ref.shape[2] // 2
    return ref.at[:, j, pl.ds(c * h, h), :]


class _Exchange:
    def __init__(self, sends, recvs):
        self.sends, self.recvs = sends, recvs

    def start(self):
        for cp in self.sends:
            cp.start()

    def finish(self):
        for cp in self.recvs:
            cp.wait_recv()
        for cp in self.sends:
            cp.wait_send()


def _chip_exchange(part_ref, land_ref, send_sems, recv_sems):
    x, y, c = _mesh_pos()
    me = 2 * x + y
    sends, recvs = [], []
    for k, (px, py) in enumerate([(1 - x, y), (x, 1 - y), (1 - x, 1 - y)]):
        sems = dict(send_sem=send_sems.at[k], recv_sem=recv_sems.at[k], device_id=(px, py, c), device_id_type=MESH)
        sends.append(pltpu.make_async_remote_copy(
            src_ref=part_ref.at[:, 2 * px + py], dst_ref=land_ref.at[:, me], **sems))
        landed = land_ref.at[:, 2 * px + py]
        recvs.append(pltpu.make_async_remote_copy(src_ref=landed, dst_ref=landed, **sems))
    return _Exchange(sends, recvs)

def _sibling_swap(halves, small):
    n = len(halves)

    def body(*refs):
        ins, small_ref, outs, red_ref = refs[:n], refs[n], refs[n + 1:2 * n + 1], refs[2 * n + 1]
        send_sems, recv_sems = refs[2 * n + 2:2 * n + 4]
        reduce = _SmallAllReduce(small_ref, red_ref, *refs[2 * n + 4:])
        x, y, c = _mesh_pos()
        copies = [pltpu.make_async_remote_copy(
            src_ref=ins[a], dst_ref=outs[a], send_sem=send_sems.at[a], recv_sem=recv_sems.at[a],
            device_id=(x, y, 1 - c), device_id_type=MESH) for a in range(n)]
        reduce.start()
        for cp in copies:
            cp.start()
        reduce.between_chips()
        reduce.finish()
        for cp in copies:
            cp.wait()

    any_spec = pl.BlockSpec(memory_space=pl.ANY)
    vm = pl.BlockSpec(memory_space=pltpu.VMEM)
    outs = pl.pallas_call(
        body, name="rs_swap",
        in_specs=[any_spec] * n + [vm], out_specs=[any_spec] * n + [vm],
        out_shape=[jax.ShapeDtypeStruct(h.shape, h.dtype) for h in halves]
        + [jax.ShapeDtypeStruct(small.shape, F32)],
        scratch_shapes=[pltpu.SemaphoreType.DMA((n,)), pltpu.SemaphoreType.DMA((n,))]
        + _SmallAllReduce.scratch(*small.shape),
    )(*halves, small)
    return outs[:n], outs[n]


class _SmallAllReduce:
    def __init__(self, x_ref, out_ref, sib_ref, part_ref, peers_ref, send_sems, recv_sems):
        self.x_ref, self.out_ref, self.sib_ref, self.part_ref, self.peers_ref = x_ref, out_ref, sib_ref, part_ref, peers_ref
        x, y, c = _mesh_pos()
        self.me = 2 * x + y
        self.swap = pltpu.make_async_remote_copy(
            src_ref=x_ref, dst_ref=sib_ref, send_sem=send_sems.at[0], recv_sem=recv_sems.at[0],
            device_id=(x, y, 1 - c), device_id_type=MESH)
        self.sends, self.recvs = [], []
        for k, (px, py) in enumerate([(1 - x, y), (x, 1 - y), (1 - x, 1 - y)]):
            sems = dict(send_sem=send_sems.at[1 + k], recv_sem=recv_sems.at[1 + k],
                        device_id=(px, py, c), device_id_type=MESH)
            self.sends.append(pltpu.make_async_remote_copy(src_ref=part_ref, dst_ref=peers_ref.at[self.me], **sems))
            landed = peers_ref.at[2 * px + py]
            self.recvs.append(pltpu.make_async_remote_copy(src_ref=landed, dst_ref=landed, **sems))

    @staticmethod
    def scratch(rows, d):
        return [pltpu.VMEM((rows, d), F32), pltpu.VMEM((rows, d), F32), pltpu.VMEM((N_CHIPS, rows, d), F32),
                pltpu.SemaphoreType.DMA((4,)), pltpu.SemaphoreType.DMA((4,))]

    def start(self):
        self.swap.start()

    def between_chips(self):
        self.swap.wait()
        self.part_ref[...] = self.x_ref[...] + self.sib_ref[...]
        self.peers_ref[self.me] = self.part_ref[...]
        for cp in self.sends:
            cp.start()

    def finish(self):
        for cp in self.recvs:
            cp.wait_recv()
        for cp in self.sends:
            cp.wait_send()
        p = self.peers_ref
        self.out_ref[...] = ((p[0] + p[1]) + p[2]) + p[3]


def _sum_chips(parts, cw, pos, name):
    s, _, h, _ = parts[0][0].shape
    hb = min(h, ROW_BLOCK)
    widths = [own.shape[3] // cw for own, _ in parts]
    starts = [sum(widths[:a]) for a in range(len(parts))]

    def body(pos_ref, *refs):
        out_ref = refs[-1]
        n = pl.program_id(2)
        total = None
        for a in range(len(parts)):
            own, l1, l2, l3 = refs[4 * a:4 * a + 4]
            val = ((own[...] + l1[...].astype(F32)) + l2[...].astype(F32)) + l3[...].astype(F32)
            total = val if total is None else jnp.where(n >= starts[a], val, total)
        out_ref[...] = total

    def slot(a, k):
        col = lambda n: jnp.clip(n - starts[a], 0, widths[a] - 1)
        return pl.BlockSpec((None, None, hb, cw),
                            lambda si, b, n, pos_ref: (si, (pos_ref[1] + k) % N_CHIPS, b, col(n)))

    operands, specs = [], []
    for a, (own, landed) in enumerate(parts):
        operands += [own, landed, landed, landed]
        specs += [slot(a, 0), slot(a, 1), slot(a, 2), slot(a, 3)]
    return pl.pallas_call(
        body, name=name,
        grid_spec=pltpu.PrefetchScalarGridSpec(
            num_scalar_prefetch=1, grid=(s, h // hb, sum(widths)), in_specs=specs,
            out_specs=pl.BlockSpec((None, hb, cw), lambda si, b, n, pos_ref: (si, b, n))),
        out_shape=jax.ShapeDtypeStruct((s, h, sum(widths) * cw), F32),
        compiler_params=_params(("arbitrary",) * 3),
    )(pos, *operands)


def _adamw(w, g, m, v):
    m = ADAM_B1 * m + (1.0 - ADAM_B1) * g
    v = ADAM_B2 * v + (1.0 - ADAM_B2) * (g * g)
    m_hat = m / (1.0 - ADAM_B1 ** ADAM_STEP)
    v_hat = v / (1.0 - ADAM_B2 ** ADAM_STEP)
    delta = -ADAM_LR * (m_hat / (jnp.sqrt(v_hat) + ADAM_EPS) + ADAM_WD * w)
    return delta, m, v


def _adam_halves(ws, ms, vs, g_own, g_recv, pos, name):
    n = len(ws)
    _, r, c = ws[0].shape
    h = r // 2
    rb = min(h, ROW_BLOCK)
    nb = h // rb

    def body(pos_ref, *refs):
        w_refs, m_refs, v_refs = refs[:n], refs[n:2 * n], refs[2 * n:3 * n]
        go_ref, gr_ref = refs[3 * n:3 * n + 2]
        outs = refs[3 * n + 2:]
        mine = pl.program_id(0) == pos_ref[0]
        for a in range(n):
            g = jnp.where(mine, go_ref[a], gr_ref[a])
            delta, m, v = _adamw(w_refs[a][...], g, m_refs[a][...], v_refs[a][...])
            outs[4 * a][...], outs[4 * a + 1][...], outs[4 * a + 2][...], outs[4 * a + 3][...] = g, delta, m, v

    spec_w = pl.BlockSpec((None, rb, c), lambda hf, b, pos_ref: (0, hf * nb + b, 0))
    spec_g = pl.BlockSpec((n, rb, c), lambda hf, b, pos_ref: (0, b, 0))
    return pl.pallas_call(
        body, name=name,
        grid_spec=pltpu.PrefetchScalarGridSpec(
            num_scalar_prefetch=1, grid=(2, nb), in_specs=[spec_w] * (3 * n) + [spec_g] * 2,
            out_specs=[spec_w] * (4 * n)),
        out_shape=[jax.ShapeDtypeStruct((1, r, c), F32)] * (4 * n),
        compiler_params=_params(("arbitrary",) * 2),
    )(pos, *ws, *ms, *vs, g_own, g_recv)


def _adam_small(items):
    n = len(items)

    def body(*refs):
        ins, outs = refs[:4 * n], refs[4 * n:]
        for a in range(n):
            w_ref, g_ref, m_ref, v_ref = ins[4 * a:4 * a + 4]
            d, m, v = _adamw(w_ref[...], g_ref[...], m_ref[...], v_ref[...])
            outs[3 * a][...] = d
            outs[3 * a + 1][...] = m
            outs[3 * a + 2][...] = v

    vm = pl.BlockSpec(memory_space=pltpu.VMEM)
    flat = [t for it in items for t in it]
    outs = pl.pallas_call(
        body, name="adam_small", in_specs=[vm] * (4 * n), out_specs=[vm] * (3 * n),
        out_shape=[jax.ShapeDtypeStruct(it[0].shape, F32) for it in items for _ in range(3)],
    )(*flat)
    return [tuple(outs[3 * a:3 * a + 3]) for a in range(n)]


def _big_row_spec(seq, tmb, d, tile_of):
    return pl.BlockSpec((pl.Element(tmb), pl.Element(d)),
                        lambda *args: (pl.multiple_of(jnp.minimum(tile_of(*args) * tmb, seq - tmb), 8), 0))


def _big_row_tile(x_ref, front_ref, i):
    rows = x_ref[...]
    last = jnp.concatenate([rows[TM:], front_ref[...]], axis=0)
    return jnp.where(i == N_ROW_TILES_BIG - 1, last, rows)


def _h_prep(x, meta, norm_g, shards, pos):
    seq, d = x.shape
    tp = seq + TM
    dc = meta.shape[1]
    tmb = tp // N_ROW_TILES_BIG
    assert tmb >= TM and tp == tmb * N_ROW_TILES_BIG
    last = N_ROW_TILES_BIG - 1
    ns = len(shards)

    def body(pos_ref, x_ref, meta_ref, g_ref, *refs):
        shard_refs, (h_ref, front_ref), placed_refs = refs[:ns], refs[ns:ns + 2], refs[ns + 2:2 * ns + 2]
        metas, msend, mrecv = refs[2 * ns + 2:]
        for a in range(ns):
            placed_refs[a][...] = shard_refs[a][...].astype(placed_refs[a].dtype)
        x, y, c = _mesh_pos()
        me = 2 * x + y
        chips = [(1 - x, y), (x, 1 - y), (1 - x, 1 - y)]
        i = pl.program_id(0)

        def meta_copy(k, chip):
            return pltpu.make_async_remote_copy(
                src_ref=metas.at[chip], dst_ref=metas.at[chip], send_sem=msend.at[k], recv_sem=mrecv.at[k],
                device_id=(*chips[k], c), device_id_type=MESH)

        @pl.when(i == 0)
        def _():
            metas[me] = meta_ref[...]
            for k in range(3):
                meta_copy(k, me).start()
            front_ref[...] = jnp.zeros_like(front_ref)

        @pl.when(i == last)
        def _():
            for k, (px, py) in enumerate(chips):
                meta_copy(k, 2 * px + py).wait_recv()
            for q in range(N_CHIPS):
                front_ref[TM - N_META:TM, q * dc:(q + 1) * dc] = metas[q]

        s = _big_row_tile(x_ref, front_ref, i)
        r = lax.rsqrt(jnp.mean(s * s, axis=-1, keepdims=True) + EPS)
        h_ref[...] = (s * r * g_ref[...]).astype(BF16)

        @pl.when(i == last)
        def _():
            for k in range(3):
                meta_copy(k, me).wait_send()

    shard_in, shard_out, shard_shapes = [], [], []
    for arr, dtype in shards:
        s, r, c = arr.shape
        sliced = r % (N_ROW_TILES_BIG * 16) == 0
        rp = r // N_ROW_TILES_BIG if sliced else r
        step = (lambda i: i) if sliced else (lambda i: 0)
        shard_in.append(pl.BlockSpec((s, rp, c), lambda i, pos_ref, step=step: (0, step(i), 0)))
        shard_out.append(pl.BlockSpec((s, None, rp, c), lambda i, pos_ref, step=step: (0, pos_ref[1], step(i), 0)))
        shard_shapes.append(jax.ShapeDtypeStruct((s, N_CHIPS, r, c), dtype))
    outs = pl.pallas_call(
        body, name="f0_norm",
        grid_spec=pltpu.PrefetchScalarGridSpec(
            num_scalar_prefetch=1, grid=(N_ROW_TILES_BIG,),
            in_specs=[_big_row_spec(seq, tmb, d, lambda i, pos_ref: i),
                      pl.BlockSpec(meta.shape, lambda i, pos_ref: (0, 0)),
                      pl.BlockSpec((1, d), lambda i, pos_ref: (0, 0))] + shard_in,
            out_specs=[pl.BlockSpec((tmb, d), lambda i, pos_ref: (i, 0)),
                       pl.BlockSpec((TM, d), lambda i, pos_ref: (0, 0))] + shard_out,
            scratch_shapes=[pltpu.VMEM((N_CHIPS,) + meta.shape, F32),
                            pltpu.SemaphoreType.DMA((3,)), pltpu.SemaphoreType.DMA((3,))]),
        out_shape=[jax.ShapeDtypeStruct((tp, d), BF16), jax.ShapeDtypeStruct((TM, d), F32)] + shard_shapes,
        compiler_params=_params(("arbitrary",)),
    )(pos, x, meta, norm_g, *[arr for arr, _ in shards])
    return outs[0], outs[1], outs[2:]


N_UNITS = 3
N_STEPS_PROJ = N_CHIPS * N_UNITS


def _proj_plan(u):
    v = u - N_UNITS
    if v < 2 * N_UNITS:
        return v % 2, v // 2
    return 2, v - 2 * N_UNITS


def _proj_unit(t, me):
    v = t - N_UNITS
    near = v < 2 * N_UNITS
    rel = jnp.where(near, v % 2, 2)
    unit = jnp.where(near, v // 2, v - 2 * N_UNITS)
    flip = jnp.where(rel == 0, 2, jnp.where(rel == 1, 1, 3))
    own = t < N_UNITS
    return jnp.where(own, me, lax.bitwise_xor(me, flip)), jnp.where(own, t, unit)


def _proj_fwd(h, bufs, pos):
    tp, d = h.shape
    _, nsh, _, sw = bufs[0].shape
    cu = sw // N_UNITS
    assert cu % 128 == 0
    n = len(bufs)
    w_sems = 6 * N_UNITS
    last = N_STEPS_PROJ - 1
    late = N_STEPS_PROJ - N_UNITS

    def body(pos_ref, h_ref, *refs):
        proj_ref = refs[n]
        gbufs = refs[n + 1:2 * n + 1]
        wbuf, wsems, send_sems, recv_sems = refs[2 * n + 1:]
        x, y, c = _mesh_pos()
        me = 2 * x + y
        sibling = (x, y, 1 - c)
        chips = [(1 - x, y), (x, 1 - y), (1 - x, 1 - y)]
        chip_ids = [2 * px + py for px, py in chips]
        relayed_chip = jnp.where(c == 0, chip_ids[0], chip_ids[1])
        relay_to = (jnp.where(c == 0, x, 1 - x), jnp.where(c == 0, 1 - y, y), c)
        t = pl.program_id(0)

        def remote(idx, piece, to):
            return pltpu.make_async_remote_copy(
                src_ref=piece, dst_ref=piece, send_sem=send_sems.at[idx], recv_sem=recv_sems.at[idx],
                device_id=to, device_id_type=MESH)

        hr = d // 2

        def chunk_of(chip, half, k):
            return gbufs[0].at[0, chip, pl.ds(half * hr, hr), pl.ds(k * cu, cu)]

        def own_chunk(r, k):
            return remote(6 * k + r, chunk_of(me, c, k), (*chips[r], c))

        def landed_chunk(r, k):
            return remote(6 * k + r, chunk_of(chip_ids[r], c, k), (*chips[r], c))

        def relay_chunk(k):
            return remote(6 * k + 2, chunk_of(relayed_chip, c, k), relay_to)

        def sibling_chunk(r, k, half):
            return remote(6 * k + 3 + r, chunk_of(chip_ids[r], half, k), sibling)

        def fetch(u):
            chip, unit = _proj_unit(jnp.int32(u), me)
            return pltpu.make_async_copy(gbufs[0].at[0, chip, :, pl.ds(pl.multiple_of(unit * cu, 128), cu)],
                                         wbuf.at[u % 2], wsems.at[u % 2])

        def make_available(u):
            r, k = _proj_plan(u)
            landed_chunk(r, k).wait_recv()
            if r < 2:
                pl.when(c == r)(lambda: relay_chunk(k).start())
            sibling_chunk(r, k, c).start()
            sibling_chunk(r, k, 1 - c).wait_recv()

        def own_piece(a, r):
            return remote(w_sems + 6 * (a - 1) + r, _half(gbufs[a], me, c), (*chips[r], c))

        def relay(a):
            return remote(w_sems + 6 * (a - 1) + 2, _half(gbufs[a], relayed_chip, c), relay_to)

        def to_sibling(a, r, core):
            return remote(w_sems + 6 * (a - 1) + 3 + r, _half(gbufs[a], chip_ids[r], core), sibling)

        def landed(a, r):
            return remote(w_sems + 6 * (a - 1) + r, _half(gbufs[a], chip_ids[r], c), (*chips[r], c))

        for u in range(N_STEPS_PROJ):
            @pl.when(t == u)
            def _(u=u):
                if u == 0:
                    for k in range(N_UNITS):
                        for r in range(2):
                            own_chunk(r, k).start()
                    for a in range(1, n):
                        for r in range(2):
                            own_piece(a, r).start()
                    fetch(0).start()
                if u < last:
                    if u + 1 >= N_UNITS:
                        make_available(u + 1)
                    fetch(u + 1).start()
                if u == late:
                    for a in range(1, n):
                        landed(a, 0).wait_recv()
                        landed(a, 1).wait_recv()
                        relay(a).start()
                        for r in range(2):
                            to_sibling(a, r, c).start()
                        for r in range(2):
                            to_sibling(a, r, 1 - c).wait_recv()
                fetch(u).wait()

        proj_ref[...] = jnp.dot(h_ref[...], wbuf[t % 2], preferred_element_type=F32).astype(BF16)

        @pl.when(t == last)
        def _():
            for a in range(1, n):
                landed(a, 2).wait_recv()
                to_sibling(a, 2, c).start()
                to_sibling(a, 2, 1 - c).wait_recv()
            for k in range(N_UNITS):
                for r in range(2):
                    own_chunk(r, k).wait_send()
                relay_chunk(k).wait_send()
                for r in range(3):
                    sibling_chunk(r, k, c).wait_send()
            for a in range(1, n):
                for r in range(2):
                    own_piece(a, r).wait_send()
                relay(a).wait_send()
                for r in range(3):
                    to_sibling(a, r, c).wait_send()

    def out_index(t, pos_ref):
        chip, unit = _proj_unit(t, pos_ref[1])
        return 0, chip * N_UNITS + unit

    any_spec = pl.BlockSpec(memory_space=pl.ANY)
    outs = pl.pallas_call(
        body, name="f1_proj",
        grid_spec=pltpu.PrefetchScalarGridSpec(
            num_scalar_prefetch=1, grid=(N_STEPS_PROJ,),
            in_specs=[pl.BlockSpec((tp, d), lambda t, pos_ref: (0, 0))] + [any_spec] * n,
            out_specs=[pl.BlockSpec((tp, cu), out_index)] + [any_spec] * n,
            scratch_shapes=[pltpu.VMEM((2, d, cu), BF16), pltpu.SemaphoreType.DMA((2,)),
                            pltpu.SemaphoreType.DMA((w_sems + 6 * (n - 1),)),
                            pltpu.SemaphoreType.DMA((w_sems + 6 * (n - 1),))]),
        out_shape=[jax.ShapeDtypeStruct((tp, nsh * sw), BF16)]
        + [jax.ShapeDtypeStruct(b.shape, b.dtype) for b in bufs],
        input_output_aliases={2 + a: 1 + a for a in range(n)},
        compiler_params=_params(("arbitrary",)),
    )(pos, h, *bufs)
    return outs[0], outs[1:]


def _dh_bwd(dproj, wg_in, x, front, ds2, norm_g, part):
    seq, d = x.shape
    tp = seq + TM
    _, nsh, _, sw = wg_in.shape
    tmb = tp // N_ROW_TILES_BIG
    tail = tmb - TM
    last = N_ROW_TILES_BIG - 1

    def body(dp_ref, w_hbm, x_ref, front_ref, ds2_ref, g_ref, part_ref, gx_hbm, dfront_ref, dng_ref, land_ref,
             wbuf, gacc, dsbuf, wsem, osems, send_sems, recv_sems):
        exchange = _chip_exchange(part_ref, land_ref, send_sems, recv_sems)
        i = pl.program_id(0)

        def x_rows_out(step):
            return pltpu.make_async_copy(dsbuf.at[step % 2], gx_hbm.at[pl.ds(step * tmb, tmb), :], osems.at[step % 2])

        last_out = pltpu.make_async_copy(dsbuf.at[last % 2, pl.ds(0, tail), :],
                                         gx_hbm.at[pl.ds(last * tmb, tail), :], osems.at[last % 2])

        @pl.when(i == 0)
        def _():
            exchange.start()
            gacc[...] = jnp.zeros_like(gacc)
            whole = pltpu.make_async_copy(w_hbm.at[0], wbuf, wsem)
            whole.start()
            whole.wait()

        dh = None
        for j in range(nsh):
            part = lax.dot_general(dp_ref[:, j * sw:(j + 1) * sw], wbuf[j], (((1,), (1,)), ((), ())),
                                   preferred_element_type=F32)
            dh = part if dh is None else dh + part
        s = _big_row_tile(x_ref, front_ref, i)
        r = lax.rsqrt(jnp.mean(s * s, axis=-1, keepdims=True) + EPS)
        gacc[...] += (dh * s * r).reshape(tmb // 8, 8, d).sum(axis=0)
        t = dh * g_ref[...]

        @pl.when(i >= 2)
        def _():
            x_rows_out(i - 2).wait()

        dsbuf[i % 2] = ds2_ref[...] + r * t - s * (r * r * r) * jnp.mean(t * s, axis=-1, keepdims=True)

        @pl.when(i < last)
        def _():
            x_rows_out(i).start()

        @pl.when(i == last)
        def _():
            last_out.start()
            dfront_ref[...] = dsbuf[last % 2, tail:, :]
            dng_ref[...] = jnp.broadcast_to(jnp.sum(gacc[...], axis=0, keepdims=True), (8, d))
            exchange.finish()
            x_rows_out(last - 1).wait()
            last_out.wait()

    any_spec = pl.BlockSpec(memory_space=pl.ANY)
    return pl.pallas_call(
        body, name="b2_dh", grid=(N_ROW_TILES_BIG,),
        in_specs=[pl.BlockSpec((tmb, nsh * sw), lambda i: (i, 0)), any_spec,
                  _big_row_spec(seq, tmb, d, lambda i: i),
                  pl.BlockSpec((TM, d), lambda i: (0, 0)),
                  pl.BlockSpec((tmb, d), lambda i: (i, 0)),
                  pl.BlockSpec((1, d), lambda i: (0, 0)), any_spec],
        out_specs=[any_spec, pl.BlockSpec((TM, d), lambda i: (0, 0)),
                   pl.BlockSpec((8, d), lambda i: (0, 0)), any_spec],
        out_shape=[jax.ShapeDtypeStruct((seq, d), F32), jax.ShapeDtypeStruct((TM, d), F32),
                   jax.ShapeDtypeStruct((8, d), F32), jax.ShapeDtypeStruct(part.shape, part.dtype)],
        scratch_shapes=[pltpu.VMEM((nsh, d, sw), BF16), pltpu.VMEM((8, d), F32), pltpu.VMEM((2, tmb, d), F32),
                        pltpu.SemaphoreType.DMA, pltpu.SemaphoreType.DMA((2,)),
                        pltpu.SemaphoreType.DMA((3,)), pltpu.SemaphoreType.DMA((3,))],
        compiler_params=pltpu.CompilerParams(dimension_semantics=("arbitrary",),
                                             vmem_limit_bytes=VMEM_LIMIT_BIG),
    )(dproj, wg_in, x, front, ds2, norm_g, part)


def _col_block(width, cap):
    return max(b for b in range(128, cap + 1, 128) if width % b == 0)


def _dw_reduced(lhs_t, rhs, cw, nblk, operands, groups, out_dims, out_block, out_index, carried, name):
    na, d, tp = lhs_t.shape
    rg = d // groups
    hh = rg // 2
    nc = len(carried)

    def body(*refs):
        l_ref, r_ref = refs[:2]
        part_refs = refs[2:2 + nc]
        p32_ref, pbf_ref = refs[2 + nc:4 + nc]
        land_refs = refs[4 + nc:4 + 2 * nc]
        res, rbuf, send_sems, recv_sems = refs[4 + 2 * nc:8 + 2 * nc]
        xsems = refs[8 + 2 * nc:]
        exchanges = [_chip_exchange(part_refs[e], land_refs[e], xsems[2 * e], xsems[2 * e + 1]) for e in range(nc)]
        exchange = _Exchange([s for ex in exchanges for s in ex.sends], [r for ex in exchanges for r in ex.recvs])
        x, y, c = _mesh_pos()
        t = pl.program_id(0)
        u = jnp.maximum(t - 1, 0)

        def to_sibling(blk):
            return pltpu.make_async_remote_copy(
                src_ref=res.at[blk % 2, :, pl.ds((1 - c) * hh, hh), :], dst_ref=rbuf.at[blk % 2],
                send_sem=send_sems.at[blk], recv_sem=recv_sems.at[blk],
                device_id=(x, y, 1 - c), device_id_type=MESH)

        @pl.when(t == 0)
        def _():
            exchange.start()

        @pl.when(t < nblk)
        def _():
            res[t % 2] = jnp.dot(l_ref[...], r_ref[...], preferred_element_type=F32).reshape(groups, rg, cw)

        @pl.when(t >= 1)
        def _():
            to_sibling(u).wait_recv()
            p = res[u % 2, :, pl.ds(c * hh, hh), :] + rbuf[u % 2]
            p32_ref[...] = p.reshape(p32_ref.shape)
            pbf_ref[...] = p.reshape(pbf_ref.shape).astype(BF16)

        @pl.when(t < nblk)
        def _():
            to_sibling(t).start()

        @pl.when(t >= 1)
        def _():
            to_sibling(u).wait_send()

        @pl.when(t == nblk)
        def _():
            exchange.finish()

    any_spec = pl.BlockSpec(memory_space=pl.ANY)
    last = nblk - 1
    out_spec = pl.BlockSpec(out_block, lambda t: out_index(jnp.maximum(t - 1, 0)))
    outs = pl.pallas_call(
        body, name=name, grid=(nblk + 1,),
        in_specs=[pl.BlockSpec((None, d, tp), lambda t: (operands(jnp.minimum(t, last))[0], 0, 0)),
                  pl.BlockSpec((None, tp, cw), lambda t: (operands(jnp.minimum(t, last))[0], 0,
                                                          operands(jnp.minimum(t, last))[1]))]
        + [any_spec] * nc,
        out_specs=[out_spec, out_spec] + [any_spec] * nc,
        out_shape=[jax.ShapeDtypeStruct(out_dims, F32), jax.ShapeDtypeStruct(out_dims, BF16)]
        + [jax.ShapeDtypeStruct(e.shape, e.dtype) for e in carried],
        scratch_shapes=[pltpu.VMEM((2, groups, rg, cw), F32), pltpu.VMEM((2, groups, hh, cw), F32),
                        pltpu.SemaphoreType.DMA((nblk,)), pltpu.SemaphoreType.DMA((nblk,))]
        + [pltpu.SemaphoreType.DMA((3,)), pltpu.SemaphoreType.DMA((3,))] * nc,
        compiler_params=_params(("arbitrary",)),
    )(lhs_t, rhs, *carried)
    return outs[0], outs[1], outs[2:]


def _conv_a_taps(first_lag, last_lag):
    out = []
    for r in range(8):
        taps = [(q, 8 * q + r) for q in range(5) if first_lag <= 8 * q + r <= last_lag]
        if taps:
            out.append((r, taps))
    return out


def _tile_block(i, nt):
    return jnp.where(i == 0, nt - 1, i - 1)


def _mix_fwd(x, front, proj, target, w3, wa, wb, conv_a_b, ln_g, ln_b, b_a_out, final_g, norm_g):
    seq, d = x.shape
    tp = seq + TM
    nt = tp // TM
    RB = RB_FWD
    nrb = TM // RB
    shl = TM + SHIFT_ROWS

    def body(x_ref, front_ref, proj_ref, tgt_ref, w3_ref, wa_ref, wb_ref, cab_ref, lng_ref, lnb_ref, bao_ref, fg_ref,
             ng_ref, ca_ref, cb_ref, ya_ref, yb_ref, abmt_ref, ds2_ref, ht_ref, loss_ref, dfg_ref,
             abm_ref, ext_a, ext_b, sh, s2_s, lacc, gacc):
        i = pl.program_id(0)

        def split(k, rows):
            return proj_ref[rows, k * d:(k + 1) * d].astype(F32)

        def s_tile():
            return jnp.where(i == 0, front_ref[...], x_ref[...])

        s_in = s_tile()
        h = s_in * lax.rsqrt(jnp.mean(s_in * s_in, axis=-1, keepdims=True) + EPS) * ng_ref[...]
        ht_ref[...] = h.astype(BF16).T

        @pl.when(i == 0)
        def _():
            ext_a[0:HALO_A, :] = jnp.zeros((HALO_A, d), F32)
            ext_b[0:HALO_B, :] = jnp.zeros((HALO_B, d), F32)
            lacc[...] = jnp.zeros_like(lacc)
            gacc[...] = jnp.zeros_like(gacc)

        def conv_in(rb, carry):
            rows = _rows(rb, RB)
            ua0 = split(0, rows) * _sigmoid(split(1, rows))
            ext_a[pl.ds(pl.multiple_of(HALO_A + rb * RB, 8), RB), :] = ua0
            ext_b[pl.ds(pl.multiple_of(HALO_B + rb * RB, 8), RB), :] = split(4, rows) * split(5, rows)
            ca_ref[rows, :] = jnp.broadcast_to(cab_ref[...], (RB, d))
            return carry
        lax.fori_loop(0, nrb, conv_in, 0)

        @pl.when(i == 0)
        def _():
            abmt_ref[...] = jnp.zeros_like(abmt_ref)
            ds2_ref[...] = jnp.zeros_like(ds2_ref)

        @pl.when(i > 0)
        def _():
            tile_after_conv_inputs(split, s_tile, tgt_ref, w3_ref, wa_ref, wb_ref, lng_ref, lnb_ref, bao_ref, fg_ref,
                                   ca_ref, cb_ref, ya_ref, yb_ref, abmt_ref, ds2_ref, abm_ref, ext_a, ext_b, sh,
                                   s2_s, lacc, gacc)

        ext_a[0:HALO_A, :] = ext_a[TM:TM + HALO_A, :]
        ext_b[0:HALO_B, :] = ext_b[TM:TM + HALO_B, :]

        @pl.when(i == nt - 1)
        def _():
            loss_ref[...] = jnp.broadcast_to(0.5 * jnp.sum(lacc[...]) * (1.0 / d), (8, 128))
            dfg_ref[...] = jnp.broadcast_to(jnp.sum(gacc[...], axis=0, keepdims=True), (8, d))

    def tile_after_conv_inputs(split, s_tile, tgt_ref, w3_ref, wa_ref, wb_ref, lng_ref, lnb_ref, bao_ref, fg_ref,
                               ca_ref, cb_ref, ya_ref, yb_ref, abmt_ref, ds2_ref, abm_ref, ext_a, ext_b, sh, s2_s,
                               lacc, gacc):
        for r, taps in _conv_a_taps(HALO_A - CONV_A + 1, HALO_A):
            if r == 0:
                src = ext_a
            else:
                sh[...] = ext_a[r:r + shl, :]
                src = sh

            def conv_acc(rb, carry, src=src, taps=taps):
                rows = _rows(rb, RB)
                acc = ca_ref[rows, :]
                for q, lag in taps:
                    k = lag - (HALO_A - CONV_A + 1)
                    acc = acc + src[pl.ds(pl.multiple_of(rb * RB + 8 * q, 8), RB), :] * wa_ref[k:k + 1, :]
                ca_ref[rows, :] = acc
                return carry
            lax.fori_loop(0, nrb, conv_acc, 0)

        cb_ref[...] = ext_b[HALO_B:HALO_B + TM, :] * wb_ref[2:3, :]
        for k in range(CONV_B - 1):
            off = HALO_B - CONV_B + 1 + k
            sh[0:TM, :] = ext_b[off:off + TM, :]
            cb_ref[...] += sh[0:TM, :] * wb_ref[k:k + 1, :]

        def branches(rb, carry):
            rows = _rows(rb, RB)
            ca = ca_ref[rows, :]
            mu = jnp.mean(ca, axis=-1, keepdims=True)
            xc = ca - mu
            rstd = lax.rsqrt(jnp.mean(xc * xc, axis=-1, keepdims=True) + EPS)
            ln = xc * rstd * lng_ref[...] + lnb_ref[...]
            ua = ln * _sigmoid(ln)
            a_z = split(2, rows)
            abm_ref[0, rows, :] = (ua * (a_z * _sigmoid(a_z))).astype(BF16)
            return carry
        lax.fori_loop(0, nrb, branches, 0)

        def branch_b(rb, carry):
            rows = _rows(rb, RB)
            b_z = split(6, rows)
            ub = split(3, rows) * cb_ref[rows, :]
            abm_ref[1, rows, :] = (ub * (b_z * _sigmoid(b_z))).astype(BF16)
            return carry
        lax.fori_loop(0, nrb, branch_b, 0)

        ya_ref[...] = jnp.dot(abm_ref[0], w3_ref[0], preferred_element_type=F32) + bao_ref[...]
        yb_ref[...] = jnp.dot(abm_ref[1], w3_ref[1], preferred_element_type=F32)

        def merge(rb, carry):
            rows = _rows(rb, RB)
            m = _sigmoid(split(7, rows)) * ya_ref[rows, :] + _sigmoid(split(8, rows)) * yb_ref[rows, :]
            abm_ref[2, rows, :] = m.astype(BF16)
            return carry
        lax.fori_loop(0, nrb, merge, 0)

        s2_s[...] = s_tile() + jnp.dot(abm_ref[2], w3_ref[2], preferred_element_type=F32)
        for k in range(3):
            abmt_ref[k] = abm_ref[k].T

        def head(rb, carry):
            rows = _rows(rb, RB)
            s2 = s2_s[rows, :]
            r2 = lax.rsqrt(jnp.mean(s2 * s2, axis=-1, keepdims=True) + EPS)
            diff = s2 * r2 * fg_ref[...] - tgt_ref[rows, :]
            lacc[...] += diff * diff
            dy = diff * (1.0 / d)
            gacc[...] += (dy * s2 * r2).reshape(RB // 8, 8, d).sum(axis=0)
            t = dy * fg_ref[...]
            ds2_ref[rows, :] = r2 * t - s2 * (r2 * r2 * r2) * jnp.mean(t * s2, axis=-1, keepdims=True)
            return carry
        lax.fori_loop(0, nrb, head, 0)

    row_f32 = pl.BlockSpec((TM, d), lambda i: (_tile_block(i, nt), 0))
    x_rows = pl.BlockSpec((TM, d), lambda i: (jnp.maximum(i - 1, 0), 0))
    const = lambda shape: pl.BlockSpec(shape, lambda i: (0,) * len(shape))
    return pl.pallas_call(
        body, name="f2_mix", grid=(nt,),
        in_specs=[x_rows, const((TM, d)),
                  pl.BlockSpec((TM, N_SPLIT * d), lambda i: (_tile_block(i, nt), 0)),
                  x_rows,
                  const((3, d, d)), const(wa.shape), const(wb.shape)] + [const((1, d))] * 6,
        out_specs=[row_f32, row_f32, row_f32, row_f32,
                   pl.BlockSpec((3, d, TM), lambda i: (0, 0, _tile_block(i, nt))),
                   row_f32, pl.BlockSpec((d, TM), lambda i: (0, _tile_block(i, nt))),
                   const((8, 128)), const((8, d))],
        out_shape=[jax.ShapeDtypeStruct((tp, d), F32)] * 4
        + [jax.ShapeDtypeStruct((3, d, tp), BF16), jax.ShapeDtypeStruct((tp, d), F32),
           jax.ShapeDtypeStruct((d, tp), BF16),
           jax.ShapeDtypeStruct((8, 128), F32), jax.ShapeDtypeStruct((8, d), F32)],
        scratch_shapes=[pltpu.VMEM((3, TM, d), BF16),
                        pltpu.VMEM((HALO_A + TM, d), F32), pltpu.VMEM((HALO_B + TM, d), F32),
                        pltpu.VMEM((shl, d), F32), pltpu.VMEM((TM, d), F32),
                        pltpu.VMEM((RB, d), F32), pltpu.VMEM((8, d), F32)],
        compiler_params=_params(("arbitrary",)),
    )(x, front, proj, target, w3, wa, wb, conv_a_b, ln_g, ln_b, b_a_out, final_g, norm_g)


def _mix_bwd(ds2, proj, ca, cb, ya, yb, w3, wa, wb, ln_g, ln_b):
    tp, d = ds2.shape
    nt = tp // TM
    RB = RB_BWD
    nrb = TM // RB
    strips = [slice(c0, c0 + STRIP_COLS) for c0 in range(0, d, STRIP_COLS)]
    shl = TM + SHIFT_ROWS
    nt_dims = (((1,), (1,)), ((), ()))

    def body(ds2_ref, proj_ref, ca_ref, cb_ref, ya_ref, yb_ref, w3_ref, wa_ref, wb_ref, lng_ref, lnb_ref,
             dproj_ref, d3_ref, sm_ref, ext_d, ext_e, sh, dm_s, dpa_s, dpb_s, dua0_s, acc):
        step = pl.program_id(0)

        def split(k, rows, cols=slice(0, d)):
            return proj_ref[rows, k * d + cols.start:k * d + cols.stop].astype(F32)

        def put(k, rows, val, cols=slice(0, d)):
            dproj_ref[rows, k * d + cols.start:k * d + cols.stop] = val.astype(BF16)

        def accum(row, val, cols=slice(0, d)):
            acc[row, :, cols] += val.reshape(RB // 8, 8, val.shape[-1]).sum(axis=0)

        @pl.when(step == 0)
        def _():
            ext_d[TM:TM + HALO_A, :] = jnp.zeros((HALO_A, d), F32)
            ext_e[TM:TM + HALO_B, :] = jnp.zeros((HALO_B, d), F32)
            acc[...] = jnp.zeros_like(acc)

        front = step == nt - 1

        @pl.when(front)
        def _():
            d3_ref[...] = jnp.zeros_like(d3_ref)

            def conv_only(rb, carry):
                rows = _rows(rb, RB)
                zeros = jnp.zeros((RB, d), F32)
                for k in (2, 3, 6, 7, 8):
                    put(k, rows, zeros)
                ext_d[rows, :] = zeros
                ext_e[rows, :] = zeros
                dua0_s[rows, :] = zeros
                dm_s[rows, :] = split(0, rows) * _sigmoid(split(1, rows))
                return carry
            lax.fori_loop(0, nrb, conv_only, 0)

        @pl.when(jnp.logical_not(front))
        def _():
            tile_to_conv_outputs(split, put, accum, ds2_ref, ca_ref, cb_ref, ya_ref, yb_ref, w3_ref, lng_ref, lnb_ref,
                                 d3_ref, ext_d, ext_e, dm_s, dpa_s, dpb_s, dua0_s)

        tile_conv_transposes(split, put, accum, wa_ref, wb_ref, ext_d, ext_e, sh, dm_s, dpb_s, dua0_s)

        @pl.when(front)
        def _():
            for row in range(SM_ROWS):
                sm_ref[row:row + 1, :] = jnp.sum(acc[row], axis=0, keepdims=True)

    def tile_to_conv_outputs(split, put, accum, ds2_ref, ca_ref, cb_ref, ya_ref, yb_ref, w3_ref, lng_ref, lnb_ref,
                             d3_ref, ext_d, ext_e, dm_s, dpa_s, dpb_s, dua0_s):
        d3_ref[2] = ds2_ref[...].astype(BF16)
        dm_s[...] = lax.dot_general(d3_ref[2], w3_ref[2], nt_dims, preferred_element_type=F32)

        def gates(rb, carry):
            rows = _rows(rb, RB)
            for cols in strips:
                dm = dm_s[rows, cols]
                sa = _sigmoid(split(7, rows, cols))
                sb = _sigmoid(split(8, rows, cols))
                put(7, rows, dm * ya_ref[rows, cols] * sa * (1.0 - sa), cols)
                put(8, rows, dm * yb_ref[rows, cols] * sb * (1.0 - sb), cols)
                dya = dm * sa
                accum(ROW_DBAO, dya, cols)
                d3_ref[0, rows, cols] = dya.astype(BF16)
                d3_ref[1, rows, cols] = (dm * sb).astype(BF16)
            return carry
        lax.fori_loop(0, nrb, gates, 0)

        dpa_s[...] = lax.dot_general(d3_ref[0], w3_ref[0], nt_dims, preferred_element_type=F32)
        dpb_s[...] = lax.dot_general(d3_ref[1], w3_ref[1], nt_dims, preferred_element_type=F32)

        def branch_a(rb, carry):
            rows = _rows(rb, RB)

            def row_mean(strip_fn):
                total = strip_fn(strips[0])
                for cols in strips[1:]:
                    total = total + strip_fn(cols)
                return jnp.sum(total, axis=-1, keepdims=True) * (1.0 / d)

            mu = row_mean(lambda cols: ca_ref[rows, cols])
            rstd = lax.rsqrt(row_mean(lambda cols: jnp.square(ca_ref[rows, cols] - mu)) + EPS)
            sum_dxh = jnp.zeros((RB, STRIP_COLS), F32)
            sum_dxh_xhat = jnp.zeros((RB, STRIP_COLS), F32)
            for cols in strips:
                xhat = (ca_ref[rows, cols] - mu) * rstd
                ln = xhat * lng_ref[:, cols] + lnb_ref[:, cols]
                sl = _sigmoid(ln)
                ua = ln * sl
                a_z = split(2, rows, cols)
                sz = _sigmoid(a_z)
                dpa = dpa_s[rows, cols]
                put(2, rows, dpa * ua * (sz * (1.0 + a_z * (1.0 - sz))), cols)
                dln = dpa * (a_z * sz) * (sl * (1.0 + ln * (1.0 - sl)))
                accum(ROW_DLNG, dln * xhat, cols)
                accum(ROW_DLNB, dln, cols)
                dxh = dln * lng_ref[:, cols]
                sum_dxh = sum_dxh + dxh
                sum_dxh_xhat = sum_dxh_xhat + dxh * xhat
                ext_d[rows, cols] = dxh
                dpa_s[rows, cols] = xhat
            mean_dxh = jnp.sum(sum_dxh, axis=-1, keepdims=True) * (1.0 / d)
            mean_dxh_xhat = jnp.sum(sum_dxh_xhat, axis=-1, keepdims=True) * (1.0 / d)
            for cols in strips:
                dca = rstd * (ext_d[rows, cols] - mean_dxh - dpa_s[rows, cols] * mean_dxh_xhat)
                accum(ROW_DCAB, dca, cols)
                ext_d[rows, cols] = dca
            return carry
        lax.fori_loop(0, nrb, branch_a, 0)

        def branch_b(rb, carry):
            rows = _rows(rb, RB)
            dua0_s[rows, :] = jnp.zeros((RB, d), F32)
            for cols in strips:
                dm_s[rows, cols] = split(0, rows, cols) * _sigmoid(split(1, rows, cols))
                b_z = split(6, rows, cols)
                szb = _sigmoid(b_z)
                dpb = dpb_s[rows, cols]
                b_b = split(3, rows, cols)
                cb_v = cb_ref[rows, cols]
                put(6, rows, dpb * (b_b * cb_v) * (szb * (1.0 + b_z * (1.0 - szb))), cols)
                dub = dpb * (b_z * szb)
                put(3, rows, dub * cb_v, cols)
                ext_e[rows, cols] = dub * b_b
            return carry
        lax.fori_loop(0, nrb, branch_b, 0)

    def tile_conv_transposes(split, put, accum, wa_ref, wb_ref, ext_d, ext_e, sh, dm_s, dpb_s, dua0_s):
        for r, taps in _conv_a_taps(0, CONV_A - 1):
            if r == 0:
                src = ext_d
            else:
                sh[...] = ext_d[r:r + shl, :]
                src = sh

            def conv_t(rb, carry, src=src, taps=taps):
                rows = _rows(rb, RB)
                for cols in strips:
                    ua0 = dm_s[rows, cols]
                    dua0 = dua0_s[rows, cols]
                    for q, lag in taps:
                        k = CONV_A - 1 - lag
                        slab = src[pl.ds(pl.multiple_of(rb * RB + 8 * q, 8), RB), cols]
                        dua0 = dua0 + slab * wa_ref[k:k + 1, cols]
                        accum(ROW_DWA + k, slab * ua0, cols)
                    dua0_s[rows, cols] = dua0
                return carry
            lax.fori_loop(0, nrb, conv_t, 0)
        ext_d[TM:TM + HALO_A, :] = ext_d[0:HALO_A, :]

        dpb_s[...] = ext_e[0:TM, :] * wb_ref[CONV_B - 1:CONV_B, :]
        for lag in range(CONV_B):
            k = CONV_B - 1 - lag
            if lag > 0:
                sh[0:TM, :] = ext_e[lag:lag + TM, :]
                dpb_s[...] += sh[0:TM, :] * wb_ref[k:k + 1, :]
            src = ext_e if lag == 0 else sh

            def conv_b_w(rb, carry, src=src, k=k):
                rows = _rows(rb, RB)
                accum(ROW_DWB + k, src[rows, :] * (split(4, rows) * split(5, rows)))
                return carry
            lax.fori_loop(0, nrb, conv_b_w, 0)
        ext_e[TM:TM + HALO_B, :] = ext_e[0:HALO_B, :]

        def inputs(rb, carry):
            rows = _rows(rb, RB)
            for cols in strips:
                dua0 = dua0_s[rows, cols]
                a_val = split(0, rows, cols)
                sg = _sigmoid(split(1, rows, cols))
                put(0, rows, dua0 * sg, cols)
                put(1, rows, dua0 * a_val * sg * (1.0 - sg), cols)
                dcbin = dpb_s[rows, cols]
                put(4, rows, dcbin * split(5, rows, cols), cols)
                put(5, rows, dcbin * split(4, rows, cols), cols)
            return carry
        lax.fori_loop(0, nrb, inputs, 0)

    rev = lambda i: (_tile_block(nt - 1 - i, nt), 0)
    row_f32 = pl.BlockSpec((TM, d), rev)
    const = lambda shape: pl.BlockSpec(shape, lambda i: (0,) * len(shape))
    return pl.pallas_call(
        body, name="b1_mix", grid=(nt,),
        in_specs=[row_f32, pl.BlockSpec((TM, N_SPLIT * d), rev), row_f32, row_f32, row_f32, row_f32,
                  const((3, d, d)), const(wa.shape), const(wb.shape), const((1, d)), const((1, d))],
        out_specs=[pl.BlockSpec((TM, N_SPLIT * d), rev),
                   pl.BlockSpec((3, TM, d), lambda i: (0, _tile_block(nt - 1 - i, nt), 0)),
                   const((SM_ROWS, d))],
        out_shape=[jax.ShapeDtypeStruct((tp, N_SPLIT * d), BF16), jax.ShapeDtypeStruct((3, tp, d), BF16),
                   jax.ShapeDtypeStruct((SM_ROWS, d), F32)],
        scratch_shapes=[pltpu.VMEM((TM + HALO_A, d), F32), pltpu.VMEM((TM + HALO_B, d), F32),
                        pltpu.VMEM((shl, d), F32), pltpu.VMEM((TM, d), F32), pltpu.VMEM((TM, d), F32),
                        pltpu.VMEM((TM, d), F32), pltpu.VMEM((TM, d), F32),
                        pltpu.VMEM((SM_ROWS, 8, d), F32)],
        compiler_params=_params(("arbitrary",)),
    )(ds2, proj, ca, cb, ya, yb, w3, wa, wb, ln_g, ln_b)


def kernel(x, meta_tokens, norm_g, w_in, conv_a_w, conv_a_b, ln_a_g, ln_a_b, w_a_out, b_a_out, conv_b_w, w_b_out, w_out, final_g, loss_target, m_meta_tokens, m_norm_g, m_w_in, m_conv_a_w, m_conv_a_b, m_ln_a_g, m_ln_a_b, m_w_a_out, m_b_a_out, m_conv_b_w, m_w_b_out, m_w_out, m_final_g, v_meta_tokens, v_norm_g, v_w_in, v_conv_a_w, v_conv_a_b, v_ln_a_g, v_ln_a_b, v_w_a_out, v_b_a_out, v_conv_b_w, v_w_b_out, v_w_out, v_final_g):
    seq, d = x.shape[1], x.shape[2]
    dc = meta_tokens.shape[1]
    sw = w_in.shape[2]
    rsh = w_a_out.shape[1]
    xi, yi, ci = _mesh_pos()
    me = 2 * xi + yi
    pos = jnp.stack([ci, me]).astype(jnp.int32)

    conv_rows = HALO_A + HALO_B + 8
    convs = jnp.concatenate([
        jnp.pad(conv_a_w[0], ((0, HALO_A - CONV_A), (0, 0))),
        jnp.pad(conv_b_w[0], ((0, HALO_B - CONV_B), (0, 0))), jnp.zeros((8, dc), F32)], axis=0)[None]
    w3_own = jnp.stack([w_a_out[0], w_b_out[0], w_out[0]])
    fg2 = final_g.reshape(1, d)
    xs = x[0]

    h, front, placed = _h_prep(xs, meta_tokens, norm_g, [(w_in, BF16), (w3_own, BF16), (convs, F32)], pos)
    proj, (wg_in, wg3, convg) = _proj_fwd(h, placed, pos)
    w3 = wg3.reshape(3, N_CHIPS * rsh, d)
    convg = jnp.transpose(convg[0], (1, 0, 2)).reshape(conv_rows, N_CHIPS * dc)
    wa_full = convg[0:HALO_A]
    wb_full = convg[HALO_A:HALO_A + HALO_B]
    ca, cb, ya, yb, abm_t, ds2, h_t, loss8, dfg8 = _mix_fwd(
        xs, front, proj, loss_target[0], w3, wa_full, wb_full, conv_a_b, ln_a_g, ln_a_b, b_a_out, fg2, norm_g)
    dproj, d3, sm = _mix_bwd(ds2, proj, ca, cb, ya, yb, w3, wa_full, wb_full, ln_a_g, ln_a_b)
    cw_sq = _col_block(d, 512)
    per_sq = d // cw_sq
    p32_sq, pbf_sq, _ = _dw_reduced(
        abm_t, d3, cw_sq, 3 * per_sq, lambda t: (t // per_sq, t % per_sq), N_CHIPS,
        (3, N_CHIPS, rsh // 2, d), (None, N_CHIPS, rsh // 2, cw_sq),
        lambda u: (u // per_sq, 0, 0, u % per_sq), [], "dw_square")
    cw_in = _col_block(sw, 768)
    ncol = sw // cw_in
    p32_in, pbf_in, (l_sq,) = _dw_reduced(
        h_t[None], dproj[None], cw_in, N_CHIPS * ncol, lambda t: (0, t), 1,
        (1, N_CHIPS, d // 2, sw), (None, None, d // 2, cw_in), lambda u: (0, u // ncol, 0, u % ncol),
        [pbf_sq], "dw_in")
    grad_x, dfront, dng8, l_in = _dh_bwd(dproj, wg_in, xs, front, ds2, norm_g, pbf_in)
    half_in = _sum_chips([(p32_in, l_in)], sw, pos, "rs_sum_in")
    half_sq = _sum_chips([(p32_sq, l_sq)], d, pos, "rs_sum_sq")
    tail_row = lax.broadcasted_iota(jnp.int32, (8, d), 0)
    tail = jnp.where(tail_row == 0, dng8, jnp.where(tail_row == 1, dfg8,
                     jnp.where(tail_row == 2, loss8[0, 0], 0.0)))
    block = jnp.concatenate([sm, dfront[TM - N_META:TM], tail], axis=0)
    (other_in, other_sq), red = _sibling_swap([half_in, half_sq], block)
    col = lax.dynamic_slice(red, (0, me * dc), (AR_ROWS, dc))
    g_small = {
        "meta_tokens": col[ROW_DMETA:ROW_DMETA + N_META],
        "norm_g": red[ROW_DNG:ROW_DNG + 1],
        "conv_a_w": col[ROW_DWA:ROW_DWA + CONV_A][None],
        "conv_a_b": red[ROW_DCAB:ROW_DCAB + 1],
        "ln_a_g": red[ROW_DLNG:ROW_DLNG + 1],
        "ln_a_b": red[ROW_DLNB:ROW_DLNB + 1],
        "b_a_out": red[ROW_DBAO:ROW_DBAO + 1],
        "conv_b_w": col[ROW_DWB:ROW_DWB + CONV_B][None],
        "final_g": red[ROW_DFG],
    }

    upd_in = _adam_halves([w_in], [m_w_in], [v_w_in], half_in, other_in, pos, "adam_in")
    upd_sq = _adam_halves([w_a_out, w_b_out, w_out], [m_w_a_out, m_w_b_out, m_w_out],
                          [v_w_a_out, v_w_b_out, v_w_out], half_sq, other_sq, pos, "adam_sq")
    small_w = {"meta_tokens": (meta_tokens, m_meta_tokens, v_meta_tokens), "norm_g": (norm_g, m_norm_g, v_norm_g),
               "conv_a_w": (conv_a_w, m_conv_a_w, v_conv_a_w), "conv_a_b": (conv_a_b, m_conv_a_b, v_conv_a_b),
               "ln_a_g": (ln_a_g, m_ln_a_g, v_ln_a_g), "ln_a_b": (ln_a_b, m_ln_a_b, v_ln_a_b),
               "b_a_out": (b_a_out, m_b_a_out, v_b_a_out), "conv_b_w": (conv_b_w, m_conv_b_w, v_conv_b_w),
               "final_g": (final_g, m_final_g, v_final_g)}
    names_small = list(small_w)
    as2d = lambda t: t.reshape(-1, t.shape[-1])
    upd_small = _adam_small([(as2d(small_w[k][0]), as2d(g_small[k]), as2d(small_w[k][1]), as2d(small_w[k][2]))
                             for k in names_small])

    grads, deltas, new_m, new_v = dict(g_small), {}, {}, {}
    for k, upd in zip(names_small, upd_small):
        deltas[k], new_m[k], new_v[k] = [t.reshape(small_w[k][0].shape) for t in upd]
    grads["w_in"], deltas["w_in"], new_m["w_in"], new_v["w_in"] = upd_in
    for idx, k in enumerate(["w_a_out", "w_b_out", "w_out"]):
        grads[k], deltas[k], new_m[k], new_v[k] = upd_sq[4 * idx:4 * idx + 4]

    loss = red[ROW_LOSS, 0]
    order = ["meta_tokens", "norm_g", "w_in", "conv_a_w", "conv_a_b", "ln_a_g", "ln_a_b", "w_a_out", "b_a_out",
             "conv_b_w", "w_b_out", "w_out", "final_g"]
    return (loss, grad_x[None], *[grads[k] for k in order], *[deltas[k] for k in order],
            *[new_m[k] for k in order], *[new_v[k] for k in order])
```

```python
import jax
import jax.numpy as jnp
from jax import lax
from jax.experimental import pallas as pl
from jax.experimental.pallas import tpu as pltpu

F32 = jnp.float32
BF16 = jnp.bfloat16
MESH = pl.DeviceIdType.MESH

EPS = 1e-6
N_META = 16
N_SPLIT = 9
CONV_A = 31
CONV_B = 3
HALO_A = 32
HALO_B = 8
SHIFT_ROWS = 24
TM = 256
RB_FWD = 256
RB_BWD = 64
RB_CONV = 64
STRIP_COLS = 256
N_ROW_TILES_BIG = 8
ROW_BLOCK = 256
N_CHIPS = 4
VMEM_LIMIT = 56 * 1024 * 1024
VMEM_LIMIT_BIG = 62 * 1024 * 1024

ADAM_LR = 0.001
ADAM_B1 = 0.9
ADAM_B2 = 0.999
ADAM_EPS = 1e-08
ADAM_WD = 0.01
ADAM_STEP = 10

ROW_DWA = 0
ROW_DWB = 32
ROW_DCAB = 40
ROW_DLNG = 41
ROW_DLNB = 42
ROW_DBAO = 43
SM_ROWS = 48
ROW_DMETA = 48
ROW_DNG = 64
ROW_DFG = 65
ROW_LOSS = 66
AR_ROWS = 72


def _sigmoid(v):
    return 0.5 * jnp.tanh(0.5 * v) + 0.5


def _params(sem, **kw):
    return pltpu.CompilerParams(dimension_semantics=sem, vmem_limit_bytes=VMEM_LIMIT, **kw)


def _rows(rb, n):
    return pl.ds(pl.multiple_of(rb * n, n), n)


def _mesh_pos():
    x, y, c = lax.axis_index("x"), lax.axis_index("y"), lax.axis_index("c")
    return x, y, c


def _half(ref, j, c):
    h = ref.shape[2] // 2
    return ref.at[:, j, pl.ds(c * h, h), :]


class _Exchange:
    def __init__(self, sends, recvs):
        self.sends, self.recvs = sends, recvs

    def start(self):
        for cp in self.sends:
            cp.start()

    def finish(self):
        for cp in self.recvs:
            cp.wait_recv()
        for cp in self.sends:
            cp.wait_send()


def _chip_exchange(part_ref, land_ref, send_sems, recv_sems):
    x, y, c = _mesh_pos()
    me = 2 * x + y
    sends, recvs = [], []
    for k, (px, py) in enumerate([(1 - x, y), (x, 1 - y), (1 - x, 1 - y)]):
        sems = dict(send_sem=send_sems.at[k], recv_sem=recv_sems.at[k], device_id=(px, py, c), device_id_type=MESH)
        sends.append(pltpu.make_async_remote_copy(
            src_ref=part_ref.at[:, 2 * px + py], dst_ref=land_ref.at[:, me], **sems))
        landed = land_ref.at[:, 2 * px + py]
        recvs.append(pltpu.make_async_remote_copy(src_ref=landed, dst_ref=landed, **sems))
    return _Exchange(sends, recvs)

def _sibling_swap(halves, small):
    n = len(halves)

    def body(*refs):
        ins, small_ref, outs, red_ref = refs[:n], refs[n], refs[n + 1:2 * n + 1], refs[2 * n + 1]
        send_sems, recv_sems = refs[2 * n + 2:2 * n + 4]
        reduce = _SmallAllReduce(small_ref, red_ref, *refs[2 * n + 4:])
        x, y, c = _mesh_pos()
        copies = [pltpu.make_async_remote_copy(
            src_ref=ins[a], dst_ref=outs[a], send_sem=send_sems.at[a], recv_sem=recv_sems.at[a],
            device_id=(x, y, 1 - c), device_id_type=MESH) for a in range(n)]
        reduce.start()
        for cp in copies:
            cp.start()
        reduce.between_chips()
        reduce.finish()
        for cp in copies:
            cp.wait()

    any_spec = pl.BlockSpec(memory_space=pl.ANY)
    vm = pl.BlockSpec(memory_space=pltpu.VMEM)
    outs = pl.pallas_call(
        body, name="rs_swap",
        in_specs=[any_spec] * n + [vm], out_specs=[any_spec] * n + [vm],
        out_shape=[jax.ShapeDtypeStruct(h.shape, h.dtype) for h in halves]
        + [jax.ShapeDtypeStruct(small.shape, F32)],
        scratch_shapes=[pltpu.SemaphoreType.DMA((n,)), pltpu.SemaphoreType.DMA((n,))]
        + _SmallAllReduce.scratch(*small.shape),
    )(*halves, small)
    return outs[:n], outs[n]


class _SmallAllReduce:
    def __init__(self, x_ref, out_ref, sib_ref, part_ref, peers_ref, send_sems, recv_sems):
        self.x_ref, self.out_ref, self.sib_ref, self.part_ref, self.peers_ref = x_ref, out_ref, sib_ref, part_ref, peers_ref
        x, y, c = _mesh_pos()
        self.me = 2 * x + y
        self.swap = pltpu.make_async_remote_copy(
            src_ref=x_ref, dst_ref=sib_ref, send_sem=send_sems.at[0], recv_sem=recv_sems.at[0],
            device_id=(x, y, 1 - c), device_id_type=MESH)
        self.sends, self.recvs = [], []
        for k, (px, py) in enumerate([(1 - x, y), (x, 1 - y), (1 - x, 1 - y)]):
            sems = dict(send_sem=send_sems.at[1 + k], recv_sem=recv_sems.at[1 + k],
                        device_id=(px, py, c), device_id_type=MESH)
            self.sends.append(pltpu.make_async_remote_copy(src_ref=part_ref, dst_ref=peers_ref.at[self.me], **sems))
            landed = peers_ref.at[2 * px + py]
            self.recvs.append(pltpu.make_async_remote_copy(src_ref=landed, dst_ref=landed, **sems))

    @staticmethod
    def scratch(rows, d):
        return [pltpu.VMEM((rows, d), F32), pltpu.VMEM((rows, d), F32), pltpu.VMEM((N_CHIPS, rows, d), F32),
                pltpu.SemaphoreType.DMA((4,)), pltpu.SemaphoreType.DMA((4,))]

    def start(self):
        self.swap.start()

    def between_chips(self):
        self.swap.wait()
        self.part_ref[...] = self.x_ref[...] + self.sib_ref[...]
        self.peers_ref[self.me] = self.part_ref[...]
        for cp in self.sends:
            cp.start()

    def finish(self):
        for cp in self.recvs:
            cp.wait_recv()
        for cp in self.sends:
            cp.wait_send()
        p = self.peers_ref
        self.out_ref[...] = ((p[0] + p[1]) + p[2]) + p[3]


def _sum_chips(parts, cw, pos, name):
    s, _, h, _ = parts[0][0].shape
    hb = min(h, ROW_BLOCK)
    widths = [own.shape[3] // cw for own, _ in parts]
    starts = [sum(widths[:a]) for a in range(len(parts))]

    def body(pos_ref, *refs):
        out_ref = refs[-1]
        n = pl.program_id(2)
        total = None
        for a in range(len(parts)):
            own, l1, l2, l3 = refs[4 * a:4 * a + 4]
            val = ((own[...] + l1[...].astype(F32)) + l2[...].astype(F32)) + l3[...].astype(F32)
            total = val if total is None else jnp.where(n >= starts[a], val, total)
        out_ref[...] = total

    def slot(a, k):
        col = lambda n: jnp.clip(n - starts[a], 0, widths[a] - 1)
        return pl.BlockSpec((None, None, hb, cw),
                            lambda si, b, n, pos_ref: (si, (pos_ref[1] + k) % N_CHIPS, b, col(n)))

    operands, specs = [], []
    for a, (own, landed) in enumerate(parts):
        operands += [own, landed, landed, landed]
        specs += [slot(a, 0), slot(a, 1), slot(a, 2), slot(a, 3)]
    return pl.pallas_call(
        body, name=name,
        grid_spec=pltpu.PrefetchScalarGridSpec(
            num_scalar_prefetch=1, grid=(s, h // hb, sum(widths)), in_specs=specs,
            out_specs=pl.BlockSpec((None, hb, cw), lambda si, b, n, pos_ref: (si, b, n))),
        out_shape=jax.ShapeDtypeStruct((s, h, sum(widths) * cw), F32),
        compiler_params=_params(("arbitrary",) * 3),
    )(pos, *operands)


def _adamw(w, g, m, v):
    m = ADAM_B1 * m + (1.0 - ADAM_B1) * g
    v = ADAM_B2 * v + (1.0 - ADAM_B2) * (g * g)
    m_hat = m / (1.0 - ADAM_B1 ** ADAM_STEP)
    v_hat = v / (1.0 - ADAM_B2 ** ADAM_STEP)
    delta = -ADAM_LR * (m_hat / (jnp.sqrt(v_hat) + ADAM_EPS) + ADAM_WD * w)
    return delta, m, v


def _adam_halves(ws, ms, vs, g_own, g_recv, pos, name):
    n = len(ws)
    _, r, c = ws[0].shape
    h = r // 2
    rb = min(h, ROW_BLOCK)
    nb = h // rb

    def body(pos_ref, *refs):
        w_refs, m_refs, v_refs = refs[:n], refs[n:2 * n], refs[2 * n:3 * n]
        go_ref, gr_ref = refs[3 * n:3 * n + 2]
        outs = refs[3 * n + 2:]
        mine = pl.program_id(0) == pos_ref[0]
        for a in range(n):
            g = jnp.where(mine, go_ref[a], gr_ref[a])
            delta, m, v = _adamw(w_refs[a][...], g, m_refs[a][...], v_refs[a][...])
            outs[4 * a][...], outs[4 * a + 1][...], outs[4 * a + 2][...], outs[4 * a + 3][...] = g, delta, m, v

    spec_w = pl.BlockSpec((None, rb, c), lambda hf, b, pos_ref: (0, hf * nb + b, 0))
    spec_g = pl.BlockSpec((n, rb, c), lambda hf, b, pos_ref: (0, b, 0))
    return pl.pallas_call(
        body, name=name,
        grid_spec=pltpu.PrefetchScalarGridSpec(
            num_scalar_prefetch=1, grid=(2, nb), in_specs=[spec_w] * (3 * n) + [spec_g] * 2,
            out_specs=[spec_w] * (4 * n)),
        out_shape=[jax.ShapeDtypeStruct((1, r, c), F32)] * (4 * n),
        compiler_params=_params(("arbitrary",) * 2),
    )(pos, *ws, *ms, *vs, g_own, g_recv)


def _adam_small(items):
    n = len(items)

    def body(*refs):
        ins, outs = refs[:4 * n], refs[4 * n:]
        for a in range(n):
            w_ref, g_ref, m_ref, v_ref = ins[4 * a:4 * a + 4]
            d, m, v = _adamw(w_ref[...], g_ref[...], m_ref[...], v_ref[...])
            outs[3 * a][...] = d
            outs[3 * a + 1][...] = m
            outs[3 * a + 2][...] = v

    vm = pl.BlockSpec(memory_space=pltpu.VMEM)
    flat = [t for it in items for t in it]
    outs = pl.pallas_call(
        body, name="adam_small", in_specs=[vm] * (4 * n), out_specs=[vm] * (3 * n),
        out_shape=[jax.ShapeDtypeStruct(it[0].shape, F32) for it in items for _ in range(3)],
    )(*flat)
    return [tuple(outs[3 * a:3 * a + 3]) for a in range(n)]


def _big_row_spec(seq, tmb, d, tile_of):
    return pl.BlockSpec((pl.Element(tmb), pl.Element(d)),
                        lambda *args: (pl.multiple_of(jnp.minimum(tile_of(*args) * tmb, seq - tmb), 8), 0))


def _big_row_tile(x_ref, front_ref, i):
    rows = x_ref[...]
    last = jnp.concatenate([rows[TM:], front_ref[...]], axis=0)
    return jnp.where(i == N_ROW_TILES_BIG - 1, last, rows)


def _h_prep(x, meta, norm_g, shards, pos):
    seq, d = x.shape
    tp = seq + TM
    dc = meta.shape[1]
    tmb = tp // N_ROW_TILES_BIG
    assert tmb >= TM and tp == tmb * N_ROW_TILES_BIG
    last = N_ROW_TILES_BIG - 1
    ns = len(shards)

    def body(pos_ref, x_ref, meta_ref, g_ref, *refs):
        shard_refs, (h_ref, front_ref), placed_refs = refs[:ns], refs[ns:ns + 2], refs[ns + 2:2 * ns + 2]
        metas, msend, mrecv = refs[2 * ns + 2:]
        for a in range(ns):
            placed_refs[a][...] = shard_refs[a][...].astype(placed_refs[a].dtype)
        x, y, c = _mesh_pos()
        me = 2 * x + y
        chips = [(1 - x, y), (x, 1 - y), (1 - x, 1 - y)]
        i = pl.program_id(0)

        def meta_copy(k, chip):
            return pltpu.make_async_remote_copy(
                src_ref=metas.at[chip], dst_ref=metas.at[chip], send_sem=msend.at[k], recv_sem=mrecv.at[k],
                device_id=(*chips[k], c), device_id_type=MESH)

        @pl.when(i == 0)
        def _():
            metas[me] = meta_ref[...]
            for k in range(3):
                meta_copy(k, me).start()
            front_ref[...] = jnp.zeros_like(front_ref)

        @pl.when(i == last)
        def _():
            for k, (px, py) in enumerate(chips):
                meta_copy(k, 2 * px + py).wait_recv()
            for q in range(N_CHIPS):
                front_ref[TM - N_META:TM, q * dc:(q + 1) * dc] = metas[q]

        s = _big_row_tile(x_ref, front_ref, i)
        r = lax.rsqrt(jnp.mean(s * s, axis=-1, keepdims=True) + EPS)
        h_ref[...] = (s * r * g_ref[...]).astype(BF16)

        @pl.when(i == last)
        def _():
            for k in range(3):
                meta_copy(k, me).wait_send()

    shard_in, shard_out, shard_shapes = [], [], []
    for arr, dtype in shards:
        s, r, c = arr.shape
        sliced = r % (N_ROW_TILES_BIG * 16) == 0
        rp = r // N_ROW_TILES_BIG if sliced else r
        step = (lambda i: i) if sliced else (lambda i: 0)
        shard_in.append(pl.BlockSpec((s, rp, c), lambda i, pos_ref, step=step: (0, step(i), 0)))
        shard_out.append(pl.BlockSpec((s, None, rp, c), lambda i, pos_ref, step=step: (0, pos_ref[1], step(i), 0)))
        shard_shapes.append(jax.ShapeDtypeStruct((s, N_CHIPS, r, c), dtype))
    outs = pl.pallas_call(
        body, name="f0_norm",
        grid_spec=pltpu.PrefetchScalarGridSpec(
            num_scalar_prefetch=1, grid=(N_ROW_TILES_BIG,),
            in_specs=[_big_row_spec(seq, tmb, d, lambda i, pos_ref: i),
                      pl.BlockSpec(meta.shape, lambda i, pos_ref: (0, 0)),
                      pl.BlockSpec((1, d), lambda i, pos_ref: (0, 0))] + shard_in,
            out_specs=[pl.BlockSpec((tmb, d), lambda i, pos_ref: (i, 0)),
                       pl.BlockSpec((TM, d), lambda i, pos_ref: (0, 0))] + shard_out,
            scratch_shapes=[pltpu.VMEM((N_CHIPS,) + meta.shape, F32),
                            pltpu.SemaphoreType.DMA((3,)), pltpu.SemaphoreType.DMA((3,))]),
        out_shape=[jax.ShapeDtypeStruct((tp, d), BF16), jax.ShapeDtypeStruct((TM, d), F32)] + shard_shapes,
        compiler_params=_params(("arbitrary",)),
    )(pos, x, meta, norm_g, *[arr for arr, _ in shards])
    return outs[0], outs[1], outs[2:]


N_UNITS = 3
N_STEPS_PROJ = N_CHIPS * N_UNITS


def _proj_plan(u):
    v = u - N_UNITS
    if v < 2 * N_UNITS:
        return v % 2, v // 2
    return 2, v - 2 * N_UNITS


def _proj_unit(t, me):
    v = t - N_UNITS
    near = v < 2 * N_UNITS
    rel = jnp.where(near, v % 2, 2)
    unit = jnp.where(near, v // 2, v - 2 * N_UNITS)
    flip = jnp.where(rel == 0, 2, jnp.where(rel == 1, 1, 3))
    own = t < N_UNITS
    return jnp.where(own, me, lax.bitwise_xor(me, flip)), jnp.where(own, t, unit)


def _proj_fwd(h, bufs, pos):
    tp, d = h.shape
    _, nsh, _, sw = bufs[0].shape
    cu = sw // N_UNITS
    assert cu % 128 == 0
    n = len(bufs)
    w_sems = 6 * N_UNITS
    last = N_STEPS_PROJ - 1
    late = N_STEPS_PROJ - N_UNITS

    def body(pos_ref, h_ref, *refs):
        proj_ref = refs[n]
        gbufs = refs[n + 1:2 * n + 1]
        wbuf, wsems, send_sems, recv_sems = refs[2 * n + 1:]
        x, y, c = _mesh_pos()
        me = 2 * x + y
        sibling = (x, y, 1 - c)
        chips = [(1 - x, y), (x, 1 - y), (1 - x, 1 - y)]
        chip_ids = [2 * px + py for px, py in chips]
        relayed_chip = jnp.where(c == 0, chip_ids[0], chip_ids[1])
        relay_to = (jnp.where(c == 0, x, 1 - x), jnp.where(c == 0, 1 - y, y), c)
        t = pl.program_id(0)

        def remote(idx, piece, to):
            return pltpu.make_async_remote_copy(
                src_ref=piece, dst_ref=piece, send_sem=send_sems.at[idx], recv_sem=recv_sems.at[idx],
                device_id=to, device_id_type=MESH)

        hr = d // 2

        def chunk_of(chip, half, k):
            return gbufs[0].at[0, chip, pl.ds(half * hr, hr), pl.ds(k * cu, cu)]

        def own_chunk(r, k):
            return remote(6 * k + r, chunk_of(me, c, k), (*chips[r], c))

        def landed_chunk(r, k):
            return remote(6 * k + r, chunk_of(chip_ids[r], c, k), (*chips[r], c))

        def relay_chunk(k):
            return remote(6 * k + 2, chunk_of(relayed_chip, c, k), relay_to)

        def sibling_chunk(r, k, half):
            return remote(6 * k + 3 + r, chunk_of(chip_ids[r], half, k), sibling)

        def fetch(u):
            chip, unit = _proj_unit(jnp.int32(u), me)
            return pltpu.make_async_copy(gbufs[0].at[0, chip, :, pl.ds(pl.multiple_of(unit * cu, 128), cu)],
                                         wbuf.at[u % 2], wsems.at[u % 2])

        def make_available(u):
            r, k = _proj_plan(u)
            landed_chunk(r, k).wait_recv()
            if r < 2:
                pl.when(c == r)(lambda: relay_chunk(k).start())
            sibling_chunk(r, k, c).start()
            sibling_chunk(r, k, 1 - c).wait_recv()

        def own_piece(a, r):
            return remote(w_sems + 6 * (a - 1) + r, _half(gbufs[a], me, c), (*chips[r], c))

        def relay(a):
            return remote(w_sems + 6 * (a - 1) + 2, _half(gbufs[a], relayed_chip, c), relay_to)

        def to_sibling(a, r, core):
            return remote(w_sems + 6 * (a - 1) + 3 + r, _half(gbufs[a], chip_ids[r], core), sibling)

        def landed(a, r):
            return remote(w_sems + 6 * (a - 1) + r, _half(gbufs[a], chip_ids[r], c), (*chips[r], c))

        for u in range(N_STEPS_PROJ):
            @pl.when(t == u)
            def _(u=u):
                if u == 0:
                    for k in range(N_UNITS):
                        for r in range(2):
                            own_chunk(r, k).start()
                    for a in range(1, n):
                        for r in range(2):
                            own_piece(a, r).start()
                    fetch(0).start()
                if u < last:
                    if u + 1 >= N_UNITS:
                        make_available(u + 1)
                    fetch(u + 1).start()
                if u == late:
                    for a in range(1, n):
                        landed(a, 0).wait_recv()
                        landed(a, 1).wait_recv()
                        relay(a).start()
                        for r in range(2):
                            to_sibling(a, r, c).start()
                        for r in range(2):
                            to_sibling(a, r, 1 - c).wait_recv()
                fetch(u).wait()

        proj_ref[...] = jnp.dot(h_ref[...], wbuf[t % 2], preferred_element_type=F32).astype(BF16)

        @pl.when(t == last)
        def _():
            for a in range(1, n):
                landed(a, 2).wait_recv()
                to_sibling(a, 2, c).start()
                to_sibling(a, 2, 1 - c).wait_recv()
            for k in range(N_UNITS):
                for r in range(2):
                    own_chunk(r, k).wait_send()
                relay_chunk(k).wait_send()
                for r in range(3):
                    sibling_chunk(r, k, c).wait_send()
            for a in range(1, n):
                for r in range(2):
                    own_piece(a, r).wait_send()
                relay(a).wait_send()
                for r in range(3):
                    to_sibling(a, r, c).wait_send()

    def out_index(t, pos_ref):
        chip, unit = _proj_unit(t, pos_ref[1])
        return 0, chip * N_UNITS + unit

    any_spec = pl.BlockSpec(memory_space=pl.ANY)
    outs = pl.pallas_call(
        body, name="f1_proj",
        grid_spec=pltpu.PrefetchScalarGridSpec(
            num_scalar_prefetch=1, grid=(N_STEPS_PROJ,),
            in_specs=[pl.BlockSpec((tp, d), lambda t, pos_ref: (0, 0))] + [any_spec] * n,
            out_specs=[pl.BlockSpec((tp, cu), out_index)] + [any_spec] * n,
            scratch_shapes=[pltpu.VMEM((2, d, cu), BF16), pltpu.SemaphoreType.DMA((2,)),
                            pltpu.SemaphoreType.DMA((w_sems + 6 * (n - 1),)),
                            pltpu.SemaphoreType.DMA((w_sems + 6 * (n - 1),))]),
        out_shape=[jax.ShapeDtypeStruct((tp, nsh * sw), BF16)]
        + [jax.ShapeDtypeStruct(b.shape, b.dtype) for b in bufs],
        input_output_aliases={2 + a: 1 + a for a in range(n)},
        compiler_params=_params(("arbitrary",)),
    )(pos, h, *bufs)
    return outs[0], outs[1:]


def _dh_bwd(dproj, wg_in, x, front, ds2, norm_g, part):
    seq, d = x.shape
    tp = seq + TM
    _, nsh, _, sw = wg_in.shape
    tmb = tp // N_ROW_TILES_BIG
    tail = tmb - TM
    last = N_ROW_TILES_BIG - 1

    def body(dp_ref, w_hbm, x_ref, front_ref, ds2_ref, g_ref, part_ref, gx_hbm, dfront_ref, dng_ref, land_ref,
             wbuf, gacc, dsbuf, wsem, osems, send_sems, recv_sems):
        exchange = _chip_exchange(part_ref, land_ref, send_sems, recv_sems)
        i = pl.program_id(0)

        def x_rows_out(step):
            return pltpu.make_async_copy(dsbuf.at[step % 2], gx_hbm.at[pl.ds(step * tmb, tmb), :], osems.at[step % 2])

        last_out = pltpu.make_async_copy(dsbuf.at[last % 2, pl.ds(0, tail), :],
                                         gx_hbm.at[pl.ds(last * tmb, tail), :], osems.at[last % 2])

        @pl.when(i == 0)
        def _():
            exchange.start()
            gacc[...] = jnp.zeros_like(gacc)
            whole = pltpu.make_async_copy(w_hbm.at[0], wbuf, wsem)
            whole.start()
            whole.wait()

        dh = None
        for j in range(nsh):
            part = lax.dot_general(dp_ref[:, j * sw:(j + 1) * sw], wbuf[j], (((1,), (1,)), ((), ())),
                                   preferred_element_type=F32)
            dh = part if dh is None else dh + part
        s = _big_row_tile(x_ref, front_ref, i)
        r = lax.rsqrt(jnp.mean(s * s, axis=-1, keepdims=True) + EPS)
        gacc[...] += (dh * s * r).reshape(tmb // 8, 8, d).sum(axis=0)
        t = dh * g_ref[...]

        @pl.when(i >= 2)
        def _():
            x_rows_out(i - 2).wait()

        dsbuf[i % 2] = ds2_ref[...] + r * t - s * (r * r * r) * jnp.mean(t * s, axis=-1, keepdims=True)

        @pl.when(i < last)
        def _():
            x_rows_out(i).start()

        @pl.when(i == last)
        def _():
            last_out.start()
            dfront_ref[...] = dsbuf[last % 2, tail:, :]
            dng_ref[...] = jnp.broadcast_to(jnp.sum(gacc[...], axis=0, keepdims=True), (8, d))
            exchange.finish()
            x_rows_out(last - 1).wait()
            last_out.wait()

    any_spec = pl.BlockSpec(memory_space=pl.ANY)
    return pl.pallas_call(
        body, name="b2_dh", grid=(N_ROW_TILES_BIG,),
        in_specs=[pl.BlockSpec((tmb, nsh * sw), lambda i: (i, 0)), any_spec,
                  _big_row_spec(seq, tmb, d, lambda i: i),
                  pl.BlockSpec((TM, d), lambda i: (0, 0)),
                  pl.BlockSpec((tmb, d), lambda i: (i, 0)),
                  pl.BlockSpec((1, d), lambda i: (0, 0)), any_spec],
        out_specs=[any_spec, pl.BlockSpec((TM, d), lambda i: (0, 0)),
                   pl.BlockSpec((8, d), lambda i: (0, 0)), any_spec],
        out_shape=[jax.ShapeDtypeStruct((seq, d), F32), jax.ShapeDtypeStruct((TM, d), F32),
                   jax.ShapeDtypeStruct((8, d), F32), jax.ShapeDtypeStruct(part.shape, part.dtype)],
        scratch_shapes=[pltpu.VMEM((nsh, d, sw), BF16), pltpu.VMEM((8, d), F32), pltpu.VMEM((2, tmb, d), F32),
                        pltpu.SemaphoreType.DMA, pltpu.SemaphoreType.DMA((2,)),
                        pltpu.SemaphoreType.DMA((3,)), pltpu.SemaphoreType.DMA((3,))],
        compiler_params=pltpu.CompilerParams(dimension_semantics=("arbitrary",),
                                             vmem_limit_bytes=VMEM_LIMIT_BIG),
    )(dproj, wg_in, x, front, ds2, norm_g, part)


def _col_block(width, cap):
    return max(b for b in range(128, cap + 1, 128) if width % b == 0)


def _dw_reduced(lhs_t, rhs, cw, nblk, operands, groups, out_dims, out_block, out_index, carried, name):
    na, d, tp = lhs_t.shape
    rg = d // groups
    hh = rg // 2
    nc = len(carried)

    def body(*refs):
        l_ref, r_ref = refs[:2]
        part_refs = refs[2:2 + nc]
        p32_ref, pbf_ref = refs[2 + nc:4 + nc]
        land_refs = refs[4 + nc:4 + 2 * nc]
        res, rbuf, send_sems, recv_sems = refs[4 + 2 * nc:8 + 2 * nc]
        xsems = refs[8 + 2 * nc:]
        exchanges = [_chip_exchange(part_refs[e], land_refs[e], xsems[2 * e], xsems[2 * e + 1]) for e in range(nc)]
        exchange = _Exchange([s for ex in exchanges for s in ex.sends], [r for ex in exchanges for r in ex.recvs])
        x, y, c = _mesh_pos()
        t = pl.program_id(0)
        u = jnp.maximum(t - 1, 0)

        def to_sibling(blk):
            return pltpu.make_async_remote_copy(
                src_ref=res.at[blk % 2, :, pl.ds((1 - c) * hh, hh), :], dst_ref=rbuf.at[blk % 2],
                send_sem=send_sems.at[blk], recv_sem=recv_sems.at[blk],
                device_id=(x, y, 1 - c), device_id_type=MESH)

        @pl.when(t == 0)
        def _():
            exchange.start()

        @pl.when(t < nblk)
        def _():
            res[t % 2] = jnp.dot(l_ref[...], r_ref[...], preferred_element_type=F32).reshape(groups, rg, cw)

        @pl.when(t >= 1)
        def _():
            to_sibling(u).wait_recv()
            p = res[u % 2, :, pl.ds(c * hh, hh), :] + rbuf[u % 2]
            p32_ref[...] = p.reshape(p32_ref.shape)
            pbf_ref[...] = p.reshape(pbf_ref.shape).astype(BF16)

        @pl.when(t < nblk)
        def _():
            to_sibling(t).start()

        @pl.when(t >= 1)
        def _():
            to_sibling(u).wait_send()

        @pl.when(t == nblk)
        def _():
            exchange.finish()

    any_spec = pl.BlockSpec(memory_space=pl.ANY)
    last = nblk - 1
    out_spec = pl.BlockSpec(out_block, lambda t: out_index(jnp.maximum(t - 1, 0)))
    outs = pl.pallas_call(
        body, name=name, grid=(nblk + 1,),
        in_specs=[pl.BlockSpec((None, d, tp), lambda t: (operands(jnp.minimum(t, last))[0], 0, 0)),
                  pl.BlockSpec((None, tp, cw), lambda t: (operands(jnp.minimum(t, last))[0], 0,
                                                          operands(jnp.minimum(t, last))[1]))]
        + [any_spec] * nc,
        out_specs=[out_spec, out_spec] + [any_spec] * nc,
        out_shape=[jax.ShapeDtypeStruct(out_dims, F32), jax.ShapeDtypeStruct(out_dims, BF16)]
        + [jax.ShapeDtypeStruct(e.shape, e.dtype) for e in carried],
        scratch_shapes=[pltpu.VMEM((2, groups, rg, cw), F32), pltpu.VMEM((2, groups, hh, cw), F32),
                        pltpu.SemaphoreType.DMA((nblk,)), pltpu.SemaphoreType.DMA((nblk,))]
        + [pltpu.SemaphoreType.DMA((3,)), pltpu.SemaphoreType.DMA((3,))] * nc,
        compiler_params=_params(("arbitrary",)),
    )(lhs_t, rhs, *carried)
    return outs[0], outs[1], outs[2:]


def _conv_a_taps(first_lag, last_lag):
    out = []
    for r in range(8):
        taps = [(q, 8 * q + r) for q in range(5) if first_lag <= 8 * q + r <= last_lag]
        if taps:
            out.append((r, taps))
    return out


def _tile_block(i, nt):
    return jnp.where(i == 0, nt - 1, i - 1)


def _mix_fwd(x, front, proj, target, w3, wa, wb, conv_a_b, ln_g, ln_b, b_a_out, final_g, h):
    seq, d = x.shape
    tp = seq + TM
    nt = tp // TM
    RB = RB_FWD
    nrb = TM // RB
    strips = [slice(c0, c0 + STRIP_COLS) for c0 in range(0, d, STRIP_COLS)]
    shl = TM + SHIFT_ROWS

    def body(x_ref, front_ref, proj_ref, tgt_ref, w3_ref, wa_ref, wb_ref, cab_ref, lng_ref, lnb_ref, bao_ref, fg_ref,
             h_ref, ca_ref, cb_ref, ya_ref, yb_ref, abmt_ref, ds2_ref, ht_ref, loss_ref, dfg_ref,
             abm_ref, ext_a, ext_b, sh, s2_s, lacc, gacc):
        i = pl.program_id(0)

        def split(k, rows):
            return proj_ref[rows, k * d:(k + 1) * d].astype(F32)

        def s_tile():
            return jnp.where(i == 0, front_ref[...], x_ref[...])

        ht_ref[...] = h_ref[...].T

        @pl.when(i == 0)
        def _():
            ext_a[0:HALO_A, :] = jnp.zeros((HALO_A, d), F32)
            ext_b[0:HALO_B, :] = jnp.zeros((HALO_B, d), F32)
            lacc[...] = jnp.zeros_like(lacc)
            gacc[...] = jnp.zeros_like(gacc)

        def conv_in(rb, carry):
            rows = _rows(rb, RB)
            ua0 = split(0, rows) * _sigmoid(split(1, rows))
            ext_a[pl.ds(pl.multiple_of(HALO_A + rb * RB, 8), RB), :] = ua0
            ext_b[pl.ds(pl.multiple_of(HALO_B + rb * RB, 8), RB), :] = split(4, rows) * split(5, rows)
            ca_ref[rows, :] = jnp.broadcast_to(cab_ref[...], (RB, d))
            return carry
        lax.fori_loop(0, nrb, conv_in, 0)

        @pl.when(i == 0)
        def _():
            abmt_ref[...] = jnp.zeros_like(abmt_ref)
            ds2_ref[...] = jnp.zeros_like(ds2_ref)

        @pl.when(i > 0)
        def _():
            tile_after_conv_inputs(split, s_tile, tgt_ref, w3_ref, wa_ref, wb_ref, lng_ref, lnb_ref, bao_ref, fg_ref,
                                   ca_ref, cb_ref, ya_ref, yb_ref, abmt_ref, ds2_ref, abm_ref, ext_a, ext_b, sh,
                                   s2_s, lacc, gacc)

        ext_a[0:HALO_A, :] = ext_a[TM:TM + HALO_A, :]
        ext_b[0:HALO_B, :] = ext_b[TM:TM + HALO_B, :]

        @pl.when(i == nt - 1)
        def _():
            loss_ref[...] = jnp.broadcast_to(0.5 * jnp.sum(lacc[...]) * (1.0 / d), (8, 128))
            dfg_ref[...] = jnp.broadcast_to(jnp.sum(gacc[...], axis=0, keepdims=True), (8, d))

    def tile_after_conv_inputs(split, s_tile, tgt_ref, w3_ref, wa_ref, wb_ref, lng_ref, lnb_ref, bao_ref, fg_ref,
                               ca_ref, cb_ref, ya_ref, yb_ref, abmt_ref, ds2_ref, abm_ref, ext_a, ext_b, sh, s2_s,
                               lacc, gacc):
        for r, taps in _conv_a_taps(HALO_A - CONV_A + 1, HALO_A):
            if r == 0:
                src = ext_a
            else:
                sh[...] = ext_a[r:r + shl, :]
                src = sh

            def conv_acc(rb, carry, src=src, taps=taps):
                rows = _rows(rb, RB_CONV)
                for cols in strips:
                    acc = ca_ref[rows, cols]
                    for q, lag in taps:
                        k = lag - (HALO_A - CONV_A + 1)
                        slab = src[pl.ds(pl.multiple_of(rb * RB_CONV + 8 * q, 8), RB_CONV), cols]
                        acc = acc + slab * wa_ref[k:k + 1, cols]
                    ca_ref[rows, cols] = acc
                return carry
            lax.fori_loop(0, TM // RB_CONV, conv_acc, 0)

        cb_ref[...] = ext_b[HALO_B:HALO_B + TM, :] * wb_ref[2:3, :]
        for k in range(CONV_B - 1):
            off = HALO_B - CONV_B + 1 + k
            sh[0:TM, :] = ext_b[off:off + TM, :]
            cb_ref[...] += sh[0:TM, :] * wb_ref[k:k + 1, :]

        def branches(rb, carry):
            rows = _rows(rb, RB)
            ca = ca_ref[rows, :]
            mu = jnp.mean(ca, axis=-1, keepdims=True)
            xc = ca - mu
            rstd = lax.rsqrt(jnp.mean(xc * xc, axis=-1, keepdims=True) + EPS)
            ln = xc * rstd * lng_ref[...] + lnb_ref[...]
            ua = ln * _sigmoid(ln)
            a_z = split(2, rows)
            abm_ref[0, rows, :] = (ua * (a_z * _sigmoid(a_z))).astype(BF16)
            return carry
        lax.fori_loop(0, nrb, branches, 0)

        def branch_b(rb, carry):
            rows = _rows(rb, RB)
            b_z = split(6, rows)
            ub = split(3, rows) * cb_ref[rows, :]
            abm_ref[1, rows, :] = (ub * (b_z * _sigmoid(b_z))).astype(BF16)
            return carry
        lax.fori_loop(0, nrb, branch_b, 0)

        ya_ref[...] = jnp.dot(abm_ref[0], w3_ref[0], preferred_element_type=F32) + bao_ref[...]
        yb_ref[...] = jnp.dot(abm_ref[1], w3_ref[1], preferred_element_type=F32)

        def merge(rb, carry):
            rows = _rows(rb, RB)
            m = _sigmoid(split(7, rows)) * ya_ref[rows, :] + _sigmoid(split(8, rows)) * yb_ref[rows, :]
            abm_ref[2, rows, :] = m.astype(BF16)
            return carry
        lax.fori_loop(0, nrb, merge, 0)

        s2_s[...] = s_tile() + jnp.dot(abm_ref[2], w3_ref[2], preferred_element_type=F32)
        for k in range(3):
            abmt_ref[k] = abm_ref[k].T

        def head(rb, carry):
            rows = _rows(rb, RB)
            s2 = s2_s[rows, :]
            r2 = lax.rsqrt(jnp.mean(s2 * s2, axis=-1, keepdims=True) + EPS)
            diff = s2 * r2 * fg_ref[...] - tgt_ref[rows, :]
            lacc[...] += diff * diff
            dy = diff * (1.0 / d)
            gacc[...] += (dy * s2 * r2).reshape(RB // 8, 8, d).sum(axis=0)
            t = dy * fg_ref[...]
            ds2_ref[rows, :] = r2 * t - s2 * (r2 * r2 * r2) * jnp.mean(t * s2, axis=-1, keepdims=True)
            return carry
        lax.fori_loop(0, nrb, head, 0)

    row_f32 = pl.BlockSpec((TM, d), lambda i: (_tile_block(i, nt), 0))
    x_rows = pl.BlockSpec((TM, d), lambda i: (jnp.maximum(i - 1, 0), 0))
    const = lambda shape: pl.BlockSpec(shape, lambda i: (0,) * len(shape))
    return pl.pallas_call(
        body, name="f2_mix", grid=(nt,),
        in_specs=[x_rows, const((TM, d)),
                  pl.BlockSpec((TM, N_SPLIT * d), lambda i: (_tile_block(i, nt), 0)),
                  x_rows,
                  const((3, d, d)), const(wa.shape), const(wb.shape)] + [const((1, d))] * 5
        + [pl.BlockSpec((TM, d), lambda i: (_tile_block(i, nt), 0))],
        out_specs=[row_f32, row_f32, row_f32, row_f32,
                   pl.BlockSpec((3, d, TM), lambda i: (0, 0, _tile_block(i, nt))),
                   row_f32, pl.BlockSpec((d, TM), lambda i: (0, _tile_block(i, nt))),
                   const((8, 128)), const((8, d))],
        out_shape=[jax.ShapeDtypeStruct((tp, d), F32)] * 4
        + [jax.ShapeDtypeStruct((3, d, tp), BF16), jax.ShapeDtypeStruct((tp, d), F32),
           jax.ShapeDtypeStruct((d, tp), BF16),
           jax.ShapeDtypeStruct((8, 128), F32), jax.ShapeDtypeStruct((8, d), F32)],
        scratch_shapes=[pltpu.VMEM((3, TM, d), BF16),
                        pltpu.VMEM((HALO_A + TM, d), F32), pltpu.VMEM((HALO_B + TM, d), F32),
                        pltpu.VMEM((shl, d), F32), pltpu.VMEM((TM, d), F32),
                        pltpu.VMEM((RB, d), F32), pltpu.VMEM((8, d), F32)],
        compiler_params=_params(("arbitrary",)),
    )(x, front, proj, target, w3, wa, wb, conv_a_b, ln_g, ln_b, b_a_out, final_g, h)


def _mix_bwd(ds2, proj, ca, cb, ya, yb, w3, wa, wb, ln_g, ln_b):
    tp, d = ds2.shape
    nt = tp // TM
    RB = RB_BWD
    nrb = TM // RB
    strips = [slice(c0, c0 + STRIP_COLS) for c0 in range(0, d, STRIP_COLS)]
    shl = TM + SHIFT_ROWS
    nt_dims = (((1,), (1,)), ((), ()))

    def body(ds2_ref, proj_ref, ca_ref, cb_ref, ya_ref, yb_ref, w3_ref, wa_ref, wb_ref, lng_ref, lnb_ref,
             dproj_ref, d3_ref, sm_ref, ext_d, ext_e, sh, dm_s, dpa_s, dpb_s, dua0_s, acc):
        step = pl.program_id(0)

        def split(k, rows, cols=slice(0, d)):
            return proj_ref[rows, k * d + cols.start:k * d + cols.stop].astype(F32)

        def put(k, rows, val, cols=slice(0, d)):
            dproj_ref[rows, k * d + cols.start:k * d + cols.stop] = val.astype(BF16)

        def accum(row, val, cols=slice(0, d)):
            acc[row, :, cols] += val.reshape(RB // 8, 8, val.shape[-1]).sum(axis=0)

        @pl.when(step == 0)
        def _():
            ext_d[TM:TM + HALO_A, :] = jnp.zeros((HALO_A, d), F32)
            ext_e[TM:TM + HALO_B, :] = jnp.zeros((HALO_B, d), F32)
            acc[...] = jnp.zeros_like(acc)

        front = step == nt - 1

        @pl.when(front)
        def _():
            d3_ref[...] = jnp.zeros_like(d3_ref)

            def conv_only(rb, carry):
                rows = _rows(rb, RB)
                zeros = jnp.zeros((RB, d), F32)
                for k in (2, 3, 6, 7, 8):
                    put(k, rows, zeros)
                ext_d[rows, :] = zeros
                ext_e[rows, :] = zeros
                dua0_s[rows, :] = zeros
                dm_s[rows, :] = split(0, rows) * _sigmoid(split(1, rows))
                return carry
            lax.fori_loop(0, nrb, conv_only, 0)

        @pl.when(jnp.logical_not(front))
        def _():
            tile_to_conv_outputs(split, put, accum, ds2_ref, ca_ref, cb_ref, ya_ref, yb_ref, w3_ref, lng_ref, lnb_ref,
                                 d3_ref, ext_d, ext_e, dm_s, dpa_s, dpb_s, dua0_s)

        tile_conv_transposes(split, put, accum, wa_ref, wb_ref, ext_d, ext_e, sh, dm_s, dpb_s, dua0_s)

        @pl.when(front)
        def _():
            for row in range(SM_ROWS):
                sm_ref[row:row + 1, :] = jnp.sum(acc[row], axis=0, keepdims=True)

    def tile_to_conv_outputs(split, put, accum, ds2_ref, ca_ref, cb_ref, ya_ref, yb_ref, w3_ref, lng_ref, lnb_ref,
                             d3_ref, ext_d, ext_e, dm_s, dpa_s, dpb_s, dua0_s):
        d3_ref[2] = ds2_ref[...].astype(BF16)
        dm_s[...] = lax.dot_general(d3_ref[2], w3_ref[2], nt_dims, preferred_element_type=F32)

        def gates(rb, carry):
            rows = _rows(rb, RB)
            for cols in strips:
                dm = dm_s[rows, cols]
                sa = _sigmoid(split(7, rows, cols))
                sb = _sigmoid(split(8, rows, cols))
                put(7, rows, dm * ya_ref[rows, cols] * sa * (1.0 - sa), cols)
                put(8, rows, dm * yb_ref[rows, cols] * sb * (1.0 - sb), cols)
                dya = dm * sa
                accum(ROW_DBAO, dya, cols)
                d3_ref[0, rows, cols] = dya.astype(BF16)
                d3_ref[1, rows, cols] = (dm * sb).astype(BF16)
            return carry
        lax.fori_loop(0, nrb, gates, 0)

        dpa_s[...] = lax.dot_general(d3_ref[0], w3_ref[0], nt_dims, preferred_element_type=F32)
        dpb_s[...] = lax.dot_general(d3_ref[1], w3_ref[1], nt_dims, preferred_element_type=F32)

        def branch_a(rb, carry):
            rows = _rows(rb, RB)

            def row_mean(strip_fn):
                total = strip_fn(strips[0])
                for cols in strips[1:]:
                    total = total + strip_fn(cols)
                return jnp.sum(total, axis=-1, keepdims=True) * (1.0 / d)

            mu = row_mean(lambda cols: ca_ref[rows, cols])
            rstd = lax.rsqrt(row_mean(lambda cols: jnp.square(ca_ref[rows, cols] - mu)) + EPS)
            sum_dxh = jnp.zeros((RB, STRIP_COLS), F32)
            sum_dxh_xhat = jnp.zeros((RB, STRIP_COLS), F32)
            for cols in strips:
                xhat = (ca_ref[rows, cols] - mu) * rstd
                ln = xhat * lng_ref[:, cols] + lnb_ref[:, cols]
                sl = _sigmoid(ln)
                ua = ln * sl
                a_z = split(2, rows, cols)
                sz = _sigmoid(a_z)
                dpa = dpa_s[rows, cols]
                put(2, rows, dpa * ua * (sz * (1.0 + a_z * (1.0 - sz))), cols)
                dln = dpa * (a_z * sz) * (sl * (1.0 + ln * (1.0 - sl)))
                accum(ROW_DLNG, dln * xhat, cols)
                accum(ROW_DLNB, dln, cols)
                dxh = dln * lng_ref[:, cols]
                sum_dxh = sum_dxh + dxh
                sum_dxh_xhat = sum_dxh_xhat + dxh * xhat
                ext_d[rows, cols] = dxh
                dpa_s[rows, cols] = xhat
            mean_dxh = jnp.sum(sum_dxh, axis=-1, keepdims=True) * (1.0 / d)
            mean_dxh_xhat = jnp.sum(sum_dxh_xhat, axis=-1, keepdims=True) * (1.0 / d)
            for cols in strips:
                dca = rstd * (ext_d[rows, cols] - mean_dxh - dpa_s[rows, cols] * mean_dxh_xhat)
                accum(ROW_DCAB, dca, cols)
                ext_d[rows, cols] = dca
            return carry
        lax.fori_loop(0, nrb, branch_a, 0)

        def branch_b(rb, carry):
            rows = _rows(rb, RB)
            dua0_s[rows, :] = jnp.zeros((RB, d), F32)
            for cols in strips:
                dm_s[rows, cols] = split(0, rows, cols) * _sigmoid(split(1, rows, cols))
                b_z = split(6, rows, cols)
                szb = _sigmoid(b_z)
                dpb = dpb_s[rows, cols]
                b_b = split(3, rows, cols)
                cb_v = cb_ref[rows, cols]
                put(6, rows, dpb * (b_b * cb_v) * (szb * (1.0 + b_z * (1.0 - szb))), cols)
                dub = dpb * (b_z * szb)
                put(3, rows, dub * cb_v, cols)
                ext_e[rows, cols] = dub * b_b
            return carry
        lax.fori_loop(0, nrb, branch_b, 0)

    def tile_conv_transposes(split, put, accum, wa_ref, wb_ref, ext_d, ext_e, sh, dm_s, dpb_s, dua0_s):
        for r, taps in _conv_a_taps(0, CONV_A - 1):
            if r == 0:
                src = ext_d
            else:
                sh[...] = ext_d[r:r + shl, :]
                src = sh

            def conv_t(rb, carry, src=src, taps=taps):
                rows = _rows(rb, RB)
                for cols in strips:
                    ua0 = dm_s[rows, cols]
                    dua0 = dua0_s[rows, cols]
                    for q, lag in taps:
                        k = CONV_A - 1 - lag
                        slab = src[pl.ds(pl.multiple_of(rb * RB + 8 * q, 8), RB), cols]
                        dua0 = dua0 + slab * wa_ref[k:k + 1, cols]
                        accum(ROW_DWA + k, slab * ua0, cols)
                    dua0_s[rows, cols] = dua0
                return carry
            lax.fori_loop(0, nrb, conv_t, 0)
        ext_d[TM:TM + HALO_A, :] = ext_d[0:HALO_A, :]

        dpb_s[...] = ext_e[0:TM, :] * wb_ref[CONV_B - 1:CONV_B, :]
        for lag in range(CONV_B):
            k = CONV_B - 1 - lag
            if lag > 0:
                sh[0:TM, :] = ext_e[lag:lag + TM, :]
                dpb_s[...] += sh[0:TM, :] * wb_ref[k:k + 1, :]
            src = ext_e if lag == 0 else sh

            def conv_b_w(rb, carry, src=src, k=k):
                rows = _rows(rb, RB)
                accum(ROW_DWB + k, src[rows, :] * (split(4, rows) * split(5, rows)))
                return carry
            lax.fori_loop(0, nrb, conv_b_w, 0)
        ext_e[TM:TM + HALO_B, :] = ext_e[0:HALO_B, :]

        def inputs(rb, carry):
            rows = _rows(rb, RB)
            for cols in strips:
                dua0 = dua0_s[rows, cols]
                a_val = split(0, rows, cols)
                sg = _sigmoid(split(1, rows, cols))
                put(0, rows, dua0 * sg, cols)
                put(1, rows, dua0 * a_val * sg * (1.0 - sg), cols)
                dcbin = dpb_s[rows, cols]
                put(4, rows, dcbin * split(5, rows, cols), cols)
                put(5, rows, dcbin * split(4, rows, cols), cols)
            return carry
        lax.fori_loop(0, nrb, inputs, 0)

    rev = lambda i: (_tile_block(nt - 1 - i, nt), 0)
    row_f32 = pl.BlockSpec((TM, d), rev)
    const = lambda shape: pl.BlockSpec(shape, lambda i: (0,) * len(shape))
    return pl.pallas_call(
        body, name="b1_mix", grid=(nt,),
        in_specs=[row_f32, pl.BlockSpec((TM, N_SPLIT * d), rev), row_f32, row_f32, row_f32, row_f32,
                  const((3, d, d)), const(wa.shape), const(wb.shape), const((1, d)), const((1, d))],
        out_specs=[pl.BlockSpec((TM, N_SPLIT * d), rev),
                   pl.BlockSpec((3, TM, d), lambda i: (0, _tile_block(nt - 1 - i, nt), 0)),
                   const((SM_ROWS, d))],
        out_shape=[jax.ShapeDtypeStruct((tp, N_SPLIT * d), BF16), jax.ShapeDtypeStruct((3, tp, d), BF16),
                   jax.ShapeDtypeStruct((SM_ROWS, d), F32)],
        scratch_shapes=[pltpu.VMEM((TM + HALO_A, d), F32), pltpu.VMEM((TM + HALO_B, d), F32),
                        pltpu.VMEM((shl, d), F32), pltpu.VMEM((TM, d), F32), pltpu.VMEM((TM, d), F32),
                        pltpu.VMEM((TM, d), F32), pltpu.VMEM((TM, d), F32),
                        pltpu.VMEM((SM_ROWS, 8, d), F32)],
        compiler_params=_params(("arbitrary",)),
    )(ds2, proj, ca, cb, ya, yb, w3, wa, wb, ln_g, ln_b)


def kernel(x, meta_tokens, norm_g, w_in, conv_a_w, conv_a_b, ln_a_g, ln_a_b, w_a_out, b_a_out, conv_b_w, w_b_out, w_out, final_g, loss_target, m_meta_tokens, m_norm_g, m_w_in, m_conv_a_w, m_conv_a_b, m_ln_a_g, m_ln_a_b, m_w_a_out, m_b_a_out, m_conv_b_w, m_w_b_out, m_w_out, m_final_g, v_meta_tokens, v_norm_g, v_w_in, v_conv_a_w, v_conv_a_b, v_ln_a_g, v_ln_a_b, v_w_a_out, v_b_a_out, v_conv_b_w, v_w_b_out, v_w_out, v_final_g):
    seq, d = x.shape[1], x.shape[2]
    dc = meta_tokens.shape[1]
    sw = w_in.shape[2]
    rsh = w_a_out.shape[1]
    xi, yi, ci = _mesh_pos()
    me = 2 * xi + yi
    pos = jnp.stack([ci, me]).astype(jnp.int32)

    conv_rows = HALO_A + HALO_B + 8
    convs = jnp.concatenate([
        jnp.pad(conv_a_w[0], ((0, HALO_A - CONV_A), (0, 0))),
        jnp.pad(conv_b_w[0], ((0, HALO_B - CONV_B), (0, 0))), jnp.zeros((8, dc), F32)], axis=0)[None]
    w3_own = jnp.stack([w_a_out[0], w_b_out[0], w_out[0]])
    fg2 = final_g.reshape(1, d)
    xs = x[0]

    h, front, placed = _h_prep(xs, meta_tokens, norm_g, [(w_in, BF16), (w3_own, BF16), (convs, F32)], pos)
    proj, (wg_in, wg3, convg) = _proj_fwd(h, placed, pos)
    w3 = wg3.reshape(3, N_CHIPS * rsh, d)
    convg = jnp.transpose(convg[0], (1, 0, 2)).reshape(conv_rows, N_CHIPS * dc)
    wa_full = convg[0:HALO_A]
    wb_full = convg[HALO_A:HALO_A + HALO_B]
    ca, cb, ya, yb, abm_t, ds2, h_t, loss8, dfg8 = _mix_fwd(
        xs, front, proj, loss_target[0], w3, wa_full, wb_full, conv_a_b, ln_a_g, ln_a_b, b_a_out, fg2, h)
    dproj, d3, sm = _mix_bwd(ds2, proj, ca, cb, ya, yb, w3, wa_full, wb_full, ln_a_g, ln_a_b)
    cw_sq = _col_block(d, 512)
    per_sq = d // cw_sq
    p32_sq, pbf_sq, _ = _dw_reduced(
        abm_t, d3, cw_sq, 3 * per_sq, lambda t: (t // per_sq, t % per_sq), N_CHIPS,
        (3, N_CHIPS, rsh // 2, d), (None, N_CHIPS, rsh // 2, cw_sq),
        lambda u: (u // per_sq, 0, 0, u % per_sq), [], "dw_square")
    cw_in = _col_block(sw, 768)
    ncol = sw // cw_in
    p32_in, pbf_in, (l_sq,) = _dw_reduced(
        h_t[None], dproj[None], cw_in, N_CHIPS * ncol, lambda t: (0, t), 1,
        (1, N_CHIPS, d // 2, sw), (None, None, d // 2, cw_in), lambda u: (0, u // ncol, 0, u % ncol),
        [pbf_sq], "dw_in")
    grad_x, dfront, dng8, l_in = _dh_bwd(dproj, wg_in, xs, front, ds2, norm_g, pbf_in)
    half_in = _sum_chips([(p32_in, l_in)], sw, pos, "rs_sum_in")
    half_sq = _sum_chips([(p32_sq, l_sq)], d, pos, "rs_sum_sq")
    tail_row = lax.broadcasted_iota(jnp.int32, (8, d), 0)
    tail = jnp.where(tail_row == 0, dng8, jnp.where(tail_row == 1, dfg8,
                     jnp.where(tail_row == 2, loss8[0, 0], 0.0)))
    block = jnp.concatenate([sm, dfront[TM - N_META:TM], tail], axis=0)
    (other_in, other_sq), red = _sibling_swap([half_in, half_sq], block)
    col = lax.dynamic_slice(red, (0, me * dc), (AR_ROWS, dc))
    g_small = {
        "meta_tokens": col[ROW_DMETA:ROW_DMETA + N_META],
        "norm_g": red[ROW_DNG:ROW_DNG + 1],
        "conv_a_w": col[ROW_DWA:ROW_DWA + CONV_A][None],
        "conv_a_b": red[ROW_DCAB:ROW_DCAB + 1],
        "ln_a_g": red[ROW_DLNG:ROW_DLNG + 1],
        "ln_a_b": red[ROW_DLNB:ROW_DLNB + 1],
        "b_a_out": red[ROW_DBAO:ROW_DBAO + 1],
        "conv_b_w": col[ROW_DWB:ROW_DWB + CONV_B][None],
        "final_g": red[ROW_DFG],
    }

    upd_in = _adam_halves([w_in], [m_w_in], [v_w_in], half_in, other_in, pos, "adam_in")
    upd_sq = _adam_halves([w_a_out, w_b_out, w_out], [m_w_a_out, m_w_b_out, m_w_out],
                          [v_w_a_out, v_w_b_out, v_w_out], half_sq, other_sq, pos, "adam_sq")
    small_w = {"meta_tokens": (meta_tokens, m_meta_tokens, v_meta_tokens), "norm_g": (norm_g, m_norm_g, v_norm_g),
               "conv_a_w": (conv_a_w, m_conv_a_w, v_conv_a_w), "conv_a_b": (conv_a_b, m_conv_a_b, v_conv_a_b),
               "ln_a_g": (ln_a_g, m_ln_a_g, v_ln_a_g), "ln_a_b": (ln_a_b, m_ln_a_b, v_ln_a_b),
               "b_a_out": (b_a_out, m_b_a_out, v_b_a_out), "conv_b_w": (conv_b_w, m_conv_b_w, v_conv_b_w),
               "final_g": (final_g, m_final_g, v_final_g)}
    names_small = list(small_w)
    as2d = lambda t: t.reshape(-1, t.shape[-1])
    upd_small = _adam_small([(as2d(small_w[k][0]), as2d(g_small[k]), as2d(small_w[k][1]), as2d(small_w[k][2]))
                             for k in names_small])

    grads, deltas, new_m, new_v = dict(g_small), {}, {}, {}
    for k, upd in zip(names_small, upd_small):
        deltas[k], new_m[k], new_v[k] = [t.reshape(small_w[k][0].shape) for t in upd]
    grads["w_in"], deltas["w_in"], new_m["w_in"], new_v["w_in"] = upd_in
    for idx, k in enumerate(["w_a_out", "w_b_out", "w_out"]):
        grads[k], deltas[k], new_m[k], new_v[k] = upd_sq[4 * idx:4 * idx + 4]

    loss = red[ROW_LOSS, 0]
    order = ["meta_tokens", "norm_g", "w_in", "conv_a_w", "conv_a_b", "ln_a_g", "ln_a_b", "w_a_out", "b_a_out",
             "conv_b_w", "w_b_out", "w_out", "final_g"]
    return (loss, grad_x[None], *[grads[k] for k in order], *[deltas[k] for k in order],
            *[new_m[k] for k in order], *[new_v[k] for k in order])
```

```python
import jax
import jax.numpy as jnp
from jax import lax
from jax.experimental import pallas as pl
from jax.experimental.pallas import tpu as pltpu

F32 = jnp.float32
BF16 = jnp.bfloat16
MESH = pl.DeviceIdType.MESH

EPS = 1e-6
N_META = 16
N_SPLIT = 9
CONV_A = 31
CONV_B = 3
HALO_A = 32
HALO_B = 8
SHIFT_ROWS = 24
TM = 256
RB_FWD = 256
RB_BWD = 64
STRIP_COLS = 256
N_ROW_TILES_BIG = 8
ROW_BLOCK = 256
N_CHIPS = 4
VMEM_LIMIT = 56 * 1024 * 1024
VMEM_LIMIT_BIG = 62 * 1024 * 1024

ADAM_LR = 0.001
ADAM_B1 = 0.9
ADAM_B2 = 0.999
ADAM_EPS = 1e-08
ADAM_WD = 0.01
ADAM_STEP = 10

ROW_DWA = 0
ROW_DWB = 32
ROW_DCAB = 40
ROW_DLNG = 41
ROW_DLNB = 42
ROW_DBAO = 43
SM_ROWS = 48
ROW_DMETA = 48
ROW_DNG = 64
ROW_DFG = 65
ROW_LOSS = 66
AR_ROWS = 72


def _sigmoid(v):
    return 0.5 * jnp.tanh(0.5 * v) + 0.5


def _params(sem, **kw):
    return pltpu.CompilerParams(dimension_semantics=sem, vmem_limit_bytes=VMEM_LIMIT, **kw)


def _rows(rb, n):
    return pl.ds(pl.multiple_of(rb * n, n), n)


def _mesh_pos():
    x, y, c = lax.axis_index("x"), lax.axis_index("y"), lax.axis_index("c")
    return x, y, c


def _half(ref, j, c):
    h = ref.shape[2] // 2
    return ref.at[:, j, pl.ds(c * h, h), :]


class _Exchange:
    def __init__(self, sends, recvs):
        self.sends, self.recvs = sends, recvs

    def start(self):
        for cp in self.sends:
            cp.start()

    def finish(self):
        for cp in self.recvs:
            cp.wait_recv()
        for cp in self.sends:
            cp.wait_send()


def _chip_exchange(part_ref, land_ref, send_sems, recv_sems):
    x, y, c = _mesh_pos()
    me = 2 * x + y
    sends, recvs = [], []
    for k, (px, py) in enumerate([(1 - x, y), (x, 1 - y), (1 - x, 1 - y)]):
        sems = dict(send_sem=send_sems.at[k], recv_sem=recv_sems.at[k], device_id=(px, py, c), device_id_type=MESH)
        sends.append(pltpu.make_async_remote_copy(
            src_ref=part_ref.at[:, 2 * px + py], dst_ref=land_ref.at[:, me], **sems))
        landed = land_ref.at[:, 2 * px + py]
        recvs.append(pltpu.make_async_remote_copy(src_ref=landed, dst_ref=landed, **sems))
    return _Exchange(sends, recvs)

def _sibling_swap(halves, small):
    n = len(halves)

    def body(*refs):
        ins, small_ref, outs, red_ref = refs[:n], refs[n], refs[n + 1:2 * n + 1], refs[2 * n + 1]
        send_sems, recv_sems = refs[2 * n + 2:2 * n + 4]
        reduce = _SmallAllReduce(small_ref, red_ref, *refs[2 * n + 4:])
        x, y, c = _mesh_pos()
        copies = [pltpu.make_async_remote_copy(
            src_ref=ins[a], dst_ref=outs[a], send_sem=send_sems.at[a], recv_sem=recv_sems.at[a],
            device_id=(x, y, 1 - c), device_id_type=MESH) for a in range(n)]
        reduce.start()
        for cp in copies:
            cp.start()
        reduce.between_chips()
        reduce.finish()
        for cp in copies:
            cp.wait()

    any_spec = pl.BlockSpec(memory_space=pl.ANY)
    vm = pl.BlockSpec(memory_space=pltpu.VMEM)
    outs = pl.pallas_call(
        body, name="rs_swap",
        in_specs=[any_spec] * n + [vm], out_specs=[any_spec] * n + [vm],
        out_shape=[jax.ShapeDtypeStruct(h.shape, h.dtype) for h in halves]
        + [jax.ShapeDtypeStruct(small.shape, F32)],
        scratch_shapes=[pltpu.SemaphoreType.DMA((n,)), pltpu.SemaphoreType.DMA((n,))]
        + _SmallAllReduce.scratch(*small.shape),
    )(*halves, small)
    return outs[:n], outs[n]


class _SmallAllReduce:
    def __init__(self, x_ref, out_ref, sib_ref, part_ref, peers_ref, send_sems, recv_sems):
        self.x_ref, self.out_ref, self.sib_ref, self.part_ref, self.peers_ref = x_ref, out_ref, sib_ref, part_ref, peers_ref
        x, y, c = _mesh_pos()
        self.me = 2 * x + y
        self.swap = pltpu.make_async_remote_copy(
            src_ref=x_ref, dst_ref=sib_ref, send_sem=send_sems.at[0], recv_sem=recv_sems.at[0],
            device_id=(x, y, 1 - c), device_id_type=MESH)
        self.sends, self.recvs = [], []
        for k, (px, py) in enumerate([(1 - x, y), (x, 1 - y), (1 - x, 1 - y)]):
            sems = dict(send_sem=send_sems.at[1 + k], recv_sem=recv_sems.at[1 + k],
                        device_id=(px, py, c), device_id_type=MESH)
            self.sends.append(pltpu.make_async_remote_copy(src_ref=part_ref, dst_ref=peers_ref.at[self.me], **sems))
            landed = peers_ref.at[2 * px + py]
            self.recvs.append(pltpu.make_async_remote_copy(src_ref=landed, dst_ref=landed, **sems))

    @staticmethod
    def scratch(rows, d):
        return [pltpu.VMEM((rows, d), F32), pltpu.VMEM((rows, d), F32), pltpu.VMEM((N_CHIPS, rows, d), F32),
                pltpu.SemaphoreType.DMA((4,)), pltpu.SemaphoreType.DMA((4,))]

    def start(self):
        self.swap.start()

    def between_chips(self):
        self.swap.wait()
        self.part_ref[...] = self.x_ref[...] + self.sib_ref[...]
        self.peers_ref[self.me] = self.part_ref[...]
        for cp in self.sends:
            cp.start()

    def finish(self):
        for cp in self.recvs:
            cp.wait_recv()
        for cp in self.sends:
            cp.wait_send()
        p = self.peers_ref
        self.out_ref[...] = ((p[0] + p[1]) + p[2]) + p[3]


def _sum_chips(parts, cw, pos, name):
    s, _, h, _ = parts[0][0].shape
    hb = min(h, ROW_BLOCK)
    widths = [own.shape[3] // cw for own, _ in parts]
    starts = [sum(widths[:a]) for a in range(len(parts))]

    def body(pos_ref, *refs):
        out_ref = refs[-1]
        n = pl.program_id(2)
        total = None
        for a in range(len(parts)):
            own, l1, l2, l3 = refs[4 * a:4 * a + 4]
            val = ((own[...] + l1[...].astype(F32)) + l2[...].astype(F32)) + l3[...].astype(F32)
            total = val if total is None else jnp.where(n >= starts[a], val, total)
        out_ref[...] = total

    def slot(a, k):
        col = lambda n: jnp.clip(n - starts[a], 0, widths[a] - 1)
        return pl.BlockSpec((None, None, hb, cw),
                            lambda si, b, n, pos_ref: (si, (pos_ref[1] + k) % N_CHIPS, b, col(n)))

    operands, specs = [], []
    for a, (own, landed) in enumerate(parts):
        operands += [own, landed, landed, landed]
        specs += [slot(a, 0), slot(a, 1), slot(a, 2), slot(a, 3)]
    return pl.pallas_call(
        body, name=name,
        grid_spec=pltpu.PrefetchScalarGridSpec(
            num_scalar_prefetch=1, grid=(s, h // hb, sum(widths)), in_specs=specs,
            out_specs=pl.BlockSpec((None, hb, cw), lambda si, b, n, pos_ref: (si, b, n))),
        out_shape=jax.ShapeDtypeStruct((s, h, sum(widths) * cw), F32),
        compiler_params=_params(("arbitrary",) * 3),
    )(pos, *operands)


def _adamw(w, g, m, v):
    m = ADAM_B1 * m + (1.0 - ADAM_B1) * g
    v = ADAM_B2 * v + (1.0 - ADAM_B2) * (g * g)
    m_hat = m / (1.0 - ADAM_B1 ** ADAM_STEP)
    v_hat = v / (1.0 - ADAM_B2 ** ADAM_STEP)
    delta = -ADAM_LR * (m_hat / (jnp.sqrt(v_hat) + ADAM_EPS) + ADAM_WD * w)
    return delta, m, v


def _adam_halves(ws, ms, vs, g_own, g_recv, pos, name):
    n = len(ws)
    _, r, c = ws[0].shape
    h = r // 2
    rb = min(h, ROW_BLOCK)
    nb = h // rb

    def body(pos_ref, *refs):
        w_refs, m_refs, v_refs = refs[:n], refs[n:2 * n], refs[2 * n:3 * n]
        go_ref, gr_ref = refs[3 * n:3 * n + 2]
        outs = refs[3 * n + 2:]
        mine = pl.program_id(0) == pos_ref[0]
        for a in range(n):
            g = jnp.where(mine, go_ref[a], gr_ref[a])
            delta, m, v = _adamw(w_refs[a][...], g, m_refs[a][...], v_refs[a][...])
            outs[4 * a][...], outs[4 * a + 1][...], outs[4 * a + 2][...], outs[4 * a + 3][...] = g, delta, m, v

    spec_w = pl.BlockSpec((None, rb, c), lambda hf, b, pos_ref: (0, hf * nb + b, 0))
    spec_g = pl.BlockSpec((n, rb, c), lambda hf, b, pos_ref: (0, b, 0))
    return pl.pallas_call(
        body, name=name,
        grid_spec=pltpu.PrefetchScalarGridSpec(
            num_scalar_prefetch=1, grid=(2, nb), in_specs=[spec_w] * (3 * n) + [spec_g] * 2,
            out_specs=[spec_w] * (4 * n)),
        out_shape=[jax.ShapeDtypeStruct((1, r, c), F32)] * (4 * n),
        compiler_params=_params(("arbitrary",) * 2),
    )(pos, *ws, *ms, *vs, g_own, g_recv)


def _adam_small(items):
    n = len(items)

    def body(*refs):
        ins, outs = refs[:4 * n], refs[4 * n:]
        for a in range(n):
            w_ref, g_ref, m_ref, v_ref = ins[4 * a:4 * a + 4]
            d, m, v = _adamw(w_ref[...], g_ref[...], m_ref[...], v_ref[...])
            outs[3 * a][...] = d
            outs[3 * a + 1][...] = m
            outs[3 * a + 2][...] = v

    vm = pl.BlockSpec(memory_space=pltpu.VMEM)
    flat = [t for it in items for t in it]
    outs = pl.pallas_call(
        body, name="adam_small", in_specs=[vm] * (4 * n), out_specs=[vm] * (3 * n),
        out_shape=[jax.ShapeDtypeStruct(it[0].shape, F32) for it in items for _ in range(3)],
    )(*flat)
    return [tuple(outs[3 * a:3 * a + 3]) for a in range(n)]


def _big_row_spec(seq, tmb, d, tile_of):
    return pl.BlockSpec((pl.Element(tmb), pl.Element(d)),
                        lambda *args: (pl.multiple_of(jnp.minimum(tile_of(*args) * tmb, seq - tmb), 8), 0))


def _big_row_tile(x_ref, front_ref, i):
    rows = x_ref[...]
    last = jnp.concatenate([rows[TM:], front_ref[...]], axis=0)
    return jnp.where(i == N_ROW_TILES_BIG - 1, last, rows)


def _h_prep(x, meta, norm_g, shards, pos):
    seq, d = x.shape
    tp = seq + TM
    dc = meta.shape[1]
    tmb = tp // N_ROW_TILES_BIG
    assert tmb >= TM and tp == tmb * N_ROW_TILES_BIG
    last = N_ROW_TILES_BIG - 1
    ns = len(shards)

    def body(pos_ref, x_ref, meta_ref, g_ref, *refs):
        shard_refs, (h_ref, front_ref), placed_refs = refs[:ns], refs[ns:ns + 2], refs[ns + 2:2 * ns + 2]
        metas, msend, mrecv = refs[2 * ns + 2:]
        for a in range(ns):
            placed_refs[a][...] = shard_refs[a][...].astype(placed_refs[a].dtype)
        x, y, c = _mesh_pos()
        me = 2 * x + y
        chips = [(1 - x, y), (x, 1 - y), (1 - x, 1 - y)]
        i = pl.program_id(0)

        def meta_copy(k, chip):
            return pltpu.make_async_remote_copy(
                src_ref=metas.at[chip], dst_ref=metas.at[chip], send_sem=msend.at[k], recv_sem=mrecv.at[k],
                device_id=(*chips[k], c), device_id_type=MESH)

        @pl.when(i == 0)
        def _():
            metas[me] = meta_ref[...]
            for k in range(3):
                meta_copy(k, me).start()
            front_ref[...] = jnp.zeros_like(front_ref)

        @pl.when(i == last)
        def _():
            for k, (px, py) in enumerate(chips):
                meta_copy(k, 2 * px + py).wait_recv()
            for q in range(N_CHIPS):
                front_ref[TM - N_META:TM, q * dc:(q + 1) * dc] = metas[q]

        s = _big_row_tile(x_ref, front_ref, i)
        r = lax.rsqrt(jnp.mean(s * s, axis=-1, keepdims=True) + EPS)
        h_ref[...] = (s * r * g_ref[...]).astype(BF16)

        @pl.when(i == last)
        def _():
            for k in range(3):
                meta_copy(k, me).wait_send()

    shard_in, shard_out, shard_shapes = [], [], []
    for arr, dtype in shards:
        s, r, c = arr.shape
        sliced = r % (N_ROW_TILES_BIG * 16) == 0
        rp = r // N_ROW_TILES_BIG if sliced else r
        step = (lambda i: i) if sliced else (lambda i: 0)
        shard_in.append(pl.BlockSpec((s, rp, c), lambda i, pos_ref, step=step: (0, step(i), 0)))
        shard_out.append(pl.BlockSpec((s, None, rp, c), lambda i, pos_ref, step=step: (0, pos_ref[1], step(i), 0)))
        shard_shapes.append(jax.ShapeDtypeStruct((s, N_CHIPS, r, c), dtype))
    outs = pl.pallas_call(
        body, name="f0_norm",
        grid_spec=pltpu.PrefetchScalarGridSpec(
            num_scalar_prefetch=1, grid=(N_ROW_TILES_BIG,),
            in_specs=[_big_row_spec(seq, tmb, d, lambda i, pos_ref: i),
                      pl.BlockSpec(meta.shape, lambda i, pos_ref: (0, 0)),
                      pl.BlockSpec((1, d), lambda i, pos_ref: (0, 0))] + shard_in,
            out_specs=[pl.BlockSpec((tmb, d), lambda i, pos_ref: (i, 0)),
                       pl.BlockSpec((TM, d), lambda i, pos_ref: (0, 0))] + shard_out,
            scratch_shapes=[pltpu.VMEM((N_CHIPS,) + meta.shape, F32),
                            pltpu.SemaphoreType.DMA((3,)), pltpu.SemaphoreType.DMA((3,))]),
        out_shape=[jax.ShapeDtypeStruct((tp, d), BF16), jax.ShapeDtypeStruct((TM, d), F32)] + shard_shapes,
        compiler_params=_params(("arbitrary",)),
    )(pos, x, meta, norm_g, *[arr for arr, _ in shards])
    return outs[0], outs[1], outs[2:]


N_UNITS = 3
N_STEPS_PROJ = N_CHIPS * N_UNITS


def _proj_plan(u):
    v = u - N_UNITS
    if v < 2 * N_UNITS:
        return v % 2, v // 2
    return 2, v - 2 * N_UNITS


def _proj_unit(t, me):
    v = t - N_UNITS
    near = v < 2 * N_UNITS
    rel = jnp.where(near, v % 2, 2)
    unit = jnp.where(near, v // 2, v - 2 * N_UNITS)
    flip = jnp.where(rel == 0, 2, jnp.where(rel == 1, 1, 3))
    own = t < N_UNITS
    return jnp.where(own, me, lax.bitwise_xor(me, flip)), jnp.where(own, t, unit)


def _proj_fwd(h, bufs, pos):
    tp, d = h.shape
    _, nsh, _, sw = bufs[0].shape
    cu = sw // N_UNITS
    assert cu % 128 == 0
    n = len(bufs)
    w_sems = 6 * N_UNITS
    last = N_STEPS_PROJ - 1
    late = N_STEPS_PROJ - N_UNITS

    def body(pos_ref, h_ref, *refs):
        proj_ref = refs[n]
        gbufs = refs[n + 1:2 * n + 1]
        wbuf, wsems, send_sems, recv_sems = refs[2 * n + 1:]
        x, y, c = _mesh_pos()
        me = 2 * x + y
        sibling = (x, y, 1 - c)
        chips = [(1 - x, y), (x, 1 - y), (1 - x, 1 - y)]
        chip_ids = [2 * px + py for px, py in chips]
        relayed_chip = jnp.where(c == 0, chip_ids[0], chip_ids[1])
        relay_to = (jnp.where(c == 0, x, 1 - x), jnp.where(c == 0, 1 - y, y), c)
        t = pl.program_id(0)

        def remote(idx, piece, to):
            return pltpu.make_async_remote_copy(
                src_ref=piece, dst_ref=piece, send_sem=send_sems.at[idx], recv_sem=recv_sems.at[idx],
                device_id=to, device_id_type=MESH)

        hr = d // 2

        def chunk_of(chip, half, k):
            return gbufs[0].at[0, chip, pl.ds(half * hr, hr), pl.ds(k * cu, cu)]

        def own_chunk(r, k):
            return remote(6 * k + r, chunk_of(me, c, k), (*chips[r], c))

        def landed_chunk(r, k):
            return remote(6 * k + r, chunk_of(chip_ids[r], c, k), (*chips[r], c))

        def relay_chunk(k):
            return remote(6 * k + 2, chunk_of(relayed_chip, c, k), relay_to)

        def sibling_chunk(r, k, half):
            return remote(6 * k + 3 + r, chunk_of(chip_ids[r], half, k), sibling)

        def fetch(u):
            chip, unit = _proj_unit(jnp.int32(u), me)
            return pltpu.make_async_copy(gbufs[0].at[0, chip, :, pl.ds(pl.multiple_of(unit * cu, 128), cu)],
                                         wbuf.at[u % 2], wsems.at[u % 2])

        def make_available(u):
            r, k = _proj_plan(u)
            landed_chunk(r, k).wait_recv()
            if r < 2:
                pl.when(c == r)(lambda: relay_chunk(k).start())
            sibling_chunk(r, k, c).start()
            sibling_chunk(r, k, 1 - c).wait_recv()

        def own_piece(a, r):
            return remote(w_sems + 6 * (a - 1) + r, _half(gbufs[a], me, c), (*chips[r], c))

        def relay(a):
            return remote(w_sems + 6 * (a - 1) + 2, _half(gbufs[a], relayed_chip, c), relay_to)

        def to_sibling(a, r, core):
            return remote(w_sems + 6 * (a - 1) + 3 + r, _half(gbufs[a], chip_ids[r], core), sibling)

        def landed(a, r):
            return remote(w_sems + 6 * (a - 1) + r, _half(gbufs[a], chip_ids[r], c), (*chips[r], c))

        for u in range(N_STEPS_PROJ):
            @pl.when(t == u)
            def _(u=u):
                if u == 0:
                    for k in range(N_UNITS):
                        for r in range(2):
                            own_chunk(r, k).start()
                    for a in range(1, n):
                        for r in range(2):
                            own_piece(a, r).start()
                    fetch(0).start()
                if u < last:
                    if u + 1 >= N_UNITS:
                        make_available(u + 1)
                    fetch(u + 1).start()
                if u == late:
                    for a in range(1, n):
                        landed(a, 0).wait_recv()
                        landed(a, 1).wait_recv()
                        relay(a).start()
                        for r in range(2):
                            to_sibling(a, r, c).start()
                        for r in range(2):
                            to_sibling(a, r, 1 - c).wait_recv()
                fetch(u).wait()

        proj_ref[...] = jnp.dot(h_ref[...], wbuf[t % 2], preferred_element_type=F32).astype(BF16)

        @pl.when(t == last)
        def _():
            for a in range(1, n):
                landed(a, 2).wait_recv()
                to_sibling(a, 2, c).start()
                to_sibling(a, 2, 1 - c).wait_recv()
            for k in range(N_UNITS):
                for r in range(2):
                    own_chunk(r, k).wait_send()
                relay_chunk(k).wait_send()
                for r in range(3):
                    sibling_chunk(r, k, c).wait_send()
            for a in range(1, n):
                for r in range(2):
                    own_piece(a, r).wait_send()
                relay(a).wait_send()
                for r in range(3):
                    to_sibling(a, r, c).wait_send()

    def out_index(t, pos_ref):
        chip, unit = _proj_unit(t, pos_ref[1])
        return 0, chip * N_UNITS + unit

    any_spec = pl.BlockSpec(memory_space=pl.ANY)
    outs = pl.pallas_call(
        body, name="f1_proj",
        grid_spec=pltpu.PrefetchScalarGridSpec(
            num_scalar_prefetch=1, grid=(N_STEPS_PROJ,),
            in_specs=[pl.BlockSpec((tp, d), lambda t, pos_ref: (0, 0))] + [any_spec] * n,
            out_specs=[pl.BlockSpec((tp, cu), out_index)] + [any_spec] * n,
            scratch_shapes=[pltpu.VMEM((2, d, cu), BF16), pltpu.SemaphoreType.DMA((2,)),
                            pltpu.SemaphoreType.DMA((w_sems + 6 * (n - 1),)),
                            pltpu.SemaphoreType.DMA((w_sems + 6 * (n - 1),))]),
        out_shape=[jax.ShapeDtypeStruct((tp, nsh * sw), BF16)]
        + [jax.ShapeDtypeStruct(b.shape, b.dtype) for b in bufs],
        input_output_aliases={2 + a: 1 + a for a in range(n)},
        compiler_params=_params(("arbitrary",)),
    )(pos, h, *bufs)
    return outs[0], outs[1:]


def _dh_bwd(dproj, wg_in, x, front, ds2, norm_g, part):
    seq, d = x.shape
    tp = seq + TM
    _, nsh, _, sw = wg_in.shape
    tmb = tp // N_ROW_TILES_BIG
    tail = tmb - TM
    last = N_ROW_TILES_BIG - 1

    def body(dp_ref, w_hbm, x_ref, front_ref, ds2_ref, g_ref, part_ref, gx_hbm, dfront_ref, dng_ref, land_ref,
             wbuf, gacc, dsbuf, wsem, osems, send_sems, recv_sems):
        exchange = _chip_exchange(part_ref, land_ref, send_sems, recv_sems)
        i = pl.program_id(0)

        def x_rows_out(step):
            return pltpu.make_async_copy(dsbuf.at[step % 2], gx_hbm.at[pl.ds(step * tmb, tmb), :], osems.at[step % 2])

        last_out = pltpu.make_async_copy(dsbuf.at[last % 2, pl.ds(0, tail), :],
                                         gx_hbm.at[pl.ds(last * tmb, tail), :], osems.at[last % 2])

        @pl.when(i == 0)
        def _():
            exchange.start()
            gacc[...] = jnp.zeros_like(gacc)
            whole = pltpu.make_async_copy(w_hbm.at[0], wbuf, wsem)
            whole.start()
            whole.wait()

        dh = None
        for j in range(nsh):
            part = lax.dot_general(dp_ref[:, j * sw:(j + 1) * sw], wbuf[j], (((1,), (1,)), ((), ())),
                                   preferred_element_type=F32)
            dh = part if dh is None else dh + part
        s = _big_row_tile(x_ref, front_ref, i)
        r = lax.rsqrt(jnp.mean(s * s, axis=-1, keepdims=True) + EPS)
        gacc[...] += (dh * s * r).reshape(tmb // 8, 8, d).sum(axis=0)
        t = dh * g_ref[...]

        @pl.when(i >= 2)
        def _():
            x_rows_out(i - 2).wait()

        dsbuf[i % 2] = ds2_ref[...] + r * t - s * (r * r * r) * jnp.mean(t * s, axis=-1, keepdims=True)

        @pl.when(i < last)
        def _():
            x_rows_out(i).start()

        @pl.when(i == last)
        def _():
            last_out.start()
            dfront_ref[...] = dsbuf[last % 2, tail:, :]
            dng_ref[...] = jnp.broadcast_to(jnp.sum(gacc[...], axis=0, keepdims=True), (8, d))
            exchange.finish()
            x_rows_out(last - 1).wait()
            last_out.wait()

    any_spec = pl.BlockSpec(memory_space=pl.ANY)
    return pl.pallas_call(
        body, name="b2_dh", grid=(N_ROW_TILES_BIG,),
        in_specs=[pl.BlockSpec((tmb, nsh * sw), lambda i: (i, 0)), any_spec,
                  _big_row_spec(seq, tmb, d, lambda i: i),
                  pl.BlockSpec((TM, d), lambda i: (0, 0)),
                  pl.BlockSpec((tmb, d), lambda i: (i, 0)),
                  pl.BlockSpec((1, d), lambda i: (0, 0)), any_spec],
        out_specs=[any_spec, pl.BlockSpec((TM, d), lambda i: (0, 0)),
                   pl.BlockSpec((8, d), lambda i: (0, 0)), any_spec],
        out_shape=[jax.ShapeDtypeStruct((seq, d), F32), jax.ShapeDtypeStruct((TM, d), F32),
                   jax.ShapeDtypeStruct((8, d), F32), jax.ShapeDtypeStruct(part.shape, part.dtype)],
        scratch_shapes=[pltpu.VMEM((nsh, d, sw), BF16), pltpu.VMEM((8, d), F32), pltpu.VMEM((2, tmb, d), F32),
                        pltpu.SemaphoreType.DMA, pltpu.SemaphoreType.DMA((2,)),
                        pltpu.SemaphoreType.DMA((3,)), pltpu.SemaphoreType.DMA((3,))],
        compiler_params=pltpu.CompilerParams(dimension_semantics=("arbitrary",),
                                             vmem_limit_bytes=VMEM_LIMIT_BIG),
    )(dproj, wg_in, x, front, ds2, norm_g, part)


def _col_block(width, cap):
    return max(b for b in range(128, cap + 1, 128) if width % b == 0)


def _dw_reduced(lhs_t, rhs, cw, nblk, operands, groups, out_dims, out_block, out_index, carried, name):
    na, d, tp = lhs_t.shape
    rg = d // groups
    hh = rg // 2
    nc = len(carried)

    def body(*refs):
        l_ref, r_ref = refs[:2]
        part_refs = refs[2:2 + nc]
        p32_ref, pbf_ref = refs[2 + nc:4 + nc]
        land_refs = refs[4 + nc:4 + 2 * nc]
        res, rbuf, send_sems, recv_sems = refs[4 + 2 * nc:8 + 2 * nc]
        xsems = refs[8 + 2 * nc:]
        exchanges = [_chip_exchange(part_refs[e], land_refs[e], xsems[2 * e], xsems[2 * e + 1]) for e in range(nc)]
        exchange = _Exchange([s for ex in exchanges for s in ex.sends], [r for ex in exchanges for r in ex.recvs])
        x, y, c = _mesh_pos()
        t = pl.program_id(0)
        u = jnp.maximum(t - 1, 0)

        def to_sibling(blk):
            return pltpu.make_async_remote_copy(
                src_ref=res.at[blk % 2, :, pl.ds((1 - c) * hh, hh), :], dst_ref=rbuf.at[blk % 2],
                send_sem=send_sems.at[blk], recv_sem=recv_sems.at[blk],
                device_id=(x, y, 1 - c), device_id_type=MESH)

        @pl.when(t == 0)
        def _():
            exchange.start()

        @pl.when(t < nblk)
        def _():
            res[t % 2] = jnp.dot(l_ref[...], r_ref[...], preferred_element_type=F32).reshape(groups, rg, cw)

        @pl.when(t >= 1)
        def _():
            to_sibling(u).wait_recv()
            p = res[u % 2, :, pl.ds(c * hh, hh), :] + rbuf[u % 2]
            p32_ref[...] = p.reshape(p32_ref.shape)
            pbf_ref[...] = p.reshape(pbf_ref.shape).astype(BF16)

        @pl.when(t < nblk)
        def _():
            to_sibling(t).start()

        @pl.when(t >= 1)
        def _():
            to_sibling(u).wait_send()

        @pl.when(t == nblk)
        def _():
            exchange.finish()

    any_spec = pl.BlockSpec(memory_space=pl.ANY)
    last = nblk - 1
    out_spec = pl.BlockSpec(out_block, lambda t: out_index(jnp.maximum(t - 1, 0)))
    outs = pl.pallas_call(
        body, name=name, grid=(nblk + 1,),
        in_specs=[pl.BlockSpec((None, d, tp), lambda t: (operands(jnp.minimum(t, last))[0], 0, 0)),
                  pl.BlockSpec((None, tp, cw), lambda t: (operands(jnp.minimum(t, last))[0], 0,
                                                          operands(jnp.minimum(t, last))[1]))]
        + [any_spec] * nc,
        out_specs=[out_spec, out_spec] + [any_spec] * nc,
        out_shape=[jax.ShapeDtypeStruct(out_dims, F32), jax.ShapeDtypeStruct(out_dims, BF16)]
        + [jax.ShapeDtypeStruct(e.shape, e.dtype) for e in carried],
        scratch_shapes=[pltpu.VMEM((2, groups, rg, cw), F32), pltpu.VMEM((2, groups, hh, cw), F32),
                        pltpu.SemaphoreType.DMA((nblk,)), pltpu.SemaphoreType.DMA((nblk,))]
        + [pltpu.SemaphoreType.DMA((3,)), pltpu.SemaphoreType.DMA((3,))] * nc,
        compiler_params=_params(("arbitrary",)),
    )(lhs_t, rhs, *carried)
    return outs[0], outs[1], outs[2:]


def _conv_a_taps(first_lag, last_lag):
    out = []
    for r in range(8):
        taps = [(q, 8 * q + r) for q in range(5) if first_lag <= 8 * q + r <= last_lag]
        if taps:
            out.append((r, taps))
    return out


def _tile_block(i, nt):
    return jnp.where(i == 0, nt - 1, i - 1)


def _mix_fwd(x, front, proj, target, w3, wa, wb, conv_a_b, ln_g, ln_b, b_a_out, final_g, h):
    seq, d = x.shape
    tp = seq + TM
    nt = tp // TM
    RB = RB_FWD
    nrb = TM // RB
    shl = TM + SHIFT_ROWS

    def body(x_ref, front_ref, proj_ref, tgt_ref, w3_ref, wa_ref, wb_ref, cab_ref, lng_ref, lnb_ref, bao_ref, fg_ref,
             h_ref, ca_ref, cb_ref, ya_ref, yb_ref, abmt_ref, ds2_ref, ht_ref, loss_ref, dfg_ref,
             abm_ref, ext_a, ext_b, sh, s2_s, lacc, gacc):
        i = pl.program_id(0)

        def split(k, rows):
            return proj_ref[rows, k * d:(k + 1) * d].astype(F32)

        def s_tile():
            return jnp.where(i == 0, front_ref[...], x_ref[...])

        ht_ref[...] = h_ref[...].T

        @pl.when(i == 0)
        def _():
            ext_a[0:HALO_A, :] = jnp.zeros((HALO_A, d), F32)
            ext_b[0:HALO_B, :] = jnp.zeros((HALO_B, d), F32)
            lacc[...] = jnp.zeros_like(lacc)
            gacc[...] = jnp.zeros_like(gacc)

        def conv_in(rb, carry):
            rows = _rows(rb, RB)
            ua0 = split(0, rows) * _sigmoid(split(1, rows))
            ext_a[pl.ds(pl.multiple_of(HALO_A + rb * RB, 8), RB), :] = ua0
            ext_b[pl.ds(pl.multiple_of(HALO_B + rb * RB, 8), RB), :] = split(4, rows) * split(5, rows)
            ca_ref[rows, :] = jnp.broadcast_to(cab_ref[...], (RB, d))
            return carry
        lax.fori_loop(0, nrb, conv_in, 0)

        @pl.when(i == 0)
        def _():
            abmt_ref[...] = jnp.zeros_like(abmt_ref)
            ds2_ref[...] = jnp.zeros_like(ds2_ref)

        @pl.when(i > 0)
        def _():
            tile_after_conv_inputs(split, s_tile, tgt_ref, w3_ref, wa_ref, wb_ref, lng_ref, lnb_ref, bao_ref, fg_ref,
                                   ca_ref, cb_ref, ya_ref, yb_ref, abmt_ref, ds2_ref, abm_ref, ext_a, ext_b, sh,
                                   s2_s, lacc, gacc)

        ext_a[0:HALO_A, :] = ext_a[TM:TM + HALO_A, :]
        ext_b[0:HALO_B, :] = ext_b[TM:TM + HALO_B, :]

        @pl.when(i == nt - 1)
        def _():
            loss_ref[...] = jnp.broadcast_to(0.5 * jnp.sum(lacc[...]) * (1.0 / d), (8, 128))
            dfg_ref[...] = jnp.broadcast_to(jnp.sum(gacc[...], axis=0, keepdims=True), (8, d))

    def tile_after_conv_inputs(split, s_tile, tgt_ref, w3_ref, wa_ref, wb_ref, lng_ref, lnb_ref, bao_ref, fg_ref,
                               ca_ref, cb_ref, ya_ref, yb_ref, abmt_ref, ds2_ref, abm_ref, ext_a, ext_b, sh, s2_s,
                               lacc, gacc):
        for r, taps in _conv_a_taps(HALO_A - CONV_A + 1, HALO_A):
            if r == 0:
                src = ext_a
            else:
                sh[...] = ext_a[r:r + shl, :]
                src = sh

            def conv_acc(rb, carry, src=src, taps=taps):
                rows = _rows(rb, RB)
                acc = ca_ref[rows, :]
                for q, lag in taps:
                    k = lag - (HALO_A - CONV_A + 1)
                    acc = acc + src[pl.ds(pl.multiple_of(rb * RB + 8 * q, 8), RB), :] * wa_ref[k:k + 1, :]
                ca_ref[rows, :] = acc
                return carry
            lax.fori_loop(0, nrb, conv_acc, 0)

        cb_ref[...] = ext_b[HALO_B:HALO_B + TM, :] * wb_ref[2:3, :]
        for k in range(CONV_B - 1):
            off = HALO_B - CONV_B + 1 + k
            sh[0:TM, :] = ext_b[off:off + TM, :]
            cb_ref[...] += sh[0:TM, :] * wb_ref[k:k + 1, :]

        def branches(rb, carry):
            rows = _rows(rb, RB)
            ca = ca_ref[rows, :]
            mu = jnp.mean(ca, axis=-1, keepdims=True)
            xc = ca - mu
            rstd = lax.rsqrt(jnp.mean(xc * xc, axis=-1, keepdims=True) + EPS)
            ln = xc * rstd * lng_ref[...] + lnb_ref[...]
            ua = ln * _sigmoid(ln)
            a_z = split(2, rows)
            abm_ref[0, rows, :] = (ua * (a_z * _sigmoid(a_z))).astype(BF16)
            return carry
        lax.fori_loop(0, nrb, branches, 0)

        def branch_b(rb, carry):
            rows = _rows(rb, RB)
            b_z = split(6, rows)
            ub = split(3, rows) * cb_ref[rows, :]
            abm_ref[1, rows, :] = (ub * (b_z * _sigmoid(b_z))).astype(BF16)
            return carry
        lax.fori_loop(0, nrb, branch_b, 0)

        ya_ref[...] = jnp.dot(abm_ref[0], w3_ref[0], preferred_element_type=F32) + bao_ref[...]
        yb_ref[...] = jnp.dot(abm_ref[1], w3_ref[1], preferred_element_type=F32)

        def merge(rb, carry):
            rows = _rows(rb, RB)
            m = _sigmoid(split(7, rows)) * ya_ref[rows, :] + _sigmoid(split(8, rows)) * yb_ref[rows, :]
            abm_ref[2, rows, :] = m.astype(BF16)
            return carry
        lax.fori_loop(0, nrb, merge, 0)

        s2_s[...] = s_tile() + jnp.dot(abm_ref[2], w3_ref[2], preferred_element_type=F32)
        for k in range(3):
            abmt_ref[k] = abm_ref[k].T

        def head(rb, carry):
            rows = _rows(rb, RB)
            s2 = s2_s[rows, :]
            r2 = lax.rsqrt(jnp.mean(s2 * s2, axis=-1, keepdims=True) + EPS)
            diff = s2 * r2 * fg_ref[...] - tgt_ref[rows, :]
            lacc[...] += diff * diff
            dy = diff * (1.0 / d)
            gacc[...] += (dy * s2 * r2).reshape(RB // 8, 8, d).sum(axis=0)
            t = dy * fg_ref[...]
            ds2_ref[rows, :] = r2 * t - s2 * (r2 * r2 * r2) * jnp.mean(t * s2, axis=-1, keepdims=True)
            return carry
        lax.fori_loop(0, nrb, head, 0)

    row_f32 = pl.BlockSpec((TM, d), lambda i: (_tile_block(i, nt), 0))
    x_rows = pl.BlockSpec((TM, d), lambda i: (jnp.maximum(i - 1, 0), 0))
    const = lambda shape: pl.BlockSpec(shape, lambda i: (0,) * len(shape))
    return pl.pallas_call(
        body, name="f2_mix", grid=(nt,),
        in_specs=[x_rows, const((TM, d)),
                  pl.BlockSpec((TM, N_SPLIT * d), lambda i: (_tile_block(i, nt), 0)),
                  x_rows,
                  const((3, d, d)), const(wa.shape), const(wb.shape)] + [const((1, d))] * 5
        + [pl.BlockSpec((TM, d), lambda i: (_tile_block(i, nt), 0))],
        out_specs=[row_f32, row_f32, row_f32, row_f32,
                   pl.BlockSpec((3, d, TM), lambda i: (0, 0, _tile_block(i, nt))),
                   row_f32, pl.BlockSpec((d, TM), lambda i: (0, _tile_block(i, nt))),
                   const((8, 128)), const((8, d))],
        out_shape=[jax.ShapeDtypeStruct((tp, d), F32)] * 4
        + [jax.ShapeDtypeStruct((3, d, tp), BF16), jax.ShapeDtypeStruct((tp, d), F32),
           jax.ShapeDtypeStruct((d, tp), BF16),
           jax.ShapeDtypeStruct((8, 128), F32), jax.ShapeDtypeStruct((8, d), F32)],
        scratch_shapes=[pltpu.VMEM((3, TM, d), BF16),
                        pltpu.VMEM((HALO_A + TM, d), F32), pltpu.VMEM((HALO_B + TM, d), F32),
                        pltpu.VMEM((shl, d), F32), pltpu.VMEM((TM, d), F32),
                        pltpu.VMEM((RB, d), F32), pltpu.VMEM((8, d), F32)],
        compiler_params=_params(("arbitrary",)),
    )(x, front, proj, target, w3, wa, wb, conv_a_b, ln_g, ln_b, b_a_out, final_g, h)


def _mix_bwd(ds2, proj, ca, cb, ya, yb, w3, wa, wb, ln_g, ln_b):
    tp, d = ds2.shape
    nt = tp // TM
    RB = RB_BWD
    nrb = TM // RB
    strips = [slice(c0, c0 + STRIP_COLS) for c0 in range(0, d, STRIP_COLS)]
    shl = TM + SHIFT_ROWS
    nt_dims = (((1,), (1,)), ((), ()))

    def body(ds2_ref, proj_ref, ca_ref, cb_ref, ya_ref, yb_ref, w3_ref, wa_ref, wb_ref, lng_ref, lnb_ref,
             dproj_ref, d3_ref, sm_ref, ext_d, ext_e, sh, dm_s, dpa_s, dpb_s, dua0_s, acc):
        step = pl.program_id(0)

        def split(k, rows, cols=slice(0, d)):
            return proj_ref[rows, k * d + cols.start:k * d + cols.stop].astype(F32)

        def put(k, rows, val, cols=slice(0, d)):
            dproj_ref[rows, k * d + cols.start:k * d + cols.stop] = val.astype(BF16)

        def accum(row, val, cols=slice(0, d)):
            acc[row, :, cols] += val.reshape(RB // 8, 8, val.shape[-1]).sum(axis=0)

        @pl.when(step == 0)
        def _():
            ext_d[TM:TM + HALO_A, :] = jnp.zeros((HALO_A, d), F32)
            ext_e[TM:TM + HALO_B, :] = jnp.zeros((HALO_B, d), F32)
            acc[...] = jnp.zeros_like(acc)

        front = step == nt - 1

        @pl.when(front)
        def _():
            d3_ref[...] = jnp.zeros_like(d3_ref)

            def conv_only(rb, carry):
                rows = _rows(rb, RB)
                zeros = jnp.zeros((RB, d), F32)
                for k in (2, 3, 6, 7, 8):
                    put(k, rows, zeros)
                ext_d[rows, :] = zeros
                ext_e[rows, :] = zeros
                dua0_s[rows, :] = zeros
                dm_s[rows, :] = split(0, rows) * _sigmoid(split(1, rows))
                return carry
            lax.fori_loop(0, nrb, conv_only, 0)

        @pl.when(jnp.logical_not(front))
        def _():
            tile_to_conv_outputs(split, put, accum, ds2_ref, ca_ref, cb_ref, ya_ref, yb_ref, w3_ref, lng_ref, lnb_ref,
                                 d3_ref, ext_d, ext_e, dm_s, dpa_s, dpb_s, dua0_s)

        tile_conv_transposes(split, put, accum, wa_ref, wb_ref, ext_d, ext_e, sh, dm_s, dpb_s, dua0_s)

        @pl.when(front)
        def _():
            for row in range(SM_ROWS):
                sm_ref[row:row + 1, :] = jnp.sum(acc[row], axis=0, keepdims=True)

    def tile_to_conv_outputs(split, put, accum, ds2_ref, ca_ref, cb_ref, ya_ref, yb_ref, w3_ref, lng_ref, lnb_ref,
                             d3_ref, ext_d, ext_e, dm_s, dpa_s, dpb_s, dua0_s):
        d3_ref[2] = ds2_ref[...].astype(BF16)
        dm_s[...] = lax.dot_general(d3_ref[2], w3_ref[2], nt_dims, preferred_element_type=F32)

        def gates(rb, carry):
            rows = _rows(rb, RB)
            for cols in strips:
                dm = dm_s[rows, cols]
                sa = _sigmoid(split(7, rows, cols))
                sb = _sigmoid(split(8, rows, cols))
                put(7, rows, dm * ya_ref[rows, cols] * sa * (1.0 - sa), cols)
                put(8, rows, dm * yb_ref[rows, cols] * sb * (1.0 - sb), cols)
                dya = dm * sa
                accum(ROW_DBAO, dya, cols)
                d3_ref[0, rows, cols] = dya.astype(BF16)
                d3_ref[1, rows, cols] = (dm * sb).astype(BF16)
            return carry
        lax.fori_loop(0, nrb, gates, 0)

        dpa_s[...] = lax.dot_general(d3_ref[0], w3_ref[0], nt_dims, preferred_element_type=F32)
        dpb_s[...] = lax.dot_general(d3_ref[1], w3_ref[1], nt_dims, preferred_element_type=F32)

        def branch_a(rb, carry):
            rows = _rows(rb, RB)

            def row_mean(strip_fn):
                total = strip_fn(strips[0])
                for cols in strips[1:]:
                    total = total + strip_fn(cols)
                return jnp.sum(total, axis=-1, keepdims=True) * (1.0 / d)

            mu = row_mean(lambda cols: ca_ref[rows, cols])
            rstd = lax.rsqrt(row_mean(lambda cols: jnp.square(ca_ref[rows, cols] - mu)) + EPS)
            sum_dxh = jnp.zeros((RB, STRIP_COLS), F32)
            sum_dxh_xhat = jnp.zeros((RB, STRIP_COLS), F32)
            for cols in strips:
                xhat = (ca_ref[rows, cols] - mu) * rstd
                ln = xhat * lng_ref[:, cols] + lnb_ref[:, cols]
                sl = _sigmoid(ln)
                ua = ln * sl
                a_z = split(2, rows, cols)
                sz = _sigmoid(a_z)
                dpa = dpa_s[rows, cols]
                put(2, rows, dpa * ua * (sz * (1.0 + a_z * (1.0 - sz))), cols)
                dln = dpa * (a_z * sz) * (sl * (1.0 + ln * (1.0 - sl)))
                accum(ROW_DLNG, dln * xhat, cols)
                accum(ROW_DLNB, dln, cols)
                dxh = dln * lng_ref[:, cols]
                sum_dxh = sum_dxh + dxh
                sum_dxh_xhat = sum_dxh_xhat + dxh * xhat
                ext_d[rows, cols] = dxh
                dpa_s[rows, cols] = xhat
            mean_dxh = jnp.sum(sum_dxh, axis=-1, keepdims=True) * (1.0 / d)
            mean_dxh_xhat = jnp.sum(sum_dxh_xhat, axis=-1, keepdims=True) * (1.0 / d)
            for cols in strips:
                dca = rstd * (ext_d[rows, cols] - mean_dxh - dpa_s[rows, cols] * mean_dxh_xhat)
                accum(ROW_DCAB, dca, cols)
                ext_d[rows, cols] = dca
            return carry
        lax.fori_loop(0, nrb, branch_a, 0)

        def branch_b(rb, carry):
            rows = _rows(rb, RB)
            dua0_s[rows, :] = jnp.zeros((RB, d), F32)
            for cols in strips:
                dm_s[rows, cols] = split(0, rows, cols) * _sigmoid(split(1, rows, cols))
                b_z = split(6, rows, cols)
                szb = _sigmoid(b_z)
                dpb = dpb_s[rows, cols]
                b_b = split(3, rows, cols)
                cb_v = cb_ref[rows, cols]
                put(6, rows, dpb * (b_b * cb_v) * (szb * (1.0 + b_z * (1.0 - szb))), cols)
                dub = dpb * (b_z * szb)
                put(3, rows, dub * cb_v, cols)
                ext_e[rows, cols] = dub * b_b
            return carry
        lax.fori_loop(0, nrb, branch_b, 0)

    def tile_conv_transposes(split, put, accum, wa_ref, wb_ref, ext_d, ext_e, sh, dm_s, dpb_s, dua0_s):
        for r, taps in _conv_a_taps(0, CONV_A - 1):
            if r == 0:
                src = ext_d
            else:
                sh[...] = ext_d[r:r + shl, :]
                src = sh

            def conv_t(rb, carry, src=src, taps=taps):
                rows = _rows(rb, RB)
                for cols in strips:
                    ua0 = dm_s[rows, cols]
                    dua0 = dua0_s[rows, cols]
                    for q, lag in taps:
                        k = CONV_A - 1 - lag
                        slab = src[pl.ds(pl.multiple_of(rb * RB + 8 * q, 8), RB), cols]
                        dua0 = dua0 + slab * wa_ref[k:k + 1, cols]
                        accum(ROW_DWA + k, slab * ua0, cols)
                    dua0_s[rows, cols] = dua0
                return carry
            lax.fori_loop(0, nrb, conv_t, 0)
        ext_d[TM:TM + HALO_A, :] = ext_d[0:HALO_A, :]

        dpb_s[...] = ext_e[0:TM, :] * wb_ref[CONV_B - 1:CONV_B, :]
        for lag in range(CONV_B):
            k = CONV_B - 1 - lag
            if lag > 0:
                sh[0:TM, :] = ext_e[lag:lag + TM, :]
                dpb_s[...] += sh[0:TM, :] * wb_ref[k:k + 1, :]
            src = ext_e if lag == 0 else sh

            def conv_b_w(rb, carry, src=src, k=k):
                rows = _rows(rb, RB)
                accum(ROW_DWB + k, src[rows, :] * (split(4, rows) * split(5, rows)))
                return carry
            lax.fori_loop(0, nrb, conv_b_w, 0)
        ext_e[TM:TM + HALO_B, :] = ext_e[0:HALO_B, :]

        def inputs(rb, carry):
            rows = _rows(rb, RB)
            for cols in strips:
                dua0 = dua0_s[rows, cols]
                a_val = split(0, rows, cols)
                sg = _sigmoid(split(1, rows, cols))
                put(0, rows, dua0 * sg, cols)
                put(1, rows, dua0 * a_val * sg * (1.0 - sg), cols)
                dcbin = dpb_s[rows, cols]
                put(4, rows, dcbin * split(5, rows, cols), cols)
                put(5, rows, dcbin * split(4, rows, cols), cols)
            return carry
        lax.fori_loop(0, nrb, inputs, 0)

    rev = lambda i: (_tile_block(nt - 1 - i, nt), 0)
    row_f32 = pl.BlockSpec((TM, d), rev)
    const = lambda shape: pl.BlockSpec(shape, lambda i: (0,) * len(shape))
    return pl.pallas_call(
        body, name="b1_mix", grid=(nt,),
        in_specs=[row_f32, pl.BlockSpec((TM, N_SPLIT * d), rev), row_f32, row_f32, row_f32, row_f32,
                  const((3, d, d)), const(wa.shape), const(wb.shape), const((1, d)), const((1, d))],
        out_specs=[pl.BlockSpec((TM, N_SPLIT * d), rev),
                   pl.BlockSpec((3, TM, d), lambda i: (0, _tile_block(nt - 1 - i, nt), 0)),
                   const((SM_ROWS, d))],
        out_shape=[jax.ShapeDtypeStruct((tp, N_SPLIT * d), BF16), jax.ShapeDtypeStruct((3, tp, d), BF16),
                   jax.ShapeDtypeStruct((SM_ROWS, d), F32)],
        scratch_shapes=[pltpu.VMEM((TM + HALO_A, d), F32), pltpu.VMEM((TM + HALO_B, d), F32),
                        pltpu.VMEM((shl, d), F32), pltpu.VMEM((TM, d), F32), pltpu.VMEM((TM, d), F32),
                        pltpu.VMEM((TM, d), F32), pltpu.VMEM((TM, d), F32),
                        pltpu.VMEM((SM_ROWS, 8, d), F32)],
        compiler_params=_params(("arbitrary",)),
    )(ds2, proj, ca, cb, ya, yb, w3, wa, wb, ln_g, ln_b)


def kernel(x, meta_tokens, norm_g, w_in, conv_a_w, conv_a_b, ln_a_g, ln_a_b, w_a_out, b_a_out, conv_b_w, w_b_out, w_out, final_g, loss_target, m_meta_tokens, m_norm_g, m_w_in, m_conv_a_w, m_conv_a_b, m_ln_a_g, m_ln_a_b, m_w_a_out, m_b_a_out, m_conv_b_w, m_w_b_out, m_w_out, m_final_g, v_meta_tokens, v_norm_g, v_w_in, v_conv_a_w, v_conv_a_b, v_ln_a_g, v_ln_a_b, v_w_a_out, v_b_a_out, v_conv_b_w, v_w_b_out, v_w_out, v_final_g):
    seq, d = x.shape[1], x.shape[2]
    dc = meta_tokens.shape[1]
    sw = w_in.shape[2]
    rsh = w_a_out.shape[1]
    xi, yi, ci = _mesh_pos()
    me = 2 * xi + yi
    pos = jnp.stack([ci, me]).astype(jnp.int32)

    conv_rows = HALO_A + HALO_B + 8
    convs = jnp.concatenate([
        jnp.pad(conv_a_w[0], ((0, HALO_A - CONV_A), (0, 0))),
        jnp.pad(conv_b_w[0], ((0, HALO_B - CONV_B), (0, 0))), jnp.zeros((8, dc), F32)], axis=0)[None]
    w3_own = jnp.stack([w_a_out[0], w_b_out[0], w_out[0]])
    fg2 = final_g.reshape(1, d)
    xs = x[0]

    h, front, placed = _h_prep(xs, meta_tokens, norm_g, [(w_in, BF16), (w3_own, BF16), (convs, F32)], pos)
    proj, (wg_in, wg3, convg) = _proj_fwd(h, placed, pos)
    w3 = wg3.reshape(3, N_CHIPS * rsh, d)
    convg = jnp.transpose(convg[0], (1, 0, 2)).reshape(conv_rows, N_CHIPS * dc)
    wa_full = convg[0:HALO_A]
    wb_full = convg[HALO_A:HALO_A + HALO_B]
    ca, cb, ya, yb, abm_t, ds2, h_t, loss8, dfg8 = _mix_fwd(
        xs, front, proj, loss_target[0], w3, wa_full, wb_full, conv_a_b, ln_a_g, ln_a_b, b_a_out, fg2, h)
    dproj, d3, sm = _mix_bwd(ds2, proj, ca, cb, ya, yb, w3, wa_full, wb_full, ln_a_g, ln_a_b)
    cw_sq = _col_block(d, 512)
    per_sq = d // cw_sq
    p32_sq, pbf_sq, _ = _dw_reduced(
        abm_t, d3, cw_sq, 3 * per_sq, lambda t: (t // per_sq, t % per_sq), N_CHIPS,
        (3, N_CHIPS, rsh // 2, d), (None, N_CHIPS, rsh // 2, cw_sq),
        lambda u: (u // per_sq, 0, 0, u % per_sq), [], "dw_square")
    cw_in = _col_block(sw, 768)
    ncol = sw // cw_in
    p32_in, pbf_in, (l_sq,) = _dw_reduced(
        h_t[None], dproj[None], cw_in, N_CHIPS * ncol, lambda t: (0, t), 1,
        (1, N_CHIPS, d // 2, sw), (None, None, d // 2, cw_in), lambda u: (0, u // ncol, 0, u % ncol),
        [pbf_sq], "dw_in")
    grad_x, dfront, dng8, l_in = _dh_bwd(dproj, wg_in, xs, front, ds2, norm_g, pbf_in)
    half_in = _sum_chips([(p32_in, l_in)], sw, pos, "rs_sum_in")
    half_sq = _sum_chips([(p32_sq, l_sq)], d, pos, "rs_sum_sq")
    tail_row = lax.broadcasted_iota(jnp.int32, (8, d), 0)
    tail = jnp.where(tail_row == 0, dng8, jnp.where(tail_row == 1, dfg8,
                     jnp.where(tail_row == 2, loss8[0, 0], 0.0)))
    block = jnp.concatenate([sm, dfront[TM - N_META:TM], tail], axis=0)
    (other_in, other_sq), red = _sibling_swap([half_in, half_sq], block)
    col = lax.dynamic_slice(red, (0, me * dc), (AR_ROWS, dc))
    g_small = {
        "meta_tokens": col[ROW_DMETA:ROW_DMETA + N_META],
        "norm_g": red[ROW_DNG:ROW_DNG + 1],
        "conv_a_w": col[ROW_DWA:ROW_DWA + CONV_A][None],
        "conv_a_b": red[ROW_DCAB:ROW_DCAB + 1],
        "ln_a_g": red[ROW_DLNG:ROW_DLNG + 1],
        "ln_a_b": red[ROW_DLNB:ROW_DLNB + 1],
        "b_a_out": red[ROW_DBAO:ROW_DBAO + 1],
        "conv_b_w": col[ROW_DWB:ROW_DWB + CONV_B][None],
        "final_g": red[ROW_DFG],
    }

    upd_in = _adam_halves([w_in], [m_w_in], [v_w_in], half_in, other_in, pos, "adam_in")
    upd_sq = _adam_halves([w_a_out, w_b_out, w_out], [m_w_a_out, m_w_b_out, m_w_out],
                          [v_w_a_out, v_w_b_out, v_w_out], half_sq, other_sq, pos, "adam_sq")
    small_w = {"meta_tokens": (meta_tokens, m_meta_tokens, v_meta_tokens), "norm_g": (norm_g, m_norm_g, v_norm_g),
               "conv_a_w": (conv_a_w, m_conv_a_w, v_conv_a_w), "conv_a_b": (conv_a_b, m_conv_a_b, v_conv_a_b),
               "ln_a_g": (ln_a_g, m_ln_a_g, v_ln_a_g), "ln_a_b": (ln_a_b, m_ln_a_b, v_ln_a_b),
               "b_a_out": (b_a_out, m_b_a_out, v_b_a_out), "conv_b_w": (conv_b_w, m_conv_b_w, v_conv_b_w),
               "final_g": (final_g, m_final_g, v_final_g)}
    names_small = list(small_w)
    as2d = lambda t: t.reshape(-1, t.shape[-1])
    upd_small = _adam_small([(as2d(small_w[k][0]), as2d(g_small[k]), as2d(small_w[k][1]), as2d(small_w[k][2]))
                             for k in names_small])

    grads, deltas, new_m, new_v = dict(g_small), {}, {}, {}
    for k, upd in zip(names_small, upd_small):
        deltas[k], new_m[k], new_v[k] = [t.reshape(small_w[k][0].shape) for t in upd]
    grads["w_in"], deltas["w_in"], new_m["w_in"], new_v["w_in"] = upd_in
    for idx, k in enumerate(["w_a_out", "w_b_out", "w_out"]):
        grads[k], deltas[k], new_m[k], new_v[k] = upd_sq[4 * idx:4 * idx + 4]

    loss = red[ROW_LOSS, 0]
    order = ["meta_tokens", "norm_g", "w_in", "conv_a_w", "conv_a_b", "ln_a_g", "ln_a_b", "w_a_out", "b_a_out",
             "conv_b_w", "w_b_out", "w_out", "final_g"]
    return (loss, grad_x[None], *[grads[k] for k in order], *[deltas[k] for k in order],
            *[new_m[k] for k in order], *[new_v[k] for k in order])
```

```python
import jax
import jax.numpy as jnp
from jax import lax
from jax.experimental import pallas as pl
from jax.experimental.pallas import tpu as pltpu

F32 = jnp.float32
BF16 = jnp.bfloat16
MESH = pl.DeviceIdType.MESH

EPS = 1e-6
N_META = 16
N_SPLIT = 9
CONV_A = 31
CONV_B = 3
HALO_A = 32
HALO_B = 8
SHIFT_ROWS = 24
TM = 256
RB_FWD = 256
RB_BWD = 64
STRIP_COLS = 256
N_ROW_TILES_BIG = 8
ROW_BLOCK = 256
N_CHIPS = 4
VMEM_LIMIT = 56 * 1024 * 1024
VMEM_LIMIT_BIG = 62 * 1024 * 1024

ADAM_LR = 0.001
ADAM_B1 = 0.9
ADAM_B2 = 0.999
ADAM_EPS = 1e-08
ADAM_WD = 0.01
ADAM_STEP = 10

ROW_DWA = 0
ROW_DWB = 32
ROW_DCAB = 40
ROW_DLNG = 41
ROW_DLNB = 42
ROW_DBAO = 43
SM_ROWS = 48
ROW_DMETA = 48
ROW_DNG = 64
ROW_DFG = 65
ROW_LOSS = 66
AR_ROWS = 72


def _sigmoid(v):
    return 0.5 * jnp.tanh(0.5 * v) + 0.5


def _params(sem, **kw):
    return pltpu.CompilerParams(dimension_semantics=sem, vmem_limit_bytes=VMEM_LIMIT, **kw)


def _rows(rb, n):
    return pl.ds(pl.multiple_of(rb * n, n), n)


def _mesh_pos():
    x, y, c = lax.axis_index("x"), lax.axis_index("y"), lax.axis_index("c")
    return x, y, c


def _half(ref, j, c):
    h = ref.shape[2] // 2
    return ref.at[:, j, pl.ds(c * h, h), :]


class _Exchange:
    def __init__(self, sends, recvs):
        self.sends, self.recvs = sends, recvs

    def start(self):
        for cp in self.sends:
            cp.start()

    def finish(self):
        for cp in self.recvs:
            cp.wait_recv()
        for cp in self.sends:
            cp.wait_send()


def _chip_exchange(part_ref, land_ref, send_sems, recv_sems):
    x, y, c = _mesh_pos()
    me = 2 * x + y
    sends, recvs = [], []
    for k, (px, py) in enumerate([(1 - x, y), (x, 1 - y), (1 - x, 1 - y)]):
        sems = dict(send_sem=send_sems.at[k], recv_sem=recv_sems.at[k], device_id=(px, py, c), device_id_type=MESH)
        sends.append(pltpu.make_async_remote_copy(
            src_ref=part_ref.at[:, 2 * px + py], dst_ref=land_ref.at[:, me], **sems))
        landed = land_ref.at[:, 2 * px + py]
        recvs.append(pltpu.make_async_remote_copy(src_ref=landed, dst_ref=landed, **sems))
    return _Exchange(sends, recvs)

def _sibling_swap(halves, small):
    n = len(halves)

    def body(*refs):
        ins, small_ref, outs, red_ref = refs[:n], refs[n], refs[n + 1:2 * n + 1], refs[2 * n + 1]
        send_sems, recv_sems = refs[2 * n + 2:2 * n + 4]
        reduce = _SmallAllReduce(small_ref, red_ref, *refs[2 * n + 4:])
        x, y, c = _mesh_pos()
        copies = [pltpu.make_async_remote_copy(
            src_ref=ins[a], dst_ref=outs[a], send_sem=send_sems.at[a], recv_sem=recv_sems.at[a],
            device_id=(x, y, 1 - c), device_id_type=MESH) for a in range(n)]
        reduce.start()
        for cp in copies:
            cp.start()
        reduce.between_chips()
        reduce.finish()
        for cp in copies:
            cp.wait()

    any_spec = pl.BlockSpec(memory_space=pl.ANY)
    vm = pl.BlockSpec(memory_space=pltpu.VMEM)
    outs = pl.pallas_call(
        body, name="rs_swap",
        in_specs=[any_spec] * n + [vm], out_specs=[any_spec] * n + [vm],
        out_shape=[jax.ShapeDtypeStruct(h.shape, h.dtype) for h in halves]
        + [jax.ShapeDtypeStruct(small.shape, F32)],
        scratch_shapes=[pltpu.SemaphoreType.DMA((n,)), pltpu.SemaphoreType.DMA((n,))]
        + _SmallAllReduce.scratch(*small.shape),
    )(*halves, small)
    return outs[:n], outs[n]


class _SmallAllReduce:
    def __init__(self, x_ref, out_ref, sib_ref, part_ref, peers_ref, send_sems, recv_sems):
        self.x_ref, self.out_ref, self.sib_ref, self.part_ref, self.peers_ref = x_ref, out_ref, sib_ref, part_ref, peers_ref
        x, y, c = _mesh_pos()
        self.me = 2 * x + y
        self.swap = pltpu.make_async_remote_copy(
            src_ref=x_ref, dst_ref=sib_ref, send_sem=send_sems.at[0], recv_sem=recv_sems.at[0],
            device_id=(x, y, 1 - c), device_id_type=MESH)
        self.sends, self.recvs = [], []
        for k, (px, py) in enumerate([(1 - x, y), (x, 1 - y), (1 - x, 1 - y)]):
            sems = dict(send_sem=send_sems.at[1 + k], recv_sem=recv_sems.at[1 + k],
                        device_id=(px, py, c), device_id_type=MESH)
            self.sends.append(pltpu.make_async_remote_copy(src_ref=part_ref, dst_ref=peers_ref.at[self.me], **sems))
            landed = peers_ref.at[2 * px + py]
            self.recvs.append(pltpu.make_async_remote_copy(src_ref=landed, dst_ref=landed, **sems))

    @staticmethod
    def scratch(rows, d):
        return [pltpu.VMEM((rows, d), F32), pltpu.VMEM((rows, d), F32), pltpu.VMEM((N_CHIPS, rows, d), F32),
                pltpu.SemaphoreType.DMA((4,)), pltpu.SemaphoreType.DMA((4,))]

    def start(self):
        self.swap.start()

    def between_chips(self):
        self.swap.wait()
        self.part_ref[...] = self.x_ref[...] + self.sib_ref[...]
        self.peers_ref[self.me] = self.part_ref[...]
        for cp in self.sends:
            cp.start()

    def finish(self):
        for cp in self.recvs:
            cp.wait_recv()
        for cp in self.sends:
            cp.wait_send()
        p = self.peers_ref
        self.out_ref[...] = ((p[0] + p[1]) + p[2]) + p[3]


def _sum_chips(parts, cw, pos, name):
    s, _, h, _ = parts[0][0].shape
    hb = min(h, ROW_BLOCK)
    widths = [own.shape[3] // cw for own, _ in parts]
    starts = [sum(widths[:a]) for a in range(len(parts))]

    def body(pos_ref, *refs):
        out_ref = refs[-1]
        n = pl.program_id(2)
        total = None
        for a in range(len(parts)):
            own, l1, l2, l3 = refs[4 * a:4 * a + 4]
            val = ((own[...] + l1[...].astype(F32)) + l2[...].astype(F32)) + l3[...].astype(F32)
            total = val if total is None else jnp.where(n >= starts[a], val, total)
        out_ref[...] = total

    def slot(a, k):
        col = lambda n: jnp.clip(n - starts[a], 0, widths[a] - 1)
        return pl.BlockSpec((None, None, hb, cw),
                            lambda si, b, n, pos_ref: (si, (pos_ref[1] + k) % N_CHIPS, b, col(n)))

    operands, specs = [], []
    for a, (own, landed) in enumerate(parts):
        operands += [own, landed, landed, landed]
        specs += [slot(a, 0), slot(a, 1), slot(a, 2), slot(a, 3)]
    return pl.pallas_call(
        body, name=name,
        grid_spec=pltpu.PrefetchScalarGridSpec(
            num_scalar_prefetch=1, grid=(s, h // hb, sum(widths)), in_specs=specs,
            out_specs=pl.BlockSpec((None, hb, cw), lambda si, b, n, pos_ref: (si, b, n))),
        out_shape=jax.ShapeDtypeStruct((s, h, sum(widths) * cw), F32),
        compiler_params=_params(("arbitrary",) * 3),
    )(pos, *operands)


def _adamw(w, g, m, v):
    m = ADAM_B1 * m + (1.0 - ADAM_B1) * g
    v = ADAM_B2 * v + (1.0 - ADAM_B2) * (g * g)
    m_hat = m / (1.0 - ADAM_B1 ** ADAM_STEP)
    v_hat = v / (1.0 - ADAM_B2 ** ADAM_STEP)
    delta = -ADAM_LR * (m_hat / (jnp.sqrt(v_hat) + ADAM_EPS) + ADAM_WD * w)
    return delta, m, v


def _adam_halves(ws, ms, vs, g_own, g_recv, pos, name):
    n = len(ws)
    _, r, c = ws[0].shape
    h = r // 2
    rb = min(h, ROW_BLOCK)
    nb = h // rb

    def body(pos_ref, *refs):
        w_refs, m_refs, v_refs = refs[:n], refs[n:2 * n], refs[2 * n:3 * n]
        go_ref, gr_ref = refs[3 * n:3 * n + 2]
        outs = refs[3 * n + 2:]
        mine = pl.program_id(0) == pos_ref[0]
        for a in range(n):
            g = jnp.where(mine, go_ref[a], gr_ref[a])
            delta, m, v = _adamw(w_refs[a][...], g, m_refs[a][...], v_refs[a][...])
            outs[4 * a][...], outs[4 * a + 1][...], outs[4 * a + 2][...], outs[4 * a + 3][...] = g, delta, m, v

    spec_w = pl.BlockSpec((None, rb, c), lambda hf, b, pos_ref: (0, hf * nb + b, 0))
    spec_g = pl.BlockSpec((n, rb, c), lambda hf, b, pos_ref: (0, b, 0))
    return pl.pallas_call(
        body, name=name,
        grid_spec=pltpu.PrefetchScalarGridSpec(
            num_scalar_prefetch=1, grid=(2, nb), in_specs=[spec_w] * (3 * n) + [spec_g] * 2,
            out_specs=[spec_w] * (4 * n)),
        out_shape=[jax.ShapeDtypeStruct((1, r, c), F32)] * (4 * n),
        compiler_params=_params(("arbitrary",) * 2),
    )(pos, *ws, *ms, *vs, g_own, g_recv)


def _adam_small(items):
    n = len(items)

    def body(*refs):
        ins, outs = refs[:4 * n], refs[4 * n:]
        for a in range(n):
            w_ref, g_ref, m_ref, v_ref = ins[4 * a:4 * a + 4]
            d, m, v = _adamw(w_ref[...], g_ref[...], m_ref[...], v_ref[...])
            outs[3 * a][...] = d
            outs[3 * a + 1][...] = m
            outs[3 * a + 2][...] = v

    vm = pl.BlockSpec(memory_space=pltpu.VMEM)
    flat = [t for it in items for t in it]
    outs = pl.pallas_call(
        body, name="adam_small", in_specs=[vm] * (4 * n), out_specs=[vm] * (3 * n),
        out_shape=[jax.ShapeDtypeStruct(it[0].shape, F32) for it in items for _ in range(3)],
    )(*flat)
    return [tuple(outs[3 * a:3 * a + 3]) for a in range(n)]


def _big_row_spec(seq, tmb, d, tile_of):
    return pl.BlockSpec((pl.Element(tmb), pl.Element(d)),
                        lambda *args: (pl.multiple_of(jnp.minimum(tile_of(*args) * tmb, seq - tmb), 8), 0))


def _big_row_tile(x_ref, front_ref, i):
    rows = x_ref[...]
    last = jnp.concatenate([rows[TM:], front_ref[...]], axis=0)
    return jnp.where(i == N_ROW_TILES_BIG - 1, last, rows)


def _h_prep(x, meta, norm_g, shards, pos):
    seq, d = x.shape
    tp = seq + TM
    dc = meta.shape[1]
    tmb = tp // N_ROW_TILES_BIG
    assert tmb >= TM and tp == tmb * N_ROW_TILES_BIG
    last = N_ROW_TILES_BIG - 1
    ns = len(shards)

    def body(pos_ref, x_ref, meta_ref, g_ref, *refs):
        shard_refs, (h_ref, front_ref), placed_refs = refs[:ns], refs[ns:ns + 2], refs[ns + 2:2 * ns + 2]
        metas, msend, mrecv = refs[2 * ns + 2:]
        for a in range(ns):
            placed_refs[a][...] = shard_refs[a][...].astype(placed_refs[a].dtype)
        x, y, c = _mesh_pos()
        me = 2 * x + y
        chips = [(1 - x, y), (x, 1 - y), (1 - x, 1 - y)]
        i = pl.program_id(0)

        def meta_copy(k, chip):
            return pltpu.make_async_remote_copy(
                src_ref=metas.at[chip], dst_ref=metas.at[chip], send_sem=msend.at[k], recv_sem=mrecv.at[k],
                device_id=(*chips[k], c), device_id_type=MESH)

        @pl.when(i == 0)
        def _():
            metas[me] = meta_ref[...]
            for k in range(3):
                meta_copy(k, me).start()
            front_ref[...] = jnp.zeros_like(front_ref)

        @pl.when(i == last)
        def _():
            for k, (px, py) in enumerate(chips):
                meta_copy(k, 2 * px + py).wait_recv()
            for q in range(N_CHIPS):
                front_ref[TM - N_META:TM, q * dc:(q + 1) * dc] = metas[q]

        s = _big_row_tile(x_ref, front_ref, i)
        r = lax.rsqrt(jnp.mean(s * s, axis=-1, keepdims=True) + EPS)
        h_ref[...] = (s * r * g_ref[...]).astype(BF16)

        @pl.when(i == last)
        def _():
            for k in range(3):
                meta_copy(k, me).wait_send()

    shard_in, shard_out, shard_shapes = [], [], []
    for arr, dtype in shards:
        s, r, c = arr.shape
        sliced = r % (N_ROW_TILES_BIG * 16) == 0
        rp = r // N_ROW_TILES_BIG if sliced else r
        step = (lambda i: i) if sliced else (lambda i: 0)
        shard_in.append(pl.BlockSpec((s, rp, c), lambda i, pos_ref, step=step: (0, step(i), 0)))
        shard_out.append(pl.BlockSpec((s, None, rp, c), lambda i, pos_ref, step=step: (0, pos_ref[1], step(i), 0)))
        shard_shapes.append(jax.ShapeDtypeStruct((s, N_CHIPS, r, c), dtype))
    outs = pl.pallas_call(
        body, name="f0_norm",
        grid_spec=pltpu.PrefetchScalarGridSpec(
            num_scalar_prefetch=1, grid=(N_ROW_TILES_BIG,),
            in_specs=[_big_row_spec(seq, tmb, d, lambda i, pos_ref: i),
                      pl.BlockSpec(meta.shape, lambda i, pos_ref: (0, 0)),
                      pl.BlockSpec((1, d), lambda i, pos_ref: (0, 0))] + shard_in,
            out_specs=[pl.BlockSpec((tmb, d), lambda i, pos_ref: (i, 0)),
                       pl.BlockSpec((TM, d), lambda i, pos_ref: (0, 0))] + shard_out,
            scratch_shapes=[pltpu.VMEM((N_CHIPS,) + meta.shape, F32),
                            pltpu.SemaphoreType.DMA((3,)), pltpu.SemaphoreType.DMA((3,))]),
        out_shape=[jax.ShapeDtypeStruct((tp, d), BF16), jax.ShapeDtypeStruct((TM, d), F32)] + shard_shapes,
        compiler_params=_params(("arbitrary",)),
    )(pos, x, meta, norm_g, *[arr for arr, _ in shards])
    return outs[0], outs[1], outs[2:]


N_UNITS = 3
N_STEPS_PROJ = N_CHIPS * N_UNITS


def _proj_plan(u):
    v = u - N_UNITS
    if v < 2 * N_UNITS:
        return v % 2, v // 2
    return 2, v - 2 * N_UNITS


def _proj_unit(t, me):
    v = t - N_UNITS
    near = v < 2 * N_UNITS
    rel = jnp.where(near, v % 2, 2)
    unit = jnp.where(near, v // 2, v - 2 * N_UNITS)
    flip = jnp.where(rel == 0, 2, jnp.where(rel == 1, 1, 3))
    own = t < N_UNITS
    return jnp.where(own, me, lax.bitwise_xor(me, flip)), jnp.where(own, t, unit)


def _proj_fwd(h, bufs, pos):
    tp, d = h.shape
    _, nsh, _, sw = bufs[0].shape
    cu = sw // N_UNITS
    assert cu % 128 == 0
    n = len(bufs)
    w_sems = 6 * N_UNITS
    last = N_STEPS_PROJ - 1
    late = N_STEPS_PROJ - N_UNITS

    def body(pos_ref, h_ref, *refs):
        proj_ref = refs[n]
        gbufs = refs[n + 1:2 * n + 1]
        wbuf, wsems, send_sems, recv_sems = refs[2 * n + 1:]
        x, y, c = _mesh_pos()
        me = 2 * x + y
        sibling = (x, y, 1 - c)
        chips = [(1 - x, y), (x, 1 - y), (1 - x, 1 - y)]
        chip_ids = [2 * px + py for px, py in chips]
        relayed_chip = jnp.where(c == 0, chip_ids[0], chip_ids[1])
        relay_to = (jnp.where(c == 0, x, 1 - x), jnp.where(c == 0, 1 - y, y), c)
        t = pl.program_id(0)

        def remote(idx, piece, to):
            return pltpu.make_async_remote_copy(
                src_ref=piece, dst_ref=piece, send_sem=send_sems.at[idx], recv_sem=recv_sems.at[idx],
                device_id=to, device_id_type=MESH)

        hr = d // 2

        def chunk_of(chip, half, k):
            return gbufs[0].at[0, chip, pl.ds(half * hr, hr), pl.ds(k * cu, cu)]

        def own_chunk(r, k):
            return remote(6 * k + r, chunk_of(me, c, k), (*chips[r], c))

        def landed_chunk(r, k):
            return remote(6 * k + r, chunk_of(chip_ids[r], c, k), (*chips[r], c))

        def relay_chunk(k):
            return remote(6 * k + 2, chunk_of(relayed_chip, c, k), relay_to)

        def sibling_chunk(r, k, half):
            return remote(6 * k + 3 + r, chunk_of(chip_ids[r], half, k), sibling)

        def fetch(u):
            chip, unit = _proj_unit(jnp.int32(u), me)
            return pltpu.make_async_copy(gbufs[0].at[0, chip, :, pl.ds(pl.multiple_of(unit * cu, 128), cu)],
                                         wbuf.at[u % 2], wsems.at[u % 2])

        def make_available(u):
            r, k = _proj_plan(u)
            landed_chunk(r, k).wait_recv()
            if r < 2:
                pl.when(c == r)(lambda: relay_chunk(k).start())
            sibling_chunk(r, k, c).start()
            sibling_chunk(r, k, 1 - c).wait_recv()

        def own_piece(a, r):
            return remote(w_sems + 6 * (a - 1) + r, _half(gbufs[a], me, c), (*chips[r], c))

        def relay(a):
            return remote(w_sems + 6 * (a - 1) + 2, _half(gbufs[a], relayed_chip, c), relay_to)

        def to_sibling(a, r, core):
            return remote(w_sems + 6 * (a - 1) + 3 + r, _half(gbufs[a], chip_ids[r], core), sibling)

        def landed(a, r):
            return remote(w_sems + 6 * (a - 1) + r, _half(gbufs[a], chip_ids[r], c), (*chips[r], c))

        for u in range(N_STEPS_PROJ):
            @pl.when(t == u)
            def _(u=u):
                if u == 0:
                    for k in range(N_UNITS):
                        for r in range(2):
                            own_chunk(r, k).start()
                    for a in range(1, n):
                        for r in range(2):
                            own_piece(a, r).start()
                    fetch(0).start()
                if u < last:
                    if u + 1 >= N_UNITS:
                        make_available(u + 1)
                    fetch(u + 1).start()
                if u == late:
                    for a in range(1, n):
                        landed(a, 0).wait_recv()
                        landed(a, 1).wait_recv()
                        relay(a).start()
                        for r in range(2):
                            to_sibling(a, r, c).start()
                        for r in range(2):
                            to_sibling(a, r, 1 - c).wait_recv()
                fetch(u).wait()

        proj_ref[...] = jnp.dot(h_ref[...], wbuf[t % 2], preferred_element_type=F32).astype(BF16)

        @pl.when(t == last)
        def _():
            for a in range(1, n):
                landed(a, 2).wait_recv()
                to_sibling(a, 2, c).start()
                to_sibling(a, 2, 1 - c).wait_recv()
            for k in range(N_UNITS):
                for r in range(2):
                    own_chunk(r, k).wait_send()
                relay_chunk(k).wait_send()
                for r in range(3):
                    sibling_chunk(r, k, c).wait_send()
            for a in range(1, n):
                for r in range(2):
                    own_piece(a, r).wait_send()
                relay(a).wait_send()
                for r in range(3):
                    to_sibling(a, r, c).wait_send()

    def out_index(t, pos_ref):
        chip, unit = _proj_unit(t, pos_ref[1])
        return 0, chip * N_UNITS + unit

    any_spec = pl.BlockSpec(memory_space=pl.ANY)
    outs = pl.pallas_call(
        body, name="f1_proj",
        grid_spec=pltpu.PrefetchScalarGridSpec(
            num_scalar_prefetch=1, grid=(N_STEPS_PROJ,),
            in_specs=[pl.BlockSpec((tp, d), lambda t, pos_ref: (0, 0))] + [any_spec] * n,
            out_specs=[pl.BlockSpec((tp, cu), out_index)] + [any_spec] * n,
            scratch_shapes=[pltpu.VMEM((2, d, cu), BF16), pltpu.SemaphoreType.DMA((2,)),
                            pltpu.SemaphoreType.DMA((w_sems + 6 * (n - 1),)),
                            pltpu.SemaphoreType.DMA((w_sems + 6 * (n - 1),))]),
        out_shape=[jax.ShapeDtypeStruct((tp, nsh * sw), BF16)]
        + [jax.ShapeDtypeStruct(b.shape, b.dtype) for b in bufs],
        input_output_aliases={2 + a: 1 + a for a in range(n)},
        compiler_params=_params(("arbitrary",)),
    )(pos, h, *bufs)
    return outs[0], outs[1:]


def _dh_bwd(dproj, wg_in, x, front, ds2, norm_g, part):
    seq, d = x.shape
    tp = seq + TM
    _, nsh, _, sw = wg_in.shape
    tmb = tp // N_ROW_TILES_BIG
    tail = tmb - TM
    last = N_ROW_TILES_BIG - 1

    def body(dp_ref, w_hbm, x_ref, front_ref, ds2_ref, g_ref, part_ref, gx_hbm, dfront_ref, dng_ref, land_ref,
             wbuf, gacc, dsbuf, wsem, osems, send_sems, recv_sems):
        exchange = _chip_exchange(part_ref, land_ref, send_sems, recv_sems)
        i = pl.program_id(0)

        def x_rows_out(step):
            return pltpu.make_async_copy(dsbuf.at[step % 2], gx_hbm.at[pl.ds(step * tmb, tmb), :], osems.at[step % 2])

        last_out = pltpu.make_async_copy(dsbuf.at[last % 2, pl.ds(0, tail), :],
                                         gx_hbm.at[pl.ds(last * tmb, tail), :], osems.at[last % 2])

        @pl.when(i == 0)
        def _():
            exchange.start()
            gacc[...] = jnp.zeros_like(gacc)
            whole = pltpu.make_async_copy(w_hbm.at[0], wbuf, wsem)
            whole.start()
            whole.wait()

        dh = None
        for j in range(nsh):
            part = lax.dot_general(dp_ref[:, j * sw:(j + 1) * sw], wbuf[j], (((1,), (1,)), ((), ())),
                                   preferred_element_type=F32)
            dh = part if dh is None else dh + part
        s = _big_row_tile(x_ref, front_ref, i)
        r = lax.rsqrt(jnp.mean(s * s, axis=-1, keepdims=True) + EPS)
        gacc[...] += (dh * s * r).reshape(tmb // 8, 8, d).sum(axis=0)
        t = dh * g_ref[...]

        @pl.when(i >= 2)
        def _():
            x_rows_out(i - 2).wait()

        dsbuf[i % 2] = ds2_ref[...] + r * t - s * (r * r * r) * jnp.mean(t * s, axis=-1, keepdims=True)

        @pl.when(i < last)
        def _():
            x_rows_out(i).start()

        @pl.when(i == last)
        def _():
            last_out.start()
            dfront_ref[...] = dsbuf[last % 2, tail:, :]
            dng_ref[...] = jnp.broadcast_to(jnp.sum(gacc[...], axis=0, keepdims=True), (8, d))
            exchange.finish()
            x_rows_out(last - 1).wait()
            last_out.wait()

    any_spec = pl.BlockSpec(memory_space=pl.ANY)
    return pl.pallas_call(
        body, name="b2_dh", grid=(N_ROW_TILES_BIG,),
        in_specs=[pl.BlockSpec((tmb, nsh * sw), lambda i: (i, 0)), any_spec,
                  _big_row_spec(seq, tmb, d, lambda i: i),
                  pl.BlockSpec((TM, d), lambda i: (0, 0)),
                  pl.BlockSpec((tmb, d), lambda i: (i, 0)),
                  pl.BlockSpec((1, d), lambda i: (0, 0)), any_spec],
        out_specs=[any_spec, pl.BlockSpec((TM, d), lambda i: (0, 0)),
                   pl.BlockSpec((8, d), lambda i: (0, 0)), any_spec],
        out_shape=[jax.ShapeDtypeStruct((seq, d), F32), jax.ShapeDtypeStruct((TM, d), F32),
                   jax.ShapeDtypeStruct((8, d), F32), jax.ShapeDtypeStruct(part.shape, part.dtype)],
        scratch_shapes=[pltpu.VMEM((nsh, d, sw), BF16), pltpu.VMEM((8, d), F32), pltpu.VMEM((2, tmb, d), F32),
                        pltpu.SemaphoreType.DMA, pltpu.SemaphoreType.DMA((2,)),
                        pltpu.SemaphoreType.DMA((3,)), pltpu.SemaphoreType.DMA((3,))],
        compiler_params=pltpu.CompilerParams(dimension_semantics=("arbitrary",),
                                             vmem_limit_bytes=VMEM_LIMIT_BIG),
    )(dproj, wg_in, x, front, ds2, norm_g, part)


def _col_block(width, cap):
    return max(b for b in range(128, cap + 1, 128) if width % b == 0)


def _dw_reduced(lhs_t, rhs, cw, nblk, operands, groups, out_dims, out_block, out_index, carried, name):
    na, d, tp = lhs_t.shape
    rg = d // groups
    hh = rg // 2
    nc = len(carried)

    def body(*refs):
        l_ref, r_ref = refs[:2]
        part_refs = refs[2:2 + nc]
        p32_ref, pbf_ref = refs[2 + nc:4 + nc]
        land_refs = refs[4 + nc:4 + 2 * nc]
        res, rbuf, send_sems, recv_sems = refs[4 + 2 * nc:8 + 2 * nc]
        xsems = refs[8 + 2 * nc:]
        exchanges = [_chip_exchange(part_refs[e], land_refs[e], xsems[2 * e], xsems[2 * e + 1]) for e in range(nc)]
        exchange = _Exchange([s for ex in exchanges for s in ex.sends], [r for ex in exchanges for r in ex.recvs])
        x, y, c = _mesh_pos()
        t = pl.program_id(0)
        u = jnp.maximum(t - 1, 0)

        def to_sibling(blk):
            return pltpu.make_async_remote_copy(
                src_ref=res.at[blk % 2, :, pl.ds((1 - c) * hh, hh), :], dst_ref=rbuf.at[blk % 2],
                send_sem=send_sems.at[blk], recv_sem=recv_sems.at[blk],
                device_id=(x, y, 1 - c), device_id_type=MESH)

        @pl.when(t == 0)
        def _():
            exchange.start()

        @pl.when(t < nblk)
        def _():
            res[t % 2] = jnp.dot(l_ref[...], r_ref[...], preferred_element_type=F32).reshape(groups, rg, cw)

        @pl.when(t >= 1)
        def _():
            to_sibling(u).wait_recv()
            p = res[u % 2, :, pl.ds(c * hh, hh), :] + rbuf[u % 2]
            p32_ref[...] = p.reshape(p32_ref.shape)
            pbf_ref[...] = p.reshape(pbf_ref.shape).astype(BF16)

        @pl.when(t < nblk)
        def _():
            to_sibling(t).start()

        @pl.when(t >= 1)
        def _():
            to_sibling(u).wait_send()

        @pl.when(t == nblk)
        def _():
            exchange.finish()

    any_spec = pl.BlockSpec(memory_space=pl.ANY)
    last = nblk - 1
    out_spec = pl.BlockSpec(out_block, lambda t: out_index(jnp.maximum(t - 1, 0)))
    outs = pl.pallas_call(
        body, name=name, grid=(nblk + 1,),
        in_specs=[pl.BlockSpec((None, d, tp), lambda t: (operands(jnp.minimum(t, last))[0], 0, 0)),
                  pl.BlockSpec((None, tp, cw), lambda t: (operands(jnp.minimum(t, last))[0], 0,
                                                          operands(jnp.minimum(t, last))[1]))]
        + [any_spec] * nc,
        out_specs=[out_spec, out_spec] + [any_spec] * nc,
        out_shape=[jax.ShapeDtypeStruct(out_dims, F32), jax.ShapeDtypeStruct(out_dims, BF16)]
        + [jax.ShapeDtypeStruct(e.shape, e.dtype) for e in carried],
        scratch_shapes=[pltpu.VMEM((2, groups, rg, cw), F32), pltpu.VMEM((2, groups, hh, cw), F32),
                        pltpu.SemaphoreType.DMA((nblk,)), pltpu.SemaphoreType.DMA((nblk,))]
        + [pltpu.SemaphoreType.DMA((3,)), pltpu.SemaphoreType.DMA((3,))] * nc,
        compiler_params=_params(("arbitrary",)),
    )(lhs_t, rhs, *carried)
    return outs[0], outs[1], outs[2:]


def _conv_a_taps(first_lag, last_lag):
    out = []
    for r in range(8):
        taps = [(q, 8 * q + r) for q in range(5) if first_lag <= 8 * q + r <= last_lag]
        if taps:
            out.append((r, taps))
    return out


def _tile_block(i, nt):
    return jnp.where(i == 0, nt - 1, i - 1)


def _mix_fwd(x, front, proj, target, w3, wa, wb, conv_a_b, ln_g, ln_b, b_a_out, final_g, h):
    seq, d = x.shape
    tp = seq + TM
    nt = tp // TM
    RB = RB_FWD
    nrb = TM // RB
    shl = TM + SHIFT_ROWS

    def body(x_ref, front_ref, proj_ref, tgt_ref, w3_ref, wa_ref, wb_ref, cab_ref, lng_ref, lnb_ref, bao_ref, fg_ref,
             h_ref, ca_ref, cb_ref, ya_ref, yb_ref, abmt_ref, ds2_ref, ht_ref, loss_ref, dfg_ref,
             abm_ref, ext_a, ext_b, sh, s2_s, lacc, gacc):
        i = pl.program_id(0)

        def split(k, rows):
            return proj_ref[rows, k * d:(k + 1) * d].astype(F32)

        def s_tile():
            return jnp.where(i == 0, front_ref[...], x_ref[...])

        ht_ref[...] = h_ref[...].T

        @pl.when(i == 0)
        def _():
            ext_a[0:HALO_A, :] = jnp.zeros((HALO_A, d), F32)
            ext_b[0:HALO_B, :] = jnp.zeros((HALO_B, d), F32)
            lacc[...] = jnp.zeros_like(lacc)
            gacc[...] = jnp.zeros_like(gacc)

        def conv_in(rb, carry):
            rows = _rows(rb, RB)
            ua0 = split(0, rows) * _sigmoid(split(1, rows))
            ext_a[pl.ds(pl.multiple_of(HALO_A + rb * RB, 8), RB), :] = ua0
            ext_b[pl.ds(pl.multiple_of(HALO_B + rb * RB, 8), RB), :] = split(4, rows) * split(5, rows)
            ca_ref[rows, :] = jnp.broadcast_to(cab_ref[...], (RB, d))
            return carry
        lax.fori_loop(0, nrb, conv_in, 0)

        @pl.when(i == 0)
        def _():
            abmt_ref[...] = jnp.zeros_like(abmt_ref)
            ds2_ref[...] = jnp.zeros_like(ds2_ref)

        @pl.when(i > 0)
        def _():
            tile_after_conv_inputs(split, s_tile, tgt_ref, w3_ref, wa_ref, wb_ref, lng_ref, lnb_ref, bao_ref, fg_ref,
                                   ca_ref, cb_ref, ya_ref, yb_ref, abmt_ref, ds2_ref, abm_ref, ext_a, ext_b, sh,
                                   s2_s, lacc, gacc)

        ext_a[0:HALO_A, :] = ext_a[TM:TM + HALO_A, :]
        ext_b[0:HALO_B, :] = ext_b[TM:TM + HALO_B, :]

        @pl.when(i == nt - 1)
        def _():
            loss_ref[...] = jnp.broadcast_to(0.5 * jnp.sum(lacc[...]) * (1.0 / d), (8, 128))
            dfg_ref[...] = jnp.broadcast_to(jnp.sum(gacc[...], axis=0, keepdims=True), (8, d))

    def tile_after_conv_inputs(split, s_tile, tgt_ref, w3_ref, wa_ref, wb_ref, lng_ref, lnb_ref, bao_ref, fg_ref,
                               ca_ref, cb_ref, ya_ref, yb_ref, abmt_ref, ds2_ref, abm_ref, ext_a, ext_b, sh, s2_s,
                               lacc, gacc):
        for r, taps in _conv_a_taps(HALO_A - CONV_A + 1, HALO_A):
            if r == 0:
                src = ext_a
            else:
                sh[...] = ext_a[r:r + shl, :]
                src = sh

            def conv_acc(rb, carry, src=src, taps=taps):
                rows = _rows(rb, RB)
                acc = ca_ref[rows, :]
                for q, lag in taps:
                    k = lag - (HALO_A - CONV_A + 1)
                    acc = acc + src[pl.ds(pl.multiple_of(rb * RB + 8 * q, 8), RB), :] * wa_ref[k:k + 1, :]
                ca_ref[rows, :] = acc
                return carry
            lax.fori_loop(0, nrb, conv_acc, 0)

        cb_ref[...] = ext_b[HALO_B:HALO_B + TM, :] * wb_ref[2:3, :]
        for k in range(CONV_B - 1):
            off = HALO_B - CONV_B + 1 + k
            sh[0:TM, :] = ext_b[off:off + TM, :]
            cb_ref[...] += sh[0:TM, :] * wb_ref[k:k + 1, :]

        def branches(rb, carry):
            rows = _rows(rb, RB)
            ca = ca_ref[rows, :]
            mu = jnp.mean(ca, axis=-1, keepdims=True)
            xc = ca - mu
            rstd = lax.rsqrt(jnp.mean(xc * xc, axis=-1, keepdims=True) + EPS)
            ln = xc * rstd * lng_ref[...] + lnb_ref[...]
            ua = ln * _sigmoid(ln)
            a_z = split(2, rows)
            abm_ref[0, rows, :] = (ua * (a_z * _sigmoid(a_z))).astype(BF16)
            return carry
        lax.fori_loop(0, nrb, branches, 0)

        def branch_b(rb, carry):
            rows = _rows(rb, RB)
            b_z = split(6, rows)
            ub = split(3, rows) * cb_ref[rows, :]
            abm_ref[1, rows, :] = (ub * (b_z * _sigmoid(b_z))).astype(BF16)
            return carry
        lax.fori_loop(0, nrb, branch_b, 0)

        ya_ref[...] = jnp.dot(abm_ref[0], w3_ref[0], preferred_element_type=F32) + bao_ref[...]
        yb_ref[...] = jnp.dot(abm_ref[1], w3_ref[1], preferred_element_type=F32)

        def merge(rb, carry):
            rows = _rows(rb, RB)
            m = _sigmoid(split(7, rows)) * ya_ref[rows, :] + _sigmoid(split(8, rows)) * yb_ref[rows, :]
            abm_ref[2, rows, :] = m.astype(BF16)
            return carry
        lax.fori_loop(0, nrb, merge, 0)

        s2_s[...] = s_tile() + jnp.dot(abm_ref[2], w3_ref[2], preferred_element_type=F32)
        for k in range(3):
            abmt_ref[k] = abm_ref[k].T

        def head(rb, carry):
            rows = _rows(rb, RB)
            s2 = s2_s[rows, :]
            r2 = lax.rsqrt(jnp.mean(s2 * s2, axis=-1, keepdims=True) + EPS)
            diff = s2 * r2 * fg_ref[...] - tgt_ref[rows, :]
            lacc[...] += diff * diff
            dy = diff * (1.0 / d)
            gacc[...] += (dy * s2 * r2).reshape(RB // 8, 8, d).sum(axis=0)
            t = dy * fg_ref[...]
            ds2_ref[rows, :] = r2 * t - s2 * (r2 * r2 * r2) * jnp.mean(t * s2, axis=-1, keepdims=True)
            return carry
        lax.fori_loop(0, nrb, head, 0)

    row_f32 = pl.BlockSpec((TM, d), lambda i: (_tile_block(i, nt), 0))
    x_rows = pl.BlockSpec((TM, d), lambda i: (jnp.maximum(i - 1, 0), 0))
    const = lambda shape: pl.BlockSpec(shape, lambda i: (0,) * len(shape))
    return pl.pallas_call(
        body, name="f2_mix", grid=(nt,),
        in_specs=[x_rows, const((TM, d)),
                  pl.BlockSpec((TM, N_SPLIT * d), lambda i: (_tile_block(i, nt), 0)),
                  x_rows,
                  const((3, d, d)), const(wa.shape), const(wb.shape)] + [const((1, d))] * 5
        + [pl.BlockSpec((TM, d), lambda i: (_tile_block(i, nt), 0))],
        out_specs=[row_f32, row_f32, row_f32, row_f32,
                   pl.BlockSpec((3, d, TM), lambda i: (0, 0, _tile_block(i, nt))),
                   row_f32, pl.BlockSpec((d, TM), lambda i: (0, _tile_block(i, nt))),
                   const((8, 128)), const((8, d))],
        out_shape=[jax.ShapeDtypeStruct((tp, d), F32)] * 4
        + [jax.ShapeDtypeStruct((3, d, tp), BF16), jax.ShapeDtypeStruct((tp, d), F32),
           jax.ShapeDtypeStruct((d, tp), BF16),
           jax.ShapeDtypeStruct((8, 128), F32), jax.ShapeDtypeStruct((8, d), F32)],
        scratch_shapes=[pltpu.VMEM((3, TM, d), BF16),
                        pltpu.VMEM((HALO_A + TM, d), F32), pltpu.VMEM((HALO_B + TM, d), F32),
                        pltpu.VMEM((shl, d), F32), pltpu.VMEM((TM, d), F32),
                        pltpu.VMEM((RB, d), F32), pltpu.VMEM((8, d), F32)],
        compiler_params=_params(("arbitrary",)),
    )(x, front, proj, target, w3, wa, wb, conv_a_b, ln_g, ln_b, b_a_out, final_g, h)


def _mix_bwd(ds2, proj, ca, cb, ya, yb, w3, wa, wb, ln_g, ln_b):
    tp, d = ds2.shape
    nt = tp // TM
    RB = RB_BWD
    nrb = TM // RB
    strips = [slice(c0, c0 + STRIP_COLS) for c0 in range(0, d, STRIP_COLS)]
    shl = TM + SHIFT_ROWS
    nt_dims = (((1,), (1,)), ((), ()))

    def body(ds2_ref, proj_ref, ca_ref, cb_ref, ya_ref, yb_ref, w3_ref, wa_ref, wb_ref, lng_ref, lnb_ref,
             dproj_ref, d3_ref, sm_ref, ext_d, ext_e, sh, dm_s, dpa_s, dpb_s, dua0_s, acc):
        step = pl.program_id(0)

        def split(k, rows, cols=slice(0, d)):
            return proj_ref[rows, k * d + cols.start:k * d + cols.stop].astype(F32)

        def put(k, rows, val, cols=slice(0, d)):
            dproj_ref[rows, k * d + cols.start:k * d + cols.stop] = val.astype(BF16)

        def accum(row, val, cols=slice(0, d)):
            acc[row, :, cols] += val.reshape(RB // 8, 8, val.shape[-1]).sum(axis=0)

        @pl.when(step == 0)
        def _():
            ext_d[TM:TM + HALO_A, :] = jnp.zeros((HALO_A, d), F32)
            ext_e[TM:TM + HALO_B, :] = jnp.zeros((HALO_B, d), F32)
            acc[...] = jnp.zeros_like(acc)

        front = step == nt - 1

        @pl.when(front)
        def _():
            d3_ref[...] = jnp.zeros_like(d3_ref)

            def conv_only(rb, carry):
                rows = _rows(rb, RB)
                zeros = jnp.zeros((RB, d), F32)
                for k in (2, 3, 6, 7, 8):
                    put(k, rows, zeros)
                ext_d[rows, :] = zeros
                ext_e[rows, :] = zeros
                dua0_s[rows, :] = zeros
                dm_s[rows, :] = split(0, rows) * _sigmoid(split(1, rows))
                return carry
            lax.fori_loop(0, nrb, conv_only, 0)

        @pl.when(jnp.logical_not(front))
        def _():
            tile_to_conv_outputs(split, put, accum, ds2_ref, ca_ref, cb_ref, ya_ref, yb_ref, w3_ref, lng_ref, lnb_ref,
                                 d3_ref, ext_d, ext_e, dm_s, dpa_s, dpb_s, dua0_s)

        tile_conv_transposes(split, put, accum, wa_ref, wb_ref, ext_d, ext_e, sh, dm_s, dpb_s, dua0_s)

        @pl.when(front)
        def _():
            for row in range(SM_ROWS):
                sm_ref[row:row + 1, :] = jnp.sum(acc[row], axis=0, keepdims=True)

    def tile_to_conv_outputs(split, put, accum, ds2_ref, ca_ref, cb_ref, ya_ref, yb_ref, w3_ref, lng_ref, lnb_ref,
                             d3_ref, ext_d, ext_e, dm_s, dpa_s, dpb_s, dua0_s):
        d3_ref[2] = ds2_ref[...].astype(BF16)
        dm_s[...] = lax.dot_general(d3_ref[2], w3_ref[2], nt_dims, preferred_element_type=F32)

        def gates(rb, carry):
            rows = _rows(rb, RB)
            for cols in strips:
                dm = dm_s[rows, cols]
                sa = _sigmoid(split(7, rows, cols))
                sb = _sigmoid(split(8, rows, cols))
                put(7, rows, dm * ya_ref[rows, cols] * sa * (1.0 - sa), cols)
                put(8, rows, dm * yb_ref[rows, cols] * sb * (1.0 - sb), cols)
                dya = dm * sa
                accum(ROW_DBAO, dya, cols)
                d3_ref[0, rows, cols] = dya.astype(BF16)
                d3_ref[1, rows, cols] = (dm * sb).astype(BF16)
            return carry
        lax.fori_loop(0, nrb, gates, 0, unroll=True)

        dpa_s[...] = lax.dot_general(d3_ref[0], w3_ref[0], nt_dims, preferred_element_type=F32)
        dpb_s[...] = lax.dot_general(d3_ref[1], w3_ref[1], nt_dims, preferred_element_type=F32)

        def branch_a(rb, carry):
            rows = _rows(rb, RB)

            def row_mean(strip_fn):
                total = strip_fn(strips[0])
                for cols in strips[1:]:
                    total = total + strip_fn(cols)
                return jnp.sum(total, axis=-1, keepdims=True) * (1.0 / d)

            mu = row_mean(lambda cols: ca_ref[rows, cols])
            rstd = lax.rsqrt(row_mean(lambda cols: jnp.square(ca_ref[rows, cols] - mu)) + EPS)
            sum_dxh = jnp.zeros((RB, STRIP_COLS), F32)
            sum_dxh_xhat = jnp.zeros((RB, STRIP_COLS), F32)
            for cols in strips:
                xhat = (ca_ref[rows, cols] - mu) * rstd
                ln = xhat * lng_ref[:, cols] + lnb_ref[:, cols]
                sl = _sigmoid(ln)
                ua = ln * sl
                a_z = split(2, rows, cols)
                sz = _sigmoid(a_z)
                dpa = dpa_s[rows, cols]
                put(2, rows, dpa * ua * (sz * (1.0 + a_z * (1.0 - sz))), cols)
                dln = dpa * (a_z * sz) * (sl * (1.0 + ln * (1.0 - sl)))
                accum(ROW_DLNG, dln * xhat, cols)
                accum(ROW_DLNB, dln, cols)
                dxh = dln * lng_ref[:, cols]
                sum_dxh = sum_dxh + dxh
                sum_dxh_xhat = sum_dxh_xhat + dxh * xhat
                ext_d[rows, cols] = dxh
                dpa_s[rows, cols] = xhat
            mean_dxh = jnp.sum(sum_dxh, axis=-1, keepdims=True) * (1.0 / d)
            mean_dxh_xhat = jnp.sum(sum_dxh_xhat, axis=-1, keepdims=True) * (1.0 / d)
            for cols in strips:
                dca = rstd * (ext_d[rows, cols] - mean_dxh - dpa_s[rows, cols] * mean_dxh_xhat)
                accum(ROW_DCAB, dca, cols)
                ext_d[rows, cols] = dca
            return carry
        lax.fori_loop(0, nrb, branch_a, 0, unroll=True)

        def branch_b(rb, carry):
            rows = _rows(rb, RB)
            dua0_s[rows, :] = jnp.zeros((RB, d), F32)
            for cols in strips:
                dm_s[rows, cols] = split(0, rows, cols) * _sigmoid(split(1, rows, cols))
                b_z = split(6, rows, cols)
                szb = _sigmoid(b_z)
                dpb = dpb_s[rows, cols]
                b_b = split(3, rows, cols)
                cb_v = cb_ref[rows, cols]
                put(6, rows, dpb * (b_b * cb_v) * (szb * (1.0 + b_z * (1.0 - szb))), cols)
                dub = dpb * (b_z * szb)
                put(3, rows, dub * cb_v, cols)
                ext_e[rows, cols] = dub * b_b
            return carry
        lax.fori_loop(0, nrb, branch_b, 0, unroll=True)

    def tile_conv_transposes(split, put, accum, wa_ref, wb_ref, ext_d, ext_e, sh, dm_s, dpb_s, dua0_s):
        for r, taps in _conv_a_taps(0, CONV_A - 1):
            if r == 0:
                src = ext_d
            else:
                sh[...] = ext_d[r:r + shl, :]
                src = sh

            def conv_t(rb, carry, src=src, taps=taps):
                rows = _rows(rb, RB)
                for cols in strips:
                    ua0 = dm_s[rows, cols]
                    dua0 = dua0_s[rows, cols]
                    for q, lag in taps:
                        k = CONV_A - 1 - lag
                        slab = src[pl.ds(pl.multiple_of(rb * RB + 8 * q, 8), RB), cols]
                        dua0 = dua0 + slab * wa_ref[k:k + 1, cols]
                        accum(ROW_DWA + k, slab * ua0, cols)
                    dua0_s[rows, cols] = dua0
                return carry
            lax.fori_loop(0, nrb, conv_t, 0, unroll=True)
        ext_d[TM:TM + HALO_A, :] = ext_d[0:HALO_A, :]

        dpb_s[...] = ext_e[0:TM, :] * wb_ref[CONV_B - 1:CONV_B, :]
        for lag in range(CONV_B):
            k = CONV_B - 1 - lag
            if lag > 0:
                sh[0:TM, :] = ext_e[lag:lag + TM, :]
                dpb_s[...] += sh[0:TM, :] * wb_ref[k:k + 1, :]
            src = ext_e if lag == 0 else sh

            def conv_b_w(rb, carry, src=src, k=k):
                rows = _rows(rb, RB)
                accum(ROW_DWB + k, src[rows, :] * (split(4, rows) * split(5, rows)))
                return carry
            lax.fori_loop(0, nrb, conv_b_w, 0, unroll=True)
        ext_e[TM:TM + HALO_B, :] = ext_e[0:HALO_B, :]

        def inputs(rb, carry):
            rows = _rows(rb, RB)
            for cols in strips:
                dua0 = dua0_s[rows, cols]
                a_val = split(0, rows, cols)
                sg = _sigmoid(split(1, rows, cols))
                put(0, rows, dua0 * sg, cols)
                put(1, rows, dua0 * a_val * sg * (1.0 - sg), cols)
                dcbin = dpb_s[rows, cols]
                put(4, rows, dcbin * split(5, rows, cols), cols)
                put(5, rows, dcbin * split(4, rows, cols), cols)
            return carry
        lax.fori_loop(0, nrb, inputs, 0, unroll=True)

    rev = lambda i: (_tile_block(nt - 1 - i, nt), 0)
    row_f32 = pl.BlockSpec((TM, d), rev)
    const = lambda shape: pl.BlockSpec(shape, lambda i: (0,) * len(shape))
    return pl.pallas_call(
        body, name="b1_mix", grid=(nt,),
        in_specs=[row_f32, pl.BlockSpec((TM, N_SPLIT * d), rev), row_f32, row_f32, row_f32, row_f32,
                  const((3, d, d)), const(wa.shape), const(wb.shape), const((1, d)), const((1, d))],
        out_specs=[pl.BlockSpec((TM, N_SPLIT * d), rev),
                   pl.BlockSpec((3, TM, d), lambda i: (0, _tile_block(nt - 1 - i, nt), 0)),
                   const((SM_ROWS, d))],
        out_shape=[jax.ShapeDtypeStruct((tp, N_SPLIT * d), BF16), jax.ShapeDtypeStruct((3, tp, d), BF16),
                   jax.ShapeDtypeStruct((SM_ROWS, d), F32)],
        scratch_shapes=[pltpu.VMEM((TM + HALO_A, d), F32), pltpu.VMEM((TM + HALO_B, d), F32),
                        pltpu.VMEM((shl, d), F32), pltpu.VMEM((TM, d), F32), pltpu.VMEM((TM, d), F32),
                        pltpu.VMEM((TM, d), F32), pltpu.VMEM((TM, d), F32),
                        pltpu.VMEM((SM_ROWS, 8, d), F32)],
        compiler_params=_params(("arbitrary",)),
    )(ds2, proj, ca, cb, ya, yb, w3, wa, wb, ln_g, ln_b)


def kernel(x, meta_tokens, norm_g, w_in, conv_a_w, conv_a_b, ln_a_g, ln_a_b, w_a_out, b_a_out, conv_b_w, w_b_out, w_out, final_g, loss_target, m_meta_tokens, m_norm_g, m_w_in, m_conv_a_w, m_conv_a_b, m_ln_a_g, m_ln_a_b, m_w_a_out, m_b_a_out, m_conv_b_w, m_w_b_out, m_w_out, m_final_g, v_meta_tokens, v_norm_g, v_w_in, v_conv_a_w, v_conv_a_b, v_ln_a_g, v_ln_a_b, v_w_a_out, v_b_a_out, v_conv_b_w, v_w_b_out, v_w_out, v_final_g):
    seq, d = x.shape[1], x.shape[2]
    dc = meta_tokens.shape[1]
    sw = w_in.shape[2]
    rsh = w_a_out.shape[1]
    xi, yi, ci = _mesh_pos()
    me = 2 * xi + yi
    pos = jnp.stack([ci, me]).astype(jnp.int32)

    conv_rows = HALO_A + HALO_B + 8
    convs = jnp.concatenate([
        jnp.pad(conv_a_w[0], ((0, HALO_A - CONV_A), (0, 0))),
        jnp.pad(conv_b_w[0], ((0, HALO_B - CONV_B), (0, 0))), jnp.zeros((8, dc), F32)], axis=0)[None]
    w3_own = jnp.stack([w_a_out[0], w_b_out[0], w_out[0]])
    fg2 = final_g.reshape(1, d)
    xs = x[0]

    h, front, placed = _h_prep(xs, meta_tokens, norm_g, [(w_in, BF16), (w3_own, BF16), (convs, F32)], pos)
    proj, (wg_in, wg3, convg) = _proj_fwd(h, placed, pos)
    w3 = wg3.reshape(3, N_CHIPS * rsh, d)
    convg = jnp.transpose(convg[0], (1, 0, 2)).reshape(conv_rows, N_CHIPS * dc)
    wa_full = convg[0:HALO_A]
    wb_full = convg[HALO_A:HALO_A + HALO_B]
    ca, cb, ya, yb, abm_t, ds2, h_t, loss8, dfg8 = _mix_fwd(
        xs, front, proj, loss_target[0], w3, wa_full, wb_full, conv_a_b, ln_a_g, ln_a_b, b_a_out, fg2, h)
    dproj, d3, sm = _mix_bwd(ds2, proj, ca, cb, ya, yb, w3, wa_full, wb_full, ln_a_g, ln_a_b)
    cw_sq = _col_block(d, 512)
    per_sq = d // cw_sq
    p32_sq, pbf_sq, _ = _dw_reduced(
        abm_t, d3, cw_sq, 3 * per_sq, lambda t: (t // per_sq, t % per_sq), N_CHIPS,
        (3, N_CHIPS, rsh // 2, d), (None, N_CHIPS, rsh // 2, cw_sq),
        lambda u: (u // per_sq, 0, 0, u % per_sq), [], "dw_square")
    cw_in = _col_block(sw, 768)
    ncol = sw // cw_in
    p32_in, pbf_in, (l_sq,) = _dw_reduced(
        h_t[None], dproj[None], cw_in, N_CHIPS * ncol, lambda t: (0, t), 1,
        (1, N_CHIPS, d // 2, sw), (None, None, d // 2, cw_in), lambda u: (0, u // ncol, 0, u % ncol),
        [pbf_sq], "dw_in")
    grad_x, dfront, dng8, l_in = _dh_bwd(dproj, wg_in, xs, front, ds2, norm_g, pbf_in)
    half_in = _sum_chips([(p32_in, l_in)], sw, pos, "rs_sum_in")
    half_sq = _sum_chips([(p32_sq, l_sq)], d, pos, "rs_sum_sq")
    tail_row = lax.broadcasted_iota(jnp.int32, (8, d), 0)
    tail = jnp.where(tail_row == 0, dng8, jnp.where(tail_row == 1, dfg8,
                     jnp.where(tail_row == 2, loss8[0, 0], 0.0)))
    block = jnp.concatenate([sm, dfront[TM - N_META:TM], tail], axis=0)
    (other_in, other_sq), red = _sibling_swap([half_in, half_sq], block)
    col = lax.dynamic_slice(red, (0, me * dc), (AR_ROWS, dc))
    g_small = {
        "meta_tokens": col[ROW_DMETA:ROW_DMETA + N_META],
        "norm_g": red[ROW_DNG:ROW_DNG + 1],
        "conv_a_w": col[ROW_DWA:ROW_DWA + CONV_A][None],
        "conv_a_b": red[ROW_DCAB:ROW_DCAB + 1],
        "ln_a_g": red[ROW_DLNG:ROW_DLNG + 1],
        "ln_a_b": red[ROW_DLNB:ROW_DLNB + 1],
        "b_a_out": red[ROW_DBAO:ROW_DBAO + 1],
        "conv_b_w": col[ROW_DWB:ROW_DWB + CONV_B][None],
        "final_g": red[ROW_DFG],
    }

    upd_in = _adam_halves([w_in], [m_w_in], [v_w_in], half_in, other_in, pos, "adam_in")
    upd_sq = _adam_halves([w_a_out, w_b_out, w_out], [m_w_a_out, m_w_b_out, m_w_out],
                          [v_w_a_out, v_w_b_out, v_w_out], half_sq, other_sq, pos, "adam_sq")
    small_w = {"meta_tokens": (meta_tokens, m_meta_tokens, v_meta_tokens), "norm_g": (norm_g, m_norm_g, v_norm_g),
               "conv_a_w": (conv_a_w, m_conv_a_w, v_conv_a_w), "conv_a_b": (conv_a_b, m_conv_a_b, v_conv_a_b),
               "ln_a_g": (ln_a_g, m_ln_a_g, v_ln_a_g), "ln_a_b": (ln_a_b, m_ln_a_b, v_ln_a_b),
               "b_a_out": (b_a_out, m_b_a_out, v_b_a_out), "conv_b_w": (conv_b_w, m_conv_b_w, v_conv_b_w),
               "final_g": (final_g, m_final_g, v_final_g)}
    names_small = list(small_w)
    as2d = lambda t: t.reshape(-1, t.shape[-1])
    upd_small = _adam_small([(as2d(small_w[k][0]), as2d(g_small[k]), as2d(small_w[k][1]), as2d(small_w[k][2]))
                             for k in names_small])

    grads, deltas, new_m, new_v = dict(g_small), {}, {}, {}
    for k, upd in zip(names_small, upd_small):
        deltas[k], new_m[k], new_v[k] = [t.reshape(small_w[k][0].shape) for t in upd]
    grads["w_in"], deltas["w_in"], new_m["w_in"], new_v["w_in"] = upd_in
    for idx, k in enumerate(["w_a_out", "w_b_out", "w_out"]):
        grads[k], deltas[k], new_m[k], new_v[k] = upd_sq[4 * idx:4 * idx + 4]

    loss = red[ROW_LOSS, 0]
    order = ["meta_tokens", "norm_g", "w_in", "conv_a_w", "conv_a_b", "ln_a_g", "ln_a_b", "w_a_out", "b_a_out",
             "conv_b_w", "w_b_out", "w_out", "final_g"]
    return (loss, grad_x[None], *[grads[k] for k in order], *[deltas[k] for k in order],
            *[new_m[k] for k in order], *[new_v[k] for k in order])
```

```python
import jax
import jax.numpy as jnp
from jax import lax
from jax.experimental import pallas as pl
from jax.experimental.pallas import tpu as pltpu

F32 = jnp.float32
BF16 = jnp.bfloat16
MESH = pl.DeviceIdType.MESH

EPS = 1e-6
N_META = 16
N_SPLIT = 9
CONV_A = 31
CONV_B = 3
HALO_A = 32
HALO_B = 8
SHIFT_ROWS = 24
TM = 256
RB_FWD = 256
RB_BWD = 32
STRIP_COLS = 256
N_ROW_TILES_BIG = 8
ROW_BLOCK = 256
N_CHIPS = 4
VMEM_LIMIT = 56 * 1024 * 1024
VMEM_LIMIT_BIG = 62 * 1024 * 1024

ADAM_LR = 0.001
ADAM_B1 = 0.9
ADAM_B2 = 0.999
ADAM_EPS = 1e-08
ADAM_WD = 0.01
ADAM_STEP = 10

ROW_DWA = 0
ROW_DWB = 32
ROW_DCAB = 40
ROW_DLNG = 41
ROW_DLNB = 42
ROW_DBAO = 43
SM_ROWS = 48
ROW_DMETA = 48
ROW_DNG = 64
ROW_DFG = 65
ROW_LOSS = 66
AR_ROWS = 72


def _sigmoid(v):
    return 0.5 * jnp.tanh(0.5 * v) + 0.5


def _params(sem, **kw):
    return pltpu.CompilerParams(dimension_semantics=sem, vmem_limit_bytes=VMEM_LIMIT, **kw)


def _rows(rb, n):
    return pl.ds(pl.multiple_of(rb * n, n), n)


def _mesh_pos():
    x, y, c = lax.axis_index("x"), lax.axis_index("y"), lax.axis_index("c")
    return x, y, c


def _half(ref, j, c):
    h = ref.shape[2] // 2
    return ref.at[:, j, pl.ds(c * h, h), :]


class _Exchange:
    def __init__(self, sends, recvs):
        self.sends, self.recvs = sends, recvs

    def start(self):
        for cp in self.sends:
            cp.start()

    def finish(self):
        for cp in self.recvs:
            cp.wait_recv()
        for cp in self.sends:
            cp.wait_send()


def _chip_exchange(part_ref, land_ref, send_sems, recv_sems):
    x, y, c = _mesh_pos()
    me = 2 * x + y
    sends, recvs = [], []
    for k, (px, py) in enumerate([(1 - x, y), (x, 1 - y), (1 - x, 1 - y)]):
        sems = dict(send_sem=send_sems.at[k], recv_sem=recv_sems.at[k], device_id=(px, py, c), device_id_type=MESH)
        sends.append(pltpu.make_async_remote_copy(
            src_ref=part_ref.at[:, 2 * px + py], dst_ref=land_ref.at[:, me], **sems))
        landed = land_ref.at[:, 2 * px + py]
        recvs.append(pltpu.make_async_remote_copy(src_ref=landed, dst_ref=landed, **sems))
    return _Exchange(sends, recvs)

def _sibling_swap(halves, small):
    n = len(halves)

    def body(*refs):
        ins, small_ref, outs, red_ref = refs[:n], refs[n], refs[n + 1:2 * n + 1], refs[2 * n + 1]
        send_sems, recv_sems = refs[2 * n + 2:2 * n + 4]
        reduce = _SmallAllReduce(small_ref, red_ref, *refs[2 * n + 4:])
        x, y, c = _mesh_pos()
        copies = [pltpu.make_async_remote_copy(
            src_ref=ins[a], dst_ref=outs[a], send_sem=send_sems.at[a], recv_sem=recv_sems.at[a],
            device_id=(x, y, 1 - c), device_id_type=MESH) for a in range(n)]
        reduce.start()
        for cp in copies:
            cp.start()
        reduce.between_chips()
        reduce.finish()
        for cp in copies:
            cp.wait()

    any_spec = pl.BlockSpec(memory_space=pl.ANY)
    vm = pl.BlockSpec(memory_space=pltpu.VMEM)
    outs = pl.pallas_call(
        body, name="rs_swap",
        in_specs=[any_spec] * n + [vm], out_specs=[any_spec] * n + [vm],
        out_shape=[jax.ShapeDtypeStruct(h.shape, h.dtype) for h in halves]
        + [jax.ShapeDtypeStruct(small.shape, F32)],
        scratch_shapes=[pltpu.SemaphoreType.DMA((n,)), pltpu.SemaphoreType.DMA((n,))]
        + _SmallAllReduce.scratch(*small.shape),
    )(*halves, small)
    return outs[:n], outs[n]


class _SmallAllReduce:
    def __init__(self, x_ref, out_ref, sib_ref, part_ref, peers_ref, send_sems, recv_sems):
        self.x_ref, self.out_ref, self.sib_ref, self.part_ref, self.peers_ref = x_ref, out_ref, sib_ref, part_ref, peers_ref
        x, y, c = _mesh_pos()
        self.me = 2 * x + y
        self.swap = pltpu.make_async_remote_copy(
            src_ref=x_ref, dst_ref=sib_ref, send_sem=send_sems.at[0], recv_sem=recv_sems.at[0],
            device_id=(x, y, 1 - c), device_id_type=MESH)
        self.sends, self.recvs = [], []
        for k, (px, py) in enumerate([(1 - x, y), (x, 1 - y), (1 - x, 1 - y)]):
            sems = dict(send_sem=send_sems.at[1 + k], recv_sem=recv_sems.at[1 + k],
                        device_id=(px, py, c), device_id_type=MESH)
            self.sends.append(pltpu.make_async_remote_copy(src_ref=part_ref, dst_ref=peers_ref.at[self.me], **sems))
            landed = peers_ref.at[2 * px + py]
            self.recvs.append(pltpu.make_async_remote_copy(src_ref=landed, dst_ref=landed, **sems))

    @staticmethod
    def scratch(rows, d):
        return [pltpu.VMEM((rows, d), F32), pltpu.VMEM((rows, d), F32), pltpu.VMEM((N_CHIPS, rows, d), F32),
                pltpu.SemaphoreType.DMA((4,)), pltpu.SemaphoreType.DMA((4,))]

    def start(self):
        self.swap.start()

    def between_chips(self):
        self.swap.wait()
        self.part_ref[...] = self.x_ref[...] + self.sib_ref[...]
        self.peers_ref[self.me] = self.part_ref[...]
        for cp in self.sends:
            cp.start()

    def finish(self):
        for cp in self.recvs:
            cp.wait_recv()
        for cp in self.sends:
            cp.wait_send()
        p = self.peers_ref
        self.out_ref[...] = ((p[0] + p[1]) + p[2]) + p[3]


def _sum_chips(parts, cw, pos, name):
    s, _, h, _ = parts[0][0].shape
    hb = min(h, ROW_BLOCK)
    widths = [own.shape[3] // cw for own, _ in parts]
    starts = [sum(widths[:a]) for a in range(len(parts))]

    def body(pos_ref, *refs):
        out_ref = refs[-1]
        n = pl.program_id(2)
        total = None
        for a in range(len(parts)):
            own, l1, l2, l3 = refs[4 * a:4 * a + 4]
            val = ((own[...] + l1[...].astype(F32)) + l2[...].astype(F32)) + l3[...].astype(F32)
            total = val if total is None else jnp.where(n >= starts[a], val, total)
        out_ref[...] = total

    def slot(a, k):
        col = lambda n: jnp.clip(n - starts[a], 0, widths[a] - 1)
        return pl.BlockSpec((None, None, hb, cw),
                            lambda si, b, n, pos_ref: (si, (pos_ref[1] + k) % N_CHIPS, b, col(n)))

    operands, specs = [], []
    for a, (own, landed) in enumerate(parts):
        operands += [own, landed, landed, landed]
        specs += [slot(a, 0), slot(a, 1), slot(a, 2), slot(a, 3)]
    return pl.pallas_call(
        body, name=name,
        grid_spec=pltpu.PrefetchScalarGridSpec(
            num_scalar_prefetch=1, grid=(s, h // hb, sum(widths)), in_specs=specs,
            out_specs=pl.BlockSpec((None, hb, cw), lambda si, b, n, pos_ref: (si, b, n))),
        out_shape=jax.ShapeDtypeStruct((s, h, sum(widths) * cw), F32),
        compiler_params=_params(("arbitrary",) * 3),
    )(pos, *operands)


def _adamw(w, g, m, v):
    m = ADAM_B1 * m + (1.0 - ADAM_B1) * g
    v = ADAM_B2 * v + (1.0 - ADAM_B2) * (g * g)
    m_hat = m / (1.0 - ADAM_B1 ** ADAM_STEP)
    v_hat = v / (1.0 - ADAM_B2 ** ADAM_STEP)
    delta = -ADAM_LR * (m_hat / (jnp.sqrt(v_hat) + ADAM_EPS) + ADAM_WD * w)
    return delta, m, v


def _adam_halves(ws, ms, vs, g_own, g_recv, pos, name):
    n = len(ws)
    _, r, c = ws[0].shape
    h = r // 2
    rb = min(h, ROW_BLOCK)
    nb = h // rb

    def body(pos_ref, *refs):
        w_refs, m_refs, v_refs = refs[:n], refs[n:2 * n], refs[2 * n:3 * n]
        go_ref, gr_ref = refs[3 * n:3 * n + 2]
        outs = refs[3 * n + 2:]
        mine = pl.program_id(0) == pos_ref[0]
        for a in range(n):
            g = jnp.where(mine, go_ref[a], gr_ref[a])
            delta, m, v = _adamw(w_refs[a][...], g, m_refs[a][...], v_refs[a][...])
            outs[4 * a][...], outs[4 * a + 1][...], outs[4 * a + 2][...], outs[4 * a + 3][...] = g, delta, m, v

    spec_w = pl.BlockSpec((None, rb, c), lambda hf, b, pos_ref: (0, hf * nb + b, 0))
    spec_g = pl.BlockSpec((n, rb, c), lambda hf, b, pos_ref: (0, b, 0))
    return pl.pallas_call(
        body, name=name,
        grid_spec=pltpu.PrefetchScalarGridSpec(
            num_scalar_prefetch=1, grid=(2, nb), in_specs=[spec_w] * (3 * n) + [spec_g] * 2,
            out_specs=[spec_w] * (4 * n)),
        out_shape=[jax.ShapeDtypeStruct((1, r, c), F32)] * (4 * n),
        compiler_params=_params(("arbitrary",) * 2),
    )(pos, *ws, *ms, *vs, g_own, g_recv)


def _adam_small(items):
    n = len(items)

    def body(*refs):
        ins, outs = refs[:4 * n], refs[4 * n:]
        for a in range(n):
            w_ref, g_ref, m_ref, v_ref = ins[4 * a:4 * a + 4]
            d, m, v = _adamw(w_ref[...], g_ref[...], m_ref[...], v_ref[...])
            outs[3 * a][...] = d
            outs[3 * a + 1][...] = m
            outs[3 * a + 2][...] = v

    vm = pl.BlockSpec(memory_space=pltpu.VMEM)
    flat = [t for it in items for t in it]
    outs = pl.pallas_call(
        body, name="adam_small", in_specs=[vm] * (4 * n), out_specs=[vm] * (3 * n),
        out_shape=[jax.ShapeDtypeStruct(it[0].shape, F32) for it in items for _ in range(3)],
    )(*flat)
    return [tuple(outs[3 * a:3 * a + 3]) for a in range(n)]


def _big_row_spec(seq, tmb, d, tile_of):
    return pl.BlockSpec((pl.Element(tmb), pl.Element(d)),
                        lambda *args: (pl.multiple_of(jnp.minimum(tile_of(*args) * tmb, seq - tmb), 8), 0))


def _big_row_tile(x_ref, front_ref, i):
    rows = x_ref[...]
    last = jnp.concatenate([rows[TM:], front_ref[...]], axis=0)
    return jnp.where(i == N_ROW_TILES_BIG - 1, last, rows)


def _h_prep(x, meta, norm_g, shards, pos):
    seq, d = x.shape
    tp = seq + TM
    dc = meta.shape[1]
    tmb = tp // N_ROW_TILES_BIG
    assert tmb >= TM and tp == tmb * N_ROW_TILES_BIG
    last = N_ROW_TILES_BIG - 1
    ns = len(shards)

    def body(pos_ref, x_ref, meta_ref, g_ref, *refs):
        shard_refs, (h_ref, front_ref), placed_refs = refs[:ns], refs[ns:ns + 2], refs[ns + 2:2 * ns + 2]
        metas, msend, mrecv = refs[2 * ns + 2:]
        for a in range(ns):
            placed_refs[a][...] = shard_refs[a][...].astype(placed_refs[a].dtype)
        x, y, c = _mesh_pos()
        me = 2 * x + y
        chips = [(1 - x, y), (x, 1 - y), (1 - x, 1 - y)]
        i = pl.program_id(0)

        def meta_copy(k, chip):
            return pltpu.make_async_remote_copy(
                src_ref=metas.at[chip], dst_ref=metas.at[chip], send_sem=msend.at[k], recv_sem=mrecv.at[k],
                device_id=(*chips[k], c), device_id_type=MESH)

        @pl.when(i == 0)
        def _():
            metas[me] = meta_ref[...]
            for k in range(3):
                meta_copy(k, me).start()
            front_ref[...] = jnp.zeros_like(front_ref)

        @pl.when(i == last)
        def _():
            for k, (px, py) in enumerate(chips):
                meta_copy(k, 2 * px + py).wait_recv()
            for q in range(N_CHIPS):
                front_ref[TM - N_META:TM, q * dc:(q + 1) * dc] = metas[q]

        s = _big_row_tile(x_ref, front_ref, i)
        r = lax.rsqrt(jnp.mean(s * s, axis=-1, keepdims=True) + EPS)
        h_ref[...] = (s * r * g_ref[...]).astype(BF16)

        @pl.when(i == last)
        def _():
            for k in range(3):
                meta_copy(k, me).wait_send()

    shard_in, shard_out, shard_shapes = [], [], []
    for arr, dtype in shards:
        s, r, c = arr.shape
        sliced = r % (N_ROW_TILES_BIG * 16) == 0
        rp = r // N_ROW_TILES_BIG if sliced else r
        step = (lambda i: i) if sliced else (lambda i: 0)
        shard_in.append(pl.BlockSpec((s, rp, c), lambda i, pos_ref, step=step: (0, step(i), 0)))
        shard_out.append(pl.BlockSpec((s, None, rp, c), lambda i, pos_ref, step=step: (0, pos_ref[1], step(i), 0)))
        shard_shapes.append(jax.ShapeDtypeStruct((s, N_CHIPS, r, c), dtype))
    outs = pl.pallas_call(
        body, name="f0_norm",
        grid_spec=pltpu.PrefetchScalarGridSpec(
            num_scalar_prefetch=1, grid=(N_ROW_TILES_BIG,),
            in_specs=[_big_row_spec(seq, tmb, d, lambda i, pos_ref: i),
                      pl.BlockSpec(meta.shape, lambda i, pos_ref: (0, 0)),
                      pl.BlockSpec((1, d), lambda i, pos_ref: (0, 0))] + shard_in,
            out_specs=[pl.BlockSpec((tmb, d), lambda i, pos_ref: (i, 0)),
                       pl.BlockSpec((TM, d), lambda i, pos_ref: (0, 0))] + shard_out,
            scratch_shapes=[pltpu.VMEM((N_CHIPS,) + meta.shape, F32),
                            pltpu.SemaphoreType.DMA((3,)), pltpu.SemaphoreType.DMA((3,))]),
        out_shape=[jax.ShapeDtypeStruct((tp, d), BF16), jax.ShapeDtypeStruct((TM, d), F32)] + shard_shapes,
        compiler_params=_params(("arbitrary",)),
    )(pos, x, meta, norm_g, *[arr for arr, _ in shards])
    return outs[0], outs[1], outs[2:]


N_UNITS = 3
N_STEPS_PROJ = N_CHIPS * N_UNITS


def _proj_plan(u):
    v = u - N_UNITS
    if v < 2 * N_UNITS:
        return v % 2, v // 2
    return 2, v - 2 * N_UNITS


def _proj_unit(t, me):
    v = t - N_UNITS
    near = v < 2 * N_UNITS
    rel = jnp.where(near, v % 2, 2)
    unit = jnp.where(near, v // 2, v - 2 * N_UNITS)
    flip = jnp.where(rel == 0, 2, jnp.where(rel == 1, 1, 3))
    own = t < N_UNITS
    return jnp.where(own, me, lax.bitwise_xor(me, flip)), jnp.where(own, t, unit)


def _proj_fwd(h, bufs, pos):
    tp, d = h.shape
    _, nsh, _, sw = bufs[0].shape
    cu = sw // N_UNITS
    assert cu % 128 == 0
    n = len(bufs)
    w_sems = 6 * N_UNITS
    last = N_STEPS_PROJ - 1
    late = N_STEPS_PROJ - N_UNITS

    def body(pos_ref, h_ref, *refs):
        proj_ref = refs[n]
        gbufs = refs[n + 1:2 * n + 1]
        wbuf, wsems, send_sems, recv_sems = refs[2 * n + 1:]
        x, y, c = _mesh_pos()
        me = 2 * x + y
        sibling = (x, y, 1 - c)
        chips = [(1 - x, y), (x, 1 - y), (1 - x, 1 - y)]
        chip_ids = [2 * px + py for px, py in chips]
        relayed_chip = jnp.where(c == 0, chip_ids[0], chip_ids[1])
        relay_to = (jnp.where(c == 0, x, 1 - x), jnp.where(c == 0, 1 - y, y), c)
        t = pl.program_id(0)

        def remote(idx, piece, to):
            return pltpu.make_async_remote_copy(
                src_ref=piece, dst_ref=piece, send_sem=send_sems.at[idx], recv_sem=recv_sems.at[idx],
                device_id=to, device_id_type=MESH)

        hr = d // 2

        def chunk_of(chip, half, k):
            return gbufs[0].at[0, chip, pl.ds(half * hr, hr), pl.ds(k * cu, cu)]

        def own_chunk(r, k):
            return remote(6 * k + r, chunk_of(me, c, k), (*chips[r], c))

        def landed_chunk(r, k):
            return remote(6 * k + r, chunk_of(chip_ids[r], c, k), (*chips[r], c))

        def relay_chunk(k):
            return remote(6 * k + 2, chunk_of(relayed_chip, c, k), relay_to)

        def sibling_chunk(r, k, half):
            return remote(6 * k + 3 + r, chunk_of(chip_ids[r], half, k), sibling)

        def fetch(u):
            chip, unit = _proj_unit(jnp.int32(u), me)
            return pltpu.make_async_copy(gbufs[0].at[0, chip, :, pl.ds(pl.multiple_of(unit * cu, 128), cu)],
                                         wbuf.at[u % 2], wsems.at[u % 2])

        def make_available(u):
            r, k = _proj_plan(u)
            landed_chunk(r, k).wait_recv()
            if r < 2:
                pl.when(c == r)(lambda: relay_chunk(k).start())
            sibling_chunk(r, k, c).start()
            sibling_chunk(r, k, 1 - c).wait_recv()

        def own_piece(a, r):
            return remote(w_sems + 6 * (a - 1) + r, _half(gbufs[a], me, c), (*chips[r], c))

        def relay(a):
            return remote(w_sems + 6 * (a - 1) + 2, _half(gbufs[a], relayed_chip, c), relay_to)

        def to_sibling(a, r, core):
            return remote(w_sems + 6 * (a - 1) + 3 + r, _half(gbufs[a], chip_ids[r], core), sibling)

        def landed(a, r):
            return remote(w_sems + 6 * (a - 1) + r, _half(gbufs[a], chip_ids[r], c), (*chips[r], c))

        for u in range(N_STEPS_PROJ):
            @pl.when(t == u)
            def _(u=u):
                if u == 0:
                    for k in range(N_UNITS):
                        for r in range(2):
                            own_chunk(r, k).start()
                    for a in range(1, n):
                        for r in range(2):
                            own_piece(a, r).start()
                    fetch(0).start()
                if u < last:
                    if u + 1 >= N_UNITS:
                        make_available(u + 1)
                    fetch(u + 1).start()
                if u == late:
                    for a in range(1, n):
                        landed(a, 0).wait_recv()
                        landed(a, 1).wait_recv()
                        relay(a).start()
                        for r in range(2):
                            to_sibling(a, r, c).start()
                        for r in range(2):
                            to_sibling(a, r, 1 - c).wait_recv()
                fetch(u).wait()

        proj_ref[...] = jnp.dot(h_ref[...], wbuf[t % 2], preferred_element_type=F32).astype(BF16)

        @pl.when(t == last)
        def _():
            for a in range(1, n):
                landed(a, 2).wait_recv()
                to_sibling(a, 2, c).start()
                to_sibling(a, 2, 1 - c).wait_recv()
            for k in range(N_UNITS):
                for r in range(2):
                    own_chunk(r, k).wait_send()
                relay_chunk(k).wait_send()
                for r in range(3):
                    sibling_chunk(r, k, c).wait_send()
            for a in range(1, n):
                for r in range(2):
                    own_piece(a, r).wait_send()
                relay(a).wait_send()
                for r in range(3):
                    to_sibling(a, r, c).wait_send()

    def out_index(t, pos_ref):
        chip, unit = _proj_unit(t, pos_ref[1])
        return 0, chip * N_UNITS + unit

    any_spec = pl.BlockSpec(memory_space=pl.ANY)
    outs = pl.pallas_call(
        body, name="f1_proj",
        grid_spec=pltpu.PrefetchScalarGridSpec(
            num_scalar_prefetch=1, grid=(N_STEPS_PROJ,),
            in_specs=[pl.BlockSpec((tp, d), lambda t, pos_ref: (0, 0))] + [any_spec] * n,
            out_specs=[pl.BlockSpec((tp, cu), out_index)] + [any_spec] * n,
            scratch_shapes=[pltpu.VMEM((2, d, cu), BF16), pltpu.SemaphoreType.DMA((2,)),
                            pltpu.SemaphoreType.DMA((w_sems + 6 * (n - 1),)),
                            pltpu.SemaphoreType.DMA((w_sems + 6 * (n - 1),))]),
        out_shape=[jax.ShapeDtypeStruct((tp, nsh * sw), BF16)]
        + [jax.ShapeDtypeStruct(b.shape, b.dtype) for b in bufs],
        input_output_aliases={2 + a: 1 + a for a in range(n)},
        compiler_params=_params(("arbitrary",)),
    )(pos, h, *bufs)
    return outs[0], outs[1:]


def _dh_bwd(dproj, wg_in, x, front, ds2, norm_g, part):
    seq, d = x.shape
    tp = seq + TM
    _, nsh, _, sw = wg_in.shape
    tmb = tp // N_ROW_TILES_BIG
    tail = tmb - TM
    last = N_ROW_TILES_BIG - 1

    def body(dp_ref, w_hbm, x_ref, front_ref, ds2_ref, g_ref, part_ref, gx_hbm, dfront_ref, dng_ref, land_ref,
             wbuf, gacc, dsbuf, wsem, osems, send_sems, recv_sems):
        exchange = _chip_exchange(part_ref, land_ref, send_sems, recv_sems)
        i = pl.program_id(0)

        def x_rows_out(step):
            return pltpu.make_async_copy(dsbuf.at[step % 2], gx_hbm.at[pl.ds(step * tmb, tmb), :], osems.at[step % 2])

        last_out = pltpu.make_async_copy(dsbuf.at[last % 2, pl.ds(0, tail), :],
                                         gx_hbm.at[pl.ds(last * tmb, tail), :], osems.at[last % 2])

        @pl.when(i == 0)
        def _():
            exchange.start()
            gacc[...] = jnp.zeros_like(gacc)
            whole = pltpu.make_async_copy(w_hbm.at[0], wbuf, wsem)
            whole.start()
            whole.wait()

        dh = None
        for j in range(nsh):
            part = lax.dot_general(dp_ref[:, j * sw:(j + 1) * sw], wbuf[j], (((1,), (1,)), ((), ())),
                                   preferred_element_type=F32)
            dh = part if dh is None else dh + part
        s = _big_row_tile(x_ref, front_ref, i)
        r = lax.rsqrt(jnp.mean(s * s, axis=-1, keepdims=True) + EPS)
        gacc[...] += (dh * s * r).reshape(tmb // 8, 8, d).sum(axis=0)
        t = dh * g_ref[...]

        @pl.when(i >= 2)
        def _():
            x_rows_out(i - 2).wait()

        dsbuf[i % 2] = ds2_ref[...] + r * t - s * (r * r * r) * jnp.mean(t * s, axis=-1, keepdims=True)

        @pl.when(i < last)
        def _():
            x_rows_out(i).start()

        @pl.when(i == last)
        def _():
            last_out.start()
            dfront_ref[...] = dsbuf[last % 2, tail:, :]
            dng_ref[...] = jnp.broadcast_to(jnp.sum(gacc[...], axis=0, keepdims=True), (8, d))
            exchange.finish()
            x_rows_out(last - 1).wait()
            last_out.wait()

    any_spec = pl.BlockSpec(memory_space=pl.ANY)
    return pl.pallas_call(
        body, name="b2_dh", grid=(N_ROW_TILES_BIG,),
        in_specs=[pl.BlockSpec((tmb, nsh * sw), lambda i: (i, 0)), any_spec,
                  _big_row_spec(seq, tmb, d, lambda i: i),
                  pl.BlockSpec((TM, d), lambda i: (0, 0)),
                  pl.BlockSpec((tmb, d), lambda i: (i, 0)),
                  pl.BlockSpec((1, d), lambda i: (0, 0)), any_spec],
        out_specs=[any_spec, pl.BlockSpec((TM, d), lambda i: (0, 0)),
                   pl.BlockSpec((8, d), lambda i: (0, 0)), any_spec],
        out_shape=[jax.ShapeDtypeStruct((seq, d), F32), jax.ShapeDtypeStruct((TM, d), F32),
                   jax.ShapeDtypeStruct((8, d), F32), jax.ShapeDtypeStruct(part.shape, part.dtype)],
        scratch_shapes=[pltpu.VMEM((nsh, d, sw), BF16), pltpu.VMEM((8, d), F32), pltpu.VMEM((2, tmb, d), F32),
                        pltpu.SemaphoreType.DMA, pltpu.SemaphoreType.DMA((2,)),
                        pltpu.SemaphoreType.DMA((3,)), pltpu.SemaphoreType.DMA((3,))],
        compiler_params=pltpu.CompilerParams(dimension_semantics=("arbitrary",),
                                             vmem_limit_bytes=VMEM_LIMIT_BIG),
    )(dproj, wg_in, x, front, ds2, norm_g, part)


def _col_block(width, cap):
    return max(b for b in range(128, cap + 1, 128) if width % b == 0)


def _dw_reduced(lhs_t, rhs, cw, nblk, operands, groups, out_dims, out_block, out_index, carried, name):
    na, d, tp = lhs_t.shape
    rg = d // groups
    hh = rg // 2
    nc = len(carried)

    def body(*refs):
        l_ref, r_ref = refs[:2]
        part_refs = refs[2:2 + nc]
        p32_ref, pbf_ref = refs[2 + nc:4 + nc]
        land_refs = refs[4 + nc:4 + 2 * nc]
        res, rbuf, send_sems, recv_sems = refs[4 + 2 * nc:8 + 2 * nc]
        xsems = refs[8 + 2 * nc:]
        exchanges = [_chip_exchange(part_refs[e], land_refs[e], xsems[2 * e], xsems[2 * e + 1]) for e in range(nc)]
        exchange = _Exchange([s for ex in exchanges for s in ex.sends], [r for ex in exchanges for r in ex.recvs])
        x, y, c = _mesh_pos()
        t = pl.program_id(0)
        u = jnp.maximum(t - 1, 0)

        def to_sibling(blk):
            return pltpu.make_async_remote_copy(
                src_ref=res.at[blk % 2, :, pl.ds((1 - c) * hh, hh), :], dst_ref=rbuf.at[blk % 2],
                send_sem=send_sems.at[blk], recv_sem=recv_sems.at[blk],
                device_id=(x, y, 1 - c), device_id_type=MESH)

        @pl.when(t == 0)
        def _():
            exchange.start()

        @pl.when(t < nblk)
        def _():
            res[t % 2] = jnp.dot(l_ref[...], r_ref[...], preferred_element_type=F32).reshape(groups, rg, cw)

        @pl.when(t >= 1)
        def _():
            to_sibling(u).wait_recv()
            p = res[u % 2, :, pl.ds(c * hh, hh), :] + rbuf[u % 2]
            p32_ref[...] = p.reshape(p32_ref.shape)
            pbf_ref[...] = p.reshape(pbf_ref.shape).astype(BF16)

        @pl.when(t < nblk)
        def _():
            to_sibling(t).start()

        @pl.when(t >= 1)
        def _():
            to_sibling(u).wait_send()

        @pl.when(t == nblk)
        def _():
            exchange.finish()

    any_spec = pl.BlockSpec(memory_space=pl.ANY)
    last = nblk - 1
    out_spec = pl.BlockSpec(out_block, lambda t: out_index(jnp.maximum(t - 1, 0)))
    outs = pl.pallas_call(
        body, name=name, grid=(nblk + 1,),
        in_specs=[pl.BlockSpec((None, d, tp), lambda t: (operands(jnp.minimum(t, last))[0], 0, 0)),
                  pl.BlockSpec((None, tp, cw), lambda t: (operands(jnp.minimum(t, last))[0], 0,
                                                          operands(jnp.minimum(t, last))[1]))]
        + [any_spec] * nc,
        out_specs=[out_spec, out_spec] + [any_spec] * nc,
        out_shape=[jax.ShapeDtypeStruct(out_dims, F32), jax.ShapeDtypeStruct(out_dims, BF16)]
        + [jax.ShapeDtypeStruct(e.shape, e.dtype) for e in carried],
        scratch_shapes=[pltpu.VMEM((2, groups, rg, cw), F32), pltpu.VMEM((2, groups, hh, cw), F32),
                        pltpu.SemaphoreType.DMA((nblk,)), pltpu.SemaphoreType.DMA((nblk,))]
        + [pltpu.SemaphoreType.DMA((3,)), pltpu.SemaphoreType.DMA((3,))] * nc,
        compiler_params=_params(("arbitrary",)),
    )(lhs_t, rhs, *carried)
    return outs[0], outs[1], outs[2:]


def _conv_a_taps(first_lag, last_lag):
    out = []
    for r in range(8):
        taps = [(q, 8 * q + r) for q in range(5) if first_lag <= 8 * q + r <= last_lag]
        if taps:
            out.append((r, taps))
    return out


def _tile_block(i, nt):
    return jnp.where(i == 0, nt - 1, i - 1)


def _mix_fwd(x, front, proj, target, w3, wa, wb, conv_a_b, ln_g, ln_b, b_a_out, final_g, h):
    seq, d = x.shape
    tp = seq + TM
    nt = tp // TM
    RB = RB_FWD
    nrb = TM // RB
    shl = TM + SHIFT_ROWS

    def body(x_ref, front_ref, proj_ref, tgt_ref, w3_ref, wa_ref, wb_ref, cab_ref, lng_ref, lnb_ref, bao_ref, fg_ref,
             h_ref, ca_ref, cb_ref, ya_ref, yb_ref, abmt_ref, ds2_ref, ht_ref, loss_ref, dfg_ref,
             abm_ref, ext_a, ext_b, sh, s2_s, lacc, gacc):
        i = pl.program_id(0)

        def split(k, rows):
            return proj_ref[rows, k * d:(k + 1) * d].astype(F32)

        def s_tile():
            return jnp.where(i == 0, front_ref[...], x_ref[...])

        ht_ref[...] = h_ref[...].T

        @pl.when(i == 0)
        def _():
            ext_a[0:HALO_A, :] = jnp.zeros((HALO_A, d), F32)
            ext_b[0:HALO_B, :] = jnp.zeros((HALO_B, d), F32)
            lacc[...] = jnp.zeros_like(lacc)
            gacc[...] = jnp.zeros_like(gacc)

        def conv_in(rb, carry):
            rows = _rows(rb, RB)
            ua0 = split(0, rows) * _sigmoid(split(1, rows))
            ext_a[pl.ds(pl.multiple_of(HALO_A + rb * RB, 8), RB), :] = ua0
            ext_b[pl.ds(pl.multiple_of(HALO_B + rb * RB, 8), RB), :] = split(4, rows) * split(5, rows)
            ca_ref[rows, :] = jnp.broadcast_to(cab_ref[...], (RB, d))
            return carry
        lax.fori_loop(0, nrb, conv_in, 0)

        @pl.when(i == 0)
        def _():
            abmt_ref[...] = jnp.zeros_like(abmt_ref)
            ds2_ref[...] = jnp.zeros_like(ds2_ref)

        @pl.when(i > 0)
        def _():
            tile_after_conv_inputs(split, s_tile, tgt_ref, w3_ref, wa_ref, wb_ref, lng_ref, lnb_ref, bao_ref, fg_ref,
                                   ca_ref, cb_ref, ya_ref, yb_ref, abmt_ref, ds2_ref, abm_ref, ext_a, ext_b, sh,
                                   s2_s, lacc, gacc)

        ext_a[0:HALO_A, :] = ext_a[TM:TM + HALO_A, :]
        ext_b[0:HALO_B, :] = ext_b[TM:TM + HALO_B, :]

        @pl.when(i == nt - 1)
        def _():
            loss_ref[...] = jnp.broadcast_to(0.5 * jnp.sum(lacc[...]) * (1.0 / d), (8, 128))
            dfg_ref[...] = jnp.broadcast_to(jnp.sum(gacc[...], axis=0, keepdims=True), (8, d))

    def tile_after_conv_inputs(split, s_tile, tgt_ref, w3_ref, wa_ref, wb_ref, lng_ref, lnb_ref, bao_ref, fg_ref,
                               ca_ref, cb_ref, ya_ref, yb_ref, abmt_ref, ds2_ref, abm_ref, ext_a, ext_b, sh, s2_s,
                               lacc, gacc):
        for r, taps in _conv_a_taps(HALO_A - CONV_A + 1, HALO_A):
            if r == 0:
                src = ext_a
            else:
                sh[...] = ext_a[r:r + shl, :]
                src = sh

            def conv_acc(rb, carry, src=src, taps=taps):
                rows = _rows(rb, RB)
                acc = ca_ref[rows, :]
                for q, lag in taps:
                    k = lag - (HALO_A - CONV_A + 1)
                    acc = acc + src[pl.ds(pl.multiple_of(rb * RB + 8 * q, 8), RB), :] * wa_ref[k:k + 1, :]
                ca_ref[rows, :] = acc
                return carry
            lax.fori_loop(0, nrb, conv_acc, 0)

        cb_ref[...] = ext_b[HALO_B:HALO_B + TM, :] * wb_ref[2:3, :]
        for k in range(CONV_B - 1):
            off = HALO_B - CONV_B + 1 + k
            sh[0:TM, :] = ext_b[off:off + TM, :]
            cb_ref[...] += sh[0:TM, :] * wb_ref[k:k + 1, :]

        def branches(rb, carry):
            rows = _rows(rb, RB)
            ca = ca_ref[rows, :]
            mu = jnp.mean(ca, axis=-1, keepdims=True)
            xc = ca - mu
            rstd = lax.rsqrt(jnp.mean(xc * xc, axis=-1, keepdims=True) + EPS)
            ln = xc * rstd * lng_ref[...] + lnb_ref[...]
            ua = ln * _sigmoid(ln)
            a_z = split(2, rows)
            abm_ref[0, rows, :] = (ua * (a_z * _sigmoid(a_z))).astype(BF16)
            return carry
        lax.fori_loop(0, nrb, branches, 0)

        def branch_b(rb, carry):
            rows = _rows(rb, RB)
            b_z = split(6, rows)
            ub = split(3, rows) * cb_ref[rows, :]
            abm_ref[1, rows, :] = (ub * (b_z * _sigmoid(b_z))).astype(BF16)
            return carry
        lax.fori_loop(0, nrb, branch_b, 0)

        ya_ref[...] = jnp.dot(abm_ref[0], w3_ref[0], preferred_element_type=F32) + bao_ref[...]
        yb_ref[...] = jnp.dot(abm_ref[1], w3_ref[1], preferred_element_type=F32)

        def merge(rb, carry):
            rows = _rows(rb, RB)
            m = _sigmoid(split(7, rows)) * ya_ref[rows, :] + _sigmoid(split(8, rows)) * yb_ref[rows, :]
            abm_ref[2, rows, :] = m.astype(BF16)
            return carry
        lax.fori_loop(0, nrb, merge, 0)

        s2_s[...] = s_tile() + jnp.dot(abm_ref[2], w3_ref[2], preferred_element_type=F32)
        for k in range(3):
            abmt_ref[k] = abm_ref[k].T

        def head(rb, carry):
            rows = _rows(rb, RB)
            s2 = s2_s[rows, :]
            r2 = lax.rsqrt(jnp.mean(s2 * s2, axis=-1, keepdims=True) + EPS)
            diff = s2 * r2 * fg_ref[...] - tgt_ref[rows, :]
            lacc[...] += diff * diff
            dy = diff * (1.0 / d)
            gacc[...] += (dy * s2 * r2).reshape(RB // 8, 8, d).sum(axis=0)
            t = dy * fg_ref[...]
            ds2_ref[rows, :] = r2 * t - s2 * (r2 * r2 * r2) * jnp.mean(t * s2, axis=-1, keepdims=True)
            return carry
        lax.fori_loop(0, nrb, head, 0)

    row_f32 = pl.BlockSpec((TM, d), lambda i: (_tile_block(i, nt), 0))
    x_rows = pl.BlockSpec((TM, d), lambda i: (jnp.maximum(i - 1, 0), 0))
    const = lambda shape: pl.BlockSpec(shape, lambda i: (0,) * len(shape))
    return pl.pallas_call(
        body, name="f2_mix", grid=(nt,),
        in_specs=[x_rows, const((TM, d)),
                  pl.BlockSpec((TM, N_SPLIT * d), lambda i: (_tile_block(i, nt), 0)),
                  x_rows,
                  const((3, d, d)), const(wa.shape), const(wb.shape)] + [const((1, d))] * 5
        + [pl.BlockSpec((TM, d), lambda i: (_tile_block(i, nt), 0))],
        out_specs=[row_f32, row_f32, row_f32, row_f32,
                   pl.BlockSpec((3, d, TM), lambda i: (0, 0, _tile_block(i, nt))),
                   row_f32, pl.BlockSpec((d, TM), lambda i: (0, _tile_block(i, nt))),
                   const((8, 128)), const((8, d))],
        out_shape=[jax.ShapeDtypeStruct((tp, d), F32)] * 4
        + [jax.ShapeDtypeStruct((3, d, tp), BF16), jax.ShapeDtypeStruct((tp, d), F32),
           jax.ShapeDtypeStruct((d, tp), BF16),
           jax.ShapeDtypeStruct((8, 128), F32), jax.ShapeDtypeStruct((8, d), F32)],
        scratch_shapes=[pltpu.VMEM((3, TM, d), BF16),
                        pltpu.VMEM((HALO_A + TM, d), F32), pltpu.VMEM((HALO_B + TM, d), F32),
                        pltpu.VMEM((shl, d), F32), pltpu.VMEM((TM, d), F32),
                        pltpu.VMEM((RB, d), F32), pltpu.VMEM((8, d), F32)],
        compiler_params=_params(("arbitrary",)),
    )(x, front, proj, target, w3, wa, wb, conv_a_b, ln_g, ln_b, b_a_out, final_g, h)


def _mix_bwd(ds2, proj, ca, cb, ya, yb, w3, wa, wb, ln_g, ln_b):
    tp, d = ds2.shape
    nt = tp // TM
    RB = RB_BWD
    nrb = TM // RB
    strips = [slice(c0, c0 + STRIP_COLS) for c0 in range(0, d, STRIP_COLS)]
    shl = TM + SHIFT_ROWS
    nt_dims = (((1,), (1,)), ((), ()))

    def body(ds2_ref, proj_ref, ca_ref, cb_ref, ya_ref, yb_ref, w3_ref, wa_ref, wb_ref, lng_ref, lnb_ref,
             dproj_ref, d3_ref, sm_ref, ext_d, ext_e, sh, dm_s, dpa_s, dpb_s, dua0_s, acc):
        step = pl.program_id(0)

        def split(k, rows, cols=slice(0, d)):
            return proj_ref[rows, k * d + cols.start:k * d + cols.stop].astype(F32)

        def put(k, rows, val, cols=slice(0, d)):
            dproj_ref[rows, k * d + cols.start:k * d + cols.stop] = val.astype(BF16)

        def accum(row, val, cols=slice(0, d)):
            acc[row, :, cols] += val.reshape(RB // 8, 8, val.shape[-1]).sum(axis=0)

        @pl.when(step == 0)
        def _():
            ext_d[TM:TM + HALO_A, :] = jnp.zeros((HALO_A, d), F32)
            ext_e[TM:TM + HALO_B, :] = jnp.zeros((HALO_B, d), F32)
            acc[...] = jnp.zeros_like(acc)

        front = step == nt - 1

        @pl.when(front)
        def _():
            d3_ref[...] = jnp.zeros_like(d3_ref)

            def conv_only(rb, carry):
                rows = _rows(rb, RB)
                zeros = jnp.zeros((RB, d), F32)
                for k in (2, 3, 6, 7, 8):
                    put(k, rows, zeros)
                ext_d[rows, :] = zeros
                ext_e[rows, :] = zeros
                dua0_s[rows, :] = zeros
                dm_s[rows, :] = split(0, rows) * _sigmoid(split(1, rows))
                return carry
            lax.fori_loop(0, nrb, conv_only, 0)

        @pl.when(jnp.logical_not(front))
        def _():
            tile_to_conv_outputs(split, put, accum, ds2_ref, ca_ref, cb_ref, ya_ref, yb_ref, w3_ref, lng_ref, lnb_ref,
                                 d3_ref, ext_d, ext_e, dm_s, dpa_s, dpb_s, dua0_s)

        tile_conv_transposes(split, put, accum, wa_ref, wb_ref, ext_d, ext_e, sh, dm_s, dpb_s, dua0_s)

        @pl.when(front)
        def _():
            for row in range(SM_ROWS):
                sm_ref[row:row + 1, :] = jnp.sum(acc[row], axis=0, keepdims=True)

    def tile_to_conv_outputs(split, put, accum, ds2_ref, ca_ref, cb_ref, ya_ref, yb_ref, w3_ref, lng_ref, lnb_ref,
                             d3_ref, ext_d, ext_e, dm_s, dpa_s, dpb_s, dua0_s):
        d3_ref[2] = ds2_ref[...].astype(BF16)
        dm_s[...] = lax.dot_general(d3_ref[2], w3_ref[2], nt_dims, preferred_element_type=F32)

        def gates(rb, carry):
            rows = _rows(rb, RB)
            for cols in strips:
                dm = dm_s[rows, cols]
                sa = _sigmoid(split(7, rows, cols))
                sb = _sigmoid(split(8, rows, cols))
                put(7, rows, dm * ya_ref[rows, cols] * sa * (1.0 - sa), cols)
                put(8, rows, dm * yb_ref[rows, cols] * sb * (1.0 - sb), cols)
                dya = dm * sa
                accum(ROW_DBAO, dya, cols)
                d3_ref[0, rows, cols] = dya.astype(BF16)
                d3_ref[1, rows, cols] = (dm * sb).astype(BF16)
            return carry
        lax.fori_loop(0, nrb, gates, 0, unroll=True)

        dpa_s[...] = lax.dot_general(d3_ref[0], w3_ref[0], nt_dims, preferred_element_type=F32)
        dpb_s[...] = lax.dot_general(d3_ref[1], w3_ref[1], nt_dims, preferred_element_type=F32)

        def branch_a(rb, carry):
            rows = _rows(rb, RB)

            def row_mean(strip_fn):
                total = strip_fn(strips[0])
                for cols in strips[1:]:
                    total = total + strip_fn(cols)
                return jnp.sum(total, axis=-1, keepdims=True) * (1.0 / d)

            mu = row_mean(lambda cols: ca_ref[rows, cols])
            rstd = lax.rsqrt(row_mean(lambda cols: jnp.square(ca_ref[rows, cols] - mu)) + EPS)
            sum_dxh = jnp.zeros((RB, STRIP_COLS), F32)
            sum_dxh_xhat = jnp.zeros((RB, STRIP_COLS), F32)
            for cols in strips:
                xhat = (ca_ref[rows, cols] - mu) * rstd
                ln = xhat * lng_ref[:, cols] + lnb_ref[:, cols]
                sl = _sigmoid(ln)
                ua = ln * sl
                a_z = split(2, rows, cols)
                sz = _sigmoid(a_z)
                dpa = dpa_s[rows, cols]
                put(2, rows, dpa * ua * (sz * (1.0 + a_z * (1.0 - sz))), cols)
                dln = dpa * (a_z * sz) * (sl * (1.0 + ln * (1.0 - sl)))
                accum(ROW_DLNG, dln * xhat, cols)
                accum(ROW_DLNB, dln, cols)
                dxh = dln * lng_ref[:, cols]
                sum_dxh = sum_dxh + dxh
                sum_dxh_xhat = sum_dxh_xhat + dxh * xhat
                ext_d[rows, cols] = dxh
                dpa_s[rows, cols] = xhat
            mean_dxh = jnp.sum(sum_dxh, axis=-1, keepdims=True) * (1.0 / d)
            mean_dxh_xhat = jnp.sum(sum_dxh_xhat, axis=-1, keepdims=True) * (1.0 / d)
            for cols in strips:
                dca = rstd * (ext_d[rows, cols] - mean_dxh - dpa_s[rows, cols] * mean_dxh_xhat)
                accum(ROW_DCAB, dca, cols)
                ext_d[rows, cols] = dca
            return carry
        lax.fori_loop(0, nrb, branch_a, 0, unroll=True)

        def branch_b(rb, carry):
            rows = _rows(rb, RB)
            dua0_s[rows, :] = jnp.zeros((RB, d), F32)
            for cols in strips:
                dm_s[rows, cols] = split(0, rows, cols) * _sigmoid(split(1, rows, cols))
                b_z = split(6, rows, cols)
                szb = _sigmoid(b_z)
                dpb = dpb_s[rows, cols]
                b_b = split(3, rows, cols)
                cb_v = cb_ref[rows, cols]
                put(6, rows, dpb * (b_b * cb_v) * (szb * (1.0 + b_z * (1.0 - szb))), cols)
                dub = dpb * (b_z * szb)
                put(3, rows, dub * cb_v, cols)
                ext_e[rows, cols] = dub * b_b
            return carry
        lax.fori_loop(0, nrb, branch_b, 0, unroll=True)

    def tile_conv_transposes(split, put, accum, wa_ref, wb_ref, ext_d, ext_e, sh, dm_s, dpb_s, dua0_s):
        for r, taps in _conv_a_taps(0, CONV_A - 1):
            if r == 0:
                src = ext_d
            else:
                sh[...] = ext_d[r:r + shl, :]
                src = sh

            def conv_t(rb, carry, src=src, taps=taps):
                rows = _rows(rb, RB)
                for cols in strips:
                    ua0 = dm_s[rows, cols]
                    dua0 = dua0_s[rows, cols]
                    for q, lag in taps:
                        k = CONV_A - 1 - lag
                        slab = src[pl.ds(pl.multiple_of(rb * RB + 8 * q, 8), RB), cols]
                        dua0 = dua0 + slab * wa_ref[k:k + 1, cols]
                        accum(ROW_DWA + k, slab * ua0, cols)
                    dua0_s[rows, cols] = dua0
                return carry
            lax.fori_loop(0, nrb, conv_t, 0, unroll=True)
        ext_d[TM:TM + HALO_A, :] = ext_d[0:HALO_A, :]

        dpb_s[...] = ext_e[0:TM, :] * wb_ref[CONV_B - 1:CONV_B, :]
        for lag in range(CONV_B):
            k = CONV_B - 1 - lag
            if lag > 0:
                sh[0:TM, :] = ext_e[lag:lag + TM, :]
                dpb_s[...] += sh[0:TM, :] * wb_ref[k:k + 1, :]
            src = ext_e if lag == 0 else sh

            def conv_b_w(rb, carry, src=src, k=k):
                rows = _rows(rb, RB)
                accum(ROW_DWB + k, src[rows, :] * (split(4, rows) * split(5, rows)))
                return carry
            lax.fori_loop(0, nrb, conv_b_w, 0, unroll=True)
        ext_e[TM:TM + HALO_B, :] = ext_e[0:HALO_B, :]

        def inputs(rb, carry):
            rows = _rows(rb, RB)
            for cols in strips:
                dua0 = dua0_s[rows, cols]
                a_val = split(0, rows, cols)
                sg = _sigmoid(split(1, rows, cols))
                put(0, rows, dua0 * sg, cols)
                put(1, rows, dua0 * a_val * sg * (1.0 - sg), cols)
                dcbin = dpb_s[rows, cols]
                put(4, rows, dcbin * split(5, rows, cols), cols)
                put(5, rows, dcbin * split(4, rows, cols), cols)
            return carry
        lax.fori_loop(0, nrb, inputs, 0, unroll=True)

    rev = lambda i: (_tile_block(nt - 1 - i, nt), 0)
    row_f32 = pl.BlockSpec((TM, d), rev)
    const = lambda shape: pl.BlockSpec(shape, lambda i: (0,) * len(shape))
    return pl.pallas_call(
        body, name="b1_mix", grid=(nt,),
        in_specs=[row_f32, pl.BlockSpec((TM, N_SPLIT * d), rev), row_f32, row_f32, row_f32, row_f32,
                  const((3, d, d)), const(wa.shape), const(wb.shape), const((1, d)), const((1, d))],
        out_specs=[pl.BlockSpec((TM, N_SPLIT * d), rev),
                   pl.BlockSpec((3, TM, d), lambda i: (0, _tile_block(nt - 1 - i, nt), 0)),
                   const((SM_ROWS, d))],
        out_shape=[jax.ShapeDtypeStruct((tp, N_SPLIT * d), BF16), jax.ShapeDtypeStruct((3, tp, d), BF16),
                   jax.ShapeDtypeStruct((SM_ROWS, d), F32)],
        scratch_shapes=[pltpu.VMEM((TM + HALO_A, d), F32), pltpu.VMEM((TM + HALO_B, d), F32),
                        pltpu.VMEM((shl, d), F32), pltpu.VMEM((TM, d), F32), pltpu.VMEM((TM, d), F32),
                        pltpu.VMEM((TM, d), F32), pltpu.VMEM((TM, d), F32),
                        pltpu.VMEM((SM_ROWS, 8, d), F32)],
        compiler_params=_params(("arbitrary",)),
    )(ds2, proj, ca, cb, ya, yb, w3, wa, wb, ln_g, ln_b)


def kernel(x, meta_tokens, norm_g, w_in, conv_a_w, conv_a_b, ln_a_g, ln_a_b, w_a_out, b_a_out, conv_b_w, w_b_out, w_out, final_g, loss_target, m_meta_tokens, m_norm_g, m_w_in, m_conv_a_w, m_conv_a_b, m_ln_a_g, m_ln_a_b, m_w_a_out, m_b_a_out, m_conv_b_w, m_w_b_out, m_w_out, m_final_g, v_meta_tokens, v_norm_g, v_w_in, v_conv_a_w, v_conv_a_b, v_ln_a_g, v_ln_a_b, v_w_a_out, v_b_a_out, v_conv_b_w, v_w_b_out, v_w_out, v_final_g):
    seq, d = x.shape[1], x.shape[2]
    dc = meta_tokens.shape[1]
    sw = w_in.shape[2]
    rsh = w_a_out.shape[1]
    xi, yi, ci = _mesh_pos()
    me = 2 * xi + yi
    pos = jnp.stack([ci, me]).astype(jnp.int32)

    conv_rows = HALO_A + HALO_B + 8
    convs = jnp.concatenate([
        jnp.pad(conv_a_w[0], ((0, HALO_A - CONV_A), (0, 0))),
        jnp.pad(conv_b_w[0], ((0, HALO_B - CONV_B), (0, 0))), jnp.zeros((8, dc), F32)], axis=0)[None]
    w3_own = jnp.stack([w_a_out[0], w_b_out[0], w_out[0]])
    fg2 = final_g.reshape(1, d)
    xs = x[0]

    h, front, placed = _h_prep(xs, meta_tokens, norm_g, [(w_in, BF16), (w3_own, BF16), (convs, F32)], pos)
    proj, (wg_in, wg3, convg) = _proj_fwd(h, placed, pos)
    w3 = wg3.reshape(3, N_CHIPS * rsh, d)
    convg = jnp.transpose(convg[0], (1, 0, 2)).reshape(conv_rows, N_CHIPS * dc)
    wa_full = convg[0:HALO_A]
    wb_full = convg[HALO_A:HALO_A + HALO_B]
    ca, cb, ya, yb, abm_t, ds2, h_t, loss8, dfg8 = _mix_fwd(
        xs, front, proj, loss_target[0], w3, wa_full, wb_full, conv_a_b, ln_a_g, ln_a_b, b_a_out, fg2, h)
    dproj, d3, sm = _mix_bwd(ds2, proj, ca, cb, ya, yb, w3, wa_full, wb_full, ln_a_g, ln_a_b)
    cw_sq = _col_block(d, 512)
    per_sq = d // cw_sq
    p32_sq, pbf_sq, _ = _dw_reduced(
        abm_t, d3, cw_sq, 3 * per_sq, lambda t: (t // per_sq, t % per_sq), N_CHIPS,
        (3, N_CHIPS, rsh // 2, d), (None, N_CHIPS, rsh // 2, cw_sq),
        lambda u: (u // per_sq, 0, 0, u % per_sq), [], "dw_square")
    cw_in = _col_block(sw, 768)
    ncol = sw // cw_in
    p32_in, pbf_in, (l_sq,) = _dw_reduced(
        h_t[None], dproj[None], cw_in, N_CHIPS * ncol, lambda t: (0, t), 1,
        (1, N_CHIPS, d // 2, sw), (None, None, d // 2, cw_in), lambda u: (0, u // ncol, 0, u % ncol),
        [pbf_sq], "dw_in")
    grad_x, dfront, dng8, l_in = _dh_bwd(dproj, wg_in, xs, front, ds2, norm_g, pbf_in)
    half_in = _sum_chips([(p32_in, l_in)], sw, pos, "rs_sum_in")
    half_sq = _sum_chips([(p32_sq, l_sq)], d, pos, "rs_sum_sq")
    tail_row = lax.broadcasted_iota(jnp.int32, (8, d), 0)
    tail = jnp.where(tail_row == 0, dng8, jnp.where(tail_row == 1, dfg8,
                     jnp.where(tail_row == 2, loss8[0, 0], 0.0)))
    block = jnp.concatenate([sm, dfront[TM - N_META:TM], tail], axis=0)
    (other_in, other_sq), red = _sibling_swap([half_in, half_sq], block)
    col = lax.dynamic_slice(red, (0, me * dc), (AR_ROWS, dc))
    g_small = {
        "meta_tokens": col[ROW_DMETA:ROW_DMETA + N_META],
        "norm_g": red[ROW_DNG:ROW_DNG + 1],
        "conv_a_w": col[ROW_DWA:ROW_DWA + CONV_A][None],
        "conv_a_b": red[ROW_DCAB:ROW_DCAB + 1],
        "ln_a_g": red[ROW_DLNG:ROW_DLNG + 1],
        "ln_a_b": red[ROW_DLNB:ROW_DLNB + 1],
        "b_a_out": red[ROW_DBAO:ROW_DBAO + 1],
        "conv_b_w": col[ROW_DWB:ROW_DWB + CONV_B][None],
        "final_g": red[ROW_DFG],
    }

    upd_in = _adam_halves([w_in], [m_w_in], [v_w_in], half_in, other_in, pos, "adam_in")
    upd_sq = _adam_halves([w_a_out, w_b_out, w_out], [m_w_a_out, m_w_b_out, m_w_out],
                          [v_w_a_out, v_w_b_out, v_w_out], half_sq, other_sq, pos, "adam_sq")
    small_w = {"meta_tokens": (meta_tokens, m_meta_tokens, v_meta_tokens), "norm_g": (norm_g, m_norm_g, v_norm_g),
               "conv_a_w": (conv_a_w, m_conv_a_w, v_conv_a_w), "conv_a_b": (conv_a_b, m_conv_a_b, v_conv_a_b),
               "ln_a_g": (ln_a_g, m_ln_a_g, v_ln_a_g), "ln_a_b": (ln_a_b, m_ln_a_b, v_ln_a_b),
               "b_a_out": (b_a_out, m_b_a_out, v_b_a_out), "conv_b_w": (conv_b_w, m_conv_b_w, v_conv_b_w),
               "final_g": (final_g, m_final_g, v_final_g)}
    names_small = list(small_w)
    as2d = lambda t: t.reshape(-1, t.shape[-1])
    upd_small = _adam_small([(as2d(small_w[k][0]), as2d(g_small[k]), as2d(small_w[k][1]), as2d(small_w[k][2]))
                             for k in names_small])

    grads, deltas, new_m, new_v = dict(g_small), {}, {}, {}
    for k, upd in zip(names_small, upd_small):
        deltas[k], new_m[k], new_v[k] = [t.reshape(small_w[k][0].shape) for t in upd]
    grads["w_in"], deltas["w_in"], new_m["w_in"], new_v["w_in"] = upd_in
    for idx, k in enumerate(["w_a_out", "w_b_out", "w_out"]):
        grads[k], deltas[k], new_m[k], new_v[k] = upd_sq[4 * idx:4 * idx + 4]

    loss = red[ROW_LOSS, 0]
    order = ["meta_tokens", "norm_g", "w_in", "conv_a_w", "conv_a_b", "ln_a_g", "ln_a_b", "w_a_out", "b_a_out",
             "conv_b_w", "w_b_out", "w_out", "final_g"]
    return (loss, grad_x[None], *[grads[k] for k in order], *[deltas[k] for k in order],
            *[new_m[k] for k in order], *[new_v[k] for k in order])
```
